```python
import numpy as np
import jax, jax.numpy as jnp
from jax import lax

D_MODEL = 1024
BATCH = 4
SEQ = 4096
DEPTH = 2

MEM_LEN = 256
MIX_WIDTH = D_MODEL
NSA_WIDTH = MIX_WIDTH // 2
POOL_WIDTH = MIX_WIDTH - NSA_WIDTH
HEAD_DIM = 64
N_HEADS = NSA_WIDTH // HEAD_DIM
N_KV_HEADS = 2
GROUP = N_HEADS // N_KV_HEADS
KV_WIDTH = N_KV_HEADS * HEAD_DIM
ROPE_DIM = HEAD_DIM // 4
ROPE_THETA = 500000.0
CMP_LEN = 32
CMP_STRIDE = 16
CMP_HIDDEN = 256
SEL_BLOCK = 64
SEL_TOP_N = 16
N_LOCAL_BLOCKS = 2
WINDOW = 512
Q_BLOCK = 128
N_BRANCH = 3
POOL_WINDOWS = (2, 4, 8, 16)
N_POOL_GROUPS = len(POOL_WINDOWS)
POOL_GROUP = POOL_WIDTH // N_POOL_GROUPS
IN_WIDTH = NSA_WIDTH + 6 * KV_WIDTH + N_HEADS * N_BRANCH + POOL_WIDTH
X_HEADS = 4
X_HEAD_DIM = D_MODEL // X_HEADS
D_FF = 2816
N_EXPERTS = 8
TOP_K = 2
D_FF_EXPERT = 3584
N_DENSE = (DEPTH + 1) // 2
N_MOE = DEPTH // 2
EPS = 1e-6

kernel_name = "hybrid_nsa_pool_moe_block"


def rms_norm(x, g):
    xf = x.astype(jnp.float32)
    y = xf * lax.rsqrt(jnp.mean(xf * xf, axis=-1, keepdims=True) + EPS)
    return (y * g.astype(jnp.float32)).astype(x.dtype)


def partial_rope(x, pos):
    half = ROPE_DIM // 2
    inv = ROPE_THETA ** (-jnp.arange(half, dtype=jnp.float32) / half)
    ang = pos.astype(jnp.float32)[:, None] * inv[None, :]
    cos = jnp.cos(ang)[None, :, None, :]
    sin = jnp.sin(ang)[None, :, None, :]
    xr = x[..., :ROPE_DIM].astype(jnp.float32)
    x1, x2 = xr[..., :half], xr[..., half:]
    rot = jnp.concatenate([x1 * cos - x2 * sin, x2 * cos + x1 * sin], axis=-1)
    return jnp.concatenate([rot.astype(x.dtype), x[..., ROPE_DIM:]], axis=-1)


def masked_softmax(logits, mask):
    lf = jnp.where(mask, logits.astype(jnp.float32), -jnp.inf)
    m = jnp.max(lf, axis=-1, keepdims=True)
    m = jnp.where(jnp.isfinite(m), m, 0.0)
    e = jnp.where(mask, jnp.exp(lf - m), 0.0)
    s = jnp.sum(e, axis=-1, keepdims=True)
    return e / jnp.maximum(s, jnp.finfo(jnp.float32).tiny)


def compress_blocks(kv, pe, w1, b1, w2):
    B, S, G, dh = kv.shape
    n_cmp = (S - CMP_LEN) // CMP_STRIDE + 1
    idx = np.arange(n_cmp)[:, None] * CMP_STRIDE + np.arange(CMP_LEN)[None, :]
    blk = kv[:, idx] + pe[None, None, :, None, :]
    blk = jnp.transpose(blk, (0, 1, 3, 2, 4)).reshape(B, n_cmp, G, CMP_LEN * dh)
    hid = jax.nn.gelu(jnp.einsum('bcgf,fh->bcgh', blk, w1) + b1)
    return jnp.einsum('bcgh,hd->bcgd', hid, w2)


def nsa_attention(q, k_cmp_raw, v_cmp_raw, k_slc, v_slc, k_win, v_win, gates,
                  pe_k, w1_k, b1_k, w2_k, pe_v, w1_v, b1_v, w2_v):
    B, S, H, dh = q.shape
    scale = dh ** -0.5
    pos = jnp.arange(S)
    q_rot = partial_rope(q, pos)
    k_slc = partial_rope(k_slc, pos)
    k_win = partial_rope(k_win, pos)
    k_cmp = compress_blocks(k_cmp_raw, pe_k, w1_k, b1_k, w2_k)
    v_cmp = compress_blocks(v_cmp_raw, pe_v, w1_v, b1_v, w2_v)
    n_cmp = k_cmp.shape[1]
    cmp_start = np.arange(n_cmp) * CMP_STRIDE
    cmp_end = jnp.asarray(cmp_start + CMP_LEN - 1)
    n_blk = S // SEL_BLOCK
    n_top = min(SEL_TOP_N, n_blk)
    sel_start = np.arange(n_blk) * SEL_BLOCK
    cover = jnp.asarray(((cmp_start[:, None] < sel_start[None, :] + SEL_BLOCK) &
                         (cmp_start[:, None] + CMP_LEN > sel_start[None, :])).astype(np.float32))
    ks_blocks = jnp.transpose(k_slc.reshape(B, n_blk, SEL_BLOCK, N_KV_HEADS, dh), (0, 3, 1, 2, 4))
    vs_blocks = jnp.transpose(v_slc.reshape(B, n_blk, SEL_BLOCK, N_KV_HEADS, dh), (0, 3, 1, 2, 4))
    pad = ((0, 0), (WINDOW, 0), (0, 0), (0, 0))
    k_win_pad = jnp.pad(k_win, pad)
    v_win_pad = jnp.pad(v_win, pad)
    blk_id = jnp.arange(n_blk)
    b_idx = jnp.arange(B)[:, None, None, None]
    g_idx = jnp.arange(N_KV_HEADS)[None, :, None, None]

    def one_block(qi):
        q0 = qi * Q_BLOCK
        t = q0 + jnp.arange(Q_BLOCK)
        qb = lax.dynamic_slice_in_dim(q, q0, Q_BLOCK, axis=1).reshape(B, Q_BLOCK, N_KV_HEADS, GROUP, dh)
        qrb = lax.dynamic_slice_in_dim(q_rot, q0, Q_BLOCK, axis=1).reshape(B, Q_BLOCK, N_KV_HEADS, GROUP, dh)
        lc = jnp.einsum('bqgrd,bcgd->bgrqc', qb, k_cmp) * scale
        pc = masked_softmax(lc, cmp_end[None, :] <= t[:, None])
        oc = jnp.einsum('bgrqc,bcgd->bqgrd', pc.astype(v_cmp.dtype), v_cmp)
        imp = jnp.einsum('bgrqc,cn->bgqn', pc, cover)
        tb = t // SEL_BLOCK
        future = blk_id[None, :] > tb[:, None]
        dist = tb[:, None] - blk_id[None, :]
        forced = (blk_id[None, :] == 0) | ((dist >= 0) & (dist < N_LOCAL_BLOCKS))
        score = jnp.where(future, -jnp.inf, jnp.where(forced, jnp.inf, imp))
        _, sel = lax.top_k(score, n_top)
        ksel = ks_blocks[b_idx, g_idx, sel].reshape(B, N_KV_HEADS, Q_BLOCK, n_top * SEL_BLOCK, dh)
        vsel = vs_blocks[b_idx, g_idx, sel].reshape(B, N_KV_HEADS, Q_BLOCK, n_top * SEL_BLOCK, dh)
        kpos = (sel[..., None] * SEL_BLOCK + jnp.arange(SEL_BLOCK)).reshape(B, N_KV_HEADS, Q_BLOCK, n_top * SEL_BLOCK)
        ls = jnp.einsum('bqgrd,bgqkd->bgrqk', qrb, ksel) * scale
        ps = masked_softmax(ls, (kpos <= t[None, None, :, None])[:, :, None])
        osl = jnp.einsum('bgrqk,bgqkd->bqgrd', ps.astype(vsel.dtype), vsel)
        kw = lax.dynamic_slice_in_dim(k_win_pad, q0, Q_BLOCK + WINDOW, axis=1)
        vw = lax.dynamic_slice_in_dim(v_win_pad, q0, Q_BLOCK + WINDOW, axis=1)
        wpos = q0 - WINDOW + jnp.arange(Q_BLOCK + WINDOW)
        diff = t[:, None] - wpos[None, :]
        mw = (diff >= 0) & (diff < WINDOW) & (wpos[None, :] >= 0)
        lw = jnp.einsum('bqgrd,bkgd->bgrqk', qrb, kw) * scale
        pw = masked_softmax(lw, mw)
        ow = jnp.einsum('bgrqk,bkgd->bqgrd', pw.astype(vw.dtype), vw)
        gb = lax.dynamic_slice_in_dim(gates, q0, Q_BLOCK, axis=1).reshape(B, Q_BLOCK, N_KV_HEADS, GROUP, N_BRANCH)
        o = gb[..., 0:1] * oc + gb[..., 1:2] * osl + gb[..., 2:3] * ow
        return o.reshape(B, Q_BLOCK, H * dh)

    out = lax.map(one_block, jnp.arange(S // Q_BLOCK))
    return jnp.transpose(out, (1, 0, 2, 3)).reshape(B, S, H * dh)


def pool_mixer(u, w_pool, pool_scale):
    B, S, _ = u.shape
    ug = u.reshape(B, S, N_POOL_GROUPS, POOL_GROUP).astype(jnp.float32)
    c = jnp.pad(jnp.cumsum(ug, axis=1), ((0, 0), (1, 0), (0, 0), (0, 0)))
    outs = []
    for g, w in enumerate(POOL_WINDOWS):
        cg = c[:, :, g]
        hi = cg[:, 1:]
        lo = jnp.pad(cg[:, :S + 1 - w], ((0, 0), (w - 1, 0), (0, 0)))
        cnt = jnp.minimum(jnp.arange(1, S + 1), w).astype(jnp.float32)[None, :, None]
        outs.append((hi - lo) / cnt - ug[:, :, g])
    d = jnp.stack(outs, axis=2).astype(u.dtype)
    y = jnp.einsum('bsgc,gce->bsge', d, w_pool).reshape(B, S, POOL_WIDTH)
    return y * pool_scale


def hybrid_mixer(h, w_in, pe_k, w1_k, b1_k, w2_k, pe_v, w1_v, b1_v, w2_v, w_pool, pool_scale, w_out):
    B, S, _ = h.shape
    z = h @ w_in
    o1 = NSA_WIDTH
    o2 = o1 + 6 * KV_WIDTH
    o3 = o2 + N_HEADS * N_BRANCH
    q = z[..., :o1].reshape(B, S, N_HEADS, HEAD_DIM)
    kv = z[..., o1:o2].reshape(B, S, 6, N_KV_HEADS, HEAD_DIM)
    gates = jax.nn.sigmoid(z[..., o2:o3]).reshape(B, S, N_HEADS, N_BRANCH)
    u = z[..., o3:]
    a = nsa_attention(q, kv[:, :, 0], kv[:, :, 1], kv[:, :, 2], kv[:, :, 3], kv[:, :, 4], kv[:, :, 5], gates,
                      pe_k, w1_k, b1_k, w2_k, pe_v, w1_v, b1_v, w2_v)
    p = pool_mixer(u, w_pool, pool_scale)
    return jnp.concatenate([a, p], axis=-1) @ w_out


def memory_cross_attention(h, m, wq, wk, wv, wo):
    B, S, _ = h.shape
    M = m.shape[1]
    q = (h @ wq).reshape(B, S, X_HEADS, X_HEAD_DIM)
    k = (m @ wk).reshape(B, M, X_HEADS, X_HEAD_DIM)
    v = (m @ wv).reshape(B, M, X_HEADS, X_HEAD_DIM)
    logits = jnp.einsum('bshd,bmhd->bhsm', q, k).astype(jnp.float32) * (X_HEAD_DIM ** -0.5)
    p = jax.nn.softmax(logits, axis=-1).astype(v.dtype)
    o = jnp.einsum('bhsm,bmhd->bshd', p, v).reshape(B, S, X_HEADS * X_HEAD_DIM)
    return o @ wo


def swiglu(h, wg, wu, wd):
    return (jax.nn.silu(h @ wg) * (h @ wu)) @ wd


def moe_swiglu(h, w_router, wg, wu, wd):
    logits = (h @ w_router).astype(jnp.float32)
    top_v, top_i = lax.top_k(logits, TOP_K)
    gate = jax.nn.softmax(top_v, axis=-1)
    comb = jnp.sum(jax.nn.one_hot(top_i, N_EXPERTS, dtype=jnp.float32) * gate[..., None], axis=-2)
    out = jnp.zeros_like(h)
    for e in range(N_EXPERTS):
        out = out + comb[..., e:e + 1].astype(h.dtype) * swiglu(h, wg[e], wu[e], wd[e])
    return out


def setup_inputs(seed: int = 0) -> dict:
    key = jax.random.key(seed)
    ks = iter(jax.random.split(key, 40))

    def nrm(shape, std):
        return jax.random.normal(next(ks), shape, jnp.float32) * std

    def gain(shape):
        return 1.0 + nrm(shape, 0.02)

    L = DEPTH
    return {
        "x": nrm((BATCH, SEQ, D_MODEL), 1.0),
        "mem": nrm((BATCH, MEM_LEN, D_MODEL), 1.0),
        "norm_mix": gain((L, D_MODEL)),
        "w_in": nrm((L, D_MODEL, IN_WIDTH), D_MODEL ** -0.5),
        "cmp_pe_k": nrm((L, CMP_LEN, HEAD_DIM), 0.02),
        "cmp_w1_k": nrm((L, CMP_LEN * HEAD_DIM, CMP_HIDDEN), (CMP_LEN * HEAD_DIM) ** -0.5),
        "cmp_b1_k": nrm((L, CMP_HIDDEN), 0.02),
        "cmp_w2_k": nrm((L, CMP_HIDDEN, HEAD_DIM), CMP_HIDDEN ** -0.5),
        "cmp_pe_v": nrm((L, CMP_LEN, HEAD_DIM), 0.02),
        "cmp_w1_v": nrm((L, CMP_LEN * HEAD_DIM, CMP_HIDDEN), (CMP_LEN * HEAD_DIM) ** -0.5),
        "cmp_b1_v": nrm((L, CMP_HIDDEN), 0.02),
        "cmp_w2_v": nrm((L, CMP_HIDDEN, HEAD_DIM), CMP_HIDDEN ** -0.5),
        "w_pool": nrm((L, N_POOL_GROUPS, POOL_GROUP, POOL_GROUP), POOL_GROUP ** -0.5),
        "pool_scale": gain((L, POOL_WIDTH)),
        "w_out": nrm((L, MIX_WIDTH, D_MODEL), MIX_WIDTH ** -0.5),
        "norm_x": gain((L, D_MODEL)),
        "norm_mem": gain((L, D_MODEL)),
        "wq_x": nrm((L, D_MODEL, X_HEADS * X_HEAD_DIM), D_MODEL ** -0.5),
        "wk_x": nrm((L, D_MODEL, X_HEADS * X_HEAD_DIM), D_MODEL ** -0.5),
        "wv_x": nrm((L, D_MODEL, X_HEADS * X_HEAD_DIM), D_MODEL ** -0.5),
        "wo_x": nrm((L, X_HEADS * X_HEAD_DIM, D_MODEL), D_MODEL ** -0.5),
        "norm_ffn": gain((L, D_MODEL)),
        "ffn_wg": nrm((N_DENSE, D_MODEL, D_FF), D_MODEL ** -0.5),
        "ffn_wu": nrm((N_DENSE, D_MODEL, D_FF), D_MODEL ** -0.5),
        "ffn_wd": nrm((N_DENSE, D_FF, D_MODEL), D_FF ** -0.5),
        "moe_router": nrm((N_MOE, D_MODEL, N_EXPERTS), D_MODEL ** -0.5),
        "moe_wg": nrm((N_MOE, N_EXPERTS, D_MODEL, D_FF_EXPERT), D_MODEL ** -0.5),
        "moe_wu": nrm((N_MOE, N_EXPERTS, D_MODEL, D_FF_EXPERT), D_MODEL ** -0.5),
        "moe_wd": nrm((N_MOE, N_EXPERTS, D_FF_EXPERT, D_MODEL), D_FF_EXPERT ** -0.5),
        "norm_final": gain((D_MODEL,)),
    }


def reference(x, mem, norm_mix, w_in, cmp_pe_k, cmp_w1_k, cmp_b1_k, cmp_w2_k,
              cmp_pe_v, cmp_w1_v, cmp_b1_v, cmp_w2_v, w_pool, pool_scale, w_out,
              norm_x, norm_mem, wq_x, wk_x, wv_x, wo_x, norm_ffn,
              ffn_wg, ffn_wu, ffn_wd, moe_router, moe_wg, moe_wu, moe_wd, norm_final):
    for layer in range(DEPTH):
        h = rms_norm(x, norm_mix[layer])
        x = x + hybrid_mixer(h, w_in[layer],
                             cmp_pe_k[layer], cmp_w1_k[layer], cmp_b1_k[layer], cmp_w2_k[layer],
                             cmp_pe_v[layer], cmp_w1_v[layer], cmp_b1_v[layer], cmp_w2_v[layer],
                             w_pool[layer], pool_scale[layer], w_out[layer])
        h = rms_norm(x, norm_x[layer])
        m = rms_norm(mem, norm_mem[layer])
        x = x + memory_cross_attention(h, m, wq_x[layer], wk_x[layer], wv_x[layer], wo_x[layer])
        h = rms_norm(x, norm_ffn[layer])
        j = layer // 2
        if layer % 2 == 0:
            x = x + swiglu(h, ffn_wg[j], ffn_wu[j], ffn_wd[j])
        else:
            x = x + moe_swiglu(h, moe_router[j], moe_wg[j], moe_wu[j], moe_wd[j])
    return rms_norm(x, norm_final)
```

```python
import functools
import math

import numpy as np
import jax
import jax.numpy as jnp
from jax import lax
from jax.experimental import pallas as pl
from jax.experimental.pallas import tpu as pltpu

F32 = jnp.float32
BF16 = jnp.bfloat16

D_MODEL = 1024
HEAD_DIM = 64
N_HEADS = 8
N_KV_HEADS = 2
GROUP = N_HEADS // N_KV_HEADS
NSA_WIDTH = N_HEADS * HEAD_DIM
KV_WIDTH = N_KV_HEADS * HEAD_DIM
N_BRANCH = 3
N_GATES = N_HEADS * N_BRANCH
POOL_WIDTH = 512
POOL_WINDOWS = (2, 4, 8, 16)
POOL_GROUP = 128
POOL_HALO = 16
ROPE_DIM = 16
ROPE_HALF = 8
ROPE_THETA = 500000.0
CMP_LEN = 32
CMP_STRIDE = 16
CMP_HIDDEN = 256
SEL_BLOCK = 64
SEL_TOP_N = 16
N_LOCAL_BLOCKS = 2
WINDOW = 512
X_HEADS = 4
X_HEAD_DIM = 256
N_EXPERTS = 8
EPS = 1e-6

LANE = 128
IN_PACKED = NSA_WIDTH + 6 * KV_WIDTH + LANE + POOL_WIDTH
GATE_OFF = NSA_WIDTH + 6 * KV_WIDTH
POOL_OFF = GATE_OFF + LANE

NEG = -1e30
TINY = float(np.finfo(np.float32).tiny)
VMEM_LIMIT = 56 * 1024 * 1024


def _dot(a, b):
    return jnp.dot(a, b, preferred_element_type=F32)


def _dot_nt(a, b):
    return lax.dot_general(a, b, (((1,), (1,)), ((), ())), preferred_element_type=F32)


def _rms(x, g):
    y = x * lax.rsqrt(jnp.mean(x * x, axis=-1, keepdims=True) + EPS)
    return y * g


def _params(sem, limit=VMEM_LIMIT):
    return pltpu.CompilerParams(dimension_semantics=sem, vmem_limit_bytes=limit)


def _mixer_proj_kernel(x_ref, g_ref, w_ref, cos_ref, sin_ref,
                       q_ref, qr_ref, kcr_ref, vcr_ref, ks_ref, vs_ref, kw_ref, vw_ref,
                       gate_ref, u_ref):
    h = _rms(x_ref[0], g_ref[...]).astype(BF16)
    z = _dot(h, w_ref[...])
    tm = z.shape[0]
    cos = cos_ref[...]
    sin = sin_ref[...]
    lane = lax.broadcasted_iota(jnp.int32, (tm, LANE), 1)
    first = (lane & (HEAD_DIM - 1)) < ROPE_HALF
    scale = HEAD_DIM ** -0.5

    def rope(xs):
        partner = jnp.where(first, pltpu.roll(xs, LANE - ROPE_HALF, 1), pltpu.roll(xs, ROPE_HALF, 1))
        return xs * cos + partner * sin

    for s in range(NSA_WIDTH // LANE):
        xs = z[:, s * LANE:(s + 1) * LANE]
        xr = rope(xs)
        for hh in range(2):
            sl = slice(hh * HEAD_DIM, (hh + 1) * HEAD_DIM)
            q_ref[0, 2 * s + hh] = (xs[:, sl] * scale).astype(BF16)
            qr_ref[0, 2 * s + hh] = (xr[:, sl] * scale).astype(BF16)

    def kv_slab(i):
        return z[:, NSA_WIDTH + i * KV_WIDTH:NSA_WIDTH + (i + 1) * KV_WIDTH]

    slabs = (kv_slab(0), kv_slab(1), rope(kv_slab(2)), kv_slab(3), rope(kv_slab(4)), kv_slab(5))
    refs = (kcr_ref, vcr_ref, ks_ref, vs_ref, kw_ref, vw_ref)
    for slab, ref in zip(slabs, refs):
        for gg in range(N_KV_HEADS):
            ref[0, gg] = slab[:, gg * HEAD_DIM:(gg + 1) * HEAD_DIM].astype(ref.dtype)

    sig = jax.nn.sigmoid(z[:, GATE_OFF:GATE_OFF + LANE])
    gate_ref[0, 0] = sig
    gate_ref[0, 1] = pltpu.roll(sig, LANE - GROUP * N_BRANCH, 1)
    u_ref[0] = z[:, POOL_OFF:POOL_OFF + POOL_WIDTH]


def _mixer_proj(x, g, w_packed, cos_t, sin_t, tm=512):
    B, S, D = x.shape
    hm = lambda n, dt: jax.ShapeDtypeStruct((B, n, S, HEAD_DIM), dt)
    hspec = lambda n: pl.BlockSpec((1, n, tm, HEAD_DIM), lambda b, i: (b, 0, i, 0))
    out_shape = (hm(N_HEADS, BF16), hm(N_HEADS, BF16),
                 hm(N_KV_HEADS, F32), hm(N_KV_HEADS, F32),
                 hm(N_KV_HEADS, BF16), hm(N_KV_HEADS, BF16), hm(N_KV_HEADS, BF16), hm(N_KV_HEADS, BF16),
                 jax.ShapeDtypeStruct((B, N_KV_HEADS, S, LANE), F32),
                 jax.ShapeDtypeStruct((B, S, POOL_WIDTH), F32))
    out_specs = (hspec(N_HEADS), hspec(N_HEADS),
                 hspec(N_KV_HEADS), hspec(N_KV_HEADS),
                 hspec(N_KV_HEADS), hspec(N_KV_HEADS), hspec(N_KV_HEADS), hspec(N_KV_HEADS),
                 pl.BlockSpec((1, N_KV_HEADS, tm, LANE), lambda b, i: (b, 0, i, 0)),
                 pl.BlockSpec((1, tm, POOL_WIDTH), lambda b, i: (b, i, 0)))
    return pl.pallas_call(
        _mixer_proj_kernel,
        grid=(B, S // tm),
        in_specs=[pl.BlockSpec((1, tm, D), lambda b, i: (b, i, 0)),
                  pl.BlockSpec((1, D), lambda b, i: (0, 0)),
                  pl.BlockSpec((D, IN_PACKED), lambda b, i: (0, 0)),
                  pl.BlockSpec((tm, LANE), lambda b, i: (i, 0)),
                  pl.BlockSpec((tm, LANE), lambda b, i: (i, 0))],
        out_specs=out_specs,
        out_shape=out_shape,
        compiler_params=_params(("parallel", "parallel")),
        name="mixer_proj",
    )(x, g, w_packed, cos_t, sin_t)


def _gelu_tanh(x):
    return 0.5 * x * (1.0 + jnp.tanh(math.sqrt(2.0 / math.pi) * (x + 0.044715 * (x * x * x))))


def _compress_kernel(kc_ref, vc_ref, pek_ref, w1k_ref, b1k_ref, w2k_ref,
                     pev_ref, w1v_ref, b1v_ref, w2v_ref, ko_ref, vo_ref):
    def comp(x_ref, pe_ref, w1_ref, b1_ref, w2_ref):
        x = x_ref[0, 0]
        n = x.shape[0]
        a = _dot((x + pe_ref[0:1, :]).astype(BF16), w1_ref[0])
        b = _dot((x + pe_ref[1:2, :]).astype(BF16), w1_ref[1])
        pre = a + pltpu.roll(b, n - 1, 0) + b1_ref[...]
        return _dot(_gelu_tanh(pre).astype(BF16), w2_ref[...])

    ko_ref[0, 0] = comp(kc_ref, pek_ref, w1k_ref, b1k_ref, w2k_ref).astype(ko_ref.dtype)
    vo_ref[0, 0] = comp(vc_ref, pev_ref, w1v_ref, b1v_ref, w2v_ref).astype(vo_ref.dtype)


def _compress(kcr, vcr, pek, w1k, b1k, w2k, pev, w1v, b1v, w2v):
    B, G, S, _ = kcr.shape
    n = S // CMP_STRIDE
    cw = CMP_STRIDE * HEAD_DIM
    kc = kcr.reshape(B, G, n, cw)
    vc = vcr.reshape(B, G, n, cw)
    xspec = pl.BlockSpec((1, 1, n, cw), lambda b, g: (b, g, 0, 0))
    full = lambda a: pl.BlockSpec(a.shape, lambda b, g: (0,) * a.ndim)
    ospec = pl.BlockSpec((1, 1, n, HEAD_DIM), lambda b, g: (b, g, 0, 0))
    oshape = jax.ShapeDtypeStruct((B, G, n, HEAD_DIM), BF16)
    return pl.pallas_call(
        _compress_kernel,
        grid=(B, G),
        in_specs=[xspec, xspec, full(pek), full(w1k), full(b1k), full(w2k),
                  full(pev), full(w1v), full(b1v), full(w2v)],
        out_specs=(ospec, ospec),
        out_shape=(oshape, oshape),
        compiler_params=_params(("parallel", "parallel")),
        name="compress_kv",
    )(kc, vc, pek, w1k, b1k, w2k, pev, w1v, b1v, w2v)


def _nsa_kernel(q_ref, qr_ref, kc_ref, vc_ref, ks_ref, vs_ref, kw_ref, vw_ref, gate_ref,
                covt_ref, expand_ref, o_ref, *, tq, tk, tw):
    qt = pl.program_id(2)
    q0 = qt * tq
    R = GROUP * tq
    q = q_ref[0].reshape(R, HEAD_DIM)
    qr = qr_ref[0].reshape(R, HEAD_DIM)
    n_cmp = kc_ref.shape[2]
    n_blk = covt_ref.shape[0]

    t_row = q0 + (lax.broadcasted_iota(jnp.int32, (R, 1), 0) & (tq - 1))
    t_q = q0 + lax.broadcasted_iota(jnp.int32, (tq, 1), 0)

    s = _dot_nt(q, kc_ref[0, 0])
    cmp_end = lax.broadcasted_iota(jnp.int32, (1, n_cmp), 1) * CMP_STRIDE + (CMP_LEN - 1)
    valid = cmp_end <= t_row
    s = jnp.where(valid, s, NEG)
    m = jnp.max(s, axis=-1, keepdims=True)
    e = jnp.where(valid, jnp.exp(s - m), 0.0)
    pc = e / jnp.maximum(jnp.sum(e, axis=-1, keepdims=True), TINY)
    oc = _dot(pc.astype(BF16), vc_ref[0, 0])

    psum = jnp.sum(pc.reshape(GROUP, tq, n_cmp), axis=0)
    p_hi = psum.astype(BF16)
    r1 = psum - p_hi.astype(F32)
    p_mid = r1.astype(BF16)
    p_lo = (r1 - p_mid.astype(F32)).astype(BF16)
    covt = covt_ref[...]
    imp_t = _dot_nt(covt, p_hi) + _dot_nt(covt, p_mid) + _dot_nt(covt, p_lo)

    jb = lax.broadcasted_iota(jnp.int32, (n_blk, tq), 0)
    tb = (q0 + lax.broadcasted_iota(jnp.int32, (n_blk, tq), 1)) // SEL_BLOCK
    dist = tb - jb
    forced = (jb == 0) | ((dist >= 0) & (dist < N_LOCAL_BLOCKS))
    score = jnp.where(jb > tb, -jnp.inf, jnp.where(forced, jnp.inf, imp_t))
    rank = jnp.zeros((n_blk, tq), jnp.int32)
    for jp in range(n_blk):
        rowv = score[jp:jp + 1, :]
        beats = (rowv > score) | ((rowv == score) & (jb > jp))
        rank = rank + jnp.where(beats, 1, 0)
    sel_t = jnp.where(rank < min(SEL_TOP_N, n_blk), 1.0, 0.0)
    if n_blk < LANE:
        sel_t = jnp.concatenate([sel_t, jnp.zeros((LANE - n_blk, tq), F32)], axis=0)
    sel = sel_t.T.astype(BF16)

    def flash(qv, k_ref, v_ref, lo, hi, width, bias_fn):
        def body(c, carry):
            m_i, l_i, acc = carry
            start = pl.multiple_of(c * width, width)
            kc = k_ref[0, 0, pl.ds(start, width), :]
            vc = v_ref[0, 0, pl.ds(start, width), :]
            kpos = start + lax.broadcasted_iota(jnp.int32, (1, width), 1)
            sc = _dot_nt(qv, kc).reshape(GROUP, tq, width) + bias_fn(c, kpos)[None]
            sc = sc.reshape(R, width)
            m_new = jnp.maximum(m_i, jnp.max(sc, axis=-1, keepdims=True))
            alpha = jnp.exp(m_i - m_new)
            p = jnp.exp(sc - m_new)
            l_new = alpha * l_i + jnp.sum(p, axis=-1, keepdims=True)
            acc_new = alpha * acc + _dot(p.astype(BF16), vc)
            return m_new, l_new, acc_new

        init = (jnp.full((R, 1), NEG, F32), jnp.zeros((R, 1), F32), jnp.zeros((R, HEAD_DIM), F32))
        _, l_f, acc_f = lax.fori_loop(lo, hi, body, init)
        return acc_f / jnp.maximum(l_f, TINY)

    def sel_bias(c, kpos):
        picked = _dot(sel, expand_ref[c])
        return jnp.where((picked > 0.5) & (kpos <= t_q), 0.0, NEG)

    osl = flash(qr, ks_ref, vs_ref, 0, (q0 + tq + tk - 1) // tk, tk, sel_bias)

    def win_bias(c, kpos):
        diff = t_q - kpos
        return jnp.where((diff >= 0) & (diff < WINDOW), 0.0, NEG)

    w_lo = jnp.maximum(q0 - WINDOW, 0) // tw
    ow = flash(qr, kw_ref, vw_ref, w_lo, (q0 + tq) // tw, tw, win_bias)

    gt = gate_ref[0, 0]
    oc = oc.reshape(GROUP, tq, HEAD_DIM)
    osl = osl.reshape(GROUP, tq, HEAD_DIM)
    ow = ow.reshape(GROUP, tq, HEAD_DIM)
    outs = []
    for hh in range(GROUP):
        c0 = hh * N_BRANCH
        outs.append(gt[:, c0:c0 + 1] * oc[hh] + gt[:, c0 + 1:c0 + 2] * osl[hh]
                    + gt[:, c0 + 2:c0 + 3] * ow[hh])
    o_ref[0] = jnp.concatenate(outs, axis=-1).astype(o_ref.dtype)


def _nsa_tables(S, tk):
    n_cmp = S // CMP_STRIDE
    n_blk = S // SEL_BLOCK
    cs = np.arange(n_cmp) * CMP_STRIDE
    ss = np.arange(n_blk) * SEL_BLOCK
    cover_t = ((cs[None, :] < ss[:, None] + SEL_BLOCK) & (cs[None, :] + CMP_LEN > ss[:, None]))
    cover_t[:, n_cmp - 1] = False
    rows = max(n_blk, LANE)
    key_blk = np.arange(S) // SEL_BLOCK
    expand = (np.arange(rows)[:, None] == key_blk[None, :])
    expand = expand.reshape(rows, S // tk, tk).transpose(1, 0, 2)
    return jnp.asarray(cover_t, BF16), jnp.asarray(expand, BF16)


def _nsa_attention(q, qr, kc, vc, ks, vs, kw, vw, gates, tq=128, tk=512, tw=128):
    B, _, S, _ = q.shape
    tk = min(tk, S)
    covt, expand = _nsa_tables(S, tk)
    n_cmp = kc.shape[2]
    qspec = pl.BlockSpec((1, GROUP, tq, HEAD_DIM), lambda b, g, i: (b, g, i, 0))
    cspec = pl.BlockSpec((1, 1, n_cmp, HEAD_DIM), lambda b, g, i: (b, g, 0, 0))
    sspec = pl.BlockSpec((1, 1, S, HEAD_DIM), lambda b, g, i: (b, g, 0, 0))
    return pl.pallas_call(
        functools.partial(_nsa_kernel, tq=tq, tk=tk, tw=tw),
        grid=(B, N_KV_HEADS, S // tq),
        in_specs=[qspec, qspec, cspec, cspec, sspec, sspec, sspec, sspec,
                  pl.BlockSpec((1, 1, tq, LANE), lambda b, g, i: (b, g, i, 0)),
                  pl.BlockSpec(covt.shape, lambda b, g, i: (0, 0)),
                  pl.BlockSpec(expand.shape, lambda b, g, i: (0, 0, 0))],
        out_specs=pl.BlockSpec((1, tq, GROUP * HEAD_DIM), lambda b, g, i: (b, i, g)),
        out_shape=jax.ShapeDtypeStruct((B, S, NSA_WIDTH), BF16),
        compiler_params=_params(("parallel", "parallel", "parallel")),
        name="nsa_attention",
    )(q, qr, kc, vc, ks, vs, kw, vw, gates, covt, expand)


def _pool_kernel(u_ref, w_ref, scale_ref, o_ref, buf):
    i = pl.program_id(1)
    tm = u_ref.shape[1]

    @pl.when(i == 0)
    def _():
        buf[0:POOL_HALO, :] = jnp.zeros((POOL_HALO, POOL_WIDTH), F32)

    @pl.when(i > 0)
    def _():
        buf[0:POOL_HALO, :] = buf[tm:tm + POOL_HALO, :]

    buf[POOL_HALO:POOL_HALO + tm, :] = u_ref[0]
    t1 = i * tm + lax.broadcasted_iota(jnp.int32, (tm, 1), 0) + 1
    for gi, w in enumerate(POOL_WINDOWS):
        cols = slice(gi * POOL_GROUP, (gi + 1) * POOL_GROUP)
        cur = buf[POOL_HALO:POOL_HALO + tm, cols]
        tot = cur
        for k in range(1, w):
            tot = tot + buf[POOL_HALO - k:POOL_HALO - k + tm, cols]
        cnt = jnp.minimum(t1, w).astype(F32)
        d = tot / cnt - cur
        y = _dot(d.astype(BF16), w_ref[gi])
        o_ref[0, :, cols] = (y * scale_ref[:, cols]).astype(o_ref.dtype)


def _pool_mixer(u, w_pool, pool_scale, tm=512):
    B, S, _ = u.shape
    return pl.pallas_call(
        _pool_kernel,
        grid=(B, S // tm),
        in_specs=[pl.BlockSpec((1, tm, POOL_WIDTH), lambda b, i: (b, i, 0)),
                  pl.BlockSpec(w_pool.shape, lambda b, i: (0, 0, 0)),
                  pl.BlockSpec((1, POOL_WIDTH), lambda b, i: (0, 0))],
        out_specs=pl.BlockSpec((1, tm, POOL_WIDTH), lambda b, i: (b, i, 0)),
        out_shape=jax.ShapeDtypeStruct((B, S, POOL_WIDTH), BF16),
        scratch_shapes=[pltpu.VMEM((tm + POOL_HALO, POOL_WIDTH), F32)],
        compiler_params=_params(("parallel", "arbitrary")),
        name="pool_mixer",
    )(u, w_pool, pool_scale)


def _out_proj_kernel(x_ref, a_ref, p_ref, wa_ref, wp_ref, o_ref):
    o_ref[...] = x_ref[...] + _dot(a_ref[...], wa_ref[...]) + _dot(p_ref[...], wp_ref[...])


def _out_proj(x2, a2, p2, wa, wp, tm=512):
    T, D = x2.shape
    row = lambda n: pl.BlockSpec((tm, n), lambda i: (i, 0))
    full = lambda a: pl.BlockSpec(a.shape, lambda i: (0, 0))
    return pl.pallas_call(
        _out_proj_kernel,
        grid=(T // tm,),
        in_specs=[row(D), row(a2.shape[1]), row(p2.shape[1]), full(wa), full(wp)],
        out_specs=row(D),
        out_shape=jax.ShapeDtypeStruct((T, D), F32),
        compiler_params=_params(("parallel",)),
        name="mixer_out_proj",
    )(x2, a2, p2, wa, wp)


def _xkv_kernel(m_ref, g_ref, w_ref, k_ref, v_ref):
    h = _rms(m_ref[0], g_ref[...]).astype(BF16)
    kv = _dot(h, w_ref[...])
    d = k_ref.shape[2]
    k_ref[0] = kv[:, :d].astype(k_ref.dtype)
    v_ref[0] = kv[:, d:].astype(v_ref.dtype)


def _xattn_kv(mem, g, wkv):
    B, M, D = mem.shape
    ospec = pl.BlockSpec((1, M, D), lambda b: (b, 0, 0))
    oshape = jax.ShapeDtypeStruct((B, M, D), BF16)
    return pl.pallas_call(
        _xkv_kernel,
        grid=(B,),
        in_specs=[pl.BlockSpec((1, M, D), lambda b: (b, 0, 0)),
                  pl.BlockSpec((1, D), lambda b: (0, 0)),
                  pl.BlockSpec(wkv.shape, lambda b: (0, 0))],
        out_specs=(ospec, ospec),
        out_shape=(oshape, oshape),
        compiler_params=_params(("parallel",)),
        name="xattn_kv",
    )(mem, g, wkv)


def _xattn_kernel(x_ref, g_ref, wq_ref, k_ref, v_ref, wo_ref, o_ref):
    x = x_ref[0]
    h = _rms(x, g_ref[...]).astype(BF16)
    scale = X_HEAD_DIM ** -0.5
    q = (_dot(h, wq_ref[...]) * scale).astype(BF16)
    outs = []
    for hd in range(X_HEADS):
        sl = slice(hd * X_HEAD_DIM, (hd + 1) * X_HEAD_DIM)
        s = _dot_nt(q[:, sl], k_ref[0, :, sl])
        e = jnp.exp(s - jnp.max(s, axis=-1, keepdims=True))
        p = e / jnp.sum(e, axis=-1, keepdims=True)
        outs.append(_dot(p.astype(BF16), v_ref[0, :, sl]).astype(BF16))
    o = jnp.concatenate(outs, axis=-1)
    o_ref[0] = x + _dot(o, wo_ref[...])


def _xattn(x, g, wq, kx, vx, wo, tm=512):
    B, S, D = x.shape
    M = kx.shape[1]
    xspec = pl.BlockSpec((1, tm, D), lambda b, i: (b, i, 0))
    full = lambda a: pl.BlockSpec(a.shape, lambda b, i: (0, 0))
    mspec = pl.BlockSpec((1, M, D), lambda b, i: (b, 0, 0))
    return pl.pallas_call(
        _xattn_kernel,
        grid=(B, S // tm),
        in_specs=[xspec, full(g), full(wq), mspec, mspec, full(wo)],
        out_specs=xspec,
        out_shape=jax.ShapeDtypeStruct((B, S, D), F32),
        compiler_params=_params(("parallel", "parallel")),
        name="xattn",
    )(x, g, wq, kx, vx, wo)


def _silu(x):
    return x * jax.nn.sigmoid(x)


def _ffn_kernel(x_ref, g_ref, wg_ref, wu_ref, wd_ref, o_ref, h_ref, acc_ref):
    f = pl.program_id(1)

    @pl.when(f == 0)
    def _():
        h_ref[...] = _rms(x_ref[...], g_ref[...]).astype(BF16)
        acc_ref[...] = jnp.zeros_like(acc_ref)

    h = h_ref[...]
    act = _silu(_dot(h, wg_ref[...])) * _dot(h, wu_ref[...])
    acc_ref[...] += _dot(act.astype(BF16), wd_ref[...])

    @pl.when(f == pl.num_programs(1) - 1)
    def _():
        o_ref[...] = x_ref[...] + acc_ref[...]


def _ffn(x2, g, wg, wu, wd, tm=512, tf=1408):
    T, D = x2.shape
    F = wg.shape[1]
    return pl.pallas_call(
        _ffn_kernel,
        grid=(T // tm, F // tf),
        in_specs=[pl.BlockSpec((tm, D), lambda i, f: (i, 0)),
                  pl.BlockSpec((1, D), lambda i, f: (0, 0)),
                  pl.BlockSpec((D, tf), lambda i, f: (0, f)),
                  pl.BlockSpec((D, tf), lambda i, f: (0, f)),
                  pl.BlockSpec((tf, D), lambda i, f: (f, 0))],
        out_specs=pl.BlockSpec((tm, D), lambda i, f: (i, 0)),
        out_shape=jax.ShapeDtypeStruct((T, D), F32),
        scratch_shapes=[pltpu.VMEM((tm, D), BF16), pltpu.VMEM((tm, D), F32)],
        compiler_params=_params(("parallel", "arbitrary")),
        name="dense_swiglu",
    )(x2, g, wg, wu, wd)


def _moe_kernel(x_ref, g_ref, wr_ref, wg_ref, wu_ref, wd_ref, gf_ref, o_ref, h_ref, comb_ref, acc_ref):
    e = pl.program_id(1)
    f = pl.program_id(2)
    first = (e == 0) & (f == 0)
    last = (e == pl.num_programs(1) - 1) & (f == pl.num_programs(2) - 1)

    @pl.when(first)
    def _():
        hf = _rms(x_ref[...], g_ref[...])
        h_ref[...] = hf.astype(BF16)
        acc_ref[...] = jnp.zeros_like(acc_ref)
        logits = jnp.dot(hf, wr_ref[...], preferred_element_type=F32, precision=lax.Precision.HIGHEST)
        lane = lax.broadcasted_iota(jnp.int32, logits.shape, 1)
        logits = jnp.where(lane < N_EXPERTS, logits, -jnp.inf)
        v1 = jnp.max(logits, axis=-1, keepdims=True)
        i1 = jnp.min(jnp.where(logits == v1, lane, LANE), axis=-1, keepdims=True)
        rest = jnp.where(lane == i1, -jnp.inf, logits)
        v2 = jnp.max(rest, axis=-1, keepdims=True)
        i2 = jnp.min(jnp.where(rest == v2, lane, LANE), axis=-1, keepdims=True)
        e2 = jnp.exp(v2 - v1)
        den = 1.0 + e2
        comb_ref[...] = jnp.where(lane == i1, 1.0 / den, jnp.where(lane == i2, e2 / den, 0.0))

    lane = lax.broadcasted_iota(jnp.int32, comb_ref.shape, 1)
    ce = jnp.sum(jnp.where(lane == e, comb_ref[...], 0.0), axis=-1, keepdims=True)
    h = h_ref[...]
    act = _silu(_dot(h, wg_ref[0])) * _dot(h, wu_ref[0])
    acc_ref[...] += ce * _dot(act.astype(BF16), wd_ref[0])

    @pl.when(last)
    def _():
        o_ref[...] = _rms(x_ref[...] + acc_ref[...], gf_ref[...])


def _moe(x2, g, w_router, wg, wu, wd, g_final, tm=1024, tf=512):
    T, D = x2.shape
    E, _, F = wg.shape
    return pl.pallas_call(
        _moe_kernel,
        grid=(T // tm, E, F // tf),
        in_specs=[pl.BlockSpec((tm, D), lambda i, e, f: (i, 0)),
                  pl.BlockSpec((1, D), lambda i, e, f: (0, 0)),
                  pl.BlockSpec((D, LANE), lambda i, e, f: (0, 0)),
                  pl.BlockSpec((1, D, tf), lambda i, e, f: (e, 0, f)),
                  pl.BlockSpec((1, D, tf), lambda i, e, f: (e, 0, f)),
                  pl.BlockSpec((1, tf, D), lambda i, e, f: (e, f, 0)),
                  pl.BlockSpec((1, D), lambda i, e, f: (0, 0))],
        out_specs=pl.BlockSpec((tm, D), lambda i, e, f: (i, 0)),
        out_shape=jax.ShapeDtypeStruct((T, D), F32),
        scratch_shapes=[pltpu.VMEM((tm, D), BF16), pltpu.VMEM((tm, LANE), F32), pltpu.VMEM((tm, D), F32)],
        compiler_params=_params(("parallel", "arbitrary", "arbitrary")),
        name="moe_swiglu_final_norm",
    )(x2, g, w_router, wg, wu, wd, g_final)


def _rope_tables(S):
    inv = ROPE_THETA ** (-jnp.arange(ROPE_HALF, dtype=F32) / ROPE_HALF)
    ang = jnp.arange(S, dtype=F32)[:, None] * inv[None, :]
    cos, sin = jnp.cos(ang), jnp.sin(ang)
    pad = HEAD_DIM - ROPE_DIM
    cos_h = jnp.concatenate([cos, cos, jnp.ones((S, pad), F32)], axis=-1)
    sin_h = jnp.concatenate([-sin, sin, jnp.zeros((S, pad), F32)], axis=-1)
    reps = LANE // HEAD_DIM
    return jnp.tile(cos_h, (1, reps)), jnp.tile(sin_h, (1, reps))


def _pack_w_in(w):
    gates = jnp.pad(w[:, GATE_OFF:GATE_OFF + N_GATES], ((0, 0), (0, LANE - N_GATES)))
    return jnp.concatenate([w[:, :GATE_OFF], gates, w[:, GATE_OFF + N_GATES:]], axis=1).astype(BF16)


def _mixer_layer(x, norm_g, w_in, pe_k, w1_k, b1_k, w2_k, pe_v, w1_v, b1_v, w2_v,
                 w_pool, pool_scale, w_out, cos_t, sin_t):
    B, S, D = x.shape
    q, qr, kcr, vcr, ks, vs, kw, vw, gates, u = _mixer_proj(
        x, norm_g.reshape(1, D), _pack_w_in(w_in), cos_t, sin_t)
    half = CMP_LEN // 2
    cw = half * HEAD_DIM
    kc, vc = _compress(
        kcr, vcr,
        pe_k.reshape(2, cw), w1_k.reshape(2, cw, CMP_HIDDEN).astype(BF16), b1_k.reshape(1, -1), w2_k.astype(BF16),
        pe_v.reshape(2, cw), w1_v.reshape(2, cw, CMP_HIDDEN).astype(BF16), b1_v.reshape(1, -1), w2_v.astype(BF16))
    a = _nsa_attention(q, qr, kc, vc, ks, vs, kw, vw, gates)
    p = _pool_mixer(u, w_pool.astype(BF16), pool_scale.reshape(1, -1))
    w_out = w_out.astype(BF16)
    out = _out_proj(x.reshape(B * S, D), a.reshape(B * S, NSA_WIDTH), p.reshape(B * S, POOL_WIDTH),
                    w_out[:NSA_WIDTH], w_out[NSA_WIDTH:])
    return out.reshape(B, S, D)


def _moe_layer(x2, g_ffn, router, wg, wu, wd, g_final):
    D = x2.shape[1]
    w_router = jnp.pad(router, ((0, 0), (0, LANE - N_EXPERTS)))
    return _moe(x2, g_ffn.reshape(1, D), w_router, wg.astype(BF16), wu.astype(BF16), wd.astype(BF16),
                g_final.reshape(1, D))


def kernel(x, mem, norm_mix, w_in, cmp_pe_k, cmp_w1_k, cmp_b1_k, cmp_w2_k, cmp_pe_v, cmp_w1_v, cmp_b1_v, cmp_w2_v, w_pool, pool_scale, w_out, norm_x, norm_mem, wq_x, wk_x, wv_x, wo_x, norm_ffn, ffn_wg, ffn_wu, ffn_wd, moe_router, moe_wg, moe_wu, moe_wd, norm_final):
    B, S, D = x.shape
    depth = norm_mix.shape[0]
    assert depth == 2, "the final RMSNorm is fused into the expert layer, which must be the last one"
    cos_t, sin_t = _rope_tables(S)
    for layer in range(depth):
        x = _mixer_layer(x, norm_mix[layer], w_in[layer],
                         cmp_pe_k[layer], cmp_w1_k[layer], cmp_b1_k[layer], cmp_w2_k[layer],
                         cmp_pe_v[layer], cmp_w1_v[layer], cmp_b1_v[layer], cmp_w2_v[layer],
                         w_pool[layer], pool_scale[layer], w_out[layer], cos_t, sin_t)
        wkv = jnp.concatenate([wk_x[layer], wv_x[layer]], axis=1).astype(BF16)
        kx, vx = _xattn_kv(mem, norm_mem[layer].reshape(1, D), wkv)
        x = _xattn(x, norm_x[layer].reshape(1, D), wq_x[layer].astype(BF16), kx, vx, wo_x[layer].astype(BF16))
        x2 = x.reshape(B * S, D)
        j = layer // 2
        if layer % 2 == 0:
            x2 = _ffn(x2, norm_ffn[layer].reshape(1, D), ffn_wg[j].astype(BF16), ffn_wu[j].astype(BF16),
                      ffn_wd[j].astype(BF16))
        else:
            x2 = _moe_layer(x2, norm_ffn[layer], moe_router[j], moe_wg[j], moe_wu[j], moe_wd[j], norm_final)
        x = x2.reshape(B, S, D)
    return x
```

```python
import functools
import math

import numpy as np
import jax
import jax.numpy as jnp
from jax import lax
from jax.experimental import pallas as pl
from jax.experimental.pallas import tpu as pltpu

F32 = jnp.float32
BF16 = jnp.bfloat16

D_MODEL = 1024
HEAD_DIM = 64
N_HEADS = 8
N_KV_HEADS = 2
GROUP = N_HEADS // N_KV_HEADS
NSA_WIDTH = N_HEADS * HEAD_DIM
KV_WIDTH = N_KV_HEADS * HEAD_DIM
N_BRANCH = 3
GATE_ROWS = 16
POOL_WIDTH = 512
POOL_WINDOWS = (2, 4, 8, 16)
POOL_GROUP = 128
POOL_HALO = 16
ROPE_DIM = 16
ROPE_HALF = 8
ROPE_THETA = 500000.0
CMP_LEN = 32
CMP_STRIDE = 16
CMP_HIDDEN = 256
SEL_BLOCK = 64
SEL_TOP_N = 16
N_LOCAL_BLOCKS = 2
WINDOW = 512
X_HEADS = 4
X_HEAD_DIM = 256
N_EXPERTS = 8
EPS = 1e-6

LANE = 128
IN_PACKED = NSA_WIDTH + 6 * KV_WIDTH + LANE + POOL_WIDTH
GATE_OFF = NSA_WIDTH + 6 * KV_WIDTH
POOL_OFF = GATE_OFF + LANE

NEG = -1e30
TINY = float(np.finfo(np.float32).tiny)
VMEM_LIMIT = 56 * 1024 * 1024


def _dot(a, b):
    return jnp.dot(a, b, preferred_element_type=F32)


def _dot_nt(a, b):
    return lax.dot_general(a, b, (((1,), (1,)), ((), ())), preferred_element_type=F32)


def _rms(x, g):
    y = x * lax.rsqrt(jnp.mean(x * x, axis=-1, keepdims=True) + EPS)
    return y * g


def _params(sem, limit=VMEM_LIMIT):
    return pltpu.CompilerParams(dimension_semantics=sem, vmem_limit_bytes=limit)


def _mixer_proj_kernel(x_ref, g_ref, w_ref, cos_ref, sin_ref,
                       qt_ref, qrt_ref, kcr_ref, vcr_ref, ks_ref, vst_ref, kw_ref, vwt_ref,
                       gate_ref, u_ref):
    h = _rms(x_ref[0], g_ref[...]).astype(BF16)
    z = _dot(h, w_ref[...])
    tm = z.shape[0]
    cos = cos_ref[...]
    sin = sin_ref[...]
    lane = lax.broadcasted_iota(jnp.int32, (tm, LANE), 1)
    first = (lane & (HEAD_DIM - 1)) < ROPE_HALF
    scale = HEAD_DIM ** -0.5

    def rope(xs):
        partner = jnp.where(first, pltpu.roll(xs, LANE - ROPE_HALF, 1), pltpu.roll(xs, ROPE_HALF, 1))
        return xs * cos + partner * sin

    for s in range(NSA_WIDTH // LANE):
        xs = z[:, s * LANE:(s + 1) * LANE]
        for src, ref in ((xs, qt_ref), (rope(xs), qrt_ref)):
            t = (src * scale).T.astype(BF16)
            ref[0, 2 * s] = t[:HEAD_DIM]
            ref[0, 2 * s + 1] = t[HEAD_DIM:]

    def kv_slab(i):
        return z[:, NSA_WIDTH + i * KV_WIDTH:NSA_WIDTH + (i + 1) * KV_WIDTH]

    for slab, ref in ((kv_slab(0), kcr_ref), (kv_slab(1), vcr_ref),
                      (rope(kv_slab(2)), ks_ref), (rope(kv_slab(4)), kw_ref)):
        for gg in range(N_KV_HEADS):
            ref[0, gg] = slab[:, gg * HEAD_DIM:(gg + 1) * HEAD_DIM].astype(ref.dtype)

    for slab, ref in ((kv_slab(3), vst_ref), (kv_slab(5), vwt_ref)):
        t = slab.T.astype(BF16)
        for gg in range(N_KV_HEADS):
            for k in range(tm // LANE):
                ref[0, gg, k] = t[gg * HEAD_DIM:(gg + 1) * HEAD_DIM, k * LANE:(k + 1) * LANE]

    sig_t = jax.nn.sigmoid(z[:, GATE_OFF:GATE_OFF + LANE]).T
    for gg in range(N_KV_HEADS):
        gate_ref[0, gg] = sig_t[gg * GATE_ROWS:(gg + 1) * GATE_ROWS]
    u_ref[0] = z[:, POOL_OFF:POOL_OFF + POOL_WIDTH]


def _mixer_proj(x, g, w_packed, cos_t, sin_t, tm=512):
    B, S, D = x.shape
    G = N_KV_HEADS
    sd = jax.ShapeDtypeStruct
    out_shape = (sd((B, N_HEADS, HEAD_DIM, S), BF16), sd((B, N_HEADS, HEAD_DIM, S), BF16),
                 sd((B, G, S, HEAD_DIM), F32), sd((B, G, S, HEAD_DIM), F32),
                 sd((B, G, S, HEAD_DIM), BF16), sd((B, G, S // LANE, HEAD_DIM, LANE), BF16),
                 sd((B, G, S, HEAD_DIM), BF16), sd((B, G, S // LANE, HEAD_DIM, LANE), BF16),
                 sd((B, G, GATE_ROWS, S), F32),
                 sd((B, S, POOL_WIDTH), F32))
    qspec = pl.BlockSpec((1, N_HEADS, HEAD_DIM, tm), lambda b, i: (b, 0, 0, i))
    kspec = pl.BlockSpec((1, G, tm, HEAD_DIM), lambda b, i: (b, 0, i, 0))
    vspec = pl.BlockSpec((1, G, tm // LANE, HEAD_DIM, LANE), lambda b, i: (b, 0, i, 0, 0))
    out_specs = (qspec, qspec, kspec, kspec, kspec, vspec, kspec, vspec,
                 pl.BlockSpec((1, G, GATE_ROWS, tm), lambda b, i: (b, 0, 0, i)),
                 pl.BlockSpec((1, tm, POOL_WIDTH), lambda b, i: (b, i, 0)))
    return pl.pallas_call(
        _mixer_proj_kernel,
        grid=(B, S // tm),
        in_specs=[pl.BlockSpec((1, tm, D), lambda b, i: (b, i, 0)),
                  pl.BlockSpec((1, D), lambda b, i: (0, 0)),
                  pl.BlockSpec((D, IN_PACKED), lambda b, i: (0, 0)),
                  pl.BlockSpec((tm, LANE), lambda b, i: (i, 0)),
                  pl.BlockSpec((tm, LANE), lambda b, i: (i, 0))],
        out_specs=out_specs,
        out_shape=out_shape,
        compiler_params=_params(("parallel", "parallel")),
        name="mixer_proj",
    )(x, g, w_packed, cos_t, sin_t)


def _gelu_tanh(x):
    return 0.5 * x * (1.0 + jnp.tanh(math.sqrt(2.0 / math.pi) * (x + 0.044715 * (x * x * x))))


def _compress_kernel(kc_ref, vc_ref, pek_ref, w1k_ref, b1k_ref, w2k_ref,
                     pev_ref, w1v_ref, b1v_ref, w2v_ref, ko_ref, vto_ref):
    def comp(x_ref, pe_ref, w1_ref, b1_ref, w2_ref):
        x = x_ref[0, 0]
        n = x.shape[0]
        a = _dot((x + pe_ref[0:1, :]).astype(BF16), w1_ref[0])
        b = _dot((x + pe_ref[1:2, :]).astype(BF16), w1_ref[1])
        pre = a + pltpu.roll(b, n - 1, 0) + b1_ref[...]
        return _dot(_gelu_tanh(pre).astype(BF16), w2_ref[...])

    ko_ref[0, 0] = comp(kc_ref, pek_ref, w1k_ref, b1k_ref, w2k_ref)[:, :HEAD_DIM].astype(ko_ref.dtype)
    vto_ref[0, 0] = comp(vc_ref, pev_ref, w1v_ref, b1v_ref, w2v_ref).T[:HEAD_DIM].astype(vto_ref.dtype)


def _compress(kcr, vcr, pek, w1k, b1k, w2k, pev, w1v, b1v, w2v):
    B, G, S, _ = kcr.shape
    n = S // CMP_STRIDE
    cw = CMP_STRIDE * HEAD_DIM
    kc = kcr.reshape(B, G, n, cw)
    vc = vcr.reshape(B, G, n, cw)
    xspec = pl.BlockSpec((1, 1, n, cw), lambda b, g: (b, g, 0, 0))
    full = lambda a: pl.BlockSpec(a.shape, lambda b, g: (0,) * a.ndim)
    return pl.pallas_call(
        _compress_kernel,
        grid=(B, G),
        in_specs=[xspec, xspec, full(pek), full(w1k), full(b1k), full(w2k),
                  full(pev), full(w1v), full(b1v), full(w2v)],
        out_specs=(pl.BlockSpec((1, 1, n, HEAD_DIM), lambda b, g: (b, g, 0, 0)),
                   pl.BlockSpec((1, 1, HEAD_DIM, n), lambda b, g: (b, g, 0, 0))),
        out_shape=(jax.ShapeDtypeStruct((B, G, n, HEAD_DIM), BF16),
                   jax.ShapeDtypeStruct((B, G, HEAD_DIM, n), BF16)),
        compiler_params=_params(("parallel", "parallel")),
        name="compress_kv",
    )(kc, vc, pek, w1k, b1k, w2k, pev, w1v, b1v, w2v)


def _nsa_kernel(qt_ref, qrt_ref, kc_ref, vct_ref, ks_ref, vst_ref, kw_ref, vwt_ref, gate_ref,
                covt_ref, o_ref, selbias_ref, *, tq, tk):
    qt = pl.program_id(2)
    q0 = qt * tq
    n_cmp = kc_ref.shape[2]
    n_blk = covt_ref.shape[0]
    heads = lambda ref: jnp.concatenate([ref[0, hh] for hh in range(GROUP)], axis=1)
    q_t = heads(qt_ref)
    qr_t = heads(qrt_ref)
    t_lane = q0 + lax.broadcasted_iota(jnp.int32, (1, tq), 1)
    per_head = lambda a: [a[:, hh * tq:(hh + 1) * tq] for hh in range(GROUP)]
    all_heads = lambda a: jnp.concatenate([a] * GROUP, axis=1)

    sc = _dot(kc_ref[0, 0], q_t)
    cmp_end = lax.broadcasted_iota(jnp.int32, (n_cmp, 1), 0) * CMP_STRIDE + (CMP_LEN - 1)
    valid = cmp_end <= t_lane
    pcs = []
    for s_h in per_head(sc):
        s_h = jnp.where(valid, s_h, NEG)
        e = jnp.where(valid, jnp.exp(s_h - jnp.max(s_h, axis=0, keepdims=True)), 0.0)
        pcs.append(e * (1.0 / jnp.maximum(jnp.sum(e, axis=0, keepdims=True), TINY)))
    oc_t = _dot(vct_ref[0, 0], jnp.concatenate(pcs, axis=1).astype(BF16))

    psum = pcs[0] + pcs[1] + pcs[2] + pcs[3]
    p_hi = psum.astype(BF16)
    r1 = psum - p_hi.astype(F32)
    p_mid = r1.astype(BF16)
    p_lo = (r1 - p_mid.astype(F32)).astype(BF16)
    covt = covt_ref[...]
    imp_t = _dot(covt, p_hi) + _dot(covt, p_mid) + _dot(covt, p_lo)

    jb = lax.broadcasted_iota(jnp.int32, (n_blk, tq), 0)
    tb = (q0 + lax.broadcasted_iota(jnp.int32, (n_blk, tq), 1)) // SEL_BLOCK
    dist = tb - jb
    forced = (jb == 0) | ((dist >= 0) & (dist < N_LOCAL_BLOCKS))
    score = jnp.where(jb > tb, -jnp.inf, jnp.where(forced, jnp.inf, imp_t))
    rank = jnp.zeros((n_blk, tq), jnp.int32)
    for jp in range(n_blk):
        rowv = score[jp:jp + 1, :]
        beats = (rowv > score) | ((rowv == score) & (jb > jp))
        rank = rank + jnp.where(beats, 1, 0)
    selbias_ref[...] = jnp.where(rank < min(SEL_TOP_N, n_blk), 0.0, NEG)

    bpc = tk // SEL_BLOCK
    vpc = tk // LANE

    def sel_body(c, carry):
        m_i, l_i, acc = carry
        start = pl.multiple_of(c * tk, tk)
        k_c = ks_ref[0, 0, pl.ds(start, tk), :]
        v_t = jnp.concatenate([vst_ref[0, 0, c * vpc + k] for k in range(vpc)], axis=1)
        rows = [jnp.broadcast_to(selbias_ref[pl.ds(c * bpc + j, 1), :], (SEL_BLOCK, tq)) for j in range(bpc)]
        kpos = start + lax.broadcasted_iota(jnp.int32, (tk, 1), 0)
        bias = jnp.where(kpos <= t_lane, jnp.concatenate(rows, axis=0), NEG)
        s = _dot(k_c, qr_t) + all_heads(bias)
        m_new = jnp.maximum(m_i, jnp.max(s, axis=0, keepdims=True))
        alpha = jnp.exp(m_i - m_new)
        p = jnp.exp(s - m_new)
        l_new = alpha * l_i + jnp.sum(p, axis=0, keepdims=True)
        acc_new = alpha * acc + _dot(v_t, p.astype(BF16))
        return m_new, l_new, acc_new

    R = GROUP * tq
    init = (jnp.full((1, R), NEG, F32), jnp.zeros((1, R), F32), jnp.zeros((HEAD_DIM, R), F32))
    _, l_s, acc_s = lax.fori_loop(0, (q0 + tq + tk - 1) // tk, sel_body, init)
    os_t = acc_s * (1.0 / jnp.maximum(l_s, TINY))

    wk = WINDOW + tq
    w0 = pl.multiple_of(jnp.maximum(q0 - WINDOW, 0), LANE)
    k_w = kw_ref[0, 0, pl.ds(w0, wk), :]
    v_t = jnp.concatenate([vwt_ref[0, 0, w0 // LANE + k] for k in range(wk // LANE)], axis=1)
    diff = t_lane - (w0 + lax.broadcasted_iota(jnp.int32, (wk, 1), 0))
    bias = jnp.where((diff >= 0) & (diff < WINDOW), 0.0, NEG)
    s = _dot(k_w, qr_t) + all_heads(bias)
    p = jnp.exp(s - jnp.max(s, axis=0, keepdims=True))
    ow_t = _dot(v_t, p.astype(BF16)) * (1.0 / jnp.sum(p, axis=0, keepdims=True))

    gt = gate_ref[0, 0]
    outs = []
    for hh, (a, b, c) in enumerate(zip(per_head(oc_t), per_head(os_t), per_head(ow_t))):
        r = hh * N_BRANCH
        outs.append(gt[r:r + 1] * a + gt[r + 1:r + 2] * b + gt[r + 2:r + 3] * c)
    o_ref[0] = jnp.concatenate(outs, axis=0).T.astype(o_ref.dtype)


def _cover_table(S):
    n_cmp = S // CMP_STRIDE
    n_blk = S // SEL_BLOCK
    cs = np.arange(n_cmp) * CMP_STRIDE
    ss = np.arange(n_blk) * SEL_BLOCK
    cover_t = ((cs[None, :] < ss[:, None] + SEL_BLOCK) & (cs[None, :] + CMP_LEN > ss[:, None]))
    cover_t[:, n_cmp - 1] = False
    return jnp.asarray(cover_t, BF16)


def _nsa_attention(qt, qrt, kc, vct, ks, vst, kw, vwt, gates, tq=128, tk=512):
    B, _, _, S = qt.shape
    tk = min(tk, S)
    assert tq == LANE and S % tk == 0 and S >= WINDOW + tq
    covt = _cover_table(S)
    n_cmp = kc.shape[2]
    n_blk = covt.shape[0]
    qspec = pl.BlockSpec((1, GROUP, HEAD_DIM, tq), lambda b, g, i: (b, g, 0, i))
    kspec = pl.BlockSpec((1, 1, S, HEAD_DIM), lambda b, g, i: (b, g, 0, 0))
    vspec = pl.BlockSpec((1, 1, S // LANE, HEAD_DIM, LANE), lambda b, g, i: (b, g, 0, 0, 0))
    return pl.pallas_call(
        functools.partial(_nsa_kernel, tq=tq, tk=tk),
        grid=(B, N_KV_HEADS, S // tq),
        in_specs=[qspec, qspec,
                  pl.BlockSpec((1, 1, n_cmp, HEAD_DIM), lambda b, g, i: (b, g, 0, 0)),
                  pl.BlockSpec((1, 1, HEAD_DIM, n_cmp), lambda b, g, i: (b, g, 0, 0)),
                  kspec, vspec, kspec, vspec,
                  pl.BlockSpec((1, 1, GATE_ROWS, tq), lambda b, g, i: (b, g, 0, i)),
                  pl.BlockSpec(covt.shape, lambda b, g, i: (0, 0))],
        out_specs=pl.BlockSpec((1, tq, GROUP * HEAD_DIM), lambda b, g, i: (b, i, g)),
        out_shape=jax.ShapeDtypeStruct((B, S, NSA_WIDTH), BF16),
        scratch_shapes=[pltpu.VMEM((n_blk, tq), F32)],
        compiler_params=_params(("parallel", "parallel", "parallel")),
        name="nsa_attention",
    )(qt, qrt, kc, vct, ks, vst, kw, vwt, gates, covt)


def _pool_kernel(u_ref, w_ref, scale_ref, o_ref, buf):
    i = pl.program_id(1)
    tm = u_ref.shape[1]

    @pl.when(i == 0)
    def _():
        buf[0:POOL_HALO, :] = jnp.zeros((POOL_HALO, POOL_WIDTH), F32)

    @pl.when(i > 0)
    def _():
        buf[0:POOL_HALO, :] = buf[tm:tm + POOL_HALO, :]

    buf[POOL_HALO:POOL_HALO + tm, :] = u_ref[0]
    t1 = i * tm + lax.broadcasted_iota(jnp.int32, (tm, 1), 0) + 1
    for gi, w in enumerate(POOL_WINDOWS):
        cols = slice(gi * POOL_GROUP, (gi + 1) * POOL_GROUP)
        cur = buf[POOL_HALO:POOL_HALO + tm, cols]
        tot = cur
        for k in range(1, w):
            tot = tot + buf[POOL_HALO - k:POOL_HALO - k + tm, cols]
        cnt = jnp.minimum(t1, w).astype(F32)
        d = tot / cnt - cur
        y = _dot(d.astype(BF16), w_ref[gi])
        o_ref[0, :, cols] = (y * scale_ref[:, cols]).astype(o_ref.dtype)


def _pool_mixer(u, w_pool, pool_scale, tm=512):
    B, S, _ = u.shape
    return pl.pallas_call(
        _pool_kernel,
        grid=(B, S // tm),
        in_specs=[pl.BlockSpec((1, tm, POOL_WIDTH), lambda b, i: (b, i, 0)),
                  pl.BlockSpec(w_pool.shape, lambda b, i: (0, 0, 0)),
                  pl.BlockSpec((1, POOL_WIDTH), lambda b, i: (0, 0))],
        out_specs=pl.BlockSpec((1, tm, POOL_WIDTH), lambda b, i: (b, i, 0)),
        out_shape=jax.ShapeDtypeStruct((B, S, POOL_WIDTH), BF16),
        scratch_shapes=[pltpu.VMEM((tm + POOL_HALO, POOL_WIDTH), F32)],
        compiler_params=_params(("parallel", "arbitrary")),
        name="pool_mixer",
    )(u, w_pool, pool_scale)


def _out_proj_kernel(x_ref, a_ref, p_ref, wa_ref, wp_ref, o_ref):
    o_ref[...] = x_ref[...] + _dot(a_ref[...], wa_ref[...]) + _dot(p_ref[...], wp_ref[...])


def _out_proj(x2, a2, p2, wa, wp, tm=512):
    T, D = x2.shape
    row = lambda n: pl.BlockSpec((tm, n), lambda i: (i, 0))
    full = lambda a: pl.BlockSpec(a.shape, lambda i: (0, 0))
    return pl.pallas_call(
        _out_proj_kernel,
        grid=(T // tm,),
        in_specs=[row(D), row(a2.shape[1]), row(p2.shape[1]), full(wa), full(wp)],
        out_specs=row(D),
        out_shape=jax.ShapeDtypeStruct((T, D), F32),
        compiler_params=_params(("parallel",)),
        name="mixer_out_proj",
    )(x2, a2, p2, wa, wp)


def _xkv_kernel(m_ref, g_ref, w_ref, k_ref, v_ref):
    h = _rms(m_ref[0], g_ref[...]).astype(BF16)
    kv = _dot(h, w_ref[...])
    d = k_ref.shape[2]
    k_ref[0] = kv[:, :d].astype(k_ref.dtype)
    v_ref[0] = kv[:, d:].astype(v_ref.dtype)


def _xattn_kv(mem, g, wkv):
    B, M, D = mem.shape
    ospec = pl.BlockSpec((1, M, D), lambda b: (b, 0, 0))
    oshape = jax.ShapeDtypeStruct((B, M, D), BF16)
    return pl.pallas_call(
        _xkv_kernel,
        grid=(B,),
        in_specs=[pl.BlockSpec((1, M, D), lambda b: (b, 0, 0)),
                  pl.BlockSpec((1, D), lambda b: (0, 0)),
                  pl.BlockSpec(wkv.shape, lambda b: (0, 0))],
        out_specs=(ospec, ospec),
        out_shape=(oshape, oshape),
        compiler_params=_params(("parallel",)),
        name="xattn_kv",
    )(mem, g, wkv)


def _xattn_kernel(x_ref, g_ref, wq_ref, k_ref, v_ref, wo_ref, o_ref):
    x = x_ref[0]
    h = _rms(x, g_ref[...]).astype(BF16)
    scale = X_HEAD_DIM ** -0.5
    q = (_dot(h, wq_ref[...]) * scale).astype(BF16)
    outs = []
    for hd in range(X_HEADS):
        sl = slice(hd * X_HEAD_DIM, (hd + 1) * X_HEAD_DIM)
        s = _dot_nt(q[:, sl], k_ref[0, :, sl])
        e = jnp.exp(s - jnp.max(s, axis=-1, keepdims=True))
        p = e / jnp.sum(e, axis=-1, keepdims=True)
        outs.append(_dot(p.astype(BF16), v_ref[0, :, sl]).astype(BF16))
    o = jnp.concatenate(outs, axis=-1)
    o_ref[0] = x + _dot(o, wo_ref[...])


def _xattn(x, g, wq, kx, vx, wo, tm=512):
    B, S, D = x.shape
    M = kx.shape[1]
    xspec = pl.BlockSpec((1, tm, D), lambda b, i: (b, i, 0))
    full = lambda a: pl.BlockSpec(a.shape, lambda b, i: (0, 0))
    mspec = pl.BlockSpec((1, M, D), lambda b, i: (b, 0, 0))
    return pl.pallas_call(
        _xattn_kernel,
        grid=(B, S // tm),
        in_specs=[xspec, full(g), full(wq), mspec, mspec, full(wo)],
        out_specs=xspec,
        out_shape=jax.ShapeDtypeStruct((B, S, D), F32),
        compiler_params=_params(("parallel", "parallel")),
        name="xattn",
    )(x, g, wq, kx, vx, wo)


def _silu(x):
    return x * jax.nn.sigmoid(x)


def _ffn_kernel(x_ref, g_ref, wg_ref, wu_ref, wd_ref, o_ref, h_ref, acc_ref):
    f = pl.program_id(1)

    @pl.when(f == 0)
    def _():
        h_ref[...] = _rms(x_ref[...], g_ref[...]).astype(BF16)
        acc_ref[...] = jnp.zeros_like(acc_ref)

    h = h_ref[...]
    act = _silu(_dot(h, wg_ref[...])) * _dot(h, wu_ref[...])
    acc_ref[...] += _dot(act.astype(BF16), wd_ref[...])

    @pl.when(f == pl.num_programs(1) - 1)
    def _():
        o_ref[...] = x_ref[...] + acc_ref[...]


def _ffn(x2, g, wg, wu, wd, tm=512, tf=1408):
    T, D = x2.shape
    F = wg.shape[1]
    return pl.pallas_call(
        _ffn_kernel,
        grid=(T // tm, F // tf),
        in_specs=[pl.BlockSpec((tm, D), lambda i, f: (i, 0)),
                  pl.BlockSpec((1, D), lambda i, f: (0, 0)),
                  pl.BlockSpec((D, tf), lambda i, f: (0, f)),
                  pl.BlockSpec((D, tf), lambda i, f: (0, f)),
                  pl.BlockSpec((tf, D), lambda i, f: (f, 0))],
        out_specs=pl.BlockSpec((tm, D), lambda i, f: (i, 0)),
        out_shape=jax.ShapeDtypeStruct((T, D), F32),
        scratch_shapes=[pltpu.VMEM((tm, D), BF16), pltpu.VMEM((tm, D), F32)],
        compiler_params=_params(("parallel", "arbitrary")),
        name="dense_swiglu",
    )(x2, g, wg, wu, wd)


def _moe_kernel(x_ref, g_ref, wr_ref, wg_ref, wu_ref, wd_ref, gf_ref, o_ref, h_ref, comb_ref, acc_ref):
    e = pl.program_id(1)
    f = pl.program_id(2)
    first = (e == 0) & (f == 0)
    last = (e == pl.num_programs(1) - 1) & (f == pl.num_programs(2) - 1)

    @pl.when(first)
    def _():
        hf = _rms(x_ref[...], g_ref[...])
        h_ref[...] = hf.astype(BF16)
        acc_ref[...] = jnp.zeros_like(acc_ref)
        logits = jnp.dot(hf, wr_ref[...], preferred_element_type=F32, precision=lax.Precision.HIGHEST)
        lane = lax.broadcasted_iota(jnp.int32, logits.shape, 1)
        logits = jnp.where(lane < N_EXPERTS, logits, -jnp.inf)
        v1 = jnp.max(logits, axis=-1, keepdims=True)
        i1 = jnp.min(jnp.where(logits == v1, lane, LANE), axis=-1, keepdims=True)
        rest = jnp.where(lane == i1, -jnp.inf, logits)
        v2 = jnp.max(rest, axis=-1, keepdims=True)
        i2 = jnp.min(jnp.where(rest == v2, lane, LANE), axis=-1, keepdims=True)
        e2 = jnp.exp(v2 - v1)
        den = 1.0 + e2
        comb_ref[...] = jnp.where(lane == i1, 1.0 / den, jnp.where(lane == i2, e2 / den, 0.0))

    lane = lax.broadcasted_iota(jnp.int32, comb_ref.shape, 1)
    ce = jnp.sum(jnp.where(lane == e, comb_ref[...], 0.0), axis=-1, keepdims=True)
    h = h_ref[...]
    act = _silu(_dot(h, wg_ref[0])) * _dot(h, wu_ref[0])
    acc_ref[...] += ce * _dot(act.astype(BF16), wd_ref[0])

    @pl.when(last)
    def _():
        o_ref[...] = _rms(x_ref[...] + acc_ref[...], gf_ref[...])


def _moe(x2, g, w_router, wg, wu, wd, g_final, tm=1024, tf=512):
    T, D = x2.shape
    E, _, F = wg.shape
    return pl.pallas_call(
        _moe_kernel,
        grid=(T // tm, E, F // tf),
        in_specs=[pl.BlockSpec((tm, D), lambda i, e, f: (i, 0)),
                  pl.BlockSpec((1, D), lambda i, e, f: (0, 0)),
                  pl.BlockSpec((D, LANE), lambda i, e, f: (0, 0)),
                  pl.BlockSpec((1, D, tf), lambda i, e, f: (e, 0, f)),
                  pl.BlockSpec((1, D, tf), lambda i, e, f: (e, 0, f)),
                  pl.BlockSpec((1, tf, D), lambda i, e, f: (e, f, 0)),
                  pl.BlockSpec((1, D), lambda i, e, f: (0, 0))],
        out_specs=pl.BlockSpec((tm, D), lambda i, e, f: (i, 0)),
        out_shape=jax.ShapeDtypeStruct((T, D), F32),
        scratch_shapes=[pltpu.VMEM((tm, D), BF16), pltpu.VMEM((tm, LANE), F32), pltpu.VMEM((tm, D), F32)],
        compiler_params=_params(("parallel", "arbitrary", "arbitrary")),
        name="moe_swiglu_final_norm",
    )(x2, g, w_router, wg, wu, wd, g_final)


def _rope_tables(S):
    inv = ROPE_THETA ** (-jnp.arange(ROPE_HALF, dtype=F32) / ROPE_HALF)
    ang = jnp.arange(S, dtype=F32)[:, None] * inv[None, :]
    cos, sin = jnp.cos(ang), jnp.sin(ang)
    pad = HEAD_DIM - ROPE_DIM
    cos_h = jnp.concatenate([cos, cos, jnp.ones((S, pad), F32)], axis=-1)
    sin_h = jnp.concatenate([-sin, sin, jnp.zeros((S, pad), F32)], axis=-1)
    reps = LANE // HEAD_DIM
    return jnp.tile(cos_h, (1, reps)), jnp.tile(sin_h, (1, reps))


def _pack_w_in(w):
    per_group = GROUP * N_BRANCH
    gates = w[:, GATE_OFF:GATE_OFF + N_KV_HEADS * per_group].reshape(-1, N_KV_HEADS, per_group)
    gates = jnp.pad(gates, ((0, 0), (0, LANE // GATE_ROWS - N_KV_HEADS), (0, GATE_ROWS - per_group)))
    gates = gates.reshape(-1, LANE)
    return jnp.concatenate([w[:, :GATE_OFF], gates, w[:, GATE_OFF + N_KV_HEADS * per_group:]], axis=1).astype(BF16)


def _mixer_layer(x, norm_g, w_in, pe_k, w1_k, b1_k, w2_k, pe_v, w1_v, b1_v, w2_v,
                 w_pool, pool_scale, w_out, cos_t, sin_t):
    B, S, D = x.shape
    qt, qrt, kcr, vcr, ks, vst, kw, vwt, gates, u = _mixer_proj(
        x, norm_g.reshape(1, D), _pack_w_in(w_in), cos_t, sin_t)
    half = CMP_LEN // 2
    cw = half * HEAD_DIM
    pad_w2 = lambda w2: jnp.pad(w2, ((0, 0), (0, LANE - HEAD_DIM))).astype(BF16)
    kc, vct = _compress(
        kcr, vcr,
        pe_k.reshape(2, cw), w1_k.reshape(2, cw, CMP_HIDDEN).astype(BF16), b1_k.reshape(1, -1), pad_w2(w2_k),
        pe_v.reshape(2, cw), w1_v.reshape(2, cw, CMP_HIDDEN).astype(BF16), b1_v.reshape(1, -1), pad_w2(w2_v))
    a = _nsa_attention(qt, qrt, kc, vct, ks, vst, kw, vwt, gates)
    p = _pool_mixer(u, w_pool.astype(BF16), pool_scale.reshape(1, -1))
    w_out = w_out.astype(BF16)
    out = _out_proj(x.reshape(B * S, D), a.reshape(B * S, NSA_WIDTH), p.reshape(B * S, POOL_WIDTH),
                    w_out[:NSA_WIDTH], w_out[NSA_WIDTH:])
    return out.reshape(B, S, D)


def _moe_layer(x2, g_ffn, router, wg, wu, wd, g_final):
    D = x2.shape[1]
    w_router = jnp.pad(router, ((0, 0), (0, LANE - N_EXPERTS)))
    return _moe(x2, g_ffn.reshape(1, D), w_router, wg.astype(BF16), wu.astype(BF16), wd.astype(BF16),
                g_final.reshape(1, D))


def kernel(x, mem, norm_mix, w_in, cmp_pe_k, cmp_w1_k, cmp_b1_k, cmp_w2_k, cmp_pe_v, cmp_w1_v, cmp_b1_v, cmp_w2_v, w_pool, pool_scale, w_out, norm_x, norm_mem, wq_x, wk_x, wv_x, wo_x, norm_ffn, ffn_wg, ffn_wu, ffn_wd, moe_router, moe_wg, moe_wu, moe_wd, norm_final):
    B, S, D = x.shape
    depth = norm_mix.shape[0]
    assert depth == 2, "the final RMSNorm is fused into the expert layer, which must be the last one"
    cos_t, sin_t = _rope_tables(S)
    for layer in range(depth):
        x = _mixer_layer(x, norm_mix[layer], w_in[layer],
                         cmp_pe_k[layer], cmp_w1_k[layer], cmp_b1_k[layer], cmp_w2_k[layer],
                         cmp_pe_v[layer], cmp_w1_v[layer], cmp_b1_v[layer], cmp_w2_v[layer],
                         w_pool[layer], pool_scale[layer], w_out[layer], cos_t, sin_t)
        wkv = jnp.concatenate([wk_x[layer], wv_x[layer]], axis=1).astype(BF16)
        kx, vx = _xattn_kv(mem, norm_mem[layer].reshape(1, D), wkv)
        x = _xattn(x, norm_x[layer].reshape(1, D), wq_x[layer].astype(BF16), kx, vx, wo_x[layer].astype(BF16))
        x2 = x.reshape(B * S, D)
        j = layer // 2
        if layer % 2 == 0:
            x2 = _ffn(x2, norm_ffn[layer].reshape(1, D), ffn_wg[j].astype(BF16), ffn_wu[j].astype(BF16),
                      ffn_wd[j].astype(BF16))
        else:
            x2 = _moe_layer(x2, norm_ffn[layer], moe_router[j], moe_wg[j], moe_wu[j], moe_wd[j], norm_final)
        x = x2.reshape(B, S, D)
    return x
```

```python
import functools
import math

import numpy as np
import jax
import jax.numpy as jnp
from jax import lax
from jax.experimental import pallas as pl
from jax.experimental.pallas import tpu as pltpu

F32 = jnp.float32
BF16 = jnp.bfloat16

D_MODEL = 1024
HEAD_DIM = 64
N_HEADS = 8
N_KV_HEADS = 2
GROUP = N_HEADS // N_KV_HEADS
NSA_WIDTH = N_HEADS * HEAD_DIM
KV_WIDTH = N_KV_HEADS * HEAD_DIM
N_BRANCH = 3
GATE_ROWS = 16
POOL_WIDTH = 512
POOL_WINDOWS = (2, 4, 8, 16)
POOL_GROUP = 128
POOL_HALO = 16
ROPE_DIM = 16
ROPE_HALF = 8
ROPE_THETA = 500000.0
CMP_LEN = 32
CMP_STRIDE = 16
CMP_HIDDEN = 256
SEL_BLOCK = 64
SEL_TOP_N = 16
N_LOCAL_BLOCKS = 2
WINDOW = 512
X_HEADS = 4
X_HEAD_DIM = 256
N_EXPERTS = 8
EPS = 1e-6

LANE = 128
IN_PACKED = NSA_WIDTH + 6 * KV_WIDTH + LANE + POOL_WIDTH
GATE_OFF = NSA_WIDTH + 6 * KV_WIDTH
POOL_OFF = GATE_OFF + LANE

NEG = -1e30
TINY = float(np.finfo(np.float32).tiny)
VMEM_LIMIT = 56 * 1024 * 1024


def _dot(a, b):
    return jnp.dot(a, b, preferred_element_type=F32)


def _dot_nt(a, b):
    return lax.dot_general(a, b, (((1,), (1,)), ((), ())), preferred_element_type=F32)


def _rms(x, g):
    y = x * lax.rsqrt(jnp.mean(x * x, axis=-1, keepdims=True) + EPS)
    return y * g


def _params(sem, limit=VMEM_LIMIT):
    return pltpu.CompilerParams(dimension_semantics=sem, vmem_limit_bytes=limit)


def _mixer_proj_kernel(x_ref, g_ref, w_ref, cos_ref, sin_ref,
                       qt_ref, qrt_ref, kcr_ref, vcr_ref, ks_ref, vst_ref, kw_ref, vwt_ref,
                       gate_ref, u_ref):
    h = _rms(x_ref[0], g_ref[...]).astype(BF16)
    z = _dot(h, w_ref[...])
    tm = z.shape[0]
    cos = cos_ref[...]
    sin = sin_ref[...]
    lane = lax.broadcasted_iota(jnp.int32, (tm, LANE), 1)
    first = (lane & (HEAD_DIM - 1)) < ROPE_HALF
    scale = HEAD_DIM ** -0.5

    def rope(xs):
        partner = jnp.where(first, pltpu.roll(xs, LANE - ROPE_HALF, 1), pltpu.roll(xs, ROPE_HALF, 1))
        return xs * cos + partner * sin

    for s in range(NSA_WIDTH // LANE):
        xs = z[:, s * LANE:(s + 1) * LANE]
        for src, ref in ((xs, qt_ref), (rope(xs), qrt_ref)):
            t = (src * scale).T.astype(BF16)
            ref[0, 2 * s] = t[:HEAD_DIM]
            ref[0, 2 * s + 1] = t[HEAD_DIM:]

    def kv_slab(i):
        return z[:, NSA_WIDTH + i * KV_WIDTH:NSA_WIDTH + (i + 1) * KV_WIDTH]

    for slab, ref in ((kv_slab(0), kcr_ref), (kv_slab(1), vcr_ref),
                      (rope(kv_slab(2)), ks_ref), (rope(kv_slab(4)), kw_ref)):
        for gg in range(N_KV_HEADS):
            ref[0, gg] = slab[:, gg * HEAD_DIM:(gg + 1) * HEAD_DIM].astype(ref.dtype)

    for slab, ref in ((kv_slab(3), vst_ref), (kv_slab(5), vwt_ref)):
        t = slab.T.astype(BF16)
        for gg in range(N_KV_HEADS):
            for k in range(tm // LANE):
                ref[0, gg, k] = t[gg * HEAD_DIM:(gg + 1) * HEAD_DIM, k * LANE:(k + 1) * LANE]

    sig_t = jax.nn.sigmoid(z[:, GATE_OFF:GATE_OFF + LANE]).T
    for gg in range(N_KV_HEADS):
        gate_ref[0, gg] = sig_t[gg * GATE_ROWS:(gg + 1) * GATE_ROWS]
    u_ref[0] = z[:, POOL_OFF:POOL_OFF + POOL_WIDTH]


def _mixer_proj(x, g, w_packed, cos_t, sin_t, tm=512):
    B, S, D = x.shape
    G = N_KV_HEADS
    sd = jax.ShapeDtypeStruct
    out_shape = (sd((B, N_HEADS, HEAD_DIM, S), BF16), sd((B, N_HEADS, HEAD_DIM, S), BF16),
                 sd((B, G, S, HEAD_DIM), F32), sd((B, G, S, HEAD_DIM), F32),
                 sd((B, G, S, HEAD_DIM), BF16), sd((B, G, S // LANE, HEAD_DIM, LANE), BF16),
                 sd((B, G, S, HEAD_DIM), BF16), sd((B, G, S // LANE, HEAD_DIM, LANE), BF16),
                 sd((B, G, GATE_ROWS, S), F32),
                 sd((B, S, POOL_WIDTH), F32))
    qspec = pl.BlockSpec((1, N_HEADS, HEAD_DIM, tm), lambda b, i: (b, 0, 0, i))
    kspec = pl.BlockSpec((1, G, tm, HEAD_DIM), lambda b, i: (b, 0, i, 0))
    vspec = pl.BlockSpec((1, G, tm // LANE, HEAD_DIM, LANE), lambda b, i: (b, 0, i, 0, 0))
    out_specs = (qspec, qspec, kspec, kspec, kspec, vspec, kspec, vspec,
                 pl.BlockSpec((1, G, GATE_ROWS, tm), lambda b, i: (b, 0, 0, i)),
                 pl.BlockSpec((1, tm, POOL_WIDTH), lambda b, i: (b, i, 0)))
    return pl.pallas_call(
        _mixer_proj_kernel,
        grid=(B, S // tm),
        in_specs=[pl.BlockSpec((1, tm, D), lambda b, i: (b, i, 0)),
                  pl.BlockSpec((1, D), lambda b, i: (0, 0)),
                  pl.BlockSpec((D, IN_PACKED), lambda b, i: (0, 0)),
                  pl.BlockSpec((tm, LANE), lambda b, i: (i, 0)),
                  pl.BlockSpec((tm, LANE), lambda b, i: (i, 0))],
        out_specs=out_specs,
        out_shape=out_shape,
        compiler_params=_params(("parallel", "parallel")),
        name="mixer_proj",
    )(x, g, w_packed, cos_t, sin_t)


def _gelu_tanh(x):
    return 0.5 * x * (1.0 + jnp.tanh(math.sqrt(2.0 / math.pi) * (x + 0.044715 * (x * x * x))))


def _compress_kernel(kc_ref, vc_ref, pek_ref, w1k_ref, b1k_ref, w2k_ref,
                     pev_ref, w1v_ref, b1v_ref, w2v_ref, ko_ref, vto_ref):
    def comp(x_ref, pe_ref, w1_ref, b1_ref, w2_ref):
        x = x_ref[0, 0]
        n = x.shape[0]
        a = _dot((x + pe_ref[0:1, :]).astype(BF16), w1_ref[0])
        b = _dot((x + pe_ref[1:2, :]).astype(BF16), w1_ref[1])
        pre = a + pltpu.roll(b, n - 1, 0) + b1_ref[...]
        return _dot(_gelu_tanh(pre).astype(BF16), w2_ref[...])

    ko_ref[0, 0] = comp(kc_ref, pek_ref, w1k_ref, b1k_ref, w2k_ref)[:, :HEAD_DIM].astype(ko_ref.dtype)
    vto_ref[0, 0] = comp(vc_ref, pev_ref, w1v_ref, b1v_ref, w2v_ref).T[:HEAD_DIM].astype(vto_ref.dtype)


def _compress(kcr, vcr, pek, w1k, b1k, w2k, pev, w1v, b1v, w2v):
    B, G, S, _ = kcr.shape
    n = S // CMP_STRIDE
    cw = CMP_STRIDE * HEAD_DIM
    kc = kcr.reshape(B, G, n, cw)
    vc = vcr.reshape(B, G, n, cw)
    xspec = pl.BlockSpec((1, 1, n, cw), lambda b, g: (b, g, 0, 0))
    full = lambda a: pl.BlockSpec(a.shape, lambda b, g: (0,) * a.ndim)
    return pl.pallas_call(
        _compress_kernel,
        grid=(B, G),
        in_specs=[xspec, xspec, full(pek), full(w1k), full(b1k), full(w2k),
                  full(pev), full(w1v), full(b1v), full(w2v)],
        out_specs=(pl.BlockSpec((1, 1, n, HEAD_DIM), lambda b, g: (b, g, 0, 0)),
                   pl.BlockSpec((1, 1, HEAD_DIM, n), lambda b, g: (b, g, 0, 0))),
        out_shape=(jax.ShapeDtypeStruct((B, G, n, HEAD_DIM), BF16),
                   jax.ShapeDtypeStruct((B, G, HEAD_DIM, n), BF16)),
        compiler_params=_params(("parallel", "parallel")),
        name="compress_kv",
    )(kc, vc, pek, w1k, b1k, w2k, pev, w1v, b1v, w2v)


def _nsa_kernel(qt_ref, qrt_ref, kc_ref, vct_ref, ks_ref, vst_ref, kw_ref, vwt_ref, gate_ref,
                covt_ref, o_ref, selbias_ref, *, tq, tk):
    qt = pl.program_id(2)
    q0 = qt * tq
    n_cmp = kc_ref.shape[2]
    n_blk = covt_ref.shape[0]
    heads = lambda ref: jnp.concatenate([ref[0, hh] for hh in range(GROUP)], axis=1)
    q_t = heads(qt_ref)
    qr_t = heads(qrt_ref)
    t_lane = q0 + lax.broadcasted_iota(jnp.int32, (1, tq), 1)
    per_head = lambda a: [a[:, hh * tq:(hh + 1) * tq] for hh in range(GROUP)]
    all_heads = lambda a: jnp.concatenate([a] * GROUP, axis=1)

    sc = _dot(kc_ref[0, 0], q_t)
    cmp_end = lax.broadcasted_iota(jnp.int32, (n_cmp, 1), 0) * CMP_STRIDE + (CMP_LEN - 1)
    valid = cmp_end <= t_lane
    pcs = []
    for s_h in per_head(sc):
        s_h = jnp.where(valid, s_h, NEG)
        e = jnp.where(valid, jnp.exp(s_h - jnp.max(s_h, axis=0, keepdims=True)), 0.0)
        pcs.append(e * (1.0 / jnp.maximum(jnp.sum(e, axis=0, keepdims=True), TINY)))
    oc_t = _dot(vct_ref[0, 0], jnp.concatenate(pcs, axis=1).astype(BF16))

    psum = pcs[0] + pcs[1] + pcs[2] + pcs[3]
    p_hi = psum.astype(BF16)
    r1 = psum - p_hi.astype(F32)
    p_mid = r1.astype(BF16)
    p_lo = (r1 - p_mid.astype(F32)).astype(BF16)
    covt = covt_ref[...]
    imp_t = _dot(covt, p_hi) + _dot(covt, p_mid) + _dot(covt, p_lo)

    jb = lax.broadcasted_iota(jnp.int32, (n_blk, tq), 0)
    tb = (q0 + lax.broadcasted_iota(jnp.int32, (n_blk, tq), 1)) // SEL_BLOCK
    dist = tb - jb
    forced = (jb == 0) | ((dist >= 0) & (dist < N_LOCAL_BLOCKS))
    score = jnp.where(jb > tb, -jnp.inf, jnp.where(forced, jnp.inf, imp_t))
    rank = jnp.zeros((n_blk, tq), jnp.int32)
    for jp in range(n_blk):
        rowv = score[jp:jp + 1, :]
        beats = (rowv > score) | ((rowv == score) & (jb > jp))
        rank = rank + jnp.where(beats, 1, 0)
    selbias_ref[...] = jnp.where(rank < min(SEL_TOP_N, n_blk), 0.0, NEG)

    bpc = tk // SEL_BLOCK
    vpc = tk // LANE

    def sel_body(c, carry):
        m_i, l_i, acc = carry
        start = pl.multiple_of(c * tk, tk)
        k_c = ks_ref[0, 0, pl.ds(start, tk), :]
        v_t = jnp.concatenate([vst_ref[0, 0, c * vpc + k] for k in range(vpc)], axis=1)
        rows = [jnp.broadcast_to(selbias_ref[pl.ds(c * bpc + j, 1), :], (SEL_BLOCK, tq)) for j in range(bpc)]
        kpos = start + lax.broadcasted_iota(jnp.int32, (tk, 1), 0)
        bias = jnp.where(kpos <= t_lane, jnp.concatenate(rows, axis=0), NEG)
        s = _dot(k_c, qr_t) + all_heads(bias)
        m_new = jnp.maximum(m_i, jnp.max(s, axis=0, keepdims=True))
        alpha = jnp.exp(m_i - m_new)
        p = jnp.exp(s - m_new)
        l_new = alpha * l_i + jnp.sum(p, axis=0, keepdims=True)
        acc_new = alpha * acc + _dot(v_t, p.astype(BF16))
        return m_new, l_new, acc_new

    R = GROUP * tq
    init = (jnp.full((1, R), NEG, F32), jnp.zeros((1, R), F32), jnp.zeros((HEAD_DIM, R), F32))
    _, l_s, acc_s = lax.fori_loop(0, (q0 + tq + tk - 1) // tk, sel_body, init)
    os_t = acc_s * (1.0 / jnp.maximum(l_s, TINY))

    wk = WINDOW + tq
    w0 = pl.multiple_of(jnp.maximum(q0 - WINDOW, 0), LANE)
    k_w = kw_ref[0, 0, pl.ds(w0, wk), :]
    v_t = jnp.concatenate([vwt_ref[0, 0, w0 // LANE + k] for k in range(wk // LANE)], axis=1)
    diff = t_lane - (w0 + lax.broadcasted_iota(jnp.int32, (wk, 1), 0))
    bias = jnp.where((diff >= 0) & (diff < WINDOW), 0.0, NEG)
    s = _dot(k_w, qr_t) + all_heads(bias)
    p = jnp.exp(s - jnp.max(s, axis=0, keepdims=True))
    ow_t = _dot(v_t, p.astype(BF16)) * (1.0 / jnp.sum(p, axis=0, keepdims=True))

    gt = gate_ref[0, 0]
    outs = []
    for hh, (a, b, c) in enumerate(zip(per_head(oc_t), per_head(os_t), per_head(ow_t))):
        r = hh * N_BRANCH
        outs.append(gt[r:r + 1] * a + gt[r + 1:r + 2] * b + gt[r + 2:r + 3] * c)
    o_ref[0] = jnp.concatenate(outs, axis=0).T.astype(o_ref.dtype)


def _cover_table(S):
    n_cmp = S // CMP_STRIDE
    n_blk = S // SEL_BLOCK
    cs = np.arange(n_cmp) * CMP_STRIDE
    ss = np.arange(n_blk) * SEL_BLOCK
    cover_t = ((cs[None, :] < ss[:, None] + SEL_BLOCK) & (cs[None, :] + CMP_LEN > ss[:, None]))
    cover_t[:, n_cmp - 1] = False
    return jnp.asarray(cover_t, BF16)


def _nsa_attention(qt, qrt, kc, vct, ks, vst, kw, vwt, gates, tq=128, tk=512):
    B, _, _, S = qt.shape
    tk = min(tk, S)
    assert tq == LANE and S % tk == 0 and S >= WINDOW + tq
    covt = _cover_table(S)
    n_cmp = kc.shape[2]
    n_blk = covt.shape[0]
    qspec = pl.BlockSpec((1, GROUP, HEAD_DIM, tq), lambda b, g, i: (b, g, 0, i))
    kspec = pl.BlockSpec((1, 1, S, HEAD_DIM), lambda b, g, i: (b, g, 0, 0))
    vspec = pl.BlockSpec((1, 1, S // LANE, HEAD_DIM, LANE), lambda b, g, i: (b, g, 0, 0, 0))
    return pl.pallas_call(
        functools.partial(_nsa_kernel, tq=tq, tk=tk),
        grid=(B, N_KV_HEADS, S // tq),
        in_specs=[qspec, qspec,
                  pl.BlockSpec((1, 1, n_cmp, HEAD_DIM), lambda b, g, i: (b, g, 0, 0)),
                  pl.BlockSpec((1, 1, HEAD_DIM, n_cmp), lambda b, g, i: (b, g, 0, 0)),
                  kspec, vspec, kspec, vspec,
                  pl.BlockSpec((1, 1, GATE_ROWS, tq), lambda b, g, i: (b, g, 0, i)),
                  pl.BlockSpec(covt.shape, lambda b, g, i: (0, 0))],
        out_specs=pl.BlockSpec((1, tq, GROUP * HEAD_DIM), lambda b, g, i: (b, i, g)),
        out_shape=jax.ShapeDtypeStruct((B, S, NSA_WIDTH), BF16),
        scratch_shapes=[pltpu.VMEM((n_blk, tq), F32)],
        compiler_params=_params(("parallel", "parallel", "parallel")),
        name="nsa_attention",
    )(qt, qrt, kc, vct, ks, vst, kw, vwt, gates, covt)


def _pool_kernel(u_ref, w_ref, scale_ref, o_ref, buf):
    i = pl.program_id(1)
    tm = u_ref.shape[1]

    @pl.when(i == 0)
    def _():
        buf[0:POOL_HALO, :] = jnp.zeros((POOL_HALO, POOL_WIDTH), F32)

    @pl.when(i > 0)
    def _():
        buf[0:POOL_HALO, :] = buf[tm:tm + POOL_HALO, :]

    buf[POOL_HALO:POOL_HALO + tm, :] = u_ref[0]
    t1 = i * tm + lax.broadcasted_iota(jnp.int32, (tm, 1), 0) + 1
    for gi, w in enumerate(POOL_WINDOWS):
        cols = slice(gi * POOL_GROUP, (gi + 1) * POOL_GROUP)
        cur = buf[POOL_HALO:POOL_HALO + tm, cols]
        tot = cur
        for k in range(1, w):
            tot = tot + buf[POOL_HALO - k:POOL_HALO - k + tm, cols]
        cnt = jnp.minimum(t1, w).astype(F32)
        d = tot / cnt - cur
        y = _dot(d.astype(BF16), w_ref[gi])
        o_ref[0, :, cols] = (y * scale_ref[:, cols]).astype(o_ref.dtype)


def _pool_mixer(u, w_pool, pool_scale, tm=512):
    B, S, _ = u.shape
    return pl.pallas_call(
        _pool_kernel,
        grid=(B, S // tm),
        in_specs=[pl.BlockSpec((1, tm, POOL_WIDTH), lambda b, i: (b, i, 0)),
                  pl.BlockSpec(w_pool.shape, lambda b, i: (0, 0, 0)),
                  pl.BlockSpec((1, POOL_WIDTH), lambda b, i: (0, 0))],
        out_specs=pl.BlockSpec((1, tm, POOL_WIDTH), lambda b, i: (b, i, 0)),
        out_shape=jax.ShapeDtypeStruct((B, S, POOL_WIDTH), BF16),
        scratch_shapes=[pltpu.VMEM((tm + POOL_HALO, POOL_WIDTH), F32)],
        compiler_params=_params(("parallel", "arbitrary")),
        name="pool_mixer",
    )(u, w_pool, pool_scale)


def _out_proj_kernel(x_ref, a_ref, p_ref, wa_ref, wp_ref, o_ref):
    o_ref[...] = x_ref[...] + _dot(a_ref[...], wa_ref[...]) + _dot(p_ref[...], wp_ref[...])


def _out_proj(x2, a2, p2, wa, wp, tm=512):
    T, D = x2.shape
    row = lambda n: pl.BlockSpec((tm, n), lambda i: (i, 0))
    full = lambda a: pl.BlockSpec(a.shape, lambda i: (0, 0))
    return pl.pallas_call(
        _out_proj_kernel,
        grid=(T // tm,),
        in_specs=[row(D), row(a2.shape[1]), row(p2.shape[1]), full(wa), full(wp)],
        out_specs=row(D),
        out_shape=jax.ShapeDtypeStruct((T, D), F32),
        compiler_params=_params(("parallel",)),
        name="mixer_out_proj",
    )(x2, a2, p2, wa, wp)


def _xkv_kernel(m_ref, g_ref, w_ref, k_ref, v_ref):
    h = _rms(m_ref[0], g_ref[...]).astype(BF16)
    kv = _dot(h, w_ref[...])
    d = k_ref.shape[2]
    k_ref[0] = kv[:, :d].astype(k_ref.dtype)
    v_ref[0] = kv[:, d:].astype(v_ref.dtype)


def _xattn_kv(mem, g, wkv):
    B, M, D = mem.shape
    ospec = pl.BlockSpec((1, M, D), lambda b: (b, 0, 0))
    oshape = jax.ShapeDtypeStruct((B, M, D), BF16)
    return pl.pallas_call(
        _xkv_kernel,
        grid=(B,),
        in_specs=[pl.BlockSpec((1, M, D), lambda b: (b, 0, 0)),
                  pl.BlockSpec((1, D), lambda b: (0, 0)),
                  pl.BlockSpec(wkv.shape, lambda b: (0, 0))],
        out_specs=(ospec, ospec),
        out_shape=(oshape, oshape),
        compiler_params=_params(("parallel",)),
        name="xattn_kv",
    )(mem, g, wkv)


def _xattn_kernel(x_ref, g_ref, wq_ref, k_ref, v_ref, wo_ref, o_ref):
    x = x_ref[0]
    h = _rms(x, g_ref[...]).astype(BF16)
    scale = X_HEAD_DIM ** -0.5
    q = (_dot(h, wq_ref[...]) * scale).astype(BF16)
    outs = []
    for hd in range(X_HEADS):
        sl = slice(hd * X_HEAD_DIM, (hd + 1) * X_HEAD_DIM)
        s = _dot_nt(q[:, sl], k_ref[0, :, sl])
        e = jnp.exp(s - jnp.max(s, axis=-1, keepdims=True))
        p = e / jnp.sum(e, axis=-1, keepdims=True)
        outs.append(_dot(p.astype(BF16), v_ref[0, :, sl]).astype(BF16))
    o = jnp.concatenate(outs, axis=-1)
    o_ref[0] = x + _dot(o, wo_ref[...])


def _xattn(x, g, wq, kx, vx, wo, tm=512):
    B, S, D = x.shape
    M = kx.shape[1]
    xspec = pl.BlockSpec((1, tm, D), lambda b, i: (b, i, 0))
    full = lambda a: pl.BlockSpec(a.shape, lambda b, i: (0, 0))
    mspec = pl.BlockSpec((1, M, D), lambda b, i: (b, 0, 0))
    return pl.pallas_call(
        _xattn_kernel,
        grid=(B, S // tm),
        in_specs=[xspec, full(g), full(wq), mspec, mspec, full(wo)],
        out_specs=xspec,
        out_shape=jax.ShapeDtypeStruct((B, S, D), F32),
        compiler_params=_params(("parallel", "parallel")),
        name="xattn",
    )(x, g, wq, kx, vx, wo)


def _silu(x):
    return x * jax.nn.sigmoid(x)


def _ffn_kernel(x_ref, g_ref, wg_ref, wu_ref, wd_ref, o_ref, h_ref, acc_ref):
    f = pl.program_id(1)

    @pl.when(f == 0)
    def _():
        h_ref[...] = _rms(x_ref[...], g_ref[...]).astype(BF16)
        acc_ref[...] = jnp.zeros_like(acc_ref)

    h = h_ref[...]
    act = _silu(_dot(h, wg_ref[...])) * _dot(h, wu_ref[...])
    acc_ref[...] += _dot(act.astype(BF16), wd_ref[...])

    @pl.when(f == pl.num_programs(1) - 1)
    def _():
        o_ref[...] = x_ref[...] + acc_ref[...]


def _ffn(x2, g, wg, wu, wd, tm=512, tf=1408):
    T, D = x2.shape
    F = wg.shape[1]
    return pl.pallas_call(
        _ffn_kernel,
        grid=(T // tm, F // tf),
        in_specs=[pl.BlockSpec((tm, D), lambda i, f: (i, 0)),
                  pl.BlockSpec((1, D), lambda i, f: (0, 0)),
                  pl.BlockSpec((D, tf), lambda i, f: (0, f)),
                  pl.BlockSpec((D, tf), lambda i, f: (0, f)),
                  pl.BlockSpec((tf, D), lambda i, f: (f, 0))],
        out_specs=pl.BlockSpec((tm, D), lambda i, f: (i, 0)),
        out_shape=jax.ShapeDtypeStruct((T, D), F32),
        scratch_shapes=[pltpu.VMEM((tm, D), BF16), pltpu.VMEM((tm, D), F32)],
        compiler_params=_params(("parallel", "arbitrary")),
        name="dense_swiglu",
    )(x2, g, wg, wu, wd)


MOE_CHUNK = 256
MOE_ROW_TILE = 512


def _router_kernel(x_ref, g_ref, wr_ref, tri_ref, h_ref, route_ref, gate_ref, before_ref, total_ref, run_ref):
    c = pl.program_id(0)

    @pl.when(c == 0)
    def _():
        run_ref[...] = jnp.zeros_like(run_ref)

    hf = _rms(x_ref[...], g_ref[...])
    h_ref[...] = hf.astype(BF16)
    logits = jnp.dot(hf, wr_ref[...], preferred_element_type=F32, precision=lax.Precision.HIGHEST)
    lane = lax.broadcasted_iota(jnp.int32, logits.shape, 1)
    logits = jnp.where(lane < N_EXPERTS, logits, -jnp.inf)
    v1 = jnp.max(logits, axis=-1, keepdims=True)
    i1 = jnp.min(jnp.where(logits == v1, lane, LANE), axis=-1, keepdims=True)
    rest = jnp.where(lane == i1, -jnp.inf, logits)
    v2 = jnp.max(rest, axis=-1, keepdims=True)
    i2 = jnp.min(jnp.where(rest == v2, lane, LANE), axis=-1, keepdims=True)
    e2 = jnp.exp(v2 - v1)
    den = 1.0 + e2
    gate_ref[...] = jnp.where(lane == 0, 1.0 / den, jnp.where(lane == 1, e2 / den, 0.0))

    onehot = jnp.where((lane == i1) | (lane == i2), 1.0, 0.0)
    run = run_ref[...]
    rank = run + _dot(tri_ref[...], onehot.astype(BF16))
    r1 = jnp.sum(jnp.where(lane == i1, rank, 0.0), axis=-1, keepdims=True).astype(jnp.int32)
    r2 = jnp.sum(jnp.where(lane == i2, rank, 0.0), axis=-1, keepdims=True).astype(jnp.int32)
    route_ref[...] = jnp.where(lane == 0, i1, jnp.where(lane == 1, i2,
                               jnp.where(lane == 2, r1, jnp.where(lane == 3, r2, 0))))
    before_ref[0] = run
    run = run + jnp.sum(onehot, axis=0, keepdims=True)
    run_ref[...] = run
    total_ref[...] = run


def _router(x2, g, w_router):
    T, D = x2.shape
    C = T // MOE_CHUNK
    tri = jnp.asarray(np.tril(np.ones((MOE_CHUNK, MOE_CHUNK), np.float32), -1), BF16)
    row = lambda n: pl.BlockSpec((MOE_CHUNK, n), lambda c: (c, 0))
    full = lambda a: pl.BlockSpec(a.shape, lambda c: (0, 0))
    return pl.pallas_call(
        _router_kernel,
        grid=(C,),
        in_specs=[row(D), full(g), full(w_router), full(tri)],
        out_specs=(row(D), row(LANE), row(LANE),
                   pl.BlockSpec((1, 1, LANE), lambda c: (c, 0, 0)),
                   pl.BlockSpec((1, LANE), lambda c: (0, 0))),
        out_shape=(jax.ShapeDtypeStruct((T, D), BF16),
                   jax.ShapeDtypeStruct((T, LANE), jnp.int32),
                   jax.ShapeDtypeStruct((T, LANE), F32),
                   jax.ShapeDtypeStruct((C, 1, LANE), F32),
                   jax.ShapeDtypeStruct((1, LANE), F32)),
        scratch_shapes=[pltpu.VMEM((1, LANE), F32)],
        compiler_params=_params(("arbitrary",)),
        name="moe_router",
    )(x2, g, w_router, tri)


def _dispatch_kernel(a_ref, h_ref, tok_ref, gate_ref, zlo_ref, zhi_ref, zglo_ref, zghi_ref,
                     lo_ref, hi_ref, glo_ref, ghi_ref):
    del zlo_ref, zhi_ref, zglo_ref, zghi_ref
    e = pl.program_id(0)
    c = pl.program_id(1)
    idx = e * pl.num_programs(1) + c
    a = a_ref[idx]
    first = (c == 0) | (a != a_ref[jnp.maximum(idx - 1, 0)])

    @pl.when(first)
    def _():
        lo_ref[...] = jnp.zeros_like(lo_ref)
        hi_ref[...] = jnp.zeros_like(hi_ref)
        glo_ref[...] = jnp.zeros_like(glo_ref)
        ghi_ref[...] = jnp.zeros_like(ghi_ref)

    tok = tok_ref[0]
    gts = gate_ref[0]
    rows = a * MOE_CHUNK + lax.broadcasted_iota(jnp.int32, (2 * MOE_CHUNK, 1), 0)
    c1 = (tok[0:1] == rows) & (tok[2:3] == e)
    c2 = (tok[1:2] == rows) & (tok[3:4] == e)
    xs = _dot(jnp.where(c1 | c2, 1.0, 0.0).astype(BF16), h_ref[...])
    gs = jnp.sum(jnp.where(c1, gts[0:1], 0.0) + jnp.where(c2, gts[1:2], 0.0), axis=1, keepdims=True)
    lo_ref[...] += xs[:MOE_CHUNK].astype(lo_ref.dtype)
    hi_ref[...] += xs[MOE_CHUNK:].astype(hi_ref.dtype)
    glo_ref[...] += gs[:MOE_CHUNK]
    ghi_ref[...] += gs[MOE_CHUNK:]


def _dispatch(a_ec, h, tok_rows, gate_rows, n_slots):
    T, D = h.shape
    C = T // MOE_CHUNK
    zx = jnp.zeros((n_slots, D), BF16)
    zg = jnp.zeros((n_slots, 1), F32)
    any_spec = pl.BlockSpec(memory_space=pl.ANY)
    lo = lambda n: pl.BlockSpec((MOE_CHUNK, n), lambda e, c, a: (a[e * C + c], 0))
    hi = lambda n: pl.BlockSpec((MOE_CHUNK, n), lambda e, c, a: (a[e * C + c] + 1, 0))
    return pl.pallas_call(
        _dispatch_kernel,
        grid_spec=pltpu.PrefetchScalarGridSpec(
            num_scalar_prefetch=1,
            grid=(N_EXPERTS, C),
            in_specs=[pl.BlockSpec((MOE_CHUNK, D), lambda e, c, a: (c, 0)),
                      pl.BlockSpec((1, 8, MOE_CHUNK), lambda e, c, a: (c, 0, 0)),
                      pl.BlockSpec((1, 8, MOE_CHUNK), lambda e, c, a: (c, 0, 0)),
                      any_spec, any_spec, any_spec, any_spec],
            out_specs=(lo(D), hi(D), lo(1), hi(1))),
        out_shape=(jax.ShapeDtypeStruct(zx.shape, BF16), jax.ShapeDtypeStruct(zx.shape, BF16),
                   jax.ShapeDtypeStruct(zg.shape, F32), jax.ShapeDtypeStruct(zg.shape, F32)),
        input_output_aliases={4: 0, 5: 1, 6: 2, 7: 3},
        compiler_params=_params(("arbitrary", "arbitrary")),
        name="moe_dispatch",
    )(a_ec, h, tok_rows, gate_rows, zx, zx, zg, zg)


def _expert_ffn_kernel(te_ref, nu_ref, lo_ref, hi_ref, glo_ref, ghi_ref, wg_ref, wu_ref, wd_ref, o_ref, acc_ref):
    i = pl.program_id(0)
    f = pl.program_id(1)
    used = i < nu_ref[0]

    @pl.when(used & (f == 0))
    def _():
        acc_ref[...] = jnp.zeros_like(acc_ref)

    @pl.when(used)
    def _():
        x = lo_ref[...] + hi_ref[...]
        act = _silu(_dot(x, wg_ref[0])) * _dot(x, wu_ref[0])
        acc_ref[...] += _dot(act.astype(BF16), wd_ref[0])

    @pl.when(f == pl.num_programs(1) - 1)
    def _():
        y = acc_ref[...] * (glo_ref[...] + ghi_ref[...])
        o_ref[...] = jnp.where(used, y, 0.0).astype(o_ref.dtype)


def _expert_ffn(tile_expert, n_used, xs_lo, xs_hi, gs_lo, gs_hi, wg, wu, wd, tf=512):
    N, D = xs_lo.shape
    F = wg.shape[2]
    nf = F // tf
    tr = MOE_ROW_TILE
    fidx = lambda i, f, te, nu: jnp.where(i < nu[0], f, nf - 1)
    row = lambda n: pl.BlockSpec((tr, n), lambda i, f, te, nu: (i, 0))
    return pl.pallas_call(
        _expert_ffn_kernel,
        grid_spec=pltpu.PrefetchScalarGridSpec(
            num_scalar_prefetch=2,
            grid=(N // tr, nf),
            in_specs=[row(D), row(D), row(1), row(1),
                      pl.BlockSpec((1, D, tf), lambda i, f, te, nu: (te[i], 0, fidx(i, f, te, nu))),
                      pl.BlockSpec((1, D, tf), lambda i, f, te, nu: (te[i], 0, fidx(i, f, te, nu))),
                      pl.BlockSpec((1, tf, D), lambda i, f, te, nu: (te[i], fidx(i, f, te, nu), 0))],
            out_specs=row(D),
            scratch_shapes=[pltpu.VMEM((tr, D), F32)]),
        out_shape=jax.ShapeDtypeStruct((N, D), BF16),
        compiler_params=_params(("arbitrary", "arbitrary")),
        name="moe_expert_ffn",
    )(tile_expert, n_used, xs_lo, xs_hi, gs_lo, gs_hi, wg, wu, wd)


def _combine_kernel(a_ref, x_ref, ylo_ref, yhi_ref, tok_ref, gf_ref, o_ref, acc_ref):
    c = pl.program_id(0)
    e = pl.program_id(1)

    @pl.when(e == 0)
    def _():
        acc_ref[...] = jnp.zeros_like(acc_ref)

    tok = tok_ref[...]
    cols = a_ref[c * pl.num_programs(1) + e] * MOE_CHUNK + lax.broadcasted_iota(jnp.int32, (1, 2 * MOE_CHUNK), 1)
    pick = ((tok[:, 0:1] == cols) & (tok[:, 2:3] == e)) | ((tok[:, 1:2] == cols) & (tok[:, 3:4] == e))
    y2 = jnp.concatenate([ylo_ref[...], yhi_ref[...]], axis=0)
    acc_ref[...] += _dot(jnp.where(pick, 1.0, 0.0).astype(BF16), y2)

    @pl.when(e == pl.num_programs(1) - 1)
    def _():
        o_ref[...] = _rms(x_ref[...] + acc_ref[...], gf_ref[...])


def _combine(a_ce, x2, ys, tok_cols, g_final):
    T, D = x2.shape
    C = T // MOE_CHUNK
    E = N_EXPERTS
    return pl.pallas_call(
        _combine_kernel,
        grid_spec=pltpu.PrefetchScalarGridSpec(
            num_scalar_prefetch=1,
            grid=(C, E),
            in_specs=[pl.BlockSpec((MOE_CHUNK, D), lambda c, e, a: (c, 0)),
                      pl.BlockSpec((MOE_CHUNK, D), lambda c, e, a: (a[c * E + e], 0)),
                      pl.BlockSpec((MOE_CHUNK, D), lambda c, e, a: (a[c * E + e] + 1, 0)),
                      pl.BlockSpec((MOE_CHUNK, LANE), lambda c, e, a: (c, 0)),
                      pl.BlockSpec((1, D), lambda c, e, a: (0, 0))],
            out_specs=pl.BlockSpec((MOE_CHUNK, D), lambda c, e, a: (c, 0)),
            scratch_shapes=[pltpu.VMEM((MOE_CHUNK, D), F32)]),
        out_shape=jax.ShapeDtypeStruct((T, D), F32),
        compiler_params=_params(("arbitrary", "arbitrary")),
        name="moe_combine_final_norm",
    )(a_ce, x2, ys, ys, tok_cols, g_final)


def _rope_tables(S):
    inv = ROPE_THETA ** (-jnp.arange(ROPE_HALF, dtype=F32) / ROPE_HALF)
    ang = jnp.arange(S, dtype=F32)[:, None] * inv[None, :]
    cos, sin = jnp.cos(ang), jnp.sin(ang)
    pad = HEAD_DIM - ROPE_DIM
    cos_h = jnp.concatenate([cos, cos, jnp.ones((S, pad), F32)], axis=-1)
    sin_h = jnp.concatenate([-sin, sin, jnp.zeros((S, pad), F32)], axis=-1)
    reps = LANE // HEAD_DIM
    return jnp.tile(cos_h, (1, reps)), jnp.tile(sin_h, (1, reps))


def _pack_w_in(w):
    per_group = GROUP * N_BRANCH
    gates = w[:, GATE_OFF:GATE_OFF + N_KV_HEADS * per_group].reshape(-1, N_KV_HEADS, per_group)
    gates = jnp.pad(gates, ((0, 0), (0, LANE // GATE_ROWS - N_KV_HEADS), (0, GATE_ROWS - per_group)))
    gates = gates.reshape(-1, LANE)
    return jnp.concatenate([w[:, :GATE_OFF], gates, w[:, GATE_OFF + N_KV_HEADS * per_group:]], axis=1).astype(BF16)


def _mixer_layer(x, norm_g, w_in, pe_k, w1_k, b1_k, w2_k, pe_v, w1_v, b1_v, w2_v,
                 w_pool, pool_scale, w_out, cos_t, sin_t):
    B, S, D = x.shape
    qt, qrt, kcr, vcr, ks, vst, kw, vwt, gates, u = _mixer_proj(
        x, norm_g.reshape(1, D), _pack_w_in(w_in), cos_t, sin_t)
    half = CMP_LEN // 2
    cw = half * HEAD_DIM
    pad_w2 = lambda w2: jnp.pad(w2, ((0, 0), (0, LANE - HEAD_DIM))).astype(BF16)
    kc, vct = _compress(
        kcr, vcr,
        pe_k.reshape(2, cw), w1_k.reshape(2, cw, CMP_HIDDEN).astype(BF16), b1_k.reshape(1, -1), pad_w2(w2_k),
        pe_v.reshape(2, cw), w1_v.reshape(2, cw, CMP_HIDDEN).astype(BF16), b1_v.reshape(1, -1), pad_w2(w2_v))
    a = _nsa_attention(qt, qrt, kc, vct, ks, vst, kw, vwt, gates)
    p = _pool_mixer(u, w_pool.astype(BF16), pool_scale.reshape(1, -1))
    w_out = w_out.astype(BF16)
    out = _out_proj(x.reshape(B * S, D), a.reshape(B * S, NSA_WIDTH), p.reshape(B * S, POOL_WIDTH),
                    w_out[:NSA_WIDTH], w_out[NSA_WIDTH:])
    return out.reshape(B, S, D)


def _moe_layer(x2, g_ffn, router, wg, wu, wd, g_final):
    T, D = x2.shape
    E = N_EXPERTS
    C = T // MOE_CHUNK
    i32 = jnp.int32
    w_router = jnp.pad(router, ((0, 0), (0, LANE - E)))
    h, route, gates, before, total = _router(x2, g_ffn.reshape(1, D), w_router)

    counts = total[0, :E].astype(i32)
    tiles = (counts + MOE_ROW_TILE - 1) // MOE_ROW_TILE
    ends = jnp.cumsum(tiles)
    off = (ends - tiles) * MOE_ROW_TILE
    e1, e2, r1, r2 = route[:, 0], route[:, 1], route[:, 2], route[:, 3]
    slot1 = off[e1] + r1
    slot2 = off[e2] + r2
    lo_tile = (off[None, :] + before[:, 0, :E].astype(i32)) // MOE_CHUNK
    n_row_tiles = (2 * T) // MOE_ROW_TILE + E + 1
    n_used = ends[-1:]
    tile_ids = jnp.minimum(jnp.arange(n_row_tiles, dtype=i32), n_used[0] - 1)
    tile_expert = jnp.minimum(jnp.sum(ends[None, :] <= tile_ids[:, None], axis=1), E - 1).astype(i32)

    zero = jnp.zeros_like(slot1)
    tok = jnp.stack([slot1, slot2, e1, e2, zero, zero, zero, zero])
    tok_rows = tok.reshape(8, C, MOE_CHUNK).transpose(1, 0, 2)
    gz = jnp.zeros((T,), F32)
    gate_rows = jnp.stack([gates[:, 0], gates[:, 1], gz, gz, gz, gz, gz, gz]).reshape(8, C, MOE_CHUNK).transpose(1, 0, 2)
    tok_cols = jnp.pad(tok[:4].T, ((0, 0), (0, LANE - 4)))

    xs_lo, xs_hi, gs_lo, gs_hi = _dispatch(lo_tile.T.reshape(-1).astype(i32), h, tok_rows, gate_rows,
                                           n_row_tiles * MOE_ROW_TILE)
    ys = _expert_ffn(tile_expert, n_used.astype(i32), xs_lo, xs_hi, gs_lo, gs_hi,
                     wg.astype(BF16), wu.astype(BF16), wd.astype(BF16))
    return _combine(lo_tile.reshape(-1).astype(i32), x2, ys, tok_cols, g_final.reshape(1, D))


def kernel(x, mem, norm_mix, w_in, cmp_pe_k, cmp_w1_k, cmp_b1_k, cmp_w2_k, cmp_pe_v, cmp_w1_v, cmp_b1_v, cmp_w2_v, w_pool, pool_scale, w_out, norm_x, norm_mem, wq_x, wk_x, wv_x, wo_x, norm_ffn, ffn_wg, ffn_wu, ffn_wd, moe_router, moe_wg, moe_wu, moe_wd, norm_final):
    B, S, D = x.shape
    depth = norm_mix.shape[0]
    assert depth == 2, "the final RMSNorm is fused into the expert layer, which must be the last one"
    cos_t, sin_t = _rope_tables(S)
    for layer in range(depth):
        x = _mixer_layer(x, norm_mix[layer], w_in[layer],
                         cmp_pe_k[layer], cmp_w1_k[layer], cmp_b1_k[layer], cmp_w2_k[layer],
                         cmp_pe_v[layer], cmp_w1_v[layer], cmp_b1_v[layer], cmp_w2_v[layer],
                         w_pool[layer], pool_scale[layer], w_out[layer], cos_t, sin_t)
        wkv = jnp.concatenate([wk_x[layer], wv_x[layer]], axis=1).astype(BF16)
        kx, vx = _xattn_kv(mem, norm_mem[layer].reshape(1, D), wkv)
        x = _xattn(x, norm_x[layer].reshape(1, D), wq_x[layer].astype(BF16), kx, vx, wo_x[layer].astype(BF16))
        x2 = x.reshape(B * S, D)
        j = layer // 2
        if layer % 2 == 0:
            x2 = _ffn(x2, norm_ffn[layer].reshape(1, D), ffn_wg[j].astype(BF16), ffn_wu[j].astype(BF16),
                      ffn_wd[j].astype(BF16))
        else:
            x2 = _moe_layer(x2, norm_ffn[layer], moe_router[j], moe_wg[j], moe_wu[j], moe_wd[j], norm_final)
        x = x2.reshape(B, S, D)
    return x
```

```python
import functools
import math

import numpy as np
import jax
import jax.numpy as jnp
from jax import lax
from jax.experimental import pallas as pl
from jax.experimental.pallas import tpu as pltpu

F32 = jnp.float32
BF16 = jnp.bfloat16

D_MODEL = 1024
HEAD_DIM = 64
N_HEADS = 8
N_KV_HEADS = 2
GROUP = N_HEADS // N_KV_HEADS
NSA_WIDTH = N_HEADS * HEAD_DIM
KV_WIDTH = N_KV_HEADS * HEAD_DIM
N_BRANCH = 3
GATE_ROWS = 16
POOL_WIDTH = 512
POOL_WINDOWS = (2, 4, 8, 16)
POOL_GROUP = 128
POOL_HALO = 16
ROPE_DIM = 16
ROPE_HALF = 8
ROPE_THETA = 500000.0
CMP_LEN = 32
CMP_STRIDE = 16
CMP_HIDDEN = 256
SEL_BLOCK = 64
SEL_TOP_N = 16
N_LOCAL_BLOCKS = 2
WINDOW = 512
X_HEADS = 4
X_HEAD_DIM = 256
N_EXPERTS = 8
EPS = 1e-6

LANE = 128
IN_PACKED = NSA_WIDTH + 6 * KV_WIDTH + LANE + POOL_WIDTH
GATE_OFF = NSA_WIDTH + 6 * KV_WIDTH
POOL_OFF = GATE_OFF + LANE

NEG = -1e30
TINY = float(np.finfo(np.float32).tiny)
VMEM_LIMIT = 56 * 1024 * 1024


def _dot(a, b):
    return jnp.dot(a, b, preferred_element_type=F32)


def _dot_nt(a, b):
    return lax.dot_general(a, b, (((1,), (1,)), ((), ())), preferred_element_type=F32)


def _rms(x, g):
    y = x * lax.rsqrt(jnp.mean(x * x, axis=-1, keepdims=True) + EPS)
    return y * g


def _params(sem, limit=VMEM_LIMIT):
    return pltpu.CompilerParams(dimension_semantics=sem, vmem_limit_bytes=limit)


def _mixer_proj_kernel(x_ref, g_ref, w_ref, cos_ref, sin_ref,
                       qt_ref, qrt_ref, kcr_ref, vcr_ref, ks_ref, vst_ref, kw_ref, vwt_ref,
                       gate_ref, u_ref):
    h = _rms(x_ref[0], g_ref[...]).astype(BF16)
    z = _dot(h, w_ref[...])
    tm = z.shape[0]
    cos = cos_ref[...]
    sin = sin_ref[...]
    lane = lax.broadcasted_iota(jnp.int32, (tm, LANE), 1)
    first = (lane & (HEAD_DIM - 1)) < ROPE_HALF
    scale = HEAD_DIM ** -0.5

    def rope(xs):
        partner = jnp.where(first, pltpu.roll(xs, LANE - ROPE_HALF, 1), pltpu.roll(xs, ROPE_HALF, 1))
        return xs * cos + partner * sin

    for s in range(NSA_WIDTH // LANE):
        xs = z[:, s * LANE:(s + 1) * LANE]
        for src, ref in ((xs, qt_ref), (rope(xs), qrt_ref)):
            t = (src * scale).T.astype(BF16)
            ref[0, 2 * s] = t[:HEAD_DIM]
            ref[0, 2 * s + 1] = t[HEAD_DIM:]

    def kv_slab(i):
        return z[:, NSA_WIDTH + i * KV_WIDTH:NSA_WIDTH + (i + 1) * KV_WIDTH]

    for slab, ref in ((kv_slab(0), kcr_ref), (kv_slab(1), vcr_ref),
                      (rope(kv_slab(2)), ks_ref), (rope(kv_slab(4)), kw_ref)):
        for gg in range(N_KV_HEADS):
            ref[0, gg] = slab[:, gg * HEAD_DIM:(gg + 1) * HEAD_DIM].astype(ref.dtype)

    for slab, ref in ((kv_slab(3), vst_ref), (kv_slab(5), vwt_ref)):
        t = slab.T.astype(BF16)
        for gg in range(N_KV_HEADS):
            for k in range(tm // LANE):
                ref[0, gg, k] = t[gg * HEAD_DIM:(gg + 1) * HEAD_DIM, k * LANE:(k + 1) * LANE]

    sig_t = jax.nn.sigmoid(z[:, GATE_OFF:GATE_OFF + LANE]).T
    for gg in range(N_KV_HEADS):
        gate_ref[0, gg] = sig_t[gg * GATE_ROWS:(gg + 1) * GATE_ROWS]
    u_ref[0] = z[:, POOL_OFF:POOL_OFF + POOL_WIDTH]


def _mixer_proj(x, g, w_packed, cos_t, sin_t, tm=512):
    B, S, D = x.shape
    G = N_KV_HEADS
    sd = jax.ShapeDtypeStruct
    out_shape = (sd((B, N_HEADS, HEAD_DIM, S), BF16), sd((B, N_HEADS, HEAD_DIM, S), BF16),
                 sd((B, G, S, HEAD_DIM), F32), sd((B, G, S, HEAD_DIM), F32),
                 sd((B, G, S, HEAD_DIM), BF16), sd((B, G, S // LANE, HEAD_DIM, LANE), BF16),
                 sd((B, G, S, HEAD_DIM), BF16), sd((B, G, S // LANE, HEAD_DIM, LANE), BF16),
                 sd((B, G, GATE_ROWS, S), F32),
                 sd((B, S, POOL_WIDTH), F32))
    qspec = pl.BlockSpec((1, N_HEADS, HEAD_DIM, tm), lambda b, i: (b, 0, 0, i))
    kspec = pl.BlockSpec((1, G, tm, HEAD_DIM), lambda b, i: (b, 0, i, 0))
    vspec = pl.BlockSpec((1, G, tm // LANE, HEAD_DIM, LANE), lambda b, i: (b, 0, i, 0, 0))
    out_specs = (qspec, qspec, kspec, kspec, kspec, vspec, kspec, vspec,
                 pl.BlockSpec((1, G, GATE_ROWS, tm), lambda b, i: (b, 0, 0, i)),
                 pl.BlockSpec((1, tm, POOL_WIDTH), lambda b, i: (b, i, 0)))
    return pl.pallas_call(
        _mixer_proj_kernel,
        grid=(B, S // tm),
        in_specs=[pl.BlockSpec((1, tm, D), lambda b, i: (b, i, 0)),
                  pl.BlockSpec((1, D), lambda b, i: (0, 0)),
                  pl.BlockSpec((D, IN_PACKED), lambda b, i: (0, 0)),
                  pl.BlockSpec((tm, LANE), lambda b, i: (i, 0)),
                  pl.BlockSpec((tm, LANE), lambda b, i: (i, 0))],
        out_specs=out_specs,
        out_shape=out_shape,
        compiler_params=_params(("parallel", "parallel")),
        name="mixer_proj",
    )(x, g, w_packed, cos_t, sin_t)


def _gelu_tanh(x):
    return 0.5 * x * (1.0 + jnp.tanh(math.sqrt(2.0 / math.pi) * (x + 0.044715 * (x * x * x))))


def _compress_kernel(kc_ref, vc_ref, pek_ref, w1k_ref, b1k_ref, w2k_ref,
                     pev_ref, w1v_ref, b1v_ref, w2v_ref, ko_ref, vto_ref):
    def comp(x_ref, pe_ref, w1_ref, b1_ref, w2_ref):
        x = x_ref[0, 0]
        n = x.shape[0]
        a = _dot((x + pe_ref[0:1, :]).astype(BF16), w1_ref[0])
        b = _dot((x + pe_ref[1:2, :]).astype(BF16), w1_ref[1])
        pre = a + pltpu.roll(b, n - 1, 0) + b1_ref[...]
        return _dot(_gelu_tanh(pre).astype(BF16), w2_ref[...])

    ko_ref[0, 0] = comp(kc_ref, pek_ref, w1k_ref, b1k_ref, w2k_ref)[:, :HEAD_DIM].astype(ko_ref.dtype)
    vto_ref[0, 0] = comp(vc_ref, pev_ref, w1v_ref, b1v_ref, w2v_ref).T[:HEAD_DIM].astype(vto_ref.dtype)


def _compress(kcr, vcr, pek, w1k, b1k, w2k, pev, w1v, b1v, w2v):
    B, G, S, _ = kcr.shape
    n = S // CMP_STRIDE
    cw = CMP_STRIDE * HEAD_DIM
    kc = kcr.reshape(B, G, n, cw)
    vc = vcr.reshape(B, G, n, cw)
    xspec = pl.BlockSpec((1, 1, n, cw), lambda b, g: (b, g, 0, 0))
    full = lambda a: pl.BlockSpec(a.shape, lambda b, g: (0,) * a.ndim)
    return pl.pallas_call(
        _compress_kernel,
        grid=(B, G),
        in_specs=[xspec, xspec, full(pek), full(w1k), full(b1k), full(w2k),
                  full(pev), full(w1v), full(b1v), full(w2v)],
        out_specs=(pl.BlockSpec((1, 1, n, HEAD_DIM), lambda b, g: (b, g, 0, 0)),
                   pl.BlockSpec((1, 1, HEAD_DIM, n), lambda b, g: (b, g, 0, 0))),
        out_shape=(jax.ShapeDtypeStruct((B, G, n, HEAD_DIM), BF16),
                   jax.ShapeDtypeStruct((B, G, HEAD_DIM, n), BF16)),
        compiler_params=_params(("parallel", "parallel")),
        name="compress_kv",
    )(kc, vc, pek, w1k, b1k, w2k, pev, w1v, b1v, w2v)


def _nsa_kernel(qt_ref, qrt_ref, kc_ref, vct_ref, ks_ref, vst_ref, kw_ref, vwt_ref, gate_ref,
                covt_ref, o_ref, selbias_ref, *, tq, tk):
    qt = pl.program_id(2)
    q0 = qt * tq
    n_cmp = kc_ref.shape[2]
    n_blk = covt_ref.shape[0]
    heads = lambda ref: jnp.concatenate([ref[0, hh] for hh in range(GROUP)], axis=1)
    q_t = heads(qt_ref)
    qr_t = heads(qrt_ref)
    t_lane = q0 + lax.broadcasted_iota(jnp.int32, (1, tq), 1)
    per_head = lambda a: [a[:, hh * tq:(hh + 1) * tq] for hh in range(GROUP)]
    all_heads = lambda a: jnp.concatenate([a] * GROUP, axis=1)

    sc = _dot(kc_ref[0, 0], q_t)
    cmp_end = lax.broadcasted_iota(jnp.int32, (n_cmp, 1), 0) * CMP_STRIDE + (CMP_LEN - 1)
    valid = cmp_end <= t_lane
    pcs = []
    for s_h in per_head(sc):
        s_h = jnp.where(valid, s_h, NEG)
        e = jnp.where(valid, jnp.exp(s_h - jnp.max(s_h, axis=0, keepdims=True)), 0.0)
        pcs.append(e * (1.0 / jnp.maximum(jnp.sum(e, axis=0, keepdims=True), TINY)))
    oc_t = _dot(vct_ref[0, 0], jnp.concatenate(pcs, axis=1).astype(BF16))

    psum = pcs[0] + pcs[1] + pcs[2] + pcs[3]
    p_hi = psum.astype(BF16)
    r1 = psum - p_hi.astype(F32)
    p_mid = r1.astype(BF16)
    p_lo = (r1 - p_mid.astype(F32)).astype(BF16)
    covt = covt_ref[...]
    imp_t = _dot(covt, p_hi) + _dot(covt, p_mid) + _dot(covt, p_lo)

    jb = lax.broadcasted_iota(jnp.int32, (n_blk, tq), 0)
    tb = (q0 + lax.broadcasted_iota(jnp.int32, (n_blk, tq), 1)) // SEL_BLOCK
    dist = tb - jb
    forced = (jb == 0) | ((dist >= 0) & (dist < N_LOCAL_BLOCKS))
    score = jnp.where(jb > tb, -jnp.inf, jnp.where(forced, jnp.inf, imp_t))
    rank = jnp.zeros((n_blk, tq), jnp.int32)
    for jp in range(n_blk):
        rowv = score[jp:jp + 1, :]
        beats = (rowv > score) | ((rowv == score) & (jb > jp))
        rank = rank + jnp.where(beats, 1, 0)
    selbias_ref[...] = jnp.where(rank < min(SEL_TOP_N, n_blk), 0.0, NEG)

    bpc = tk // SEL_BLOCK
    vpc = tk // LANE

    def sel_body(c, carry):
        m_i, l_i, acc = carry
        start = pl.multiple_of(c * tk, tk)
        k_c = ks_ref[0, 0, pl.ds(start, tk), :]
        v_t = jnp.concatenate([vst_ref[0, 0, c * vpc + k] for k in range(vpc)], axis=1)
        rows = [jnp.broadcast_to(selbias_ref[pl.ds(c * bpc + j, 1), :], (SEL_BLOCK, tq)) for j in range(bpc)]
        kpos = start + lax.broadcasted_iota(jnp.int32, (tk, 1), 0)
        bias = jnp.where(kpos <= t_lane, jnp.concatenate(rows, axis=0), NEG)
        s = _dot(k_c, qr_t) + all_heads(bias)
        m_new = jnp.maximum(m_i, jnp.max(s, axis=0, keepdims=True))
        alpha = jnp.exp(m_i - m_new)
        p = jnp.exp(s - m_new)
        l_new = alpha * l_i + jnp.sum(p, axis=0, keepdims=True)
        acc_new = alpha * acc + _dot(v_t, p.astype(BF16))
        return m_new, l_new, acc_new

    R = GROUP * tq
    init = (jnp.full((1, R), NEG, F32), jnp.zeros((1, R), F32), jnp.zeros((HEAD_DIM, R), F32))
    _, l_s, acc_s = lax.fori_loop(0, (q0 + tq + tk - 1) // tk, sel_body, init)
    os_t = acc_s * (1.0 / jnp.maximum(l_s, TINY))

    wk = WINDOW + tq
    w0 = pl.multiple_of(jnp.maximum(q0 - WINDOW, 0), LANE)
    k_w = kw_ref[0, 0, pl.ds(w0, wk), :]
    v_t = jnp.concatenate([vwt_ref[0, 0, w0 // LANE + k] for k in range(wk // LANE)], axis=1)
    diff = t_lane - (w0 + lax.broadcasted_iota(jnp.int32, (wk, 1), 0))
    bias = jnp.where((diff >= 0) & (diff < WINDOW), 0.0, NEG)
    s = _dot(k_w, qr_t) + all_heads(bias)
    p = jnp.exp(s - jnp.max(s, axis=0, keepdims=True))
    ow_t = _dot(v_t, p.astype(BF16)) * (1.0 / jnp.sum(p, axis=0, keepdims=True))

    gt = gate_ref[0, 0]
    outs = []
    for hh, (a, b, c) in enumerate(zip(per_head(oc_t), per_head(os_t), per_head(ow_t))):
        r = hh * N_BRANCH
        outs.append(gt[r:r + 1] * a + gt[r + 1:r + 2] * b + gt[r + 2:r + 3] * c)
    o_ref[0] = jnp.concatenate(outs, axis=0).T.astype(o_ref.dtype)


def _cover_table(S):
    n_cmp = S // CMP_STRIDE
    n_blk = S // SEL_BLOCK
    cs = np.arange(n_cmp) * CMP_STRIDE
    ss = np.arange(n_blk) * SEL_BLOCK
    cover_t = ((cs[None, :] < ss[:, None] + SEL_BLOCK) & (cs[None, :] + CMP_LEN > ss[:, None]))
    cover_t[:, n_cmp - 1] = False
    return jnp.asarray(cover_t, BF16)


def _nsa_attention(qt, qrt, kc, vct, ks, vst, kw, vwt, gates, tq=128, tk=512):
    B, _, _, S = qt.shape
    tk = min(tk, S)
    assert tq == LANE and S % tk == 0 and S >= WINDOW + tq
    covt = _cover_table(S)
    n_cmp = kc.shape[2]
    n_blk = covt.shape[0]
    qspec = pl.BlockSpec((1, GROUP, HEAD_DIM, tq), lambda b, g, i: (b, g, 0, i))
    kspec = lambda a: pl.BlockSpec((1, 1) + a.shape[2:], lambda b, g, i: (b, g, 0, 0))
    vspec = pl.BlockSpec((1, 1, S // LANE, HEAD_DIM, LANE), lambda b, g, i: (b, g, 0, 0, 0))
    return pl.pallas_call(
        functools.partial(_nsa_kernel, tq=tq, tk=tk),
        grid=(B, N_KV_HEADS, S // tq),
        in_specs=[qspec, qspec,
                  pl.BlockSpec((1, 1, n_cmp, HEAD_DIM), lambda b, g, i: (b, g, 0, 0)),
                  pl.BlockSpec((1, 1, HEAD_DIM, n_cmp), lambda b, g, i: (b, g, 0, 0)),
                  kspec(ks), vspec, kspec(kw), vspec,
                  pl.BlockSpec((1, 1, GATE_ROWS, tq), lambda b, g, i: (b, g, 0, i)),
                  pl.BlockSpec(covt.shape, lambda b, g, i: (0, 0))],
        out_specs=pl.BlockSpec((1, tq, GROUP * HEAD_DIM), lambda b, g, i: (b, i, g)),
        out_shape=jax.ShapeDtypeStruct((B, S, NSA_WIDTH), BF16),
        scratch_shapes=[pltpu.VMEM((n_blk, tq), F32)],
        compiler_params=_params(("parallel", "parallel", "parallel")),
        name="nsa_attention",
    )(qt, qrt, kc, vct, ks, vst, kw, vwt, gates, covt)


def _pool_kernel(u_ref, w_ref, scale_ref, o_ref, buf):
    i = pl.program_id(1)
    tm = u_ref.shape[1]

    @pl.when(i == 0)
    def _():
        buf[0:POOL_HALO, :] = jnp.zeros((POOL_HALO, POOL_WIDTH), F32)

    @pl.when(i > 0)
    def _():
        buf[0:POOL_HALO, :] = buf[tm:tm + POOL_HALO, :]

    buf[POOL_HALO:POOL_HALO + tm, :] = u_ref[0]
    t1 = i * tm + lax.broadcasted_iota(jnp.int32, (tm, 1), 0) + 1
    for gi, w in enumerate(POOL_WINDOWS):
        cols = slice(gi * POOL_GROUP, (gi + 1) * POOL_GROUP)
        cur = buf[POOL_HALO:POOL_HALO + tm, cols]
        tot = cur
        for k in range(1, w):
            tot = tot + buf[POOL_HALO - k:POOL_HALO - k + tm, cols]
        cnt = jnp.minimum(t1, w).astype(F32)
        d = tot / cnt - cur
        y = _dot(d.astype(BF16), w_ref[gi])
        o_ref[0, :, cols] = (y * scale_ref[:, cols]).astype(o_ref.dtype)


def _pool_mixer(u, w_pool, pool_scale, tm=512):
    B, S, _ = u.shape
    return pl.pallas_call(
        _pool_kernel,
        grid=(B, S // tm),
        in_specs=[pl.BlockSpec((1, tm, POOL_WIDTH), lambda b, i: (b, i, 0)),
                  pl.BlockSpec(w_pool.shape, lambda b, i: (0, 0, 0)),
                  pl.BlockSpec((1, POOL_WIDTH), lambda b, i: (0, 0))],
        out_specs=pl.BlockSpec((1, tm, POOL_WIDTH), lambda b, i: (b, i, 0)),
        out_shape=jax.ShapeDtypeStruct((B, S, POOL_WIDTH), BF16),
        scratch_shapes=[pltpu.VMEM((tm + POOL_HALO, POOL_WIDTH), F32)],
        compiler_params=_params(("parallel", "arbitrary")),
        name="pool_mixer",
    )(u, w_pool, pool_scale)


def _out_proj_kernel(x_ref, a_ref, p_ref, wa_ref, wp_ref, o_ref):
    o_ref[...] = x_ref[...] + _dot(a_ref[...], wa_ref[...]) + _dot(p_ref[...], wp_ref[...])


def _out_proj(x2, a2, p2, wa, wp, tm=512):
    T, D = x2.shape
    row = lambda n: pl.BlockSpec((tm, n), lambda i: (i, 0))
    full = lambda a: pl.BlockSpec(a.shape, lambda i: (0, 0))
    return pl.pallas_call(
        _out_proj_kernel,
        grid=(T // tm,),
        in_specs=[row(D), row(a2.shape[1]), row(p2.shape[1]), full(wa), full(wp)],
        out_specs=row(D),
        out_shape=jax.ShapeDtypeStruct((T, D), F32),
        compiler_params=_params(("parallel",)),
        name="mixer_out_proj",
    )(x2, a2, p2, wa, wp)


def _xkv_kernel(m_ref, g_ref, w_ref, k_ref, v_ref):
    h = _rms(m_ref[0], g_ref[...]).astype(BF16)
    kv = _dot(h, w_ref[...])
    d = k_ref.shape[2]
    k_ref[0] = kv[:, :d].astype(k_ref.dtype)
    v_ref[0] = kv[:, d:].astype(v_ref.dtype)


def _xattn_kv(mem, g, wkv):
    B, M, D = mem.shape
    ospec = pl.BlockSpec((1, M, D), lambda b: (b, 0, 0))
    oshape = jax.ShapeDtypeStruct((B, M, D), BF16)
    return pl.pallas_call(
        _xkv_kernel,
        grid=(B,),
        in_specs=[pl.BlockSpec((1, M, D), lambda b: (b, 0, 0)),
                  pl.BlockSpec((1, D), lambda b: (0, 0)),
                  pl.BlockSpec(wkv.shape, lambda b: (0, 0))],
        out_specs=(ospec, ospec),
        out_shape=(oshape, oshape),
        compiler_params=_params(("parallel",)),
        name="xattn_kv",
    )(mem, g, wkv)


def _xattn_kernel(x_ref, g_ref, wq_ref, k_ref, v_ref, wo_ref, o_ref):
    x = x_ref[0]
    h = _rms(x, g_ref[...]).astype(BF16)
    scale = X_HEAD_DIM ** -0.5
    q = (_dot(h, wq_ref[...]) * scale).astype(BF16)
    outs = []
    for hd in range(X_HEADS):
        sl = slice(hd * X_HEAD_DIM, (hd + 1) * X_HEAD_DIM)
        s = _dot_nt(q[:, sl], k_ref[0, :, sl])
        e = jnp.exp(s - jnp.max(s, axis=-1, keepdims=True))
        p = e / jnp.sum(e, axis=-1, keepdims=True)
        outs.append(_dot(p.astype(BF16), v_ref[0, :, sl]).astype(BF16))
    o = jnp.concatenate(outs, axis=-1)
    o_ref[0] = x + _dot(o, wo_ref[...])


def _xattn(x, g, wq, kx, vx, wo, tm=512):
    B, S, D = x.shape
    M = kx.shape[1]
    xspec = pl.BlockSpec((1, tm, D), lambda b, i: (b, i, 0))
    full = lambda a: pl.BlockSpec(a.shape, lambda b, i: (0, 0))
    mspec = pl.BlockSpec((1, M, D), lambda b, i: (b, 0, 0))
    return pl.pallas_call(
        _xattn_kernel,
        grid=(B, S // tm),
        in_specs=[xspec, full(g), full(wq), mspec, mspec, full(wo)],
        out_specs=xspec,
        out_shape=jax.ShapeDtypeStruct((B, S, D), F32),
        compiler_params=_params(("parallel", "parallel")),
        name="xattn",
    )(x, g, wq, kx, vx, wo)


def _silu(x):
    return x * jax.nn.sigmoid(x)


def _ffn_kernel(x_ref, g_ref, wg_ref, wu_ref, wd_ref, o_ref, h_ref, acc_ref):
    f = pl.program_id(1)

    @pl.when(f == 0)
    def _():
        h_ref[...] = _rms(x_ref[...], g_ref[...]).astype(BF16)
        acc_ref[...] = jnp.zeros_like(acc_ref)

    h = h_ref[...]
    act = _silu(_dot(h, wg_ref[...])) * _dot(h, wu_ref[...])
    acc_ref[...] += _dot(act.astype(BF16), wd_ref[...])

    @pl.when(f == pl.num_programs(1) - 1)
    def _():
        o_ref[...] = x_ref[...] + acc_ref[...]


def _ffn(x2, g, wg, wu, wd, tm=512, tf=1408):
    T, D = x2.shape
    F = wg.shape[1]
    return pl.pallas_call(
        _ffn_kernel,
        grid=(T // tm, F // tf),
        in_specs=[pl.BlockSpec((tm, D), lambda i, f: (i, 0)),
                  pl.BlockSpec((1, D), lambda i, f: (0, 0)),
                  pl.BlockSpec((D, tf), lambda i, f: (0, f)),
                  pl.BlockSpec((D, tf), lambda i, f: (0, f)),
                  pl.BlockSpec((tf, D), lambda i, f: (f, 0))],
        out_specs=pl.BlockSpec((tm, D), lambda i, f: (i, 0)),
        out_shape=jax.ShapeDtypeStruct((T, D), F32),
        scratch_shapes=[pltpu.VMEM((tm, D), BF16), pltpu.VMEM((tm, D), F32)],
        compiler_params=_params(("parallel", "arbitrary")),
        name="dense_swiglu",
    )(x2, g, wg, wu, wd)


MOE_CHUNK = 256
MOE_ROW_TILE = 512


def _tile_meta(m):
    return m >> 2, (m & 1) != 0, (m & 2) != 0


def _router_kernel(x_ref, g_ref, wr_ref, tri_ref, h_ref, route_ref, gate_ref, before_ref, total_ref, run_ref):
    c = pl.program_id(0)

    @pl.when(c == 0)
    def _():
        run_ref[...] = jnp.zeros_like(run_ref)

    hf = _rms(x_ref[...], g_ref[...])
    h_ref[...] = hf.astype(BF16)
    logits = jnp.dot(hf, wr_ref[...], preferred_element_type=F32, precision=lax.Precision.HIGHEST)
    lane = lax.broadcasted_iota(jnp.int32, logits.shape, 1)
    logits = jnp.where(lane < N_EXPERTS, logits, -jnp.inf)
    v1 = jnp.max(logits, axis=-1, keepdims=True)
    i1 = jnp.min(jnp.where(logits == v1, lane, LANE), axis=-1, keepdims=True)
    rest = jnp.where(lane == i1, -jnp.inf, logits)
    v2 = jnp.max(rest, axis=-1, keepdims=True)
    i2 = jnp.min(jnp.where(rest == v2, lane, LANE), axis=-1, keepdims=True)
    e2 = jnp.exp(v2 - v1)
    den = 1.0 + e2
    gate_ref[...] = jnp.where(lane == 0, 1.0 / den, jnp.where(lane == 1, e2 / den, 0.0))

    onehot = jnp.where((lane == i1) | (lane == i2), 1.0, 0.0)
    run = run_ref[...]
    rank = run + _dot(tri_ref[...], onehot.astype(BF16))
    r1 = jnp.sum(jnp.where(lane == i1, rank, 0.0), axis=-1, keepdims=True).astype(jnp.int32)
    r2 = jnp.sum(jnp.where(lane == i2, rank, 0.0), axis=-1, keepdims=True).astype(jnp.int32)
    route_ref[...] = jnp.where(lane == 0, i1, jnp.where(lane == 1, i2,
                               jnp.where(lane == 2, r1, jnp.where(lane == 3, r2, 0))))
    before_ref[0] = run
    run = run + jnp.sum(onehot, axis=0, keepdims=True)
    run_ref[...] = run
    total_ref[...] = run


def _router(x2, g, w_router):
    T, D = x2.shape
    C = T // MOE_CHUNK
    tri = jnp.asarray(np.tril(np.ones((MOE_CHUNK, MOE_CHUNK), np.float32), -1), BF16)
    row = lambda n: pl.BlockSpec((MOE_CHUNK, n), lambda c: (c, 0))
    full = lambda a: pl.BlockSpec(a.shape, lambda c: (0, 0))
    return pl.pallas_call(
        _router_kernel,
        grid=(C,),
        in_specs=[row(D), full(g), full(w_router), full(tri)],
        out_specs=(row(D), row(LANE), row(LANE),
                   pl.BlockSpec((1, 1, LANE), lambda c: (c, 0, 0)),
                   pl.BlockSpec((1, LANE), lambda c: (0, 0))),
        out_shape=(jax.ShapeDtypeStruct((T, D), BF16),
                   jax.ShapeDtypeStruct((T, LANE), jnp.int32),
                   jax.ShapeDtypeStruct((T, LANE), F32),
                   jax.ShapeDtypeStruct((C, 1, LANE), F32),
                   jax.ShapeDtypeStruct((1, LANE), F32)),
        scratch_shapes=[pltpu.VMEM((1, LANE), F32)],
        compiler_params=_params(("arbitrary",)),
        name="moe_router",
    )(x2, g, w_router, tri)


def _dispatch_kernel(a_ref, h_ref, tok_ref, gate_ref, zlo_ref, zhi_ref, zglo_ref, zghi_ref,
                     lo_ref, hi_ref, glo_ref, ghi_ref):
    del zlo_ref, zhi_ref, zglo_ref, zghi_ref
    e = pl.program_id(0)
    c = pl.program_id(1)
    idx = e * pl.num_programs(1) + c
    a, has_lo, has_hi = _tile_meta(a_ref[idx])
    first = (c == 0) | (a != _tile_meta(a_ref[jnp.maximum(idx - 1, 0)])[0])

    @pl.when(first)
    def _():
        lo_ref[...] = jnp.zeros_like(lo_ref)
        hi_ref[...] = jnp.zeros_like(hi_ref)
        glo_ref[...] = jnp.zeros_like(glo_ref)
        ghi_ref[...] = jnp.zeros_like(ghi_ref)

    def scatter(tile, x_ref, g_ref):
        tok = tok_ref[0]
        gts = gate_ref[0]
        rows = tile * MOE_CHUNK + lax.broadcasted_iota(jnp.int32, (MOE_CHUNK, 1), 0)
        c1 = (tok[0:1] == rows) & (tok[2:3] == e)
        c2 = (tok[1:2] == rows) & (tok[3:4] == e)
        x_ref[...] += _dot(jnp.where(c1 | c2, 1.0, 0.0).astype(BF16), h_ref[...]).astype(x_ref.dtype)
        g_ref[...] += jnp.sum(jnp.where(c1, gts[0:1], 0.0) + jnp.where(c2, gts[1:2], 0.0), axis=1, keepdims=True)

    @pl.when(has_lo)
    def _():
        scatter(a, lo_ref, glo_ref)

    @pl.when(has_hi)
    def _():
        scatter(a + 1, hi_ref, ghi_ref)


def _dispatch(a_ec, h, tok_rows, gate_rows, n_slots):
    T, D = h.shape
    C = T // MOE_CHUNK
    zx = jnp.zeros((n_slots, D), BF16)
    zg = jnp.zeros((n_slots, 1), F32)
    any_spec = pl.BlockSpec(memory_space=pl.ANY)
    lo = lambda n: pl.BlockSpec((MOE_CHUNK, n), lambda e, c, a: (a[e * C + c] >> 2, 0))
    hi = lambda n: pl.BlockSpec((MOE_CHUNK, n), lambda e, c, a: ((a[e * C + c] >> 2) + 1, 0))
    return pl.pallas_call(
        _dispatch_kernel,
        grid_spec=pltpu.PrefetchScalarGridSpec(
            num_scalar_prefetch=1,
            grid=(N_EXPERTS, C),
            in_specs=[pl.BlockSpec((MOE_CHUNK, D), lambda e, c, a: (c, 0)),
                      pl.BlockSpec((1, 8, MOE_CHUNK), lambda e, c, a: (c, 0, 0)),
                      pl.BlockSpec((1, 8, MOE_CHUNK), lambda e, c, a: (c, 0, 0)),
                      any_spec, any_spec, any_spec, any_spec],
            out_specs=(lo(D), hi(D), lo(1), hi(1))),
        out_shape=(jax.ShapeDtypeStruct(zx.shape, BF16), jax.ShapeDtypeStruct(zx.shape, BF16),
                   jax.ShapeDtypeStruct(zg.shape, F32), jax.ShapeDtypeStruct(zg.shape, F32)),
        input_output_aliases={4: 0, 5: 1, 6: 2, 7: 3},
        compiler_params=_params(("arbitrary", "arbitrary")),
        name="moe_dispatch",
    )(a_ec, h, tok_rows, gate_rows, zx, zx, zg, zg)


def _expert_ffn_kernel(te_ref, nu_ref, lo_ref, hi_ref, glo_ref, ghi_ref, wg_ref, wu_ref, wd_ref, o_ref, acc_ref):
    i = pl.program_id(0)
    f = pl.program_id(1)
    used = i < nu_ref[0]

    @pl.when(used & (f == 0))
    def _():
        acc_ref[...] = jnp.zeros_like(acc_ref)

    @pl.when(used)
    def _():
        x = lo_ref[...] + hi_ref[...]
        act = _silu(_dot(x, wg_ref[0])) * _dot(x, wu_ref[0])
        acc_ref[...] += _dot(act.astype(BF16), wd_ref[0])

    last = f == pl.num_programs(1) - 1

    @pl.when(used & last)
    def _():
        o_ref[...] = (acc_ref[...] * (glo_ref[...] + ghi_ref[...])).astype(o_ref.dtype)

    @pl.when(jnp.logical_not(used) & last)
    def _():
        o_ref[...] = jnp.zeros_like(o_ref)


def _expert_ffn(tile_expert, n_used, xs_lo, xs_hi, gs_lo, gs_hi, wg, wu, wd, tf=512):
    N, D = xs_lo.shape
    F = wg.shape[2]
    nf = F // tf
    tr = MOE_ROW_TILE
    fidx = lambda i, f, te, nu: jnp.where(i < nu[0], f, nf - 1)
    row = lambda n: pl.BlockSpec((tr, n), lambda i, f, te, nu: (i, 0))
    return pl.pallas_call(
        _expert_ffn_kernel,
        grid_spec=pltpu.PrefetchScalarGridSpec(
            num_scalar_prefetch=2,
            grid=(N // tr, nf),
            in_specs=[row(D), row(D), row(1), row(1),
                      pl.BlockSpec((1, D, tf), lambda i, f, te, nu: (te[i], 0, fidx(i, f, te, nu))),
                      pl.BlockSpec((1, D, tf), lambda i, f, te, nu: (te[i], 0, fidx(i, f, te, nu))),
                      pl.BlockSpec((1, tf, D), lambda i, f, te, nu: (te[i], fidx(i, f, te, nu), 0))],
            out_specs=row(D),
            scratch_shapes=[pltpu.VMEM((tr, D), F32)]),
        out_shape=jax.ShapeDtypeStruct((N, D), BF16),
        compiler_params=_params(("arbitrary", "arbitrary")),
        name="moe_expert_ffn",
    )(tile_expert, n_used, xs_lo, xs_hi, gs_lo, gs_hi, wg, wu, wd)


def _combine_kernel(a_ref, x_ref, ylo_ref, yhi_ref, tok_ref, gf_ref, o_ref, acc_ref):
    c = pl.program_id(0)
    e = pl.program_id(1)

    @pl.when(e == 0)
    def _():
        acc_ref[...] = jnp.zeros_like(acc_ref)

    a, has_lo, has_hi = _tile_meta(a_ref[c * pl.num_programs(1) + e])

    def gather(tile, y_ref):
        tok = tok_ref[...]
        cols = tile * MOE_CHUNK + lax.broadcasted_iota(jnp.int32, (1, MOE_CHUNK), 1)
        pick = ((tok[:, 0:1] == cols) & (tok[:, 2:3] == e)) | ((tok[:, 1:2] == cols) & (tok[:, 3:4] == e))
        acc_ref[...] += _dot(jnp.where(pick, 1.0, 0.0).astype(BF16), y_ref[...])

    @pl.when(has_lo)
    def _():
        gather(a, ylo_ref)

    @pl.when(has_hi)
    def _():
        gather(a + 1, yhi_ref)

    @pl.when(e == pl.num_programs(1) - 1)
    def _():
        o_ref[...] = _rms(x_ref[...] + acc_ref[...], gf_ref[...])


def _combine(a_ce, x2, ys, tok_cols, g_final):
    T, D = x2.shape
    C = T // MOE_CHUNK
    E = N_EXPERTS
    return pl.pallas_call(
        _combine_kernel,
        grid_spec=pltpu.PrefetchScalarGridSpec(
            num_scalar_prefetch=1,
            grid=(C, E),
            in_specs=[pl.BlockSpec((MOE_CHUNK, D), lambda c, e, a: (c, 0)),
                      pl.BlockSpec((MOE_CHUNK, D), lambda c, e, a: (a[c * E + e] >> 2, 0)),
                      pl.BlockSpec((MOE_CHUNK, D), lambda c, e, a: ((a[c * E + e] >> 2) + 1, 0)),
                      pl.BlockSpec((MOE_CHUNK, LANE), lambda c, e, a: (c, 0)),
                      pl.BlockSpec((1, D), lambda c, e, a: (0, 0))],
            out_specs=pl.BlockSpec((MOE_CHUNK, D), lambda c, e, a: (c, 0)),
            scratch_shapes=[pltpu.VMEM((MOE_CHUNK, D), F32)]),
        out_shape=jax.ShapeDtypeStruct((T, D), F32),
        compiler_params=_params(("arbitrary", "arbitrary")),
        name="moe_combine_final_norm",
    )(a_ce, x2, ys, ys, tok_cols, g_final)


def _rope_tables(S):
    inv = ROPE_THETA ** (-jnp.arange(ROPE_HALF, dtype=F32) / ROPE_HALF)
    ang = jnp.arange(S, dtype=F32)[:, None] * inv[None, :]
    cos, sin = jnp.cos(ang), jnp.sin(ang)
    pad = HEAD_DIM - ROPE_DIM
    cos_h = jnp.concatenate([cos, cos, jnp.ones((S, pad), F32)], axis=-1)
    sin_h = jnp.concatenate([-sin, sin, jnp.zeros((S, pad), F32)], axis=-1)
    reps = LANE // HEAD_DIM
    return jnp.tile(cos_h, (1, reps)), jnp.tile(sin_h, (1, reps))


def _pack_w_in(w):
    per_group = GROUP * N_BRANCH
    gates = w[:, GATE_OFF:GATE_OFF + N_KV_HEADS * per_group].reshape(-1, N_KV_HEADS, per_group)
    gates = jnp.pad(gates, ((0, 0), (0, LANE // GATE_ROWS - N_KV_HEADS), (0, GATE_ROWS - per_group)))
    gates = gates.reshape(-1, LANE)
    return jnp.concatenate([w[:, :GATE_OFF], gates, w[:, GATE_OFF + N_KV_HEADS * per_group:]], axis=1).astype(BF16)


def _mixer_layer(x, norm_g, w_in, pe_k, w1_k, b1_k, w2_k, pe_v, w1_v, b1_v, w2_v,
                 w_pool, pool_scale, w_out, cos_t, sin_t):
    B, S, D = x.shape
    qt, qrt, kcr, vcr, ks, vst, kw, vwt, gates, u = _mixer_proj(
        x, norm_g.reshape(1, D), _pack_w_in(w_in), cos_t, sin_t)
    half = CMP_LEN // 2
    cw = half * HEAD_DIM
    pad_w2 = lambda w2: jnp.pad(w2, ((0, 0), (0, LANE - HEAD_DIM))).astype(BF16)
    kc, vct = _compress(
        kcr, vcr,
        pe_k.reshape(2, cw), w1_k.reshape(2, cw, CMP_HIDDEN).astype(BF16), b1_k.reshape(1, -1), pad_w2(w2_k),
        pe_v.reshape(2, cw), w1_v.reshape(2, cw, CMP_HIDDEN).astype(BF16), b1_v.reshape(1, -1), pad_w2(w2_v))
    a = _nsa_attention(qt, qrt, kc, vct, ks, vst, kw, vwt, gates)
    p = _pool_mixer(u, w_pool.astype(BF16), pool_scale.reshape(1, -1))
    w_out = w_out.astype(BF16)
    out = _out_proj(x.reshape(B * S, D), a.reshape(B * S, NSA_WIDTH), p.reshape(B * S, POOL_WIDTH),
                    w_out[:NSA_WIDTH], w_out[NSA_WIDTH:])
    return out.reshape(B, S, D)


def _moe_layer(x2, g_ffn, router, wg, wu, wd, g_final):
    T, D = x2.shape
    E = N_EXPERTS
    C = T // MOE_CHUNK
    i32 = jnp.int32
    w_router = jnp.pad(router, ((0, 0), (0, LANE - E)))
    h, route, gates, before, total = _router(x2, g_ffn.reshape(1, D), w_router)

    counts = total[0, :E].astype(i32)
    tiles = (counts + MOE_ROW_TILE - 1) // MOE_ROW_TILE
    ends = jnp.cumsum(tiles)
    off = (ends - tiles) * MOE_ROW_TILE
    e1, e2, r1, r2 = route[:, 0], route[:, 1], route[:, 2], route[:, 3]
    slot1 = off[e1] + r1
    slot2 = off[e2] + r2
    first = off[None, :] + before[:, 0, :E].astype(i32)
    after = jnp.concatenate([before[1:, 0, :E], total[:, :E]], axis=0).astype(i32)
    last = off[None, :] + after - 1
    lo_tile = first // MOE_CHUNK
    has_lo = (last >= first).astype(i32)
    has_hi = ((last >= first) & (last // MOE_CHUNK > lo_tile)).astype(i32)
    tile_meta = lo_tile * 4 + has_hi * 2 + has_lo
    n_row_tiles = (2 * T) // MOE_ROW_TILE + E + 1
    n_used = ends[-1:]
    tile_ids = jnp.minimum(jnp.arange(n_row_tiles, dtype=i32), n_used[0] - 1)
    tile_expert = jnp.minimum(jnp.sum(ends[None, :] <= tile_ids[:, None], axis=1), E - 1).astype(i32)

    zero = jnp.zeros_like(slot1)
    tok = jnp.stack([slot1, slot2, e1, e2, zero, zero, zero, zero])
    tok_rows = tok.reshape(8, C, MOE_CHUNK).transpose(1, 0, 2)
    gz = jnp.zeros((T,), F32)
    gate_rows = jnp.stack([gates[:, 0], gates[:, 1], gz, gz, gz, gz, gz, gz]).reshape(8, C, MOE_CHUNK).transpose(1, 0, 2)
    tok_cols = jnp.pad(tok[:4].T, ((0, 0), (0, LANE - 4)))

    xs_lo, xs_hi, gs_lo, gs_hi = _dispatch(tile_meta.T.reshape(-1).astype(i32), h, tok_rows, gate_rows,
                                           n_row_tiles * MOE_ROW_TILE)
    ys = _expert_ffn(tile_expert, n_used.astype(i32), xs_lo, xs_hi, gs_lo, gs_hi,
                     wg.astype(BF16), wu.astype(BF16), wd.astype(BF16))
    return _combine(tile_meta.reshape(-1).astype(i32), x2, ys, tok_cols, g_final.reshape(1, D))


def kernel(x, mem, norm_mix, w_in, cmp_pe_k, cmp_w1_k, cmp_b1_k, cmp_w2_k, cmp_pe_v, cmp_w1_v, cmp_b1_v, cmp_w2_v, w_pool, pool_scale, w_out, norm_x, norm_mem, wq_x, wk_x, wv_x, wo_x, norm_ffn, ffn_wg, ffn_wu, ffn_wd, moe_router, moe_wg, moe_wu, moe_wd, norm_final):
    B, S, D = x.shape
    depth = norm_mix.shape[0]
    assert depth == 2, "the final RMSNorm is fused into the expert layer, which must be the last one"
    cos_t, sin_t = _rope_tables(S)
    for layer in range(depth):
        x = _mixer_layer(x, norm_mix[layer], w_in[layer],
                         cmp_pe_k[layer], cmp_w1_k[layer], cmp_b1_k[layer], cmp_w2_k[layer],
                         cmp_pe_v[layer], cmp_w1_v[layer], cmp_b1_v[layer], cmp_w2_v[layer],
                         w_pool[layer], pool_scale[layer], w_out[layer], cos_t, sin_t)
        wkv = jnp.concatenate([wk_x[layer], wv_x[layer]], axis=1).astype(BF16)
        kx, vx = _xattn_kv(mem, norm_mem[layer].reshape(1, D), wkv)
        x = _xattn(x, norm_x[layer].reshape(1, D), wq_x[layer].astype(BF16), kx, vx, wo_x[layer].astype(BF16))
        x2 = x.reshape(B * S, D)
        j = layer // 2
        if layer % 2 == 0:
            x2 = _ffn(x2, norm_ffn[layer].reshape(1, D), ffn_wg[j].astype(BF16), ffn_wu[j].astype(BF16),
                      ffn_wd[j].astype(BF16))
        else:
            x2 = _moe_layer(x2, norm_ffn[layer], moe_router[j], moe_wg[j], moe_wu[j], moe_wd[j], norm_final)
        x = x2.reshape(B, S, D)
    return x
```

```python
import functools
import math

import numpy as np
import jax
import jax.numpy as jnp
from jax import lax
from jax.experimental import pallas as pl
from jax.experimental.pallas import tpu as pltpu

F32 = jnp.float32
BF16 = jnp.bfloat16

D_MODEL = 1024
HEAD_DIM = 64
N_HEADS = 8
N_KV_HEADS = 2
GROUP = N_HEADS // N_KV_HEADS
NSA_WIDTH = N_HEADS * HEAD_DIM
KV_WIDTH = N_KV_HEADS * HEAD_DIM
N_BRANCH = 3
GATE_ROWS = 16
POOL_WIDTH = 512
POOL_WINDOWS = (2, 4, 8, 16)
POOL_GROUP = 128
POOL_HALO = 16
ROPE_DIM = 16
ROPE_HALF = 8
ROPE_THETA = 500000.0
CMP_LEN = 32
CMP_STRIDE = 16
CMP_HIDDEN = 256
SEL_BLOCK = 64
SEL_TOP_N = 16
N_LOCAL_BLOCKS = 2
WINDOW = 512
X_HEADS = 4
X_HEAD_DIM = 256
N_EXPERTS = 8
EPS = 1e-6

LANE = 128
IN_PACKED = NSA_WIDTH + 6 * KV_WIDTH + LANE + POOL_WIDTH
GATE_OFF = NSA_WIDTH + 6 * KV_WIDTH
POOL_OFF = GATE_OFF + LANE

NEG = -1e30
TINY = float(np.finfo(np.float32).tiny)
VMEM_LIMIT = 56 * 1024 * 1024


def _dot(a, b):
    return jnp.dot(a, b, preferred_element_type=F32)


def _dot_nt(a, b):
    return lax.dot_general(a, b, (((1,), (1,)), ((), ())), preferred_element_type=F32)


def _rms(x, g):
    y = x * lax.rsqrt(jnp.mean(x * x, axis=-1, keepdims=True) + EPS)
    return y * g


def _params(sem, limit=VMEM_LIMIT):
    return pltpu.CompilerParams(dimension_semantics=sem, vmem_limit_bytes=limit)


def _mixer_proj_kernel(x_ref, g_ref, w_ref, cos_ref, sin_ref,
                       qt_ref, qrt_ref, kcr_ref, vcr_ref, ks_ref, vst_ref, kw_ref, vwt_ref,
                       gate_ref, u_ref):
    h = _rms(x_ref[0], g_ref[...]).astype(BF16)
    z = _dot(h, w_ref[...])
    tm = z.shape[0]
    cos = cos_ref[...]
    sin = sin_ref[...]
    lane = lax.broadcasted_iota(jnp.int32, (tm, LANE), 1)
    first = (lane & (HEAD_DIM - 1)) < ROPE_HALF
    scale = HEAD_DIM ** -0.5

    def rope(xs):
        partner = jnp.where(first, pltpu.roll(xs, LANE - ROPE_HALF, 1), pltpu.roll(xs, ROPE_HALF, 1))
        return xs * cos + partner * sin

    for s in range(NSA_WIDTH // LANE):
        xs = z[:, s * LANE:(s + 1) * LANE]
        for src, ref in ((xs, qt_ref), (rope(xs), qrt_ref)):
            t = (src * scale).T.astype(BF16)
            ref[0, 2 * s] = t[:HEAD_DIM]
            ref[0, 2 * s + 1] = t[HEAD_DIM:]

    def kv_slab(i):
        return z[:, NSA_WIDTH + i * KV_WIDTH:NSA_WIDTH + (i + 1) * KV_WIDTH]

    for slab, ref in ((kv_slab(0), kcr_ref), (kv_slab(1), vcr_ref),
                      (rope(kv_slab(2)), ks_ref), (rope(kv_slab(4)), kw_ref)):
        for gg in range(N_KV_HEADS):
            ref[0, gg] = slab[:, gg * HEAD_DIM:(gg + 1) * HEAD_DIM].astype(ref.dtype)

    for slab, ref in ((kv_slab(3), vst_ref), (kv_slab(5), vwt_ref)):
        t = slab.T.astype(BF16)
        for gg in range(N_KV_HEADS):
            for k in range(tm // LANE):
                ref[0, gg, k] = t[gg * HEAD_DIM:(gg + 1) * HEAD_DIM, k * LANE:(k + 1) * LANE]

    sig_t = jax.nn.sigmoid(z[:, GATE_OFF:GATE_OFF + LANE]).T
    for gg in range(N_KV_HEADS):
        gate_ref[0, gg] = sig_t[gg * GATE_ROWS:(gg + 1) * GATE_ROWS]
    u_ref[0] = z[:, POOL_OFF:POOL_OFF + POOL_WIDTH]


def _mixer_proj(x, g, w_packed, cos_t, sin_t, tm=512):
    B, S, D = x.shape
    G = N_KV_HEADS
    sd = jax.ShapeDtypeStruct
    out_shape = (sd((B, N_HEADS, HEAD_DIM, S), BF16), sd((B, N_HEADS, HEAD_DIM, S), BF16),
                 sd((B, G, S, HEAD_DIM), F32), sd((B, G, S, HEAD_DIM), F32),
                 sd((B, G, S, HEAD_DIM), BF16), sd((B, G, S // LANE, HEAD_DIM, LANE), BF16),
                 sd((B, G, S, HEAD_DIM), BF16), sd((B, G, S // LANE, HEAD_DIM, LANE), BF16),
                 sd((B, G, GATE_ROWS, S), F32),
                 sd((B, S, POOL_WIDTH), F32))
    qspec = pl.BlockSpec((1, N_HEADS, HEAD_DIM, tm), lambda b, i: (b, 0, 0, i))
    kspec = pl.BlockSpec((1, G, tm, HEAD_DIM), lambda b, i: (b, 0, i, 0))
    vspec = pl.BlockSpec((1, G, tm // LANE, HEAD_DIM, LANE), lambda b, i: (b, 0, i, 0, 0))
    out_specs = (qspec, qspec, kspec, kspec, kspec, vspec, kspec, vspec,
                 pl.BlockSpec((1, G, GATE_ROWS, tm), lambda b, i: (b, 0, 0, i)),
                 pl.BlockSpec((1, tm, POOL_WIDTH), lambda b, i: (b, i, 0)))
    return pl.pallas_call(
        _mixer_proj_kernel,
        grid=(B, S // tm),
        in_specs=[pl.BlockSpec((1, tm, D), lambda b, i: (b, i, 0)),
                  pl.BlockSpec((1, D), lambda b, i: (0, 0)),
                  pl.BlockSpec((D, IN_PACKED), lambda b, i: (0, 0)),
                  pl.BlockSpec((tm, LANE), lambda b, i: (i, 0)),
                  pl.BlockSpec((tm, LANE), lambda b, i: (i, 0))],
        out_specs=out_specs,
        out_shape=out_shape,
        compiler_params=_params(("parallel", "parallel")),
        name="mixer_proj",
    )(x, g, w_packed, cos_t, sin_t)


def _gelu_tanh(x):
    return 0.5 * x * (1.0 + jnp.tanh(math.sqrt(2.0 / math.pi) * (x + 0.044715 * (x * x * x))))


def _compress_kernel(kc_ref, vc_ref, pek_ref, w1k_ref, b1k_ref, w2k_ref,
                     pev_ref, w1v_ref, b1v_ref, w2v_ref, ko_ref, vto_ref):
    def comp(x_ref, pe_ref, w1_ref, b1_ref, w2_ref):
        x = x_ref[0, 0]
        n = x.shape[0]
        a = _dot((x + pe_ref[0:1, :]).astype(BF16), w1_ref[0])
        b = _dot((x + pe_ref[1:2, :]).astype(BF16), w1_ref[1])
        pre = a + pltpu.roll(b, n - 1, 0) + b1_ref[...]
        return _dot(_gelu_tanh(pre).astype(BF16), w2_ref[...])

    ko_ref[0, 0] = comp(kc_ref, pek_ref, w1k_ref, b1k_ref, w2k_ref)[:, :HEAD_DIM].astype(ko_ref.dtype)
    vto_ref[0, 0] = comp(vc_ref, pev_ref, w1v_ref, b1v_ref, w2v_ref).T[:HEAD_DIM].astype(vto_ref.dtype)


def _compress(kcr, vcr, pek, w1k, b1k, w2k, pev, w1v, b1v, w2v):
    B, G, S, _ = kcr.shape
    n = S // CMP_STRIDE
    cw = CMP_STRIDE * HEAD_DIM
    kc = kcr.reshape(B, G, n, cw)
    vc = vcr.reshape(B, G, n, cw)
    xspec = pl.BlockSpec((1, 1, n, cw), lambda b, g: (b, g, 0, 0))
    full = lambda a: pl.BlockSpec(a.shape, lambda b, g: (0,) * a.ndim)
    return pl.pallas_call(
        _compress_kernel,
        grid=(B, G),
        in_specs=[xspec, xspec, full(pek), full(w1k), full(b1k), full(w2k),
                  full(pev), full(w1v), full(b1v), full(w2v)],
        out_specs=(pl.BlockSpec((1, 1, n, HEAD_DIM), lambda b, g: (b, g, 0, 0)),
                   pl.BlockSpec((1, 1, HEAD_DIM, n), lambda b, g: (b, g, 0, 0))),
        out_shape=(jax.ShapeDtypeStruct((B, G, n, HEAD_DIM), BF16),
                   jax.ShapeDtypeStruct((B, G, HEAD_DIM, n), BF16)),
        compiler_params=_params(("parallel", "parallel")),
        name="compress_kv",
    )(kc, vc, pek, w1k, b1k, w2k, pev, w1v, b1v, w2v)


def _nsa_kernel(qt_ref, qrt_ref, kc_ref, vct_ref, ks_ref, vst_ref, kw_ref, vwt_ref, gate_ref,
                covt_ref, o_ref, selbias_ref, *, tq, tk):
    qt = pl.program_id(2)
    q0 = qt * tq
    n_cmp = kc_ref.shape[2]
    n_blk = covt_ref.shape[0]
    heads = lambda ref: jnp.concatenate([ref[0, hh] for hh in range(GROUP)], axis=1)
    q_t = heads(qt_ref)
    qr_t = heads(qrt_ref)
    t_lane = q0 + lax.broadcasted_iota(jnp.int32, (1, tq), 1)
    per_head = lambda a: [a[:, hh * tq:(hh + 1) * tq] for hh in range(GROUP)]
    all_heads = lambda a: jnp.concatenate([a] * GROUP, axis=1)

    sc = _dot(kc_ref[0, 0], q_t)
    cmp_end = lax.broadcasted_iota(jnp.int32, (n_cmp, 1), 0) * CMP_STRIDE + (CMP_LEN - 1)
    valid = cmp_end <= t_lane
    pcs = []
    for s_h in per_head(sc):
        s_h = jnp.where(valid, s_h, NEG)
        e = jnp.where(valid, jnp.exp(s_h - jnp.max(s_h, axis=0, keepdims=True)), 0.0)
        pcs.append(e * (1.0 / jnp.maximum(jnp.sum(e, axis=0, keepdims=True), TINY)))
    oc_t = _dot(vct_ref[0, 0], jnp.concatenate(pcs, axis=1).astype(BF16))

    psum = pcs[0] + pcs[1] + pcs[2] + pcs[3]
    p_hi = psum.astype(BF16)
    r1 = psum - p_hi.astype(F32)
    p_mid = r1.astype(BF16)
    p_lo = (r1 - p_mid.astype(F32)).astype(BF16)
    covt = covt_ref[...]
    imp_t = _dot(covt, p_hi) + _dot(covt, p_mid) + _dot(covt, p_lo)

    jb = lax.broadcasted_iota(jnp.int32, (n_blk, tq), 0)
    tb = (q0 + lax.broadcasted_iota(jnp.int32, (n_blk, tq), 1)) // SEL_BLOCK
    dist = tb - jb
    forced = (jb == 0) | ((dist >= 0) & (dist < N_LOCAL_BLOCKS))
    score = jnp.where(jb > tb, -jnp.inf, jnp.where(forced, jnp.inf, imp_t))
    rank = jnp.zeros((n_blk, tq), jnp.int32)
    for jp in range(n_blk):
        rowv = score[jp:jp + 1, :]
        beats = (rowv > score) | ((rowv == score) & (jb > jp))
        rank = rank + jnp.where(beats, 1, 0)
    selbias_ref[...] = jnp.where(rank < min(SEL_TOP_N, n_blk), 0.0, NEG)

    bpc = tk // SEL_BLOCK
    vpc = tk // LANE

    def sel_body(c, carry):
        m_i, l_i, acc = carry
        start = pl.multiple_of(c * tk, tk)
        k_c = ks_ref[0, 0, pl.ds(start, tk), :]
        v_t = jnp.concatenate([vst_ref[0, 0, c * vpc + k] for k in range(vpc)], axis=1)
        rows = [jnp.broadcast_to(selbias_ref[pl.ds(c * bpc + j, 1), :], (SEL_BLOCK, tq)) for j in range(bpc)]
        kpos = start + lax.broadcasted_iota(jnp.int32, (tk, 1), 0)
        bias = jnp.where(kpos <= t_lane, jnp.concatenate(rows, axis=0), NEG)
        s = _dot(k_c, qr_t) + all_heads(bias)
        m_new = jnp.maximum(m_i, jnp.max(s, axis=0, keepdims=True))
        alpha = jnp.exp(m_i - m_new)
        p = jnp.exp(s - m_new)
        l_new = alpha * l_i + jnp.sum(p, axis=0, keepdims=True)
        acc_new = alpha * acc + _dot(v_t, p.astype(BF16))
        return m_new, l_new, acc_new

    R = GROUP * tq
    init = (jnp.full((1, R), NEG, F32), jnp.zeros((1, R), F32), jnp.zeros((HEAD_DIM, R), F32))
    _, l_s, acc_s = lax.fori_loop(0, (q0 + tq + tk - 1) // tk, sel_body, init)
    os_t = acc_s * (1.0 / jnp.maximum(l_s, TINY))

    wk = WINDOW + tq
    w0 = pl.multiple_of(jnp.maximum(q0 - WINDOW, 0), LANE)
    k_w = kw_ref[0, 0, pl.ds(w0, wk), :]
    v_t = jnp.concatenate([vwt_ref[0, 0, w0 // LANE + k] for k in range(wk // LANE)], axis=1)
    diff = t_lane - (w0 + lax.broadcasted_iota(jnp.int32, (wk, 1), 0))
    bias = jnp.where((diff >= 0) & (diff < WINDOW), 0.0, NEG)
    s = _dot(k_w, qr_t) + all_heads(bias)
    p = jnp.exp(s - jnp.max(s, axis=0, keepdims=True))
    ow_t = _dot(v_t, p.astype(BF16)) * (1.0 / jnp.sum(p, axis=0, keepdims=True))

    gt = gate_ref[0, 0]
    outs = []
    for hh, (a, b, c) in enumerate(zip(per_head(oc_t), per_head(os_t), per_head(ow_t))):
        r = hh * N_BRANCH
        outs.append(gt[r:r + 1] * a + gt[r + 1:r + 2] * b + gt[r + 2:r + 3] * c)
    o_ref[0] = jnp.concatenate(outs, axis=0).T.astype(o_ref.dtype)


def _cover_table(S):
    n_cmp = S // CMP_STRIDE
    n_blk = S // SEL_BLOCK
    cs = np.arange(n_cmp) * CMP_STRIDE
    ss = np.arange(n_blk) * SEL_BLOCK
    cover_t = ((cs[None, :] < ss[:, None] + SEL_BLOCK) & (cs[None, :] + CMP_LEN > ss[:, None]))
    cover_t[:, n_cmp - 1] = False
    return jnp.asarray(cover_t, BF16)


def _nsa_attention(qt, qrt, kc, vct, ks, vst, kw, vwt, gates, tq=128, tk=512):
    B, _, _, S = qt.shape
    tk = min(tk, S)
    assert tq == LANE and S % tk == 0 and S >= WINDOW + tq
    covt = _cover_table(S)
    n_cmp = kc.shape[2]
    n_blk = covt.shape[0]
    qspec = pl.BlockSpec((1, GROUP, HEAD_DIM, tq), lambda b, g, i: (b, g, 0, i))
    kspec = lambda a: pl.BlockSpec((1, 1) + a.shape[2:], lambda b, g, i: (b, g, 0, 0))
    vspec = pl.BlockSpec((1, 1, S // LANE, HEAD_DIM, LANE), lambda b, g, i: (b, g, 0, 0, 0))
    return pl.pallas_call(
        functools.partial(_nsa_kernel, tq=tq, tk=tk),
        grid=(B, N_KV_HEADS, S // tq),
        in_specs=[qspec, qspec,
                  pl.BlockSpec((1, 1, n_cmp, HEAD_DIM), lambda b, g, i: (b, g, 0, 0)),
                  pl.BlockSpec((1, 1, HEAD_DIM, n_cmp), lambda b, g, i: (b, g, 0, 0)),
                  kspec(ks), vspec, kspec(kw), vspec,
                  pl.BlockSpec((1, 1, GATE_ROWS, tq), lambda b, g, i: (b, g, 0, i)),
                  pl.BlockSpec(covt.shape, lambda b, g, i: (0, 0))],
        out_specs=pl.BlockSpec((1, tq, GROUP * HEAD_DIM), lambda b, g, i: (b, i, g)),
        out_shape=jax.ShapeDtypeStruct((B, S, NSA_WIDTH), BF16),
        scratch_shapes=[pltpu.VMEM((n_blk, tq), F32)],
        compiler_params=_params(("parallel", "parallel", "parallel")),
        name="nsa_attention",
    )(qt, qrt, kc, vct, ks, vst, kw, vwt, gates, covt)


def _pool_kernel(u_ref, w_ref, scale_ref, o_ref, buf):
    i = pl.program_id(1)
    tm = u_ref.shape[1]

    @pl.when(i == 0)
    def _():
        buf[0:POOL_HALO, :] = jnp.zeros((POOL_HALO, POOL_WIDTH), F32)

    @pl.when(i > 0)
    def _():
        buf[0:POOL_HALO, :] = buf[tm:tm + POOL_HALO, :]

    buf[POOL_HALO:POOL_HALO + tm, :] = u_ref[0]
    t1 = i * tm + lax.broadcasted_iota(jnp.int32, (tm, 1), 0) + 1
    for gi, w in enumerate(POOL_WINDOWS):
        cols = slice(gi * POOL_GROUP, (gi + 1) * POOL_GROUP)
        cur = buf[POOL_HALO:POOL_HALO + tm, cols]
        tot = cur
        for k in range(1, w):
            tot = tot + buf[POOL_HALO - k:POOL_HALO - k + tm, cols]
        cnt = jnp.minimum(t1, w).astype(F32)
        d = tot / cnt - cur
        y = _dot(d.astype(BF16), w_ref[gi])
        o_ref[0, :, cols] = (y * scale_ref[:, cols]).astype(o_ref.dtype)


def _pool_mixer(u, w_pool, pool_scale, tm=512):
    B, S, _ = u.shape
    return pl.pallas_call(
        _pool_kernel,
        grid=(B, S // tm),
        in_specs=[pl.BlockSpec((1, tm, POOL_WIDTH), lambda b, i: (b, i, 0)),
                  pl.BlockSpec(w_pool.shape, lambda b, i: (0, 0, 0)),
                  pl.BlockSpec((1, POOL_WIDTH), lambda b, i: (0, 0))],
        out_specs=pl.BlockSpec((1, tm, POOL_WIDTH), lambda b, i: (b, i, 0)),
        out_shape=jax.ShapeDtypeStruct((B, S, POOL_WIDTH), BF16),
        scratch_shapes=[pltpu.VMEM((tm + POOL_HALO, POOL_WIDTH), F32)],
        compiler_params=_params(("parallel", "arbitrary")),
        name="pool_mixer",
    )(u, w_pool, pool_scale)


def _out_proj_kernel(x_ref, a_ref, p_ref, wa_ref, wp_ref, o_ref):
    o_ref[...] = x_ref[...] + _dot(a_ref[...], wa_ref[...]) + _dot(p_ref[...], wp_ref[...])


def _out_proj(x2, a2, p2, wa, wp, tm=512):
    T, D = x2.shape
    row = lambda n: pl.BlockSpec((tm, n), lambda i: (i, 0))
    full = lambda a: pl.BlockSpec(a.shape, lambda i: (0, 0))
    return pl.pallas_call(
        _out_proj_kernel,
        grid=(T // tm,),
        in_specs=[row(D), row(a2.shape[1]), row(p2.shape[1]), full(wa), full(wp)],
        out_specs=row(D),
        out_shape=jax.ShapeDtypeStruct((T, D), F32),
        compiler_params=_params(("parallel",)),
        name="mixer_out_proj",
    )(x2, a2, p2, wa, wp)


def _xkv_kernel(m_ref, g_ref, w_ref, k_ref, v_ref):
    h = _rms(m_ref[0], g_ref[...]).astype(BF16)
    kv = _dot(h, w_ref[...])
    d = k_ref.shape[2]
    k_ref[0] = kv[:, :d].astype(k_ref.dtype)
    v_ref[0] = kv[:, d:].astype(v_ref.dtype)


def _xattn_kv(mem, g, wkv):
    B, M, D = mem.shape
    ospec = pl.BlockSpec((1, M, D), lambda b: (b, 0, 0))
    oshape = jax.ShapeDtypeStruct((B, M, D), BF16)
    return pl.pallas_call(
        _xkv_kernel,
        grid=(B,),
        in_specs=[pl.BlockSpec((1, M, D), lambda b: (b, 0, 0)),
                  pl.BlockSpec((1, D), lambda b: (0, 0)),
                  pl.BlockSpec(wkv.shape, lambda b: (0, 0))],
        out_specs=(ospec, ospec),
        out_shape=(oshape, oshape),
        compiler_params=_params(("parallel",)),
        name="xattn_kv",
    )(mem, g, wkv)


def _xattn_kernel(x_ref, g_ref, wq_ref, k_ref, v_ref, wo_ref, o_ref):
    x = x_ref[0]
    h = _rms(x, g_ref[...]).astype(BF16)
    scale = X_HEAD_DIM ** -0.5
    q = (_dot(h, wq_ref[...]) * scale).astype(BF16)
    outs = []
    for hd in range(X_HEADS):
        sl = slice(hd * X_HEAD_DIM, (hd + 1) * X_HEAD_DIM)
        s = _dot_nt(q[:, sl], k_ref[0, :, sl])
        e = jnp.exp(s - jnp.max(s, axis=-1, keepdims=True))
        p = e / jnp.sum(e, axis=-1, keepdims=True)
        outs.append(_dot(p.astype(BF16), v_ref[0, :, sl]).astype(BF16))
    o = jnp.concatenate(outs, axis=-1)
    o_ref[0] = x + _dot(o, wo_ref[...])


def _xattn(x, g, wq, kx, vx, wo, tm=512):
    B, S, D = x.shape
    M = kx.shape[1]
    xspec = pl.BlockSpec((1, tm, D), lambda b, i: (b, i, 0))
    full = lambda a: pl.BlockSpec(a.shape, lambda b, i: (0, 0))
    mspec = pl.BlockSpec((1, M, D), lambda b, i: (b, 0, 0))
    return pl.pallas_call(
        _xattn_kernel,
        grid=(B, S // tm),
        in_specs=[xspec, full(g), full(wq), mspec, mspec, full(wo)],
        out_specs=xspec,
        out_shape=jax.ShapeDtypeStruct((B, S, D), F32),
        compiler_params=_params(("parallel", "parallel")),
        name="xattn",
    )(x, g, wq, kx, vx, wo)


def _silu(x):
    return x * jax.nn.sigmoid(x)


def _ffn_kernel(x_ref, g_ref, wg_ref, wu_ref, wd_ref, o_ref, h_ref, acc_ref):
    f = pl.program_id(1)

    @pl.when(f == 0)
    def _():
        h_ref[...] = _rms(x_ref[...], g_ref[...]).astype(BF16)
        acc_ref[...] = jnp.zeros_like(acc_ref)

    h = h_ref[...]
    act = _silu(_dot(h, wg_ref[...])) * _dot(h, wu_ref[...])
    acc_ref[...] += _dot(act.astype(BF16), wd_ref[...])

    @pl.when(f == pl.num_programs(1) - 1)
    def _():
        o_ref[...] = x_ref[...] + acc_ref[...]


def _ffn(x2, g, wg, wu, wd, tm=512, tf=1408):
    T, D = x2.shape
    F = wg.shape[1]
    return pl.pallas_call(
        _ffn_kernel,
        grid=(T // tm, F // tf),
        in_specs=[pl.BlockSpec((tm, D), lambda i, f: (i, 0)),
                  pl.BlockSpec((1, D), lambda i, f: (0, 0)),
                  pl.BlockSpec((D, tf), lambda i, f: (0, f)),
                  pl.BlockSpec((D, tf), lambda i, f: (0, f)),
                  pl.BlockSpec((tf, D), lambda i, f: (f, 0))],
        out_specs=pl.BlockSpec((tm, D), lambda i, f: (i, 0)),
        out_shape=jax.ShapeDtypeStruct((T, D), F32),
        scratch_shapes=[pltpu.VMEM((tm, D), BF16), pltpu.VMEM((tm, D), F32)],
        compiler_params=_params(("parallel", "arbitrary")),
        name="dense_swiglu",
    )(x2, g, wg, wu, wd)


MOE_CHUNK = 512
MOE_ROW_TILE = 512


def _tile_meta(m):
    return m >> 2, (m & 1) != 0, (m & 2) != 0


def _router_kernel(x_ref, g_ref, wr_ref, tri_ref, h_ref, route_ref, gate_ref, before_ref, total_ref, run_ref):
    c = pl.program_id(0)

    @pl.when(c == 0)
    def _():
        run_ref[...] = jnp.zeros_like(run_ref)

    hf = _rms(x_ref[...], g_ref[...])
    h_ref[...] = hf.astype(BF16)
    logits = jnp.dot(hf, wr_ref[...], preferred_element_type=F32, precision=lax.Precision.HIGHEST)
    lane = lax.broadcasted_iota(jnp.int32, logits.shape, 1)
    logits = jnp.where(lane < N_EXPERTS, logits, -jnp.inf)
    v1 = jnp.max(logits, axis=-1, keepdims=True)
    i1 = jnp.min(jnp.where(logits == v1, lane, LANE), axis=-1, keepdims=True)
    rest = jnp.where(lane == i1, -jnp.inf, logits)
    v2 = jnp.max(rest, axis=-1, keepdims=True)
    i2 = jnp.min(jnp.where(rest == v2, lane, LANE), axis=-1, keepdims=True)
    e2 = jnp.exp(v2 - v1)
    den = 1.0 + e2
    gate_ref[...] = jnp.where(lane == 0, 1.0 / den, jnp.where(lane == 1, e2 / den, 0.0))

    onehot = jnp.where((lane == i1) | (lane == i2), 1.0, 0.0)
    run = run_ref[...]
    rank = run + _dot(tri_ref[...], onehot.astype(BF16))
    r1 = jnp.sum(jnp.where(lane == i1, rank, 0.0), axis=-1, keepdims=True).astype(jnp.int32)
    r2 = jnp.sum(jnp.where(lane == i2, rank, 0.0), axis=-1, keepdims=True).astype(jnp.int32)
    route_ref[...] = jnp.where(lane == 0, i1, jnp.where(lane == 1, i2,
                               jnp.where(lane == 2, r1, jnp.where(lane == 3, r2, 0))))
    before_ref[0] = run
    run = run + jnp.sum(onehot, axis=0, keepdims=True)
    run_ref[...] = run
    total_ref[...] = run


def _router(x2, g, w_router):
    T, D = x2.shape
    C = T // MOE_CHUNK
    tri = jnp.asarray(np.tril(np.ones((MOE_CHUNK, MOE_CHUNK), np.float32), -1), BF16)
    row = lambda n: pl.BlockSpec((MOE_CHUNK, n), lambda c: (c, 0))
    full = lambda a: pl.BlockSpec(a.shape, lambda c: (0, 0))
    return pl.pallas_call(
        _router_kernel,
        grid=(C,),
        in_specs=[row(D), full(g), full(w_router), full(tri)],
        out_specs=(row(D), row(LANE), row(LANE),
                   pl.BlockSpec((1, 1, LANE), lambda c: (c, 0, 0)),
                   pl.BlockSpec((1, LANE), lambda c: (0, 0))),
        out_shape=(jax.ShapeDtypeStruct((T, D), BF16),
                   jax.ShapeDtypeStruct((T, LANE), jnp.int32),
                   jax.ShapeDtypeStruct((T, LANE), F32),
                   jax.ShapeDtypeStruct((C, 1, LANE), F32),
                   jax.ShapeDtypeStruct((1, LANE), F32)),
        scratch_shapes=[pltpu.VMEM((1, LANE), F32)],
        compiler_params=_params(("arbitrary",)),
        name="moe_router",
    )(x2, g, w_router, tri)


def _dispatch_kernel(a_ref, h_ref, tok_ref, gate_ref, zlo_ref, zhi_ref, zglo_ref, zghi_ref,
                     lo_ref, hi_ref, glo_ref, ghi_ref):
    del zlo_ref, zhi_ref, zglo_ref, zghi_ref
    e = pl.program_id(0)
    c = pl.program_id(1)
    idx = e * pl.num_programs(1) + c
    a, has_lo, has_hi = _tile_meta(a_ref[idx])
    first = (c == 0) | (a != _tile_meta(a_ref[jnp.maximum(idx - 1, 0)])[0])

    @pl.when(first)
    def _():
        lo_ref[...] = jnp.zeros_like(lo_ref)
        hi_ref[...] = jnp.zeros_like(hi_ref)
        glo_ref[...] = jnp.zeros_like(glo_ref)
        ghi_ref[...] = jnp.zeros_like(ghi_ref)

    def scatter(tile, x_ref, g_ref):
        tok = tok_ref[0]
        gts = gate_ref[0]
        rows = tile * MOE_CHUNK + lax.broadcasted_iota(jnp.int32, (MOE_CHUNK, 1), 0)
        c1 = (tok[0:1] == rows) & (tok[2:3] == e)
        c2 = (tok[1:2] == rows) & (tok[3:4] == e)
        x_ref[...] += _dot(jnp.where(c1 | c2, 1.0, 0.0).astype(BF16), h_ref[...]).astype(x_ref.dtype)
        g_ref[...] += jnp.sum(jnp.where(c1, gts[0:1], 0.0) + jnp.where(c2, gts[1:2], 0.0), axis=1, keepdims=True)

    @pl.when(has_lo)
    def _():
        scatter(a, lo_ref, glo_ref)

    @pl.when(has_hi)
    def _():
        scatter(a + 1, hi_ref, ghi_ref)


def _dispatch(a_ec, h, tok_rows, gate_rows, n_slots):
    T, D = h.shape
    C = T // MOE_CHUNK
    zx = jnp.zeros((n_slots, D), BF16)
    zg = jnp.zeros((n_slots, 1), F32)
    any_spec = pl.BlockSpec(memory_space=pl.ANY)
    lo = lambda n: pl.BlockSpec((MOE_CHUNK, n), lambda e, c, a: (a[e * C + c] >> 2, 0))
    hi = lambda n: pl.BlockSpec((MOE_CHUNK, n), lambda e, c, a: ((a[e * C + c] >> 2) + 1, 0))
    return pl.pallas_call(
        _dispatch_kernel,
        grid_spec=pltpu.PrefetchScalarGridSpec(
            num_scalar_prefetch=1,
            grid=(N_EXPERTS, C),
            in_specs=[pl.BlockSpec((MOE_CHUNK, D), lambda e, c, a: (c, 0)),
                      pl.BlockSpec((1, 8, MOE_CHUNK), lambda e, c, a: (c, 0, 0)),
                      pl.BlockSpec((1, 8, MOE_CHUNK), lambda e, c, a: (c, 0, 0)),
                      any_spec, any_spec, any_spec, any_spec],
            out_specs=(lo(D), hi(D), lo(1), hi(1))),
        out_shape=(jax.ShapeDtypeStruct(zx.shape, BF16), jax.ShapeDtypeStruct(zx.shape, BF16),
                   jax.ShapeDtypeStruct(zg.shape, F32), jax.ShapeDtypeStruct(zg.shape, F32)),
        input_output_aliases={4: 0, 5: 1, 6: 2, 7: 3},
        compiler_params=_params(("arbitrary", "arbitrary")),
        name="moe_dispatch",
    )(a_ec, h, tok_rows, gate_rows, zx, zx, zg, zg)


def _expert_ffn_kernel(te_ref, nu_ref, lo_ref, hi_ref, glo_ref, ghi_ref, wg_ref, wu_ref, wd_ref, o_ref, acc_ref):
    i = pl.program_id(0)
    f = pl.program_id(1)
    used = i < nu_ref[0]

    @pl.when(used & (f == 0))
    def _():
        acc_ref[...] = jnp.zeros_like(acc_ref)

    @pl.when(used)
    def _():
        x = lo_ref[...] + hi_ref[...]
        act = _silu(_dot(x, wg_ref[0])) * _dot(x, wu_ref[0])
        acc_ref[...] += _dot(act.astype(BF16), wd_ref[0])

    last = f == pl.num_programs(1) - 1

    @pl.when(used & last)
    def _():
        o_ref[...] = (acc_ref[...] * (glo_ref[...] + ghi_ref[...])).astype(o_ref.dtype)

    @pl.when(jnp.logical_not(used) & last)
    def _():
        o_ref[...] = jnp.zeros_like(o_ref)


def _expert_ffn(tile_expert, n_used, xs_lo, xs_hi, gs_lo, gs_hi, wg, wu, wd, tf=512):
    N, D = xs_lo.shape
    F = wg.shape[2]
    nf = F // tf
    tr = MOE_ROW_TILE
    fidx = lambda i, f, te, nu: jnp.where(i < nu[0], f, nf - 1)
    row = lambda n: pl.BlockSpec((tr, n), lambda i, f, te, nu: (i, 0))
    return pl.pallas_call(
        _expert_ffn_kernel,
        grid_spec=pltpu.PrefetchScalarGridSpec(
            num_scalar_prefetch=2,
            grid=(N // tr, nf),
            in_specs=[row(D), row(D), row(1), row(1),
                      pl.BlockSpec((1, D, tf), lambda i, f, te, nu: (te[i], 0, fidx(i, f, te, nu))),
                      pl.BlockSpec((1, D, tf), lambda i, f, te, nu: (te[i], 0, fidx(i, f, te, nu))),
                      pl.BlockSpec((1, tf, D), lambda i, f, te, nu: (te[i], fidx(i, f, te, nu), 0))],
            out_specs=row(D),
            scratch_shapes=[pltpu.VMEM((tr, D), F32)]),
        out_shape=jax.ShapeDtypeStruct((N, D), BF16),
        compiler_params=_params(("arbitrary", "arbitrary")),
        name="moe_expert_ffn",
    )(tile_expert, n_used, xs_lo, xs_hi, gs_lo, gs_hi, wg, wu, wd)


def _combine_kernel(a_ref, x_ref, ylo_ref, yhi_ref, tok_ref, gf_ref, o_ref, acc_ref):
    c = pl.program_id(0)
    e = pl.program_id(1)

    @pl.when(e == 0)
    def _():
        acc_ref[...] = jnp.zeros_like(acc_ref)

    a, has_lo, has_hi = _tile_meta(a_ref[c * pl.num_programs(1) + e])

    def gather(tile, y_ref):
        tok = tok_ref[...]
        cols = tile * MOE_CHUNK + lax.broadcasted_iota(jnp.int32, (1, MOE_CHUNK), 1)
        pick = ((tok[:, 0:1] == cols) & (tok[:, 2:3] == e)) | ((tok[:, 1:2] == cols) & (tok[:, 3:4] == e))
        acc_ref[...] += _dot(jnp.where(pick, 1.0, 0.0).astype(BF16), y_ref[...])

    @pl.when(has_lo)
    def _():
        gather(a, ylo_ref)

    @pl.when(has_hi)
    def _():
        gather(a + 1, yhi_ref)

    @pl.when(e == pl.num_programs(1) - 1)
    def _():
        o_ref[...] = _rms(x_ref[...] + acc_ref[...], gf_ref[...])


def _combine(a_ce, x2, ys, tok_cols, g_final):
    T, D = x2.shape
    C = T // MOE_CHUNK
    E = N_EXPERTS
    return pl.pallas_call(
        _combine_kernel,
        grid_spec=pltpu.PrefetchScalarGridSpec(
            num_scalar_prefetch=1,
            grid=(C, E),
            in_specs=[pl.BlockSpec((MOE_CHUNK, D), lambda c, e, a: (c, 0)),
                      pl.BlockSpec((MOE_CHUNK, D), lambda c, e, a: (a[c * E + e] >> 2, 0)),
                      pl.BlockSpec((MOE_CHUNK, D), lambda c, e, a: ((a[c * E + e] >> 2) + 1, 0)),
                      pl.BlockSpec((MOE_CHUNK, LANE), lambda c, e, a: (c, 0)),
                      pl.BlockSpec((1, D), lambda c, e, a: (0, 0))],
            out_specs=pl.BlockSpec((MOE_CHUNK, D), lambda c, e, a: (c, 0)),
            scratch_shapes=[pltpu.VMEM((MOE_CHUNK, D), F32)]),
        out_shape=jax.ShapeDtypeStruct((T, D), F32),
        compiler_params=_params(("arbitrary", "arbitrary")),
        name="moe_combine_final_norm",
    )(a_ce, x2, ys, ys, tok_cols, g_final)


def _rope_tables(S):
    inv = ROPE_THETA ** (-jnp.arange(ROPE_HALF, dtype=F32) / ROPE_HALF)
    ang = jnp.arange(S, dtype=F32)[:, None] * inv[None, :]
    cos, sin = jnp.cos(ang), jnp.sin(ang)
    pad = HEAD_DIM - ROPE_DIM
    cos_h = jnp.concatenate([cos, cos, jnp.ones((S, pad), F32)], axis=-1)
    sin_h = jnp.concatenate([-sin, sin, jnp.zeros((S, pad), F32)], axis=-1)
    reps = LANE // HEAD_DIM
    return jnp.tile(cos_h, (1, reps)), jnp.tile(sin_h, (1, reps))


def _pack_w_in(w):
    per_group = GROUP * N_BRANCH
    gates = w[:, GATE_OFF:GATE_OFF + N_KV_HEADS * per_group].reshape(-1, N_KV_HEADS, per_group)
    gates = jnp.pad(gates, ((0, 0), (0, LANE // GATE_ROWS - N_KV_HEADS), (0, GATE_ROWS - per_group)))
    gates = gates.reshape(-1, LANE)
    return jnp.concatenate([w[:, :GATE_OFF], gates, w[:, GATE_OFF + N_KV_HEADS * per_group:]], axis=1).astype(BF16)


def _mixer_layer(x, norm_g, w_in, pe_k, w1_k, b1_k, w2_k, pe_v, w1_v, b1_v, w2_v,
                 w_pool, pool_scale, w_out, cos_t, sin_t):
    B, S, D = x.shape
    qt, qrt, kcr, vcr, ks, vst, kw, vwt, gates, u = _mixer_proj(
        x, norm_g.reshape(1, D), _pack_w_in(w_in), cos_t, sin_t)
    half = CMP_LEN // 2
    cw = half * HEAD_DIM
    pad_w2 = lambda w2: jnp.pad(w2, ((0, 0), (0, LANE - HEAD_DIM))).astype(BF16)
    kc, vct = _compress(
        kcr, vcr,
        pe_k.reshape(2, cw), w1_k.reshape(2, cw, CMP_HIDDEN).astype(BF16), b1_k.reshape(1, -1), pad_w2(w2_k),
        pe_v.reshape(2, cw), w1_v.reshape(2, cw, CMP_HIDDEN).astype(BF16), b1_v.reshape(1, -1), pad_w2(w2_v))
    a = _nsa_attention(qt, qrt, kc, vct, ks, vst, kw, vwt, gates)
    p = _pool_mixer(u, w_pool.astype(BF16), pool_scale.reshape(1, -1))
    w_out = w_out.astype(BF16)
    out = _out_proj(x.reshape(B * S, D), a.reshape(B * S, NSA_WIDTH), p.reshape(B * S, POOL_WIDTH),
                    w_out[:NSA_WIDTH], w_out[NSA_WIDTH:])
    return out.reshape(B, S, D)


def _moe_layer(x2, g_ffn, router, wg, wu, wd, g_final):
    T, D = x2.shape
    E = N_EXPERTS
    C = T // MOE_CHUNK
    i32 = jnp.int32
    w_router = jnp.pad(router, ((0, 0), (0, LANE - E)))
    h, route, gates, before, total = _router(x2, g_ffn.reshape(1, D), w_router)

    counts = total[0, :E].astype(i32)
    tiles = (counts + MOE_ROW_TILE - 1) // MOE_ROW_TILE
    ends = jnp.cumsum(tiles)
    off = (ends - tiles) * MOE_ROW_TILE
    e1, e2, r1, r2 = route[:, 0], route[:, 1], route[:, 2], route[:, 3]
    slot1 = off[e1] + r1
    slot2 = off[e2] + r2
    first = off[None, :] + before[:, 0, :E].astype(i32)
    after = jnp.concatenate([before[1:, 0, :E], total[:, :E]], axis=0).astype(i32)
    last = off[None, :] + after - 1
    lo_tile = first // MOE_CHUNK
    has_lo = (last >= first).astype(i32)
    has_hi = ((last >= first) & (last // MOE_CHUNK > lo_tile)).astype(i32)
    tile_meta = lo_tile * 4 + has_hi * 2 + has_lo
    n_row_tiles = (2 * T) // MOE_ROW_TILE + E + 1 + MOE_CHUNK // MOE_ROW_TILE
    n_used = ends[-1:]
    tile_ids = jnp.minimum(jnp.arange(n_row_tiles, dtype=i32), n_used[0] - 1)
    tile_expert = jnp.minimum(jnp.sum(ends[None, :] <= tile_ids[:, None], axis=1), E - 1).astype(i32)

    zero = jnp.zeros_like(slot1)
    tok = jnp.stack([slot1, slot2, e1, e2, zero, zero, zero, zero])
    tok_rows = tok.reshape(8, C, MOE_CHUNK).transpose(1, 0, 2)
    gz = jnp.zeros((T,), F32)
    gate_rows = jnp.stack([gates[:, 0], gates[:, 1], gz, gz, gz, gz, gz, gz]).reshape(8, C, MOE_CHUNK).transpose(1, 0, 2)
    tok_cols = jnp.pad(tok[:4].T, ((0, 0), (0, LANE - 4)))

    xs_lo, xs_hi, gs_lo, gs_hi = _dispatch(tile_meta.T.reshape(-1).astype(i32), h, tok_rows, gate_rows,
                                           n_row_tiles * MOE_ROW_TILE)
    ys = _expert_ffn(tile_expert, n_used.astype(i32), xs_lo, xs_hi, gs_lo, gs_hi,
                     wg.astype(BF16), wu.astype(BF16), wd.astype(BF16))
    return _combine(tile_meta.reshape(-1).astype(i32), x2, ys, tok_cols, g_final.reshape(1, D))


def kernel(x, mem, norm_mix, w_in, cmp_pe_k, cmp_w1_k, cmp_b1_k, cmp_w2_k, cmp_pe_v, cmp_w1_v, cmp_b1_v, cmp_w2_v, w_pool, pool_scale, w_out, norm_x, norm_mem, wq_x, wk_x, wv_x, wo_x, norm_ffn, ffn_wg, ffn_wu, ffn_wd, moe_router, moe_wg, moe_wu, moe_wd, norm_final):
    B, S, D = x.shape
    depth = norm_mix.shape[0]
    assert depth == 2, "the final RMSNorm is fused into the expert layer, which must be the last one"
    cos_t, sin_t = _rope_tables(S)
    for layer in range(depth):
        x = _mixer_layer(x, norm_mix[layer], w_in[layer],
                         cmp_pe_k[layer], cmp_w1_k[layer], cmp_b1_k[layer], cmp_w2_k[layer],
                         cmp_pe_v[layer], cmp_w1_v[layer], cmp_b1_v[layer], cmp_w2_v[layer],
                         w_pool[layer], pool_scale[layer], w_out[layer], cos_t, sin_t)
        wkv = jnp.concatenate([wk_x[layer], wv_x[layer]], axis=1).astype(BF16)
        kx, vx = _xattn_kv(mem, norm_mem[layer].reshape(1, D), wkv)
        x = _xattn(x, norm_x[layer].reshape(1, D), wq_x[layer].astype(BF16), kx, vx, wo_x[layer].astype(BF16))
        x2 = x.reshape(B * S, D)
        j = layer // 2
        if layer % 2 == 0:
            x2 = _ffn(x2, norm_ffn[layer].reshape(1, D), ffn_wg[j].astype(BF16), ffn_wu[j].astype(BF16),
                      ffn_wd[j].astype(BF16))
        else:
            x2 = _moe_layer(x2, norm_ffn[layer], moe_router[j], moe_wg[j], moe_wu[j], moe_wd[j], norm_final)
        x = x2.reshape(B, S, D)
    return x
```

```python
import functools
import math

import numpy as np
import jax
import jax.numpy as jnp
from jax import lax
from jax.experimental import pallas as pl
from jax.experimental.pallas import tpu as pltpu

F32 = jnp.float32
BF16 = jnp.bfloat16

D_MODEL = 1024
HEAD_DIM = 64
N_HEADS = 8
N_KV_HEADS = 2
GROUP = N_HEADS // N_KV_HEADS
NSA_WIDTH = N_HEADS * HEAD_DIM
KV_WIDTH = N_KV_HEADS * HEAD_DIM
N_BRANCH = 3
GATE_ROWS = 16
POOL_WIDTH = 512
POOL_WINDOWS = (2, 4, 8, 16)
POOL_GROUP = 128
POOL_HALO = 16
ROPE_DIM = 16
ROPE_HALF = 8
ROPE_THETA = 500000.0
CMP_LEN = 32
CMP_STRIDE = 16
CMP_HIDDEN = 256
SEL_BLOCK = 64
SEL_TOP_N = 16
N_LOCAL_BLOCKS = 2
WINDOW = 512
X_HEADS = 4
X_HEAD_DIM = 256
N_EXPERTS = 8
EPS = 1e-6

LANE = 128
IN_PACKED = NSA_WIDTH + 6 * KV_WIDTH + LANE + POOL_WIDTH
GATE_OFF = NSA_WIDTH + 6 * KV_WIDTH
POOL_OFF = GATE_OFF + LANE

NEG = -1e30
TINY = float(np.finfo(np.float32).tiny)
VMEM_LIMIT = 56 * 1024 * 1024


def _dot(a, b):
    return jnp.dot(a, b, preferred_element_type=F32)


def _dot_nt(a, b):
    return lax.dot_general(a, b, (((1,), (1,)), ((), ())), preferred_element_type=F32)


def _rms(x, g):
    y = x * lax.rsqrt(jnp.mean(x * x, axis=-1, keepdims=True) + EPS)
    return y * g


def _params(sem, limit=VMEM_LIMIT):
    return pltpu.CompilerParams(dimension_semantics=sem, vmem_limit_bytes=limit)


def _mixer_proj_kernel(x_ref, g_ref, w_ref, cos_ref, sin_ref,
                       qt_ref, qrt_ref, kcr_ref, vcr_ref, ks_ref, vst_ref, kw_ref, vwt_ref,
                       gate_ref, u_ref):
    h = _rms(x_ref[0], g_ref[...]).astype(BF16)
    z = _dot(h, w_ref[...])
    tm = z.shape[0]
    cos = cos_ref[...]
    sin = sin_ref[...]
    lane = lax.broadcasted_iota(jnp.int32, (tm, LANE), 1)
    first = (lane & (HEAD_DIM - 1)) < ROPE_HALF
    scale = HEAD_DIM ** -0.5

    def rope(xs):
        partner = jnp.where(first, pltpu.roll(xs, LANE - ROPE_HALF, 1), pltpu.roll(xs, ROPE_HALF, 1))
        return xs * cos + partner * sin

    for s in range(NSA_WIDTH // LANE):
        xs = z[:, s * LANE:(s + 1) * LANE]
        for src, ref in ((xs, qt_ref), (rope(xs), qrt_ref)):
            t = (src * scale).T.astype(BF16)
            ref[0, 2 * s] = t[:HEAD_DIM]
            ref[0, 2 * s + 1] = t[HEAD_DIM:]

    def kv_slab(i):
        return z[:, NSA_WIDTH + i * KV_WIDTH:NSA_WIDTH + (i + 1) * KV_WIDTH]

    for slab, ref in ((kv_slab(0), kcr_ref), (kv_slab(1), vcr_ref),
                      (rope(kv_slab(2)), ks_ref), (rope(kv_slab(4)), kw_ref)):
        for gg in range(N_KV_HEADS):
            ref[0, gg] = slab[:, gg * HEAD_DIM:(gg + 1) * HEAD_DIM].astype(ref.dtype)

    for slab, ref in ((kv_slab(3), vst_ref), (kv_slab(5), vwt_ref)):
        t = slab.T.astype(BF16)
        for gg in range(N_KV_HEADS):
            for k in range(tm // LANE):
                ref[0, gg, k] = t[gg * HEAD_DIM:(gg + 1) * HEAD_DIM, k * LANE:(k + 1) * LANE]

    sig_t = jax.nn.sigmoid(z[:, GATE_OFF:GATE_OFF + LANE]).T
    for gg in range(N_KV_HEADS):
        gate_ref[0, gg] = sig_t[gg * GATE_ROWS:(gg + 1) * GATE_ROWS]
    u_ref[0] = z[:, POOL_OFF:POOL_OFF + POOL_WIDTH]


def _mixer_proj(x, g, w_packed, cos_t, sin_t, tm=512):
    B, S, D = x.shape
    G = N_KV_HEADS
    sd = jax.ShapeDtypeStruct
    out_shape = (sd((B, N_HEADS, HEAD_DIM, S), BF16), sd((B, N_HEADS, HEAD_DIM, S), BF16),
                 sd((B, G, S, HEAD_DIM), F32), sd((B, G, S, HEAD_DIM), F32),
                 sd((B, G, S, HEAD_DIM), BF16), sd((B, G, S // LANE, HEAD_DIM, LANE), BF16),
                 sd((B, G, S, HEAD_DIM), BF16), sd((B, G, S // LANE, HEAD_DIM, LANE), BF16),
                 sd((B, G, GATE_ROWS, S), F32),
                 sd((B, S, POOL_WIDTH), F32))
    qspec = pl.BlockSpec((1, N_HEADS, HEAD_DIM, tm), lambda b, i: (b, 0, 0, i))
    kspec = pl.BlockSpec((1, G, tm, HEAD_DIM), lambda b, i: (b, 0, i, 0))
    vspec = pl.BlockSpec((1, G, tm // LANE, HEAD_DIM, LANE), lambda b, i: (b, 0, i, 0, 0))
    out_specs = (qspec, qspec, kspec, kspec, kspec, vspec, kspec, vspec,
                 pl.BlockSpec((1, G, GATE_ROWS, tm), lambda b, i: (b, 0, 0, i)),
                 pl.BlockSpec((1, tm, POOL_WIDTH), lambda b, i: (b, i, 0)))
    return pl.pallas_call(
        _mixer_proj_kernel,
        grid=(B, S // tm),
        in_specs=[pl.BlockSpec((1, tm, D), lambda b, i: (b, i, 0)),
                  pl.BlockSpec((1, D), lambda b, i: (0, 0)),
                  pl.BlockSpec((D, IN_PACKED), lambda b, i: (0, 0)),
                  pl.BlockSpec((tm, LANE), lambda b, i: (i, 0)),
                  pl.BlockSpec((tm, LANE), lambda b, i: (i, 0))],
        out_specs=out_specs,
        out_shape=out_shape,
        compiler_params=_params(("parallel", "parallel")),
        name="mixer_proj",
    )(x, g, w_packed, cos_t, sin_t)


def _gelu_tanh(x):
    return 0.5 * x * (1.0 + jnp.tanh(math.sqrt(2.0 / math.pi) * (x + 0.044715 * (x * x * x))))


def _compress_kernel(kc_ref, vc_ref, pek_ref, w1k_ref, b1k_ref, w2k_ref,
                     pev_ref, w1v_ref, b1v_ref, w2v_ref, ko_ref, vto_ref):
    def comp(x_ref, pe_ref, w1_ref, b1_ref, w2_ref):
        x = x_ref[0, 0]
        n = x.shape[0]
        a = _dot((x + pe_ref[0:1, :]).astype(BF16), w1_ref[0])
        b = _dot((x + pe_ref[1:2, :]).astype(BF16), w1_ref[1])
        pre = a + pltpu.roll(b, n - 1, 0) + b1_ref[...]
        return _dot(_gelu_tanh(pre).astype(BF16), w2_ref[...])

    ko_ref[0, 0] = comp(kc_ref, pek_ref, w1k_ref, b1k_ref, w2k_ref)[:, :HEAD_DIM].astype(ko_ref.dtype)
    vto_ref[0, 0] = comp(vc_ref, pev_ref, w1v_ref, b1v_ref, w2v_ref).T[:HEAD_DIM].astype(vto_ref.dtype)


def _compress(kcr, vcr, pek, w1k, b1k, w2k, pev, w1v, b1v, w2v):
    B, G, S, _ = kcr.shape
    n = S // CMP_STRIDE
    cw = CMP_STRIDE * HEAD_DIM
    kc = kcr.reshape(B, G, n, cw)
    vc = vcr.reshape(B, G, n, cw)
    xspec = pl.BlockSpec((1, 1, n, cw), lambda b, g: (b, g, 0, 0))
    full = lambda a: pl.BlockSpec(a.shape, lambda b, g: (0,) * a.ndim)
    return pl.pallas_call(
        _compress_kernel,
        grid=(B, G),
        in_specs=[xspec, xspec, full(pek), full(w1k), full(b1k), full(w2k),
                  full(pev), full(w1v), full(b1v), full(w2v)],
        out_specs=(pl.BlockSpec((1, 1, n, HEAD_DIM), lambda b, g: (b, g, 0, 0)),
                   pl.BlockSpec((1, 1, HEAD_DIM, n), lambda b, g: (b, g, 0, 0))),
        out_shape=(jax.ShapeDtypeStruct((B, G, n, HEAD_DIM), BF16),
                   jax.ShapeDtypeStruct((B, G, HEAD_DIM, n), BF16)),
        compiler_params=_params(("parallel", "parallel")),
        name="compress_kv",
    )(kc, vc, pek, w1k, b1k, w2k, pev, w1v, b1v, w2v)


def _nsa_kernel(qt_ref, qrt_ref, kc_ref, vct_ref, ks_ref, vst_ref, kw_ref, vwt_ref, gate_ref,
                covt_ref, o_ref, selbias_ref, *, tq, tk):
    qt = pl.program_id(2)
    q0 = qt * tq
    n_cmp = kc_ref.shape[2]
    n_blk = covt_ref.shape[0]
    heads = lambda ref: jnp.concatenate([ref[0, hh] for hh in range(GROUP)], axis=1)
    q_t = heads(qt_ref)
    qr_t = heads(qrt_ref)
    t_lane = q0 + lax.broadcasted_iota(jnp.int32, (1, tq), 1)
    per_head = lambda a: [a[:, hh * tq:(hh + 1) * tq] for hh in range(GROUP)]
    all_heads = lambda a: jnp.concatenate([a] * GROUP, axis=1)

    sc = _dot(kc_ref[0, 0], q_t)
    cmp_end = lax.broadcasted_iota(jnp.int32, (n_cmp, 1), 0) * CMP_STRIDE + (CMP_LEN - 1)
    valid = cmp_end <= t_lane
    pcs = []
    for s_h in per_head(sc):
        s_h = jnp.where(valid, s_h, NEG)
        e = jnp.where(valid, jnp.exp(s_h - jnp.max(s_h, axis=0, keepdims=True)), 0.0)
        pcs.append(e * (1.0 / jnp.maximum(jnp.sum(e, axis=0, keepdims=True), TINY)))
    oc_t = _dot(vct_ref[0, 0], jnp.concatenate(pcs, axis=1).astype(BF16))

    psum = pcs[0] + pcs[1] + pcs[2] + pcs[3]
    p_hi = psum.astype(BF16)
    r1 = psum - p_hi.astype(F32)
    p_mid = r1.astype(BF16)
    p_lo = (r1 - p_mid.astype(F32)).astype(BF16)
    covt = covt_ref[...]
    imp_t = _dot(covt, p_hi) + _dot(covt, p_mid) + _dot(covt, p_lo)

    jb = lax.broadcasted_iota(jnp.int32, (n_blk, tq), 0)
    tb = (q0 + lax.broadcasted_iota(jnp.int32, (n_blk, tq), 1)) // SEL_BLOCK
    dist = tb - jb
    forced = (jb == 0) | ((dist >= 0) & (dist < N_LOCAL_BLOCKS))
    score = jnp.where(jb > tb, -jnp.inf, jnp.where(forced, jnp.inf, imp_t))
    rank = jnp.zeros((n_blk, tq), jnp.int32)
    for jp in range(n_blk):
        rowv = score[jp:jp + 1, :]
        beats = (rowv > score) | ((rowv == score) & (jb > jp))
        rank = rank + jnp.where(beats, 1, 0)
    selbias_ref[...] = jnp.where(rank < min(SEL_TOP_N, n_blk), 0.0, NEG)

    bpc = tk // SEL_BLOCK
    vpc = tk // LANE

    def sel_body(c, carry):
        m_i, l_i, acc = carry
        start = pl.multiple_of(c * tk, tk)
        k_c = ks_ref[0, 0, pl.ds(start, tk), :]
        v_t = jnp.concatenate([vst_ref[0, 0, c * vpc + k] for k in range(vpc)], axis=1)
        rows = [jnp.broadcast_to(selbias_ref[pl.ds(c * bpc + j, 1), :], (SEL_BLOCK, tq)) for j in range(bpc)]
        kpos = start + lax.broadcasted_iota(jnp.int32, (tk, 1), 0)
        bias = jnp.where(kpos <= t_lane, jnp.concatenate(rows, axis=0), NEG)
        s = _dot(k_c, qr_t) + all_heads(bias)
        m_new = jnp.maximum(m_i, jnp.max(s, axis=0, keepdims=True))
        alpha = jnp.exp(m_i - m_new)
        p = jnp.exp(s - m_new)
        l_new = alpha * l_i + jnp.sum(p, axis=0, keepdims=True)
        acc_new = alpha * acc + _dot(v_t, p.astype(BF16))
        return m_new, l_new, acc_new

    R = GROUP * tq
    init = (jnp.full((1, R), NEG, F32), jnp.zeros((1, R), F32), jnp.zeros((HEAD_DIM, R), F32))
    _, l_s, acc_s = lax.fori_loop(0, (q0 + tq + tk - 1) // tk, sel_body, init)
    os_t = acc_s * (1.0 / jnp.maximum(l_s, TINY))

    wk = WINDOW + tq
    w0 = pl.multiple_of(jnp.maximum(q0 - WINDOW, 0), LANE)
    k_w = kw_ref[0, 0, pl.ds(w0, wk), :]
    v_t = jnp.concatenate([vwt_ref[0, 0, w0 // LANE + k] for k in range(wk // LANE)], axis=1)
    diff = t_lane - (w0 + lax.broadcasted_iota(jnp.int32, (wk, 1), 0))
    bias = jnp.where((diff >= 0) & (diff < WINDOW), 0.0, NEG)
    s = _dot(k_w, qr_t) + all_heads(bias)
    p = jnp.exp(s - jnp.max(s, axis=0, keepdims=True))
    ow_t = _dot(v_t, p.astype(BF16)) * (1.0 / jnp.sum(p, axis=0, keepdims=True))

    gt = gate_ref[0, 0]
    outs = []
    for hh, (a, b, c) in enumerate(zip(per_head(oc_t), per_head(os_t), per_head(ow_t))):
        r = hh * N_BRANCH
        outs.append(gt[r:r + 1] * a + gt[r + 1:r + 2] * b + gt[r + 2:r + 3] * c)
    o_ref[0] = jnp.concatenate(outs, axis=0).T.astype(o_ref.dtype)


def _cover_table(S):
    n_cmp = S // CMP_STRIDE
    n_blk = S // SEL_BLOCK
    cs = np.arange(n_cmp) * CMP_STRIDE
    ss = np.arange(n_blk) * SEL_BLOCK
    cover_t = ((cs[None, :] < ss[:, None] + SEL_BLOCK) & (cs[None, :] + CMP_LEN > ss[:, None]))
    cover_t[:, n_cmp - 1] = False
    return jnp.asarray(cover_t, BF16)


def _nsa_attention(qt, qrt, kc, vct, ks, vst, kw, vwt, gates, tq=128, tk=512):
    B, _, _, S = qt.shape
    tk = min(tk, S)
    assert tq == LANE and S % tk == 0 and S >= WINDOW + tq
    covt = _cover_table(S)
    n_cmp = kc.shape[2]
    n_blk = covt.shape[0]
    qspec = pl.BlockSpec((1, GROUP, HEAD_DIM, tq), lambda b, g, i: (b, g, 0, i))
    kspec = lambda a: pl.BlockSpec((1, 1) + a.shape[2:], lambda b, g, i: (b, g, 0, 0))
    vspec = pl.BlockSpec((1, 1, S // LANE, HEAD_DIM, LANE), lambda b, g, i: (b, g, 0, 0, 0))
    return pl.pallas_call(
        functools.partial(_nsa_kernel, tq=tq, tk=tk),
        grid=(B, N_KV_HEADS, S // tq),
        in_specs=[qspec, qspec,
                  pl.BlockSpec((1, 1, n_cmp, HEAD_DIM), lambda b, g, i: (b, g, 0, 0)),
                  pl.BlockSpec((1, 1, HEAD_DIM, n_cmp), lambda b, g, i: (b, g, 0, 0)),
                  kspec(ks), vspec, kspec(kw), vspec,
                  pl.BlockSpec((1, 1, GATE_ROWS, tq), lambda b, g, i: (b, g, 0, i)),
                  pl.BlockSpec(covt.shape, lambda b, g, i: (0, 0))],
        out_specs=pl.BlockSpec((1, tq, GROUP * HEAD_DIM), lambda b, g, i: (b, i, g)),
        out_shape=jax.ShapeDtypeStruct((B, S, NSA_WIDTH), BF16),
        scratch_shapes=[pltpu.VMEM((n_blk, tq), F32)],
        compiler_params=_params(("parallel", "parallel", "parallel")),
        name="nsa_attention",
    )(qt, qrt, kc, vct, ks, vst, kw, vwt, gates, covt)


def _pool_kernel(u_ref, w_ref, scale_ref, o_ref, buf):
    i = pl.program_id(1)
    tm = u_ref.shape[1]

    @pl.when(i == 0)
    def _():
        buf[0:POOL_HALO, :] = jnp.zeros((POOL_HALO, POOL_WIDTH), F32)

    @pl.when(i > 0)
    def _():
        buf[0:POOL_HALO, :] = buf[tm:tm + POOL_HALO, :]

    buf[POOL_HALO:POOL_HALO + tm, :] = u_ref[0]
    t1 = i * tm + lax.broadcasted_iota(jnp.int32, (tm, 1), 0) + 1
    for gi, w in enumerate(POOL_WINDOWS):
        cols = slice(gi * POOL_GROUP, (gi + 1) * POOL_GROUP)
        cur = buf[POOL_HALO:POOL_HALO + tm, cols]
        tot = cur
        for k in range(1, w):
            tot = tot + buf[POOL_HALO - k:POOL_HALO - k + tm, cols]
        cnt = jnp.minimum(t1, w).astype(F32)
        d = tot / cnt - cur
        y = _dot(d.astype(BF16), w_ref[gi])
        o_ref[0, :, cols] = (y * scale_ref[:, cols]).astype(o_ref.dtype)


def _pool_mixer(u, w_pool, pool_scale, tm=512):
    B, S, _ = u.shape
    return pl.pallas_call(
        _pool_kernel,
        grid=(B, S // tm),
        in_specs=[pl.BlockSpec((1, tm, POOL_WIDTH), lambda b, i: (b, i, 0)),
                  pl.BlockSpec(w_pool.shape, lambda b, i: (0, 0, 0)),
                  pl.BlockSpec((1, POOL_WIDTH), lambda b, i: (0, 0))],
        out_specs=pl.BlockSpec((1, tm, POOL_WIDTH), lambda b, i: (b, i, 0)),
        out_shape=jax.ShapeDtypeStruct((B, S, POOL_WIDTH), BF16),
        scratch_shapes=[pltpu.VMEM((tm + POOL_HALO, POOL_WIDTH), F32)],
        compiler_params=_params(("parallel", "arbitrary")),
        name="pool_mixer",
    )(u, w_pool, pool_scale)


def _out_proj_kernel(x_ref, a_ref, p_ref, wa_ref, wp_ref, o_ref):
    o_ref[...] = x_ref[...] + _dot(a_ref[...], wa_ref[...]) + _dot(p_ref[...], wp_ref[...])


def _out_proj(x2, a2, p2, wa, wp, tm=512):
    T, D = x2.shape
    row = lambda n: pl.BlockSpec((tm, n), lambda i: (i, 0))
    full = lambda a: pl.BlockSpec(a.shape, lambda i: (0, 0))
    return pl.pallas_call(
        _out_proj_kernel,
        grid=(T // tm,),
        in_specs=[row(D), row(a2.shape[1]), row(p2.shape[1]), full(wa), full(wp)],
        out_specs=row(D),
        out_shape=jax.ShapeDtypeStruct((T, D), F32),
        compiler_params=_params(("parallel",)),
        name="mixer_out_proj",
    )(x2, a2, p2, wa, wp)


def _xkv_kernel(m_ref, g_ref, w_ref, k_ref, v_ref):
    h = _rms(m_ref[0], g_ref[...]).astype(BF16)
    kv = _dot(h, w_ref[...])
    d = k_ref.shape[2]
    k_ref[0] = kv[:, :d].astype(k_ref.dtype)
    v_ref[0] = kv[:, d:].astype(v_ref.dtype)


def _xattn_kv(mem, g, wkv):
    B, M, D = mem.shape
    ospec = pl.BlockSpec((1, M, D), lambda b: (b, 0, 0))
    oshape = jax.ShapeDtypeStruct((B, M, D), BF16)
    return pl.pallas_call(
        _xkv_kernel,
        grid=(B,),
        in_specs=[pl.BlockSpec((1, M, D), lambda b: (b, 0, 0)),
                  pl.BlockSpec((1, D), lambda b: (0, 0)),
                  pl.BlockSpec(wkv.shape, lambda b: (0, 0))],
        out_specs=(ospec, ospec),
        out_shape=(oshape, oshape),
        compiler_params=_params(("parallel",)),
        name="xattn_kv",
    )(mem, g, wkv)


def _xattn_kernel(x_ref, g_ref, wq_ref, k_ref, v_ref, wo_ref, o_ref):
    x = x_ref[0]
    h = _rms(x, g_ref[...]).astype(BF16)
    scale = X_HEAD_DIM ** -0.5
    q = (_dot(h, wq_ref[...]) * scale).astype(BF16)
    outs = []
    for hd in range(X_HEADS):
        sl = slice(hd * X_HEAD_DIM, (hd + 1) * X_HEAD_DIM)
        s = _dot_nt(q[:, sl], k_ref[0, :, sl])
        e = jnp.exp(s - jnp.max(s, axis=-1, keepdims=True))
        p = e / jnp.sum(e, axis=-1, keepdims=True)
        outs.append(_dot(p.astype(BF16), v_ref[0, :, sl]).astype(BF16))
    o = jnp.concatenate(outs, axis=-1)
    o_ref[0] = x + _dot(o, wo_ref[...])


def _xattn(x, g, wq, kx, vx, wo, tm=512):
    B, S, D = x.shape
    M = kx.shape[1]
    xspec = pl.BlockSpec((1, tm, D), lambda b, i: (b, i, 0))
    full = lambda a: pl.BlockSpec(a.shape, lambda b, i: (0, 0))
    mspec = pl.BlockSpec((1, M, D), lambda b, i: (b, 0, 0))
    return pl.pallas_call(
        _xattn_kernel,
        grid=(B, S // tm),
        in_specs=[xspec, full(g), full(wq), mspec, mspec, full(wo)],
        out_specs=xspec,
        out_shape=jax.ShapeDtypeStruct((B, S, D), F32),
        compiler_params=_params(("parallel", "parallel")),
        name="xattn",
    )(x, g, wq, kx, vx, wo)


def _silu(x):
    return x * jax.nn.sigmoid(x)


def _ffn_kernel(x_ref, g_ref, wg_ref, wu_ref, wd_ref, o_ref, h_ref, acc_ref):
    f = pl.program_id(1)

    @pl.when(f == 0)
    def _():
        h_ref[...] = _rms(x_ref[...], g_ref[...]).astype(BF16)
        acc_ref[...] = jnp.zeros_like(acc_ref)

    h = h_ref[...]
    act = _silu(_dot(h, wg_ref[...])) * _dot(h, wu_ref[...])
    acc_ref[...] += _dot(act.astype(BF16), wd_ref[...])

    @pl.when(f == pl.num_programs(1) - 1)
    def _():
        o_ref[...] = x_ref[...] + acc_ref[...]


def _ffn(x2, g, wg, wu, wd, tm=512, tf=1408):
    T, D = x2.shape
    F = wg.shape[1]
    return pl.pallas_call(
        _ffn_kernel,
        grid=(T // tm, F // tf),
        in_specs=[pl.BlockSpec((tm, D), lambda i, f: (i, 0)),
                  pl.BlockSpec((1, D), lambda i, f: (0, 0)),
                  pl.BlockSpec((D, tf), lambda i, f: (0, f)),
                  pl.BlockSpec((D, tf), lambda i, f: (0, f)),
                  pl.BlockSpec((tf, D), lambda i, f: (f, 0))],
        out_specs=pl.BlockSpec((tm, D), lambda i, f: (i, 0)),
        out_shape=jax.ShapeDtypeStruct((T, D), F32),
        scratch_shapes=[pltpu.VMEM((tm, D), BF16), pltpu.VMEM((tm, D), F32)],
        compiler_params=_params(("parallel", "arbitrary")),
        name="dense_swiglu",
    )(x2, g, wg, wu, wd)


MOE_CHUNK = 512
MOE_ROW_TILE = 512
assert MOE_ROW_TILE % MOE_CHUNK == 0


def _tile_meta(m):
    return m >> 2, (m & 1) != 0, (m & 2) != 0


def _router_kernel(x_ref, g_ref, wr_ref, tri_ref, h_ref, route_ref, gate_ref, before_ref, total_ref, run_ref):
    c = pl.program_id(0)

    @pl.when(c == 0)
    def _():
        run_ref[...] = jnp.zeros_like(run_ref)

    hf = _rms(x_ref[...], g_ref[...])
    h_ref[...] = hf.astype(BF16)
    logits = jnp.dot(hf, wr_ref[...], preferred_element_type=F32, precision=lax.Precision.HIGHEST)
    lane = lax.broadcasted_iota(jnp.int32, logits.shape, 1)
    logits = jnp.where(lane < N_EXPERTS, logits, -jnp.inf)
    v1 = jnp.max(logits, axis=-1, keepdims=True)
    i1 = jnp.min(jnp.where(logits == v1, lane, LANE), axis=-1, keepdims=True)
    rest = jnp.where(lane == i1, -jnp.inf, logits)
    v2 = jnp.max(rest, axis=-1, keepdims=True)
    i2 = jnp.min(jnp.where(rest == v2, lane, LANE), axis=-1, keepdims=True)
    e2 = jnp.exp(v2 - v1)
    den = 1.0 + e2
    gate_ref[...] = jnp.where(lane == 0, 1.0 / den, jnp.where(lane == 1, e2 / den, 0.0))

    onehot = jnp.where((lane == i1) | (lane == i2), 1.0, 0.0)
    run = run_ref[...]
    rank = run + _dot(tri_ref[...], onehot.astype(BF16))
    r1 = jnp.sum(jnp.where(lane == i1, rank, 0.0), axis=-1, keepdims=True).astype(jnp.int32)
    r2 = jnp.sum(jnp.where(lane == i2, rank, 0.0), axis=-1, keepdims=True).astype(jnp.int32)
    route_ref[...] = jnp.where(lane == 0, i1, jnp.where(lane == 1, i2,
                               jnp.where(lane == 2, r1, jnp.where(lane == 3, r2, 0))))
    before_ref[0] = run
    run = run + jnp.sum(onehot, axis=0, keepdims=True)
    run_ref[...] = run
    total_ref[...] = run


def _router(x2, g, w_router):
    T, D = x2.shape
    C = T // MOE_CHUNK
    tri = jnp.asarray(np.tril(np.ones((MOE_CHUNK, MOE_CHUNK), np.float32), -1), BF16)
    row = lambda n: pl.BlockSpec((MOE_CHUNK, n), lambda c: (c, 0))
    full = lambda a: pl.BlockSpec(a.shape, lambda c: (0, 0))
    return pl.pallas_call(
        _router_kernel,
        grid=(C,),
        in_specs=[row(D), full(g), full(w_router), full(tri)],
        out_specs=(row(D), row(LANE), row(LANE),
                   pl.BlockSpec((1, 1, LANE), lambda c: (c, 0, 0)),
                   pl.BlockSpec((1, LANE), lambda c: (0, 0))),
        out_shape=(jax.ShapeDtypeStruct((T, D), BF16),
                   jax.ShapeDtypeStruct((T, LANE), jnp.int32),
                   jax.ShapeDtypeStruct((T, LANE), F32),
                   jax.ShapeDtypeStruct((C, 1, LANE), F32),
                   jax.ShapeDtypeStruct((1, LANE), F32)),
        scratch_shapes=[pltpu.VMEM((1, LANE), F32)],
        compiler_params=_params(("arbitrary",)),
        name="moe_router",
    )(x2, g, w_router, tri)


def _dispatch_kernel(a_ref, h_ref, tok_ref, gate_ref, zlo_ref, zhi_ref, zglo_ref, zghi_ref,
                     lo_ref, hi_ref, glo_ref, ghi_ref):
    del zlo_ref, zhi_ref, zglo_ref, zghi_ref
    e = pl.program_id(0)
    c = pl.program_id(1)
    idx = e * pl.num_programs(1) + c
    a, has_lo, has_hi = _tile_meta(a_ref[idx])
    first = (c == 0) | (a != _tile_meta(a_ref[jnp.maximum(idx - 1, 0)])[0])

    @pl.when(first)
    def _():
        lo_ref[...] = jnp.zeros_like(lo_ref)
        hi_ref[...] = jnp.zeros_like(hi_ref)
        glo_ref[...] = jnp.zeros_like(glo_ref)
        ghi_ref[...] = jnp.zeros_like(ghi_ref)

    def scatter(tile, x_ref, g_ref):
        tok = tok_ref[0]
        gts = gate_ref[0]
        rows = tile * MOE_CHUNK + lax.broadcasted_iota(jnp.int32, (MOE_CHUNK, MOE_CHUNK), 0)
        c1 = tok[0:1] == rows
        c2 = tok[1:2] == rows
        x_ref[...] += _dot(jnp.where(c1 | c2, 1.0, 0.0).astype(BF16), h_ref[...]).astype(x_ref.dtype)
        g_ref[...] += jnp.sum(jnp.where(c1, gts[0:1], 0.0) + jnp.where(c2, gts[1:2], 0.0), axis=1, keepdims=True)

    @pl.when(has_lo)
    def _():
        scatter(a, lo_ref, glo_ref)

    @pl.when(has_hi)
    def _():
        scatter(a + 1, hi_ref, ghi_ref)


def _dispatch(a_ec, h, tok_rows, gate_rows, n_slots):
    T, D = h.shape
    C = T // MOE_CHUNK
    zx = jnp.zeros((n_slots, D), BF16)
    zg = jnp.zeros((n_slots, 1), F32)
    any_spec = pl.BlockSpec(memory_space=pl.ANY)
    lo = lambda n: pl.BlockSpec((MOE_CHUNK, n), lambda e, c, a: (a[e * C + c] >> 2, 0))
    hi = lambda n: pl.BlockSpec((MOE_CHUNK, n), lambda e, c, a: ((a[e * C + c] >> 2) + 1, 0))
    return pl.pallas_call(
        _dispatch_kernel,
        grid_spec=pltpu.PrefetchScalarGridSpec(
            num_scalar_prefetch=1,
            grid=(N_EXPERTS, C),
            in_specs=[pl.BlockSpec((MOE_CHUNK, D), lambda e, c, a: (c, 0)),
                      pl.BlockSpec((1, 8, MOE_CHUNK), lambda e, c, a: (c, 0, 0)),
                      pl.BlockSpec((1, 8, MOE_CHUNK), lambda e, c, a: (c, 0, 0)),
                      any_spec, any_spec, any_spec, any_spec],
            out_specs=(lo(D), hi(D), lo(1), hi(1))),
        out_shape=(jax.ShapeDtypeStruct(zx.shape, BF16), jax.ShapeDtypeStruct(zx.shape, BF16),
                   jax.ShapeDtypeStruct(zg.shape, F32), jax.ShapeDtypeStruct(zg.shape, F32)),
        input_output_aliases={4: 0, 5: 1, 6: 2, 7: 3},
        compiler_params=_params(("arbitrary", "arbitrary")),
        name="moe_dispatch",
    )(a_ec, h, tok_rows, gate_rows, zx, zx, zg, zg)


def _expert_ffn_kernel(te_ref, nu_ref, lo_ref, hi_ref, glo_ref, ghi_ref, wg_ref, wu_ref, wd_ref, o_ref, acc_ref):
    i = pl.program_id(0)
    f = pl.program_id(1)
    used = i < nu_ref[0]

    @pl.when(used & (f == 0))
    def _():
        acc_ref[...] = jnp.zeros_like(acc_ref)

    @pl.when(used)
    def _():
        x = lo_ref[...] + hi_ref[...]
        act = _silu(_dot(x, wg_ref[0])) * _dot(x, wu_ref[0])
        acc_ref[...] += _dot(act.astype(BF16), wd_ref[0])

    last = f == pl.num_programs(1) - 1

    @pl.when(used & last)
    def _():
        o_ref[...] = (acc_ref[...] * (glo_ref[...] + ghi_ref[...])).astype(o_ref.dtype)

    @pl.when(jnp.logical_not(used) & last)
    def _():
        o_ref[...] = jnp.zeros_like(o_ref)


def _expert_ffn(tile_expert, n_used, xs_lo, xs_hi, gs_lo, gs_hi, wg, wu, wd, tf=512):
    N, D = xs_lo.shape
    F = wg.shape[2]
    nf = F // tf
    tr = MOE_ROW_TILE
    fidx = lambda i, f, te, nu: jnp.where(i < nu[0], f, nf - 1)
    row = lambda n: pl.BlockSpec((tr, n), lambda i, f, te, nu: (i, 0))
    return pl.pallas_call(
        _expert_ffn_kernel,
        grid_spec=pltpu.PrefetchScalarGridSpec(
            num_scalar_prefetch=2,
            grid=(N // tr, nf),
            in_specs=[row(D), row(D), row(1), row(1),
                      pl.BlockSpec((1, D, tf), lambda i, f, te, nu: (te[i], 0, fidx(i, f, te, nu))),
                      pl.BlockSpec((1, D, tf), lambda i, f, te, nu: (te[i], 0, fidx(i, f, te, nu))),
                      pl.BlockSpec((1, tf, D), lambda i, f, te, nu: (te[i], fidx(i, f, te, nu), 0))],
            out_specs=row(D),
            scratch_shapes=[pltpu.VMEM((tr, D), F32)]),
        out_shape=jax.ShapeDtypeStruct((N, D), BF16),
        compiler_params=_params(("arbitrary", "arbitrary")),
        name="moe_expert_ffn",
    )(tile_expert, n_used, xs_lo, xs_hi, gs_lo, gs_hi, wg, wu, wd)


def _combine_kernel(a_ref, x_ref, ylo_ref, yhi_ref, s1_ref, s2_ref, gf_ref, o_ref, acc_ref):
    c = pl.program_id(0)
    e = pl.program_id(1)

    @pl.when(e == 0)
    def _():
        acc_ref[...] = jnp.zeros_like(acc_ref)

    a, has_lo, has_hi = _tile_meta(a_ref[c * pl.num_programs(1) + e])

    def gather(tile, y_ref):
        s1 = s1_ref[...]
        s2 = s2_ref[...]
        parts = []
        for j in range(MOE_CHUNK // LANE):
            cols = tile * MOE_CHUNK + j * LANE + lax.broadcasted_iota(jnp.int32, (MOE_CHUNK, LANE), 1)
            parts.append(jnp.where((s1 == cols) | (s2 == cols), 1.0, 0.0).astype(BF16))
        acc_ref[...] += _dot(jnp.concatenate(parts, axis=1), y_ref[...])

    @pl.when(has_lo)
    def _():
        gather(a, ylo_ref)

    @pl.when(has_hi)
    def _():
        gather(a + 1, yhi_ref)

    @pl.when(e == pl.num_programs(1) - 1)
    def _():
        o_ref[...] = _rms(x_ref[...] + acc_ref[...], gf_ref[...])


def _combine(a_ce, x2, ys, slot1_rep, slot2_rep, g_final):
    T, D = x2.shape
    C = T // MOE_CHUNK
    E = N_EXPERTS
    return pl.pallas_call(
        _combine_kernel,
        grid_spec=pltpu.PrefetchScalarGridSpec(
            num_scalar_prefetch=1,
            grid=(C, E),
            in_specs=[pl.BlockSpec((MOE_CHUNK, D), lambda c, e, a: (c, 0)),
                      pl.BlockSpec((MOE_CHUNK, D), lambda c, e, a: (a[c * E + e] >> 2, 0)),
                      pl.BlockSpec((MOE_CHUNK, D), lambda c, e, a: ((a[c * E + e] >> 2) + 1, 0)),
                      pl.BlockSpec((MOE_CHUNK, LANE), lambda c, e, a: (c, 0)),
                      pl.BlockSpec((MOE_CHUNK, LANE), lambda c, e, a: (c, 0)),
                      pl.BlockSpec((1, D), lambda c, e, a: (0, 0))],
            out_specs=pl.BlockSpec((MOE_CHUNK, D), lambda c, e, a: (c, 0)),
            scratch_shapes=[pltpu.VMEM((MOE_CHUNK, D), F32)]),
        out_shape=jax.ShapeDtypeStruct((T, D), F32),
        compiler_params=_params(("arbitrary", "arbitrary")),
        name="moe_combine_final_norm",
    )(a_ce, x2, ys, ys, slot1_rep, slot2_rep, g_final)


def _rope_tables(S):
    inv = ROPE_THETA ** (-jnp.arange(ROPE_HALF, dtype=F32) / ROPE_HALF)
    ang = jnp.arange(S, dtype=F32)[:, None] * inv[None, :]
    cos, sin = jnp.cos(ang), jnp.sin(ang)
    pad = HEAD_DIM - ROPE_DIM
    cos_h = jnp.concatenate([cos, cos, jnp.ones((S, pad), F32)], axis=-1)
    sin_h = jnp.concatenate([-sin, sin, jnp.zeros((S, pad), F32)], axis=-1)
    reps = LANE // HEAD_DIM
    return jnp.tile(cos_h, (1, reps)), jnp.tile(sin_h, (1, reps))


def _pack_w_in(w):
    per_group = GROUP * N_BRANCH
    gates = w[:, GATE_OFF:GATE_OFF + N_KV_HEADS * per_group].reshape(-1, N_KV_HEADS, per_group)
    gates = jnp.pad(gates, ((0, 0), (0, LANE // GATE_ROWS - N_KV_HEADS), (0, GATE_ROWS - per_group)))
    gates = gates.reshape(-1, LANE)
    return jnp.concatenate([w[:, :GATE_OFF], gates, w[:, GATE_OFF + N_KV_HEADS * per_group:]], axis=1).astype(BF16)


def _mixer_layer(x, norm_g, w_in, pe_k, w1_k, b1_k, w2_k, pe_v, w1_v, b1_v, w2_v,
                 w_pool, pool_scale, w_out, cos_t, sin_t):
    B, S, D = x.shape
    qt, qrt, kcr, vcr, ks, vst, kw, vwt, gates, u = _mixer_proj(
        x, norm_g.reshape(1, D), _pack_w_in(w_in), cos_t, sin_t)
    half = CMP_LEN // 2
    cw = half * HEAD_DIM
    pad_w2 = lambda w2: jnp.pad(w2, ((0, 0), (0, LANE - HEAD_DIM))).astype(BF16)
    kc, vct = _compress(
        kcr, vcr,
        pe_k.reshape(2, cw), w1_k.reshape(2, cw, CMP_HIDDEN).astype(BF16), b1_k.reshape(1, -1), pad_w2(w2_k),
        pe_v.reshape(2, cw), w1_v.reshape(2, cw, CMP_HIDDEN).astype(BF16), b1_v.reshape(1, -1), pad_w2(w2_v))
    a = _nsa_attention(qt, qrt, kc, vct, ks, vst, kw, vwt, gates)
    p = _pool_mixer(u, w_pool.astype(BF16), pool_scale.reshape(1, -1))
    w_out = w_out.astype(BF16)
    out = _out_proj(x.reshape(B * S, D), a.reshape(B * S, NSA_WIDTH), p.reshape(B * S, POOL_WIDTH),
                    w_out[:NSA_WIDTH], w_out[NSA_WIDTH:])
    return out.reshape(B, S, D)


def _moe_layer(x2, g_ffn, router, wg, wu, wd, g_final):
    T, D = x2.shape
    E = N_EXPERTS
    C = T // MOE_CHUNK
    i32 = jnp.int32
    w_router = jnp.pad(router, ((0, 0), (0, LANE - E)))
    h, route, gates, before, total = _router(x2, g_ffn.reshape(1, D), w_router)

    counts = total[0, :E].astype(i32)
    tiles = (counts + MOE_ROW_TILE - 1) // MOE_ROW_TILE
    ends = jnp.cumsum(tiles)
    off = (ends - tiles) * MOE_ROW_TILE
    e1, e2, r1, r2 = route[:, 0], route[:, 1], route[:, 2], route[:, 3]
    slot1 = off[e1] + r1
    slot2 = off[e2] + r2
    first = off[None, :] + before[:, 0, :E].astype(i32)
    after = jnp.concatenate([before[1:, 0, :E], total[:, :E]], axis=0).astype(i32)
    last = off[None, :] + after - 1
    lo_tile = first // MOE_CHUNK
    has_lo = (last >= first).astype(i32)
    has_hi = ((last >= first) & (last // MOE_CHUNK > lo_tile)).astype(i32)
    tile_meta = lo_tile * 4 + has_hi * 2 + has_lo
    n_row_tiles = (2 * T) // MOE_ROW_TILE + E + 1 + MOE_CHUNK // MOE_ROW_TILE
    n_used = ends[-1:]
    tile_ids = jnp.minimum(jnp.arange(n_row_tiles, dtype=i32), n_used[0] - 1)
    tile_expert = jnp.minimum(jnp.sum(ends[None, :] <= tile_ids[:, None], axis=1), E - 1).astype(i32)

    zero = jnp.zeros_like(slot1)
    tok = jnp.stack([slot1, slot2, zero, zero, zero, zero, zero, zero])
    tok_rows = tok.reshape(8, C, MOE_CHUNK).transpose(1, 0, 2)
    gz = jnp.zeros((T,), F32)
    gate_rows = jnp.stack([gates[:, 0], gates[:, 1], gz, gz, gz, gz, gz, gz]).reshape(8, C, MOE_CHUNK).transpose(1, 0, 2)
    slot1_rep = jnp.broadcast_to(slot1[:, None], (T, LANE))
    slot2_rep = jnp.broadcast_to(slot2[:, None], (T, LANE))

    xs_lo, xs_hi, gs_lo, gs_hi = _dispatch(tile_meta.T.reshape(-1).astype(i32), h, tok_rows, gate_rows,
                                           n_row_tiles * MOE_ROW_TILE)
    ys = _expert_ffn(tile_expert, n_used.astype(i32), xs_lo, xs_hi, gs_lo, gs_hi,
                     wg.astype(BF16), wu.astype(BF16), wd.astype(BF16))
    return _combine(tile_meta.reshape(-1).astype(i32), x2, ys, slot1_rep, slot2_rep, g_final.reshape(1, D))


def kernel(x, mem, norm_mix, w_in, cmp_pe_k, cmp_w1_k, cmp_b1_k, cmp_w2_k, cmp_pe_v, cmp_w1_v, cmp_b1_v, cmp_w2_v, w_pool, pool_scale, w_out, norm_x, norm_mem, wq_x, wk_x, wv_x, wo_x, norm_ffn, ffn_wg, ffn_wu, ffn_wd, moe_router, moe_wg, moe_wu, moe_wd, norm_final):
    B, S, D = x.shape
    depth = norm_mix.shape[0]
    assert depth == 2, "the final RMSNorm is fused into the expert layer, which must be the last one"
    cos_t, sin_t = _rope_tables(S)
    for layer in range(depth):
        x = _mixer_layer(x, norm_mix[layer], w_in[layer],
                         cmp_pe_k[layer], cmp_w1_k[layer], cmp_b1_k[layer], cmp_w2_k[layer],
                         cmp_pe_v[layer], cmp_w1_v[layer], cmp_b1_v[layer], cmp_w2_v[layer],
                         w_pool[layer], pool_scale[layer], w_out[layer], cos_t, sin_t)
        wkv = jnp.concatenate([wk_x[layer], wv_x[layer]], axis=1).astype(BF16)
        kx, vx = _xattn_kv(mem, norm_mem[layer].reshape(1, D), wkv)
        x = _xattn(x, norm_x[layer].reshape(1, D), wq_x[layer].astype(BF16), kx, vx, wo_x[layer].astype(BF16))
        x2 = x.reshape(B * S, D)
        j = layer // 2
        if layer % 2 == 0:
            x2 = _ffn(x2, norm_ffn[layer].reshape(1, D), ffn_wg[j].astype(BF16), ffn_wu[j].astype(BF16),
                      ffn_wd[j].astype(BF16))
        else:
            x2 = _moe_layer(x2, norm_ffn[layer], moe_router[j], moe_wg[j], moe_wu[j], moe_wd[j], norm_final)
        x = x2.reshape(B, S, D)
    return x
```

```python
import functools
import math

import numpy as np
import jax
import jax.numpy as jnp
from jax import lax
from jax.experimental import pallas as pl
from jax.experimental.pallas import tpu as pltpu

F32 = jnp.float32
BF16 = jnp.bfloat16

D_MODEL = 1024
HEAD_DIM = 64
N_HEADS = 8
N_KV_HEADS = 2
GROUP = N_HEADS // N_KV_HEADS
NSA_WIDTH = N_HEADS * HEAD_DIM
KV_WIDTH = N_KV_HEADS * HEAD_DIM
N_BRANCH = 3
GATE_ROWS = 16
POOL_WIDTH = 512
POOL_WINDOWS = (2, 4, 8, 16)
POOL_GROUP = 128
POOL_HALO = 16
ROPE_DIM = 16
ROPE_HALF = 8
ROPE_THETA = 500000.0
CMP_LEN = 32
CMP_STRIDE = 16
CMP_HIDDEN = 256
SEL_BLOCK = 64
SEL_TOP_N = 16
N_LOCAL_BLOCKS = 2
WINDOW = 512
X_HEADS = 4
X_HEAD_DIM = 256
N_EXPERTS = 8
EPS = 1e-6

LANE = 128
IN_PACKED = NSA_WIDTH + 6 * KV_WIDTH + LANE + POOL_WIDTH
GATE_OFF = NSA_WIDTH + 6 * KV_WIDTH
POOL_OFF = GATE_OFF + LANE

NEG = -1e30
TINY = float(np.finfo(np.float32).tiny)
VMEM_LIMIT = 56 * 1024 * 1024


def _dot(a, b):
    return jnp.dot(a, b, preferred_element_type=F32)


def _dot_nt(a, b):
    return lax.dot_general(a, b, (((1,), (1,)), ((), ())), preferred_element_type=F32)


def _rms(x, g):
    y = x * lax.rsqrt(jnp.mean(x * x, axis=-1, keepdims=True) + EPS)
    return y * g


def _params(sem, limit=VMEM_LIMIT):
    return pltpu.CompilerParams(dimension_semantics=sem, vmem_limit_bytes=limit)


def _mixer_proj_kernel(x_ref, g_ref, w_ref, cos_ref, sin_ref,
                       qt_ref, qrt_ref, kcr_ref, vcr_ref, ks_ref, vst_ref, kw_ref, vwt_ref,
                       gate_ref, u_ref):
    h = _rms(x_ref[0], g_ref[...]).astype(BF16)
    z = _dot(h, w_ref[...])
    tm = z.shape[0]
    cos = cos_ref[...]
    sin = sin_ref[...]
    lane = lax.broadcasted_iota(jnp.int32, (tm, LANE), 1)
    first = (lane & (HEAD_DIM - 1)) < ROPE_HALF
    scale = HEAD_DIM ** -0.5 * math.log2(math.e)

    def rope(xs):
        partner = jnp.where(first, pltpu.roll(xs, LANE - ROPE_HALF, 1), pltpu.roll(xs, ROPE_HALF, 1))
        return xs * cos + partner * sin

    for s in range(NSA_WIDTH // LANE):
        xs = z[:, s * LANE:(s + 1) * LANE]
        for src, ref in ((xs, qt_ref), (rope(xs), qrt_ref)):
            t = (src * scale).T.astype(BF16)
            ref[0, 2 * s] = t[:HEAD_DIM]
            ref[0, 2 * s + 1] = t[HEAD_DIM:]

    def kv_slab(i):
        return z[:, NSA_WIDTH + i * KV_WIDTH:NSA_WIDTH + (i + 1) * KV_WIDTH]

    for slab, ref in ((kv_slab(0), kcr_ref), (kv_slab(1), vcr_ref),
                      (rope(kv_slab(2)), ks_ref), (rope(kv_slab(4)), kw_ref)):
        for gg in range(N_KV_HEADS):
            ref[0, gg] = slab[:, gg * HEAD_DIM:(gg + 1) * HEAD_DIM].astype(ref.dtype)

    for slab, ref in ((kv_slab(3), vst_ref), (kv_slab(5), vwt_ref)):
        t = slab.T.astype(BF16)
        for gg in range(N_KV_HEADS):
            for k in range(tm // LANE):
                ref[0, gg, k] = t[gg * HEAD_DIM:(gg + 1) * HEAD_DIM, k * LANE:(k + 1) * LANE]

    sig_t = jax.nn.sigmoid(z[:, GATE_OFF:GATE_OFF + LANE]).T
    for gg in range(N_KV_HEADS):
        gate_ref[0, gg] = sig_t[gg * GATE_ROWS:(gg + 1) * GATE_ROWS]
    u_ref[0] = z[:, POOL_OFF:POOL_OFF + POOL_WIDTH]


def _mixer_proj(x, g, w_packed, cos_t, sin_t, tm=512):
    B, S, D = x.shape
    G = N_KV_HEADS
    sd = jax.ShapeDtypeStruct
    out_shape = (sd((B, N_HEADS, HEAD_DIM, S), BF16), sd((B, N_HEADS, HEAD_DIM, S), BF16),
                 sd((B, G, S, HEAD_DIM), F32), sd((B, G, S, HEAD_DIM), F32),
                 sd((B, G, S, HEAD_DIM), BF16), sd((B, G, S // LANE, HEAD_DIM, LANE), BF16),
                 sd((B, G, S, HEAD_DIM), BF16), sd((B, G, S // LANE, HEAD_DIM, LANE), BF16),
                 sd((B, G, GATE_ROWS, S), F32),
                 sd((B, S, POOL_WIDTH), F32))
    qspec = pl.BlockSpec((1, N_HEADS, HEAD_DIM, tm), lambda b, i: (b, 0, 0, i))
    kspec = pl.BlockSpec((1, G, tm, HEAD_DIM), lambda b, i: (b, 0, i, 0))
    vspec = pl.BlockSpec((1, G, tm // LANE, HEAD_DIM, LANE), lambda b, i: (b, 0, i, 0, 0))
    out_specs = (qspec, qspec, kspec, kspec, kspec, vspec, kspec, vspec,
                 pl.BlockSpec((1, G, GATE_ROWS, tm), lambda b, i: (b, 0, 0, i)),
                 pl.BlockSpec((1, tm, POOL_WIDTH), lambda b, i: (b, i, 0)))
    return pl.pallas_call(
        _mixer_proj_kernel,
        grid=(B, S // tm),
        in_specs=[pl.BlockSpec((1, tm, D), lambda b, i: (b, i, 0)),
                  pl.BlockSpec((1, D), lambda b, i: (0, 0)),
                  pl.BlockSpec((D, IN_PACKED), lambda b, i: (0, 0)),
                  pl.BlockSpec((tm, LANE), lambda b, i: (i, 0)),
                  pl.BlockSpec((tm, LANE), lambda b, i: (i, 0))],
        out_specs=out_specs,
        out_shape=out_shape,
        compiler_params=_params(("parallel", "parallel")),
        name="mixer_proj",
    )(x, g, w_packed, cos_t, sin_t)


def _gelu_tanh(x):
    return 0.5 * x * (1.0 + jnp.tanh(math.sqrt(2.0 / math.pi) * (x + 0.044715 * (x * x * x))))


def _compress_kernel(kc_ref, vc_ref, pek_ref, w1k_ref, b1k_ref, w2k_ref,
                     pev_ref, w1v_ref, b1v_ref, w2v_ref, ko_ref, vto_ref):
    def comp(x_ref, pe_ref, w1_ref, b1_ref, w2_ref):
        x = x_ref[0, 0]
        n = x.shape[0]
        a = _dot((x + pe_ref[0:1, :]).astype(BF16), w1_ref[0])
        b = _dot((x + pe_ref[1:2, :]).astype(BF16), w1_ref[1])
        pre = a + pltpu.roll(b, n - 1, 0) + b1_ref[...]
        return _dot(_gelu_tanh(pre).astype(BF16), w2_ref[...])

    ko_ref[0, 0] = comp(kc_ref, pek_ref, w1k_ref, b1k_ref, w2k_ref)[:, :HEAD_DIM].astype(ko_ref.dtype)
    vto_ref[0, 0] = comp(vc_ref, pev_ref, w1v_ref, b1v_ref, w2v_ref).T[:HEAD_DIM].astype(vto_ref.dtype)


def _compress(kcr, vcr, pek, w1k, b1k, w2k, pev, w1v, b1v, w2v):
    B, G, S, _ = kcr.shape
    n = S // CMP_STRIDE
    cw = CMP_STRIDE * HEAD_DIM
    kc = kcr.reshape(B, G, n, cw)
    vc = vcr.reshape(B, G, n, cw)
    xspec = pl.BlockSpec((1, 1, n, cw), lambda b, g: (b, g, 0, 0))
    full = lambda a: pl.BlockSpec(a.shape, lambda b, g: (0,) * a.ndim)
    return pl.pallas_call(
        _compress_kernel,
        grid=(B, G),
        in_specs=[xspec, xspec, full(pek), full(w1k), full(b1k), full(w2k),
                  full(pev), full(w1v), full(b1v), full(w2v)],
        out_specs=(pl.BlockSpec((1, 1, n, HEAD_DIM), lambda b, g: (b, g, 0, 0)),
                   pl.BlockSpec((1, 1, HEAD_DIM, n), lambda b, g: (b, g, 0, 0))),
        out_shape=(jax.ShapeDtypeStruct((B, G, n, HEAD_DIM), BF16),
                   jax.ShapeDtypeStruct((B, G, HEAD_DIM, n), BF16)),
        compiler_params=_params(("parallel", "parallel")),
        name="compress_kv",
    )(kc, vc, pek, w1k, b1k, w2k, pev, w1v, b1v, w2v)


def _nsa_kernel(qt_ref, qrt_ref, kc_ref, vct_ref, ks_ref, vst_ref, kw_ref, vwt_ref, gate_ref,
                covt_ref, o_ref, selbias_ref, *, tq, tk):
    qt = pl.program_id(2)
    q0 = qt * tq
    n_cmp = kc_ref.shape[2]
    n_blk = covt_ref.shape[0]
    heads = lambda ref: jnp.concatenate([ref[0, hh] for hh in range(GROUP)], axis=1)
    q_t = heads(qt_ref)
    qr_t = heads(qrt_ref)
    t_lane = q0 + lax.broadcasted_iota(jnp.int32, (1, tq), 1)
    per_head = lambda a: [a[:, hh * tq:(hh + 1) * tq] for hh in range(GROUP)]
    all_heads = lambda a: jnp.concatenate([a] * GROUP, axis=1)

    sc = _dot(kc_ref[0, 0], q_t)
    cmp_end = lax.broadcasted_iota(jnp.int32, (n_cmp, 1), 0) * CMP_STRIDE + (CMP_LEN - 1)
    valid = cmp_end <= t_lane
    pcs = []
    for s_h in per_head(sc):
        s_h = jnp.where(valid, s_h, NEG)
        e = jnp.where(valid, jnp.exp2(s_h - jnp.max(s_h, axis=0, keepdims=True)), 0.0)
        pcs.append(e * (1.0 / jnp.maximum(jnp.sum(e, axis=0, keepdims=True), TINY)))
    oc_t = _dot(vct_ref[0, 0], jnp.concatenate(pcs, axis=1).astype(BF16))

    psum = pcs[0] + pcs[1] + pcs[2] + pcs[3]
    p_hi = psum.astype(BF16)
    r1 = psum - p_hi.astype(F32)
    p_mid = r1.astype(BF16)
    p_lo = (r1 - p_mid.astype(F32)).astype(BF16)
    covt = covt_ref[...]
    imp_t = _dot(covt, p_hi) + _dot(covt, p_mid) + _dot(covt, p_lo)

    jb = lax.broadcasted_iota(jnp.int32, (n_blk, tq), 0)
    tb = (q0 + lax.broadcasted_iota(jnp.int32, (n_blk, tq), 1)) // SEL_BLOCK
    dist = tb - jb
    forced = (jb == 0) | ((dist >= 0) & (dist < N_LOCAL_BLOCKS))
    score = jnp.where(jb > tb, -jnp.inf, jnp.where(forced, jnp.inf, imp_t))
    rank = jnp.zeros((n_blk, tq), jnp.int32)
    for jp in range(n_blk):
        rowv = score[jp:jp + 1, :]
        beats = (rowv > score) | ((rowv == score) & (jb > jp))
        rank = rank + jnp.where(beats, 1, 0)
    selbias_ref[...] = jnp.where(rank < min(SEL_TOP_N, n_blk), 0.0, NEG)

    R = GROUP * tq

    def with_ones(v_t):
        extra = jnp.where(lax.broadcasted_iota(jnp.int32, (16, v_t.shape[1]), 0) == 0, 1.0, 0.0)
        return jnp.concatenate([v_t, extra.astype(BF16)], axis=0)

    bpc = tk // SEL_BLOCK
    vpc = tk // LANE

    def sel_chunk(c, carry, causal):
        m_i, l_i, acc = carry
        start = pl.multiple_of(c * tk, tk)
        k_c = ks_ref[0, 0, pl.ds(start, tk), :]
        v_t = jnp.concatenate([vst_ref[0, 0, c * vpc + k] for k in range(vpc)], axis=1)
        rows = [jnp.broadcast_to(selbias_ref[pl.ds(c * bpc + j, 1), :], (SEL_BLOCK, tq)) for j in range(bpc)]
        bias = jnp.concatenate(rows, axis=0)
        if causal:
            kpos = start + lax.broadcasted_iota(jnp.int32, (tk, 1), 0)
            bias = jnp.where(kpos <= t_lane, bias, NEG)
        s = _dot(k_c, qr_t) + all_heads(bias)
        m_new = jnp.maximum(m_i, jnp.max(s, axis=0, keepdims=True))
        alpha = jnp.exp2(m_i - m_new)
        pv = _dot(with_ones(v_t), jnp.exp2(s - m_new).astype(BF16))
        return m_new, alpha * l_i + pv[HEAD_DIM:HEAD_DIM + 1], alpha * acc + pv[:HEAD_DIM]

    init = (jnp.full((1, R), NEG, F32), jnp.zeros((1, R), F32), jnp.zeros((HEAD_DIM, R), F32))
    c_diag = q0 // tk
    carry = lax.fori_loop(0, c_diag, lambda c, cr: sel_chunk(c, cr, False), init)
    _, l_s, acc_s = sel_chunk(c_diag, carry, True)
    os_t = acc_s * (1.0 / jnp.maximum(l_s, TINY))

    wk = WINDOW + tq
    w0 = pl.multiple_of(jnp.maximum(q0 - WINDOW, 0), LANE)
    k_w = kw_ref[0, 0, pl.ds(w0, wk), :]
    v_t = jnp.concatenate([vwt_ref[0, 0, w0 // LANE + k] for k in range(wk // LANE)], axis=1)
    diff = t_lane - (w0 + lax.broadcasted_iota(jnp.int32, (wk, 1), 0))
    bias = jnp.where((diff >= 0) & (diff < WINDOW), 0.0, NEG)
    s = _dot(k_w, qr_t) + all_heads(bias)
    pv = _dot(with_ones(v_t), jnp.exp2(s - jnp.max(s, axis=0, keepdims=True)).astype(BF16))
    ow_t = pv[:HEAD_DIM] * (1.0 / pv[HEAD_DIM:HEAD_DIM + 1])

    gt = gate_ref[0, 0]
    outs = []
    for hh, (a, b, c) in enumerate(zip(per_head(oc_t), per_head(os_t), per_head(ow_t))):
        r = hh * N_BRANCH
        outs.append(gt[r:r + 1] * a + gt[r + 1:r + 2] * b + gt[r + 2:r + 3] * c)
    o_ref[0] = jnp.concatenate(outs, axis=0).T.astype(o_ref.dtype)


def _cover_table(S):
    n_cmp = S // CMP_STRIDE
    n_blk = S // SEL_BLOCK
    cs = np.arange(n_cmp) * CMP_STRIDE
    ss = np.arange(n_blk) * SEL_BLOCK
    cover_t = ((cs[None, :] < ss[:, None] + SEL_BLOCK) & (cs[None, :] + CMP_LEN > ss[:, None]))
    cover_t[:, n_cmp - 1] = False
    return jnp.asarray(cover_t, BF16)


def _nsa_attention(qt, qrt, kc, vct, ks, vst, kw, vwt, gates, tq=128, tk=1024):
    B, _, _, S = qt.shape
    tk = min(tk, S)
    assert tq == LANE and S % tk == 0 and S >= WINDOW + tq
    covt = _cover_table(S)
    n_cmp = kc.shape[2]
    n_blk = covt.shape[0]
    qspec = pl.BlockSpec((1, GROUP, HEAD_DIM, tq), lambda b, g, i: (b, g, 0, i))
    kspec = lambda a: pl.BlockSpec((1, 1) + a.shape[2:], lambda b, g, i: (b, g, 0, 0))
    vspec = pl.BlockSpec((1, 1, S // LANE, HEAD_DIM, LANE), lambda b, g, i: (b, g, 0, 0, 0))
    return pl.pallas_call(
        functools.partial(_nsa_kernel, tq=tq, tk=tk),
        grid=(B, N_KV_HEADS, S // tq),
        in_specs=[qspec, qspec,
                  pl.BlockSpec((1, 1, n_cmp, HEAD_DIM), lambda b, g, i: (b, g, 0, 0)),
                  pl.BlockSpec((1, 1, HEAD_DIM, n_cmp), lambda b, g, i: (b, g, 0, 0)),
                  kspec(ks), vspec, kspec(kw), vspec,
                  pl.BlockSpec((1, 1, GATE_ROWS, tq), lambda b, g, i: (b, g, 0, i)),
                  pl.BlockSpec(covt.shape, lambda b, g, i: (0, 0))],
        out_specs=pl.BlockSpec((1, tq, GROUP * HEAD_DIM), lambda b, g, i: (b, i, g)),
        out_shape=jax.ShapeDtypeStruct((B, S, NSA_WIDTH), BF16),
        scratch_shapes=[pltpu.VMEM((n_blk, tq), F32)],
        compiler_params=_params(("parallel", "parallel", "parallel")),
        name="nsa_attention",
    )(qt, qrt, kc, vct, ks, vst, kw, vwt, gates, covt)


def _pool_kernel(u_ref, w_ref, scale_ref, o_ref, buf):
    i = pl.program_id(1)
    tm = u_ref.shape[1]

    @pl.when(i == 0)
    def _():
        buf[0:POOL_HALO, :] = jnp.zeros((POOL_HALO, POOL_WIDTH), F32)

    @pl.when(i > 0)
    def _():
        buf[0:POOL_HALO, :] = buf[tm:tm + POOL_HALO, :]

    buf[POOL_HALO:POOL_HALO + tm, :] = u_ref[0]
    t1 = i * tm + lax.broadcasted_iota(jnp.int32, (tm, 1), 0) + 1
    for gi, w in enumerate(POOL_WINDOWS):
        cols = slice(gi * POOL_GROUP, (gi + 1) * POOL_GROUP)
        cur = buf[POOL_HALO:POOL_HALO + tm, cols]
        tot = cur
        for k in range(1, w):
            tot = tot + buf[POOL_HALO - k:POOL_HALO - k + tm, cols]
        cnt = jnp.minimum(t1, w).astype(F32)
        d = tot / cnt - cur
        y = _dot(d.astype(BF16), w_ref[gi])
        o_ref[0, :, cols] = (y * scale_ref[:, cols]).astype(o_ref.dtype)


def _pool_mixer(u, w_pool, pool_scale, tm=512):
    B, S, _ = u.shape
    return pl.pallas_call(
        _pool_kernel,
        grid=(B, S // tm),
        in_specs=[pl.BlockSpec((1, tm, POOL_WIDTH), lambda b, i: (b, i, 0)),
                  pl.BlockSpec(w_pool.shape, lambda b, i: (0, 0, 0)),
                  pl.BlockSpec((1, POOL_WIDTH), lambda b, i: (0, 0))],
        out_specs=pl.BlockSpec((1, tm, POOL_WIDTH), lambda b, i: (b, i, 0)),
        out_shape=jax.ShapeDtypeStruct((B, S, POOL_WIDTH), BF16),
        scratch_shapes=[pltpu.VMEM((tm + POOL_HALO, POOL_WIDTH), F32)],
        compiler_params=_params(("parallel", "arbitrary")),
        name="pool_mixer",
    )(u, w_pool, pool_scale)


def _out_proj_kernel(x_ref, a_ref, p_ref, wa_ref, wp_ref, o_ref):
    o_ref[...] = x_ref[...] + _dot(a_ref[...], wa_ref[...]) + _dot(p_ref[...], wp_ref[...])


def _out_proj(x2, a2, p2, wa, wp, tm=512):
    T, D = x2.shape
    row = lambda n: pl.BlockSpec((tm, n), lambda i: (i, 0))
    full = lambda a: pl.BlockSpec(a.shape, lambda i: (0, 0))
    return pl.pallas_call(
        _out_proj_kernel,
        grid=(T // tm,),
        in_specs=[row(D), row(a2.shape[1]), row(p2.shape[1]), full(wa), full(wp)],
        out_specs=row(D),
        out_shape=jax.ShapeDtypeStruct((T, D), F32),
        compiler_params=_params(("parallel",)),
        name="mixer_out_proj",
    )(x2, a2, p2, wa, wp)


def _xkv_kernel(m_ref, g_ref, w_ref, k_ref, v_ref):
    h = _rms(m_ref[0], g_ref[...]).astype(BF16)
    kv = _dot(h, w_ref[...])
    d = k_ref.shape[2]
    k_ref[0] = kv[:, :d].astype(k_ref.dtype)
    v_ref[0] = kv[:, d:].astype(v_ref.dtype)


def _xattn_kv(mem, g, wkv):
    B, M, D = mem.shape
    ospec = pl.BlockSpec((1, M, D), lambda b: (b, 0, 0))
    oshape = jax.ShapeDtypeStruct((B, M, D), BF16)
    return pl.pallas_call(
        _xkv_kernel,
        grid=(B,),
        in_specs=[pl.BlockSpec((1, M, D), lambda b: (b, 0, 0)),
                  pl.BlockSpec((1, D), lambda b: (0, 0)),
                  pl.BlockSpec(wkv.shape, lambda b: (0, 0))],
        out_specs=(ospec, ospec),
        out_shape=(oshape, oshape),
        compiler_params=_params(("parallel",)),
        name="xattn_kv",
    )(mem, g, wkv)


def _xattn_kernel(x_ref, g_ref, wq_ref, k_ref, v_ref, wo_ref, o_ref):
    x = x_ref[0]
    h = _rms(x, g_ref[...]).astype(BF16)
    scale = X_HEAD_DIM ** -0.5
    q = (_dot(h, wq_ref[...]) * scale).astype(BF16)
    outs = []
    for hd in range(X_HEADS):
        sl = slice(hd * X_HEAD_DIM, (hd + 1) * X_HEAD_DIM)
        s = _dot_nt(q[:, sl], k_ref[0, :, sl])
        e = jnp.exp(s - jnp.max(s, axis=-1, keepdims=True))
        p = e / jnp.sum(e, axis=-1, keepdims=True)
        outs.append(_dot(p.astype(BF16), v_ref[0, :, sl]).astype(BF16))
    o = jnp.concatenate(outs, axis=-1)
    o_ref[0] = x + _dot(o, wo_ref[...])


def _xattn(x, g, wq, kx, vx, wo, tm=512):
    B, S, D = x.shape
    M = kx.shape[1]
    xspec = pl.BlockSpec((1, tm, D), lambda b, i: (b, i, 0))
    full = lambda a: pl.BlockSpec(a.shape, lambda b, i: (0, 0))
    mspec = pl.BlockSpec((1, M, D), lambda b, i: (b, 0, 0))
    return pl.pallas_call(
        _xattn_kernel,
        grid=(B, S // tm),
        in_specs=[xspec, full(g), full(wq), mspec, mspec, full(wo)],
        out_specs=xspec,
        out_shape=jax.ShapeDtypeStruct((B, S, D), F32),
        compiler_params=_params(("parallel", "parallel")),
        name="xattn",
    )(x, g, wq, kx, vx, wo)


def _silu(x):
    return x * jax.nn.sigmoid(x)


def _ffn_kernel(x_ref, g_ref, wg_ref, wu_ref, wd_ref, o_ref, h_ref, acc_ref):
    f = pl.program_id(1)

    @pl.when(f == 0)
    def _():
        h_ref[...] = _rms(x_ref[...], g_ref[...]).astype(BF16)
        acc_ref[...] = jnp.zeros_like(acc_ref)

    h = h_ref[...]
    act = _silu(_dot(h, wg_ref[...])) * _dot(h, wu_ref[...])
    acc_ref[...] += _dot(act.astype(BF16), wd_ref[...])

    @pl.when(f == pl.num_programs(1) - 1)
    def _():
        o_ref[...] = x_ref[...] + acc_ref[...]


def _ffn(x2, g, wg, wu, wd, tm=512, tf=1408):
    T, D = x2.shape
    F = wg.shape[1]
    return pl.pallas_call(
        _ffn_kernel,
        grid=(T // tm, F // tf),
        in_specs=[pl.BlockSpec((tm, D), lambda i, f: (i, 0)),
                  pl.BlockSpec((1, D), lambda i, f: (0, 0)),
                  pl.BlockSpec((D, tf), lambda i, f: (0, f)),
                  pl.BlockSpec((D, tf), lambda i, f: (0, f)),
                  pl.BlockSpec((tf, D), lambda i, f: (f, 0))],
        out_specs=pl.BlockSpec((tm, D), lambda i, f: (i, 0)),
        out_shape=jax.ShapeDtypeStruct((T, D), F32),
        scratch_shapes=[pltpu.VMEM((tm, D), BF16), pltpu.VMEM((tm, D), F32)],
        compiler_params=_params(("parallel", "arbitrary")),
        name="dense_swiglu",
    )(x2, g, wg, wu, wd)


MOE_CHUNK = 512
MOE_ROW_TILE = 512
assert MOE_ROW_TILE % MOE_CHUNK == 0


def _tile_meta(m):
    return m >> 2, (m & 1) != 0, (m & 2) != 0


def _router_kernel(x_ref, g_ref, wr_ref, tri_ref, h_ref, route_ref, gate_ref, before_ref, total_ref, run_ref):
    c = pl.program_id(0)

    @pl.when(c == 0)
    def _():
        run_ref[...] = jnp.zeros_like(run_ref)

    hf = _rms(x_ref[...], g_ref[...])
    h_ref[...] = hf.astype(BF16)
    logits = jnp.dot(hf, wr_ref[...], preferred_element_type=F32, precision=lax.Precision.HIGHEST)
    lane = lax.broadcasted_iota(jnp.int32, logits.shape, 1)
    logits = jnp.where(lane < N_EXPERTS, logits, -jnp.inf)
    v1 = jnp.max(logits, axis=-1, keepdims=True)
    i1 = jnp.min(jnp.where(logits == v1, lane, LANE), axis=-1, keepdims=True)
    rest = jnp.where(lane == i1, -jnp.inf, logits)
    v2 = jnp.max(rest, axis=-1, keepdims=True)
    i2 = jnp.min(jnp.where(rest == v2, lane, LANE), axis=-1, keepdims=True)
    e2 = jnp.exp(v2 - v1)
    den = 1.0 + e2
    gate_ref[...] = jnp.where(lane == 0, 1.0 / den, jnp.where(lane == 1, e2 / den, 0.0))

    onehot = jnp.where((lane == i1) | (lane == i2), 1.0, 0.0)
    run = run_ref[...]
    rank = run + _dot(tri_ref[...], onehot.astype(BF16))
    r1 = jnp.sum(jnp.where(lane == i1, rank, 0.0), axis=-1, keepdims=True).astype(jnp.int32)
    r2 = jnp.sum(jnp.where(lane == i2, rank, 0.0), axis=-1, keepdims=True).astype(jnp.int32)
    route_ref[...] = jnp.where(lane == 0, i1, jnp.where(lane == 1, i2,
                               jnp.where(lane == 2, r1, jnp.where(lane == 3, r2, 0))))
    before_ref[0] = run
    run = run + jnp.sum(onehot, axis=0, keepdims=True)
    run_ref[...] = run
    total_ref[...] = run


def _router(x2, g, w_router):
    T, D = x2.shape
    C = T // MOE_CHUNK
    tri = jnp.asarray(np.tril(np.ones((MOE_CHUNK, MOE_CHUNK), np.float32), -1), BF16)
    row = lambda n: pl.BlockSpec((MOE_CHUNK, n), lambda c: (c, 0))
    full = lambda a: pl.BlockSpec(a.shape, lambda c: (0, 0))
    return pl.pallas_call(
        _router_kernel,
        grid=(C,),
        in_specs=[row(D), full(g), full(w_router), full(tri)],
        out_specs=(row(D), row(LANE), row(LANE),
                   pl.BlockSpec((1, 1, LANE), lambda c: (c, 0, 0)),
                   pl.BlockSpec((1, LANE), lambda c: (0, 0))),
        out_shape=(jax.ShapeDtypeStruct((T, D), BF16),
                   jax.ShapeDtypeStruct((T, LANE), jnp.int32),
                   jax.ShapeDtypeStruct((T, LANE), F32),
                   jax.ShapeDtypeStruct((C, 1, LANE), F32),
                   jax.ShapeDtypeStruct((1, LANE), F32)),
        scratch_shapes=[pltpu.VMEM((1, LANE), F32)],
        compiler_params=_params(("arbitrary",)),
        name="moe_router",
    )(x2, g, w_router, tri)


def _dispatch_kernel(a_ref, h_ref, tok_ref, gate_ref, zlo_ref, zhi_ref, zglo_ref, zghi_ref,
                     lo_ref, hi_ref, glo_ref, ghi_ref):
    del zlo_ref, zhi_ref, zglo_ref, zghi_ref
    e = pl.program_id(0)
    c = pl.program_id(1)
    idx = e * pl.num_programs(1) + c
    a, has_lo, has_hi = _tile_meta(a_ref[idx])
    first = (c == 0) | (a != _tile_meta(a_ref[jnp.maximum(idx - 1, 0)])[0])

    @pl.when(first)
    def _():
        lo_ref[...] = jnp.zeros_like(lo_ref)
        hi_ref[...] = jnp.zeros_like(hi_ref)
        glo_ref[...] = jnp.zeros_like(glo_ref)
        ghi_ref[...] = jnp.zeros_like(ghi_ref)

    def scatter(tile, x_ref, g_ref):
        tok = tok_ref[0]
        gts = gate_ref[0]
        rows = tile * MOE_CHUNK + lax.broadcasted_iota(jnp.int32, (MOE_CHUNK, MOE_CHUNK), 0)
        c1 = tok[0:1] == rows
        c2 = tok[1:2] == rows
        x_ref[...] += _dot(jnp.where(c1 | c2, 1.0, 0.0).astype(BF16), h_ref[...]).astype(x_ref.dtype)
        g_ref[...] += jnp.sum(jnp.where(c1, gts[0:1], 0.0) + jnp.where(c2, gts[1:2], 0.0), axis=1, keepdims=True)

    @pl.when(has_lo)
    def _():
        scatter(a, lo_ref, glo_ref)

    @pl.when(has_hi)
    def _():
        scatter(a + 1, hi_ref, ghi_ref)


def _dispatch(a_ec, h, tok_rows, gate_rows, n_slots):
    T, D = h.shape
    C = T // MOE_CHUNK
    zx = jnp.zeros((n_slots, D), BF16)
    zg = jnp.zeros((n_slots, 1), F32)
    any_spec = pl.BlockSpec(memory_space=pl.ANY)
    lo = lambda n: pl.BlockSpec((MOE_CHUNK, n), lambda e, c, a: (a[e * C + c] >> 2, 0))
    hi = lambda n: pl.BlockSpec((MOE_CHUNK, n), lambda e, c, a: ((a[e * C + c] >> 2) + 1, 0))
    return pl.pallas_call(
        _dispatch_kernel,
        grid_spec=pltpu.PrefetchScalarGridSpec(
            num_scalar_prefetch=1,
            grid=(N_EXPERTS, C),
            in_specs=[pl.BlockSpec((MOE_CHUNK, D), lambda e, c, a: (c, 0)),
                      pl.BlockSpec((1, 8, MOE_CHUNK), lambda e, c, a: (c, 0, 0)),
                      pl.BlockSpec((1, 8, MOE_CHUNK), lambda e, c, a: (c, 0, 0)),
                      any_spec, any_spec, any_spec, any_spec],
            out_specs=(lo(D), hi(D), lo(1), hi(1))),
        out_shape=(jax.ShapeDtypeStruct(zx.shape, BF16), jax.ShapeDtypeStruct(zx.shape, BF16),
                   jax.ShapeDtypeStruct(zg.shape, F32), jax.ShapeDtypeStruct(zg.shape, F32)),
        input_output_aliases={4: 0, 5: 1, 6: 2, 7: 3},
        compiler_params=_params(("arbitrary", "arbitrary")),
        name="moe_dispatch",
    )(a_ec, h, tok_rows, gate_rows, zx, zx, zg, zg)


def _expert_ffn_kernel(te_ref, nu_ref, lo_ref, hi_ref, glo_ref, ghi_ref, wg_ref, wu_ref, wd_ref, o_ref, acc_ref):
    i = pl.program_id(0)
    f = pl.program_id(1)
    used = i < nu_ref[0]

    @pl.when(used & (f == 0))
    def _():
        acc_ref[...] = jnp.zeros_like(acc_ref)

    @pl.when(used)
    def _():
        x = lo_ref[...] + hi_ref[...]
        act = _silu(_dot(x, wg_ref[0])) * _dot(x, wu_ref[0])
        acc_ref[...] += _dot(act.astype(BF16), wd_ref[0])

    last = f == pl.num_programs(1) - 1

    @pl.when(used & last)
    def _():
        o_ref[...] = (acc_ref[...] * (glo_ref[...] + ghi_ref[...])).astype(o_ref.dtype)

    @pl.when(jnp.logical_not(used) & last)
    def _():
        o_ref[...] = jnp.zeros_like(o_ref)


def _expert_ffn(tile_expert, n_used, xs_lo, xs_hi, gs_lo, gs_hi, wg, wu, wd, tf=512):
    N, D = xs_lo.shape
    F = wg.shape[2]
    nf = F // tf
    tr = MOE_ROW_TILE
    fidx = lambda i, f, te, nu: jnp.where(i < nu[0], f, nf - 1)
    row = lambda n: pl.BlockSpec((tr, n), lambda i, f, te, nu: (i, 0))
    return pl.pallas_call(
        _expert_ffn_kernel,
        grid_spec=pltpu.PrefetchScalarGridSpec(
            num_scalar_prefetch=2,
            grid=(N // tr, nf),
            in_specs=[row(D), row(D), row(1), row(1),
                      pl.BlockSpec((1, D, tf), lambda i, f, te, nu: (te[i], 0, fidx(i, f, te, nu))),
                      pl.BlockSpec((1, D, tf), lambda i, f, te, nu: (te[i], 0, fidx(i, f, te, nu))),
                      pl.BlockSpec((1, tf, D), lambda i, f, te, nu: (te[i], fidx(i, f, te, nu), 0))],
            out_specs=row(D),
            scratch_shapes=[pltpu.VMEM((tr, D), F32)]),
        out_shape=jax.ShapeDtypeStruct((N, D), BF16),
        compiler_params=_params(("arbitrary", "arbitrary")),
        name="moe_expert_ffn",
    )(tile_expert, n_used, xs_lo, xs_hi, gs_lo, gs_hi, wg, wu, wd)


def _combine_kernel(a_ref, x_ref, ylo_ref, yhi_ref, s1_ref, s2_ref, gf_ref, o_ref, acc_ref):
    c = pl.program_id(0)
    e = pl.program_id(1)

    @pl.when(e == 0)
    def _():
        acc_ref[...] = jnp.zeros_like(acc_ref)

    a, has_lo, has_hi = _tile_meta(a_ref[c * pl.num_programs(1) + e])

    def gather(tile, y_ref):
        s1 = s1_ref[...]
        s2 = s2_ref[...]
        parts = []
        for j in range(MOE_CHUNK // LANE):
            cols = tile * MOE_CHUNK + j * LANE + lax.broadcasted_iota(jnp.int32, (MOE_CHUNK, LANE), 1)
            parts.append(jnp.where((s1 == cols) | (s2 == cols), 1.0, 0.0).astype(BF16))
        acc_ref[...] += _dot(jnp.concatenate(parts, axis=1), y_ref[...])

    @pl.when(has_lo)
    def _():
        gather(a, ylo_ref)

    @pl.when(has_hi)
    def _():
        gather(a + 1, yhi_ref)

    @pl.when(e == pl.num_programs(1) - 1)
    def _():
        o_ref[...] = _rms(x_ref[...] + acc_ref[...], gf_ref[...])


def _combine(a_ce, x2, ys, slot1_rep, slot2_rep, g_final):
    T, D = x2.shape
    C = T // MOE_CHUNK
    E = N_EXPERTS
    return pl.pallas_call(
        _combine_kernel,
        grid_spec=pltpu.PrefetchScalarGridSpec(
            num_scalar_prefetch=1,
            grid=(C, E),
            in_specs=[pl.BlockSpec((MOE_CHUNK, D), lambda c, e, a: (c, 0)),
                      pl.BlockSpec((MOE_CHUNK, D), lambda c, e, a: (a[c * E + e] >> 2, 0)),
                      pl.BlockSpec((MOE_CHUNK, D), lambda c, e, a: ((a[c * E + e] >> 2) + 1, 0)),
                      pl.BlockSpec((MOE_CHUNK, LANE), lambda c, e, a: (c, 0)),
                      pl.BlockSpec((MOE_CHUNK, LANE), lambda c, e, a: (c, 0)),
                      pl.BlockSpec((1, D), lambda c, e, a: (0, 0))],
            out_specs=pl.BlockSpec((MOE_CHUNK, D), lambda c, e, a: (c, 0)),
            scratch_shapes=[pltpu.VMEM((MOE_CHUNK, D), F32)]),
        out_shape=jax.ShapeDtypeStruct((T, D), F32),
        compiler_params=_params(("arbitrary", "arbitrary")),
        name="moe_combine_final_norm",
    )(a_ce, x2, ys, ys, slot1_rep, slot2_rep, g_final)


def _rope_tables(S):
    inv = ROPE_THETA ** (-jnp.arange(ROPE_HALF, dtype=F32) / ROPE_HALF)
    ang = jnp.arange(S, dtype=F32)[:, None] * inv[None, :]
    cos, sin = jnp.cos(ang), jnp.sin(ang)
    pad = HEAD_DIM - ROPE_DIM
    cos_h = jnp.concatenate([cos, cos, jnp.ones((S, pad), F32)], axis=-1)
    sin_h = jnp.concatenate([-sin, sin, jnp.zeros((S, pad), F32)], axis=-1)
    reps = LANE // HEAD_DIM
    return jnp.tile(cos_h, (1, reps)), jnp.tile(sin_h, (1, reps))


def _pack_w_in(w):
    per_group = GROUP * N_BRANCH
    gates = w[:, GATE_OFF:GATE_OFF + N_KV_HEADS * per_group].reshape(-1, N_KV_HEADS, per_group)
    gates = jnp.pad(gates, ((0, 0), (0, LANE // GATE_ROWS - N_KV_HEADS), (0, GATE_ROWS - per_group)))
    gates = gates.reshape(-1, LANE)
    return jnp.concatenate([w[:, :GATE_OFF], gates, w[:, GATE_OFF + N_KV_HEADS * per_group:]], axis=1).astype(BF16)


def _mixer_layer(x, norm_g, w_in, pe_k, w1_k, b1_k, w2_k, pe_v, w1_v, b1_v, w2_v,
                 w_pool, pool_scale, w_out, cos_t, sin_t):
    B, S, D = x.shape
    qt, qrt, kcr, vcr, ks, vst, kw, vwt, gates, u = _mixer_proj(
        x, norm_g.reshape(1, D), _pack_w_in(w_in), cos_t, sin_t)
    half = CMP_LEN // 2
    cw = half * HEAD_DIM
    pad_w2 = lambda w2: jnp.pad(w2, ((0, 0), (0, LANE - HEAD_DIM))).astype(BF16)
    kc, vct = _compress(
        kcr, vcr,
        pe_k.reshape(2, cw), w1_k.reshape(2, cw, CMP_HIDDEN).astype(BF16), b1_k.reshape(1, -1), pad_w2(w2_k),
        pe_v.reshape(2, cw), w1_v.reshape(2, cw, CMP_HIDDEN).astype(BF16), b1_v.reshape(1, -1), pad_w2(w2_v))
    a = _nsa_attention(qt, qrt, kc, vct, ks, vst, kw, vwt, gates)
    p = _pool_mixer(u, w_pool.astype(BF16), pool_scale.reshape(1, -1))
    w_out = w_out.astype(BF16)
    out = _out_proj(x.reshape(B * S, D), a.reshape(B * S, NSA_WIDTH), p.reshape(B * S, POOL_WIDTH),
                    w_out[:NSA_WIDTH], w_out[NSA_WIDTH:])
    return out.reshape(B, S, D)


def _moe_layer(x2, g_ffn, router, wg, wu, wd, g_final):
    T, D = x2.shape
    E = N_EXPERTS
    C = T // MOE_CHUNK
    i32 = jnp.int32
    w_router = jnp.pad(router, ((0, 0), (0, LANE - E)))
    h, route, gates, before, total = _router(x2, g_ffn.reshape(1, D), w_router)

    counts = total[0, :E].astype(i32)
    tiles = (counts + MOE_ROW_TILE - 1) // MOE_ROW_TILE
    ends = jnp.cumsum(tiles)
    off = (ends - tiles) * MOE_ROW_TILE
    e1, e2, r1, r2 = route[:, 0], route[:, 1], route[:, 2], route[:, 3]
    slot1 = off[e1] + r1
    slot2 = off[e2] + r2
    first = off[None, :] + before[:, 0, :E].astype(i32)
    after = jnp.concatenate([before[1:, 0, :E], total[:, :E]], axis=0).astype(i32)
    last = off[None, :] + after - 1
    lo_tile = first // MOE_CHUNK
    has_lo = (last >= first).astype(i32)
    has_hi = ((last >= first) & (last // MOE_CHUNK > lo_tile)).astype(i32)
    tile_meta = lo_tile * 4 + has_hi * 2 + has_lo
    n_row_tiles = (2 * T) // MOE_ROW_TILE + E + 1 + MOE_CHUNK // MOE_ROW_TILE
    n_used = ends[-1:]
    tile_ids = jnp.minimum(jnp.arange(n_row_tiles, dtype=i32), n_used[0] - 1)
    tile_expert = jnp.minimum(jnp.sum(ends[None, :] <= tile_ids[:, None], axis=1), E - 1).astype(i32)

    zero = jnp.zeros_like(slot1)
    tok = jnp.stack([slot1, slot2, zero, zero, zero, zero, zero, zero])
    tok_rows = tok.reshape(8, C, MOE_CHUNK).transpose(1, 0, 2)
    gz = jnp.zeros((T,), F32)
    gate_rows = jnp.stack([gates[:, 0], gates[:, 1], gz, gz, gz, gz, gz, gz]).reshape(8, C, MOE_CHUNK).transpose(1, 0, 2)
    slot1_rep = jnp.broadcast_to(slot1[:, None], (T, LANE))
    slot2_rep = jnp.broadcast_to(slot2[:, None], (T, LANE))

    xs_lo, xs_hi, gs_lo, gs_hi = _dispatch(tile_meta.T.reshape(-1).astype(i32), h, tok_rows, gate_rows,
                                           n_row_tiles * MOE_ROW_TILE)
    ys = _expert_ffn(tile_expert, n_used.astype(i32), xs_lo, xs_hi, gs_lo, gs_hi,
                     wg.astype(BF16), wu.astype(BF16), wd.astype(BF16))
    return _combine(tile_meta.reshape(-1).astype(i32), x2, ys, slot1_rep, slot2_rep, g_final.reshape(1, D))


def kernel(x, mem, norm_mix, w_in, cmp_pe_k, cmp_w1_k, cmp_b1_k, cmp_w2_k, cmp_pe_v, cmp_w1_v, cmp_b1_v, cmp_w2_v, w_pool, pool_scale, w_out, norm_x, norm_mem, wq_x, wk_x, wv_x, wo_x, norm_ffn, ffn_wg, ffn_wu, ffn_wd, moe_router, moe_wg, moe_wu, moe_wd, norm_final):
    B, S, D = x.shape
    depth = norm_mix.shape[0]
    assert depth == 2, "the final RMSNorm is fused into the expert layer, which must be the last one"
    cos_t, sin_t = _rope_tables(S)
    for layer in range(depth):
        x = _mixer_layer(x, norm_mix[layer], w_in[layer],
                         cmp_pe_k[layer], cmp_w1_k[layer], cmp_b1_k[layer], cmp_w2_k[layer],
                         cmp_pe_v[layer], cmp_w1_v[layer], cmp_b1_v[layer], cmp_w2_v[layer],
                         w_pool[layer], pool_scale[layer], w_out[layer], cos_t, sin_t)
        wkv = jnp.concatenate([wk_x[layer], wv_x[layer]], axis=1).astype(BF16)
        kx, vx = _xattn_kv(mem, norm_mem[layer].reshape(1, D), wkv)
        x = _xattn(x, norm_x[layer].reshape(1, D), wq_x[layer].astype(BF16), kx, vx, wo_x[layer].astype(BF16))
        x2 = x.reshape(B * S, D)
        j = layer // 2
        if layer % 2 == 0:
            x2 = _ffn(x2, norm_ffn[layer].reshape(1, D), ffn_wg[j].astype(BF16), ffn_wu[j].astype(BF16),
                      ffn_wd[j].astype(BF16))
        else:
            x2 = _moe_layer(x2, norm_ffn[layer], moe_router[j], moe_wg[j], moe_wu[j], moe_wd[j], norm_final)
        x = x2.reshape(B, S, D)
    return x
```

```python
import functools
import math

import numpy as np
import jax
import jax.numpy as jnp
from jax import lax
from jax.experimental import pallas as pl
from jax.experimental.pallas import tpu as pltpu

F32 = jnp.float32
BF16 = jnp.bfloat16

D_MODEL = 1024
HEAD_DIM = 64
N_HEADS = 8
N_KV_HEADS = 2
GROUP = N_HEADS // N_KV_HEADS
NSA_WIDTH = N_HEADS * HEAD_DIM
KV_WIDTH = N_KV_HEADS * HEAD_DIM
N_BRANCH = 3
GATE_ROWS = 16
POOL_WIDTH = 512
POOL_WINDOWS = (2, 4, 8, 16)
POOL_GROUP = 128
POOL_HALO = 16
ROPE_DIM = 16
ROPE_HALF = 8
ROPE_THETA = 500000.0
CMP_LEN = 32
CMP_STRIDE = 16
CMP_HIDDEN = 256
SEL_BLOCK = 64
SEL_TOP_N = 16
N_LOCAL_BLOCKS = 2
WINDOW = 512
X_HEADS = 4
X_HEAD_DIM = 256
N_EXPERTS = 8
EPS = 1e-6

LANE = 128
IN_PACKED = NSA_WIDTH + 6 * KV_WIDTH + LANE + POOL_WIDTH
GATE_OFF = NSA_WIDTH + 6 * KV_WIDTH
POOL_OFF = GATE_OFF + LANE

NEG = -1e30
TINY = float(np.finfo(np.float32).tiny)
VMEM_LIMIT = 56 * 1024 * 1024


def _dot(a, b):
    return jnp.dot(a, b, preferred_element_type=F32)


def _dot_nt(a, b):
    return lax.dot_general(a, b, (((1,), (1,)), ((), ())), preferred_element_type=F32)


def _rms(x, g):
    y = x * lax.rsqrt(jnp.mean(x * x, axis=-1, keepdims=True) + EPS)
    return y * g


def _params(sem, limit=VMEM_LIMIT):
    return pltpu.CompilerParams(dimension_semantics=sem, vmem_limit_bytes=limit)


def _mixer_proj_kernel(x_ref, g_ref, w_ref, cos_ref, sin_ref,
                       qt_ref, qrt_ref, kcr_ref, vcr_ref, ks_ref, vst_ref, kw_ref, vwt_ref,
                       gate_ref, u_ref):
    h = _rms(x_ref[0], g_ref[...]).astype(BF16)
    z = _dot(h, w_ref[...])
    tm = z.shape[0]
    cos = cos_ref[...]
    sin = sin_ref[...]
    lane = lax.broadcasted_iota(jnp.int32, (tm, LANE), 1)
    first = (lane & (HEAD_DIM - 1)) < ROPE_HALF
    scale = HEAD_DIM ** -0.5 * math.log2(math.e)

    def rope(xs):
        partner = jnp.where(first, pltpu.roll(xs, LANE - ROPE_HALF, 1), pltpu.roll(xs, ROPE_HALF, 1))
        return xs * cos + partner * sin

    for s in range(NSA_WIDTH // LANE):
        xs = z[:, s * LANE:(s + 1) * LANE]
        for src, ref in ((xs, qt_ref), (rope(xs), qrt_ref)):
            t = (src * scale).T.astype(BF16)
            ref[0, 2 * s] = t[:HEAD_DIM]
            ref[0, 2 * s + 1] = t[HEAD_DIM:]

    def kv_slab(i):
        return z[:, NSA_WIDTH + i * KV_WIDTH:NSA_WIDTH + (i + 1) * KV_WIDTH]

    for slab, ref in ((kv_slab(0), kcr_ref), (kv_slab(1), vcr_ref),
                      (rope(kv_slab(2)), ks_ref), (rope(kv_slab(4)), kw_ref)):
        for gg in range(N_KV_HEADS):
            ref[0, gg] = slab[:, gg * HEAD_DIM:(gg + 1) * HEAD_DIM].astype(ref.dtype)

    for slab, ref in ((kv_slab(3), vst_ref), (kv_slab(5), vwt_ref)):
        t = slab.T.astype(BF16)
        for gg in range(N_KV_HEADS):
            for k in range(tm // LANE):
                ref[0, gg, k] = t[gg * HEAD_DIM:(gg + 1) * HEAD_DIM, k * LANE:(k + 1) * LANE]

    sig_t = jax.nn.sigmoid(z[:, GATE_OFF:GATE_OFF + LANE]).T
    for gg in range(N_KV_HEADS):
        gate_ref[0, gg] = sig_t[gg * GATE_ROWS:(gg + 1) * GATE_ROWS]
    u_ref[0] = z[:, POOL_OFF:POOL_OFF + POOL_WIDTH]


def _mixer_proj(x, g, w_packed, cos_t, sin_t, tm=512):
    B, S, D = x.shape
    G = N_KV_HEADS
    sd = jax.ShapeDtypeStruct
    out_shape = (sd((B, N_HEADS, HEAD_DIM, S), BF16), sd((B, N_HEADS, HEAD_DIM, S), BF16),
                 sd((B, G, S, HEAD_DIM), F32), sd((B, G, S, HEAD_DIM), F32),
                 sd((B, G, S, HEAD_DIM), BF16), sd((B, G, S // LANE, HEAD_DIM, LANE), BF16),
                 sd((B, G, S, HEAD_DIM), BF16), sd((B, G, S // LANE, HEAD_DIM, LANE), BF16),
                 sd((B, G, GATE_ROWS, S), F32),
                 sd((B, S, POOL_WIDTH), F32))
    qspec = pl.BlockSpec((1, N_HEADS, HEAD_DIM, tm), lambda b, i: (b, 0, 0, i))
    kspec = pl.BlockSpec((1, G, tm, HEAD_DIM), lambda b, i: (b, 0, i, 0))
    vspec = pl.BlockSpec((1, G, tm // LANE, HEAD_DIM, LANE), lambda b, i: (b, 0, i, 0, 0))
    out_specs = (qspec, qspec, kspec, kspec, kspec, vspec, kspec, vspec,
                 pl.BlockSpec((1, G, GATE_ROWS, tm), lambda b, i: (b, 0, 0, i)),
                 pl.BlockSpec((1, tm, POOL_WIDTH), lambda b, i: (b, i, 0)))
    return pl.pallas_call(
        _mixer_proj_kernel,
        grid=(B, S // tm),
        in_specs=[pl.BlockSpec((1, tm, D), lambda b, i: (b, i, 0)),
                  pl.BlockSpec((1, D), lambda b, i: (0, 0)),
                  pl.BlockSpec((D, IN_PACKED), lambda b, i: (0, 0)),
                  pl.BlockSpec((tm, LANE), lambda b, i: (i, 0)),
                  pl.BlockSpec((tm, LANE), lambda b, i: (i, 0))],
        out_specs=out_specs,
        out_shape=out_shape,
        compiler_params=_params(("parallel", "parallel")),
        name="mixer_proj",
    )(x, g, w_packed, cos_t, sin_t)


def _gelu_tanh(x):
    return 0.5 * x * (1.0 + jnp.tanh(math.sqrt(2.0 / math.pi) * (x + 0.044715 * (x * x * x))))


def _compress_kernel(kc_ref, vc_ref, pek_ref, w1k_ref, b1k_ref, w2k_ref,
                     pev_ref, w1v_ref, b1v_ref, w2v_ref, ko_ref, vto_ref):
    def comp(x_ref, pe_ref, w1_ref, b1_ref, w2_ref):
        x = x_ref[0, 0]
        n = x.shape[0]
        a = _dot((x + pe_ref[0:1, :]).astype(BF16), w1_ref[0])
        b = _dot((x + pe_ref[1:2, :]).astype(BF16), w1_ref[1])
        pre = a + pltpu.roll(b, n - 1, 0) + b1_ref[...]
        return _dot(_gelu_tanh(pre).astype(BF16), w2_ref[...])

    ko_ref[0, 0] = comp(kc_ref, pek_ref, w1k_ref, b1k_ref, w2k_ref)[:, :HEAD_DIM].astype(ko_ref.dtype)
    vto_ref[0, 0] = comp(vc_ref, pev_ref, w1v_ref, b1v_ref, w2v_ref).T[:HEAD_DIM].astype(vto_ref.dtype)


def _compress(kcr, vcr, pek, w1k, b1k, w2k, pev, w1v, b1v, w2v):
    B, G, S, _ = kcr.shape
    n = S // CMP_STRIDE
    cw = CMP_STRIDE * HEAD_DIM
    kc = kcr.reshape(B, G, n, cw)
    vc = vcr.reshape(B, G, n, cw)
    xspec = pl.BlockSpec((1, 1, n, cw), lambda b, g: (b, g, 0, 0))
    full = lambda a: pl.BlockSpec(a.shape, lambda b, g: (0,) * a.ndim)
    return pl.pallas_call(
        _compress_kernel,
        grid=(B, G),
        in_specs=[xspec, xspec, full(pek), full(w1k), full(b1k), full(w2k),
                  full(pev), full(w1v), full(b1v), full(w2v)],
        out_specs=(pl.BlockSpec((1, 1, n, HEAD_DIM), lambda b, g: (b, g, 0, 0)),
                   pl.BlockSpec((1, 1, HEAD_DIM, n), lambda b, g: (b, g, 0, 0))),
        out_shape=(jax.ShapeDtypeStruct((B, G, n, HEAD_DIM), BF16),
                   jax.ShapeDtypeStruct((B, G, HEAD_DIM, n), BF16)),
        compiler_params=_params(("parallel", "parallel")),
        name="compress_kv",
    )(kc, vc, pek, w1k, b1k, w2k, pev, w1v, b1v, w2v)


def _nsa_kernel(qt_ref, qrt_ref, kc_ref, vct_ref, ks_ref, vst_ref, kw_ref, vwt_ref, gate_ref,
                covt_ref, o_ref, selbias_ref, *, tq, tk):
    qt = pl.program_id(2)
    q0 = qt * tq
    n_cmp = kc_ref.shape[2]
    n_blk = covt_ref.shape[0]
    heads = lambda ref: jnp.concatenate([ref[0, hh] for hh in range(GROUP)], axis=1)
    q_t = heads(qt_ref)
    qr_t = heads(qrt_ref)
    t_lane = q0 + lax.broadcasted_iota(jnp.int32, (1, tq), 1)
    per_head = lambda a: [a[:, hh * tq:(hh + 1) * tq] for hh in range(GROUP)]
    all_heads = lambda a: jnp.concatenate([a] * GROUP, axis=1)

    sc = _dot(kc_ref[0, 0], q_t)
    cmp_end = lax.broadcasted_iota(jnp.int32, (n_cmp, 1), 0) * CMP_STRIDE + (CMP_LEN - 1)
    valid = cmp_end <= t_lane
    pcs = []
    for s_h in per_head(sc):
        s_h = jnp.where(valid, s_h, NEG)
        e = jnp.where(valid, jnp.exp2(s_h - jnp.max(s_h, axis=0, keepdims=True)), 0.0)
        pcs.append(e * (1.0 / jnp.maximum(jnp.sum(e, axis=0, keepdims=True), TINY)))
    oc_t = _dot(vct_ref[0, 0], jnp.concatenate(pcs, axis=1).astype(BF16))

    psum = pcs[0] + pcs[1] + pcs[2] + pcs[3]
    p_hi = psum.astype(BF16)
    r1 = psum - p_hi.astype(F32)
    p_mid = r1.astype(BF16)
    p_lo = (r1 - p_mid.astype(F32)).astype(BF16)
    covt = covt_ref[...]
    imp_t = _dot(covt, p_hi) + _dot(covt, p_mid) + _dot(covt, p_lo)

    jb = lax.broadcasted_iota(jnp.int32, (n_blk, tq), 0)
    tb = (q0 + lax.broadcasted_iota(jnp.int32, (n_blk, tq), 1)) // SEL_BLOCK
    dist = tb - jb
    forced = (jb == 0) | ((dist >= 0) & (dist < N_LOCAL_BLOCKS))
    score = jnp.where(jb > tb, -jnp.inf, jnp.where(forced, jnp.inf, imp_t))
    rank = jnp.zeros((n_blk, tq), jnp.int32)
    for jp in range(n_blk):
        rowv = score[jp:jp + 1, :]
        beats = (rowv > score) | ((rowv == score) & (jb > jp))
        rank = rank + jnp.where(beats, 1, 0)
    selbias_ref[...] = jnp.where(rank < min(SEL_TOP_N, n_blk), 0.0, NEG)

    R = GROUP * tq

    def with_ones(v_t):
        extra = jnp.where(lax.broadcasted_iota(jnp.int32, (16, v_t.shape[1]), 0) == 0, 1.0, 0.0)
        return jnp.concatenate([v_t, extra.astype(BF16)], axis=0)

    bpc = tk // SEL_BLOCK
    vpc = tk // LANE

    def sel_chunk(c, carry, causal):
        m_i, l_i, acc = carry
        start = pl.multiple_of(c * tk, tk)
        k_c = ks_ref[0, 0, pl.ds(start, tk), :]
        v_t = jnp.concatenate([vst_ref[0, 0, c * vpc + k] for k in range(vpc)], axis=1)
        rows = [jnp.broadcast_to(selbias_ref[pl.ds(c * bpc + j, 1), :], (SEL_BLOCK, tq)) for j in range(bpc)]
        bias = jnp.concatenate(rows, axis=0)
        if causal:
            kpos = start + lax.broadcasted_iota(jnp.int32, (tk, 1), 0)
            bias = jnp.where(kpos <= t_lane, bias, NEG)
        s = _dot(k_c, qr_t) + all_heads(bias)
        m_new = jnp.maximum(m_i, jnp.max(s, axis=0, keepdims=True))
        alpha = jnp.exp2(m_i - m_new)
        pv = _dot(with_ones(v_t), jnp.exp2(s - m_new).astype(BF16))
        return m_new, alpha * l_i + pv[HEAD_DIM:HEAD_DIM + 1], alpha * acc + pv[:HEAD_DIM]

    init = (jnp.full((1, R), NEG, F32), jnp.zeros((1, R), F32), jnp.zeros((HEAD_DIM, R), F32))
    c_diag = q0 // tk
    carry = lax.fori_loop(0, c_diag, lambda c, cr: sel_chunk(c, cr, False), init)
    _, l_s, acc_s = sel_chunk(c_diag, carry, True)
    os_t = acc_s * (1.0 / jnp.maximum(l_s, TINY))

    wk = WINDOW + tq
    w0 = pl.multiple_of(jnp.maximum(q0 - WINDOW, 0), LANE)
    k_w = kw_ref[0, 0, pl.ds(w0, wk), :]
    v_t = jnp.concatenate([vwt_ref[0, 0, w0 // LANE + k] for k in range(wk // LANE)], axis=1)
    diff = t_lane - (w0 + lax.broadcasted_iota(jnp.int32, (wk, 1), 0))
    bias = jnp.where((diff >= 0) & (diff < WINDOW), 0.0, NEG)
    s = _dot(k_w, qr_t) + all_heads(bias)
    pv = _dot(with_ones(v_t), jnp.exp2(s - jnp.max(s, axis=0, keepdims=True)).astype(BF16))
    ow_t = pv[:HEAD_DIM] * (1.0 / pv[HEAD_DIM:HEAD_DIM + 1])

    gt = gate_ref[0, 0]
    outs = []
    for hh, (a, b, c) in enumerate(zip(per_head(oc_t), per_head(os_t), per_head(ow_t))):
        r = hh * N_BRANCH
        outs.append(gt[r:r + 1] * a + gt[r + 1:r + 2] * b + gt[r + 2:r + 3] * c)
    o_ref[0] = jnp.concatenate(outs, axis=0).T.astype(o_ref.dtype)


def _cover_table(S):
    n_cmp = S // CMP_STRIDE
    n_blk = S // SEL_BLOCK
    cs = np.arange(n_cmp) * CMP_STRIDE
    ss = np.arange(n_blk) * SEL_BLOCK
    cover_t = ((cs[None, :] < ss[:, None] + SEL_BLOCK) & (cs[None, :] + CMP_LEN > ss[:, None]))
    cover_t[:, n_cmp - 1] = False
    return jnp.asarray(cover_t, BF16)


def _nsa_attention(qt, qrt, kc, vct, ks, vst, kw, vwt, gates, tq=256, tk=1024):
    B, _, _, S = qt.shape
    tk = min(tk, S)
    assert tq % LANE == 0 and tk % tq == 0 and S % tk == 0 and S >= WINDOW + tq
    covt = _cover_table(S)
    n_cmp = kc.shape[2]
    n_blk = covt.shape[0]
    qspec = pl.BlockSpec((1, GROUP, HEAD_DIM, tq), lambda b, g, i: (b, g, 0, i))
    kspec = lambda a: pl.BlockSpec((1, 1) + a.shape[2:], lambda b, g, i: (b, g, 0, 0))
    vspec = pl.BlockSpec((1, 1, S // LANE, HEAD_DIM, LANE), lambda b, g, i: (b, g, 0, 0, 0))
    return pl.pallas_call(
        functools.partial(_nsa_kernel, tq=tq, tk=tk),
        grid=(B, N_KV_HEADS, S // tq),
        in_specs=[qspec, qspec,
                  pl.BlockSpec((1, 1, n_cmp, HEAD_DIM), lambda b, g, i: (b, g, 0, 0)),
                  pl.BlockSpec((1, 1, HEAD_DIM, n_cmp), lambda b, g, i: (b, g, 0, 0)),
                  kspec(ks), vspec, kspec(kw), vspec,
                  pl.BlockSpec((1, 1, GATE_ROWS, tq), lambda b, g, i: (b, g, 0, i)),
                  pl.BlockSpec(covt.shape, lambda b, g, i: (0, 0))],
        out_specs=pl.BlockSpec((1, tq, GROUP * HEAD_DIM), lambda b, g, i: (b, i, g)),
        out_shape=jax.ShapeDtypeStruct((B, S, NSA_WIDTH), BF16),
        scratch_shapes=[pltpu.VMEM((n_blk, tq), F32)],
        compiler_params=_params(("parallel", "parallel", "parallel")),
        name="nsa_attention",
    )(qt, qrt, kc, vct, ks, vst, kw, vwt, gates, covt)


def _pool_kernel(u_ref, w_ref, scale_ref, o_ref, buf):
    i = pl.program_id(1)
    tm = u_ref.shape[1]

    @pl.when(i == 0)
    def _():
        buf[0:POOL_HALO, :] = jnp.zeros((POOL_HALO, POOL_WIDTH), F32)

    @pl.when(i > 0)
    def _():
        buf[0:POOL_HALO, :] = buf[tm:tm + POOL_HALO, :]

    buf[POOL_HALO:POOL_HALO + tm, :] = u_ref[0]
    t1 = i * tm + lax.broadcasted_iota(jnp.int32, (tm, 1), 0) + 1
    for gi, w in enumerate(POOL_WINDOWS):
        cols = slice(gi * POOL_GROUP, (gi + 1) * POOL_GROUP)
        cur = buf[POOL_HALO:POOL_HALO + tm, cols]
        tot = cur
        for k in range(1, w):
            tot = tot + buf[POOL_HALO - k:POOL_HALO - k + tm, cols]
        cnt = jnp.minimum(t1, w).astype(F32)
        d = tot / cnt - cur
        y = _dot(d.astype(BF16), w_ref[gi])
        o_ref[0, :, cols] = (y * scale_ref[:, cols]).astype(o_ref.dtype)


def _pool_mixer(u, w_pool, pool_scale, tm=512):
    B, S, _ = u.shape
    return pl.pallas_call(
        _pool_kernel,
        grid=(B, S // tm),
        in_specs=[pl.BlockSpec((1, tm, POOL_WIDTH), lambda b, i: (b, i, 0)),
                  pl.BlockSpec(w_pool.shape, lambda b, i: (0, 0, 0)),
                  pl.BlockSpec((1, POOL_WIDTH), lambda b, i: (0, 0))],
        out_specs=pl.BlockSpec((1, tm, POOL_WIDTH), lambda b, i: (b, i, 0)),
        out_shape=jax.ShapeDtypeStruct((B, S, POOL_WIDTH), BF16),
        scratch_shapes=[pltpu.VMEM((tm + POOL_HALO, POOL_WIDTH), F32)],
        compiler_params=_params(("parallel", "arbitrary")),
        name="pool_mixer",
    )(u, w_pool, pool_scale)


def _out_proj_kernel(x_ref, a_ref, p_ref, wa_ref, wp_ref, o_ref):
    o_ref[...] = x_ref[...] + _dot(a_ref[...], wa_ref[...]) + _dot(p_ref[...], wp_ref[...])


def _out_proj(x2, a2, p2, wa, wp, tm=512):
    T, D = x2.shape
    row = lambda n: pl.BlockSpec((tm, n), lambda i: (i, 0))
    full = lambda a: pl.BlockSpec(a.shape, lambda i: (0, 0))
    return pl.pallas_call(
        _out_proj_kernel,
        grid=(T // tm,),
        in_specs=[row(D), row(a2.shape[1]), row(p2.shape[1]), full(wa), full(wp)],
        out_specs=row(D),
        out_shape=jax.ShapeDtypeStruct((T, D), F32),
        compiler_params=_params(("parallel",)),
        name="mixer_out_proj",
    )(x2, a2, p2, wa, wp)


def _xkv_kernel(m_ref, g_ref, w_ref, k_ref, v_ref):
    h = _rms(m_ref[0], g_ref[...]).astype(BF16)
    kv = _dot(h, w_ref[...])
    d = k_ref.shape[2]
    k_ref[0] = kv[:, :d].astype(k_ref.dtype)
    v_ref[0] = kv[:, d:].astype(v_ref.dtype)


def _xattn_kv(mem, g, wkv):
    B, M, D = mem.shape
    ospec = pl.BlockSpec((1, M, D), lambda b: (b, 0, 0))
    oshape = jax.ShapeDtypeStruct((B, M, D), BF16)
    return pl.pallas_call(
        _xkv_kernel,
        grid=(B,),
        in_specs=[pl.BlockSpec((1, M, D), lambda b: (b, 0, 0)),
                  pl.BlockSpec((1, D), lambda b: (0, 0)),
                  pl.BlockSpec(wkv.shape, lambda b: (0, 0))],
        out_specs=(ospec, ospec),
        out_shape=(oshape, oshape),
        compiler_params=_params(("parallel",)),
        name="xattn_kv",
    )(mem, g, wkv)


def _xattn_kernel(x_ref, g_ref, wq_ref, k_ref, v_ref, wo_ref, o_ref):
    x = x_ref[0]
    h = _rms(x, g_ref[...]).astype(BF16)
    scale = X_HEAD_DIM ** -0.5
    q = (_dot(h, wq_ref[...]) * scale).astype(BF16)
    outs = []
    for hd in range(X_HEADS):
        sl = slice(hd * X_HEAD_DIM, (hd + 1) * X_HEAD_DIM)
        s = _dot_nt(q[:, sl], k_ref[0, :, sl])
        e = jnp.exp(s - jnp.max(s, axis=-1, keepdims=True))
        p = e / jnp.sum(e, axis=-1, keepdims=True)
        outs.append(_dot(p.astype(BF16), v_ref[0, :, sl]).astype(BF16))
    o = jnp.concatenate(outs, axis=-1)
    o_ref[0] = x + _dot(o, wo_ref[...])


def _xattn(x, g, wq, kx, vx, wo, tm=512):
    B, S, D = x.shape
    M = kx.shape[1]
    xspec = pl.BlockSpec((1, tm, D), lambda b, i: (b, i, 0))
    full = lambda a: pl.BlockSpec(a.shape, lambda b, i: (0, 0))
    mspec = pl.BlockSpec((1, M, D), lambda b, i: (b, 0, 0))
    return pl.pallas_call(
        _xattn_kernel,
        grid=(B, S // tm),
        in_specs=[xspec, full(g), full(wq), mspec, mspec, full(wo)],
        out_specs=xspec,
        out_shape=jax.ShapeDtypeStruct((B, S, D), F32),
        compiler_params=_params(("parallel", "parallel")),
        name="xattn",
    )(x, g, wq, kx, vx, wo)


def _silu(x):
    return x * jax.nn.sigmoid(x)


def _ffn_kernel(x_ref, g_ref, wg_ref, wu_ref, wd_ref, o_ref, h_ref, acc_ref):
    f = pl.program_id(1)

    @pl.when(f == 0)
    def _():
        h_ref[...] = _rms(x_ref[...], g_ref[...]).astype(BF16)
        acc_ref[...] = jnp.zeros_like(acc_ref)

    h = h_ref[...]
    act = _silu(_dot(h, wg_ref[...])) * _dot(h, wu_ref[...])
    acc_ref[...] += _dot(act.astype(BF16), wd_ref[...])

    @pl.when(f == pl.num_programs(1) - 1)
    def _():
        o_ref[...] = x_ref[...] + acc_ref[...]


def _ffn(x2, g, wg, wu, wd, tm=512, tf=1408):
    T, D = x2.shape
    F = wg.shape[1]
    return pl.pallas_call(
        _ffn_kernel,
        grid=(T // tm, F // tf),
        in_specs=[pl.BlockSpec((tm, D), lambda i, f: (i, 0)),
                  pl.BlockSpec((1, D), lambda i, f: (0, 0)),
                  pl.BlockSpec((D, tf), lambda i, f: (0, f)),
                  pl.BlockSpec((D, tf), lambda i, f: (0, f)),
                  pl.BlockSpec((tf, D), lambda i, f: (f, 0))],
        out_specs=pl.BlockSpec((tm, D), lambda i, f: (i, 0)),
        out_shape=jax.ShapeDtypeStruct((T, D), F32),
        scratch_shapes=[pltpu.VMEM((tm, D), BF16), pltpu.VMEM((tm, D), F32)],
        compiler_params=_params(("parallel", "arbitrary")),
        name="dense_swiglu",
    )(x2, g, wg, wu, wd)


MOE_CHUNK = 512
MOE_ROW_TILE = 512
assert MOE_ROW_TILE % MOE_CHUNK == 0


def _tile_meta(m):
    return m >> 2, (m & 1) != 0, (m & 2) != 0


def _router_kernel(x_ref, g_ref, wr_ref, tri_ref, h_ref, route_ref, gate_ref, before_ref, total_ref, run_ref):
    c = pl.program_id(0)

    @pl.when(c == 0)
    def _():
        run_ref[...] = jnp.zeros_like(run_ref)

    hf = _rms(x_ref[...], g_ref[...])
    h_ref[...] = hf.astype(BF16)
    logits = jnp.dot(hf, wr_ref[...], preferred_element_type=F32, precision=lax.Precision.HIGHEST)
    lane = lax.broadcasted_iota(jnp.int32, logits.shape, 1)
    logits = jnp.where(lane < N_EXPERTS, logits, -jnp.inf)
    v1 = jnp.max(logits, axis=-1, keepdims=True)
    i1 = jnp.min(jnp.where(logits == v1, lane, LANE), axis=-1, keepdims=True)
    rest = jnp.where(lane == i1, -jnp.inf, logits)
    v2 = jnp.max(rest, axis=-1, keepdims=True)
    i2 = jnp.min(jnp.where(rest == v2, lane, LANE), axis=-1, keepdims=True)
    e2 = jnp.exp(v2 - v1)
    den = 1.0 + e2
    gate_ref[...] = jnp.where(lane == 0, 1.0 / den, jnp.where(lane == 1, e2 / den, 0.0))

    onehot = jnp.where((lane == i1) | (lane == i2), 1.0, 0.0)
    run = run_ref[...]
    rank = run + _dot(tri_ref[...], onehot.astype(BF16))
    r1 = jnp.sum(jnp.where(lane == i1, rank, 0.0), axis=-1, keepdims=True).astype(jnp.int32)
    r2 = jnp.sum(jnp.where(lane == i2, rank, 0.0), axis=-1, keepdims=True).astype(jnp.int32)
    route_ref[...] = jnp.where(lane == 0, i1, jnp.where(lane == 1, i2,
                               jnp.where(lane == 2, r1, jnp.where(lane == 3, r2, 0))))
    before_ref[0] = run
    run = run + jnp.sum(onehot, axis=0, keepdims=True)
    run_ref[...] = run
    total_ref[...] = run


def _router(x2, g, w_router):
    T, D = x2.shape
    C = T // MOE_CHUNK
    tri = jnp.asarray(np.tril(np.ones((MOE_CHUNK, MOE_CHUNK), np.float32), -1), BF16)
    row = lambda n: pl.BlockSpec((MOE_CHUNK, n), lambda c: (c, 0))
    full = lambda a: pl.BlockSpec(a.shape, lambda c: (0, 0))
    return pl.pallas_call(
        _router_kernel,
        grid=(C,),
        in_specs=[row(D), full(g), full(w_router), full(tri)],
        out_specs=(row(D), row(LANE), row(LANE),
                   pl.BlockSpec((1, 1, LANE), lambda c: (c, 0, 0)),
                   pl.BlockSpec((1, LANE), lambda c: (0, 0))),
        out_shape=(jax.ShapeDtypeStruct((T, D), BF16),
                   jax.ShapeDtypeStruct((T, LANE), jnp.int32),
                   jax.ShapeDtypeStruct((T, LANE), F32),
                   jax.ShapeDtypeStruct((C, 1, LANE), F32),
                   jax.ShapeDtypeStruct((1, LANE), F32)),
        scratch_shapes=[pltpu.VMEM((1, LANE), F32)],
        compiler_params=_params(("arbitrary",)),
        name="moe_router",
    )(x2, g, w_router, tri)


def _dispatch_kernel(a_ref, h_ref, tok_ref, gate_ref, zlo_ref, zhi_ref, zglo_ref, zghi_ref,
                     lo_ref, hi_ref, glo_ref, ghi_ref):
    del zlo_ref, zhi_ref, zglo_ref, zghi_ref
    e = pl.program_id(0)
    c = pl.program_id(1)
    idx = e * pl.num_programs(1) + c
    a, has_lo, has_hi = _tile_meta(a_ref[idx])
    first = (c == 0) | (a != _tile_meta(a_ref[jnp.maximum(idx - 1, 0)])[0])

    @pl.when(first)
    def _():
        lo_ref[...] = jnp.zeros_like(lo_ref)
        hi_ref[...] = jnp.zeros_like(hi_ref)
        glo_ref[...] = jnp.zeros_like(glo_ref)
        ghi_ref[...] = jnp.zeros_like(ghi_ref)

    def scatter(tile, x_ref, g_ref):
        tok = tok_ref[0]
        gts = gate_ref[0]
        rows = tile * MOE_CHUNK + lax.broadcasted_iota(jnp.int32, (MOE_CHUNK, MOE_CHUNK), 0)
        c1 = tok[0:1] == rows
        c2 = tok[1:2] == rows
        x_ref[...] += _dot(jnp.where(c1 | c2, 1.0, 0.0).astype(BF16), h_ref[...]).astype(x_ref.dtype)
        g_ref[...] += jnp.sum(jnp.where(c1, gts[0:1], 0.0) + jnp.where(c2, gts[1:2], 0.0), axis=1, keepdims=True)

    @pl.when(has_lo)
    def _():
        scatter(a, lo_ref, glo_ref)

    @pl.when(has_hi)
    def _():
        scatter(a + 1, hi_ref, ghi_ref)


def _dispatch(a_ec, h, tok_rows, gate_rows, n_slots):
    T, D = h.shape
    C = T // MOE_CHUNK
    zx = jnp.zeros((n_slots, D), BF16)
    zg = jnp.zeros((n_slots, 1), F32)
    any_spec = pl.BlockSpec(memory_space=pl.ANY)
    lo = lambda n: pl.BlockSpec((MOE_CHUNK, n), lambda e, c, a: (a[e * C + c] >> 2, 0))
    hi = lambda n: pl.BlockSpec((MOE_CHUNK, n), lambda e, c, a: ((a[e * C + c] >> 2) + 1, 0))
    return pl.pallas_call(
        _dispatch_kernel,
        grid_spec=pltpu.PrefetchScalarGridSpec(
            num_scalar_prefetch=1,
            grid=(N_EXPERTS, C),
            in_specs=[pl.BlockSpec((MOE_CHUNK, D), lambda e, c, a: (c, 0)),
                      pl.BlockSpec((1, 8, MOE_CHUNK), lambda e, c, a: (c, 0, 0)),
                      pl.BlockSpec((1, 8, MOE_CHUNK), lambda e, c, a: (c, 0, 0)),
                      any_spec, any_spec, any_spec, any_spec],
            out_specs=(lo(D), hi(D), lo(1), hi(1))),
        out_shape=(jax.ShapeDtypeStruct(zx.shape, BF16), jax.ShapeDtypeStruct(zx.shape, BF16),
                   jax.ShapeDtypeStruct(zg.shape, F32), jax.ShapeDtypeStruct(zg.shape, F32)),
        input_output_aliases={4: 0, 5: 1, 6: 2, 7: 3},
        compiler_params=_params(("arbitrary", "arbitrary")),
        name="moe_dispatch",
    )(a_ec, h, tok_rows, gate_rows, zx, zx, zg, zg)


def _expert_ffn_kernel(te_ref, nu_ref, lo_ref, hi_ref, glo_ref, ghi_ref, wg_ref, wu_ref, wd_ref, o_ref, acc_ref):
    i = pl.program_id(0)
    f = pl.program_id(1)
    used = i < nu_ref[0]

    @pl.when(used & (f == 0))
    def _():
        acc_ref[...] = jnp.zeros_like(acc_ref)

    @pl.when(used)
    def _():
        x = lo_ref[...] + hi_ref[...]
        act = _silu(_dot(x, wg_ref[0])) * _dot(x, wu_ref[0])
        acc_ref[...] += _dot(act.astype(BF16), wd_ref[0])

    last = f == pl.num_programs(1) - 1

    @pl.when(used & last)
    def _():
        o_ref[...] = (acc_ref[...] * (glo_ref[...] + ghi_ref[...])).astype(o_ref.dtype)

    @pl.when(jnp.logical_not(used) & last)
    def _():
        o_ref[...] = jnp.zeros_like(o_ref)


def _expert_ffn(tile_expert, n_used, xs_lo, xs_hi, gs_lo, gs_hi, wg, wu, wd, tf=512):
    N, D = xs_lo.shape
    F = wg.shape[2]
    nf = F // tf
    tr = MOE_ROW_TILE
    fidx = lambda i, f, te, nu: jnp.where(i < nu[0], f, nf - 1)
    row = lambda n: pl.BlockSpec((tr, n), lambda i, f, te, nu: (i, 0))
    return pl.pallas_call(
        _expert_ffn_kernel,
        grid_spec=pltpu.PrefetchScalarGridSpec(
            num_scalar_prefetch=2,
            grid=(N // tr, nf),
            in_specs=[row(D), row(D), row(1), row(1),
                      pl.BlockSpec((1, D, tf), lambda i, f, te, nu: (te[i], 0, fidx(i, f, te, nu))),
                      pl.BlockSpec((1, D, tf), lambda i, f, te, nu: (te[i], 0, fidx(i, f, te, nu))),
                      pl.BlockSpec((1, tf, D), lambda i, f, te, nu: (te[i], fidx(i, f, te, nu), 0))],
            out_specs=row(D),
            scratch_shapes=[pltpu.VMEM((tr, D), F32)]),
        out_shape=jax.ShapeDtypeStruct((N, D), BF16),
        compiler_params=_params(("arbitrary", "arbitrary")),
        name="moe_expert_ffn",
    )(tile_expert, n_used, xs_lo, xs_hi, gs_lo, gs_hi, wg, wu, wd)


def _combine_kernel(a_ref, x_ref, ylo_ref, yhi_ref, s1_ref, s2_ref, gf_ref, o_ref, acc_ref):
    c = pl.program_id(0)
    e = pl.program_id(1)

    @pl.when(e == 0)
    def _():
        acc_ref[...] = jnp.zeros_like(acc_ref)

    a, has_lo, has_hi = _tile_meta(a_ref[c * pl.num_programs(1) + e])

    def gather(tile, y_ref):
        s1 = s1_ref[...]
        s2 = s2_ref[...]
        parts = []
        for j in range(MOE_CHUNK // LANE):
            cols = tile * MOE_CHUNK + j * LANE + lax.broadcasted_iota(jnp.int32, (MOE_CHUNK, LANE), 1)
            parts.append(jnp.where((s1 == cols) | (s2 == cols), 1.0, 0.0).astype(BF16))
        acc_ref[...] += _dot(jnp.concatenate(parts, axis=1), y_ref[...])

    @pl.when(has_lo)
    def _():
        gather(a, ylo_ref)

    @pl.when(has_hi)
    def _():
        gather(a + 1, yhi_ref)

    @pl.when(e == pl.num_programs(1) - 1)
    def _():
        o_ref[...] = _rms(x_ref[...] + acc_ref[...], gf_ref[...])


def _combine(a_ce, x2, ys, slot1_rep, slot2_rep, g_final):
    T, D = x2.shape
    C = T // MOE_CHUNK
    E = N_EXPERTS
    return pl.pallas_call(
        _combine_kernel,
        grid_spec=pltpu.PrefetchScalarGridSpec(
            num_scalar_prefetch=1,
            grid=(C, E),
            in_specs=[pl.BlockSpec((MOE_CHUNK, D), lambda c, e, a: (c, 0)),
                      pl.BlockSpec((MOE_CHUNK, D), lambda c, e, a: (a[c * E + e] >> 2, 0)),
                      pl.BlockSpec((MOE_CHUNK, D), lambda c, e, a: ((a[c * E + e] >> 2) + 1, 0)),
                      pl.BlockSpec((MOE_CHUNK, LANE), lambda c, e, a: (c, 0)),
                      pl.BlockSpec((MOE_CHUNK, LANE), lambda c, e, a: (c, 0)),
                      pl.BlockSpec((1, D), lambda c, e, a: (0, 0))],
            out_specs=pl.BlockSpec((MOE_CHUNK, D), lambda c, e, a: (c, 0)),
            scratch_shapes=[pltpu.VMEM((MOE_CHUNK, D), F32)]),
        out_shape=jax.ShapeDtypeStruct((T, D), F32),
        compiler_params=_params(("arbitrary", "arbitrary")),
        name="moe_combine_final_norm",
    )(a_ce, x2, ys, ys, slot1_rep, slot2_rep, g_final)


def _rope_tables(S):
    inv = ROPE_THETA ** (-jnp.arange(ROPE_HALF, dtype=F32) / ROPE_HALF)
    ang = jnp.arange(S, dtype=F32)[:, None] * inv[None, :]
    cos, sin = jnp.cos(ang), jnp.sin(ang)
    pad = HEAD_DIM - ROPE_DIM
    cos_h = jnp.concatenate([cos, cos, jnp.ones((S, pad), F32)], axis=-1)
    sin_h = jnp.concatenate([-sin, sin, jnp.zeros((S, pad), F32)], axis=-1)
    reps = LANE // HEAD_DIM
    return jnp.tile(cos_h, (1, reps)), jnp.tile(sin_h, (1, reps))


def _pack_w_in(w):
    per_group = GROUP * N_BRANCH
    gates = w[:, GATE_OFF:GATE_OFF + N_KV_HEADS * per_group].reshape(-1, N_KV_HEADS, per_group)
    gates = jnp.pad(gates, ((0, 0), (0, LANE // GATE_ROWS - N_KV_HEADS), (0, GATE_ROWS - per_group)))
    gates = gates.reshape(-1, LANE)
    return jnp.concatenate([w[:, :GATE_OFF], gates, w[:, GATE_OFF + N_KV_HEADS * per_group:]], axis=1).astype(BF16)


def _mixer_layer(x, norm_g, w_in, pe_k, w1_k, b1_k, w2_k, pe_v, w1_v, b1_v, w2_v,
                 w_pool, pool_scale, w_out, cos_t, sin_t):
    B, S, D = x.shape
    qt, qrt, kcr, vcr, ks, vst, kw, vwt, gates, u = _mixer_proj(
        x, norm_g.reshape(1, D), _pack_w_in(w_in), cos_t, sin_t)
    half = CMP_LEN // 2
    cw = half * HEAD_DIM
    pad_w2 = lambda w2: jnp.pad(w2, ((0, 0), (0, LANE - HEAD_DIM))).astype(BF16)
    kc, vct = _compress(
        kcr, vcr,
        pe_k.reshape(2, cw), w1_k.reshape(2, cw, CMP_HIDDEN).astype(BF16), b1_k.reshape(1, -1), pad_w2(w2_k),
        pe_v.reshape(2, cw), w1_v.reshape(2, cw, CMP_HIDDEN).astype(BF16), b1_v.reshape(1, -1), pad_w2(w2_v))
    a = _nsa_attention(qt, qrt, kc, vct, ks, vst, kw, vwt, gates)
    p = _pool_mixer(u, w_pool.astype(BF16), pool_scale.reshape(1, -1))
    w_out = w_out.astype(BF16)
    out = _out_proj(x.reshape(B * S, D), a.reshape(B * S, NSA_WIDTH), p.reshape(B * S, POOL_WIDTH),
                    w_out[:NSA_WIDTH], w_out[NSA_WIDTH:])
    return out.reshape(B, S, D)


def _moe_layer(x2, g_ffn, router, wg, wu, wd, g_final):
    T, D = x2.shape
    E = N_EXPERTS
    C = T // MOE_CHUNK
    i32 = jnp.int32
    w_router = jnp.pad(router, ((0, 0), (0, LANE - E)))
    h, route, gates, before, total = _router(x2, g_ffn.reshape(1, D), w_router)

    counts = total[0, :E].astype(i32)
    tiles = (counts + MOE_ROW_TILE - 1) // MOE_ROW_TILE
    ends = jnp.cumsum(tiles)
    off = (ends - tiles) * MOE_ROW_TILE
    e1, e2, r1, r2 = route[:, 0], route[:, 1], route[:, 2], route[:, 3]
    slot1 = off[e1] + r1
    slot2 = off[e2] + r2
    first = off[None, :] + before[:, 0, :E].astype(i32)
    after = jnp.concatenate([before[1:, 0, :E], total[:, :E]], axis=0).astype(i32)
    last = off[None, :] + after - 1
    lo_tile = first // MOE_CHUNK
    has_lo = (last >= first).astype(i32)
    has_hi = ((last >= first) & (last // MOE_CHUNK > lo_tile)).astype(i32)
    tile_meta = lo_tile * 4 + has_hi * 2 + has_lo
    n_row_tiles = (2 * T) // MOE_ROW_TILE + E + 1 + MOE_CHUNK // MOE_ROW_TILE
    n_used = ends[-1:]
    tile_ids = jnp.minimum(jnp.arange(n_row_tiles, dtype=i32), n_used[0] - 1)
    tile_expert = jnp.minimum(jnp.sum(ends[None, :] <= tile_ids[:, None], axis=1), E - 1).astype(i32)

    zero = jnp.zeros_like(slot1)
    tok = jnp.stack([slot1, slot2, zero, zero, zero, zero, zero, zero])
    tok_rows = tok.reshape(8, C, MOE_CHUNK).transpose(1, 0, 2)
    gz = jnp.zeros((T,), F32)
    gate_rows = jnp.stack([gates[:, 0], gates[:, 1], gz, gz, gz, gz, gz, gz]).reshape(8, C, MOE_CHUNK).transpose(1, 0, 2)
    slot1_rep = jnp.broadcast_to(slot1[:, None], (T, LANE))
    slot2_rep = jnp.broadcast_to(slot2[:, None], (T, LANE))

    xs_lo, xs_hi, gs_lo, gs_hi = _dispatch(tile_meta.T.reshape(-1).astype(i32), h, tok_rows, gate_rows,
                                           n_row_tiles * MOE_ROW_TILE)
    ys = _expert_ffn(tile_expert, n_used.astype(i32), xs_lo, xs_hi, gs_lo, gs_hi,
                     wg.astype(BF16), wu.astype(BF16), wd.astype(BF16))
    return _combine(tile_meta.reshape(-1).astype(i32), x2, ys, slot1_rep, slot2_rep, g_final.reshape(1, D))


def kernel(x, mem, norm_mix, w_in, cmp_pe_k, cmp_w1_k, cmp_b1_k, cmp_w2_k, cmp_pe_v, cmp_w1_v, cmp_b1_v, cmp_w2_v, w_pool, pool_scale, w_out, norm_x, norm_mem, wq_x, wk_x, wv_x, wo_x, norm_ffn, ffn_wg, ffn_wu, ffn_wd, moe_router, moe_wg, moe_wu, moe_wd, norm_final):
    B, S, D = x.shape
    depth = norm_mix.shape[0]
    assert depth == 2, "the final RMSNorm is fused into the expert layer, which must be the last one"
    cos_t, sin_t = _rope_tables(S)
    for layer in range(depth):
        x = _mixer_layer(x, norm_mix[layer], w_in[layer],
                         cmp_pe_k[layer], cmp_w1_k[layer], cmp_b1_k[layer], cmp_w2_k[layer],
                         cmp_pe_v[layer], cmp_w1_v[layer], cmp_b1_v[layer], cmp_w2_v[layer],
                         w_pool[layer], pool_scale[layer], w_out[layer], cos_t, sin_t)
        wkv = jnp.concatenate([wk_x[layer], wv_x[layer]], axis=1).astype(BF16)
        kx, vx = _xattn_kv(mem, norm_mem[layer].reshape(1, D), wkv)
        x = _xattn(x, norm_x[layer].reshape(1, D), wq_x[layer].astype(BF16), kx, vx, wo_x[layer].astype(BF16))
        x2 = x.reshape(B * S, D)
        j = layer // 2
        if layer % 2 == 0:
            x2 = _ffn(x2, norm_ffn[layer].reshape(1, D), ffn_wg[j].astype(BF16), ffn_wu[j].astype(BF16),
                      ffn_wd[j].astype(BF16))
        else:
            x2 = _moe_layer(x2, norm_ffn[layer], moe_router[j], moe_wg[j], moe_wu[j], moe_wd[j], norm_final)
        x = x2.reshape(B, S, D)
    return x
```

```python
import functools
import math

import numpy as np
import jax
import jax.numpy as jnp
from jax import lax
from jax.experimental import pallas as pl
from jax.experimental.pallas import tpu as pltpu

F32 = jnp.float32
BF16 = jnp.bfloat16

D_MODEL = 1024
HEAD_DIM = 64
N_HEADS = 8
N_KV_HEADS = 2
GROUP = N_HEADS // N_KV_HEADS
NSA_WIDTH = N_HEADS * HEAD_DIM
KV_WIDTH = N_KV_HEADS * HEAD_DIM
N_BRANCH = 3
GATE_ROWS = 16
POOL_WIDTH = 512
POOL_WINDOWS = (2, 4, 8, 16)
POOL_GROUP = 128
POOL_HALO = 16
ROPE_DIM = 16
ROPE_HALF = 8
ROPE_THETA = 500000.0
CMP_LEN = 32
CMP_STRIDE = 16
CMP_HIDDEN = 256
SEL_BLOCK = 64
SEL_TOP_N = 16
N_LOCAL_BLOCKS = 2
WINDOW = 512
X_HEADS = 4
X_HEAD_DIM = 256
N_EXPERTS = 8
EPS = 1e-6

LANE = 128
IN_PACKED = NSA_WIDTH + 6 * KV_WIDTH + LANE + POOL_WIDTH
GATE_OFF = NSA_WIDTH + 6 * KV_WIDTH
POOL_OFF = GATE_OFF + LANE

NEG = -1e30
TINY = float(np.finfo(np.float32).tiny)
VMEM_LIMIT = 56 * 1024 * 1024


def _dot(a, b):
    return jnp.dot(a, b, preferred_element_type=F32)


def _dot_nt(a, b):
    return lax.dot_general(a, b, (((1,), (1,)), ((), ())), preferred_element_type=F32)


def _rms(x, g):
    y = x * lax.rsqrt(jnp.mean(x * x, axis=-1, keepdims=True) + EPS)
    return y * g


def _params(sem, limit=VMEM_LIMIT):
    return pltpu.CompilerParams(dimension_semantics=sem, vmem_limit_bytes=limit)


def _mixer_proj_kernel(x_ref, g_ref, w_ref, cos_ref, sin_ref,
                       qt_ref, qrt_ref, kcr_ref, vcr_ref, ks_ref, vst_ref, kw_ref, vwt_ref,
                       gate_ref, u_ref):
    h = _rms(x_ref[0], g_ref[...]).astype(BF16)
    z = _dot(h, w_ref[...])
    tm = z.shape[0]
    cos = cos_ref[...]
    sin = sin_ref[...]
    lane = lax.broadcasted_iota(jnp.int32, (tm, LANE), 1)
    first = (lane & (HEAD_DIM - 1)) < ROPE_HALF
    scale = HEAD_DIM ** -0.5 * math.log2(math.e)

    def rope(xs):
        partner = jnp.where(first, pltpu.roll(xs, LANE - ROPE_HALF, 1), pltpu.roll(xs, ROPE_HALF, 1))
        return xs * cos + partner * sin

    for s in range(NSA_WIDTH // LANE):
        xs = z[:, s * LANE:(s + 1) * LANE]
        for src, ref in ((xs, qt_ref), (rope(xs), qrt_ref)):
            t = (src * scale).T.astype(BF16)
            ref[0, 2 * s] = t[:HEAD_DIM]
            ref[0, 2 * s + 1] = t[HEAD_DIM:]

    def kv_slab(i):
        return z[:, NSA_WIDTH + i * KV_WIDTH:NSA_WIDTH + (i + 1) * KV_WIDTH]

    for slab, ref in ((kv_slab(0), kcr_ref), (kv_slab(1), vcr_ref),
                      (rope(kv_slab(2)), ks_ref), (rope(kv_slab(4)), kw_ref)):
        for gg in range(N_KV_HEADS):
            ref[0, gg] = slab[:, gg * HEAD_DIM:(gg + 1) * HEAD_DIM].astype(ref.dtype)

    for slab, ref in ((kv_slab(3), vst_ref), (kv_slab(5), vwt_ref)):
        t = slab.T.astype(BF16)
        for gg in range(N_KV_HEADS):
            for k in range(tm // LANE):
                ref[0, gg, k] = t[gg * HEAD_DIM:(gg + 1) * HEAD_DIM, k * LANE:(k + 1) * LANE]

    sig_t = jax.nn.sigmoid(z[:, GATE_OFF:GATE_OFF + LANE]).T
    for gg in range(N_KV_HEADS):
        gate_ref[0, gg] = sig_t[gg * GATE_ROWS:(gg + 1) * GATE_ROWS]
    u_ref[0] = z[:, POOL_OFF:POOL_OFF + POOL_WIDTH]


def _mixer_proj(x, g, w_packed, cos_t, sin_t, tm=512):
    B, S, D = x.shape
    G = N_KV_HEADS
    sd = jax.ShapeDtypeStruct
    out_shape = (sd((B, N_HEADS, HEAD_DIM, S), BF16), sd((B, N_HEADS, HEAD_DIM, S), BF16),
                 sd((B, G, S, HEAD_DIM), F32), sd((B, G, S, HEAD_DIM), F32),
                 sd((B, G, S, HEAD_DIM), BF16), sd((B, G, S // LANE, HEAD_DIM, LANE), BF16),
                 sd((B, G, S, HEAD_DIM), BF16), sd((B, G, S // LANE, HEAD_DIM, LANE), BF16),
                 sd((B, G, GATE_ROWS, S), F32),
                 sd((B, S, POOL_WIDTH), F32))
    qspec = pl.BlockSpec((1, N_HEADS, HEAD_DIM, tm), lambda b, i: (b, 0, 0, i))
    kspec = pl.BlockSpec((1, G, tm, HEAD_DIM), lambda b, i: (b, 0, i, 0))
    vspec = pl.BlockSpec((1, G, tm // LANE, HEAD_DIM, LANE), lambda b, i: (b, 0, i, 0, 0))
    out_specs = (qspec, qspec, kspec, kspec, kspec, vspec, kspec, vspec,
                 pl.BlockSpec((1, G, GATE_ROWS, tm), lambda b, i: (b, 0, 0, i)),
                 pl.BlockSpec((1, tm, POOL_WIDTH), lambda b, i: (b, i, 0)))
    return pl.pallas_call(
        _mixer_proj_kernel,
        grid=(B, S // tm),
        in_specs=[pl.BlockSpec((1, tm, D), lambda b, i: (b, i, 0)),
                  pl.BlockSpec((1, D), lambda b, i: (0, 0)),
                  pl.BlockSpec((D, IN_PACKED), lambda b, i: (0, 0)),
                  pl.BlockSpec((tm, LANE), lambda b, i: (i, 0)),
                  pl.BlockSpec((tm, LANE), lambda b, i: (i, 0))],
        out_specs=out_specs,
        out_shape=out_shape,
        compiler_params=_params(("parallel", "parallel")),
        name="mixer_proj",
    )(x, g, w_packed, cos_t, sin_t)


def _gelu_tanh(x):
    return 0.5 * x * (1.0 + jnp.tanh(math.sqrt(2.0 / math.pi) * (x + 0.044715 * (x * x * x))))


def _compress_kernel(kc_ref, vc_ref, pek_ref, w1k_ref, b1k_ref, w2k_ref,
                     pev_ref, w1v_ref, b1v_ref, w2v_ref, ko_ref, vto_ref):
    def comp(x_ref, pe_ref, w1_ref, b1_ref, w2_ref):
        x = x_ref[0, 0]
        n = x.shape[0]
        a = _dot((x + pe_ref[0:1, :]).astype(BF16), w1_ref[0])
        b = _dot((x + pe_ref[1:2, :]).astype(BF16), w1_ref[1])
        pre = a + pltpu.roll(b, n - 1, 0) + b1_ref[...]
        return _dot(_gelu_tanh(pre).astype(BF16), w2_ref[...])

    ko_ref[0, 0] = comp(kc_ref, pek_ref, w1k_ref, b1k_ref, w2k_ref)[:, :HEAD_DIM].astype(ko_ref.dtype)
    vto_ref[0, 0] = comp(vc_ref, pev_ref, w1v_ref, b1v_ref, w2v_ref).T[:HEAD_DIM].astype(vto_ref.dtype)


def _compress(kcr, vcr, pek, w1k, b1k, w2k, pev, w1v, b1v, w2v):
    B, G, S, _ = kcr.shape
    n = S // CMP_STRIDE
    cw = CMP_STRIDE * HEAD_DIM
    kc = kcr.reshape(B, G, n, cw)
    vc = vcr.reshape(B, G, n, cw)
    xspec = pl.BlockSpec((1, 1, n, cw), lambda b, g: (b, g, 0, 0))
    full = lambda a: pl.BlockSpec(a.shape, lambda b, g: (0,) * a.ndim)
    return pl.pallas_call(
        _compress_kernel,
        grid=(B, G),
        in_specs=[xspec, xspec, full(pek), full(w1k), full(b1k), full(w2k),
                  full(pev), full(w1v), full(b1v), full(w2v)],
        out_specs=(pl.BlockSpec((1, 1, n, HEAD_DIM), lambda b, g: (b, g, 0, 0)),
                   pl.BlockSpec((1, 1, HEAD_DIM, n), lambda b, g: (b, g, 0, 0))),
        out_shape=(jax.ShapeDtypeStruct((B, G, n, HEAD_DIM), BF16),
                   jax.ShapeDtypeStruct((B, G, HEAD_DIM, n), BF16)),
        compiler_params=_params(("parallel", "parallel")),
        name="compress_kv",
    )(kc, vc, pek, w1k, b1k, w2k, pev, w1v, b1v, w2v)


def _nsa_kernel(qt_ref, qrt_ref, kc_ref, vct_ref, ks_ref, vst_ref, kw_ref, vwt_ref, gate_ref,
                covt_ref, o_ref, selbias_ref, *, tq, tk):
    qt = pl.program_id(2)
    q0 = qt * tq
    n_cmp = kc_ref.shape[2]
    n_blk = covt_ref.shape[0]
    heads = lambda ref: jnp.concatenate([ref[0, hh] for hh in range(GROUP)], axis=1)
    q_t = heads(qt_ref)
    qr_t = heads(qrt_ref)
    t_lane = q0 + lax.broadcasted_iota(jnp.int32, (1, tq), 1)
    per_head = lambda a: [a[:, hh * tq:(hh + 1) * tq] for hh in range(GROUP)]
    all_heads = lambda a: jnp.concatenate([a] * GROUP, axis=1)

    sc = _dot(kc_ref[0, 0], q_t)
    cmp_end = lax.broadcasted_iota(jnp.int32, (n_cmp, 1), 0) * CMP_STRIDE + (CMP_LEN - 1)
    valid = cmp_end <= t_lane
    pcs = []
    for s_h in per_head(sc):
        s_h = jnp.where(valid, s_h, NEG)
        e = jnp.where(valid, jnp.exp2(s_h - jnp.max(s_h, axis=0, keepdims=True)), 0.0)
        pcs.append(e * (1.0 / jnp.maximum(jnp.sum(e, axis=0, keepdims=True), TINY)))
    oc_t = _dot(vct_ref[0, 0], jnp.concatenate(pcs, axis=1).astype(BF16))

    psum = pcs[0] + pcs[1] + pcs[2] + pcs[3]
    p_hi = psum.astype(BF16)
    r1 = psum - p_hi.astype(F32)
    p_mid = r1.astype(BF16)
    p_lo = (r1 - p_mid.astype(F32)).astype(BF16)
    covt = covt_ref[...]
    imp_t = _dot(covt, p_hi) + _dot(covt, p_mid) + _dot(covt, p_lo)

    jb = lax.broadcasted_iota(jnp.int32, (n_blk, tq), 0)
    tb = (q0 + lax.broadcasted_iota(jnp.int32, (n_blk, tq), 1)) // SEL_BLOCK
    dist = tb - jb
    forced = (jb == 0) | ((dist >= 0) & (dist < N_LOCAL_BLOCKS))
    score = jnp.where(jb > tb, -jnp.inf, jnp.where(forced, jnp.inf, imp_t))
    rank = jnp.zeros((n_blk, tq), jnp.int32)
    for jp in range(n_blk):
        rowv = score[jp:jp + 1, :]
        beats = (rowv > score) | ((rowv == score) & (jb > jp))
        rank = rank + jnp.where(beats, 1, 0)
    selbias_ref[...] = jnp.where(rank < min(SEL_TOP_N, n_blk), 0.0, NEG)

    R = GROUP * tq

    def with_ones(v_t):
        extra = jnp.where(lax.broadcasted_iota(jnp.int32, (16, v_t.shape[1]), 0) == 0, 1.0, 0.0)
        return jnp.concatenate([v_t, extra.astype(BF16)], axis=0)

    bpc = tk // SEL_BLOCK
    vpc = tk // LANE

    def sel_chunk(c, carry, causal):
        m_i, l_i, acc = carry
        start = pl.multiple_of(c * tk, tk)
        k_c = ks_ref[0, 0, pl.ds(start, tk), :]
        v_t = jnp.concatenate([vst_ref[0, 0, c * vpc + k] for k in range(vpc)], axis=1)
        rows = [jnp.broadcast_to(selbias_ref[pl.ds(c * bpc + j, 1), :], (SEL_BLOCK, tq)) for j in range(bpc)]
        bias = jnp.concatenate(rows, axis=0)
        if causal:
            kpos = start + lax.broadcasted_iota(jnp.int32, (tk, 1), 0)
            bias = jnp.where(kpos <= t_lane, bias, NEG)
        s = _dot(k_c, qr_t) + all_heads(bias)
        m_new = jnp.maximum(m_i, jnp.max(s, axis=0, keepdims=True))
        alpha = jnp.exp2(m_i - m_new)
        pv = _dot(with_ones(v_t), jnp.exp2(s - m_new).astype(BF16))
        return m_new, alpha * l_i + pv[HEAD_DIM:HEAD_DIM + 1], alpha * acc + pv[:HEAD_DIM]

    init = (jnp.full((1, R), NEG, F32), jnp.zeros((1, R), F32), jnp.zeros((HEAD_DIM, R), F32))
    c_diag = q0 // tk
    carry = lax.fori_loop(0, c_diag, lambda c, cr: sel_chunk(c, cr, False), init)
    _, l_s, acc_s = sel_chunk(c_diag, carry, True)
    os_t = acc_s * (1.0 / jnp.maximum(l_s, TINY))

    wk = WINDOW + tq
    w0 = pl.multiple_of(jnp.maximum(q0 - WINDOW, 0), LANE)
    k_w = kw_ref[0, 0, pl.ds(w0, wk), :]
    v_t = jnp.concatenate([vwt_ref[0, 0, w0 // LANE + k] for k in range(wk // LANE)], axis=1)
    diff = t_lane - (w0 + lax.broadcasted_iota(jnp.int32, (wk, 1), 0))
    bias = jnp.where((diff >= 0) & (diff < WINDOW), 0.0, NEG)
    s = _dot(k_w, qr_t) + all_heads(bias)
    pv = _dot(with_ones(v_t), jnp.exp2(s - jnp.max(s, axis=0, keepdims=True)).astype(BF16))
    ow_t = pv[:HEAD_DIM] * (1.0 / pv[HEAD_DIM:HEAD_DIM + 1])

    gt = gate_ref[0, 0]
    outs = []
    for hh, (a, b, c) in enumerate(zip(per_head(oc_t), per_head(os_t), per_head(ow_t))):
        r = hh * N_BRANCH
        outs.append(gt[r:r + 1] * a + gt[r + 1:r + 2] * b + gt[r + 2:r + 3] * c)
    o_ref[0] = jnp.concatenate(outs, axis=0).T.astype(o_ref.dtype)


def _cover_table(S):
    n_cmp = S // CMP_STRIDE
    n_blk = S // SEL_BLOCK
    cs = np.arange(n_cmp) * CMP_STRIDE
    ss = np.arange(n_blk) * SEL_BLOCK
    cover_t = ((cs[None, :] < ss[:, None] + SEL_BLOCK) & (cs[None, :] + CMP_LEN > ss[:, None]))
    cover_t[:, n_cmp - 1] = False
    return jnp.asarray(cover_t, BF16)


def _nsa_attention(qt, qrt, kc, vct, ks, vst, kw, vwt, gates, tq=256, tk=1024):
    B, _, _, S = qt.shape
    tk = min(tk, S)
    assert tq % LANE == 0 and tk % tq == 0 and S % tk == 0 and S >= WINDOW + tq
    covt = _cover_table(S)
    n_cmp = kc.shape[2]
    n_blk = covt.shape[0]
    qspec = pl.BlockSpec((1, GROUP, HEAD_DIM, tq), lambda b, g, i: (b, g, 0, i))
    kspec = lambda a: pl.BlockSpec((1, 1) + a.shape[2:], lambda b, g, i: (b, g, 0, 0))
    vspec = pl.BlockSpec((1, 1, S // LANE, HEAD_DIM, LANE), lambda b, g, i: (b, g, 0, 0, 0))
    return pl.pallas_call(
        functools.partial(_nsa_kernel, tq=tq, tk=tk),
        grid=(B, N_KV_HEADS, S // tq),
        in_specs=[qspec, qspec,
                  pl.BlockSpec((1, 1, n_cmp, HEAD_DIM), lambda b, g, i: (b, g, 0, 0)),
                  pl.BlockSpec((1, 1, HEAD_DIM, n_cmp), lambda b, g, i: (b, g, 0, 0)),
                  kspec(ks), vspec, kspec(kw), vspec,
                  pl.BlockSpec((1, 1, GATE_ROWS, tq), lambda b, g, i: (b, g, 0, i)),
                  pl.BlockSpec(covt.shape, lambda b, g, i: (0, 0))],
        out_specs=pl.BlockSpec((1, tq, GROUP * HEAD_DIM), lambda b, g, i: (b, i, g)),
        out_shape=jax.ShapeDtypeStruct((B, S, NSA_WIDTH), BF16),
        scratch_shapes=[pltpu.VMEM((n_blk, tq), F32)],
        compiler_params=_params(("parallel", "parallel", "parallel")),
        name="nsa_attention",
    )(qt, qrt, kc, vct, ks, vst, kw, vwt, gates, covt)


def _pool_kernel(u_ref, w_ref, scale_ref, o_ref, buf):
    i = pl.program_id(1)
    tm = u_ref.shape[1]

    @pl.when(i == 0)
    def _():
        buf[0:POOL_HALO, :] = jnp.zeros((POOL_HALO, POOL_WIDTH), F32)

    @pl.when(i > 0)
    def _():
        buf[0:POOL_HALO, :] = buf[tm:tm + POOL_HALO, :]

    buf[POOL_HALO:POOL_HALO + tm, :] = u_ref[0]
    t1 = i * tm + lax.broadcasted_iota(jnp.int32, (tm, 1), 0) + 1
    for gi, w in enumerate(POOL_WINDOWS):
        cols = slice(gi * POOL_GROUP, (gi + 1) * POOL_GROUP)
        cur = buf[POOL_HALO:POOL_HALO + tm, cols]
        tot = cur
        for k in range(1, w):
            tot = tot + buf[POOL_HALO - k:POOL_HALO - k + tm, cols]
        cnt = jnp.minimum(t1, w).astype(F32)
        d = tot / cnt - cur
        y = _dot(d.astype(BF16), w_ref[gi])
        o_ref[0, :, cols] = (y * scale_ref[:, cols]).astype(o_ref.dtype)


def _pool_mixer(u, w_pool, pool_scale, tm=512):
    B, S, _ = u.shape
    return pl.pallas_call(
        _pool_kernel,
        grid=(B, S // tm),
        in_specs=[pl.BlockSpec((1, tm, POOL_WIDTH), lambda b, i: (b, i, 0)),
                  pl.BlockSpec(w_pool.shape, lambda b, i: (0, 0, 0)),
                  pl.BlockSpec((1, POOL_WIDTH), lambda b, i: (0, 0))],
        out_specs=pl.BlockSpec((1, tm, POOL_WIDTH), lambda b, i: (b, i, 0)),
        out_shape=jax.ShapeDtypeStruct((B, S, POOL_WIDTH), BF16),
        scratch_shapes=[pltpu.VMEM((tm + POOL_HALO, POOL_WIDTH), F32)],
        compiler_params=_params(("parallel", "arbitrary")),
        name="pool_mixer",
    )(u, w_pool, pool_scale)


def _out_proj_kernel(x_ref, a_ref, p_ref, wa_ref, wp_ref, o_ref):
    o_ref[...] = x_ref[...] + _dot(a_ref[...], wa_ref[...]) + _dot(p_ref[...], wp_ref[...])


def _out_proj(x2, a2, p2, wa, wp, tm=512):
    T, D = x2.shape
    row = lambda n: pl.BlockSpec((tm, n), lambda i: (i, 0))
    full = lambda a: pl.BlockSpec(a.shape, lambda i: (0, 0))
    return pl.pallas_call(
        _out_proj_kernel,
        grid=(T // tm,),
        in_specs=[row(D), row(a2.shape[1]), row(p2.shape[1]), full(wa), full(wp)],
        out_specs=row(D),
        out_shape=jax.ShapeDtypeStruct((T, D), F32),
        compiler_params=_params(("parallel",)),
        name="mixer_out_proj",
    )(x2, a2, p2, wa, wp)


def _xkv_kernel(m_ref, g_ref, w_ref, k_ref, v_ref):
    h = _rms(m_ref[0], g_ref[...]).astype(BF16)
    kv = _dot(h, w_ref[...])
    d = k_ref.shape[2]
    k_ref[0] = kv[:, :d].astype(k_ref.dtype)
    v_ref[0] = kv[:, d:].astype(v_ref.dtype)


def _xattn_kv(mem, g, wkv):
    B, M, D = mem.shape
    ospec = pl.BlockSpec((1, M, D), lambda b: (b, 0, 0))
    oshape = jax.ShapeDtypeStruct((B, M, D), BF16)
    return pl.pallas_call(
        _xkv_kernel,
        grid=(B,),
        in_specs=[pl.BlockSpec((1, M, D), lambda b: (b, 0, 0)),
                  pl.BlockSpec((1, D), lambda b: (0, 0)),
                  pl.BlockSpec(wkv.shape, lambda b: (0, 0))],
        out_specs=(ospec, ospec),
        out_shape=(oshape, oshape),
        compiler_params=_params(("parallel",)),
        name="xattn_kv",
    )(mem, g, wkv)


def _xattn_kernel(x_ref, g_ref, wq_ref, k_ref, v_ref, wo_ref, o_ref):
    x = x_ref[0]
    h = _rms(x, g_ref[...]).astype(BF16)
    scale = X_HEAD_DIM ** -0.5
    q = (_dot(h, wq_ref[...]) * scale).astype(BF16)
    outs = []
    for hd in range(X_HEADS):
        sl = slice(hd * X_HEAD_DIM, (hd + 1) * X_HEAD_DIM)
        s = _dot_nt(q[:, sl], k_ref[0, :, sl])
        e = jnp.exp(s - jnp.max(s, axis=-1, keepdims=True))
        p = e / jnp.sum(e, axis=-1, keepdims=True)
        outs.append(_dot(p.astype(BF16), v_ref[0, :, sl]).astype(BF16))
    o = jnp.concatenate(outs, axis=-1)
    o_ref[0] = x + _dot(o, wo_ref[...])


def _xattn(x, g, wq, kx, vx, wo, tm=512):
    B, S, D = x.shape
    M = kx.shape[1]
    xspec = pl.BlockSpec((1, tm, D), lambda b, i: (b, i, 0))
    full = lambda a: pl.BlockSpec(a.shape, lambda b, i: (0, 0))
    mspec = pl.BlockSpec((1, M, D), lambda b, i: (b, 0, 0))
    return pl.pallas_call(
        _xattn_kernel,
        grid=(B, S // tm),
        in_specs=[xspec, full(g), full(wq), mspec, mspec, full(wo)],
        out_specs=xspec,
        out_shape=jax.ShapeDtypeStruct((B, S, D), F32),
        compiler_params=_params(("parallel", "parallel")),
        name="xattn",
    )(x, g, wq, kx, vx, wo)


def _silu(x):
    return x * jax.nn.sigmoid(x)


def _ffn_kernel(x_ref, g_ref, wg_ref, wu_ref, wd_ref, o_ref, h_ref, acc_ref):
    f = pl.program_id(1)

    @pl.when(f == 0)
    def _():
        h_ref[...] = _rms(x_ref[...], g_ref[...]).astype(BF16)
        acc_ref[...] = jnp.zeros_like(acc_ref)

    h = h_ref[...]
    act = _silu(_dot(h, wg_ref[...])) * _dot(h, wu_ref[...])
    acc_ref[...] += _dot(act.astype(BF16), wd_ref[...])

    @pl.when(f == pl.num_programs(1) - 1)
    def _():
        o_ref[...] = x_ref[...] + acc_ref[...]


def _ffn(x2, g, wg, wu, wd, tm=512, tf=1408):
    T, D = x2.shape
    F = wg.shape[1]
    return pl.pallas_call(
        _ffn_kernel,
        grid=(T // tm, F // tf),
        in_specs=[pl.BlockSpec((tm, D), lambda i, f: (i, 0)),
                  pl.BlockSpec((1, D), lambda i, f: (0, 0)),
                  pl.BlockSpec((D, tf), lambda i, f: (0, f)),
                  pl.BlockSpec((D, tf), lambda i, f: (0, f)),
                  pl.BlockSpec((tf, D), lambda i, f: (f, 0))],
        out_specs=pl.BlockSpec((tm, D), lambda i, f: (i, 0)),
        out_shape=jax.ShapeDtypeStruct((T, D), F32),
        scratch_shapes=[pltpu.VMEM((tm, D), BF16), pltpu.VMEM((tm, D), F32)],
        compiler_params=_params(("parallel", "arbitrary")),
        name="dense_swiglu",
    )(x2, g, wg, wu, wd)


MOE_CHUNK = 512
MOE_ROW_TILE = 512
assert MOE_ROW_TILE % MOE_CHUNK == 0


def _tile_meta(m):
    return m >> 2, (m & 1) != 0, (m & 2) != 0


def _router_kernel(x_ref, g_ref, wr_ref, tri_ref, h_ref, route_ref, gate_ref, before_ref, total_ref, run_ref):
    c = pl.program_id(0)

    @pl.when(c == 0)
    def _():
        run_ref[...] = jnp.zeros_like(run_ref)

    hf = _rms(x_ref[...], g_ref[...])
    h_ref[...] = hf.astype(BF16)
    logits = jnp.dot(hf, wr_ref[...], preferred_element_type=F32, precision=lax.Precision.HIGHEST)
    lane = lax.broadcasted_iota(jnp.int32, logits.shape, 1)
    logits = jnp.where(lane < N_EXPERTS, logits, -jnp.inf)
    v1 = jnp.max(logits, axis=-1, keepdims=True)
    i1 = jnp.min(jnp.where(logits == v1, lane, LANE), axis=-1, keepdims=True)
    rest = jnp.where(lane == i1, -jnp.inf, logits)
    v2 = jnp.max(rest, axis=-1, keepdims=True)
    i2 = jnp.min(jnp.where(rest == v2, lane, LANE), axis=-1, keepdims=True)
    e2 = jnp.exp(v2 - v1)
    den = 1.0 + e2
    gate_ref[...] = jnp.where(lane == 0, 1.0 / den, jnp.where(lane == 1, e2 / den, 0.0))

    onehot = jnp.where((lane == i1) | (lane == i2), 1.0, 0.0)
    run = run_ref[...]
    rank = run + _dot(tri_ref[...], onehot.astype(BF16))
    r1 = jnp.sum(jnp.where(lane == i1, rank, 0.0), axis=-1, keepdims=True).astype(jnp.int32)
    r2 = jnp.sum(jnp.where(lane == i2, rank, 0.0), axis=-1, keepdims=True).astype(jnp.int32)
    route_ref[...] = jnp.where(lane == 0, i1, jnp.where(lane == 1, i2,
                               jnp.where(lane == 2, r1, jnp.where(lane == 3, r2, 0))))
    before_ref[0] = run
    run = run + jnp.sum(onehot, axis=0, keepdims=True)
    run_ref[...] = run
    total_ref[...] = run


def _router(x2, g, w_router):
    T, D = x2.shape
    C = T // MOE_CHUNK
    tri = jnp.asarray(np.tril(np.ones((MOE_CHUNK, MOE_CHUNK), np.float32), -1), BF16)
    row = lambda n: pl.BlockSpec((MOE_CHUNK, n), lambda c: (c, 0))
    full = lambda a: pl.BlockSpec(a.shape, lambda c: (0, 0))
    return pl.pallas_call(
        _router_kernel,
        grid=(C,),
        in_specs=[row(D), full(g), full(w_router), full(tri)],
        out_specs=(row(D), row(LANE), row(LANE),
                   pl.BlockSpec((1, 1, LANE), lambda c: (c, 0, 0)),
                   pl.BlockSpec((1, LANE), lambda c: (0, 0))),
        out_shape=(jax.ShapeDtypeStruct((T, D), BF16),
                   jax.ShapeDtypeStruct((T, LANE), jnp.int32),
                   jax.ShapeDtypeStruct((T, LANE), F32),
                   jax.ShapeDtypeStruct((C, 1, LANE), F32),
                   jax.ShapeDtypeStruct((1, LANE), F32)),
        scratch_shapes=[pltpu.VMEM((1, LANE), F32)],
        compiler_params=_params(("arbitrary",)),
        name="moe_router",
    )(x2, g, w_router, tri)


def _dispatch_kernel(a_ref, h_ref, tok_ref, gate_ref, zlo_ref, zhi_ref, zglo_ref, zghi_ref,
                     lo_ref, hi_ref, glo_ref, ghi_ref):
    del zlo_ref, zhi_ref, zglo_ref, zghi_ref
    e = pl.program_id(0)
    c = pl.program_id(1)
    idx = e * pl.num_programs(1) + c
    a, has_lo, has_hi = _tile_meta(a_ref[idx])
    first = (c == 0) | (a != _tile_meta(a_ref[jnp.maximum(idx - 1, 0)])[0])

    @pl.when(first)
    def _():
        lo_ref[...] = jnp.zeros_like(lo_ref)
        hi_ref[...] = jnp.zeros_like(hi_ref)
        glo_ref[...] = jnp.zeros_like(glo_ref)
        ghi_ref[...] = jnp.zeros_like(ghi_ref)

    def scatter(tile, x_ref, g_ref):
        tok = tok_ref[0]
        gts = gate_ref[0]
        rows = tile * MOE_CHUNK + lax.broadcasted_iota(jnp.int32, (MOE_CHUNK, MOE_CHUNK), 0)
        c1 = tok[0:1] == rows
        c2 = tok[1:2] == rows
        x_ref[...] += _dot(jnp.where(c1 | c2, 1.0, 0.0).astype(BF16), h_ref[...]).astype(x_ref.dtype)
        g_ref[...] += jnp.sum(jnp.where(c1, gts[0:1], 0.0) + jnp.where(c2, gts[1:2], 0.0), axis=1, keepdims=True)

    @pl.when(has_lo)
    def _():
        scatter(a, lo_ref, glo_ref)

    @pl.when(has_hi)
    def _():
        scatter(a + 1, hi_ref, ghi_ref)


def _dispatch(a_ec, h, tok_rows, gate_rows, n_slots):
    T, D = h.shape
    C = T // MOE_CHUNK
    zx = jnp.zeros((n_slots, D), BF16)
    zg = jnp.zeros((n_slots, 1), F32)
    any_spec = pl.BlockSpec(memory_space=pl.ANY)
    lo = lambda n: pl.BlockSpec((MOE_CHUNK, n), lambda e, c, a: (a[e * C + c] >> 2, 0))
    hi = lambda n: pl.BlockSpec((MOE_CHUNK, n), lambda e, c, a: ((a[e * C + c] >> 2) + 1, 0))
    return pl.pallas_call(
        _dispatch_kernel,
        grid_spec=pltpu.PrefetchScalarGridSpec(
            num_scalar_prefetch=1,
            grid=(N_EXPERTS, C),
            in_specs=[pl.BlockSpec((MOE_CHUNK, D), lambda e, c, a: (c, 0)),
                      pl.BlockSpec((1, 8, MOE_CHUNK), lambda e, c, a: (c, 0, 0)),
                      pl.BlockSpec((1, 8, MOE_CHUNK), lambda e, c, a: (c, 0, 0)),
                      any_spec, any_spec, any_spec, any_spec],
            out_specs=(lo(D), hi(D), lo(1), hi(1))),
        out_shape=(jax.ShapeDtypeStruct(zx.shape, BF16), jax.ShapeDtypeStruct(zx.shape, BF16),
                   jax.ShapeDtypeStruct(zg.shape, F32), jax.ShapeDtypeStruct(zg.shape, F32)),
        input_output_aliases={4: 0, 5: 1, 6: 2, 7: 3},
        compiler_params=_params(("arbitrary", "arbitrary")),
        name="moe_dispatch",
    )(a_ec, h, tok_rows, gate_rows, zx, zx, zg, zg)


def _expert_ffn_kernel(te_ref, nu_ref, lo_ref, hi_ref, glo_ref, ghi_ref, wg_ref, wu_ref, wd_ref, o_ref, acc_ref):
    i = pl.program_id(0)
    f = pl.program_id(1)
    used = i < nu_ref[0]

    @pl.when(used & (f == 0))
    def _():
        acc_ref[...] = jnp.zeros_like(acc_ref)

    @pl.when(used)
    def _():
        x = lo_ref[...] + hi_ref[...]
        act = _silu(_dot(x, wg_ref[0])) * _dot(x, wu_ref[0])
        acc_ref[...] += _dot(act.astype(BF16), wd_ref[0])

    last = f == pl.num_programs(1) - 1

    @pl.when(used & last)
    def _():
        o_ref[...] = (acc_ref[...] * (glo_ref[...] + ghi_ref[...])).astype(o_ref.dtype)

    @pl.when(jnp.logical_not(used) & last)
    def _():
        o_ref[...] = jnp.zeros_like(o_ref)


def _expert_ffn(tile_expert, n_used, xs_lo, xs_hi, gs_lo, gs_hi, wg, wu, wd, tf=1792):
    N, D = xs_lo.shape
    F = wg.shape[2]
    nf = F // tf
    tr = MOE_ROW_TILE
    fidx = lambda i, f, te, nu: jnp.where(i < nu[0], f, nf - 1)
    row = lambda n: pl.BlockSpec((tr, n), lambda i, f, te, nu: (i, 0))
    return pl.pallas_call(
        _expert_ffn_kernel,
        grid_spec=pltpu.PrefetchScalarGridSpec(
            num_scalar_prefetch=2,
            grid=(N // tr, nf),
            in_specs=[row(D), row(D), row(1), row(1),
                      pl.BlockSpec((1, D, tf), lambda i, f, te, nu: (te[i], 0, fidx(i, f, te, nu))),
                      pl.BlockSpec((1, D, tf), lambda i, f, te, nu: (te[i], 0, fidx(i, f, te, nu))),
                      pl.BlockSpec((1, tf, D), lambda i, f, te, nu: (te[i], fidx(i, f, te, nu), 0))],
            out_specs=row(D),
            scratch_shapes=[pltpu.VMEM((tr, D), F32)]),
        out_shape=jax.ShapeDtypeStruct((N, D), BF16),
        compiler_params=_params(("arbitrary", "arbitrary")),
        name="moe_expert_ffn",
    )(tile_expert, n_used, xs_lo, xs_hi, gs_lo, gs_hi, wg, wu, wd)


def _combine_kernel(a_ref, x_ref, ylo_ref, yhi_ref, s1_ref, s2_ref, gf_ref, o_ref, acc_ref):
    c = pl.program_id(0)
    e = pl.program_id(1)

    @pl.when(e == 0)
    def _():
        acc_ref[...] = jnp.zeros_like(acc_ref)

    a, has_lo, has_hi = _tile_meta(a_ref[c * pl.num_programs(1) + e])

    def gather(tile, y_ref):
        s1 = s1_ref[...]
        s2 = s2_ref[...]
        parts = []
        for j in range(MOE_CHUNK // LANE):
            cols = tile * MOE_CHUNK + j * LANE + lax.broadcasted_iota(jnp.int32, (MOE_CHUNK, LANE), 1)
            parts.append(jnp.where((s1 == cols) | (s2 == cols), 1.0, 0.0).astype(BF16))
        acc_ref[...] += _dot(jnp.concatenate(parts, axis=1), y_ref[...])

    @pl.when(has_lo)
    def _():
        gather(a, ylo_ref)

    @pl.when(has_hi)
    def _():
        gather(a + 1, yhi_ref)

    @pl.when(e == pl.num_programs(1) - 1)
    def _():
        o_ref[...] = _rms(x_ref[...] + acc_ref[...], gf_ref[...])


def _combine(a_ce, x2, ys, slot1_rep, slot2_rep, g_final):
    T, D = x2.shape
    C = T // MOE_CHUNK
    E = N_EXPERTS
    return pl.pallas_call(
        _combine_kernel,
        grid_spec=pltpu.PrefetchScalarGridSpec(
            num_scalar_prefetch=1,
            grid=(C, E),
            in_specs=[pl.BlockSpec((MOE_CHUNK, D), lambda c, e, a: (c, 0)),
                      pl.BlockSpec((MOE_CHUNK, D), lambda c, e, a: (a[c * E + e] >> 2, 0)),
                      pl.BlockSpec((MOE_CHUNK, D), lambda c, e, a: ((a[c * E + e] >> 2) + 1, 0)),
                      pl.BlockSpec((MOE_CHUNK, LANE), lambda c, e, a: (c, 0)),
                      pl.BlockSpec((MOE_CHUNK, LANE), lambda c, e, a: (c, 0)),
                      pl.BlockSpec((1, D), lambda c, e, a: (0, 0))],
            out_specs=pl.BlockSpec((MOE_CHUNK, D), lambda c, e, a: (c, 0)),
            scratch_shapes=[pltpu.VMEM((MOE_CHUNK, D), F32)]),
        out_shape=jax.ShapeDtypeStruct((T, D), F32),
        compiler_params=_params(("arbitrary", "arbitrary")),
        name="moe_combine_final_norm",
    )(a_ce, x2, ys, ys, slot1_rep, slot2_rep, g_final)


def _rope_tables(S):
    inv = ROPE_THETA ** (-jnp.arange(ROPE_HALF, dtype=F32) / ROPE_HALF)
    ang = jnp.arange(S, dtype=F32)[:, None] * inv[None, :]
    cos, sin = jnp.cos(ang), jnp.sin(ang)
    pad = HEAD_DIM - ROPE_DIM
    cos_h = jnp.concatenate([cos, cos, jnp.ones((S, pad), F32)], axis=-1)
    sin_h = jnp.concatenate([-sin, sin, jnp.zeros((S, pad), F32)], axis=-1)
    reps = LANE // HEAD_DIM
    return jnp.tile(cos_h, (1, reps)), jnp.tile(sin_h, (1, reps))


def _pack_w_in(w):
    per_group = GROUP * N_BRANCH
    gates = w[:, GATE_OFF:GATE_OFF + N_KV_HEADS * per_group].reshape(-1, N_KV_HEADS, per_group)
    gates = jnp.pad(gates, ((0, 0), (0, LANE // GATE_ROWS - N_KV_HEADS), (0, GATE_ROWS - per_group)))
    gates = gates.reshape(-1, LANE)
    return jnp.concatenate([w[:, :GATE_OFF], gates, w[:, GATE_OFF + N_KV_HEADS * per_group:]], axis=1).astype(BF16)


def _mixer_layer(x, norm_g, w_in, pe_k, w1_k, b1_k, w2_k, pe_v, w1_v, b1_v, w2_v,
                 w_pool, pool_scale, w_out, cos_t, sin_t):
    B, S, D = x.shape
    qt, qrt, kcr, vcr, ks, vst, kw, vwt, gates, u = _mixer_proj(
        x, norm_g.reshape(1, D), _pack_w_in(w_in), cos_t, sin_t)
    half = CMP_LEN // 2
    cw = half * HEAD_DIM
    pad_w2 = lambda w2: jnp.pad(w2, ((0, 0), (0, LANE - HEAD_DIM))).astype(BF16)
    kc, vct = _compress(
        kcr, vcr,
        pe_k.reshape(2, cw), w1_k.reshape(2, cw, CMP_HIDDEN).astype(BF16), b1_k.reshape(1, -1), pad_w2(w2_k),
        pe_v.reshape(2, cw), w1_v.reshape(2, cw, CMP_HIDDEN).astype(BF16), b1_v.reshape(1, -1), pad_w2(w2_v))
    a = _nsa_attention(qt, qrt, kc, vct, ks, vst, kw, vwt, gates)
    p = _pool_mixer(u, w_pool.astype(BF16), pool_scale.reshape(1, -1))
    w_out = w_out.astype(BF16)
    out = _out_proj(x.reshape(B * S, D), a.reshape(B * S, NSA_WIDTH), p.reshape(B * S, POOL_WIDTH),
                    w_out[:NSA_WIDTH], w_out[NSA_WIDTH:])
    return out.reshape(B, S, D)


def _moe_layer(x2, g_ffn, router, wg, wu, wd, g_final):
    T, D = x2.shape
    E = N_EXPERTS
    C = T // MOE_CHUNK
    i32 = jnp.int32
    w_router = jnp.pad(router, ((0, 0), (0, LANE - E)))
    h, route, gates, before, total = _router(x2, g_ffn.reshape(1, D), w_router)

    counts = total[0, :E].astype(i32)
    tiles = (counts + MOE_ROW_TILE - 1) // MOE_ROW_TILE
    ends = jnp.cumsum(tiles)
    off = (ends - tiles) * MOE_ROW_TILE
    e1, e2, r1, r2 = route[:, 0], route[:, 1], route[:, 2], route[:, 3]
    slot1 = off[e1] + r1
    slot2 = off[e2] + r2
    first = off[None, :] + before[:, 0, :E].astype(i32)
    after = jnp.concatenate([before[1:, 0, :E], total[:, :E]], axis=0).astype(i32)
    last = off[None, :] + after - 1
    lo_tile = first // MOE_CHUNK
    has_lo = (last >= first).astype(i32)
    has_hi = ((last >= first) & (last // MOE_CHUNK > lo_tile)).astype(i32)
    tile_meta = lo_tile * 4 + has_hi * 2 + has_lo
    n_row_tiles = (2 * T) // MOE_ROW_TILE + E + 1 + MOE_CHUNK // MOE_ROW_TILE
    n_used = ends[-1:]
    tile_ids = jnp.minimum(jnp.arange(n_row_tiles, dtype=i32), n_used[0] - 1)
    tile_expert = jnp.minimum(jnp.sum(ends[None, :] <= tile_ids[:, None], axis=1), E - 1).astype(i32)

    zero = jnp.zeros_like(slot1)
    tok = jnp.stack([slot1, slot2, zero, zero, zero, zero, zero, zero])
    tok_rows = tok.reshape(8, C, MOE_CHUNK).transpose(1, 0, 2)
    gz = jnp.zeros((T,), F32)
    gate_rows = jnp.stack([gates[:, 0], gates[:, 1], gz, gz, gz, gz, gz, gz]).reshape(8, C, MOE_CHUNK).transpose(1, 0, 2)
    slot1_rep = jnp.broadcast_to(slot1[:, None], (T, LANE))
    slot2_rep = jnp.broadcast_to(slot2[:, None], (T, LANE))

    xs_lo, xs_hi, gs_lo, gs_hi = _dispatch(tile_meta.T.reshape(-1).astype(i32), h, tok_rows, gate_rows,
                                           n_row_tiles * MOE_ROW_TILE)
    ys = _expert_ffn(tile_expert, n_used.astype(i32), xs_lo, xs_hi, gs_lo, gs_hi,
                     wg.astype(BF16), wu.astype(BF16), wd.astype(BF16))
    return _combine(tile_meta.reshape(-1).astype(i32), x2, ys, slot1_rep, slot2_rep, g_final.reshape(1, D))


def kernel(x, mem, norm_mix, w_in, cmp_pe_k, cmp_w1_k, cmp_b1_k, cmp_w2_k, cmp_pe_v, cmp_w1_v, cmp_b1_v, cmp_w2_v, w_pool, pool_scale, w_out, norm_x, norm_mem, wq_x, wk_x, wv_x, wo_x, norm_ffn, ffn_wg, ffn_wu, ffn_wd, moe_router, moe_wg, moe_wu, moe_wd, norm_final):
    B, S, D = x.shape
    depth = norm_mix.shape[0]
    assert depth == 2, "the final RMSNorm is fused into the expert layer, which must be the last one"
    cos_t, sin_t = _rope_tables(S)
    for layer in range(depth):
        x = _mixer_layer(x, norm_mix[layer], w_in[layer],
                         cmp_pe_k[layer], cmp_w1_k[layer], cmp_b1_k[layer], cmp_w2_k[layer],
                         cmp_pe_v[layer], cmp_w1_v[layer], cmp_b1_v[layer], cmp_w2_v[layer],
                         w_pool[layer], pool_scale[layer], w_out[layer], cos_t, sin_t)
        wkv = jnp.concatenate([wk_x[layer], wv_x[layer]], axis=1).astype(BF16)
        kx, vx = _xattn_kv(mem, norm_mem[layer].reshape(1, D), wkv)
        x = _xattn(x, norm_x[layer].reshape(1, D), wq_x[layer].astype(BF16), kx, vx, wo_x[layer].astype(BF16))
        x2 = x.reshape(B * S, D)
        j = layer // 2
        if layer % 2 == 0:
            x2 = _ffn(x2, norm_ffn[layer].reshape(1, D), ffn_wg[j].astype(BF16), ffn_wu[j].astype(BF16),
                      ffn_wd[j].astype(BF16))
        else:
            x2 = _moe_layer(x2, norm_ffn[layer], moe_router[j], moe_wg[j], moe_wu[j], moe_wd[j], norm_final)
        x = x2.reshape(B, S, D)
    return x
```

```python
import functools
import math

import numpy as np
import jax
import jax.numpy as jnp
from jax import lax
from jax.experimental import pallas as pl
from jax.experimental.pallas import tpu as pltpu

F32 = jnp.float32
BF16 = jnp.bfloat16

D_MODEL = 1024
HEAD_DIM = 64
N_HEADS = 8
N_KV_HEADS = 2
GROUP = N_HEADS // N_KV_HEADS
NSA_WIDTH = N_HEADS * HEAD_DIM
KV_WIDTH = N_KV_HEADS * HEAD_DIM
N_BRANCH = 3
GATE_ROWS = 16
POOL_WIDTH = 512
POOL_WINDOWS = (2, 4, 8, 16)
POOL_GROUP = 128
POOL_HALO = 16
ROPE_DIM = 16
ROPE_HALF = 8
ROPE_THETA = 500000.0
CMP_LEN = 32
CMP_STRIDE = 16
CMP_HIDDEN = 256
SEL_BLOCK = 64
SEL_TOP_N = 16
N_LOCAL_BLOCKS = 2
WINDOW = 512
X_HEADS = 4
X_HEAD_DIM = 256
N_EXPERTS = 8
EPS = 1e-6

LANE = 128
IN_PACKED = NSA_WIDTH + 6 * KV_WIDTH + LANE + POOL_WIDTH
GATE_OFF = NSA_WIDTH + 6 * KV_WIDTH
POOL_OFF = GATE_OFF + LANE

NEG = -1e30
TINY = float(np.finfo(np.float32).tiny)
VMEM_LIMIT = 56 * 1024 * 1024


def _dot(a, b):
    return jnp.dot(a, b, preferred_element_type=F32)


def _dot_nt(a, b):
    return lax.dot_general(a, b, (((1,), (1,)), ((), ())), preferred_element_type=F32)


def _rms(x, g):
    y = x * lax.rsqrt(jnp.mean(x * x, axis=-1, keepdims=True) + EPS)
    return y * g


def _params(sem, limit=VMEM_LIMIT):
    return pltpu.CompilerParams(dimension_semantics=sem, vmem_limit_bytes=limit)


def _mixer_proj_kernel(x_ref, g_ref, w_ref, cos_ref, sin_ref, wpool_ref, pscale_ref,
                       qt_ref, qrt_ref, kcr_ref, vcr_ref, ks_ref, vst_ref, kw_ref, vwt_ref,
                       gate_ref, p_ref, pool_buf):
    h = _rms(x_ref[0], g_ref[...]).astype(BF16)
    z = _dot(h, w_ref[...])
    tm = z.shape[0]
    cos = cos_ref[...]
    sin = sin_ref[...]
    lane = lax.broadcasted_iota(jnp.int32, (tm, LANE), 1)
    first = (lane & (HEAD_DIM - 1)) < ROPE_HALF
    scale = HEAD_DIM ** -0.5 * math.log2(math.e)

    def rope(xs):
        partner = jnp.where(first, pltpu.roll(xs, LANE - ROPE_HALF, 1), pltpu.roll(xs, ROPE_HALF, 1))
        return xs * cos + partner * sin

    for s in range(NSA_WIDTH // LANE):
        xs = z[:, s * LANE:(s + 1) * LANE]
        for src, ref in ((xs, qt_ref), (rope(xs), qrt_ref)):
            t = (src * scale).T.astype(BF16)
            ref[0, 2 * s] = t[:HEAD_DIM]
            ref[0, 2 * s + 1] = t[HEAD_DIM:]

    def kv_slab(i):
        return z[:, NSA_WIDTH + i * KV_WIDTH:NSA_WIDTH + (i + 1) * KV_WIDTH]

    for slab, ref in ((kv_slab(0), kcr_ref), (kv_slab(1), vcr_ref),
                      (rope(kv_slab(2)), ks_ref), (rope(kv_slab(4)), kw_ref)):
        for gg in range(N_KV_HEADS):
            ref[0, gg] = slab[:, gg * HEAD_DIM:(gg + 1) * HEAD_DIM].astype(ref.dtype)

    for slab, ref in ((kv_slab(3), vst_ref), (kv_slab(5), vwt_ref)):
        t = slab.T.astype(BF16)
        for gg in range(N_KV_HEADS):
            for k in range(tm // LANE):
                ref[0, gg, k] = t[gg * HEAD_DIM:(gg + 1) * HEAD_DIM, k * LANE:(k + 1) * LANE]

    sig_t = jax.nn.sigmoid(z[:, GATE_OFF:GATE_OFF + LANE]).T
    for gg in range(N_KV_HEADS):
        gate_ref[0, gg] = sig_t[gg * GATE_ROWS:(gg + 1) * GATE_ROWS]

    i = pl.program_id(1)

    @pl.when(i == 0)
    def _():
        pool_buf[0:POOL_HALO, :] = jnp.zeros((POOL_HALO, POOL_WIDTH), F32)

    @pl.when(i > 0)
    def _():
        pool_buf[0:POOL_HALO, :] = pool_buf[tm:tm + POOL_HALO, :]

    pool_buf[POOL_HALO:POOL_HALO + tm, :] = z[:, POOL_OFF:POOL_OFF + POOL_WIDTH]
    t1 = i * tm + lax.broadcasted_iota(jnp.int32, (tm, 1), 0) + 1
    for gi, w in enumerate(POOL_WINDOWS):
        cols = slice(gi * POOL_GROUP, (gi + 1) * POOL_GROUP)
        cur = pool_buf[POOL_HALO:POOL_HALO + tm, cols]
        tot = cur
        for k in range(1, w):
            tot = tot + pool_buf[POOL_HALO - k:POOL_HALO - k + tm, cols]
        d = tot / jnp.minimum(t1, w).astype(F32) - cur
        p_ref[0, :, cols] = (_dot(d.astype(BF16), wpool_ref[gi]) * pscale_ref[:, cols]).astype(p_ref.dtype)


def _mixer_proj(x, g, w_packed, cos_t, sin_t, w_pool, pool_scale, tm=512):
    B, S, D = x.shape
    G = N_KV_HEADS
    sd = jax.ShapeDtypeStruct
    out_shape = (sd((B, N_HEADS, HEAD_DIM, S), BF16), sd((B, N_HEADS, HEAD_DIM, S), BF16),
                 sd((B, G, S, HEAD_DIM), F32), sd((B, G, S, HEAD_DIM), F32),
                 sd((B, G, S, HEAD_DIM), BF16), sd((B, G, S // LANE, HEAD_DIM, LANE), BF16),
                 sd((B, G, S, HEAD_DIM), BF16), sd((B, G, S // LANE, HEAD_DIM, LANE), BF16),
                 sd((B, G, GATE_ROWS, S), F32),
                 sd((B, S, POOL_WIDTH), BF16))
    qspec = pl.BlockSpec((1, N_HEADS, HEAD_DIM, tm), lambda b, i: (b, 0, 0, i))
    kspec = pl.BlockSpec((1, G, tm, HEAD_DIM), lambda b, i: (b, 0, i, 0))
    vspec = pl.BlockSpec((1, G, tm // LANE, HEAD_DIM, LANE), lambda b, i: (b, 0, i, 0, 0))
    out_specs = (qspec, qspec, kspec, kspec, kspec, vspec, kspec, vspec,
                 pl.BlockSpec((1, G, GATE_ROWS, tm), lambda b, i: (b, 0, 0, i)),
                 pl.BlockSpec((1, tm, POOL_WIDTH), lambda b, i: (b, i, 0)))
    return pl.pallas_call(
        _mixer_proj_kernel,
        grid=(B, S // tm),
        in_specs=[pl.BlockSpec((1, tm, D), lambda b, i: (b, i, 0)),
                  pl.BlockSpec((1, D), lambda b, i: (0, 0)),
                  pl.BlockSpec((D, IN_PACKED), lambda b, i: (0, 0)),
                  pl.BlockSpec((tm, LANE), lambda b, i: (i, 0)),
                  pl.BlockSpec((tm, LANE), lambda b, i: (i, 0)),
                  pl.BlockSpec(w_pool.shape, lambda b, i: (0, 0, 0)),
                  pl.BlockSpec((1, POOL_WIDTH), lambda b, i: (0, 0))],
        out_specs=out_specs,
        out_shape=out_shape,
        scratch_shapes=[pltpu.VMEM((tm + POOL_HALO, POOL_WIDTH), F32)],
        compiler_params=_params(("parallel", "arbitrary")),
        name="mixer_proj",
    )(x, g, w_packed, cos_t, sin_t, w_pool, pool_scale)


def _gelu_tanh(x):
    return 0.5 * x * (1.0 + jnp.tanh(math.sqrt(2.0 / math.pi) * (x + 0.044715 * (x * x * x))))


def _compress_kernel(kc_ref, vc_ref, pek_ref, w1k_ref, b1k_ref, w2k_ref,
                     pev_ref, w1v_ref, b1v_ref, w2v_ref, ko_ref, vto_ref):
    def comp(x_ref, pe_ref, w1_ref, b1_ref, w2_ref):
        x = x_ref[0, 0]
        n = x.shape[0]
        a = _dot((x + pe_ref[0:1, :]).astype(BF16), w1_ref[0])
        b = _dot((x + pe_ref[1:2, :]).astype(BF16), w1_ref[1])
        pre = a + pltpu.roll(b, n - 1, 0) + b1_ref[...]
        return _dot(_gelu_tanh(pre).astype(BF16), w2_ref[...])

    ko_ref[0, 0] = comp(kc_ref, pek_ref, w1k_ref, b1k_ref, w2k_ref)[:, :HEAD_DIM].astype(ko_ref.dtype)
    vto_ref[0, 0] = comp(vc_ref, pev_ref, w1v_ref, b1v_ref, w2v_ref).T[:HEAD_DIM].astype(vto_ref.dtype)


def _compress(kcr, vcr, pek, w1k, b1k, w2k, pev, w1v, b1v, w2v):
    B, G, S, _ = kcr.shape
    n = S // CMP_STRIDE
    cw = CMP_STRIDE * HEAD_DIM
    kc = kcr.reshape(B, G, n, cw)
    vc = vcr.reshape(B, G, n, cw)
    xspec = pl.BlockSpec((1, 1, n, cw), lambda b, g: (b, g, 0, 0))
    full = lambda a: pl.BlockSpec(a.shape, lambda b, g: (0,) * a.ndim)
    return pl.pallas_call(
        _compress_kernel,
        grid=(B, G),
        in_specs=[xspec, xspec, full(pek), full(w1k), full(b1k), full(w2k),
                  full(pev), full(w1v), full(b1v), full(w2v)],
        out_specs=(pl.BlockSpec((1, 1, n, HEAD_DIM), lambda b, g: (b, g, 0, 0)),
                   pl.BlockSpec((1, 1, HEAD_DIM, n), lambda b, g: (b, g, 0, 0))),
        out_shape=(jax.ShapeDtypeStruct((B, G, n, HEAD_DIM), BF16),
                   jax.ShapeDtypeStruct((B, G, HEAD_DIM, n), BF16)),
        compiler_params=_params(("parallel", "parallel")),
        name="compress_kv",
    )(kc, vc, pek, w1k, b1k, w2k, pev, w1v, b1v, w2v)


def _nsa_kernel(qt_ref, qrt_ref, kc_ref, vct_ref, ks_ref, vst_ref, kw_ref, vwt_ref, gate_ref,
                covt_ref, o_ref, selbias_ref, *, tq, tk):
    qt = pl.program_id(2)
    q0 = qt * tq
    n_cmp = kc_ref.shape[2]
    n_blk = covt_ref.shape[0]
    heads = lambda ref: jnp.concatenate([ref[0, hh] for hh in range(GROUP)], axis=1)
    q_t = heads(qt_ref)
    qr_t = heads(qrt_ref)
    t_lane = q0 + lax.broadcasted_iota(jnp.int32, (1, tq), 1)
    per_head = lambda a: [a[:, hh * tq:(hh + 1) * tq] for hh in range(GROUP)]
    all_heads = lambda a: jnp.concatenate([a] * GROUP, axis=1)

    sc = _dot(kc_ref[0, 0], q_t)
    cmp_end = lax.broadcasted_iota(jnp.int32, (n_cmp, 1), 0) * CMP_STRIDE + (CMP_LEN - 1)
    valid = cmp_end <= t_lane
    pcs = []
    for s_h in per_head(sc):
        s_h = jnp.where(valid, s_h, NEG)
        e = jnp.where(valid, jnp.exp2(s_h - jnp.max(s_h, axis=0, keepdims=True)), 0.0)
        pcs.append(e * (1.0 / jnp.maximum(jnp.sum(e, axis=0, keepdims=True), TINY)))
    oc_t = _dot(vct_ref[0, 0], jnp.concatenate(pcs, axis=1).astype(BF16))

    psum = pcs[0] + pcs[1] + pcs[2] + pcs[3]
    p_hi = psum.astype(BF16)
    r1 = psum - p_hi.astype(F32)
    p_mid = r1.astype(BF16)
    p_lo = (r1 - p_mid.astype(F32)).astype(BF16)
    covt = covt_ref[...]
    imp_t = _dot(covt, p_hi) + _dot(covt, p_mid) + _dot(covt, p_lo)

    jb = lax.broadcasted_iota(jnp.int32, (n_blk, tq), 0)
    tb = (q0 + lax.broadcasted_iota(jnp.int32, (n_blk, tq), 1)) // SEL_BLOCK
    dist = tb - jb
    forced = (jb == 0) | ((dist >= 0) & (dist < N_LOCAL_BLOCKS))
    score = jnp.where(jb > tb, -jnp.inf, jnp.where(forced, jnp.inf, imp_t))
    rank = jnp.zeros((n_blk, tq), jnp.int32)
    for jp in range(n_blk):
        rowv = score[jp:jp + 1, :]
        beats = (rowv > score) | ((rowv == score) & (jb > jp))
        rank = rank + jnp.where(beats, 1, 0)
    selbias_ref[...] = jnp.where(rank < min(SEL_TOP_N, n_blk), 0.0, NEG)

    R = GROUP * tq

    def with_ones(v_t):
        extra = jnp.where(lax.broadcasted_iota(jnp.int32, (16, v_t.shape[1]), 0) == 0, 1.0, 0.0)
        return jnp.concatenate([v_t, extra.astype(BF16)], axis=0)

    bpc = tk // SEL_BLOCK
    vpc = tk // LANE

    def sel_chunk(c, carry, causal):
        m_i, l_i, acc = carry
        start = pl.multiple_of(c * tk, tk)
        k_c = ks_ref[0, 0, pl.ds(start, tk), :]
        v_t = jnp.concatenate([vst_ref[0, 0, c * vpc + k] for k in range(vpc)], axis=1)
        rows = [jnp.broadcast_to(selbias_ref[pl.ds(c * bpc + j, 1), :], (SEL_BLOCK, tq)) for j in range(bpc)]
        bias = jnp.concatenate(rows, axis=0)
        if causal:
            kpos = start + lax.broadcasted_iota(jnp.int32, (tk, 1), 0)
            bias = jnp.where(kpos <= t_lane, bias, NEG)
        s = _dot(k_c, qr_t) + all_heads(bias)
        m_new = jnp.maximum(m_i, jnp.max(s, axis=0, keepdims=True))
        alpha = jnp.exp2(m_i - m_new)
        pv = _dot(with_ones(v_t), jnp.exp2(s - m_new).astype(BF16))
        return m_new, alpha * l_i + pv[HEAD_DIM:HEAD_DIM + 1], alpha * acc + pv[:HEAD_DIM]

    init = (jnp.full((1, R), NEG, F32), jnp.zeros((1, R), F32), jnp.zeros((HEAD_DIM, R), F32))
    c_diag = q0 // tk
    carry = lax.fori_loop(0, c_diag, lambda c, cr: sel_chunk(c, cr, False), init)
    _, l_s, acc_s = sel_chunk(c_diag, carry, True)
    os_t = acc_s * (1.0 / jnp.maximum(l_s, TINY))

    wk = WINDOW + tq
    w0 = pl.multiple_of(jnp.maximum(q0 - WINDOW, 0), LANE)
    k_w = kw_ref[0, 0, pl.ds(w0, wk), :]
    v_t = jnp.concatenate([vwt_ref[0, 0, w0 // LANE + k] for k in range(wk // LANE)], axis=1)
    diff = t_lane - (w0 + lax.broadcasted_iota(jnp.int32, (wk, 1), 0))
    bias = jnp.where((diff >= 0) & (diff < WINDOW), 0.0, NEG)
    s = _dot(k_w, qr_t) + all_heads(bias)
    pv = _dot(with_ones(v_t), jnp.exp2(s - jnp.max(s, axis=0, keepdims=True)).astype(BF16))
    ow_t = pv[:HEAD_DIM] * (1.0 / pv[HEAD_DIM:HEAD_DIM + 1])

    gt = gate_ref[0, 0]
    outs = []
    for hh, (a, b, c) in enumerate(zip(per_head(oc_t), per_head(os_t), per_head(ow_t))):
        r = hh * N_BRANCH
        outs.append(gt[r:r + 1] * a + gt[r + 1:r + 2] * b + gt[r + 2:r + 3] * c)
    o_ref[0] = jnp.concatenate(outs, axis=0).T.astype(o_ref.dtype)


def _cover_table(S):
    n_cmp = S // CMP_STRIDE
    n_blk = S // SEL_BLOCK
    cs = np.arange(n_cmp) * CMP_STRIDE
    ss = np.arange(n_blk) * SEL_BLOCK
    cover_t = ((cs[None, :] < ss[:, None] + SEL_BLOCK) & (cs[None, :] + CMP_LEN > ss[:, None]))
    cover_t[:, n_cmp - 1] = False
    return jnp.asarray(cover_t, BF16)


def _nsa_attention(qt, qrt, kc, vct, ks, vst, kw, vwt, gates, tq=256, tk=1024):
    B, _, _, S = qt.shape
    tk = min(tk, S)
    assert tq % LANE == 0 and tk % tq == 0 and S % tk == 0 and S >= WINDOW + tq
    covt = _cover_table(S)
    n_cmp = kc.shape[2]
    n_blk = covt.shape[0]
    qspec = pl.BlockSpec((1, GROUP, HEAD_DIM, tq), lambda b, g, i: (b, g, 0, i))
    kspec = lambda a: pl.BlockSpec((1, 1) + a.shape[2:], lambda b, g, i: (b, g, 0, 0))
    vspec = pl.BlockSpec((1, 1, S // LANE, HEAD_DIM, LANE), lambda b, g, i: (b, g, 0, 0, 0))
    return pl.pallas_call(
        functools.partial(_nsa_kernel, tq=tq, tk=tk),
        grid=(B, N_KV_HEADS, S // tq),
        in_specs=[qspec, qspec,
                  pl.BlockSpec((1, 1, n_cmp, HEAD_DIM), lambda b, g, i: (b, g, 0, 0)),
                  pl.BlockSpec((1, 1, HEAD_DIM, n_cmp), lambda b, g, i: (b, g, 0, 0)),
                  kspec(ks), vspec, kspec(kw), vspec,
                  pl.BlockSpec((1, 1, GATE_ROWS, tq), lambda b, g, i: (b, g, 0, i)),
                  pl.BlockSpec(covt.shape, lambda b, g, i: (0, 0))],
        out_specs=pl.BlockSpec((1, tq, GROUP * HEAD_DIM), lambda b, g, i: (b, i, g)),
        out_shape=jax.ShapeDtypeStruct((B, S, NSA_WIDTH), BF16),
        scratch_shapes=[pltpu.VMEM((n_blk, tq), F32)],
        compiler_params=_params(("parallel", "parallel", "parallel")),
        name="nsa_attention",
    )(qt, qrt, kc, vct, ks, vst, kw, vwt, gates, covt)


def _xkv_kernel(m_ref, g_ref, w_ref, k_ref, v_ref):
    h = _rms(m_ref[0], g_ref[...]).astype(BF16)
    kv = _dot(h, w_ref[...])
    d = k_ref.shape[2]
    k_ref[0] = kv[:, :d].astype(k_ref.dtype)
    v_ref[0] = kv[:, d:].astype(v_ref.dtype)


def _xattn_kv(mem, g, wkv):
    B, M, D = mem.shape
    ospec = pl.BlockSpec((1, M, D), lambda b: (b, 0, 0))
    oshape = jax.ShapeDtypeStruct((B, M, D), BF16)
    return pl.pallas_call(
        _xkv_kernel,
        grid=(B,),
        in_specs=[pl.BlockSpec((1, M, D), lambda b: (b, 0, 0)),
                  pl.BlockSpec((1, D), lambda b: (0, 0)),
                  pl.BlockSpec(wkv.shape, lambda b: (0, 0))],
        out_specs=(ospec, ospec),
        out_shape=(oshape, oshape),
        compiler_params=_params(("parallel",)),
        name="xattn_kv",
    )(mem, g, wkv)


def _xattn_kernel(x_ref, a_ref, p_ref, wa_ref, wp_ref, g_ref, wq_ref, k_ref, v_ref, wo_ref, o_ref):
    x = x_ref[0] + _dot(a_ref[0], wa_ref[...]) + _dot(p_ref[0], wp_ref[...])
    h = _rms(x, g_ref[...]).astype(BF16)
    scale = X_HEAD_DIM ** -0.5
    q = (_dot(h, wq_ref[...]) * scale).astype(BF16)
    outs = []
    for hd in range(X_HEADS):
        sl = slice(hd * X_HEAD_DIM, (hd + 1) * X_HEAD_DIM)
        s = _dot_nt(q[:, sl], k_ref[0, :, sl])
        e = jnp.exp(s - jnp.max(s, axis=-1, keepdims=True))
        p = e / jnp.sum(e, axis=-1, keepdims=True)
        outs.append(_dot(p.astype(BF16), v_ref[0, :, sl]).astype(BF16))
    o = jnp.concatenate(outs, axis=-1)
    o_ref[0] = x + _dot(o, wo_ref[...])


def _xattn(x, a, p, wa, wp, g, wq, kx, vx, wo, tm=512):
    B, S, D = x.shape
    M = kx.shape[1]
    row = lambda n: pl.BlockSpec((1, tm, n), lambda b, i: (b, i, 0))
    full = lambda w: pl.BlockSpec(w.shape, lambda b, i: (0, 0))
    mspec = pl.BlockSpec((1, M, D), lambda b, i: (b, 0, 0))
    return pl.pallas_call(
        _xattn_kernel,
        grid=(B, S // tm),
        in_specs=[row(D), row(a.shape[2]), row(p.shape[2]), full(wa), full(wp),
                  full(g), full(wq), mspec, mspec, full(wo)],
        out_specs=row(D),
        out_shape=jax.ShapeDtypeStruct((B, S, D), F32),
        compiler_params=_params(("parallel", "parallel")),
        name="out_proj_xattn",
    )(x, a, p, wa, wp, g, wq, kx, vx, wo)


def _silu(x):
    return x * jax.nn.sigmoid(x)


def _ffn_kernel(x_ref, g_ref, wg_ref, wu_ref, wd_ref, o_ref, h_ref, acc_ref):
    f = pl.program_id(1)

    @pl.when(f == 0)
    def _():
        h_ref[...] = _rms(x_ref[...], g_ref[...]).astype(BF16)
        acc_ref[...] = jnp.zeros_like(acc_ref)

    h = h_ref[...]
    act = _silu(_dot(h, wg_ref[...])) * _dot(h, wu_ref[...])
    acc_ref[...] += _dot(act.astype(BF16), wd_ref[...])

    @pl.when(f == pl.num_programs(1) - 1)
    def _():
        o_ref[...] = x_ref[...] + acc_ref[...]


def _ffn(x2, g, wg, wu, wd, tm=512, tf=1408):
    T, D = x2.shape
    F = wg.shape[1]
    return pl.pallas_call(
        _ffn_kernel,
        grid=(T // tm, F // tf),
        in_specs=[pl.BlockSpec((tm, D), lambda i, f: (i, 0)),
                  pl.BlockSpec((1, D), lambda i, f: (0, 0)),
                  pl.BlockSpec((D, tf), lambda i, f: (0, f)),
                  pl.BlockSpec((D, tf), lambda i, f: (0, f)),
                  pl.BlockSpec((tf, D), lambda i, f: (f, 0))],
        out_specs=pl.BlockSpec((tm, D), lambda i, f: (i, 0)),
        out_shape=jax.ShapeDtypeStruct((T, D), F32),
        scratch_shapes=[pltpu.VMEM((tm, D), BF16), pltpu.VMEM((tm, D), F32)],
        compiler_params=_params(("parallel", "arbitrary")),
        name="dense_swiglu",
    )(x2, g, wg, wu, wd)


MOE_CHUNK = 512
MOE_ROW_TILE = 512
assert MOE_ROW_TILE % MOE_CHUNK == 0


def _tile_meta(m):
    return m >> 2, (m & 1) != 0, (m & 2) != 0


def _router_kernel(x_ref, g_ref, wr_ref, tri_ref, h_ref, route_ref, gate_ref, before_ref, total_ref, run_ref):
    c = pl.program_id(0)

    @pl.when(c == 0)
    def _():
        run_ref[...] = jnp.zeros_like(run_ref)

    hf = _rms(x_ref[...], g_ref[...])
    h_ref[...] = hf.astype(BF16)
    logits = jnp.dot(hf, wr_ref[...], preferred_element_type=F32, precision=lax.Precision.HIGHEST)
    lane = lax.broadcasted_iota(jnp.int32, logits.shape, 1)
    logits = jnp.where(lane < N_EXPERTS, logits, -jnp.inf)
    v1 = jnp.max(logits, axis=-1, keepdims=True)
    i1 = jnp.min(jnp.where(logits == v1, lane, LANE), axis=-1, keepdims=True)
    rest = jnp.where(lane == i1, -jnp.inf, logits)
    v2 = jnp.max(rest, axis=-1, keepdims=True)
    i2 = jnp.min(jnp.where(rest == v2, lane, LANE), axis=-1, keepdims=True)
    e2 = jnp.exp(v2 - v1)
    den = 1.0 + e2
    gate_ref[...] = jnp.where(lane == 0, 1.0 / den, jnp.where(lane == 1, e2 / den, 0.0))

    onehot = jnp.where((lane == i1) | (lane == i2), 1.0, 0.0)
    run = run_ref[...]
    rank = run + _dot(tri_ref[...], onehot.astype(BF16))
    r1 = jnp.sum(jnp.where(lane == i1, rank, 0.0), axis=-1, keepdims=True).astype(jnp.int32)
    r2 = jnp.sum(jnp.where(lane == i2, rank, 0.0), axis=-1, keepdims=True).astype(jnp.int32)
    route_ref[...] = jnp.where(lane == 0, i1, jnp.where(lane == 1, i2,
                               jnp.where(lane == 2, r1, jnp.where(lane == 3, r2, 0))))
    before_ref[0] = run
    run = run + jnp.sum(onehot, axis=0, keepdims=True)
    run_ref[...] = run
    total_ref[...] = run


def _router(x2, g, w_router):
    T, D = x2.shape
    C = T // MOE_CHUNK
    tri = jnp.asarray(np.tril(np.ones((MOE_CHUNK, MOE_CHUNK), np.float32), -1), BF16)
    row = lambda n: pl.BlockSpec((MOE_CHUNK, n), lambda c: (c, 0))
    full = lambda a: pl.BlockSpec(a.shape, lambda c: (0, 0))
    return pl.pallas_call(
        _router_kernel,
        grid=(C,),
        in_specs=[row(D), full(g), full(w_router), full(tri)],
        out_specs=(row(D), row(LANE), row(LANE),
                   pl.BlockSpec((1, 1, LANE), lambda c: (c, 0, 0)),
                   pl.BlockSpec((1, LANE), lambda c: (0, 0))),
        out_shape=(jax.ShapeDtypeStruct((T, D), BF16),
                   jax.ShapeDtypeStruct((T, LANE), jnp.int32),
                   jax.ShapeDtypeStruct((T, LANE), F32),
                   jax.ShapeDtypeStruct((C, 1, LANE), F32),
                   jax.ShapeDtypeStruct((1, LANE), F32)),
        scratch_shapes=[pltpu.VMEM((1, LANE), F32)],
        compiler_params=_params(("arbitrary",)),
        name="moe_router",
    )(x2, g, w_router, tri)


def _dispatch_kernel(a_ref, h_ref, tok_ref, gate_ref, zlo_ref, zhi_ref, zglo_ref, zghi_ref,
                     lo_ref, hi_ref, glo_ref, ghi_ref):
    del zlo_ref, zhi_ref, zglo_ref, zghi_ref
    e = pl.program_id(0)
    c = pl.program_id(1)
    idx = e * pl.num_programs(1) + c
    a, has_lo, has_hi = _tile_meta(a_ref[idx])
    first = (c == 0) | (a != _tile_meta(a_ref[jnp.maximum(idx - 1, 0)])[0])

    @pl.when(first)
    def _():
        lo_ref[...] = jnp.zeros_like(lo_ref)
        hi_ref[...] = jnp.zeros_like(hi_ref)
        glo_ref[...] = jnp.zeros_like(glo_ref)
        ghi_ref[...] = jnp.zeros_like(ghi_ref)

    def scatter(tile, x_ref, g_ref):
        tok = tok_ref[0]
        gts = gate_ref[0]
        rows = tile * MOE_CHUNK + lax.broadcasted_iota(jnp.int32, (MOE_CHUNK, MOE_CHUNK), 0)
        c1 = tok[0:1] == rows
        c2 = tok[1:2] == rows
        x_ref[...] += _dot(jnp.where(c1 | c2, 1.0, 0.0).astype(BF16), h_ref[...]).astype(x_ref.dtype)
        g_ref[...] += jnp.sum(jnp.where(c1, gts[0:1], 0.0) + jnp.where(c2, gts[1:2], 0.0), axis=1, keepdims=True)

    @pl.when(has_lo)
    def _():
        scatter(a, lo_ref, glo_ref)

    @pl.when(has_hi)
    def _():
        scatter(a + 1, hi_ref, ghi_ref)


def _dispatch(a_ec, h, tok_rows, gate_rows, n_slots):
    T, D = h.shape
    C = T // MOE_CHUNK
    zx = jnp.zeros((n_slots, D), BF16)
    zg = jnp.zeros((n_slots, 1), F32)
    any_spec = pl.BlockSpec(memory_space=pl.ANY)
    lo = lambda n: pl.BlockSpec((MOE_CHUNK, n), lambda e, c, a: (a[e * C + c] >> 2, 0))
    hi = lambda n: pl.BlockSpec((MOE_CHUNK, n), lambda e, c, a: ((a[e * C + c] >> 2) + 1, 0))
    return pl.pallas_call(
        _dispatch_kernel,
        grid_spec=pltpu.PrefetchScalarGridSpec(
            num_scalar_prefetch=1,
            grid=(N_EXPERTS, C),
            in_specs=[pl.BlockSpec((MOE_CHUNK, D), lambda e, c, a: (c, 0)),
                      pl.BlockSpec((1, 8, MOE_CHUNK), lambda e, c, a: (c, 0, 0)),
                      pl.BlockSpec((1, 8, MOE_CHUNK), lambda e, c, a: (c, 0, 0)),
                      any_spec, any_spec, any_spec, any_spec],
            out_specs=(lo(D), hi(D), lo(1), hi(1))),
        out_shape=(jax.ShapeDtypeStruct(zx.shape, BF16), jax.ShapeDtypeStruct(zx.shape, BF16),
                   jax.ShapeDtypeStruct(zg.shape, F32), jax.ShapeDtypeStruct(zg.shape, F32)),
        input_output_aliases={4: 0, 5: 1, 6: 2, 7: 3},
        compiler_params=_params(("arbitrary", "arbitrary")),
        name="moe_dispatch",
    )(a_ec, h, tok_rows, gate_rows, zx, zx, zg, zg)


def _expert_ffn_kernel(te_ref, nu_ref, lo_ref, hi_ref, glo_ref, ghi_ref, wg_ref, wu_ref, wd_ref, o_ref, acc_ref):
    i = pl.program_id(0)
    f = pl.program_id(1)
    used = i < nu_ref[0]

    @pl.when(used & (f == 0))
    def _():
        acc_ref[...] = jnp.zeros_like(acc_ref)

    @pl.when(used)
    def _():
        x = lo_ref[...] + hi_ref[...]
        act = _silu(_dot(x, wg_ref[0])) * _dot(x, wu_ref[0])
        acc_ref[...] += _dot(act.astype(BF16), wd_ref[0])

    last = f == pl.num_programs(1) - 1

    @pl.when(used & last)
    def _():
        o_ref[...] = (acc_ref[...] * (glo_ref[...] + ghi_ref[...])).astype(o_ref.dtype)

    @pl.when(jnp.logical_not(used) & last)
    def _():
        o_ref[...] = jnp.zeros_like(o_ref)


def _expert_ffn(tile_expert, n_used, xs_lo, xs_hi, gs_lo, gs_hi, wg, wu, wd, tf=1792):
    N, D = xs_lo.shape
    F = wg.shape[2]
    nf = F // tf
    tr = MOE_ROW_TILE
    fidx = lambda i, f, te, nu: jnp.where(i < nu[0], f, nf - 1)
    row = lambda n: pl.BlockSpec((tr, n), lambda i, f, te, nu: (i, 0))
    return pl.pallas_call(
        _expert_ffn_kernel,
        grid_spec=pltpu.PrefetchScalarGridSpec(
            num_scalar_prefetch=2,
            grid=(N // tr, nf),
            in_specs=[row(D), row(D), row(1), row(1),
                      pl.BlockSpec((1, D, tf), lambda i, f, te, nu: (te[i], 0, fidx(i, f, te, nu))),
                      pl.BlockSpec((1, D, tf), lambda i, f, te, nu: (te[i], 0, fidx(i, f, te, nu))),
                      pl.BlockSpec((1, tf, D), lambda i, f, te, nu: (te[i], fidx(i, f, te, nu), 0))],
            out_specs=row(D),
            scratch_shapes=[pltpu.VMEM((tr, D), F32)]),
        out_shape=jax.ShapeDtypeStruct((N, D), BF16),
        compiler_params=_params(("arbitrary", "arbitrary")),
        name="moe_expert_ffn",
    )(tile_expert, n_used, xs_lo, xs_hi, gs_lo, gs_hi, wg, wu, wd)


def _combine_kernel(a_ref, x_ref, ylo_ref, yhi_ref, s1_ref, s2_ref, gf_ref, o_ref, acc_ref):
    c = pl.program_id(0)
    e = pl.program_id(1)

    @pl.when(e == 0)
    def _():
        acc_ref[...] = jnp.zeros_like(acc_ref)

    a, has_lo, has_hi = _tile_meta(a_ref[c * pl.num_programs(1) + e])

    def gather(tile, y_ref):
        s1 = s1_ref[...]
        s2 = s2_ref[...]
        parts = []
        for j in range(MOE_CHUNK // LANE):
            cols = tile * MOE_CHUNK + j * LANE + lax.broadcasted_iota(jnp.int32, (MOE_CHUNK, LANE), 1)
            parts.append(jnp.where((s1 == cols) | (s2 == cols), 1.0, 0.0).astype(BF16))
        acc_ref[...] += _dot(jnp.concatenate(parts, axis=1), y_ref[...])

    @pl.when(has_lo)
    def _():
        gather(a, ylo_ref)

    @pl.when(has_hi)
    def _():
        gather(a + 1, yhi_ref)

    @pl.when(e == pl.num_programs(1) - 1)
    def _():
        o_ref[...] = _rms(x_ref[...] + acc_ref[...], gf_ref[...])


def _combine(a_ce, x2, ys, slot1_rep, slot2_rep, g_final):
    T, D = x2.shape
    C = T // MOE_CHUNK
    E = N_EXPERTS
    return pl.pallas_call(
        _combine_kernel,
        grid_spec=pltpu.PrefetchScalarGridSpec(
            num_scalar_prefetch=1,
            grid=(C, E),
            in_specs=[pl.BlockSpec((MOE_CHUNK, D), lambda c, e, a: (c, 0)),
                      pl.BlockSpec((MOE_CHUNK, D), lambda c, e, a: (a[c * E + e] >> 2, 0)),
                      pl.BlockSpec((MOE_CHUNK, D), lambda c, e, a: ((a[c * E + e] >> 2) + 1, 0)),
                      pl.BlockSpec((MOE_CHUNK, LANE), lambda c, e, a: (c, 0)),
                      pl.BlockSpec((MOE_CHUNK, LANE), lambda c, e, a: (c, 0)),
                      pl.BlockSpec((1, D), lambda c, e, a: (0, 0))],
            out_specs=pl.BlockSpec((MOE_CHUNK, D), lambda c, e, a: (c, 0)),
            scratch_shapes=[pltpu.VMEM((MOE_CHUNK, D), F32)]),
        out_shape=jax.ShapeDtypeStruct((T, D), F32),
        compiler_params=_params(("arbitrary", "arbitrary")),
        name="moe_combine_final_norm",
    )(a_ce, x2, ys, ys, slot1_rep, slot2_rep, g_final)


def _rope_tables(S):
    inv = ROPE_THETA ** (-jnp.arange(ROPE_HALF, dtype=F32) / ROPE_HALF)
    ang = jnp.arange(S, dtype=F32)[:, None] * inv[None, :]
    cos, sin = jnp.cos(ang), jnp.sin(ang)
    pad = HEAD_DIM - ROPE_DIM
    cos_h = jnp.concatenate([cos, cos, jnp.ones((S, pad), F32)], axis=-1)
    sin_h = jnp.concatenate([-sin, sin, jnp.zeros((S, pad), F32)], axis=-1)
    reps = LANE // HEAD_DIM
    return jnp.tile(cos_h, (1, reps)), jnp.tile(sin_h, (1, reps))


def _pack_w_in(w):
    per_group = GROUP * N_BRANCH
    gates = w[:, GATE_OFF:GATE_OFF + N_KV_HEADS * per_group].reshape(-1, N_KV_HEADS, per_group)
    gates = jnp.pad(gates, ((0, 0), (0, LANE // GATE_ROWS - N_KV_HEADS), (0, GATE_ROWS - per_group)))
    gates = gates.reshape(-1, LANE)
    return jnp.concatenate([w[:, :GATE_OFF], gates, w[:, GATE_OFF + N_KV_HEADS * per_group:]], axis=1).astype(BF16)


def _mixer_heads(x, norm_g, w_in, pe_k, w1_k, b1_k, w2_k, pe_v, w1_v, b1_v, w2_v,
                 w_pool, pool_scale, cos_t, sin_t):
    B, S, D = x.shape
    qt, qrt, kcr, vcr, ks, vst, kw, vwt, gates, p = _mixer_proj(
        x, norm_g.reshape(1, D), _pack_w_in(w_in), cos_t, sin_t, w_pool.astype(BF16), pool_scale.reshape(1, -1))
    half = CMP_LEN // 2
    cw = half * HEAD_DIM
    pad_w2 = lambda w2: jnp.pad(w2, ((0, 0), (0, LANE - HEAD_DIM))).astype(BF16)
    kc, vct = _compress(
        kcr, vcr,
        pe_k.reshape(2, cw), w1_k.reshape(2, cw, CMP_HIDDEN).astype(BF16), b1_k.reshape(1, -1), pad_w2(w2_k),
        pe_v.reshape(2, cw), w1_v.reshape(2, cw, CMP_HIDDEN).astype(BF16), b1_v.reshape(1, -1), pad_w2(w2_v))
    return _nsa_attention(qt, qrt, kc, vct, ks, vst, kw, vwt, gates), p


def _moe_layer(x2, g_ffn, router, wg, wu, wd, g_final):
    T, D = x2.shape
    E = N_EXPERTS
    C = T // MOE_CHUNK
    i32 = jnp.int32
    w_router = jnp.pad(router, ((0, 0), (0, LANE - E)))
    h, route, gates, before, total = _router(x2, g_ffn.reshape(1, D), w_router)

    counts = total[0, :E].astype(i32)
    tiles = (counts + MOE_ROW_TILE - 1) // MOE_ROW_TILE
    ends = jnp.cumsum(tiles)
    off = (ends - tiles) * MOE_ROW_TILE
    e1, e2, r1, r2 = route[:, 0], route[:, 1], route[:, 2], route[:, 3]
    slot1 = off[e1] + r1
    slot2 = off[e2] + r2
    first = off[None, :] + before[:, 0, :E].astype(i32)
    after = jnp.concatenate([before[1:, 0, :E], total[:, :E]], axis=0).astype(i32)
    last = off[None, :] + after - 1
    lo_tile = first // MOE_CHUNK
    has_lo = (last >= first).astype(i32)
    has_hi = ((last >= first) & (last // MOE_CHUNK > lo_tile)).astype(i32)
    tile_meta = lo_tile * 4 + has_hi * 2 + has_lo
    n_row_tiles = (2 * T) // MOE_ROW_TILE + E + 1 + MOE_CHUNK // MOE_ROW_TILE
    n_used = ends[-1:]
    tile_ids = jnp.minimum(jnp.arange(n_row_tiles, dtype=i32), n_used[0] - 1)
    tile_expert = jnp.minimum(jnp.sum(ends[None, :] <= tile_ids[:, None], axis=1), E - 1).astype(i32)

    zero = jnp.zeros_like(slot1)
    tok = jnp.stack([slot1, slot2, zero, zero, zero, zero, zero, zero])
    tok_rows = tok.reshape(8, C, MOE_CHUNK).transpose(1, 0, 2)
    gz = jnp.zeros((T,), F32)
    gate_rows = jnp.stack([gates[:, 0], gates[:, 1], gz, gz, gz, gz, gz, gz]).reshape(8, C, MOE_CHUNK).transpose(1, 0, 2)
    slot1_rep = jnp.broadcast_to(slot1[:, None], (T, LANE))
    slot2_rep = jnp.broadcast_to(slot2[:, None], (T, LANE))

    xs_lo, xs_hi, gs_lo, gs_hi = _dispatch(tile_meta.T.reshape(-1).astype(i32), h, tok_rows, gate_rows,
                                           n_row_tiles * MOE_ROW_TILE)
    ys = _expert_ffn(tile_expert, n_used.astype(i32), xs_lo, xs_hi, gs_lo, gs_hi,
                     wg.astype(BF16), wu.astype(BF16), wd.astype(BF16))
    return _combine(tile_meta.reshape(-1).astype(i32), x2, ys, slot1_rep, slot2_rep, g_final.reshape(1, D))


def kernel(x, mem, norm_mix, w_in, cmp_pe_k, cmp_w1_k, cmp_b1_k, cmp_w2_k, cmp_pe_v, cmp_w1_v, cmp_b1_v, cmp_w2_v, w_pool, pool_scale, w_out, norm_x, norm_mem, wq_x, wk_x, wv_x, wo_x, norm_ffn, ffn_wg, ffn_wu, ffn_wd, moe_router, moe_wg, moe_wu, moe_wd, norm_final):
    B, S, D = x.shape
    depth = norm_mix.shape[0]
    assert depth == 2, "the final RMSNorm is fused into the expert layer, which must be the last one"
    cos_t, sin_t = _rope_tables(S)
    for layer in range(depth):
        a, p = _mixer_heads(x, norm_mix[layer], w_in[layer],
                            cmp_pe_k[layer], cmp_w1_k[layer], cmp_b1_k[layer], cmp_w2_k[layer],
                            cmp_pe_v[layer], cmp_w1_v[layer], cmp_b1_v[layer], cmp_w2_v[layer],
                            w_pool[layer], pool_scale[layer], cos_t, sin_t)
        wkv = jnp.concatenate([wk_x[layer], wv_x[layer]], axis=1).astype(BF16)
        kx, vx = _xattn_kv(mem, norm_mem[layer].reshape(1, D), wkv)
        wo_mix = w_out[layer].astype(BF16)
        x = _xattn(x, a, p, wo_mix[:NSA_WIDTH], wo_mix[NSA_WIDTH:], norm_x[layer].reshape(1, D),
                   wq_x[layer].astype(BF16), kx, vx, wo_x[layer].astype(BF16))
        x2 = x.reshape(B * S, D)
        j = layer // 2
        if layer % 2 == 0:
            x2 = _ffn(x2, norm_ffn[layer].reshape(1, D), ffn_wg[j].astype(BF16), ffn_wu[j].astype(BF16),
                      ffn_wd[j].astype(BF16))
        else:
            x2 = _moe_layer(x2, norm_ffn[layer], moe_router[j], moe_wg[j], moe_wu[j], moe_wd[j], norm_final)
        x = x2.reshape(B, S, D)
    return x
```

```python
import functools
import math

import numpy as np
import jax
import jax.numpy as jnp
from jax import lax
from jax.experimental import pallas as pl
from jax.experimental.pallas import tpu as pltpu

F32 = jnp.float32
BF16 = jnp.bfloat16

D_MODEL = 1024
HEAD_DIM = 64
N_HEADS = 8
N_KV_HEADS = 2
GROUP = N_HEADS // N_KV_HEADS
NSA_WIDTH = N_HEADS * HEAD_DIM
KV_WIDTH = N_KV_HEADS * HEAD_DIM
N_BRANCH = 3
GATE_ROWS = 16
POOL_WIDTH = 512
POOL_WINDOWS = (2, 4, 8, 16)
POOL_GROUP = 128
POOL_HALO = 16
ROPE_DIM = 16
ROPE_HALF = 8
ROPE_THETA = 500000.0
CMP_LEN = 32
CMP_STRIDE = 16
CMP_HIDDEN = 256
SEL_BLOCK = 64
SEL_TOP_N = 16
N_LOCAL_BLOCKS = 2
WINDOW = 512
X_HEADS = 4
X_HEAD_DIM = 256
N_EXPERTS = 8
EPS = 1e-6

LANE = 128
IN_PACKED = NSA_WIDTH + 6 * KV_WIDTH + LANE + POOL_WIDTH
GATE_OFF = NSA_WIDTH + 6 * KV_WIDTH
POOL_OFF = GATE_OFF + LANE

NEG = -1e30
TINY = float(np.finfo(np.float32).tiny)
VMEM_LIMIT = 56 * 1024 * 1024


def _dot(a, b):
    return jnp.dot(a, b, preferred_element_type=F32)


def _dot_nt(a, b):
    return lax.dot_general(a, b, (((1,), (1,)), ((), ())), preferred_element_type=F32)


def _rms(x, g):
    y = x * lax.rsqrt(jnp.mean(x * x, axis=-1, keepdims=True) + EPS)
    return y * g


def _params(sem, limit=VMEM_LIMIT):
    return pltpu.CompilerParams(dimension_semantics=sem, vmem_limit_bytes=limit)


def _mixer_proj_kernel(x_ref, g_ref, w_ref, cos_ref, sin_ref, wpool_ref, pscale_ref,
                       qt_ref, qrt_ref, kcr_ref, vcr_ref, ks_ref, vst_ref, kw_ref, vwt_ref,
                       gate_ref, p_ref, pool_buf):
    h = _rms(x_ref[0], g_ref[...]).astype(BF16)
    z = _dot(h, w_ref[...])
    tm = z.shape[0]
    cos = cos_ref[...]
    sin = sin_ref[...]
    lane = lax.broadcasted_iota(jnp.int32, (tm, LANE), 1)
    first = (lane & (HEAD_DIM - 1)) < ROPE_HALF
    scale = HEAD_DIM ** -0.5 * math.log2(math.e)

    def rope(xs):
        partner = jnp.where(first, pltpu.roll(xs, LANE - ROPE_HALF, 1), pltpu.roll(xs, ROPE_HALF, 1))
        return xs * cos + partner * sin

    for s in range(NSA_WIDTH // LANE):
        xs = z[:, s * LANE:(s + 1) * LANE]
        for src, ref in ((xs, qt_ref), (rope(xs), qrt_ref)):
            t = (src * scale).T.astype(BF16)
            ref[0, 2 * s] = t[:HEAD_DIM]
            ref[0, 2 * s + 1] = t[HEAD_DIM:]

    def kv_slab(i):
        return z[:, NSA_WIDTH + i * KV_WIDTH:NSA_WIDTH + (i + 1) * KV_WIDTH]

    for slab, ref in ((kv_slab(0), kcr_ref), (kv_slab(1), vcr_ref),
                      (rope(kv_slab(2)), ks_ref), (rope(kv_slab(4)), kw_ref)):
        for gg in range(N_KV_HEADS):
            ref[0, gg] = slab[:, gg * HEAD_DIM:(gg + 1) * HEAD_DIM].astype(ref.dtype)

    for slab, ref in ((kv_slab(3), vst_ref), (kv_slab(5), vwt_ref)):
        t = slab.T.astype(BF16)
        for gg in range(N_KV_HEADS):
            for k in range(tm // LANE):
                ref[0, gg, k] = t[gg * HEAD_DIM:(gg + 1) * HEAD_DIM, k * LANE:(k + 1) * LANE]

    sig_t = jax.nn.sigmoid(z[:, GATE_OFF:GATE_OFF + LANE]).T
    for gg in range(N_KV_HEADS):
        gate_ref[0, gg] = sig_t[gg * GATE_ROWS:(gg + 1) * GATE_ROWS]

    i = pl.program_id(1)

    @pl.when(i == 0)
    def _():
        pool_buf[0:POOL_HALO, :] = jnp.zeros((POOL_HALO, POOL_WIDTH), F32)

    @pl.when(i > 0)
    def _():
        pool_buf[0:POOL_HALO, :] = pool_buf[tm:tm + POOL_HALO, :]

    pool_buf[POOL_HALO:POOL_HALO + tm, :] = z[:, POOL_OFF:POOL_OFF + POOL_WIDTH]
    t1 = i * tm + lax.broadcasted_iota(jnp.int32, (tm, 1), 0) + 1
    for gi, w in enumerate(POOL_WINDOWS):
        cols = slice(gi * POOL_GROUP, (gi + 1) * POOL_GROUP)
        cur = pool_buf[POOL_HALO:POOL_HALO + tm, cols]
        tot = cur
        for k in range(1, w):
            tot = tot + pool_buf[POOL_HALO - k:POOL_HALO - k + tm, cols]
        d = tot / jnp.minimum(t1, w).astype(F32) - cur
        p_ref[0, :, cols] = (_dot(d.astype(BF16), wpool_ref[gi]) * pscale_ref[:, cols]).astype(p_ref.dtype)


def _mixer_proj(x, g, w_packed, cos_t, sin_t, w_pool, pool_scale, tm=512):
    B, S, D = x.shape
    G = N_KV_HEADS
    sd = jax.ShapeDtypeStruct
    out_shape = (sd((B, N_HEADS, HEAD_DIM, S), BF16), sd((B, N_HEADS, HEAD_DIM, S), BF16),
                 sd((B, G, S, HEAD_DIM), F32), sd((B, G, S, HEAD_DIM), F32),
                 sd((B, G, S, HEAD_DIM), BF16), sd((B, G, S // LANE, HEAD_DIM, LANE), BF16),
                 sd((B, G, S, HEAD_DIM), BF16), sd((B, G, S // LANE, HEAD_DIM, LANE), BF16),
                 sd((B, G, GATE_ROWS, S), F32),
                 sd((B, S, POOL_WIDTH), BF16))
    qspec = pl.BlockSpec((1, N_HEADS, HEAD_DIM, tm), lambda b, i: (b, 0, 0, i))
    kspec = pl.BlockSpec((1, G, tm, HEAD_DIM), lambda b, i: (b, 0, i, 0))
    vspec = pl.BlockSpec((1, G, tm // LANE, HEAD_DIM, LANE), lambda b, i: (b, 0, i, 0, 0))
    out_specs = (qspec, qspec, kspec, kspec, kspec, vspec, kspec, vspec,
                 pl.BlockSpec((1, G, GATE_ROWS, tm), lambda b, i: (b, 0, 0, i)),
                 pl.BlockSpec((1, tm, POOL_WIDTH), lambda b, i: (b, i, 0)))
    return pl.pallas_call(
        _mixer_proj_kernel,
        grid=(B, S // tm),
        in_specs=[pl.BlockSpec((1, tm, D), lambda b, i: (b, i, 0)),
                  pl.BlockSpec((1, D), lambda b, i: (0, 0)),
                  pl.BlockSpec((D, IN_PACKED), lambda b, i: (0, 0)),
                  pl.BlockSpec((tm, LANE), lambda b, i: (i, 0)),
                  pl.BlockSpec((tm, LANE), lambda b, i: (i, 0)),
                  pl.BlockSpec(w_pool.shape, lambda b, i: (0, 0, 0)),
                  pl.BlockSpec((1, POOL_WIDTH), lambda b, i: (0, 0))],
        out_specs=out_specs,
        out_shape=out_shape,
        scratch_shapes=[pltpu.VMEM((tm + POOL_HALO, POOL_WIDTH), F32)],
        compiler_params=_params(("parallel", "arbitrary")),
        name="mixer_proj",
    )(x, g, w_packed, cos_t, sin_t, w_pool, pool_scale)


def _gelu_tanh(x):
    return 0.5 * x * (1.0 + jnp.tanh(math.sqrt(2.0 / math.pi) * (x + 0.044715 * (x * x * x))))


def _compress_kernel(kc_ref, vc_ref, pek_ref, w1k_ref, b1k_ref, w2k_ref,
                     pev_ref, w1v_ref, b1v_ref, w2v_ref, ko_ref, vto_ref):
    def comp(x_ref, pe_ref, w1_ref, b1_ref, w2_ref):
        x = x_ref[0, 0]
        n = x.shape[0]
        a = _dot((x + pe_ref[0:1, :]).astype(BF16), w1_ref[0])
        b = _dot((x + pe_ref[1:2, :]).astype(BF16), w1_ref[1])
        pre = a + pltpu.roll(b, n - 1, 0) + b1_ref[...]
        return _dot(_gelu_tanh(pre).astype(BF16), w2_ref[...])

    ko_ref[0, 0] = comp(kc_ref, pek_ref, w1k_ref, b1k_ref, w2k_ref)[:, :HEAD_DIM].astype(ko_ref.dtype)
    vto_ref[0, 0] = comp(vc_ref, pev_ref, w1v_ref, b1v_ref, w2v_ref).T[:HEAD_DIM].astype(vto_ref.dtype)


def _compress(kcr, vcr, pek, w1k, b1k, w2k, pev, w1v, b1v, w2v):
    B, G, S, _ = kcr.shape
    n = S // CMP_STRIDE
    cw = CMP_STRIDE * HEAD_DIM
    kc = kcr.reshape(B, G, n, cw)
    vc = vcr.reshape(B, G, n, cw)
    xspec = pl.BlockSpec((1, 1, n, cw), lambda b, g: (b, g, 0, 0))
    full = lambda a: pl.BlockSpec(a.shape, lambda b, g: (0,) * a.ndim)
    return pl.pallas_call(
        _compress_kernel,
        grid=(B, G),
        in_specs=[xspec, xspec, full(pek), full(w1k), full(b1k), full(w2k),
                  full(pev), full(w1v), full(b1v), full(w2v)],
        out_specs=(pl.BlockSpec((1, 1, n, HEAD_DIM), lambda b, g: (b, g, 0, 0)),
                   pl.BlockSpec((1, 1, HEAD_DIM, n), lambda b, g: (b, g, 0, 0))),
        out_shape=(jax.ShapeDtypeStruct((B, G, n, HEAD_DIM), BF16),
                   jax.ShapeDtypeStruct((B, G, HEAD_DIM, n), BF16)),
        compiler_params=_params(("parallel", "parallel")),
        name="compress_kv",
    )(kc, vc, pek, w1k, b1k, w2k, pev, w1v, b1v, w2v)


def _nsa_kernel(qt_ref, qrt_ref, kc_ref, vct_ref, ks_ref, vst_ref, kw_ref, vwt_ref, gate_ref,
                covt_ref, o_ref, selbias_ref, *, tq, tk):
    qt = pl.program_id(2)
    q0 = qt * tq
    n_cmp = kc_ref.shape[2]
    n_blk = covt_ref.shape[0]
    heads = lambda ref: jnp.concatenate([ref[0, hh] for hh in range(GROUP)], axis=1)
    q_t = heads(qt_ref)
    qr_t = heads(qrt_ref)
    t_lane = q0 + lax.broadcasted_iota(jnp.int32, (1, tq), 1)
    per_head = lambda a: [a[:, hh * tq:(hh + 1) * tq] for hh in range(GROUP)]
    all_heads = lambda a: jnp.concatenate([a] * GROUP, axis=1)

    sc = _dot(kc_ref[0, 0], q_t)
    cmp_end = lax.broadcasted_iota(jnp.int32, (n_cmp, 1), 0) * CMP_STRIDE + (CMP_LEN - 1)
    valid = cmp_end <= t_lane
    pcs = []
    for s_h in per_head(sc):
        s_h = jnp.where(valid, s_h, NEG)
        e = jnp.where(valid, jnp.exp2(s_h - jnp.max(s_h, axis=0, keepdims=True)), 0.0)
        pcs.append(e * (1.0 / jnp.maximum(jnp.sum(e, axis=0, keepdims=True), TINY)))
    oc_t = _dot(vct_ref[0, 0], jnp.concatenate(pcs, axis=1).astype(BF16))

    psum = pcs[0] + pcs[1] + pcs[2] + pcs[3]
    p_hi = psum.astype(BF16)
    r1 = psum - p_hi.astype(F32)
    p_mid = r1.astype(BF16)
    p_lo = (r1 - p_mid.astype(F32)).astype(BF16)
    covt = covt_ref[...]
    imp_t = _dot(covt, p_hi) + _dot(covt, p_mid) + _dot(covt, p_lo)

    jb = lax.broadcasted_iota(jnp.int32, (n_blk, tq), 0)
    tb = (q0 + lax.broadcasted_iota(jnp.int32, (n_blk, tq), 1)) // SEL_BLOCK
    dist = tb - jb
    forced = (jb == 0) | ((dist >= 0) & (dist < N_LOCAL_BLOCKS))
    score = jnp.where(jb > tb, -jnp.inf, jnp.where(forced, jnp.inf, imp_t))
    rank = jnp.zeros((n_blk, tq), jnp.int32)
    for jp in range(n_blk):
        rowv = score[jp:jp + 1, :]
        beats = (rowv > score) | ((rowv == score) & (jb > jp))
        rank = rank + jnp.where(beats, 1, 0)
    selbias_ref[...] = jnp.where(rank < min(SEL_TOP_N, n_blk), 0.0, NEG)

    R = GROUP * tq

    def with_ones(v_t):
        extra = jnp.where(lax.broadcasted_iota(jnp.int32, (16, v_t.shape[1]), 0) == 0, 1.0, 0.0)
        return jnp.concatenate([v_t, extra.astype(BF16)], axis=0)

    bpc = tk // SEL_BLOCK
    vpc = tk // LANE

    def sel_chunk(c, carry, causal):
        m_i, l_i, acc = carry
        start = pl.multiple_of(c * tk, tk)
        k_c = ks_ref[0, 0, pl.ds(start, tk), :]
        v_t = jnp.concatenate([vst_ref[0, 0, c * vpc + k] for k in range(vpc)], axis=1)
        rows = [jnp.broadcast_to(selbias_ref[pl.ds(c * bpc + j, 1), :], (SEL_BLOCK, tq)) for j in range(bpc)]
        bias = jnp.concatenate(rows, axis=0)
        if causal:
            kpos = start + lax.broadcasted_iota(jnp.int32, (tk, 1), 0)
            bias = jnp.where(kpos <= t_lane, bias, NEG)
        s = _dot(k_c, qr_t) + all_heads(bias)
        m_new = jnp.maximum(m_i, jnp.max(s, axis=0, keepdims=True))
        alpha = jnp.exp2(m_i - m_new)
        pv = _dot(with_ones(v_t), jnp.exp2(s - m_new).astype(BF16))
        return m_new, alpha * l_i + pv[HEAD_DIM:HEAD_DIM + 1], alpha * acc + pv[:HEAD_DIM]

    init = (jnp.full((1, R), NEG, F32), jnp.zeros((1, R), F32), jnp.zeros((HEAD_DIM, R), F32))
    c_diag = q0 // tk
    carry = lax.fori_loop(0, c_diag, lambda c, cr: sel_chunk(c, cr, False), init)
    _, l_s, acc_s = sel_chunk(c_diag, carry, True)
    os_t = acc_s * (1.0 / jnp.maximum(l_s, TINY))

    wk = WINDOW + tq
    w0 = pl.multiple_of(jnp.maximum(q0 - WINDOW, 0), LANE)
    k_w = kw_ref[0, 0, pl.ds(w0, wk), :]
    v_t = jnp.concatenate([vwt_ref[0, 0, w0 // LANE + k] for k in range(wk // LANE)], axis=1)
    diff = t_lane - (w0 + lax.broadcasted_iota(jnp.int32, (wk, 1), 0))
    bias = jnp.where((diff >= 0) & (diff < WINDOW), 0.0, NEG)
    s = _dot(k_w, qr_t) + all_heads(bias)
    pv = _dot(with_ones(v_t), jnp.exp2(s - jnp.max(s, axis=0, keepdims=True)).astype(BF16))
    ow_t = pv[:HEAD_DIM] * (1.0 / pv[HEAD_DIM:HEAD_DIM + 1])

    gt = gate_ref[0, 0]
    outs = []
    for hh, (a, b, c) in enumerate(zip(per_head(oc_t), per_head(os_t), per_head(ow_t))):
        r = hh * N_BRANCH
        outs.append(gt[r:r + 1] * a + gt[r + 1:r + 2] * b + gt[r + 2:r + 3] * c)
    o_ref[0] = jnp.concatenate(outs, axis=0).T.astype(o_ref.dtype)


def _cover_table(S):
    n_cmp = S // CMP_STRIDE
    n_blk = S // SEL_BLOCK
    cs = np.arange(n_cmp) * CMP_STRIDE
    ss = np.arange(n_blk) * SEL_BLOCK
    cover_t = ((cs[None, :] < ss[:, None] + SEL_BLOCK) & (cs[None, :] + CMP_LEN > ss[:, None]))
    cover_t[:, n_cmp - 1] = False
    return jnp.asarray(cover_t, BF16)


def _nsa_attention(qt, qrt, kc, vct, ks, vst, kw, vwt, gates, tq=256, tk=1024):
    B, _, _, S = qt.shape
    tk = min(tk, S)
    assert tq % LANE == 0 and tk % tq == 0 and S % tk == 0 and S >= WINDOW + tq
    covt = _cover_table(S)
    n_cmp = kc.shape[2]
    n_blk = covt.shape[0]
    qspec = pl.BlockSpec((1, GROUP, HEAD_DIM, tq), lambda b, g, i: (b, g, 0, i))
    kspec = lambda a: pl.BlockSpec((1, 1) + a.shape[2:], lambda b, g, i: (b, g, 0, 0))
    vspec = pl.BlockSpec((1, 1, S // LANE, HEAD_DIM, LANE), lambda b, g, i: (b, g, 0, 0, 0))
    return pl.pallas_call(
        functools.partial(_nsa_kernel, tq=tq, tk=tk),
        grid=(B, N_KV_HEADS, S // tq),
        in_specs=[qspec, qspec,
                  pl.BlockSpec((1, 1, n_cmp, HEAD_DIM), lambda b, g, i: (b, g, 0, 0)),
                  pl.BlockSpec((1, 1, HEAD_DIM, n_cmp), lambda b, g, i: (b, g, 0, 0)),
                  kspec(ks), vspec, kspec(kw), vspec,
                  pl.BlockSpec((1, 1, GATE_ROWS, tq), lambda b, g, i: (b, g, 0, i)),
                  pl.BlockSpec(covt.shape, lambda b, g, i: (0, 0))],
        out_specs=pl.BlockSpec((1, tq, GROUP * HEAD_DIM), lambda b, g, i: (b, i, g)),
        out_shape=jax.ShapeDtypeStruct((B, S, NSA_WIDTH), BF16),
        scratch_shapes=[pltpu.VMEM((n_blk, tq), F32)],
        compiler_params=_params(("parallel", "parallel", "parallel")),
        name="nsa_attention",
    )(qt, qrt, kc, vct, ks, vst, kw, vwt, gates, covt)


def _xkv_kernel(m_ref, g_ref, w_ref, k_ref, v_ref):
    h = _rms(m_ref[0], g_ref[...]).astype(BF16)
    kv = _dot(h, w_ref[...])
    d = k_ref.shape[2]
    k_ref[0] = kv[:, :d].astype(k_ref.dtype)
    v_ref[0] = kv[:, d:].astype(v_ref.dtype)


def _xattn_kv(mem, g, wkv):
    B, M, D = mem.shape
    ospec = pl.BlockSpec((1, M, D), lambda b: (b, 0, 0))
    oshape = jax.ShapeDtypeStruct((B, M, D), BF16)
    return pl.pallas_call(
        _xkv_kernel,
        grid=(B,),
        in_specs=[pl.BlockSpec((1, M, D), lambda b: (b, 0, 0)),
                  pl.BlockSpec((1, D), lambda b: (0, 0)),
                  pl.BlockSpec(wkv.shape, lambda b: (0, 0))],
        out_specs=(ospec, ospec),
        out_shape=(oshape, oshape),
        compiler_params=_params(("parallel",)),
        name="xattn_kv",
    )(mem, g, wkv)


def _xattn_kernel(x_ref, a_ref, p_ref, wa_ref, wp_ref, g_ref, wq_ref, k_ref, v_ref, wo_ref, o_ref):
    x = x_ref[0] + _dot(a_ref[0], wa_ref[...]) + _dot(p_ref[0], wp_ref[...])
    h = _rms(x, g_ref[...]).astype(BF16)
    scale = X_HEAD_DIM ** -0.5
    q = (_dot(h, wq_ref[...]) * scale).astype(BF16)
    outs = []
    for hd in range(X_HEADS):
        sl = slice(hd * X_HEAD_DIM, (hd + 1) * X_HEAD_DIM)
        s = _dot_nt(q[:, sl], k_ref[0, :, sl])
        e = jnp.exp(s - jnp.max(s, axis=-1, keepdims=True))
        p = e / jnp.sum(e, axis=-1, keepdims=True)
        outs.append(_dot(p.astype(BF16), v_ref[0, :, sl]).astype(BF16))
    o = jnp.concatenate(outs, axis=-1)
    o_ref[0] = x + _dot(o, wo_ref[...])


def _xattn(x, a, p, wa, wp, g, wq, kx, vx, wo, tm=512):
    B, S, D = x.shape
    M = kx.shape[1]
    row = lambda n: pl.BlockSpec((1, tm, n), lambda b, i: (b, i, 0))
    full = lambda w: pl.BlockSpec(w.shape, lambda b, i: (0, 0))
    mspec = pl.BlockSpec((1, M, D), lambda b, i: (b, 0, 0))
    return pl.pallas_call(
        _xattn_kernel,
        grid=(B, S // tm),
        in_specs=[row(D), row(a.shape[2]), row(p.shape[2]), full(wa), full(wp),
                  full(g), full(wq), mspec, mspec, full(wo)],
        out_specs=row(D),
        out_shape=jax.ShapeDtypeStruct((B, S, D), F32),
        compiler_params=_params(("parallel", "parallel")),
        name="out_proj_xattn",
    )(x, a, p, wa, wp, g, wq, kx, vx, wo)


def _silu(x):
    return x * jax.nn.sigmoid(x)


def _ffn_kernel(x_ref, g_ref, wg_ref, wu_ref, wd_ref, o_ref, h_ref, acc_ref):
    f = pl.program_id(1)

    @pl.when(f == 0)
    def _():
        h_ref[...] = _rms(x_ref[...], g_ref[...]).astype(BF16)
        acc_ref[...] = jnp.zeros_like(acc_ref)

    h = h_ref[...]
    act = _silu(_dot(h, wg_ref[...])) * _dot(h, wu_ref[...])
    acc_ref[...] += _dot(act.astype(BF16), wd_ref[...])

    @pl.when(f == pl.num_programs(1) - 1)
    def _():
        o_ref[...] = x_ref[...] + acc_ref[...]


def _ffn(x2, g, wg, wu, wd, tm=512, tf=1408):
    T, D = x2.shape
    F = wg.shape[1]
    return pl.pallas_call(
        _ffn_kernel,
        grid=(T // tm, F // tf),
        in_specs=[pl.BlockSpec((tm, D), lambda i, f: (i, 0)),
                  pl.BlockSpec((1, D), lambda i, f: (0, 0)),
                  pl.BlockSpec((D, tf), lambda i, f: (0, f)),
                  pl.BlockSpec((D, tf), lambda i, f: (0, f)),
                  pl.BlockSpec((tf, D), lambda i, f: (f, 0))],
        out_specs=pl.BlockSpec((tm, D), lambda i, f: (i, 0)),
        out_shape=jax.ShapeDtypeStruct((T, D), F32),
        scratch_shapes=[pltpu.VMEM((tm, D), BF16), pltpu.VMEM((tm, D), F32)],
        compiler_params=_params(("parallel", "arbitrary")),
        name="dense_swiglu",
    )(x2, g, wg, wu, wd)


MOE_CHUNK = 512
MOE_ROW_TILE = 512
assert MOE_ROW_TILE % MOE_CHUNK == 0


MOE_PARTS = 2


def _tile_meta(m):
    return m >> (2 * MOE_PARTS), [(m & (1 << k)) != 0 for k in range(2 * MOE_PARTS)]


def _router_kernel(x_ref, g_ref, wr_ref, tri_ref, h_ref, route_ref, gate_ref, before_ref, total_ref, run_ref):
    c = pl.program_id(0)

    @pl.when(c == 0)
    def _():
        run_ref[...] = jnp.zeros_like(run_ref)

    hf = _rms(x_ref[...], g_ref[...])
    h_ref[...] = hf.astype(BF16)
    logits = jnp.dot(hf, wr_ref[...], preferred_element_type=F32, precision=lax.Precision.HIGHEST)
    lane = lax.broadcasted_iota(jnp.int32, logits.shape, 1)
    logits = jnp.where(lane < N_EXPERTS, logits, -jnp.inf)
    v1 = jnp.max(logits, axis=-1, keepdims=True)
    i1 = jnp.min(jnp.where(logits == v1, lane, LANE), axis=-1, keepdims=True)
    rest = jnp.where(lane == i1, -jnp.inf, logits)
    v2 = jnp.max(rest, axis=-1, keepdims=True)
    i2 = jnp.min(jnp.where(rest == v2, lane, LANE), axis=-1, keepdims=True)
    e2 = jnp.exp(v2 - v1)
    den = 1.0 + e2
    gate_ref[...] = jnp.where(lane == 0, 1.0 / den, jnp.where(lane == 1, e2 / den, 0.0))

    onehot = jnp.where((lane == i1) | (lane == i2), 1.0, 0.0)
    run = run_ref[...]
    rank = run + _dot(tri_ref[...], onehot.astype(BF16))
    r1 = jnp.sum(jnp.where(lane == i1, rank, 0.0), axis=-1, keepdims=True).astype(jnp.int32)
    r2 = jnp.sum(jnp.where(lane == i2, rank, 0.0), axis=-1, keepdims=True).astype(jnp.int32)
    route_ref[...] = jnp.where(lane == 0, i1, jnp.where(lane == 1, i2,
                               jnp.where(lane == 2, r1, jnp.where(lane == 3, r2, 0))))
    before_ref[0] = run
    run = run + jnp.sum(onehot, axis=0, keepdims=True)
    run_ref[...] = run
    total_ref[...] = run


def _router(x2, g, w_router):
    T, D = x2.shape
    C = T // MOE_CHUNK
    tri = jnp.asarray(np.tril(np.ones((MOE_CHUNK, MOE_CHUNK), np.float32), -1), BF16)
    row = lambda n: pl.BlockSpec((MOE_CHUNK, n), lambda c: (c, 0))
    full = lambda a: pl.BlockSpec(a.shape, lambda c: (0, 0))
    return pl.pallas_call(
        _router_kernel,
        grid=(C,),
        in_specs=[row(D), full(g), full(w_router), full(tri)],
        out_specs=(row(D), row(LANE), row(LANE),
                   pl.BlockSpec((1, 1, LANE), lambda c: (c, 0, 0)),
                   pl.BlockSpec((1, LANE), lambda c: (0, 0))),
        out_shape=(jax.ShapeDtypeStruct((T, D), BF16),
                   jax.ShapeDtypeStruct((T, LANE), jnp.int32),
                   jax.ShapeDtypeStruct((T, LANE), F32),
                   jax.ShapeDtypeStruct((C, 1, LANE), F32),
                   jax.ShapeDtypeStruct((1, LANE), F32)),
        scratch_shapes=[pltpu.VMEM((1, LANE), F32)],
        compiler_params=_params(("arbitrary",)),
        name="moe_router",
    )(x2, g, w_router, tri)


def _dispatch_kernel(a_ref, h_ref, tok_ref, gate_ref, zlo_ref, zhi_ref, zglo_ref, zghi_ref,
                     lo_ref, hi_ref, glo_ref, ghi_ref):
    del zlo_ref, zhi_ref, zglo_ref, zghi_ref
    e = pl.program_id(0)
    c = pl.program_id(1)
    idx = e * pl.num_programs(1) + c
    a, touched = _tile_meta(a_ref[idx])
    first = (c == 0) | (a != _tile_meta(a_ref[jnp.maximum(idx - 1, 0)])[0])

    @pl.when(first)
    def _():
        lo_ref[...] = jnp.zeros_like(lo_ref)
        hi_ref[...] = jnp.zeros_like(hi_ref)
        glo_ref[...] = jnp.zeros_like(glo_ref)
        ghi_ref[...] = jnp.zeros_like(ghi_ref)

    rows_per_part = MOE_CHUNK // MOE_PARTS

    def scatter(tile, part, x_ref, g_ref):
        tok = tok_ref[0]
        gts = gate_ref[0]
        r0 = part * rows_per_part
        rows = tile * MOE_CHUNK + r0 + lax.broadcasted_iota(jnp.int32, (rows_per_part, MOE_CHUNK), 0)
        c1 = tok[0:1] == rows
        c2 = tok[1:2] == rows
        x_ref[r0:r0 + rows_per_part, :] += _dot(jnp.where(c1 | c2, 1.0, 0.0).astype(BF16),
                                                h_ref[...]).astype(x_ref.dtype)
        g_ref[r0:r0 + rows_per_part, :] += jnp.sum(jnp.where(c1, gts[0:1], 0.0) + jnp.where(c2, gts[1:2], 0.0),
                                                   axis=1, keepdims=True)

    for k, (x_ref, g_ref) in enumerate(((lo_ref, glo_ref), (hi_ref, ghi_ref))):
        for part in range(MOE_PARTS):
            pl.when(touched[k * MOE_PARTS + part])(functools.partial(scatter, a + k, part, x_ref, g_ref))


def _dispatch(a_ec, h, tok_rows, gate_rows, n_slots):
    T, D = h.shape
    C = T // MOE_CHUNK
    zx = jnp.zeros((n_slots, D), BF16)
    zg = jnp.zeros((n_slots, 1), F32)
    any_spec = pl.BlockSpec(memory_space=pl.ANY)
    lo = lambda n: pl.BlockSpec((MOE_CHUNK, n), lambda e, c, a: (a[e * C + c] >> (2 * MOE_PARTS), 0))
    hi = lambda n: pl.BlockSpec((MOE_CHUNK, n), lambda e, c, a: ((a[e * C + c] >> (2 * MOE_PARTS)) + 1, 0))
    return pl.pallas_call(
        _dispatch_kernel,
        grid_spec=pltpu.PrefetchScalarGridSpec(
            num_scalar_prefetch=1,
            grid=(N_EXPERTS, C),
            in_specs=[pl.BlockSpec((MOE_CHUNK, D), lambda e, c, a: (c, 0)),
                      pl.BlockSpec((1, 8, MOE_CHUNK), lambda e, c, a: (c, 0, 0)),
                      pl.BlockSpec((1, 8, MOE_CHUNK), lambda e, c, a: (c, 0, 0)),
                      any_spec, any_spec, any_spec, any_spec],
            out_specs=(lo(D), hi(D), lo(1), hi(1))),
        out_shape=(jax.ShapeDtypeStruct(zx.shape, BF16), jax.ShapeDtypeStruct(zx.shape, BF16),
                   jax.ShapeDtypeStruct(zg.shape, F32), jax.ShapeDtypeStruct(zg.shape, F32)),
        input_output_aliases={4: 0, 5: 1, 6: 2, 7: 3},
        compiler_params=_params(("arbitrary", "arbitrary")),
        name="moe_dispatch",
    )(a_ec, h, tok_rows, gate_rows, zx, zx, zg, zg)


def _expert_ffn_kernel(te_ref, nu_ref, lo_ref, hi_ref, glo_ref, ghi_ref, wg_ref, wu_ref, wd_ref, o_ref, acc_ref):
    i = pl.program_id(0)
    f = pl.program_id(1)
    used = i < nu_ref[0]

    @pl.when(used & (f == 0))
    def _():
        acc_ref[...] = jnp.zeros_like(acc_ref)

    @pl.when(used)
    def _():
        x = lo_ref[...] + hi_ref[...]
        act = _silu(_dot(x, wg_ref[0])) * _dot(x, wu_ref[0])
        acc_ref[...] += _dot(act.astype(BF16), wd_ref[0])

    last = f == pl.num_programs(1) - 1

    @pl.when(used & last)
    def _():
        o_ref[...] = (acc_ref[...] * (glo_ref[...] + ghi_ref[...])).astype(o_ref.dtype)

    @pl.when(jnp.logical_not(used) & last)
    def _():
        o_ref[...] = jnp.zeros_like(o_ref)


def _expert_ffn(tile_expert, n_used, xs_lo, xs_hi, gs_lo, gs_hi, wg, wu, wd, tf=1792):
    N, D = xs_lo.shape
    F = wg.shape[2]
    nf = F // tf
    tr = MOE_ROW_TILE
    fidx = lambda i, f, te, nu: jnp.where(i < nu[0], f, nf - 1)
    row = lambda n: pl.BlockSpec((tr, n), lambda i, f, te, nu: (i, 0))
    return pl.pallas_call(
        _expert_ffn_kernel,
        grid_spec=pltpu.PrefetchScalarGridSpec(
            num_scalar_prefetch=2,
            grid=(N // tr, nf),
            in_specs=[row(D), row(D), row(1), row(1),
                      pl.BlockSpec((1, D, tf), lambda i, f, te, nu: (te[i], 0, fidx(i, f, te, nu))),
                      pl.BlockSpec((1, D, tf), lambda i, f, te, nu: (te[i], 0, fidx(i, f, te, nu))),
                      pl.BlockSpec((1, tf, D), lambda i, f, te, nu: (te[i], fidx(i, f, te, nu), 0))],
            out_specs=row(D),
            scratch_shapes=[pltpu.VMEM((tr, D), F32)]),
        out_shape=jax.ShapeDtypeStruct((N, D), BF16),
        compiler_params=_params(("arbitrary", "arbitrary")),
        name="moe_expert_ffn",
    )(tile_expert, n_used, xs_lo, xs_hi, gs_lo, gs_hi, wg, wu, wd)


def _combine_kernel(a_ref, x_ref, ylo_ref, yhi_ref, s1_ref, s2_ref, gf_ref, o_ref, acc_ref):
    c = pl.program_id(0)
    e = pl.program_id(1)

    @pl.when(e == 0)
    def _():
        acc_ref[...] = jnp.zeros_like(acc_ref)

    a, touched = _tile_meta(a_ref[c * pl.num_programs(1) + e])
    cols_per_part = MOE_CHUNK // MOE_PARTS

    def gather(tile, part, y_ref):
        s1 = s1_ref[...]
        s2 = s2_ref[...]
        c0 = part * cols_per_part
        picks = []
        for j in range(cols_per_part // LANE):
            cols = tile * MOE_CHUNK + c0 + j * LANE + lax.broadcasted_iota(jnp.int32, (MOE_CHUNK, LANE), 1)
            picks.append(jnp.where((s1 == cols) | (s2 == cols), 1.0, 0.0).astype(BF16))
        acc_ref[...] += _dot(jnp.concatenate(picks, axis=1), y_ref[c0:c0 + cols_per_part, :])

    for k, y_ref in enumerate((ylo_ref, yhi_ref)):
        for part in range(MOE_PARTS):
            pl.when(touched[k * MOE_PARTS + part])(functools.partial(gather, a + k, part, y_ref))

    @pl.when(e == pl.num_programs(1) - 1)
    def _():
        o_ref[...] = _rms(x_ref[...] + acc_ref[...], gf_ref[...])


def _combine(a_ce, x2, ys, slot1_rep, slot2_rep, g_final):
    T, D = x2.shape
    C = T // MOE_CHUNK
    E = N_EXPERTS
    return pl.pallas_call(
        _combine_kernel,
        grid_spec=pltpu.PrefetchScalarGridSpec(
            num_scalar_prefetch=1,
            grid=(C, E),
            in_specs=[pl.BlockSpec((MOE_CHUNK, D), lambda c, e, a: (c, 0)),
                      pl.BlockSpec((MOE_CHUNK, D), lambda c, e, a: (a[c * E + e] >> (2 * MOE_PARTS), 0)),
                      pl.BlockSpec((MOE_CHUNK, D), lambda c, e, a: ((a[c * E + e] >> (2 * MOE_PARTS)) + 1, 0)),
                      pl.BlockSpec((MOE_CHUNK, LANE), lambda c, e, a: (c, 0)),
                      pl.BlockSpec((MOE_CHUNK, LANE), lambda c, e, a: (c, 0)),
                      pl.BlockSpec((1, D), lambda c, e, a: (0, 0))],
            out_specs=pl.BlockSpec((MOE_CHUNK, D), lambda c, e, a: (c, 0)),
            scratch_shapes=[pltpu.VMEM((MOE_CHUNK, D), F32)]),
        out_shape=jax.ShapeDtypeStruct((T, D), F32),
        compiler_params=_params(("arbitrary", "arbitrary")),
        name="moe_combine_final_norm",
    )(a_ce, x2, ys, ys, slot1_rep, slot2_rep, g_final)


def _rope_tables(S):
    inv = ROPE_THETA ** (-jnp.arange(ROPE_HALF, dtype=F32) / ROPE_HALF)
    ang = jnp.arange(S, dtype=F32)[:, None] * inv[None, :]
    cos, sin = jnp.cos(ang), jnp.sin(ang)
    pad = HEAD_DIM - ROPE_DIM
    cos_h = jnp.concatenate([cos, cos, jnp.ones((S, pad), F32)], axis=-1)
    sin_h = jnp.concatenate([-sin, sin, jnp.zeros((S, pad), F32)], axis=-1)
    reps = LANE // HEAD_DIM
    return jnp.tile(cos_h, (1, reps)), jnp.tile(sin_h, (1, reps))


def _pack_w_in(w):
    per_group = GROUP * N_BRANCH
    gates = w[:, GATE_OFF:GATE_OFF + N_KV_HEADS * per_group].reshape(-1, N_KV_HEADS, per_group)
    gates = jnp.pad(gates, ((0, 0), (0, LANE // GATE_ROWS - N_KV_HEADS), (0, GATE_ROWS - per_group)))
    gates = gates.reshape(-1, LANE)
    return jnp.concatenate([w[:, :GATE_OFF], gates, w[:, GATE_OFF + N_KV_HEADS * per_group:]], axis=1).astype(BF16)


def _mixer_heads(x, norm_g, w_in, pe_k, w1_k, b1_k, w2_k, pe_v, w1_v, b1_v, w2_v,
                 w_pool, pool_scale, cos_t, sin_t):
    B, S, D = x.shape
    qt, qrt, kcr, vcr, ks, vst, kw, vwt, gates, p = _mixer_proj(
        x, norm_g.reshape(1, D), _pack_w_in(w_in), cos_t, sin_t, w_pool.astype(BF16), pool_scale.reshape(1, -1))
    half = CMP_LEN // 2
    cw = half * HEAD_DIM
    pad_w2 = lambda w2: jnp.pad(w2, ((0, 0), (0, LANE - HEAD_DIM))).astype(BF16)
    kc, vct = _compress(
        kcr, vcr,
        pe_k.reshape(2, cw), w1_k.reshape(2, cw, CMP_HIDDEN).astype(BF16), b1_k.reshape(1, -1), pad_w2(w2_k),
        pe_v.reshape(2, cw), w1_v.reshape(2, cw, CMP_HIDDEN).astype(BF16), b1_v.reshape(1, -1), pad_w2(w2_v))
    return _nsa_attention(qt, qrt, kc, vct, ks, vst, kw, vwt, gates), p


def _moe_layer(x2, g_ffn, router, wg, wu, wd, g_final):
    T, D = x2.shape
    E = N_EXPERTS
    C = T // MOE_CHUNK
    i32 = jnp.int32
    w_router = jnp.pad(router, ((0, 0), (0, LANE - E)))
    h, route, gates, before, total = _router(x2, g_ffn.reshape(1, D), w_router)

    counts = total[0, :E].astype(i32)
    tiles = (counts + MOE_ROW_TILE - 1) // MOE_ROW_TILE
    ends = jnp.cumsum(tiles)
    off = (ends - tiles) * MOE_ROW_TILE
    e1, e2, r1, r2 = route[:, 0], route[:, 1], route[:, 2], route[:, 3]
    slot1 = off[e1] + r1
    slot2 = off[e2] + r2
    first = off[None, :] + before[:, 0, :E].astype(i32)
    after = jnp.concatenate([before[1:, 0, :E], total[:, :E]], axis=0).astype(i32)
    last = off[None, :] + after - 1
    lo_tile = first // MOE_CHUNK
    tile_meta = lo_tile << (2 * MOE_PARTS)
    part_rows = MOE_CHUNK // MOE_PARTS
    for k in range(2 * MOE_PARTS):
        start = lo_tile * MOE_CHUNK + k * part_rows
        touched = (last >= first) & (first < start + part_rows) & (last >= start)
        tile_meta = tile_meta + (touched.astype(i32) << k)
    n_row_tiles = (2 * T) // MOE_ROW_TILE + E + 1 + MOE_CHUNK // MOE_ROW_TILE
    n_used = ends[-1:]
    tile_ids = jnp.minimum(jnp.arange(n_row_tiles, dtype=i32), n_used[0] - 1)
    tile_expert = jnp.minimum(jnp.sum(ends[None, :] <= tile_ids[:, None], axis=1), E - 1).astype(i32)

    zero = jnp.zeros_like(slot1)
    tok = jnp.stack([slot1, slot2, zero, zero, zero, zero, zero, zero])
    tok_rows = tok.reshape(8, C, MOE_CHUNK).transpose(1, 0, 2)
    gz = jnp.zeros((T,), F32)
    gate_rows = jnp.stack([gates[:, 0], gates[:, 1], gz, gz, gz, gz, gz, gz]).reshape(8, C, MOE_CHUNK).transpose(1, 0, 2)
    slot1_rep = jnp.broadcast_to(slot1[:, None], (T, LANE))
    slot2_rep = jnp.broadcast_to(slot2[:, None], (T, LANE))

    xs_lo, xs_hi, gs_lo, gs_hi = _dispatch(tile_meta.T.reshape(-1).astype(i32), h, tok_rows, gate_rows,
                                           n_row_tiles * MOE_ROW_TILE)
    ys = _expert_ffn(tile_expert, n_used.astype(i32), xs_lo, xs_hi, gs_lo, gs_hi,
                     wg.astype(BF16), wu.astype(BF16), wd.astype(BF16))
    return _combine(tile_meta.reshape(-1).astype(i32), x2, ys, slot1_rep, slot2_rep, g_final.reshape(1, D))


def kernel(x, mem, norm_mix, w_in, cmp_pe_k, cmp_w1_k, cmp_b1_k, cmp_w2_k, cmp_pe_v, cmp_w1_v, cmp_b1_v, cmp_w2_v, w_pool, pool_scale, w_out, norm_x, norm_mem, wq_x, wk_x, wv_x, wo_x, norm_ffn, ffn_wg, ffn_wu, ffn_wd, moe_router, moe_wg, moe_wu, moe_wd, norm_final):
    B, S, D = x.shape
    depth = norm_mix.shape[0]
    assert depth == 2, "the final RMSNorm is fused into the expert layer, which must be the last one"
    cos_t, sin_t = _rope_tables(S)
    for layer in range(depth):
        a, p = _mixer_heads(x, norm_mix[layer], w_in[layer],
                            cmp_pe_k[layer], cmp_w1_k[layer], cmp_b1_k[layer], cmp_w2_k[layer],
                            cmp_pe_v[layer], cmp_w1_v[layer], cmp_b1_v[layer], cmp_w2_v[layer],
                            w_pool[layer], pool_scale[layer], cos_t, sin_t)
        wkv = jnp.concatenate([wk_x[layer], wv_x[layer]], axis=1).astype(BF16)
        kx, vx = _xattn_kv(mem, norm_mem[layer].reshape(1, D), wkv)
        wo_mix = w_out[layer].astype(BF16)
        x = _xattn(x, a, p, wo_mix[:NSA_WIDTH], wo_mix[NSA_WIDTH:], norm_x[layer].reshape(1, D),
                   wq_x[layer].astype(BF16), kx, vx, wo_x[layer].astype(BF16))
        x2 = x.reshape(B * S, D)
        j = layer // 2
        if layer % 2 == 0:
            x2 = _ffn(x2, norm_ffn[layer].reshape(1, D), ffn_wg[j].astype(BF16), ffn_wu[j].astype(BF16),
                      ffn_wd[j].astype(BF16))
        else:
            x2 = _moe_layer(x2, norm_ffn[layer], moe_router[j], moe_wg[j], moe_wu[j], moe_wd[j], norm_final)
        x = x2.reshape(B, S, D)
    return x
```

```python
import functools
import math

import numpy as np
import jax
import jax.numpy as jnp
from jax import lax
from jax.experimental import pallas as pl
from jax.experimental.pallas import tpu as pltpu

F32 = jnp.float32
BF16 = jnp.bfloat16

D_MODEL = 1024
HEAD_DIM = 64
N_HEADS = 8
N_KV_HEADS = 2
GROUP = N_HEADS // N_KV_HEADS
NSA_WIDTH = N_HEADS * HEAD_DIM
KV_WIDTH = N_KV_HEADS * HEAD_DIM
N_BRANCH = 3
GATE_ROWS = 16
POOL_WIDTH = 512
POOL_WINDOWS = (2, 4, 8, 16)
POOL_GROUP = 128
POOL_HALO = 16
ROPE_DIM = 16
ROPE_HALF = 8
ROPE_THETA = 500000.0
CMP_LEN = 32
CMP_STRIDE = 16
CMP_HIDDEN = 256
SEL_BLOCK = 64
SEL_TOP_N = 16
N_LOCAL_BLOCKS = 2
WINDOW = 512
X_HEADS = 4
X_HEAD_DIM = 256
N_EXPERTS = 8
EPS = 1e-6

LANE = 128
IN_PACKED = NSA_WIDTH + 6 * KV_WIDTH + LANE + POOL_WIDTH
GATE_OFF = NSA_WIDTH + 6 * KV_WIDTH
POOL_OFF = GATE_OFF + LANE

NEG = -1e30
TINY = float(np.finfo(np.float32).tiny)
VMEM_LIMIT = 56 * 1024 * 1024


def _dot(a, b):
    return jnp.dot(a, b, preferred_element_type=F32)


def _dot_nt(a, b):
    return lax.dot_general(a, b, (((1,), (1,)), ((), ())), preferred_element_type=F32)


def _rms(x, g):
    y = x * lax.rsqrt(jnp.mean(x * x, axis=-1, keepdims=True) + EPS)
    return y * g


def _params(sem, limit=VMEM_LIMIT):
    return pltpu.CompilerParams(dimension_semantics=sem, vmem_limit_bytes=limit)


def _mixer_proj_kernel(x_ref, g_ref, w_ref, cos_ref, sin_ref, wpool_ref, pscale_ref,
                       qt_ref, qrt_ref, kcr_ref, vcr_ref, ks_ref, vst_ref, kw_ref, vwt_ref,
                       gate_ref, p_ref, pool_buf):
    h = _rms(x_ref[0], g_ref[...]).astype(BF16)
    z = _dot(h, w_ref[...])
    tm = z.shape[0]
    cos = cos_ref[...]
    sin = sin_ref[...]
    lane = lax.broadcasted_iota(jnp.int32, (tm, LANE), 1)
    first = (lane & (HEAD_DIM - 1)) < ROPE_HALF
    scale = HEAD_DIM ** -0.5 * math.log2(math.e)

    def rope(xs):
        partner = jnp.where(first, pltpu.roll(xs, LANE - ROPE_HALF, 1), pltpu.roll(xs, ROPE_HALF, 1))
        return xs * cos + partner * sin

    for s in range(NSA_WIDTH // LANE):
        xs = z[:, s * LANE:(s + 1) * LANE]
        for src, ref in ((xs, qt_ref), (rope(xs), qrt_ref)):
            t = (src * scale).T.astype(BF16)
            ref[0, 2 * s] = t[:HEAD_DIM]
            ref[0, 2 * s + 1] = t[HEAD_DIM:]

    def kv_slab(i):
        return z[:, NSA_WIDTH + i * KV_WIDTH:NSA_WIDTH + (i + 1) * KV_WIDTH]

    for slab, ref in ((kv_slab(0), kcr_ref), (kv_slab(1), vcr_ref),
                      (rope(kv_slab(2)), ks_ref), (rope(kv_slab(4)), kw_ref)):
        for gg in range(N_KV_HEADS):
            ref[0, gg] = slab[:, gg * HEAD_DIM:(gg + 1) * HEAD_DIM].astype(ref.dtype)

    for slab, ref in ((kv_slab(3), vst_ref), (kv_slab(5), vwt_ref)):
        t = slab.T.astype(BF16)
        for gg in range(N_KV_HEADS):
            for k in range(tm // LANE):
                ref[0, gg, k] = t[gg * HEAD_DIM:(gg + 1) * HEAD_DIM, k * LANE:(k + 1) * LANE]

    sig_t = jax.nn.sigmoid(z[:, GATE_OFF:GATE_OFF + LANE]).T
    for gg in range(N_KV_HEADS):
        gate_ref[0, gg] = sig_t[gg * GATE_ROWS:(gg + 1) * GATE_ROWS]

    i = pl.program_id(1)

    @pl.when(i == 0)
    def _():
        pool_buf[0:POOL_HALO, :] = jnp.zeros((POOL_HALO, POOL_WIDTH), F32)

    @pl.when(i > 0)
    def _():
        pool_buf[0:POOL_HALO, :] = pool_buf[tm:tm + POOL_HALO, :]

    pool_buf[POOL_HALO:POOL_HALO + tm, :] = z[:, POOL_OFF:POOL_OFF + POOL_WIDTH]
    t1 = i * tm + lax.broadcasted_iota(jnp.int32, (tm, 1), 0) + 1
    for gi, w in enumerate(POOL_WINDOWS):
        cols = slice(gi * POOL_GROUP, (gi + 1) * POOL_GROUP)
        cur = pool_buf[POOL_HALO:POOL_HALO + tm, cols]
        tot = cur
        for k in range(1, w):
            tot = tot + pool_buf[POOL_HALO - k:POOL_HALO - k + tm, cols]
        d = tot / jnp.minimum(t1, w).astype(F32) - cur
        p_ref[0, :, cols] = (_dot(d.astype(BF16), wpool_ref[gi]) * pscale_ref[:, cols]).astype(p_ref.dtype)


def _mixer_proj(x, g, w_packed, cos_t, sin_t, w_pool, pool_scale, tm=512):
    B, S, D = x.shape
    G = N_KV_HEADS
    sd = jax.ShapeDtypeStruct
    out_shape = (sd((B, N_HEADS, HEAD_DIM, S), BF16), sd((B, N_HEADS, HEAD_DIM, S), BF16),
                 sd((B, G, S, HEAD_DIM), F32), sd((B, G, S, HEAD_DIM), F32),
                 sd((B, G, S, HEAD_DIM), BF16), sd((B, G, S // LANE, HEAD_DIM, LANE), BF16),
                 sd((B, G, S, HEAD_DIM), BF16), sd((B, G, S // LANE, HEAD_DIM, LANE), BF16),
                 sd((B, G, GATE_ROWS, S), F32),
                 sd((B, S, POOL_WIDTH), BF16))
    qspec = pl.BlockSpec((1, N_HEADS, HEAD_DIM, tm), lambda b, i: (b, 0, 0, i))
    kspec = pl.BlockSpec((1, G, tm, HEAD_DIM), lambda b, i: (b, 0, i, 0))
    vspec = pl.BlockSpec((1, G, tm // LANE, HEAD_DIM, LANE), lambda b, i: (b, 0, i, 0, 0))
    out_specs = (qspec, qspec, kspec, kspec, kspec, vspec, kspec, vspec,
                 pl.BlockSpec((1, G, GATE_ROWS, tm), lambda b, i: (b, 0, 0, i)),
                 pl.BlockSpec((1, tm, POOL_WIDTH), lambda b, i: (b, i, 0)))
    return pl.pallas_call(
        _mixer_proj_kernel,
        grid=(B, S // tm),
        in_specs=[pl.BlockSpec((1, tm, D), lambda b, i: (b, i, 0)),
                  pl.BlockSpec((1, D), lambda b, i: (0, 0)),
                  pl.BlockSpec((D, IN_PACKED), lambda b, i: (0, 0)),
                  pl.BlockSpec((tm, LANE), lambda b, i: (i, 0)),
                  pl.BlockSpec((tm, LANE), lambda b, i: (i, 0)),
                  pl.BlockSpec(w_pool.shape, lambda b, i: (0, 0, 0)),
                  pl.BlockSpec((1, POOL_WIDTH), lambda b, i: (0, 0))],
        out_specs=out_specs,
        out_shape=out_shape,
        scratch_shapes=[pltpu.VMEM((tm + POOL_HALO, POOL_WIDTH), F32)],
        compiler_params=_params(("parallel", "arbitrary")),
        name="mixer_proj",
    )(x, g, w_packed, cos_t, sin_t, w_pool, pool_scale)


def _gelu_tanh(x):
    return 0.5 * x * (1.0 + jnp.tanh(math.sqrt(2.0 / math.pi) * (x + 0.044715 * (x * x * x))))


def _compress_kernel(kc_ref, vc_ref, pek_ref, w1k_ref, b1k_ref, w2k_ref,
                     pev_ref, w1v_ref, b1v_ref, w2v_ref, ko_ref, vto_ref):
    def comp(x_ref, pe_ref, w1_ref, b1_ref, w2_ref):
        x = x_ref[0, 0]
        n = x.shape[0]
        a = _dot((x + pe_ref[0:1, :]).astype(BF16), w1_ref[0])
        b = _dot((x + pe_ref[1:2, :]).astype(BF16), w1_ref[1])
        pre = a + pltpu.roll(b, n - 1, 0) + b1_ref[...]
        return _dot(_gelu_tanh(pre).astype(BF16), w2_ref[...])

    ko_ref[0, 0] = comp(kc_ref, pek_ref, w1k_ref, b1k_ref, w2k_ref)[:, :HEAD_DIM].astype(ko_ref.dtype)
    vto_ref[0, 0] = comp(vc_ref, pev_ref, w1v_ref, b1v_ref, w2v_ref).T[:HEAD_DIM].astype(vto_ref.dtype)


def _compress(kcr, vcr, pek, w1k, b1k, w2k, pev, w1v, b1v, w2v):
    B, G, S, _ = kcr.shape
    n = S // CMP_STRIDE
    cw = CMP_STRIDE * HEAD_DIM
    kc = kcr.reshape(B, G, n, cw)
    vc = vcr.reshape(B, G, n, cw)
    xspec = pl.BlockSpec((1, 1, n, cw), lambda b, g: (b, g, 0, 0))
    full = lambda a: pl.BlockSpec(a.shape, lambda b, g: (0,) * a.ndim)
    return pl.pallas_call(
        _compress_kernel,
        grid=(B, G),
        in_specs=[xspec, xspec, full(pek), full(w1k), full(b1k), full(w2k),
                  full(pev), full(w1v), full(b1v), full(w2v)],
        out_specs=(pl.BlockSpec((1, 1, n, HEAD_DIM), lambda b, g: (b, g, 0, 0)),
                   pl.BlockSpec((1, 1, HEAD_DIM, n), lambda b, g: (b, g, 0, 0))),
        out_shape=(jax.ShapeDtypeStruct((B, G, n, HEAD_DIM), BF16),
                   jax.ShapeDtypeStruct((B, G, HEAD_DIM, n), BF16)),
        compiler_params=_params(("parallel", "parallel")),
        name="compress_kv",
    )(kc, vc, pek, w1k, b1k, w2k, pev, w1v, b1v, w2v)


def _nsa_kernel(qt_ref, qrt_ref, kc_ref, vct_ref, ks_ref, vst_ref, kw_ref, vwt_ref, gate_ref,
                covt_ref, o_ref, selbias_ref, *, tq, tk):
    qt = pl.program_id(2)
    q0 = qt * tq
    n_cmp = kc_ref.shape[2]
    n_blk = covt_ref.shape[0]
    heads = lambda ref: jnp.concatenate([ref[0, hh] for hh in range(GROUP)], axis=1)
    q_t = heads(qt_ref)
    qr_t = heads(qrt_ref)
    t_lane = q0 + lax.broadcasted_iota(jnp.int32, (1, tq), 1)
    per_head = lambda a: [a[:, hh * tq:(hh + 1) * tq] for hh in range(GROUP)]
    all_heads = lambda a: jnp.concatenate([a] * GROUP, axis=1)

    sc = _dot(kc_ref[0, 0], q_t)
    cmp_end = lax.broadcasted_iota(jnp.int32, (n_cmp, 1), 0) * CMP_STRIDE + (CMP_LEN - 1)
    valid = cmp_end <= t_lane
    pcs = []
    for s_h in per_head(sc):
        s_h = jnp.where(valid, s_h, NEG)
        e = jnp.where(valid, jnp.exp2(s_h - jnp.max(s_h, axis=0, keepdims=True)), 0.0)
        pcs.append(e * (1.0 / jnp.maximum(jnp.sum(e, axis=0, keepdims=True), TINY)))
    oc_t = _dot(vct_ref[0, 0], jnp.concatenate(pcs, axis=1).astype(BF16))

    psum = pcs[0] + pcs[1] + pcs[2] + pcs[3]
    p_hi = psum.astype(BF16)
    r1 = psum - p_hi.astype(F32)
    p_mid = r1.astype(BF16)
    p_lo = (r1 - p_mid.astype(F32)).astype(BF16)
    covt = covt_ref[...]
    imp_t = _dot(covt, p_hi) + _dot(covt, p_mid) + _dot(covt, p_lo)

    jb = lax.broadcasted_iota(jnp.int32, (n_blk, tq), 0)
    tb = (q0 + lax.broadcasted_iota(jnp.int32, (n_blk, tq), 1)) // SEL_BLOCK
    dist = tb - jb
    forced = (jb == 0) | ((dist >= 0) & (dist < N_LOCAL_BLOCKS))
    score = jnp.where(jb > tb, -jnp.inf, jnp.where(forced, jnp.inf, imp_t))
    rank = jnp.zeros((n_blk, tq), jnp.int32)
    for jp in range(n_blk):
        rowv = score[jp:jp + 1, :]
        beats = (rowv > score) | ((rowv == score) & (jb > jp))
        rank = rank + jnp.where(beats, 1, 0)
    selbias_ref[...] = jnp.where(rank < min(SEL_TOP_N, n_blk), 0.0, NEG)

    R = GROUP * tq

    def with_ones(v_t):
        extra = jnp.where(lax.broadcasted_iota(jnp.int32, (16, v_t.shape[1]), 0) == 0, 1.0, 0.0)
        return jnp.concatenate([v_t, extra.astype(BF16)], axis=0)

    bpc = tk // SEL_BLOCK
    vpc = tk // LANE

    def sel_chunk(c, carry, causal):
        m_i, l_i, acc = carry
        start = pl.multiple_of(c * tk, tk)
        k_c = ks_ref[0, 0, pl.ds(start, tk), :]
        v_t = jnp.concatenate([vst_ref[0, 0, c * vpc + k] for k in range(vpc)], axis=1)
        rows = [jnp.broadcast_to(selbias_ref[pl.ds(c * bpc + j, 1), :], (SEL_BLOCK, tq)) for j in range(bpc)]
        bias = jnp.concatenate(rows, axis=0)
        if causal:
            kpos = start + lax.broadcasted_iota(jnp.int32, (tk, 1), 0)
            bias = jnp.where(kpos <= t_lane, bias, NEG)
        s = _dot(k_c, qr_t) + all_heads(bias)
        m_new = jnp.maximum(m_i, jnp.max(s, axis=0, keepdims=True))
        alpha = jnp.exp2(m_i - m_new)
        pv = _dot(with_ones(v_t), jnp.exp2(s - m_new).astype(BF16))
        return m_new, alpha * l_i + pv[HEAD_DIM:HEAD_DIM + 1], alpha * acc + pv[:HEAD_DIM]

    init = (jnp.full((1, R), NEG, F32), jnp.zeros((1, R), F32), jnp.zeros((HEAD_DIM, R), F32))
    c_diag = q0 // tk
    carry = lax.fori_loop(0, c_diag, lambda c, cr: sel_chunk(c, cr, False), init)
    _, l_s, acc_s = sel_chunk(c_diag, carry, True)
    os_t = acc_s * (1.0 / jnp.maximum(l_s, TINY))

    wk = WINDOW + tq
    w0 = pl.multiple_of(jnp.maximum(q0 - WINDOW, 0), LANE)
    k_w = kw_ref[0, 0, pl.ds(w0, wk), :]
    v_t = jnp.concatenate([vwt_ref[0, 0, w0 // LANE + k] for k in range(wk // LANE)], axis=1)
    diff = t_lane - (w0 + lax.broadcasted_iota(jnp.int32, (wk, 1), 0))
    bias = jnp.where((diff >= 0) & (diff < WINDOW), 0.0, NEG)
    s = _dot(k_w, qr_t) + all_heads(bias)
    pv = _dot(with_ones(v_t), jnp.exp2(s - jnp.max(s, axis=0, keepdims=True)).astype(BF16))
    ow_t = pv[:HEAD_DIM] * (1.0 / pv[HEAD_DIM:HEAD_DIM + 1])

    gt = gate_ref[0, 0]
    outs = []
    for hh, (a, b, c) in enumerate(zip(per_head(oc_t), per_head(os_t), per_head(ow_t))):
        r = hh * N_BRANCH
        outs.append(gt[r:r + 1] * a + gt[r + 1:r + 2] * b + gt[r + 2:r + 3] * c)
    o_ref[0] = jnp.concatenate(outs, axis=0).T.astype(o_ref.dtype)


def _cover_table(S):
    n_cmp = S // CMP_STRIDE
    n_blk = S // SEL_BLOCK
    cs = np.arange(n_cmp) * CMP_STRIDE
    ss = np.arange(n_blk) * SEL_BLOCK
    cover_t = ((cs[None, :] < ss[:, None] + SEL_BLOCK) & (cs[None, :] + CMP_LEN > ss[:, None]))
    cover_t[:, n_cmp - 1] = False
    return jnp.asarray(cover_t, BF16)


def _nsa_attention(qt, qrt, kc, vct, ks, vst, kw, vwt, gates, tq=256, tk=1024):
    B, _, _, S = qt.shape
    tk = min(tk, S)
    assert tq % LANE == 0 and tk % tq == 0 and S % tk == 0 and S >= WINDOW + tq
    covt = _cover_table(S)
    n_cmp = kc.shape[2]
    n_blk = covt.shape[0]
    qspec = pl.BlockSpec((1, GROUP, HEAD_DIM, tq), lambda b, g, i: (b, g, 0, i))
    kspec = lambda a: pl.BlockSpec((1, 1) + a.shape[2:], lambda b, g, i: (b, g, 0, 0))
    vspec = pl.BlockSpec((1, 1, S // LANE, HEAD_DIM, LANE), lambda b, g, i: (b, g, 0, 0, 0))
    return pl.pallas_call(
        functools.partial(_nsa_kernel, tq=tq, tk=tk),
        grid=(B, N_KV_HEADS, S // tq),
        in_specs=[qspec, qspec,
                  pl.BlockSpec((1, 1, n_cmp, HEAD_DIM), lambda b, g, i: (b, g, 0, 0)),
                  pl.BlockSpec((1, 1, HEAD_DIM, n_cmp), lambda b, g, i: (b, g, 0, 0)),
                  kspec(ks), vspec, kspec(kw), vspec,
                  pl.BlockSpec((1, 1, GATE_ROWS, tq), lambda b, g, i: (b, g, 0, i)),
                  pl.BlockSpec(covt.shape, lambda b, g, i: (0, 0))],
        out_specs=pl.BlockSpec((1, tq, GROUP * HEAD_DIM), lambda b, g, i: (b, i, g)),
        out_shape=jax.ShapeDtypeStruct((B, S, NSA_WIDTH), BF16),
        scratch_shapes=[pltpu.VMEM((n_blk, tq), F32)],
        compiler_params=_params(("parallel", "parallel", "parallel")),
        name="nsa_attention",
    )(qt, qrt, kc, vct, ks, vst, kw, vwt, gates, covt)


def _xkv_kernel(m_ref, g_ref, w_ref, k_ref, v_ref):
    h = _rms(m_ref[0], g_ref[...]).astype(BF16)
    kv = _dot(h, w_ref[...])
    d = k_ref.shape[2]
    k_ref[0] = kv[:, :d].astype(k_ref.dtype)
    v_ref[0] = kv[:, d:].astype(v_ref.dtype)


def _xattn_kv(mem, g, wkv):
    B, M, D = mem.shape
    ospec = pl.BlockSpec((1, M, D), lambda b: (b, 0, 0))
    oshape = jax.ShapeDtypeStruct((B, M, D), BF16)
    return pl.pallas_call(
        _xkv_kernel,
        grid=(B,),
        in_specs=[pl.BlockSpec((1, M, D), lambda b: (b, 0, 0)),
                  pl.BlockSpec((1, D), lambda b: (0, 0)),
                  pl.BlockSpec(wkv.shape, lambda b: (0, 0))],
        out_specs=(ospec, ospec),
        out_shape=(oshape, oshape),
        compiler_params=_params(("parallel",)),
        name="xattn_kv",
    )(mem, g, wkv)


def _xattn_kernel(x_ref, a_ref, p_ref, wa_ref, wp_ref, g_ref, wq_ref, k_ref, v_ref, wo_ref, o_ref):
    x = x_ref[0] + _dot(a_ref[0], wa_ref[...]) + _dot(p_ref[0], wp_ref[...])
    h = _rms(x, g_ref[...]).astype(BF16)
    scale = X_HEAD_DIM ** -0.5
    q = (_dot(h, wq_ref[...]) * scale).astype(BF16)
    outs = []
    for hd in range(X_HEADS):
        sl = slice(hd * X_HEAD_DIM, (hd + 1) * X_HEAD_DIM)
        s = _dot_nt(q[:, sl], k_ref[0, :, sl])
        e = jnp.exp(s - jnp.max(s, axis=-1, keepdims=True))
        p = e / jnp.sum(e, axis=-1, keepdims=True)
        outs.append(_dot(p.astype(BF16), v_ref[0, :, sl]).astype(BF16))
    o = jnp.concatenate(outs, axis=-1)
    o_ref[0] = x + _dot(o, wo_ref[...])


def _xattn(x, a, p, wa, wp, g, wq, kx, vx, wo, tm=512):
    B, S, D = x.shape
    M = kx.shape[1]
    row = lambda n: pl.BlockSpec((1, tm, n), lambda b, i: (b, i, 0))
    full = lambda w: pl.BlockSpec(w.shape, lambda b, i: (0, 0))
    mspec = pl.BlockSpec((1, M, D), lambda b, i: (b, 0, 0))
    return pl.pallas_call(
        _xattn_kernel,
        grid=(B, S // tm),
        in_specs=[row(D), row(a.shape[2]), row(p.shape[2]), full(wa), full(wp),
                  full(g), full(wq), mspec, mspec, full(wo)],
        out_specs=row(D),
        out_shape=jax.ShapeDtypeStruct((B, S, D), F32),
        compiler_params=_params(("parallel", "parallel")),
        name="out_proj_xattn",
    )(x, a, p, wa, wp, g, wq, kx, vx, wo)


def _silu(x):
    return x * jax.nn.sigmoid(x)


def _ffn_kernel(x_ref, g_ref, wg_ref, wu_ref, wd_ref, o_ref):
    x = x_ref[...]
    h = _rms(x, g_ref[...]).astype(BF16)
    act = _silu(_dot(h, wg_ref[...])) * _dot(h, wu_ref[...])
    o_ref[...] = x + _dot(act.astype(BF16), wd_ref[...])


def _ffn(x2, g, wg, wu, wd, tm=512):
    T, D = x2.shape
    resident = lambda w: pl.BlockSpec(w.shape, lambda i: (0, 0), pipeline_mode=pl.Buffered(1))
    return pl.pallas_call(
        _ffn_kernel,
        grid=(T // tm,),
        in_specs=[pl.BlockSpec((tm, D), lambda i: (i, 0)),
                  pl.BlockSpec((1, D), lambda i: (0, 0)),
                  resident(wg), resident(wu), resident(wd)],
        out_specs=pl.BlockSpec((tm, D), lambda i: (i, 0)),
        out_shape=jax.ShapeDtypeStruct((T, D), F32),
        compiler_params=_params(("parallel",)),
        name="dense_swiglu",
    )(x2, g, wg, wu, wd)


MOE_CHUNK = 512
MOE_ROW_TILE = 512
assert MOE_ROW_TILE % MOE_CHUNK == 0


MOE_PARTS = 2


def _tile_meta(m):
    return m >> (2 * MOE_PARTS), [(m & (1 << k)) != 0 for k in range(2 * MOE_PARTS)]


def _router_kernel(x_ref, g_ref, wr_ref, tri_ref, h_ref, route_ref, gate_ref, before_ref, total_ref, run_ref):
    c = pl.program_id(0)

    @pl.when(c == 0)
    def _():
        run_ref[...] = jnp.zeros_like(run_ref)

    hf = _rms(x_ref[...], g_ref[...])
    h_ref[...] = hf.astype(BF16)
    logits = jnp.dot(hf, wr_ref[...], preferred_element_type=F32, precision=lax.Precision.HIGHEST)
    lane = lax.broadcasted_iota(jnp.int32, logits.shape, 1)
    logits = jnp.where(lane < N_EXPERTS, logits, -jnp.inf)
    v1 = jnp.max(logits, axis=-1, keepdims=True)
    i1 = jnp.min(jnp.where(logits == v1, lane, LANE), axis=-1, keepdims=True)
    rest = jnp.where(lane == i1, -jnp.inf, logits)
    v2 = jnp.max(rest, axis=-1, keepdims=True)
    i2 = jnp.min(jnp.where(rest == v2, lane, LANE), axis=-1, keepdims=True)
    e2 = jnp.exp(v2 - v1)
    den = 1.0 + e2
    gate_ref[...] = jnp.where(lane == 0, 1.0 / den, jnp.where(lane == 1, e2 / den, 0.0))

    onehot = jnp.where((lane == i1) | (lane == i2), 1.0, 0.0)
    run = run_ref[...]
    rank = run + _dot(tri_ref[...], onehot.astype(BF16))
    r1 = jnp.sum(jnp.where(lane == i1, rank, 0.0), axis=-1, keepdims=True).astype(jnp.int32)
    r2 = jnp.sum(jnp.where(lane == i2, rank, 0.0), axis=-1, keepdims=True).astype(jnp.int32)
    route_ref[...] = jnp.where(lane == 0, i1, jnp.where(lane == 1, i2,
                               jnp.where(lane == 2, r1, jnp.where(lane == 3, r2, 0))))
    before_ref[0] = run
    run = run + jnp.sum(onehot, axis=0, keepdims=True)
    run_ref[...] = run
    total_ref[...] = run


def _router(x2, g, w_router):
    T, D = x2.shape
    C = T // MOE_CHUNK
    tri = jnp.asarray(np.tril(np.ones((MOE_CHUNK, MOE_CHUNK), np.float32), -1), BF16)
    row = lambda n: pl.BlockSpec((MOE_CHUNK, n), lambda c: (c, 0))
    full = lambda a: pl.BlockSpec(a.shape, lambda c: (0, 0))
    return pl.pallas_call(
        _router_kernel,
        grid=(C,),
        in_specs=[row(D), full(g), full(w_router), full(tri)],
        out_specs=(row(D), row(LANE), row(LANE),
                   pl.BlockSpec((1, 1, LANE), lambda c: (c, 0, 0)),
                   pl.BlockSpec((1, LANE), lambda c: (0, 0))),
        out_shape=(jax.ShapeDtypeStruct((T, D), BF16),
                   jax.ShapeDtypeStruct((T, LANE), jnp.int32),
                   jax.ShapeDtypeStruct((T, LANE), F32),
                   jax.ShapeDtypeStruct((C, 1, LANE), F32),
                   jax.ShapeDtypeStruct((1, LANE), F32)),
        scratch_shapes=[pltpu.VMEM((1, LANE), F32)],
        compiler_params=_params(("arbitrary",)),
        name="moe_router",
    )(x2, g, w_router, tri)


def _dispatch_kernel(a_ref, h_ref, tok_ref, gate_ref, zlo_ref, zhi_ref, zglo_ref, zghi_ref,
                     lo_ref, hi_ref, glo_ref, ghi_ref):
    del zlo_ref, zhi_ref, zglo_ref, zghi_ref
    e = pl.program_id(0)
    c = pl.program_id(1)
    idx = e * pl.num_programs(1) + c
    a, touched = _tile_meta(a_ref[idx])
    first = (c == 0) | (a != _tile_meta(a_ref[jnp.maximum(idx - 1, 0)])[0])

    @pl.when(first)
    def _():
        lo_ref[...] = jnp.zeros_like(lo_ref)
        hi_ref[...] = jnp.zeros_like(hi_ref)
        glo_ref[...] = jnp.zeros_like(glo_ref)
        ghi_ref[...] = jnp.zeros_like(ghi_ref)

    rows_per_part = MOE_CHUNK // MOE_PARTS

    def scatter(tile, part, x_ref, g_ref):
        tok = tok_ref[0]
        gts = gate_ref[0]
        r0 = part * rows_per_part
        rows = tile * MOE_CHUNK + r0 + lax.broadcasted_iota(jnp.int32, (rows_per_part, MOE_CHUNK), 0)
        c1 = tok[0:1] == rows
        c2 = tok[1:2] == rows
        x_ref[r0:r0 + rows_per_part, :] += _dot(jnp.where(c1 | c2, 1.0, 0.0).astype(BF16),
                                                h_ref[...]).astype(x_ref.dtype)
        g_ref[r0:r0 + rows_per_part, :] += jnp.sum(jnp.where(c1, gts[0:1], 0.0) + jnp.where(c2, gts[1:2], 0.0),
                                                   axis=1, keepdims=True)

    for k, (x_ref, g_ref) in enumerate(((lo_ref, glo_ref), (hi_ref, ghi_ref))):
        for part in range(MOE_PARTS):
            pl.when(touched[k * MOE_PARTS + part])(functools.partial(scatter, a + k, part, x_ref, g_ref))


def _dispatch(a_ec, h, tok_rows, gate_rows, n_slots):
    T, D = h.shape
    C = T // MOE_CHUNK
    zx = jnp.zeros((n_slots, D), BF16)
    zg = jnp.zeros((n_slots, 1), F32)
    any_spec = pl.BlockSpec(memory_space=pl.ANY)
    lo = lambda n: pl.BlockSpec((MOE_CHUNK, n), lambda e, c, a: (a[e * C + c] >> (2 * MOE_PARTS), 0))
    hi = lambda n: pl.BlockSpec((MOE_CHUNK, n), lambda e, c, a: ((a[e * C + c] >> (2 * MOE_PARTS)) + 1, 0))
    return pl.pallas_call(
        _dispatch_kernel,
        grid_spec=pltpu.PrefetchScalarGridSpec(
            num_scalar_prefetch=1,
            grid=(N_EXPERTS, C),
            in_specs=[pl.BlockSpec((MOE_CHUNK, D), lambda e, c, a: (c, 0)),
                      pl.BlockSpec((1, 8, MOE_CHUNK), lambda e, c, a: (c, 0, 0)),
                      pl.BlockSpec((1, 8, MOE_CHUNK), lambda e, c, a: (c, 0, 0)),
                      any_spec, any_spec, any_spec, any_spec],
            out_specs=(lo(D), hi(D), lo(1), hi(1))),
        out_shape=(jax.ShapeDtypeStruct(zx.shape, BF16), jax.ShapeDtypeStruct(zx.shape, BF16),
                   jax.ShapeDtypeStruct(zg.shape, F32), jax.ShapeDtypeStruct(zg.shape, F32)),
        input_output_aliases={4: 0, 5: 1, 6: 2, 7: 3},
        compiler_params=_params(("arbitrary", "arbitrary")),
        name="moe_dispatch",
    )(a_ec, h, tok_rows, gate_rows, zx, zx, zg, zg)


def _expert_ffn_kernel(te_ref, nu_ref, lo_ref, hi_ref, glo_ref, ghi_ref, wg_ref, wu_ref, wd_ref, o_ref):
    del te_ref
    used = pl.program_id(0) < nu_ref[0]

    @pl.when(used)
    def _():
        x = lo_ref[...] + hi_ref[...]
        act = _silu(_dot(x, wg_ref[0])) * _dot(x, wu_ref[0])
        o_ref[...] = (_dot(act.astype(BF16), wd_ref[0]) * (glo_ref[...] + ghi_ref[...])).astype(o_ref.dtype)

    @pl.when(jnp.logical_not(used))
    def _():
        o_ref[...] = jnp.zeros_like(o_ref)


def _expert_ffn(tile_expert, n_used, xs_lo, xs_hi, gs_lo, gs_hi, wg, wu, wd):
    N, D = xs_lo.shape
    tr = MOE_ROW_TILE
    row = lambda n: pl.BlockSpec((tr, n), lambda i, te, nu: (i, 0))
    expert = lambda w: pl.BlockSpec((1,) + w.shape[1:], lambda i, te, nu: (te[i], 0, 0),
                                    pipeline_mode=pl.Buffered(1))
    return pl.pallas_call(
        _expert_ffn_kernel,
        grid_spec=pltpu.PrefetchScalarGridSpec(
            num_scalar_prefetch=2,
            grid=(N // tr,),
            in_specs=[row(D), row(D), row(1), row(1), expert(wg), expert(wu), expert(wd)],
            out_specs=row(D)),
        out_shape=jax.ShapeDtypeStruct((N, D), BF16),
        compiler_params=_params(("arbitrary",)),
        name="moe_expert_ffn",
    )(tile_expert, n_used, xs_lo, xs_hi, gs_lo, gs_hi, wg, wu, wd)


def _combine_kernel(a_ref, x_ref, ylo_ref, yhi_ref, s1_ref, s2_ref, gf_ref, o_ref, acc_ref):
    c = pl.program_id(0)
    e = pl.program_id(1)

    @pl.when(e == 0)
    def _():
        acc_ref[...] = jnp.zeros_like(acc_ref)

    a, touched = _tile_meta(a_ref[c * pl.num_programs(1) + e])
    cols_per_part = MOE_CHUNK // MOE_PARTS

    def gather(tile, part, y_ref):
        s1 = s1_ref[...]
        s2 = s2_ref[...]
        c0 = part * cols_per_part
        picks = []
        for j in range(cols_per_part // LANE):
            cols = tile * MOE_CHUNK + c0 + j * LANE + lax.broadcasted_iota(jnp.int32, (MOE_CHUNK, LANE), 1)
            picks.append(jnp.where((s1 == cols) | (s2 == cols), 1.0, 0.0).astype(BF16))
        acc_ref[...] += _dot(jnp.concatenate(picks, axis=1), y_ref[c0:c0 + cols_per_part, :])

    for k, y_ref in enumerate((ylo_ref, yhi_ref)):
        for part in range(MOE_PARTS):
            pl.when(touched[k * MOE_PARTS + part])(functools.partial(gather, a + k, part, y_ref))

    @pl.when(e == pl.num_programs(1) - 1)
    def _():
        o_ref[...] = _rms(x_ref[...] + acc_ref[...], gf_ref[...])


def _combine(a_ce, x2, ys, slot1_rep, slot2_rep, g_final):
    T, D = x2.shape
    C = T // MOE_CHUNK
    E = N_EXPERTS
    return pl.pallas_call(
        _combine_kernel,
        grid_spec=pltpu.PrefetchScalarGridSpec(
            num_scalar_prefetch=1,
            grid=(C, E),
            in_specs=[pl.BlockSpec((MOE_CHUNK, D), lambda c, e, a: (c, 0)),
                      pl.BlockSpec((MOE_CHUNK, D), lambda c, e, a: (a[c * E + e] >> (2 * MOE_PARTS), 0)),
                      pl.BlockSpec((MOE_CHUNK, D), lambda c, e, a: ((a[c * E + e] >> (2 * MOE_PARTS)) + 1, 0)),
                      pl.BlockSpec((MOE_CHUNK, LANE), lambda c, e, a: (c, 0)),
                      pl.BlockSpec((MOE_CHUNK, LANE), lambda c, e, a: (c, 0)),
                      pl.BlockSpec((1, D), lambda c, e, a: (0, 0))],
            out_specs=pl.BlockSpec((MOE_CHUNK, D), lambda c, e, a: (c, 0)),
            scratch_shapes=[pltpu.VMEM((MOE_CHUNK, D), F32)]),
        out_shape=jax.ShapeDtypeStruct((T, D), F32),
        compiler_params=_params(("arbitrary", "arbitrary")),
        name="moe_combine_final_norm",
    )(a_ce, x2, ys, ys, slot1_rep, slot2_rep, g_final)


def _rope_tables(S):
    inv = ROPE_THETA ** (-jnp.arange(ROPE_HALF, dtype=F32) / ROPE_HALF)
    ang = jnp.arange(S, dtype=F32)[:, None] * inv[None, :]
    cos, sin = jnp.cos(ang), jnp.sin(ang)
    pad = HEAD_DIM - ROPE_DIM
    cos_h = jnp.concatenate([cos, cos, jnp.ones((S, pad), F32)], axis=-1)
    sin_h = jnp.concatenate([-sin, sin, jnp.zeros((S, pad), F32)], axis=-1)
    reps = LANE // HEAD_DIM
    return jnp.tile(cos_h, (1, reps)), jnp.tile(sin_h, (1, reps))


def _pack_w_in(w):
    per_group = GROUP * N_BRANCH
    gates = w[:, GATE_OFF:GATE_OFF + N_KV_HEADS * per_group].reshape(-1, N_KV_HEADS, per_group)
    gates = jnp.pad(gates, ((0, 0), (0, LANE // GATE_ROWS - N_KV_HEADS), (0, GATE_ROWS - per_group)))
    gates = gates.reshape(-1, LANE)
    return jnp.concatenate([w[:, :GATE_OFF], gates, w[:, GATE_OFF + N_KV_HEADS * per_group:]], axis=1).astype(BF16)


def _mixer_heads(x, norm_g, w_in, pe_k, w1_k, b1_k, w2_k, pe_v, w1_v, b1_v, w2_v,
                 w_pool, pool_scale, cos_t, sin_t):
    B, S, D = x.shape
    qt, qrt, kcr, vcr, ks, vst, kw, vwt, gates, p = _mixer_proj(
        x, norm_g.reshape(1, D), _pack_w_in(w_in), cos_t, sin_t, w_pool.astype(BF16), pool_scale.reshape(1, -1))
    half = CMP_LEN // 2
    cw = half * HEAD_DIM
    pad_w2 = lambda w2: jnp.pad(w2, ((0, 0), (0, LANE - HEAD_DIM))).astype(BF16)
    kc, vct = _compress(
        kcr, vcr,
        pe_k.reshape(2, cw), w1_k.reshape(2, cw, CMP_HIDDEN).astype(BF16), b1_k.reshape(1, -1), pad_w2(w2_k),
        pe_v.reshape(2, cw), w1_v.reshape(2, cw, CMP_HIDDEN).astype(BF16), b1_v.reshape(1, -1), pad_w2(w2_v))
    return _nsa_attention(qt, qrt, kc, vct, ks, vst, kw, vwt, gates), p


def _moe_layer(x2, g_ffn, router, wg, wu, wd, g_final):
    T, D = x2.shape
    E = N_EXPERTS
    C = T // MOE_CHUNK
    i32 = jnp.int32
    w_router = jnp.pad(router, ((0, 0), (0, LANE - E)))
    h, route, gates, before, total = _router(x2, g_ffn.reshape(1, D), w_router)

    counts = total[0, :E].astype(i32)
    tiles = (counts + MOE_ROW_TILE - 1) // MOE_ROW_TILE
    ends = jnp.cumsum(tiles)
    off = (ends - tiles) * MOE_ROW_TILE
    e1, e2, r1, r2 = route[:, 0], route[:, 1], route[:, 2], route[:, 3]
    slot1 = off[e1] + r1
    slot2 = off[e2] + r2
    first = off[None, :] + before[:, 0, :E].astype(i32)
    after = jnp.concatenate([before[1:, 0, :E], total[:, :E]], axis=0).astype(i32)
    last = off[None, :] + after - 1
    lo_tile = first // MOE_CHUNK
    tile_meta = lo_tile << (2 * MOE_PARTS)
    part_rows = MOE_CHUNK // MOE_PARTS
    for k in range(2 * MOE_PARTS):
        start = lo_tile * MOE_CHUNK + k * part_rows
        touched = (last >= first) & (first < start + part_rows) & (last >= start)
        tile_meta = tile_meta + (touched.astype(i32) << k)
    n_row_tiles = (2 * T) // MOE_ROW_TILE + E + 1 + MOE_CHUNK // MOE_ROW_TILE
    n_used = ends[-1:]
    tile_ids = jnp.minimum(jnp.arange(n_row_tiles, dtype=i32), n_used[0] - 1)
    tile_expert = jnp.minimum(jnp.sum(ends[None, :] <= tile_ids[:, None], axis=1), E - 1).astype(i32)

    zero = jnp.zeros_like(slot1)
    tok = jnp.stack([slot1, slot2, zero, zero, zero, zero, zero, zero])
    tok_rows = tok.reshape(8, C, MOE_CHUNK).transpose(1, 0, 2)
    gz = jnp.zeros((T,), F32)
    gate_rows = jnp.stack([gates[:, 0], gates[:, 1], gz, gz, gz, gz, gz, gz]).reshape(8, C, MOE_CHUNK).transpose(1, 0, 2)
    slot1_rep = jnp.broadcast_to(slot1[:, None], (T, LANE))
    slot2_rep = jnp.broadcast_to(slot2[:, None], (T, LANE))

    xs_lo, xs_hi, gs_lo, gs_hi = _dispatch(tile_meta.T.reshape(-1).astype(i32), h, tok_rows, gate_rows,
                                           n_row_tiles * MOE_ROW_TILE)
    ys = _expert_ffn(tile_expert, n_used.astype(i32), xs_lo, xs_hi, gs_lo, gs_hi,
                     wg.astype(BF16), wu.astype(BF16), wd.astype(BF16))
    return _combine(tile_meta.reshape(-1).astype(i32), x2, ys, slot1_rep, slot2_rep, g_final.reshape(1, D))


def kernel(x, mem, norm_mix, w_in, cmp_pe_k, cmp_w1_k, cmp_b1_k, cmp_w2_k, cmp_pe_v, cmp_w1_v, cmp_b1_v, cmp_w2_v, w_pool, pool_scale, w_out, norm_x, norm_mem, wq_x, wk_x, wv_x, wo_x, norm_ffn, ffn_wg, ffn_wu, ffn_wd, moe_router, moe_wg, moe_wu, moe_wd, norm_final):
    B, S, D = x.shape
    depth = norm_mix.shape[0]
    assert depth == 2, "the final RMSNorm is fused into the expert layer, which must be the last one"
    cos_t, sin_t = _rope_tables(S)
    for layer in range(depth):
        a, p = _mixer_heads(x, norm_mix[layer], w_in[layer],
                            cmp_pe_k[layer], cmp_w1_k[layer], cmp_b1_k[layer], cmp_w2_k[layer],
                            cmp_pe_v[layer], cmp_w1_v[layer], cmp_b1_v[layer], cmp_w2_v[layer],
                            w_pool[layer], pool_scale[layer], cos_t, sin_t)
        wkv = jnp.concatenate([wk_x[layer], wv_x[layer]], axis=1).astype(BF16)
        kx, vx = _xattn_kv(mem, norm_mem[layer].reshape(1, D), wkv)
        wo_mix = w_out[layer].astype(BF16)
        x = _xattn(x, a, p, wo_mix[:NSA_WIDTH], wo_mix[NSA_WIDTH:], norm_x[layer].reshape(1, D),
                   wq_x[layer].astype(BF16), kx, vx, wo_x[layer].astype(BF16))
        x2 = x.reshape(B * S, D)
        j = layer // 2
        if layer % 2 == 0:
            x2 = _ffn(x2, norm_ffn[layer].reshape(1, D), ffn_wg[j].astype(BF16), ffn_wu[j].astype(BF16),
                      ffn_wd[j].astype(BF16))
        else:
            x2 = _moe_layer(x2, norm_ffn[layer], moe_router[j], moe_wg[j], moe_wu[j], moe_wd[j], norm_final)
        x = x2.reshape(B, S, D)
    return x
```

```python
import functools
import math

import numpy as np
import jax
import jax.numpy as jnp
from jax import lax
from jax.experimental import pallas as pl
from jax.experimental.pallas import tpu as pltpu

F32 = jnp.float32
BF16 = jnp.bfloat16

D_MODEL = 1024
HEAD_DIM = 64
N_HEADS = 8
N_KV_HEADS = 2
GROUP = N_HEADS // N_KV_HEADS
NSA_WIDTH = N_HEADS * HEAD_DIM
KV_WIDTH = N_KV_HEADS * HEAD_DIM
N_BRANCH = 3
GATE_ROWS = 16
POOL_WIDTH = 512
POOL_WINDOWS = (2, 4, 8, 16)
POOL_GROUP = 128
POOL_HALO = 16
ROPE_DIM = 16
ROPE_HALF = 8
ROPE_THETA = 500000.0
CMP_LEN = 32
CMP_STRIDE = 16
CMP_HIDDEN = 256
SEL_BLOCK = 64
SEL_TOP_N = 16
N_LOCAL_BLOCKS = 2
WINDOW = 512
X_HEADS = 4
X_HEAD_DIM = 256
N_EXPERTS = 8
EPS = 1e-6

LANE = 128
IN_PACKED = NSA_WIDTH + 6 * KV_WIDTH + LANE + POOL_WIDTH
GATE_OFF = NSA_WIDTH + 6 * KV_WIDTH
POOL_OFF = GATE_OFF + LANE

NEG = -1e30
TINY = float(np.finfo(np.float32).tiny)
VMEM_LIMIT = 56 * 1024 * 1024


def _dot(a, b):
    return jnp.dot(a, b, preferred_element_type=F32)


def _dot_nt(a, b):
    return lax.dot_general(a, b, (((1,), (1,)), ((), ())), preferred_element_type=F32)


def _rms(x, g):
    y = x * lax.rsqrt(jnp.mean(x * x, axis=-1, keepdims=True) + EPS)
    return y * g


def _params(sem, limit=VMEM_LIMIT):
    return pltpu.CompilerParams(dimension_semantics=sem, vmem_limit_bytes=limit)


def _mixer_proj_kernel(x_ref, g_ref, w_ref, cos_ref, sin_ref, wpool_ref, pscale_ref,
                       qt_ref, qrt_ref, kcr_ref, vcr_ref, ks_ref, vst_ref, kw_ref, vwt_ref,
                       gate_ref, p_ref, pool_buf):
    h = _rms(x_ref[0], g_ref[...]).astype(BF16)
    z = _dot(h, w_ref[...])
    tm = z.shape[0]
    cos = cos_ref[...]
    sin = sin_ref[...]
    lane = lax.broadcasted_iota(jnp.int32, (tm, LANE), 1)
    first = (lane & (HEAD_DIM - 1)) < ROPE_HALF
    scale = HEAD_DIM ** -0.5 * math.log2(math.e)

    def rope(xs):
        partner = jnp.where(first, pltpu.roll(xs, LANE - ROPE_HALF, 1), pltpu.roll(xs, ROPE_HALF, 1))
        return xs * cos + partner * sin

    for s in range(NSA_WIDTH // LANE):
        xs = z[:, s * LANE:(s + 1) * LANE]
        for src, ref in ((xs, qt_ref), (rope(xs), qrt_ref)):
            t = (src * scale).T.astype(BF16)
            ref[0, 2 * s] = t[:HEAD_DIM]
            ref[0, 2 * s + 1] = t[HEAD_DIM:]

    def kv_slab(i):
        return z[:, NSA_WIDTH + i * KV_WIDTH:NSA_WIDTH + (i + 1) * KV_WIDTH]

    for slab, ref in ((kv_slab(0), kcr_ref), (kv_slab(1), vcr_ref),
                      (rope(kv_slab(2)), ks_ref), (rope(kv_slab(4)), kw_ref)):
        for gg in range(N_KV_HEADS):
            ref[0, gg] = slab[:, gg * HEAD_DIM:(gg + 1) * HEAD_DIM].astype(ref.dtype)

    for slab, ref in ((kv_slab(3), vst_ref), (kv_slab(5), vwt_ref)):
        t = slab.T.astype(BF16)
        for gg in range(N_KV_HEADS):
            for k in range(tm // LANE):
                ref[0, gg, k] = t[gg * HEAD_DIM:(gg + 1) * HEAD_DIM, k * LANE:(k + 1) * LANE]

    sig_t = jax.nn.sigmoid(z[:, GATE_OFF:GATE_OFF + LANE]).T
    for gg in range(N_KV_HEADS):
        gate_ref[0, gg] = sig_t[gg * GATE_ROWS:(gg + 1) * GATE_ROWS]

    i = pl.program_id(1)

    @pl.when(i == 0)
    def _():
        pool_buf[0:POOL_HALO, :] = jnp.zeros((POOL_HALO, POOL_WIDTH), F32)

    @pl.when(i > 0)
    def _():
        pool_buf[0:POOL_HALO, :] = pool_buf[tm:tm + POOL_HALO, :]

    pool_buf[POOL_HALO:POOL_HALO + tm, :] = z[:, POOL_OFF:POOL_OFF + POOL_WIDTH]
    t1 = i * tm + lax.broadcasted_iota(jnp.int32, (tm, 1), 0) + 1
    for gi, w in enumerate(POOL_WINDOWS):
        cols = slice(gi * POOL_GROUP, (gi + 1) * POOL_GROUP)
        cur = pool_buf[POOL_HALO:POOL_HALO + tm, cols]
        tot = cur
        for k in range(1, w):
            tot = tot + pool_buf[POOL_HALO - k:POOL_HALO - k + tm, cols]
        d = tot / jnp.minimum(t1, w).astype(F32) - cur
        p_ref[0, :, cols] = (_dot(d.astype(BF16), wpool_ref[gi]) * pscale_ref[:, cols]).astype(p_ref.dtype)


def _mixer_proj(x, g, w_packed, cos_t, sin_t, w_pool, pool_scale, tm=1024):
    B, S, D = x.shape
    G = N_KV_HEADS
    sd = jax.ShapeDtypeStruct
    out_shape = (sd((B, N_HEADS, HEAD_DIM, S), BF16), sd((B, N_HEADS, HEAD_DIM, S), BF16),
                 sd((B, G, S, HEAD_DIM), F32), sd((B, G, S, HEAD_DIM), F32),
                 sd((B, G, S, HEAD_DIM), BF16), sd((B, G, S // LANE, HEAD_DIM, LANE), BF16),
                 sd((B, G, S, HEAD_DIM), BF16), sd((B, G, S // LANE, HEAD_DIM, LANE), BF16),
                 sd((B, G, GATE_ROWS, S), F32),
                 sd((B, S, POOL_WIDTH), BF16))
    qspec = pl.BlockSpec((1, N_HEADS, HEAD_DIM, tm), lambda b, i: (b, 0, 0, i))
    kspec = pl.BlockSpec((1, G, tm, HEAD_DIM), lambda b, i: (b, 0, i, 0))
    vspec = pl.BlockSpec((1, G, tm // LANE, HEAD_DIM, LANE), lambda b, i: (b, 0, i, 0, 0))
    out_specs = (qspec, qspec, kspec, kspec, kspec, vspec, kspec, vspec,
                 pl.BlockSpec((1, G, GATE_ROWS, tm), lambda b, i: (b, 0, 0, i)),
                 pl.BlockSpec((1, tm, POOL_WIDTH), lambda b, i: (b, i, 0)))
    return pl.pallas_call(
        _mixer_proj_kernel,
        grid=(B, S // tm),
        in_specs=[pl.BlockSpec((1, tm, D), lambda b, i: (b, i, 0)),
                  pl.BlockSpec((1, D), lambda b, i: (0, 0)),
                  pl.BlockSpec((D, IN_PACKED), lambda b, i: (0, 0)),
                  pl.BlockSpec((tm, LANE), lambda b, i: (i, 0)),
                  pl.BlockSpec((tm, LANE), lambda b, i: (i, 0)),
                  pl.BlockSpec(w_pool.shape, lambda b, i: (0, 0, 0)),
                  pl.BlockSpec((1, POOL_WIDTH), lambda b, i: (0, 0))],
        out_specs=out_specs,
        out_shape=out_shape,
        scratch_shapes=[pltpu.VMEM((tm + POOL_HALO, POOL_WIDTH), F32)],
        compiler_params=_params(("parallel", "arbitrary")),
        name="mixer_proj",
    )(x, g, w_packed, cos_t, sin_t, w_pool, pool_scale)


def _gelu_tanh(x):
    return 0.5 * x * (1.0 + jnp.tanh(math.sqrt(2.0 / math.pi) * (x + 0.044715 * (x * x * x))))


def _compress_kernel(kc_ref, vc_ref, pek_ref, w1k_ref, b1k_ref, w2k_ref,
                     pev_ref, w1v_ref, b1v_ref, w2v_ref, ko_ref, vto_ref):
    def comp(x_ref, pe_ref, w1_ref, b1_ref, w2_ref):
        x = x_ref[0, 0]
        n = x.shape[0]
        a = _dot((x + pe_ref[0:1, :]).astype(BF16), w1_ref[0])
        b = _dot((x + pe_ref[1:2, :]).astype(BF16), w1_ref[1])
        pre = a + pltpu.roll(b, n - 1, 0) + b1_ref[...]
        return _dot(_gelu_tanh(pre).astype(BF16), w2_ref[...])

    ko_ref[0, 0] = comp(kc_ref, pek_ref, w1k_ref, b1k_ref, w2k_ref)[:, :HEAD_DIM].astype(ko_ref.dtype)
    vto_ref[0, 0] = comp(vc_ref, pev_ref, w1v_ref, b1v_ref, w2v_ref).T[:HEAD_DIM].astype(vto_ref.dtype)


def _compress(kcr, vcr, pek, w1k, b1k, w2k, pev, w1v, b1v, w2v):
    B, G, S, _ = kcr.shape
    n = S // CMP_STRIDE
    cw = CMP_STRIDE * HEAD_DIM
    kc = kcr.reshape(B, G, n, cw)
    vc = vcr.reshape(B, G, n, cw)
    xspec = pl.BlockSpec((1, 1, n, cw), lambda b, g: (b, g, 0, 0))
    full = lambda a: pl.BlockSpec(a.shape, lambda b, g: (0,) * a.ndim)
    return pl.pallas_call(
        _compress_kernel,
        grid=(B, G),
        in_specs=[xspec, xspec, full(pek), full(w1k), full(b1k), full(w2k),
                  full(pev), full(w1v), full(b1v), full(w2v)],
        out_specs=(pl.BlockSpec((1, 1, n, HEAD_DIM), lambda b, g: (b, g, 0, 0)),
                   pl.BlockSpec((1, 1, HEAD_DIM, n), lambda b, g: (b, g, 0, 0))),
        out_shape=(jax.ShapeDtypeStruct((B, G, n, HEAD_DIM), BF16),
                   jax.ShapeDtypeStruct((B, G, HEAD_DIM, n), BF16)),
        compiler_params=_params(("parallel", "parallel")),
        name="compress_kv",
    )(kc, vc, pek, w1k, b1k, w2k, pev, w1v, b1v, w2v)


def _nsa_kernel(qt_ref, qrt_ref, kc_ref, vct_ref, ks_ref, vst_ref, kw_ref, vwt_ref, gate_ref,
                covt_ref, o_ref, selbias_ref, *, tq, tk):
    qt = pl.program_id(2)
    q0 = qt * tq
    n_cmp = kc_ref.shape[2]
    n_blk = covt_ref.shape[0]
    heads = lambda ref: jnp.concatenate([ref[0, hh] for hh in range(GROUP)], axis=1)
    q_t = heads(qt_ref)
    qr_t = heads(qrt_ref)
    t_lane = q0 + lax.broadcasted_iota(jnp.int32, (1, tq), 1)
    per_head = lambda a: [a[:, hh * tq:(hh + 1) * tq] for hh in range(GROUP)]
    all_heads = lambda a: jnp.concatenate([a] * GROUP, axis=1)

    sc = _dot(kc_ref[0, 0], q_t)
    cmp_end = lax.broadcasted_iota(jnp.int32, (n_cmp, 1), 0) * CMP_STRIDE + (CMP_LEN - 1)
    valid = cmp_end <= t_lane
    pcs = []
    for s_h in per_head(sc):
        s_h = jnp.where(valid, s_h, NEG)
        e = jnp.where(valid, jnp.exp2(s_h - jnp.max(s_h, axis=0, keepdims=True)), 0.0)
        pcs.append(e * (1.0 / jnp.maximum(jnp.sum(e, axis=0, keepdims=True), TINY)))
    oc_t = _dot(vct_ref[0, 0], jnp.concatenate(pcs, axis=1).astype(BF16))

    psum = pcs[0] + pcs[1] + pcs[2] + pcs[3]
    p_hi = psum.astype(BF16)
    r1 = psum - p_hi.astype(F32)
    p_mid = r1.astype(BF16)
    p_lo = (r1 - p_mid.astype(F32)).astype(BF16)
    covt = covt_ref[...]
    imp_t = _dot(covt, p_hi) + _dot(covt, p_mid) + _dot(covt, p_lo)

    jb = lax.broadcasted_iota(jnp.int32, (n_blk, tq), 0)
    tb = (q0 + lax.broadcasted_iota(jnp.int32, (n_blk, tq), 1)) // SEL_BLOCK
    dist = tb - jb
    forced = (jb == 0) | ((dist >= 0) & (dist < N_LOCAL_BLOCKS))
    score = jnp.where(jb > tb, -jnp.inf, jnp.where(forced, jnp.inf, imp_t))
    rank = jnp.zeros((n_blk, tq), jnp.int32)
    for jp in range(n_blk):
        rowv = score[jp:jp + 1, :]
        beats = (rowv > score) | ((rowv == score) & (jb > jp))
        rank = rank + jnp.where(beats, 1, 0)
    selbias_ref[...] = jnp.where(rank < min(SEL_TOP_N, n_blk), 0.0, NEG)

    R = GROUP * tq

    def with_ones(v_t):
        extra = jnp.where(lax.broadcasted_iota(jnp.int32, (16, v_t.shape[1]), 0) == 0, 1.0, 0.0)
        return jnp.concatenate([v_t, extra.astype(BF16)], axis=0)

    bpc = tk // SEL_BLOCK
    vpc = tk // LANE

    def sel_chunk(c, carry, causal):
        m_i, l_i, acc = carry
        start = pl.multiple_of(c * tk, tk)
        k_c = ks_ref[0, 0, pl.ds(start, tk), :]
        v_t = jnp.concatenate([vst_ref[0, 0, c * vpc + k] for k in range(vpc)], axis=1)
        rows = [jnp.broadcast_to(selbias_ref[pl.ds(c * bpc + j, 1), :], (SEL_BLOCK, tq)) for j in range(bpc)]
        bias = jnp.concatenate(rows, axis=0)
        if causal:
            kpos = start + lax.broadcasted_iota(jnp.int32, (tk, 1), 0)
            bias = jnp.where(kpos <= t_lane, bias, NEG)
        s = _dot(k_c, qr_t) + all_heads(bias)
        m_new = jnp.maximum(m_i, jnp.max(s, axis=0, keepdims=True))
        alpha = jnp.exp2(m_i - m_new)
        pv = _dot(with_ones(v_t), jnp.exp2(s - m_new).astype(BF16))
        return m_new, alpha * l_i + pv[HEAD_DIM:HEAD_DIM + 1], alpha * acc + pv[:HEAD_DIM]

    init = (jnp.full((1, R), NEG, F32), jnp.zeros((1, R), F32), jnp.zeros((HEAD_DIM, R), F32))
    c_diag = q0 // tk
    carry = lax.fori_loop(0, c_diag, lambda c, cr: sel_chunk(c, cr, False), init)
    _, l_s, acc_s = sel_chunk(c_diag, carry, True)
    os_t = acc_s * (1.0 / jnp.maximum(l_s, TINY))

    wk = WINDOW + tq
    w0 = pl.multiple_of(jnp.maximum(q0 - WINDOW, 0), LANE)
    k_w = kw_ref[0, 0, pl.ds(w0, wk), :]
    v_t = jnp.concatenate([vwt_ref[0, 0, w0 // LANE + k] for k in range(wk // LANE)], axis=1)
    diff = t_lane - (w0 + lax.broadcasted_iota(jnp.int32, (wk, 1), 0))
    bias = jnp.where((diff >= 0) & (diff < WINDOW), 0.0, NEG)
    s = _dot(k_w, qr_t) + all_heads(bias)
    pv = _dot(with_ones(v_t), jnp.exp2(s - jnp.max(s, axis=0, keepdims=True)).astype(BF16))
    ow_t = pv[:HEAD_DIM] * (1.0 / pv[HEAD_DIM:HEAD_DIM + 1])

    gt = gate_ref[0, 0]
    outs = []
    for hh, (a, b, c) in enumerate(zip(per_head(oc_t), per_head(os_t), per_head(ow_t))):
        r = hh * N_BRANCH
        outs.append(gt[r:r + 1] * a + gt[r + 1:r + 2] * b + gt[r + 2:r + 3] * c)
    o_ref[0] = jnp.concatenate(outs, axis=0).T.astype(o_ref.dtype)


def _cover_table(S):
    n_cmp = S // CMP_STRIDE
    n_blk = S // SEL_BLOCK
    cs = np.arange(n_cmp) * CMP_STRIDE
    ss = np.arange(n_blk) * SEL_BLOCK
    cover_t = ((cs[None, :] < ss[:, None] + SEL_BLOCK) & (cs[None, :] + CMP_LEN > ss[:, None]))
    cover_t[:, n_cmp - 1] = False
    return jnp.asarray(cover_t, BF16)


def _nsa_attention(qt, qrt, kc, vct, ks, vst, kw, vwt, gates, tq=256, tk=1024):
    B, _, _, S = qt.shape
    tk = min(tk, S)
    assert tq % LANE == 0 and tk % tq == 0 and S % tk == 0 and S >= WINDOW + tq
    covt = _cover_table(S)
    n_cmp = kc.shape[2]
    n_blk = covt.shape[0]
    qspec = pl.BlockSpec((1, GROUP, HEAD_DIM, tq), lambda b, g, i: (b, g, 0, i))
    kspec = lambda a: pl.BlockSpec((1, 1) + a.shape[2:], lambda b, g, i: (b, g, 0, 0))
    vspec = pl.BlockSpec((1, 1, S // LANE, HEAD_DIM, LANE), lambda b, g, i: (b, g, 0, 0, 0))
    return pl.pallas_call(
        functools.partial(_nsa_kernel, tq=tq, tk=tk),
        grid=(B, N_KV_HEADS, S // tq),
        in_specs=[qspec, qspec,
                  pl.BlockSpec((1, 1, n_cmp, HEAD_DIM), lambda b, g, i: (b, g, 0, 0)),
                  pl.BlockSpec((1, 1, HEAD_DIM, n_cmp), lambda b, g, i: (b, g, 0, 0)),
                  kspec(ks), vspec, kspec(kw), vspec,
                  pl.BlockSpec((1, 1, GATE_ROWS, tq), lambda b, g, i: (b, g, 0, i)),
                  pl.BlockSpec(covt.shape, lambda b, g, i: (0, 0))],
        out_specs=pl.BlockSpec((1, tq, GROUP * HEAD_DIM), lambda b, g, i: (b, i, g)),
        out_shape=jax.ShapeDtypeStruct((B, S, NSA_WIDTH), BF16),
        scratch_shapes=[pltpu.VMEM((n_blk, tq), F32)],
        compiler_params=_params(("parallel", "parallel", "parallel")),
        name="nsa_attention",
    )(qt, qrt, kc, vct, ks, vst, kw, vwt, gates, covt)


def _xkv_kernel(m_ref, g_ref, w_ref, k_ref, v_ref):
    h = _rms(m_ref[0], g_ref[...]).astype(BF16)
    kv = _dot(h, w_ref[...])
    d = k_ref.shape[2]
    k_ref[0] = kv[:, :d].astype(k_ref.dtype)
    v_ref[0] = kv[:, d:].astype(v_ref.dtype)


def _xattn_kv(mem, g, wkv):
    B, M, D = mem.shape
    ospec = pl.BlockSpec((1, M, D), lambda b: (b, 0, 0))
    oshape = jax.ShapeDtypeStruct((B, M, D), BF16)
    return pl.pallas_call(
        _xkv_kernel,
        grid=(B,),
        in_specs=[pl.BlockSpec((1, M, D), lambda b: (b, 0, 0)),
                  pl.BlockSpec((1, D), lambda b: (0, 0)),
                  pl.BlockSpec(wkv.shape, lambda b: (0, 0))],
        out_specs=(ospec, ospec),
        out_shape=(oshape, oshape),
        compiler_params=_params(("parallel",)),
        name="xattn_kv",
    )(mem, g, wkv)


def _xattn_kernel(x_ref, a_ref, p_ref, wa_ref, wp_ref, g_ref, wq_ref, k_ref, v_ref, wo_ref, o_ref):
    x = x_ref[0] + _dot(a_ref[0], wa_ref[...]) + _dot(p_ref[0], wp_ref[...])
    h = _rms(x, g_ref[...]).astype(BF16)
    scale = X_HEAD_DIM ** -0.5
    q = (_dot(h, wq_ref[...]) * scale).astype(BF16)
    outs = []
    for hd in range(X_HEADS):
        sl = slice(hd * X_HEAD_DIM, (hd + 1) * X_HEAD_DIM)
        s = _dot_nt(q[:, sl], k_ref[0, :, sl])
        e = jnp.exp(s - jnp.max(s, axis=-1, keepdims=True))
        p = e / jnp.sum(e, axis=-1, keepdims=True)
        outs.append(_dot(p.astype(BF16), v_ref[0, :, sl]).astype(BF16))
    o = jnp.concatenate(outs, axis=-1)
    o_ref[0] = x + _dot(o, wo_ref[...])


def _xattn(x, a, p, wa, wp, g, wq, kx, vx, wo, tm=1024):
    B, S, D = x.shape
    M = kx.shape[1]
    row = lambda n: pl.BlockSpec((1, tm, n), lambda b, i: (b, i, 0))
    full = lambda w: pl.BlockSpec(w.shape, lambda b, i: (0, 0))
    mspec = pl.BlockSpec((1, M, D), lambda b, i: (b, 0, 0))
    return pl.pallas_call(
        _xattn_kernel,
        grid=(B, S // tm),
        in_specs=[row(D), row(a.shape[2]), row(p.shape[2]), full(wa), full(wp),
                  full(g), full(wq), mspec, mspec, full(wo)],
        out_specs=row(D),
        out_shape=jax.ShapeDtypeStruct((B, S, D), F32),
        compiler_params=_params(("parallel", "parallel")),
        name="out_proj_xattn",
    )(x, a, p, wa, wp, g, wq, kx, vx, wo)


def _silu(x):
    return x * jax.nn.sigmoid(x)


def _ffn_kernel(x_ref, g_ref, wg_ref, wu_ref, wd_ref, o_ref):
    x = x_ref[...]
    h = _rms(x, g_ref[...]).astype(BF16)
    act = _silu(_dot(h, wg_ref[...])) * _dot(h, wu_ref[...])
    o_ref[...] = x + _dot(act.astype(BF16), wd_ref[...])


def _ffn(x2, g, wg, wu, wd, tm=512):
    T, D = x2.shape
    resident = lambda w: pl.BlockSpec(w.shape, lambda i: (0, 0), pipeline_mode=pl.Buffered(1))
    return pl.pallas_call(
        _ffn_kernel,
        grid=(T // tm,),
        in_specs=[pl.BlockSpec((tm, D), lambda i: (i, 0)),
                  pl.BlockSpec((1, D), lambda i: (0, 0)),
                  resident(wg), resident(wu), resident(wd)],
        out_specs=pl.BlockSpec((tm, D), lambda i: (i, 0)),
        out_shape=jax.ShapeDtypeStruct((T, D), F32),
        compiler_params=_params(("parallel",)),
        name="dense_swiglu",
    )(x2, g, wg, wu, wd)


MOE_CHUNK = 512
MOE_ROW_TILE = 512
assert MOE_ROW_TILE % MOE_CHUNK == 0


MOE_PARTS = 2


def _tile_meta(m):
    return m >> (2 * MOE_PARTS), [(m & (1 << k)) != 0 for k in range(2 * MOE_PARTS)]


def _router_kernel(x_ref, g_ref, wr_ref, tri_ref, h_ref, route_ref, gate_ref, before_ref, total_ref, run_ref):
    c = pl.program_id(0)

    @pl.when(c == 0)
    def _():
        run_ref[...] = jnp.zeros_like(run_ref)

    hf = _rms(x_ref[...], g_ref[...])
    h_ref[...] = hf.astype(BF16)
    logits = jnp.dot(hf, wr_ref[...], preferred_element_type=F32, precision=lax.Precision.HIGHEST)
    lane = lax.broadcasted_iota(jnp.int32, logits.shape, 1)
    logits = jnp.where(lane < N_EXPERTS, logits, -jnp.inf)
    v1 = jnp.max(logits, axis=-1, keepdims=True)
    i1 = jnp.min(jnp.where(logits == v1, lane, LANE), axis=-1, keepdims=True)
    rest = jnp.where(lane == i1, -jnp.inf, logits)
    v2 = jnp.max(rest, axis=-1, keepdims=True)
    i2 = jnp.min(jnp.where(rest == v2, lane, LANE), axis=-1, keepdims=True)
    e2 = jnp.exp(v2 - v1)
    den = 1.0 + e2
    gate_ref[...] = jnp.where(lane == 0, 1.0 / den, jnp.where(lane == 1, e2 / den, 0.0))

    onehot = jnp.where((lane == i1) | (lane == i2), 1.0, 0.0)
    run = run_ref[...]
    rank = run + _dot(tri_ref[...], onehot.astype(BF16))
    r1 = jnp.sum(jnp.where(lane == i1, rank, 0.0), axis=-1, keepdims=True).astype(jnp.int32)
    r2 = jnp.sum(jnp.where(lane == i2, rank, 0.0), axis=-1, keepdims=True).astype(jnp.int32)
    route_ref[...] = jnp.where(lane == 0, i1, jnp.where(lane == 1, i2,
                               jnp.where(lane == 2, r1, jnp.where(lane == 3, r2, 0))))
    before_ref[0] = run
    run = run + jnp.sum(onehot, axis=0, keepdims=True)
    run_ref[...] = run
    total_ref[...] = run


def _router(x2, g, w_router):
    T, D = x2.shape
    C = T // MOE_CHUNK
    tri = jnp.asarray(np.tril(np.ones((MOE_CHUNK, MOE_CHUNK), np.float32), -1), BF16)
    row = lambda n: pl.BlockSpec((MOE_CHUNK, n), lambda c: (c, 0))
    full = lambda a: pl.BlockSpec(a.shape, lambda c: (0, 0))
    return pl.pallas_call(
        _router_kernel,
        grid=(C,),
        in_specs=[row(D), full(g), full(w_router), full(tri)],
        out_specs=(row(D), row(LANE), row(LANE),
                   pl.BlockSpec((1, 1, LANE), lambda c: (c, 0, 0)),
                   pl.BlockSpec((1, LANE), lambda c: (0, 0))),
        out_shape=(jax.ShapeDtypeStruct((T, D), BF16),
                   jax.ShapeDtypeStruct((T, LANE), jnp.int32),
                   jax.ShapeDtypeStruct((T, LANE), F32),
                   jax.ShapeDtypeStruct((C, 1, LANE), F32),
                   jax.ShapeDtypeStruct((1, LANE), F32)),
        scratch_shapes=[pltpu.VMEM((1, LANE), F32)],
        compiler_params=_params(("arbitrary",)),
        name="moe_router",
    )(x2, g, w_router, tri)


def _dispatch_kernel(a_ref, h_ref, tok_ref, gate_ref, zlo_ref, zhi_ref, zglo_ref, zghi_ref,
                     lo_ref, hi_ref, glo_ref, ghi_ref):
    del zlo_ref, zhi_ref, zglo_ref, zghi_ref
    e = pl.program_id(0)
    c = pl.program_id(1)
    idx = e * pl.num_programs(1) + c
    a, touched = _tile_meta(a_ref[idx])
    first = (c == 0) | (a != _tile_meta(a_ref[jnp.maximum(idx - 1, 0)])[0])

    @pl.when(first)
    def _():
        lo_ref[...] = jnp.zeros_like(lo_ref)
        hi_ref[...] = jnp.zeros_like(hi_ref)
        glo_ref[...] = jnp.zeros_like(glo_ref)
        ghi_ref[...] = jnp.zeros_like(ghi_ref)

    rows_per_part = MOE_CHUNK // MOE_PARTS

    def scatter(tile, part, x_ref, g_ref):
        tok = tok_ref[0]
        gts = gate_ref[0]
        r0 = part * rows_per_part
        rows = tile * MOE_CHUNK + r0 + lax.broadcasted_iota(jnp.int32, (rows_per_part, MOE_CHUNK), 0)
        c1 = tok[0:1] == rows
        c2 = tok[1:2] == rows
        x_ref[r0:r0 + rows_per_part, :] += _dot(jnp.where(c1 | c2, 1.0, 0.0).astype(BF16),
                                                h_ref[...]).astype(x_ref.dtype)
        g_ref[r0:r0 + rows_per_part, :] += jnp.sum(jnp.where(c1, gts[0:1], 0.0) + jnp.where(c2, gts[1:2], 0.0),
                                                   axis=1, keepdims=True)

    for k, (x_ref, g_ref) in enumerate(((lo_ref, glo_ref), (hi_ref, ghi_ref))):
        for part in range(MOE_PARTS):
            pl.when(touched[k * MOE_PARTS + part])(functools.partial(scatter, a + k, part, x_ref, g_ref))


def _dispatch(a_ec, h, tok_rows, gate_rows, n_slots):
    T, D = h.shape
    C = T // MOE_CHUNK
    zx = jnp.zeros((n_slots, D), BF16)
    zg = jnp.zeros((n_slots, 1), F32)
    any_spec = pl.BlockSpec(memory_space=pl.ANY)
    lo = lambda n: pl.BlockSpec((MOE_CHUNK, n), lambda e, c, a: (a[e * C + c] >> (2 * MOE_PARTS), 0))
    hi = lambda n: pl.BlockSpec((MOE_CHUNK, n), lambda e, c, a: ((a[e * C + c] >> (2 * MOE_PARTS)) + 1, 0))
    return pl.pallas_call(
        _dispatch_kernel,
        grid_spec=pltpu.PrefetchScalarGridSpec(
            num_scalar_prefetch=1,
            grid=(N_EXPERTS, C),
            in_specs=[pl.BlockSpec((MOE_CHUNK, D), lambda e, c, a: (c, 0)),
                      pl.BlockSpec((1, 8, MOE_CHUNK), lambda e, c, a: (c, 0, 0)),
                      pl.BlockSpec((1, 8, MOE_CHUNK), lambda e, c, a: (c, 0, 0)),
                      any_spec, any_spec, any_spec, any_spec],
            out_specs=(lo(D), hi(D), lo(1), hi(1))),
        out_shape=(jax.ShapeDtypeStruct(zx.shape, BF16), jax.ShapeDtypeStruct(zx.shape, BF16),
                   jax.ShapeDtypeStruct(zg.shape, F32), jax.ShapeDtypeStruct(zg.shape, F32)),
        input_output_aliases={4: 0, 5: 1, 6: 2, 7: 3},
        compiler_params=_params(("arbitrary", "arbitrary")),
        name="moe_dispatch",
    )(a_ec, h, tok_rows, gate_rows, zx, zx, zg, zg)


def _expert_ffn_kernel(te_ref, nu_ref, lo_ref, hi_ref, glo_ref, ghi_ref, wg_ref, wu_ref, wd_ref, o_ref, acc_ref):
    i = pl.program_id(0)
    f = pl.program_id(1)
    used = i < nu_ref[0]

    @pl.when(used & (f == 0))
    def _():
        acc_ref[...] = jnp.zeros_like(acc_ref)

    @pl.when(used)
    def _():
        x = lo_ref[...] + hi_ref[...]
        act = _silu(_dot(x, wg_ref[0])) * _dot(x, wu_ref[0])
        acc_ref[...] += _dot(act.astype(BF16), wd_ref[0])

    last = f == pl.num_programs(1) - 1

    @pl.when(used & last)
    def _():
        o_ref[...] = (acc_ref[...] * (glo_ref[...] + ghi_ref[...])).astype(o_ref.dtype)

    @pl.when(jnp.logical_not(used) & last)
    def _():
        o_ref[...] = jnp.zeros_like(o_ref)


def _expert_ffn(tile_expert, n_used, xs_lo, xs_hi, gs_lo, gs_hi, wg, wu, wd, tf=1792):
    N, D = xs_lo.shape
    F = wg.shape[2]
    nf = F // tf
    tr = MOE_ROW_TILE
    fidx = lambda i, f, te, nu: jnp.where(i < nu[0], f, nf - 1)
    row = lambda n: pl.BlockSpec((tr, n), lambda i, f, te, nu: (i, 0))
    return pl.pallas_call(
        _expert_ffn_kernel,
        grid_spec=pltpu.PrefetchScalarGridSpec(
            num_scalar_prefetch=2,
            grid=(N // tr, nf),
            in_specs=[row(D), row(D), row(1), row(1),
                      pl.BlockSpec((1, D, tf), lambda i, f, te, nu: (te[i], 0, fidx(i, f, te, nu))),
                      pl.BlockSpec((1, D, tf), lambda i, f, te, nu: (te[i], 0, fidx(i, f, te, nu))),
                      pl.BlockSpec((1, tf, D), lambda i, f, te, nu: (te[i], fidx(i, f, te, nu), 0))],
            out_specs=row(D),
            scratch_shapes=[pltpu.VMEM((tr, D), F32)]),
        out_shape=jax.ShapeDtypeStruct((N, D), BF16),
        compiler_params=_params(("arbitrary", "arbitrary")),
        name="moe_expert_ffn",
    )(tile_expert, n_used, xs_lo, xs_hi, gs_lo, gs_hi, wg, wu, wd)


def _combine_kernel(a_ref, x_ref, ylo_ref, yhi_ref, s1_ref, s2_ref, gf_ref, o_ref, acc_ref):
    c = pl.program_id(0)
    e = pl.program_id(1)

    @pl.when(e == 0)
    def _():
        acc_ref[...] = jnp.zeros_like(acc_ref)

    a, touched = _tile_meta(a_ref[c * pl.num_programs(1) + e])
    cols_per_part = MOE_CHUNK // MOE_PARTS

    def gather(tile, part, y_ref):
        s1 = s1_ref[...]
        s2 = s2_ref[...]
        c0 = part * cols_per_part
        picks = []
        for j in range(cols_per_part // LANE):
            cols = tile * MOE_CHUNK + c0 + j * LANE + lax.broadcasted_iota(jnp.int32, (MOE_CHUNK, LANE), 1)
            picks.append(jnp.where((s1 == cols) | (s2 == cols), 1.0, 0.0).astype(BF16))
        acc_ref[...] += _dot(jnp.concatenate(picks, axis=1), y_ref[c0:c0 + cols_per_part, :])

    for k, y_ref in enumerate((ylo_ref, yhi_ref)):
        for part in range(MOE_PARTS):
            pl.when(touched[k * MOE_PARTS + part])(functools.partial(gather, a + k, part, y_ref))

    @pl.when(e == pl.num_programs(1) - 1)
    def _():
        o_ref[...] = _rms(x_ref[...] + acc_ref[...], gf_ref[...])


def _combine(a_ce, x2, ys, slot1_rep, slot2_rep, g_final):
    T, D = x2.shape
    C = T // MOE_CHUNK
    E = N_EXPERTS
    return pl.pallas_call(
        _combine_kernel,
        grid_spec=pltpu.PrefetchScalarGridSpec(
            num_scalar_prefetch=1,
            grid=(C, E),
            in_specs=[pl.BlockSpec((MOE_CHUNK, D), lambda c, e, a: (c, 0)),
                      pl.BlockSpec((MOE_CHUNK, D), lambda c, e, a: (a[c * E + e] >> (2 * MOE_PARTS), 0)),
                      pl.BlockSpec((MOE_CHUNK, D), lambda c, e, a: ((a[c * E + e] >> (2 * MOE_PARTS)) + 1, 0)),
                      pl.BlockSpec((MOE_CHUNK, LANE), lambda c, e, a: (c, 0)),
                      pl.BlockSpec((MOE_CHUNK, LANE), lambda c, e, a: (c, 0)),
                      pl.BlockSpec((1, D), lambda c, e, a: (0, 0))],
            out_specs=pl.BlockSpec((MOE_CHUNK, D), lambda c, e, a: (c, 0)),
            scratch_shapes=[pltpu.VMEM((MOE_CHUNK, D), F32)]),
        out_shape=jax.ShapeDtypeStruct((T, D), F32),
        compiler_params=_params(("arbitrary", "arbitrary")),
        name="moe_combine_final_norm",
    )(a_ce, x2, ys, ys, slot1_rep, slot2_rep, g_final)


def _rope_tables(S):
    inv = ROPE_THETA ** (-jnp.arange(ROPE_HALF, dtype=F32) / ROPE_HALF)
    ang = jnp.arange(S, dtype=F32)[:, None] * inv[None, :]
    cos, sin = jnp.cos(ang), jnp.sin(ang)
    pad = HEAD_DIM - ROPE_DIM
    cos_h = jnp.concatenate([cos, cos, jnp.ones((S, pad), F32)], axis=-1)
    sin_h = jnp.concatenate([-sin, sin, jnp.zeros((S, pad), F32)], axis=-1)
    reps = LANE // HEAD_DIM
    return jnp.tile(cos_h, (1, reps)), jnp.tile(sin_h, (1, reps))


def _pack_w_in(w):
    per_group = GROUP * N_BRANCH
    gates = w[:, GATE_OFF:GATE_OFF + N_KV_HEADS * per_group].reshape(-1, N_KV_HEADS, per_group)
    gates = jnp.pad(gates, ((0, 0), (0, LANE // GATE_ROWS - N_KV_HEADS), (0, GATE_ROWS - per_group)))
    gates = gates.reshape(-1, LANE)
    return jnp.concatenate([w[:, :GATE_OFF], gates, w[:, GATE_OFF + N_KV_HEADS * per_group:]], axis=1).astype(BF16)


def _mixer_heads(x, norm_g, w_in, pe_k, w1_k, b1_k, w2_k, pe_v, w1_v, b1_v, w2_v,
                 w_pool, pool_scale, cos_t, sin_t):
    B, S, D = x.shape
    qt, qrt, kcr, vcr, ks, vst, kw, vwt, gates, p = _mixer_proj(
        x, norm_g.reshape(1, D), _pack_w_in(w_in), cos_t, sin_t, w_pool.astype(BF16), pool_scale.reshape(1, -1))
    half = CMP_LEN // 2
    cw = half * HEAD_DIM
    pad_w2 = lambda w2: jnp.pad(w2, ((0, 0), (0, LANE - HEAD_DIM))).astype(BF16)
    kc, vct = _compress(
        kcr, vcr,
        pe_k.reshape(2, cw), w1_k.reshape(2, cw, CMP_HIDDEN).astype(BF16), b1_k.reshape(1, -1), pad_w2(w2_k),
        pe_v.reshape(2, cw), w1_v.reshape(2, cw, CMP_HIDDEN).astype(BF16), b1_v.reshape(1, -1), pad_w2(w2_v))
    return _nsa_attention(qt, qrt, kc, vct, ks, vst, kw, vwt, gates), p


def _moe_layer(x2, g_ffn, router, wg, wu, wd, g_final):
    T, D = x2.shape
    E = N_EXPERTS
    C = T // MOE_CHUNK
    i32 = jnp.int32
    w_router = jnp.pad(router, ((0, 0), (0, LANE - E)))
    h, route, gates, before, total = _router(x2, g_ffn.reshape(1, D), w_router)

    counts = total[0, :E].astype(i32)
    tiles = (counts + MOE_ROW_TILE - 1) // MOE_ROW_TILE
    ends = jnp.cumsum(tiles)
    off = (ends - tiles) * MOE_ROW_TILE
    e1, e2, r1, r2 = route[:, 0], route[:, 1], route[:, 2], route[:, 3]
    slot1 = off[e1] + r1
    slot2 = off[e2] + r2
    first = off[None, :] + before[:, 0, :E].astype(i32)
    after = jnp.concatenate([before[1:, 0, :E], total[:, :E]], axis=0).astype(i32)
    last = off[None, :] + after - 1
    lo_tile = first // MOE_CHUNK
    tile_meta = lo_tile << (2 * MOE_PARTS)
    part_rows = MOE_CHUNK // MOE_PARTS
    for k in range(2 * MOE_PARTS):
        start = lo_tile * MOE_CHUNK + k * part_rows
        touched = (last >= first) & (first < start + part_rows) & (last >= start)
        tile_meta = tile_meta + (touched.astype(i32) << k)
    n_row_tiles = (2 * T) // MOE_ROW_TILE + E + 1 + MOE_CHUNK // MOE_ROW_TILE
    n_used = ends[-1:]
    tile_ids = jnp.minimum(jnp.arange(n_row_tiles, dtype=i32), n_used[0] - 1)
    tile_expert = jnp.minimum(jnp.sum(ends[None, :] <= tile_ids[:, None], axis=1), E - 1).astype(i32)

    zero = jnp.zeros_like(slot1)
    tok = jnp.stack([slot1, slot2, zero, zero, zero, zero, zero, zero])
    tok_rows = tok.reshape(8, C, MOE_CHUNK).transpose(1, 0, 2)
    gz = jnp.zeros((T,), F32)
    gate_rows = jnp.stack([gates[:, 0], gates[:, 1], gz, gz, gz, gz, gz, gz]).reshape(8, C, MOE_CHUNK).transpose(1, 0, 2)
    slot1_rep = jnp.broadcast_to(slot1[:, None], (T, LANE))
    slot2_rep = jnp.broadcast_to(slot2[:, None], (T, LANE))

    xs_lo, xs_hi, gs_lo, gs_hi = _dispatch(tile_meta.T.reshape(-1).astype(i32), h, tok_rows, gate_rows,
                                           n_row_tiles * MOE_ROW_TILE)
    ys = _expert_ffn(tile_expert, n_used.astype(i32), xs_lo, xs_hi, gs_lo, gs_hi,
                     wg.astype(BF16), wu.astype(BF16), wd.astype(BF16))
    return _combine(tile_meta.reshape(-1).astype(i32), x2, ys, slot1_rep, slot2_rep, g_final.reshape(1, D))


def kernel(x, mem, norm_mix, w_in, cmp_pe_k, cmp_w1_k, cmp_b1_k, cmp_w2_k, cmp_pe_v, cmp_w1_v, cmp_b1_v, cmp_w2_v, w_pool, pool_scale, w_out, norm_x, norm_mem, wq_x, wk_x, wv_x, wo_x, norm_ffn, ffn_wg, ffn_wu, ffn_wd, moe_router, moe_wg, moe_wu, moe_wd, norm_final):
    B, S, D = x.shape
    depth = norm_mix.shape[0]
    assert depth == 2, "the final RMSNorm is fused into the expert layer, which must be the last one"
    cos_t, sin_t = _rope_tables(S)
    for layer in range(depth):
        a, p = _mixer_heads(x, norm_mix[layer], w_in[layer],
                            cmp_pe_k[layer], cmp_w1_k[layer], cmp_b1_k[layer], cmp_w2_k[layer],
                            cmp_pe_v[layer], cmp_w1_v[layer], cmp_b1_v[layer], cmp_w2_v[layer],
                            w_pool[layer], pool_scale[layer], cos_t, sin_t)
        wkv = jnp.concatenate([wk_x[layer], wv_x[layer]], axis=1).astype(BF16)
        kx, vx = _xattn_kv(mem, norm_mem[layer].reshape(1, D), wkv)
        wo_mix = w_out[layer].astype(BF16)
        x = _xattn(x, a, p, wo_mix[:NSA_WIDTH], wo_mix[NSA_WIDTH:], norm_x[layer].reshape(1, D),
                   wq_x[layer].astype(BF16), kx, vx, wo_x[layer].astype(BF16))
        x2 = x.reshape(B * S, D)
        j = layer // 2
        if layer % 2 == 0:
            x2 = _ffn(x2, norm_ffn[layer].reshape(1, D), ffn_wg[j].astype(BF16), ffn_wu[j].astype(BF16),
                      ffn_wd[j].astype(BF16))
        else:
            x2 = _moe_layer(x2, norm_ffn[layer], moe_router[j], moe_wg[j], moe_wu[j], moe_wd[j], norm_final)
        x = x2.reshape(B, S, D)
    return x
```

```python
import functools
import math

import numpy as np
import jax
import jax.numpy as jnp
from jax import lax
from jax.experimental import pallas as pl
from jax.experimental.pallas import tpu as pltpu

F32 = jnp.float32
BF16 = jnp.bfloat16

D_MODEL = 1024
HEAD_DIM = 64
N_HEADS = 8
N_KV_HEADS = 2
GROUP = N_HEADS // N_KV_HEADS
NSA_WIDTH = N_HEADS * HEAD_DIM
KV_WIDTH = N_KV_HEADS * HEAD_DIM
N_BRANCH = 3
GATE_ROWS = 16
POOL_WIDTH = 512
POOL_WINDOWS = (2, 4, 8, 16)
POOL_GROUP = 128
POOL_HALO = 16
ROPE_DIM = 16
ROPE_HALF = 8
ROPE_THETA = 500000.0
CMP_LEN = 32
CMP_STRIDE = 16
CMP_HIDDEN = 256
SEL_BLOCK = 64
SEL_TOP_N = 16
N_LOCAL_BLOCKS = 2
WINDOW = 512
X_HEADS = 4
X_HEAD_DIM = 256
N_EXPERTS = 8
EPS = 1e-6

LANE = 128
IN_PACKED = NSA_WIDTH + 6 * KV_WIDTH + LANE + POOL_WIDTH
GATE_OFF = NSA_WIDTH + 6 * KV_WIDTH
POOL_OFF = GATE_OFF + LANE

NEG = -1e30
TINY = float(np.finfo(np.float32).tiny)
VMEM_LIMIT = 56 * 1024 * 1024


def _dot(a, b):
    return jnp.dot(a, b, preferred_element_type=F32)


def _dot_nt(a, b):
    return lax.dot_general(a, b, (((1,), (1,)), ((), ())), preferred_element_type=F32)


def _rms(x, g):
    y = x * lax.rsqrt(jnp.mean(x * x, axis=-1, keepdims=True) + EPS)
    return y * g


def _params(sem, limit=VMEM_LIMIT):
    return pltpu.CompilerParams(dimension_semantics=sem, vmem_limit_bytes=limit)


def _mixer_proj_kernel(x_ref, g_ref, w_ref, cos_ref, sin_ref, wpool_ref, pscale_ref,
                       qt_ref, qrt_ref, kcr_ref, vcr_ref, ks_ref, vst_ref, kw_ref, vwt_ref,
                       gate_ref, p_ref, pool_buf):
    h = _rms(x_ref[0], g_ref[...]).astype(BF16)
    z = _dot(h, w_ref[...])
    tm = z.shape[0]
    cos = cos_ref[...]
    sin = sin_ref[...]
    lane = lax.broadcasted_iota(jnp.int32, (tm, LANE), 1)
    first = (lane & (HEAD_DIM - 1)) < ROPE_HALF
    scale = HEAD_DIM ** -0.5 * math.log2(math.e)

    def rope(xs):
        partner = jnp.where(first, pltpu.roll(xs, LANE - ROPE_HALF, 1), pltpu.roll(xs, ROPE_HALF, 1))
        return xs * cos + partner * sin

    for s in range(NSA_WIDTH // LANE):
        xs = z[:, s * LANE:(s + 1) * LANE]
        for src, ref in ((xs, qt_ref), (rope(xs), qrt_ref)):
            t = (src * scale).T.astype(BF16)
            ref[0, 2 * s] = t[:HEAD_DIM]
            ref[0, 2 * s + 1] = t[HEAD_DIM:]

    def kv_slab(i):
        return z[:, NSA_WIDTH + i * KV_WIDTH:NSA_WIDTH + (i + 1) * KV_WIDTH]

    for slab, ref in ((kv_slab(0), kcr_ref), (kv_slab(1), vcr_ref),
                      (rope(kv_slab(2)), ks_ref), (rope(kv_slab(4)), kw_ref)):
        for gg in range(N_KV_HEADS):
            ref[0, gg] = slab[:, gg * HEAD_DIM:(gg + 1) * HEAD_DIM].astype(ref.dtype)

    for slab, ref in ((kv_slab(3), vst_ref), (kv_slab(5), vwt_ref)):
        t = slab.T.astype(BF16)
        for gg in range(N_KV_HEADS):
            for k in range(tm // LANE):
                ref[0, gg, k] = t[gg * HEAD_DIM:(gg + 1) * HEAD_DIM, k * LANE:(k + 1) * LANE]

    sig_t = jax.nn.sigmoid(z[:, GATE_OFF:GATE_OFF + LANE]).T
    for gg in range(N_KV_HEADS):
        gate_ref[0, gg] = sig_t[gg * GATE_ROWS:(gg + 1) * GATE_ROWS]

    i = pl.program_id(1)

    @pl.when(i == 0)
    def _():
        pool_buf[0:POOL_HALO, :] = jnp.zeros((POOL_HALO, POOL_WIDTH), F32)

    @pl.when(i > 0)
    def _():
        pool_buf[0:POOL_HALO, :] = pool_buf[tm:tm + POOL_HALO, :]

    pool_buf[POOL_HALO:POOL_HALO + tm, :] = z[:, POOL_OFF:POOL_OFF + POOL_WIDTH]
    t1 = i * tm + lax.broadcasted_iota(jnp.int32, (tm, 1), 0) + 1
    for gi, w in enumerate(POOL_WINDOWS):
        cols = slice(gi * POOL_GROUP, (gi + 1) * POOL_GROUP)
        cur = pool_buf[POOL_HALO:POOL_HALO + tm, cols]
        tot = cur
        for k in range(1, w):
            tot = tot + pool_buf[POOL_HALO - k:POOL_HALO - k + tm, cols]
        d = tot / jnp.minimum(t1, w).astype(F32) - cur
        p_ref[0, :, cols] = (_dot(d.astype(BF16), wpool_ref[gi]) * pscale_ref[:, cols]).astype(p_ref.dtype)


def _mixer_proj(x, g, w_packed, cos_t, sin_t, w_pool, pool_scale, tm=1024):
    B, S, D = x.shape
    G = N_KV_HEADS
    sd = jax.ShapeDtypeStruct
    out_shape = (sd((B, N_HEADS, HEAD_DIM, S), BF16), sd((B, N_HEADS, HEAD_DIM, S), BF16),
                 sd((B, G, S, HEAD_DIM), F32), sd((B, G, S, HEAD_DIM), F32),
                 sd((B, G, S, HEAD_DIM), BF16), sd((B, G, S // LANE, HEAD_DIM, LANE), BF16),
                 sd((B, G, S, HEAD_DIM), BF16), sd((B, G, S // LANE, HEAD_DIM, LANE), BF16),
                 sd((B, G, GATE_ROWS, S), F32),
                 sd((B, S, POOL_WIDTH), BF16))
    qspec = pl.BlockSpec((1, N_HEADS, HEAD_DIM, tm), lambda b, i: (b, 0, 0, i))
    kspec = pl.BlockSpec((1, G, tm, HEAD_DIM), lambda b, i: (b, 0, i, 0))
    vspec = pl.BlockSpec((1, G, tm // LANE, HEAD_DIM, LANE), lambda b, i: (b, 0, i, 0, 0))
    out_specs = (qspec, qspec, kspec, kspec, kspec, vspec, kspec, vspec,
                 pl.BlockSpec((1, G, GATE_ROWS, tm), lambda b, i: (b, 0, 0, i)),
                 pl.BlockSpec((1, tm, POOL_WIDTH), lambda b, i: (b, i, 0)))
    return pl.pallas_call(
        _mixer_proj_kernel,
        grid=(B, S // tm),
        in_specs=[pl.BlockSpec((1, tm, D), lambda b, i: (b, i, 0)),
                  pl.BlockSpec((1, D), lambda b, i: (0, 0)),
                  pl.BlockSpec((D, IN_PACKED), lambda b, i: (0, 0)),
                  pl.BlockSpec((tm, LANE), lambda b, i: (i, 0)),
                  pl.BlockSpec((tm, LANE), lambda b, i: (i, 0)),
                  pl.BlockSpec(w_pool.shape, lambda b, i: (0, 0, 0)),
                  pl.BlockSpec((1, POOL_WIDTH), lambda b, i: (0, 0))],
        out_specs=out_specs,
        out_shape=out_shape,
        scratch_shapes=[pltpu.VMEM((tm + POOL_HALO, POOL_WIDTH), F32)],
        compiler_params=_params(("parallel", "arbitrary")),
        name="mixer_proj",
    )(x, g, w_packed, cos_t, sin_t, w_pool, pool_scale)


def _gelu_tanh(x):
    return 0.5 * x * (1.0 + jnp.tanh(math.sqrt(2.0 / math.pi) * (x + 0.044715 * (x * x * x))))


def _compress_kernel(kc_ref, vc_ref, pek_ref, w1k_ref, b1k_ref, w2k_ref,
                     pev_ref, w1v_ref, b1v_ref, w2v_ref, ko_ref, vto_ref):
    def comp(x_ref, pe_ref, w1_ref, b1_ref, w2_ref):
        x = x_ref[0, 0]
        n = x.shape[0]
        a = _dot((x + pe_ref[0:1, :]).astype(BF16), w1_ref[0])
        b = _dot((x + pe_ref[1:2, :]).astype(BF16), w1_ref[1])
        pre = a + pltpu.roll(b, n - 1, 0) + b1_ref[...]
        return _dot(_gelu_tanh(pre).astype(BF16), w2_ref[...])

    ko_ref[0, 0] = comp(kc_ref, pek_ref, w1k_ref, b1k_ref, w2k_ref)[:, :HEAD_DIM].astype(ko_ref.dtype)
    vto_ref[0, 0] = comp(vc_ref, pev_ref, w1v_ref, b1v_ref, w2v_ref).T[:HEAD_DIM].astype(vto_ref.dtype)


def _compress(kcr, vcr, pek, w1k, b1k, w2k, pev, w1v, b1v, w2v):
    B, G, S, _ = kcr.shape
    n = S // CMP_STRIDE
    cw = CMP_STRIDE * HEAD_DIM
    kc = kcr.reshape(B, G, n, cw)
    vc = vcr.reshape(B, G, n, cw)
    xspec = pl.BlockSpec((1, 1, n, cw), lambda b, g: (b, g, 0, 0))
    full = lambda a: pl.BlockSpec(a.shape, lambda b, g: (0,) * a.ndim)
    return pl.pallas_call(
        _compress_kernel,
        grid=(B, G),
        in_specs=[xspec, xspec, full(pek), full(w1k), full(b1k), full(w2k),
                  full(pev), full(w1v), full(b1v), full(w2v)],
        out_specs=(pl.BlockSpec((1, 1, n, HEAD_DIM), lambda b, g: (b, g, 0, 0)),
                   pl.BlockSpec((1, 1, HEAD_DIM, n), lambda b, g: (b, g, 0, 0))),
        out_shape=(jax.ShapeDtypeStruct((B, G, n, HEAD_DIM), BF16),
                   jax.ShapeDtypeStruct((B, G, HEAD_DIM, n), BF16)),
        compiler_params=_params(("parallel", "parallel")),
        name="compress_kv",
    )(kc, vc, pek, w1k, b1k, w2k, pev, w1v, b1v, w2v)


def _nsa_kernel(qt_ref, qrt_ref, kc_ref, vct_ref, ks_ref, vst_ref, kw_ref, vwt_ref, gate_ref,
                covt_ref, o_ref, selbias_ref, *, tq, tk):
    qt = pl.program_id(2)
    q0 = qt * tq
    n_cmp = kc_ref.shape[2]
    n_blk = covt_ref.shape[0]
    heads = lambda ref: jnp.concatenate([ref[0, hh] for hh in range(GROUP)], axis=1)
    q_t = heads(qt_ref)
    qr_t = heads(qrt_ref)
    t_lane = q0 + lax.broadcasted_iota(jnp.int32, (1, tq), 1)
    per_head = lambda a: [a[:, hh * tq:(hh + 1) * tq] for hh in range(GROUP)]
    all_heads = lambda a: jnp.concatenate([a] * GROUP, axis=1)

    sc = _dot(kc_ref[0, 0], q_t)
    cmp_end = lax.broadcasted_iota(jnp.int32, (n_cmp, 1), 0) * CMP_STRIDE + (CMP_LEN - 1)
    valid = cmp_end <= t_lane
    pcs = []
    for s_h in per_head(sc):
        s_h = jnp.where(valid, s_h, NEG)
        e = jnp.where(valid, jnp.exp2(s_h - jnp.max(s_h, axis=0, keepdims=True)), 0.0)
        pcs.append(e * (1.0 / jnp.maximum(jnp.sum(e, axis=0, keepdims=True), TINY)))
    oc_t = _dot(vct_ref[0, 0], jnp.concatenate(pcs, axis=1).astype(BF16))

    psum = pcs[0] + pcs[1] + pcs[2] + pcs[3]
    p_hi = psum.astype(BF16)
    r1 = psum - p_hi.astype(F32)
    p_mid = r1.astype(BF16)
    p_lo = (r1 - p_mid.astype(F32)).astype(BF16)
    covt = covt_ref[...]
    imp_t = _dot(covt, p_hi) + _dot(covt, p_mid) + _dot(covt, p_lo)

    jb = lax.broadcasted_iota(jnp.int32, (n_blk, tq), 0)
    tb = (q0 + lax.broadcasted_iota(jnp.int32, (n_blk, tq), 1)) // SEL_BLOCK
    dist = tb - jb
    forced = (jb == 0) | ((dist >= 0) & (dist < N_LOCAL_BLOCKS))
    score = jnp.where(jb > tb, -jnp.inf, jnp.where(forced, jnp.inf, imp_t))
    rank = jnp.zeros((n_blk, tq), jnp.int32)
    for jp in range(n_blk):
        rowv = score[jp:jp + 1, :]
        beats = (rowv > score) | ((rowv == score) & (jb > jp))
        rank = rank + jnp.where(beats, 1, 0)
    selbias_ref[...] = jnp.where(rank < min(SEL_TOP_N, n_blk), 0.0, NEG)

    R = GROUP * tq

    def with_ones(v_t):
        extra = jnp.where(lax.broadcasted_iota(jnp.int32, (16, v_t.shape[1]), 0) == 0, 1.0, 0.0)
        return jnp.concatenate([v_t, extra.astype(BF16)], axis=0)

    bpc = tk // SEL_BLOCK
    vpc = tk // LANE

    def sel_chunk(c, carry, causal):
        m_i, l_i, acc = carry
        start = pl.multiple_of(c * tk, tk)
        k_c = ks_ref[0, 0, pl.ds(start, tk), :]
        v_t = jnp.concatenate([vst_ref[0, 0, c * vpc + k] for k in range(vpc)], axis=1)
        rows = [jnp.broadcast_to(selbias_ref[pl.ds(c * bpc + j, 1), :], (SEL_BLOCK, tq)) for j in range(bpc)]
        bias = jnp.concatenate(rows, axis=0)
        if causal:
            kpos = start + lax.broadcasted_iota(jnp.int32, (tk, 1), 0)
            bias = jnp.where(kpos <= t_lane, bias, NEG)
        s = _dot(k_c, qr_t) + all_heads(bias)
        m_new = jnp.maximum(m_i, jnp.max(s, axis=0, keepdims=True))
        alpha = jnp.exp2(m_i - m_new)
        pv = _dot(with_ones(v_t), jnp.exp2(s - m_new).astype(BF16))
        return m_new, alpha * l_i + pv[HEAD_DIM:HEAD_DIM + 1], alpha * acc + pv[:HEAD_DIM]

    init = (jnp.full((1, R), NEG, F32), jnp.zeros((1, R), F32), jnp.zeros((HEAD_DIM, R), F32))
    c_diag = q0 // tk
    carry = lax.fori_loop(0, c_diag, lambda c, cr: sel_chunk(c, cr, False), init)
    _, l_s, acc_s = sel_chunk(c_diag, carry, True)
    os_t = acc_s * (1.0 / jnp.maximum(l_s, TINY))

    wk = WINDOW + tq
    w0 = pl.multiple_of(jnp.maximum(q0 - WINDOW, 0), LANE)
    k_w = kw_ref[0, 0, pl.ds(w0, wk), :]
    v_t = jnp.concatenate([vwt_ref[0, 0, w0 // LANE + k] for k in range(wk // LANE)], axis=1)
    diff = t_lane - (w0 + lax.broadcasted_iota(jnp.int32, (wk, 1), 0))
    bias = jnp.where((diff >= 0) & (diff < WINDOW), 0.0, NEG)
    s = _dot(k_w, qr_t) + all_heads(bias)
    pv = _dot(with_ones(v_t), jnp.exp2(s - jnp.max(s, axis=0, keepdims=True)).astype(BF16))
    ow_t = pv[:HEAD_DIM] * (1.0 / pv[HEAD_DIM:HEAD_DIM + 1])

    gt = gate_ref[0, 0]
    outs = []
    for hh, (a, b, c) in enumerate(zip(per_head(oc_t), per_head(os_t), per_head(ow_t))):
        r = hh * N_BRANCH
        outs.append(gt[r:r + 1] * a + gt[r + 1:r + 2] * b + gt[r + 2:r + 3] * c)
    o_ref[0] = jnp.concatenate(outs, axis=0).T.astype(o_ref.dtype)


def _cover_table(S):
    n_cmp = S // CMP_STRIDE
    n_blk = S // SEL_BLOCK
    cs = np.arange(n_cmp) * CMP_STRIDE
    ss = np.arange(n_blk) * SEL_BLOCK
    cover_t = ((cs[None, :] < ss[:, None] + SEL_BLOCK) & (cs[None, :] + CMP_LEN > ss[:, None]))
    cover_t[:, n_cmp - 1] = False
    return jnp.asarray(cover_t, BF16)


def _nsa_attention(qt, qrt, kc, vct, ks, vst, kw, vwt, gates, tq=256, tk=1024):
    B, _, _, S = qt.shape
    tk = min(tk, S)
    assert tq % LANE == 0 and tk % tq == 0 and S % tk == 0 and S >= WINDOW + tq
    covt = _cover_table(S)
    n_cmp = kc.shape[2]
    n_blk = covt.shape[0]
    qspec = pl.BlockSpec((1, GROUP, HEAD_DIM, tq), lambda b, g, i: (b, g, 0, i))
    kspec = lambda a: pl.BlockSpec((1, 1) + a.shape[2:], lambda b, g, i: (b, g, 0, 0))
    vspec = pl.BlockSpec((1, 1, S // LANE, HEAD_DIM, LANE), lambda b, g, i: (b, g, 0, 0, 0))
    return pl.pallas_call(
        functools.partial(_nsa_kernel, tq=tq, tk=tk),
        grid=(B, N_KV_HEADS, S // tq),
        in_specs=[qspec, qspec,
                  pl.BlockSpec((1, 1, n_cmp, HEAD_DIM), lambda b, g, i: (b, g, 0, 0)),
                  pl.BlockSpec((1, 1, HEAD_DIM, n_cmp), lambda b, g, i: (b, g, 0, 0)),
                  kspec(ks), vspec, kspec(kw), vspec,
                  pl.BlockSpec((1, 1, GATE_ROWS, tq), lambda b, g, i: (b, g, 0, i)),
                  pl.BlockSpec(covt.shape, lambda b, g, i: (0, 0))],
        out_specs=pl.BlockSpec((1, tq, GROUP * HEAD_DIM), lambda b, g, i: (b, i, g)),
        out_shape=jax.ShapeDtypeStruct((B, S, NSA_WIDTH), BF16),
        scratch_shapes=[pltpu.VMEM((n_blk, tq), F32)],
        compiler_params=_params(("parallel", "parallel", "parallel")),
        name="nsa_attention",
    )(qt, qrt, kc, vct, ks, vst, kw, vwt, gates, covt)


def _xkv_kernel(m_ref, g_ref, w_ref, k_ref, v_ref):
    h = _rms(m_ref[0], g_ref[...]).astype(BF16)
    kv = _dot(h, w_ref[...])
    d = k_ref.shape[2]
    k_ref[0] = kv[:, :d].astype(k_ref.dtype)
    v_ref[0] = kv[:, d:].astype(v_ref.dtype)


def _xattn_kv(mem, g, wkv):
    B, M, D = mem.shape
    ospec = pl.BlockSpec((1, M, D), lambda b: (b, 0, 0))
    oshape = jax.ShapeDtypeStruct((B, M, D), BF16)
    return pl.pallas_call(
        _xkv_kernel,
        grid=(B,),
        in_specs=[pl.BlockSpec((1, M, D), lambda b: (b, 0, 0)),
                  pl.BlockSpec((1, D), lambda b: (0, 0)),
                  pl.BlockSpec(wkv.shape, lambda b: (0, 0))],
        out_specs=(ospec, ospec),
        out_shape=(oshape, oshape),
        compiler_params=_params(("parallel",)),
        name="xattn_kv",
    )(mem, g, wkv)


def _xattn_kernel(x_ref, a_ref, p_ref, wa_ref, wp_ref, g_ref, wq_ref, k_ref, v_ref, wo_ref, o_ref):
    x = x_ref[0] + _dot(a_ref[0], wa_ref[...]) + _dot(p_ref[0], wp_ref[...])
    h = _rms(x, g_ref[...]).astype(BF16)
    scale = X_HEAD_DIM ** -0.5
    q = (_dot(h, wq_ref[...]) * scale).astype(BF16)
    outs = []
    for hd in range(X_HEADS):
        sl = slice(hd * X_HEAD_DIM, (hd + 1) * X_HEAD_DIM)
        s = _dot_nt(q[:, sl], k_ref[0, :, sl])
        e = jnp.exp(s - jnp.max(s, axis=-1, keepdims=True))
        p = e / jnp.sum(e, axis=-1, keepdims=True)
        outs.append(_dot(p.astype(BF16), v_ref[0, :, sl]).astype(BF16))
    o = jnp.concatenate(outs, axis=-1)
    o_ref[0] = x + _dot(o, wo_ref[...])


def _xattn(x, a, p, wa, wp, g, wq, kx, vx, wo, tm=1024):
    B, S, D = x.shape
    M = kx.shape[1]
    row = lambda n: pl.BlockSpec((1, tm, n), lambda b, i: (b, i, 0))
    full = lambda w: pl.BlockSpec(w.shape, lambda b, i: (0, 0))
    mspec = pl.BlockSpec((1, M, D), lambda b, i: (b, 0, 0))
    return pl.pallas_call(
        _xattn_kernel,
        grid=(B, S // tm),
        in_specs=[row(D), row(a.shape[2]), row(p.shape[2]), full(wa), full(wp),
                  full(g), full(wq), mspec, mspec, full(wo)],
        out_specs=row(D),
        out_shape=jax.ShapeDtypeStruct((B, S, D), F32),
        compiler_params=_params(("parallel", "parallel")),
        name="out_proj_xattn",
    )(x, a, p, wa, wp, g, wq, kx, vx, wo)


def _silu(x):
    return x * jax.nn.sigmoid(x)


def _ffn_kernel(x_ref, g_ref, wg_ref, wu_ref, wd_ref, o_ref):
    x = x_ref[...]
    h = _rms(x, g_ref[...]).astype(BF16)
    act = _silu(_dot(h, wg_ref[...])) * _dot(h, wu_ref[...])
    o_ref[...] = x + _dot(act.astype(BF16), wd_ref[...])


def _ffn(x2, g, wg, wu, wd, tm=512):
    T, D = x2.shape
    resident = lambda w: pl.BlockSpec(w.shape, lambda i: (0, 0), pipeline_mode=pl.Buffered(1))
    return pl.pallas_call(
        _ffn_kernel,
        grid=(T // tm,),
        in_specs=[pl.BlockSpec((tm, D), lambda i: (i, 0)),
                  pl.BlockSpec((1, D), lambda i: (0, 0)),
                  resident(wg), resident(wu), resident(wd)],
        out_specs=pl.BlockSpec((tm, D), lambda i: (i, 0)),
        out_shape=jax.ShapeDtypeStruct((T, D), F32),
        compiler_params=_params(("parallel",)),
        name="dense_swiglu",
    )(x2, g, wg, wu, wd)


MOE_CHUNK = 512
MOE_ROW_TILE = 512
assert MOE_ROW_TILE % MOE_CHUNK == 0


MOE_PARTS = 2


def _tile_meta(m):
    return m >> (2 * MOE_PARTS), [(m & (1 << k)) != 0 for k in range(2 * MOE_PARTS)]


def _router_kernel(x_ref, g_ref, wr_ref, tri_ref, h_ref, route_ref, gate_ref, before_ref, total_ref, run_ref):
    c = pl.program_id(0)

    @pl.when(c == 0)
    def _():
        run_ref[...] = jnp.zeros_like(run_ref)

    hf = _rms(x_ref[...], g_ref[...])
    h_ref[...] = hf.astype(BF16)
    logits = jnp.dot(hf, wr_ref[...], preferred_element_type=F32, precision=lax.Precision.HIGHEST)
    lane = lax.broadcasted_iota(jnp.int32, logits.shape, 1)
    logits = jnp.where(lane < N_EXPERTS, logits, -jnp.inf)
    v1 = jnp.max(logits, axis=-1, keepdims=True)
    i1 = jnp.min(jnp.where(logits == v1, lane, LANE), axis=-1, keepdims=True)
    rest = jnp.where(lane == i1, -jnp.inf, logits)
    v2 = jnp.max(rest, axis=-1, keepdims=True)
    i2 = jnp.min(jnp.where(rest == v2, lane, LANE), axis=-1, keepdims=True)
    e2 = jnp.exp(v2 - v1)
    den = 1.0 + e2
    gate_ref[...] = jnp.where(lane == 0, 1.0 / den, jnp.where(lane == 1, e2 / den, 0.0))

    onehot = jnp.where((lane == i1) | (lane == i2), 1.0, 0.0)
    run = run_ref[...]
    rank = run + _dot(tri_ref[...], onehot.astype(BF16))
    r1 = jnp.sum(jnp.where(lane == i1, rank, 0.0), axis=-1, keepdims=True).astype(jnp.int32)
    r2 = jnp.sum(jnp.where(lane == i2, rank, 0.0), axis=-1, keepdims=True).astype(jnp.int32)
    route_ref[...] = jnp.where(lane == 0, i1, jnp.where(lane == 1, i2,
                               jnp.where(lane == 2, r1, jnp.where(lane == 3, r2, 0))))
    before_ref[0] = run
    run = run + jnp.sum(onehot, axis=0, keepdims=True)
    run_ref[...] = run
    total_ref[...] = run


def _router(x2, g, w_router):
    T, D = x2.shape
    C = T // MOE_CHUNK
    tri = jnp.asarray(np.tril(np.ones((MOE_CHUNK, MOE_CHUNK), np.float32), -1), BF16)
    row = lambda n: pl.BlockSpec((MOE_CHUNK, n), lambda c: (c, 0))
    full = lambda a: pl.BlockSpec(a.shape, lambda c: (0, 0))
    return pl.pallas_call(
        _router_kernel,
        grid=(C,),
        in_specs=[row(D), full(g), full(w_router), full(tri)],
        out_specs=(row(D), row(LANE), row(LANE),
                   pl.BlockSpec((1, 1, LANE), lambda c: (c, 0, 0)),
                   pl.BlockSpec((1, LANE), lambda c: (0, 0))),
        out_shape=(jax.ShapeDtypeStruct((T, D), BF16),
                   jax.ShapeDtypeStruct((T, LANE), jnp.int32),
                   jax.ShapeDtypeStruct((T, LANE), F32),
                   jax.ShapeDtypeStruct((C, 1, LANE), F32),
                   jax.ShapeDtypeStruct((1, LANE), F32)),
        scratch_shapes=[pltpu.VMEM((1, LANE), F32)],
        compiler_params=_params(("arbitrary",)),
        name="moe_router",
    )(x2, g, w_router, tri)


def _dispatch_kernel(a_ref, h_ref, tok_ref, gate_ref, zlo_ref, zhi_ref, zglo_ref, zghi_ref,
                     lo_ref, hi_ref, glo_ref, ghi_ref):
    del zlo_ref, zhi_ref, zglo_ref, zghi_ref
    e = pl.program_id(0)
    c = pl.program_id(1)
    idx = e * pl.num_programs(1) + c
    a, touched = _tile_meta(a_ref[idx])
    first = (c == 0) | (a != _tile_meta(a_ref[jnp.maximum(idx - 1, 0)])[0])

    @pl.when(first)
    def _():
        lo_ref[...] = jnp.zeros_like(lo_ref)
        hi_ref[...] = jnp.zeros_like(hi_ref)
        glo_ref[...] = jnp.zeros_like(glo_ref)
        ghi_ref[...] = jnp.zeros_like(ghi_ref)

    rows_per_part = MOE_CHUNK // MOE_PARTS

    def scatter(tile, part, x_ref, g_ref):
        tok = tok_ref[0]
        gts = gate_ref[0]
        r0 = part * rows_per_part
        rows = tile * MOE_CHUNK + r0 + lax.broadcasted_iota(jnp.int32, (rows_per_part, MOE_CHUNK), 0)
        c1 = tok[0:1] == rows
        c2 = tok[1:2] == rows
        x_ref[r0:r0 + rows_per_part, :] += _dot(jnp.where(c1 | c2, 1.0, 0.0).astype(BF16),
                                                h_ref[...]).astype(x_ref.dtype)
        g_ref[r0:r0 + rows_per_part, :] += jnp.sum(jnp.where(c1, gts[0:1], 0.0) + jnp.where(c2, gts[1:2], 0.0),
                                                   axis=1, keepdims=True)

    for k, (x_ref, g_ref) in enumerate(((lo_ref, glo_ref), (hi_ref, ghi_ref))):
        for part in range(MOE_PARTS):
            pl.when(touched[k * MOE_PARTS + part])(functools.partial(scatter, a + k, part, x_ref, g_ref))


def _dispatch(a_ec, h, tok_rows, gate_rows, n_slots):
    T, D = h.shape
    C = T // MOE_CHUNK
    zx = jnp.zeros((n_slots, D), BF16)
    zg = jnp.zeros((n_slots, 1), F32)
    any_spec = pl.BlockSpec(memory_space=pl.ANY)
    lo = lambda n: pl.BlockSpec((MOE_CHUNK, n), lambda e, c, a: (a[e * C + c] >> (2 * MOE_PARTS), 0))
    hi = lambda n: pl.BlockSpec((MOE_CHUNK, n), lambda e, c, a: ((a[e * C + c] >> (2 * MOE_PARTS)) + 1, 0))
    return pl.pallas_call(
        _dispatch_kernel,
        grid_spec=pltpu.PrefetchScalarGridSpec(
            num_scalar_prefetch=1,
            grid=(N_EXPERTS, C),
            in_specs=[pl.BlockSpec((MOE_CHUNK, D), lambda e, c, a: (c, 0)),
                      pl.BlockSpec((1, 8, MOE_CHUNK), lambda e, c, a: (c, 0, 0)),
                      pl.BlockSpec((1, 8, MOE_CHUNK), lambda e, c, a: (c, 0, 0)),
                      any_spec, any_spec, any_spec, any_spec],
            out_specs=(lo(D), hi(D), lo(1), hi(1))),
        out_shape=(jax.ShapeDtypeStruct(zx.shape, BF16), jax.ShapeDtypeStruct(zx.shape, BF16),
                   jax.ShapeDtypeStruct(zg.shape, F32), jax.ShapeDtypeStruct(zg.shape, F32)),
        input_output_aliases={4: 0, 5: 1, 6: 2, 7: 3},
        compiler_params=_params(("arbitrary", "arbitrary")),
        name="moe_dispatch",
    )(a_ec, h, tok_rows, gate_rows, zx, zx, zg, zg)


def _expert_ffn_kernel(te_ref, nu_ref, lo_ref, hi_ref, glo_ref, ghi_ref, wg_ref, wu_ref, wd_ref, o_ref):
    del te_ref
    used = pl.program_id(0) < nu_ref[0]

    @pl.when(used)
    def _():
        x = lo_ref[...] + hi_ref[...]
        act = _silu(_dot(x, wg_ref[0])) * _dot(x, wu_ref[0])
        o_ref[...] = (_dot(act.astype(BF16), wd_ref[0]) * (glo_ref[...] + ghi_ref[...])).astype(o_ref.dtype)

    @pl.when(jnp.logical_not(used))
    def _():
        o_ref[...] = jnp.zeros_like(o_ref)


def _expert_ffn(tile_expert, n_used, xs_lo, xs_hi, gs_lo, gs_hi, wg, wu, wd, tr=256):
    N, D = xs_lo.shape
    sub = MOE_ROW_TILE // tr
    tile_expert = jnp.repeat(tile_expert, sub)
    n_used = n_used * sub
    row = lambda n: pl.BlockSpec((tr, n), lambda i, te, nu: (i, 0))
    expert = lambda w: pl.BlockSpec((1,) + w.shape[1:], lambda i, te, nu: (te[i], 0, 0))
    return pl.pallas_call(
        _expert_ffn_kernel,
        grid_spec=pltpu.PrefetchScalarGridSpec(
            num_scalar_prefetch=2,
            grid=(N // tr,),
            in_specs=[row(D), row(D), row(1), row(1), expert(wg), expert(wu), expert(wd)],
            out_specs=row(D)),
        out_shape=jax.ShapeDtypeStruct((N, D), BF16),
        compiler_params=_params(("arbitrary",)),
        name="moe_expert_ffn",
    )(tile_expert, n_used, xs_lo, xs_hi, gs_lo, gs_hi, wg, wu, wd)


def _combine_kernel(a_ref, x_ref, ylo_ref, yhi_ref, s1_ref, s2_ref, gf_ref, o_ref, acc_ref):
    c = pl.program_id(0)
    e = pl.program_id(1)

    @pl.when(e == 0)
    def _():
        acc_ref[...] = jnp.zeros_like(acc_ref)

    a, touched = _tile_meta(a_ref[c * pl.num_programs(1) + e])
    cols_per_part = MOE_CHUNK // MOE_PARTS

    def gather(tile, part, y_ref):
        s1 = s1_ref[...]
        s2 = s2_ref[...]
        c0 = part * cols_per_part
        picks = []
        for j in range(cols_per_part // LANE):
            cols = tile * MOE_CHUNK + c0 + j * LANE + lax.broadcasted_iota(jnp.int32, (MOE_CHUNK, LANE), 1)
            picks.append(jnp.where((s1 == cols) | (s2 == cols), 1.0, 0.0).astype(BF16))
        acc_ref[...] += _dot(jnp.concatenate(picks, axis=1), y_ref[c0:c0 + cols_per_part, :])

    for k, y_ref in enumerate((ylo_ref, yhi_ref)):
        for part in range(MOE_PARTS):
            pl.when(touched[k * MOE_PARTS + part])(functools.partial(gather, a + k, part, y_ref))

    @pl.when(e == pl.num_programs(1) - 1)
    def _():
        o_ref[...] = _rms(x_ref[...] + acc_ref[...], gf_ref[...])


def _combine(a_ce, x2, ys, slot1_rep, slot2_rep, g_final):
    T, D = x2.shape
    C = T // MOE_CHUNK
    E = N_EXPERTS
    return pl.pallas_call(
        _combine_kernel,
        grid_spec=pltpu.PrefetchScalarGridSpec(
            num_scalar_prefetch=1,
            grid=(C, E),
            in_specs=[pl.BlockSpec((MOE_CHUNK, D), lambda c, e, a: (c, 0)),
                      pl.BlockSpec((MOE_CHUNK, D), lambda c, e, a: (a[c * E + e] >> (2 * MOE_PARTS), 0)),
                      pl.BlockSpec((MOE_CHUNK, D), lambda c, e, a: ((a[c * E + e] >> (2 * MOE_PARTS)) + 1, 0)),
                      pl.BlockSpec((MOE_CHUNK, LANE), lambda c, e, a: (c, 0)),
                      pl.BlockSpec((MOE_CHUNK, LANE), lambda c, e, a: (c, 0)),
                      pl.BlockSpec((1, D), lambda c, e, a: (0, 0))],
            out_specs=pl.BlockSpec((MOE_CHUNK, D), lambda c, e, a: (c, 0)),
            scratch_shapes=[pltpu.VMEM((MOE_CHUNK, D), F32)]),
        out_shape=jax.ShapeDtypeStruct((T, D), F32),
        compiler_params=_params(("arbitrary", "arbitrary")),
        name="moe_combine_final_norm",
    )(a_ce, x2, ys, ys, slot1_rep, slot2_rep, g_final)


def _rope_tables(S):
    inv = ROPE_THETA ** (-jnp.arange(ROPE_HALF, dtype=F32) / ROPE_HALF)
    ang = jnp.arange(S, dtype=F32)[:, None] * inv[None, :]
    cos, sin = jnp.cos(ang), jnp.sin(ang)
    pad = HEAD_DIM - ROPE_DIM
    cos_h = jnp.concatenate([cos, cos, jnp.ones((S, pad), F32)], axis=-1)
    sin_h = jnp.concatenate([-sin, sin, jnp.zeros((S, pad), F32)], axis=-1)
    reps = LANE // HEAD_DIM
    return jnp.tile(cos_h, (1, reps)), jnp.tile(sin_h, (1, reps))


def _pack_w_in(w):
    per_group = GROUP * N_BRANCH
    gates = w[:, GATE_OFF:GATE_OFF + N_KV_HEADS * per_group].reshape(-1, N_KV_HEADS, per_group)
    gates = jnp.pad(gates, ((0, 0), (0, LANE // GATE_ROWS - N_KV_HEADS), (0, GATE_ROWS - per_group)))
    gates = gates.reshape(-1, LANE)
    return jnp.concatenate([w[:, :GATE_OFF], gates, w[:, GATE_OFF + N_KV_HEADS * per_group:]], axis=1).astype(BF16)


def _mixer_heads(x, norm_g, w_in, pe_k, w1_k, b1_k, w2_k, pe_v, w1_v, b1_v, w2_v,
                 w_pool, pool_scale, cos_t, sin_t):
    B, S, D = x.shape
    qt, qrt, kcr, vcr, ks, vst, kw, vwt, gates, p = _mixer_proj(
        x, norm_g.reshape(1, D), _pack_w_in(w_in), cos_t, sin_t, w_pool.astype(BF16), pool_scale.reshape(1, -1))
    half = CMP_LEN // 2
    cw = half * HEAD_DIM
    pad_w2 = lambda w2: jnp.pad(w2, ((0, 0), (0, LANE - HEAD_DIM))).astype(BF16)
    kc, vct = _compress(
        kcr, vcr,
        pe_k.reshape(2, cw), w1_k.reshape(2, cw, CMP_HIDDEN).astype(BF16), b1_k.reshape(1, -1), pad_w2(w2_k),
        pe_v.reshape(2, cw), w1_v.reshape(2, cw, CMP_HIDDEN).astype(BF16), b1_v.reshape(1, -1), pad_w2(w2_v))
    return _nsa_attention(qt, qrt, kc, vct, ks, vst, kw, vwt, gates), p


def _moe_layer(x2, g_ffn, router, wg, wu, wd, g_final):
    T, D = x2.shape
    E = N_EXPERTS
    C = T // MOE_CHUNK
    i32 = jnp.int32
    w_router = jnp.pad(router, ((0, 0), (0, LANE - E)))
    h, route, gates, before, total = _router(x2, g_ffn.reshape(1, D), w_router)

    counts = total[0, :E].astype(i32)
    tiles = (counts + MOE_ROW_TILE - 1) // MOE_ROW_TILE
    ends = jnp.cumsum(tiles)
    off = (ends - tiles) * MOE_ROW_TILE
    e1, e2, r1, r2 = route[:, 0], route[:, 1], route[:, 2], route[:, 3]
    slot1 = off[e1] + r1
    slot2 = off[e2] + r2
    first = off[None, :] + before[:, 0, :E].astype(i32)
    after = jnp.concatenate([before[1:, 0, :E], total[:, :E]], axis=0).astype(i32)
    last = off[None, :] + after - 1
    lo_tile = first // MOE_CHUNK
    tile_meta = lo_tile << (2 * MOE_PARTS)
    part_rows = MOE_CHUNK // MOE_PARTS
    for k in range(2 * MOE_PARTS):
        start = lo_tile * MOE_CHUNK + k * part_rows
        touched = (last >= first) & (first < start + part_rows) & (last >= start)
        tile_meta = tile_meta + (touched.astype(i32) << k)
    n_row_tiles = (2 * T) // MOE_ROW_TILE + E + 1 + MOE_CHUNK // MOE_ROW_TILE
    n_used = ends[-1:]
    tile_ids = jnp.minimum(jnp.arange(n_row_tiles, dtype=i32), n_used[0] - 1)
    tile_expert = jnp.minimum(jnp.sum(ends[None, :] <= tile_ids[:, None], axis=1), E - 1).astype(i32)

    zero = jnp.zeros_like(slot1)
    tok = jnp.stack([slot1, slot2, zero, zero, zero, zero, zero, zero])
    tok_rows = tok.reshape(8, C, MOE_CHUNK).transpose(1, 0, 2)
    gz = jnp.zeros((T,), F32)
    gate_rows = jnp.stack([gates[:, 0], gates[:, 1], gz, gz, gz, gz, gz, gz]).reshape(8, C, MOE_CHUNK).transpose(1, 0, 2)
    slot1_rep = jnp.broadcast_to(slot1[:, None], (T, LANE))
    slot2_rep = jnp.broadcast_to(slot2[:, None], (T, LANE))

    xs_lo, xs_hi, gs_lo, gs_hi = _dispatch(tile_meta.T.reshape(-1).astype(i32), h, tok_rows, gate_rows,
                                           n_row_tiles * MOE_ROW_TILE)
    ys = _expert_ffn(tile_expert, n_used.astype(i32), xs_lo, xs_hi, gs_lo, gs_hi,
                     wg.astype(BF16), wu.astype(BF16), wd.astype(BF16))
    return _combine(tile_meta.reshape(-1).astype(i32), x2, ys, slot1_rep, slot2_rep, g_final.reshape(1, D))


def kernel(x, mem, norm_mix, w_in, cmp_pe_k, cmp_w1_k, cmp_b1_k, cmp_w2_k, cmp_pe_v, cmp_w1_v, cmp_b1_v, cmp_w2_v, w_pool, pool_scale, w_out, norm_x, norm_mem, wq_x, wk_x, wv_x, wo_x, norm_ffn, ffn_wg, ffn_wu, ffn_wd, moe_router, moe_wg, moe_wu, moe_wd, norm_final):
    B, S, D = x.shape
    depth = norm_mix.shape[0]
    assert depth == 2, "the final RMSNorm is fused into the expert layer, which must be the last one"
    cos_t, sin_t = _rope_tables(S)
    for layer in range(depth):
        a, p = _mixer_heads(x, norm_mix[layer], w_in[layer],
                            cmp_pe_k[layer], cmp_w1_k[layer], cmp_b1_k[layer], cmp_w2_k[layer],
                            cmp_pe_v[layer], cmp_w1_v[layer], cmp_b1_v[layer], cmp_w2_v[layer],
                            w_pool[layer], pool_scale[layer], cos_t, sin_t)
        wkv = jnp.concatenate([wk_x[layer], wv_x[layer]], axis=1).astype(BF16)
        kx, vx = _xattn_kv(mem, norm_mem[layer].reshape(1, D), wkv)
        wo_mix = w_out[layer].astype(BF16)
        x = _xattn(x, a, p, wo_mix[:NSA_WIDTH], wo_mix[NSA_WIDTH:], norm_x[layer].reshape(1, D),
                   wq_x[layer].astype(BF16), kx, vx, wo_x[layer].astype(BF16))
        x2 = x.reshape(B * S, D)
        j = layer // 2
        if layer % 2 == 0:
            x2 = _ffn(x2, norm_ffn[layer].reshape(1, D), ffn_wg[j].astype(BF16), ffn_wu[j].astype(BF16),
                      ffn_wd[j].astype(BF16))
        else:
            x2 = _moe_layer(x2, norm_ffn[layer], moe_router[j], moe_wg[j], moe_wu[j], moe_wd[j], norm_final)
        x = x2.reshape(B, S, D)
    return x
```

```python
import functools
import math

import numpy as np
import jax
import jax.numpy as jnp
from jax import lax
from jax.experimental import pallas as pl
from jax.experimental.pallas import tpu as pltpu

F32 = jnp.float32
BF16 = jnp.bfloat16

D_MODEL = 1024
HEAD_DIM = 64
N_HEADS = 8
N_KV_HEADS = 2
GROUP = N_HEADS // N_KV_HEADS
NSA_WIDTH = N_HEADS * HEAD_DIM
KV_WIDTH = N_KV_HEADS * HEAD_DIM
N_BRANCH = 3
GATE_ROWS = 16
POOL_WIDTH = 512
POOL_WINDOWS = (2, 4, 8, 16)
POOL_GROUP = 128
POOL_HALO = 16
ROPE_DIM = 16
ROPE_HALF = 8
ROPE_THETA = 500000.0
CMP_LEN = 32
CMP_STRIDE = 16
CMP_HIDDEN = 256
SEL_BLOCK = 64
SEL_TOP_N = 16
N_LOCAL_BLOCKS = 2
WINDOW = 512
X_HEADS = 4
X_HEAD_DIM = 256
N_EXPERTS = 8
EPS = 1e-6

LANE = 128
IN_PACKED = NSA_WIDTH + 6 * KV_WIDTH + LANE + POOL_WIDTH
GATE_OFF = NSA_WIDTH + 6 * KV_WIDTH
POOL_OFF = GATE_OFF + LANE

NEG = -1e30
TINY = float(np.finfo(np.float32).tiny)
VMEM_LIMIT = 56 * 1024 * 1024


def _dot(a, b):
    return jnp.dot(a, b, preferred_element_type=F32)


def _dot_nt(a, b):
    return lax.dot_general(a, b, (((1,), (1,)), ((), ())), preferred_element_type=F32)


def _rms(x, g):
    y = x * lax.rsqrt(jnp.mean(x * x, axis=-1, keepdims=True) + EPS)
    return y * g


def _params(sem, limit=VMEM_LIMIT):
    return pltpu.CompilerParams(dimension_semantics=sem, vmem_limit_bytes=limit)


def _mixer_proj_kernel(x_ref, g_ref, w_ref, cos_ref, sin_ref, wpool_ref, pscale_ref,
                       qt_ref, qrt_ref, kcr_ref, vcr_ref, ks_ref, vst_ref, kw_ref, vwt_ref,
                       gate_ref, p_ref, pool_buf):
    h = _rms(x_ref[0], g_ref[...]).astype(BF16)
    z = _dot(h, w_ref[...])
    tm = z.shape[0]
    cos = cos_ref[...]
    sin = sin_ref[...]
    lane = lax.broadcasted_iota(jnp.int32, (tm, LANE), 1)
    first = (lane & (HEAD_DIM - 1)) < ROPE_HALF
    scale = HEAD_DIM ** -0.5 * math.log2(math.e)

    def rope(xs):
        partner = jnp.where(first, pltpu.roll(xs, LANE - ROPE_HALF, 1), pltpu.roll(xs, ROPE_HALF, 1))
        return xs * cos + partner * sin

    for s in range(NSA_WIDTH // LANE):
        xs = z[:, s * LANE:(s + 1) * LANE]
        for src, ref in ((xs, qt_ref), (rope(xs), qrt_ref)):
            t = (src * scale).T.astype(BF16)
            ref[0, 2 * s] = t[:HEAD_DIM]
            ref[0, 2 * s + 1] = t[HEAD_DIM:]

    def kv_slab(i):
        return z[:, NSA_WIDTH + i * KV_WIDTH:NSA_WIDTH + (i + 1) * KV_WIDTH]

    for slab, ref in ((kv_slab(0), kcr_ref), (kv_slab(1), vcr_ref),
                      (rope(kv_slab(2)), ks_ref), (rope(kv_slab(4)), kw_ref)):
        for gg in range(N_KV_HEADS):
            ref[0, gg] = slab[:, gg * HEAD_DIM:(gg + 1) * HEAD_DIM].astype(ref.dtype)

    for slab, ref in ((kv_slab(3), vst_ref), (kv_slab(5), vwt_ref)):
        t = slab.T.astype(BF16)
        for gg in range(N_KV_HEADS):
            for k in range(tm // LANE):
                ref[0, gg, k] = t[gg * HEAD_DIM:(gg + 1) * HEAD_DIM, k * LANE:(k + 1) * LANE]

    sig_t = jax.nn.sigmoid(z[:, GATE_OFF:GATE_OFF + LANE]).T
    for gg in range(N_KV_HEADS):
        gate_ref[0, gg] = sig_t[gg * GATE_ROWS:(gg + 1) * GATE_ROWS]

    i = pl.program_id(1)

    @pl.when(i == 0)
    def _():
        pool_buf[0:POOL_HALO, :] = jnp.zeros((POOL_HALO, POOL_WIDTH), F32)

    @pl.when(i > 0)
    def _():
        pool_buf[0:POOL_HALO, :] = pool_buf[tm:tm + POOL_HALO, :]

    pool_buf[POOL_HALO:POOL_HALO + tm, :] = z[:, POOL_OFF:POOL_OFF + POOL_WIDTH]
    t1 = i * tm + lax.broadcasted_iota(jnp.int32, (tm, 1), 0) + 1
    for gi, w in enumerate(POOL_WINDOWS):
        cols = slice(gi * POOL_GROUP, (gi + 1) * POOL_GROUP)
        cur = pool_buf[POOL_HALO:POOL_HALO + tm, cols]
        tot = cur
        for k in range(1, w):
            tot = tot + pool_buf[POOL_HALO - k:POOL_HALO - k + tm, cols]
        d = tot / jnp.minimum(t1, w).astype(F32) - cur
        p_ref[0, :, cols] = (_dot(d.astype(BF16), wpool_ref[gi]) * pscale_ref[:, cols]).astype(p_ref.dtype)


def _mixer_proj(x, g, w_packed, cos_t, sin_t, w_pool, pool_scale, tm=1024):
    B, S, D = x.shape
    G = N_KV_HEADS
    sd = jax.ShapeDtypeStruct
    out_shape = (sd((B, N_HEADS, HEAD_DIM, S), BF16), sd((B, N_HEADS, HEAD_DIM, S), BF16),
                 sd((B, G, S, HEAD_DIM), F32), sd((B, G, S, HEAD_DIM), F32),
                 sd((B, G, S, HEAD_DIM), BF16), sd((B, G, S // LANE, HEAD_DIM, LANE), BF16),
                 sd((B, G, S, HEAD_DIM), BF16), sd((B, G, S // LANE, HEAD_DIM, LANE), BF16),
                 sd((B, G, GATE_ROWS, S), F32),
                 sd((B, S, POOL_WIDTH), BF16))
    qspec = pl.BlockSpec((1, N_HEADS, HEAD_DIM, tm), lambda b, i: (b, 0, 0, i))
    kspec = pl.BlockSpec((1, G, tm, HEAD_DIM), lambda b, i: (b, 0, i, 0))
    vspec = pl.BlockSpec((1, G, tm // LANE, HEAD_DIM, LANE), lambda b, i: (b, 0, i, 0, 0))
    out_specs = (qspec, qspec, kspec, kspec, kspec, vspec, kspec, vspec,
                 pl.BlockSpec((1, G, GATE_ROWS, tm), lambda b, i: (b, 0, 0, i)),
                 pl.BlockSpec((1, tm, POOL_WIDTH), lambda b, i: (b, i, 0)))
    return pl.pallas_call(
        _mixer_proj_kernel,
        grid=(B, S // tm),
        in_specs=[pl.BlockSpec((1, tm, D), lambda b, i: (b, i, 0)),
                  pl.BlockSpec((1, D), lambda b, i: (0, 0)),
                  pl.BlockSpec((D, IN_PACKED), lambda b, i: (0, 0)),
                  pl.BlockSpec((tm, LANE), lambda b, i: (i, 0)),
                  pl.BlockSpec((tm, LANE), lambda b, i: (i, 0)),
                  pl.BlockSpec(w_pool.shape, lambda b, i: (0, 0, 0)),
                  pl.BlockSpec((1, POOL_WIDTH), lambda b, i: (0, 0))],
        out_specs=out_specs,
        out_shape=out_shape,
        scratch_shapes=[pltpu.VMEM((tm + POOL_HALO, POOL_WIDTH), F32)],
        compiler_params=_params(("parallel", "arbitrary")),
        name="mixer_proj",
    )(x, g, w_packed, cos_t, sin_t, w_pool, pool_scale)


def _gelu_tanh(x):
    return 0.5 * x * (1.0 + jnp.tanh(math.sqrt(2.0 / math.pi) * (x + 0.044715 * (x * x * x))))


def _compress_kernel(kc_ref, vc_ref, pek_ref, w1k_ref, b1k_ref, w2k_ref,
                     pev_ref, w1v_ref, b1v_ref, w2v_ref, ko_ref, vto_ref):
    def comp(x_ref, pe_ref, w1_ref, b1_ref, w2_ref):
        x = x_ref[0, 0]
        n = x.shape[0]
        a = _dot((x + pe_ref[0:1, :]).astype(BF16), w1_ref[0])
        b = _dot((x + pe_ref[1:2, :]).astype(BF16), w1_ref[1])
        pre = a + pltpu.roll(b, n - 1, 0) + b1_ref[...]
        return _dot(_gelu_tanh(pre).astype(BF16), w2_ref[...])

    ko_ref[0, 0] = comp(kc_ref, pek_ref, w1k_ref, b1k_ref, w2k_ref)[:, :HEAD_DIM].astype(ko_ref.dtype)
    vto_ref[0, 0] = comp(vc_ref, pev_ref, w1v_ref, b1v_ref, w2v_ref).T[:HEAD_DIM].astype(vto_ref.dtype)


def _compress(kcr, vcr, pek, w1k, b1k, w2k, pev, w1v, b1v, w2v):
    B, G, S, _ = kcr.shape
    n = S // CMP_STRIDE
    cw = CMP_STRIDE * HEAD_DIM
    kc = kcr.reshape(B, G, n, cw)
    vc = vcr.reshape(B, G, n, cw)
    xspec = pl.BlockSpec((1, 1, n, cw), lambda b, g: (b, g, 0, 0))
    full = lambda a: pl.BlockSpec(a.shape, lambda b, g: (0,) * a.ndim)
    return pl.pallas_call(
        _compress_kernel,
        grid=(B, G),
        in_specs=[xspec, xspec, full(pek), full(w1k), full(b1k), full(w2k),
                  full(pev), full(w1v), full(b1v), full(w2v)],
        out_specs=(pl.BlockSpec((1, 1, n, HEAD_DIM), lambda b, g: (b, g, 0, 0)),
                   pl.BlockSpec((1, 1, HEAD_DIM, n), lambda b, g: (b, g, 0, 0))),
        out_shape=(jax.ShapeDtypeStruct((B, G, n, HEAD_DIM), BF16),
                   jax.ShapeDtypeStruct((B, G, HEAD_DIM, n), BF16)),
        compiler_params=_params(("parallel", "parallel")),
        name="compress_kv",
    )(kc, vc, pek, w1k, b1k, w2k, pev, w1v, b1v, w2v)


def _nsa_kernel(qt_ref, qrt_ref, kc_ref, vct_ref, ks_ref, vst_ref, kw_ref, vwt_ref, gate_ref,
                covt_ref, o_ref, selbias_ref, *, tq, tk, tk_narrow):
    qt = pl.program_id(2)
    q0 = qt * tq
    n_cmp = kc_ref.shape[2]
    n_blk = covt_ref.shape[0]
    heads = lambda ref: jnp.concatenate([ref[0, hh] for hh in range(GROUP)], axis=1)
    q_t = heads(qt_ref)
    qr_t = heads(qrt_ref)
    t_lane = q0 + lax.broadcasted_iota(jnp.int32, (1, tq), 1)
    per_head = lambda a: [a[:, hh * tq:(hh + 1) * tq] for hh in range(GROUP)]
    all_heads = lambda a: jnp.concatenate([a] * GROUP, axis=1)

    sc = _dot(kc_ref[0, 0], q_t)
    cmp_end = lax.broadcasted_iota(jnp.int32, (n_cmp, 1), 0) * CMP_STRIDE + (CMP_LEN - 1)
    valid = cmp_end <= t_lane
    pcs = []
    for s_h in per_head(sc):
        s_h = jnp.where(valid, s_h, NEG)
        e = jnp.where(valid, jnp.exp2(s_h - jnp.max(s_h, axis=0, keepdims=True)), 0.0)
        pcs.append(e * (1.0 / jnp.maximum(jnp.sum(e, axis=0, keepdims=True), TINY)))
    oc_t = _dot(vct_ref[0, 0], jnp.concatenate(pcs, axis=1).astype(BF16))

    psum = pcs[0] + pcs[1] + pcs[2] + pcs[3]
    p_hi = psum.astype(BF16)
    r1 = psum - p_hi.astype(F32)
    p_mid = r1.astype(BF16)
    p_lo = (r1 - p_mid.astype(F32)).astype(BF16)
    covt = covt_ref[...]
    imp_t = _dot(covt, p_hi) + _dot(covt, p_mid) + _dot(covt, p_lo)

    jb = lax.broadcasted_iota(jnp.int32, (n_blk, tq), 0)
    tb = (q0 + lax.broadcasted_iota(jnp.int32, (n_blk, tq), 1)) // SEL_BLOCK
    dist = tb - jb
    forced = (jb == 0) | ((dist >= 0) & (dist < N_LOCAL_BLOCKS))
    score = jnp.where(jb > tb, -jnp.inf, jnp.where(forced, jnp.inf, imp_t))
    rank = jnp.zeros((n_blk, tq), jnp.int32)
    for jp in range(n_blk):
        rowv = score[jp:jp + 1, :]
        beats = (rowv > score) | ((rowv == score) & (jb > jp))
        rank = rank + jnp.where(beats, 1, 0)
    selbias_ref[...] = jnp.where(rank < min(SEL_TOP_N, n_blk), 0.0, NEG)

    R = GROUP * tq

    def with_ones(v_t):
        extra = jnp.where(lax.broadcasted_iota(jnp.int32, (16, v_t.shape[1]), 0) == 0, 1.0, 0.0)
        return jnp.concatenate([v_t, extra.astype(BF16)], axis=0)

    def sel_chunk(c, carry, width, causal):
        m_i, l_i, acc = carry
        bpc = width // SEL_BLOCK
        vpc = width // LANE
        start = pl.multiple_of(c * width, width)
        k_c = ks_ref[0, 0, pl.ds(start, width), :]
        v_t = jnp.concatenate([vst_ref[0, 0, c * vpc + k] for k in range(vpc)], axis=1)
        rows = [jnp.broadcast_to(selbias_ref[pl.ds(c * bpc + j, 1), :], (SEL_BLOCK, tq)) for j in range(bpc)]
        bias = jnp.concatenate(rows, axis=0)
        if causal:
            kpos = start + lax.broadcasted_iota(jnp.int32, (width, 1), 0)
            bias = jnp.where(kpos <= t_lane, bias, NEG)
        s = _dot(k_c, qr_t) + all_heads(bias)
        m_new = jnp.maximum(m_i, jnp.max(s, axis=0, keepdims=True))
        alpha = jnp.exp2(m_i - m_new)
        pv = _dot(with_ones(v_t), jnp.exp2(s - m_new).astype(BF16))
        return m_new, alpha * l_i + pv[HEAD_DIM:HEAD_DIM + 1], alpha * acc + pv[:HEAD_DIM]

    init = (jnp.full((1, R), NEG, F32), jnp.zeros((1, R), F32), jnp.zeros((HEAD_DIM, R), F32))
    n_wide = q0 // tk
    j_diag = q0 // tk_narrow
    carry = lax.fori_loop(0, n_wide, lambda c, cr: sel_chunk(c, cr, tk, False), init)
    carry = lax.fori_loop(n_wide * (tk // tk_narrow), j_diag, lambda c, cr: sel_chunk(c, cr, tk_narrow, False), carry)
    _, l_s, acc_s = sel_chunk(j_diag, carry, tk_narrow, True)
    os_t = acc_s * (1.0 / jnp.maximum(l_s, TINY))

    wk = WINDOW + tq
    w0 = pl.multiple_of(jnp.maximum(q0 - WINDOW, 0), LANE)
    k_w = kw_ref[0, 0, pl.ds(w0, wk), :]
    v_t = jnp.concatenate([vwt_ref[0, 0, w0 // LANE + k] for k in range(wk // LANE)], axis=1)
    diff = t_lane - (w0 + lax.broadcasted_iota(jnp.int32, (wk, 1), 0))
    bias = jnp.where((diff >= 0) & (diff < WINDOW), 0.0, NEG)
    s = _dot(k_w, qr_t) + all_heads(bias)
    pv = _dot(with_ones(v_t), jnp.exp2(s - jnp.max(s, axis=0, keepdims=True)).astype(BF16))
    ow_t = pv[:HEAD_DIM] * (1.0 / pv[HEAD_DIM:HEAD_DIM + 1])

    gt = gate_ref[0, 0]
    outs = []
    for hh, (a, b, c) in enumerate(zip(per_head(oc_t), per_head(os_t), per_head(ow_t))):
        r = hh * N_BRANCH
        outs.append(gt[r:r + 1] * a + gt[r + 1:r + 2] * b + gt[r + 2:r + 3] * c)
    o_ref[0] = jnp.concatenate(outs, axis=0).T.astype(o_ref.dtype)


def _cover_table(S):
    n_cmp = S // CMP_STRIDE
    n_blk = S // SEL_BLOCK
    cs = np.arange(n_cmp) * CMP_STRIDE
    ss = np.arange(n_blk) * SEL_BLOCK
    cover_t = ((cs[None, :] < ss[:, None] + SEL_BLOCK) & (cs[None, :] + CMP_LEN > ss[:, None]))
    cover_t[:, n_cmp - 1] = False
    return jnp.asarray(cover_t, BF16)


def _nsa_attention(qt, qrt, kc, vct, ks, vst, kw, vwt, gates, tq=256, tk=1024):
    B, _, _, S = qt.shape
    tk = min(tk, S)
    assert tq % LANE == 0 and tk % tq == 0 and S % tk == 0 and S >= WINDOW + tq
    covt = _cover_table(S)
    n_cmp = kc.shape[2]
    n_blk = covt.shape[0]
    qspec = pl.BlockSpec((1, GROUP, HEAD_DIM, tq), lambda b, g, i: (b, g, 0, i))
    kspec = lambda a: pl.BlockSpec((1, 1) + a.shape[2:], lambda b, g, i: (b, g, 0, 0))
    vspec = pl.BlockSpec((1, 1, S // LANE, HEAD_DIM, LANE), lambda b, g, i: (b, g, 0, 0, 0))
    return pl.pallas_call(
        functools.partial(_nsa_kernel, tq=tq, tk=tk, tk_narrow=min(tk, 2 * tq)),
        grid=(B, N_KV_HEADS, S // tq),
        in_specs=[qspec, qspec,
                  pl.BlockSpec((1, 1, n_cmp, HEAD_DIM), lambda b, g, i: (b, g, 0, 0)),
                  pl.BlockSpec((1, 1, HEAD_DIM, n_cmp), lambda b, g, i: (b, g, 0, 0)),
                  kspec(ks), vspec, kspec(kw), vspec,
                  pl.BlockSpec((1, 1, GATE_ROWS, tq), lambda b, g, i: (b, g, 0, i)),
                  pl.BlockSpec(covt.shape, lambda b, g, i: (0, 0))],
        out_specs=pl.BlockSpec((1, tq, GROUP * HEAD_DIM), lambda b, g, i: (b, i, g)),
        out_shape=jax.ShapeDtypeStruct((B, S, NSA_WIDTH), BF16),
        scratch_shapes=[pltpu.VMEM((n_blk, tq), F32)],
        compiler_params=_params(("parallel", "parallel", "parallel")),
        name="nsa_attention",
    )(qt, qrt, kc, vct, ks, vst, kw, vwt, gates, covt)


def _xkv_kernel(m_ref, g_ref, w_ref, k_ref, v_ref):
    h = _rms(m_ref[0], g_ref[...]).astype(BF16)
    kv = _dot(h, w_ref[...])
    d = k_ref.shape[2]
    k_ref[0] = kv[:, :d].astype(k_ref.dtype)
    v_ref[0] = kv[:, d:].astype(v_ref.dtype)


def _xattn_kv(mem, g, wkv):
    B, M, D = mem.shape
    ospec = pl.BlockSpec((1, M, D), lambda b: (b, 0, 0))
    oshape = jax.ShapeDtypeStruct((B, M, D), BF16)
    return pl.pallas_call(
        _xkv_kernel,
        grid=(B,),
        in_specs=[pl.BlockSpec((1, M, D), lambda b: (b, 0, 0)),
                  pl.BlockSpec((1, D), lambda b: (0, 0)),
                  pl.BlockSpec(wkv.shape, lambda b: (0, 0))],
        out_specs=(ospec, ospec),
        out_shape=(oshape, oshape),
        compiler_params=_params(("parallel",)),
        name="xattn_kv",
    )(mem, g, wkv)


def _xattn_kernel(x_ref, a_ref, p_ref, wa_ref, wp_ref, g_ref, wq_ref, k_ref, v_ref, wo_ref, o_ref):
    x = x_ref[0] + _dot(a_ref[0], wa_ref[...]) + _dot(p_ref[0], wp_ref[...])
    h = _rms(x, g_ref[...]).astype(BF16)
    scale = X_HEAD_DIM ** -0.5
    q = (_dot(h, wq_ref[...]) * scale).astype(BF16)
    outs = []
    for hd in range(X_HEADS):
        sl = slice(hd * X_HEAD_DIM, (hd + 1) * X_HEAD_DIM)
        s = _dot_nt(q[:, sl], k_ref[0, :, sl])
        e = jnp.exp(s - jnp.max(s, axis=-1, keepdims=True))
        p = e / jnp.sum(e, axis=-1, keepdims=True)
        outs.append(_dot(p.astype(BF16), v_ref[0, :, sl]).astype(BF16))
    o = jnp.concatenate(outs, axis=-1)
    o_ref[0] = x + _dot(o, wo_ref[...])


def _xattn(x, a, p, wa, wp, g, wq, kx, vx, wo, tm=1024):
    B, S, D = x.shape
    M = kx.shape[1]
    row = lambda n: pl.BlockSpec((1, tm, n), lambda b, i: (b, i, 0))
    full = lambda w: pl.BlockSpec(w.shape, lambda b, i: (0, 0))
    mspec = pl.BlockSpec((1, M, D), lambda b, i: (b, 0, 0))
    return pl.pallas_call(
        _xattn_kernel,
        grid=(B, S // tm),
        in_specs=[row(D), row(a.shape[2]), row(p.shape[2]), full(wa), full(wp),
                  full(g), full(wq), mspec, mspec, full(wo)],
        out_specs=row(D),
        out_shape=jax.ShapeDtypeStruct((B, S, D), F32),
        compiler_params=_params(("parallel", "parallel")),
        name="out_proj_xattn",
    )(x, a, p, wa, wp, g, wq, kx, vx, wo)


def _silu(x):
    return x * jax.nn.sigmoid(x)


def _ffn_kernel(x_ref, g_ref, wg_ref, wu_ref, wd_ref, o_ref):
    x = x_ref[...]
    h = _rms(x, g_ref[...]).astype(BF16)
    act = _silu(_dot(h, wg_ref[...])) * _dot(h, wu_ref[...])
    o_ref[...] = x + _dot(act.astype(BF16), wd_ref[...])


def _ffn(x2, g, wg, wu, wd, tm=512):
    T, D = x2.shape
    resident = lambda w: pl.BlockSpec(w.shape, lambda i: (0, 0), pipeline_mode=pl.Buffered(1))
    return pl.pallas_call(
        _ffn_kernel,
        grid=(T // tm,),
        in_specs=[pl.BlockSpec((tm, D), lambda i: (i, 0)),
                  pl.BlockSpec((1, D), lambda i: (0, 0)),
                  resident(wg), resident(wu), resident(wd)],
        out_specs=pl.BlockSpec((tm, D), lambda i: (i, 0)),
        out_shape=jax.ShapeDtypeStruct((T, D), F32),
        compiler_params=_params(("parallel",)),
        name="dense_swiglu",
    )(x2, g, wg, wu, wd)


MOE_CHUNK = 512
MOE_ROW_TILE = 512
assert MOE_ROW_TILE % MOE_CHUNK == 0


MOE_PARTS = 2


def _tile_meta(m):
    return m >> (2 * MOE_PARTS), [(m & (1 << k)) != 0 for k in range(2 * MOE_PARTS)]


def _router_kernel(x_ref, g_ref, wr_ref, tri_ref, h_ref, route_ref, gate_ref, before_ref, total_ref, run_ref):
    c = pl.program_id(0)

    @pl.when(c == 0)
    def _():
        run_ref[...] = jnp.zeros_like(run_ref)

    hf = _rms(x_ref[...], g_ref[...])
    h_ref[...] = hf.astype(BF16)
    logits = jnp.dot(hf, wr_ref[...], preferred_element_type=F32, precision=lax.Precision.HIGHEST)
    lane = lax.broadcasted_iota(jnp.int32, logits.shape, 1)
    logits = jnp.where(lane < N_EXPERTS, logits, -jnp.inf)
    v1 = jnp.max(logits, axis=-1, keepdims=True)
    i1 = jnp.min(jnp.where(logits == v1, lane, LANE), axis=-1, keepdims=True)
    rest = jnp.where(lane == i1, -jnp.inf, logits)
    v2 = jnp.max(rest, axis=-1, keepdims=True)
    i2 = jnp.min(jnp.where(rest == v2, lane, LANE), axis=-1, keepdims=True)
    e2 = jnp.exp(v2 - v1)
    den = 1.0 + e2
    gate_ref[...] = jnp.where(lane == 0, 1.0 / den, jnp.where(lane == 1, e2 / den, 0.0))

    onehot = jnp.where((lane == i1) | (lane == i2), 1.0, 0.0)
    run = run_ref[...]
    rank = run + _dot(tri_ref[...], onehot.astype(BF16))
    r1 = jnp.sum(jnp.where(lane == i1, rank, 0.0), axis=-1, keepdims=True).astype(jnp.int32)
    r2 = jnp.sum(jnp.where(lane == i2, rank, 0.0), axis=-1, keepdims=True).astype(jnp.int32)
    route_ref[...] = jnp.where(lane == 0, i1, jnp.where(lane == 1, i2,
                               jnp.where(lane == 2, r1, jnp.where(lane == 3, r2, 0))))
    before_ref[0] = run
    run = run + jnp.sum(onehot, axis=0, keepdims=True)
    run_ref[...] = run
    total_ref[...] = run


def _router(x2, g, w_router):
    T, D = x2.shape
    C = T // MOE_CHUNK
    tri = jnp.asarray(np.tril(np.ones((MOE_CHUNK, MOE_CHUNK), np.float32), -1), BF16)
    row = lambda n: pl.BlockSpec((MOE_CHUNK, n), lambda c: (c, 0))
    full = lambda a: pl.BlockSpec(a.shape, lambda c: (0, 0))
    return pl.pallas_call(
        _router_kernel,
        grid=(C,),
        in_specs=[row(D), full(g), full(w_router), full(tri)],
        out_specs=(row(D), row(LANE), row(LANE),
                   pl.BlockSpec((1, 1, LANE), lambda c: (c, 0, 0)),
                   pl.BlockSpec((1, LANE), lambda c: (0, 0))),
        out_shape=(jax.ShapeDtypeStruct((T, D), BF16),
                   jax.ShapeDtypeStruct((T, LANE), jnp.int32),
                   jax.ShapeDtypeStruct((T, LANE), F32),
                   jax.ShapeDtypeStruct((C, 1, LANE), F32),
                   jax.ShapeDtypeStruct((1, LANE), F32)),
        scratch_shapes=[pltpu.VMEM((1, LANE), F32)],
        compiler_params=_params(("arbitrary",)),
        name="moe_router",
    )(x2, g, w_router, tri)


def _dispatch_kernel(a_ref, h_ref, tok_ref, gate_ref, zlo_ref, zhi_ref, zglo_ref, zghi_ref,
                     lo_ref, hi_ref, glo_ref, ghi_ref):
    del zlo_ref, zhi_ref, zglo_ref, zghi_ref
    e = pl.program_id(0)
    c = pl.program_id(1)
    idx = e * pl.num_programs(1) + c
    a, touched = _tile_meta(a_ref[idx])
    first = (c == 0) | (a != _tile_meta(a_ref[jnp.maximum(idx - 1, 0)])[0])

    @pl.when(first)
    def _():
        lo_ref[...] = jnp.zeros_like(lo_ref)
        hi_ref[...] = jnp.zeros_like(hi_ref)
        glo_ref[...] = jnp.zeros_like(glo_ref)
        ghi_ref[...] = jnp.zeros_like(ghi_ref)

    rows_per_part = MOE_CHUNK // MOE_PARTS

    def scatter(tile, part, x_ref, g_ref):
        tok = tok_ref[0]
        gts = gate_ref[0]
        r0 = part * rows_per_part
        rows = tile * MOE_CHUNK + r0 + lax.broadcasted_iota(jnp.int32, (rows_per_part, MOE_CHUNK), 0)
        c1 = tok[0:1] == rows
        c2 = tok[1:2] == rows
        x_ref[r0:r0 + rows_per_part, :] += _dot(jnp.where(c1 | c2, 1.0, 0.0).astype(BF16),
                                                h_ref[...]).astype(x_ref.dtype)
        g_ref[r0:r0 + rows_per_part, :] += jnp.sum(jnp.where(c1, gts[0:1], 0.0) + jnp.where(c2, gts[1:2], 0.0),
                                                   axis=1, keepdims=True)

    for k, (x_ref, g_ref) in enumerate(((lo_ref, glo_ref), (hi_ref, ghi_ref))):
        for part in range(MOE_PARTS):
            pl.when(touched[k * MOE_PARTS + part])(functools.partial(scatter, a + k, part, x_ref, g_ref))


def _dispatch(a_ec, h, tok_rows, gate_rows, n_slots):
    T, D = h.shape
    C = T // MOE_CHUNK
    zx = jnp.zeros((n_slots, D), BF16)
    zg = jnp.zeros((n_slots, 1), F32)
    any_spec = pl.BlockSpec(memory_space=pl.ANY)
    lo = lambda n: pl.BlockSpec((MOE_CHUNK, n), lambda e, c, a: (a[e * C + c] >> (2 * MOE_PARTS), 0))
    hi = lambda n: pl.BlockSpec((MOE_CHUNK, n), lambda e, c, a: ((a[e * C + c] >> (2 * MOE_PARTS)) + 1, 0))
    return pl.pallas_call(
        _dispatch_kernel,
        grid_spec=pltpu.PrefetchScalarGridSpec(
            num_scalar_prefetch=1,
            grid=(N_EXPERTS, C),
            in_specs=[pl.BlockSpec((MOE_CHUNK, D), lambda e, c, a: (c, 0)),
                      pl.BlockSpec((1, 8, MOE_CHUNK), lambda e, c, a: (c, 0, 0)),
                      pl.BlockSpec((1, 8, MOE_CHUNK), lambda e, c, a: (c, 0, 0)),
                      any_spec, any_spec, any_spec, any_spec],
            out_specs=(lo(D), hi(D), lo(1), hi(1))),
        out_shape=(jax.ShapeDtypeStruct(zx.shape, BF16), jax.ShapeDtypeStruct(zx.shape, BF16),
                   jax.ShapeDtypeStruct(zg.shape, F32), jax.ShapeDtypeStruct(zg.shape, F32)),
        input_output_aliases={4: 0, 5: 1, 6: 2, 7: 3},
        compiler_params=_params(("arbitrary", "arbitrary")),
        name="moe_dispatch",
    )(a_ec, h, tok_rows, gate_rows, zx, zx, zg, zg)


def _expert_ffn_kernel(te_ref, nu_ref, lo_ref, hi_ref, glo_ref, ghi_ref, wg_ref, wu_ref, wd_ref, o_ref):
    del te_ref
    used = pl.program_id(0) < nu_ref[0]

    @pl.when(used)
    def _():
        x = lo_ref[...] + hi_ref[...]
        act = _silu(_dot(x, wg_ref[0])) * _dot(x, wu_ref[0])
        o_ref[...] = (_dot(act.astype(BF16), wd_ref[0]) * (glo_ref[...] + ghi_ref[...])).astype(o_ref.dtype)

    @pl.when(jnp.logical_not(used))
    def _():
        o_ref[...] = jnp.zeros_like(o_ref)


def _expert_ffn(tile_expert, n_used, xs_lo, xs_hi, gs_lo, gs_hi, wg, wu, wd, tr=256):
    N, D = xs_lo.shape
    sub = MOE_ROW_TILE // tr
    tile_expert = jnp.repeat(tile_expert, sub)
    n_used = n_used * sub
    row = lambda n: pl.BlockSpec((tr, n), lambda i, te, nu: (i, 0))
    expert = lambda w: pl.BlockSpec((1,) + w.shape[1:], lambda i, te, nu: (te[i], 0, 0))
    return pl.pallas_call(
        _expert_ffn_kernel,
        grid_spec=pltpu.PrefetchScalarGridSpec(
            num_scalar_prefetch=2,
            grid=(N // tr,),
            in_specs=[row(D), row(D), row(1), row(1), expert(wg), expert(wu), expert(wd)],
            out_specs=row(D)),
        out_shape=jax.ShapeDtypeStruct((N, D), BF16),
        compiler_params=_params(("arbitrary",)),
        name="moe_expert_ffn",
    )(tile_expert, n_used, xs_lo, xs_hi, gs_lo, gs_hi, wg, wu, wd)


def _combine_kernel(a_ref, x_ref, ylo_ref, yhi_ref, s1_ref, s2_ref, gf_ref, o_ref, acc_ref):
    c = pl.program_id(0)
    e = pl.program_id(1)

    @pl.when(e == 0)
    def _():
        acc_ref[...] = jnp.zeros_like(acc_ref)

    a, touched = _tile_meta(a_ref[c * pl.num_programs(1) + e])
    cols_per_part = MOE_CHUNK // MOE_PARTS

    def gather(tile, part, y_ref):
        s1 = s1_ref[...]
        s2 = s2_ref[...]
        c0 = part * cols_per_part
        picks = []
        for j in range(cols_per_part // LANE):
            cols = tile * MOE_CHUNK + c0 + j * LANE + lax.broadcasted_iota(jnp.int32, (MOE_CHUNK, LANE), 1)
            picks.append(jnp.where((s1 == cols) | (s2 == cols), 1.0, 0.0).astype(BF16))
        acc_ref[...] += _dot(jnp.concatenate(picks, axis=1), y_ref[c0:c0 + cols_per_part, :])

    for k, y_ref in enumerate((ylo_ref, yhi_ref)):
        for part in range(MOE_PARTS):
            pl.when(touched[k * MOE_PARTS + part])(functools.partial(gather, a + k, part, y_ref))

    @pl.when(e == pl.num_programs(1) - 1)
    def _():
        o_ref[...] = _rms(x_ref[...] + acc_ref[...], gf_ref[...])


def _combine(a_ce, x2, ys, slot1_rep, slot2_rep, g_final):
    T, D = x2.shape
    C = T // MOE_CHUNK
    E = N_EXPERTS
    return pl.pallas_call(
        _combine_kernel,
        grid_spec=pltpu.PrefetchScalarGridSpec(
            num_scalar_prefetch=1,
            grid=(C, E),
            in_specs=[pl.BlockSpec((MOE_CHUNK, D), lambda c, e, a: (c, 0)),
                      pl.BlockSpec((MOE_CHUNK, D), lambda c, e, a: (a[c * E + e] >> (2 * MOE_PARTS), 0)),
                      pl.BlockSpec((MOE_CHUNK, D), lambda c, e, a: ((a[c * E + e] >> (2 * MOE_PARTS)) + 1, 0)),
                      pl.BlockSpec((MOE_CHUNK, LANE), lambda c, e, a: (c, 0)),
                      pl.BlockSpec((MOE_CHUNK, LANE), lambda c, e, a: (c, 0)),
                      pl.BlockSpec((1, D), lambda c, e, a: (0, 0))],
            out_specs=pl.BlockSpec((MOE_CHUNK, D), lambda c, e, a: (c, 0)),
            scratch_shapes=[pltpu.VMEM((MOE_CHUNK, D), F32)]),
        out_shape=jax.ShapeDtypeStruct((T, D), F32),
        compiler_params=_params(("arbitrary", "arbitrary")),
        name="moe_combine_final_norm",
    )(a_ce, x2, ys, ys, slot1_rep, slot2_rep, g_final)


def _rope_tables(S):
    inv = ROPE_THETA ** (-jnp.arange(ROPE_HALF, dtype=F32) / ROPE_HALF)
    ang = jnp.arange(S, dtype=F32)[:, None] * inv[None, :]
    cos, sin = jnp.cos(ang), jnp.sin(ang)
    pad = HEAD_DIM - ROPE_DIM
    cos_h = jnp.concatenate([cos, cos, jnp.ones((S, pad), F32)], axis=-1)
    sin_h = jnp.concatenate([-sin, sin, jnp.zeros((S, pad), F32)], axis=-1)
    reps = LANE // HEAD_DIM
    return jnp.tile(cos_h, (1, reps)), jnp.tile(sin_h, (1, reps))


def _pack_w_in(w):
    per_group = GROUP * N_BRANCH
    gates = w[:, GATE_OFF:GATE_OFF + N_KV_HEADS * per_group].reshape(-1, N_KV_HEADS, per_group)
    gates = jnp.pad(gates, ((0, 0), (0, LANE // GATE_ROWS - N_KV_HEADS), (0, GATE_ROWS - per_group)))
    gates = gates.reshape(-1, LANE)
    return jnp.concatenate([w[:, :GATE_OFF], gates, w[:, GATE_OFF + N_KV_HEADS * per_group:]], axis=1).astype(BF16)


def _mixer_heads(x, norm_g, w_in, pe_k, w1_k, b1_k, w2_k, pe_v, w1_v, b1_v, w2_v,
                 w_pool, pool_scale, cos_t, sin_t):
    B, S, D = x.shape
    qt, qrt, kcr, vcr, ks, vst, kw, vwt, gates, p = _mixer_proj(
        x, norm_g.reshape(1, D), _pack_w_in(w_in), cos_t, sin_t, w_pool.astype(BF16), pool_scale.reshape(1, -1))
    half = CMP_LEN // 2
    cw = half * HEAD_DIM
    pad_w2 = lambda w2: jnp.pad(w2, ((0, 0), (0, LANE - HEAD_DIM))).astype(BF16)
    kc, vct = _compress(
        kcr, vcr,
        pe_k.reshape(2, cw), w1_k.reshape(2, cw, CMP_HIDDEN).astype(BF16), b1_k.reshape(1, -1), pad_w2(w2_k),
        pe_v.reshape(2, cw), w1_v.reshape(2, cw, CMP_HIDDEN).astype(BF16), b1_v.reshape(1, -1), pad_w2(w2_v))
    return _nsa_attention(qt, qrt, kc, vct, ks, vst, kw, vwt, gates), p


def _moe_layer(x2, g_ffn, router, wg, wu, wd, g_final):
    T, D = x2.shape
    E = N_EXPERTS
    C = T // MOE_CHUNK
    i32 = jnp.int32
    w_router = jnp.pad(router, ((0, 0), (0, LANE - E)))
    h, route, gates, before, total = _router(x2, g_ffn.reshape(1, D), w_router)

    counts = total[0, :E].astype(i32)
    tiles = (counts + MOE_ROW_TILE - 1) // MOE_ROW_TILE
    ends = jnp.cumsum(tiles)
    off = (ends - tiles) * MOE_ROW_TILE
    e1, e2, r1, r2 = route[:, 0], route[:, 1], route[:, 2], route[:, 3]
    slot1 = off[e1] + r1
    slot2 = off[e2] + r2
    first = off[None, :] + before[:, 0, :E].astype(i32)
    after = jnp.concatenate([before[1:, 0, :E], total[:, :E]], axis=0).astype(i32)
    last = off[None, :] + after - 1
    lo_tile = first // MOE_CHUNK
    tile_meta = lo_tile << (2 * MOE_PARTS)
    part_rows = MOE_CHUNK // MOE_PARTS
    for k in range(2 * MOE_PARTS):
        start = lo_tile * MOE_CHUNK + k * part_rows
        touched = (last >= first) & (first < start + part_rows) & (last >= start)
        tile_meta = tile_meta + (touched.astype(i32) << k)
    n_row_tiles = (2 * T) // MOE_ROW_TILE + E + 1 + MOE_CHUNK // MOE_ROW_TILE
    n_used = ends[-1:]
    tile_ids = jnp.minimum(jnp.arange(n_row_tiles, dtype=i32), n_used[0] - 1)
    tile_expert = jnp.minimum(jnp.sum(ends[None, :] <= tile_ids[:, None], axis=1), E - 1).astype(i32)

    zero = jnp.zeros_like(slot1)
    tok = jnp.stack([slot1, slot2, zero, zero, zero, zero, zero, zero])
    tok_rows = tok.reshape(8, C, MOE_CHUNK).transpose(1, 0, 2)
    gz = jnp.zeros((T,), F32)
    gate_rows = jnp.stack([gates[:, 0], gates[:, 1], gz, gz, gz, gz, gz, gz]).reshape(8, C, MOE_CHUNK).transpose(1, 0, 2)
    slot1_rep = jnp.broadcast_to(slot1[:, None], (T, LANE))
    slot2_rep = jnp.broadcast_to(slot2[:, None], (T, LANE))

    xs_lo, xs_hi, gs_lo, gs_hi = _dispatch(tile_meta.T.reshape(-1).astype(i32), h, tok_rows, gate_rows,
                                           n_row_tiles * MOE_ROW_TILE)
    ys = _expert_ffn(tile_expert, n_used.astype(i32), xs_lo, xs_hi, gs_lo, gs_hi,
                     wg.astype(BF16), wu.astype(BF16), wd.astype(BF16))
    return _combine(tile_meta.reshape(-1).astype(i32), x2, ys, slot1_rep, slot2_rep, g_final.reshape(1, D))


def kernel(x, mem, norm_mix, w_in, cmp_pe_k, cmp_w1_k, cmp_b1_k, cmp_w2_k, cmp_pe_v, cmp_w1_v, cmp_b1_v, cmp_w2_v, w_pool, pool_scale, w_out, norm_x, norm_mem, wq_x, wk_x, wv_x, wo_x, norm_ffn, ffn_wg, ffn_wu, ffn_wd, moe_router, moe_wg, moe_wu, moe_wd, norm_final):
    B, S, D = x.shape
    depth = norm_mix.shape[0]
    assert depth == 2, "the final RMSNorm is fused into the expert layer, which must be the last one"
    cos_t, sin_t = _rope_tables(S)
    for layer in range(depth):
        a, p = _mixer_heads(x, norm_mix[layer], w_in[layer],
                            cmp_pe_k[layer], cmp_w1_k[layer], cmp_b1_k[layer], cmp_w2_k[layer],
                            cmp_pe_v[layer], cmp_w1_v[layer], cmp_b1_v[layer], cmp_w2_v[layer],
                            w_pool[layer], pool_scale[layer], cos_t, sin_t)
        wkv = jnp.concatenate([wk_x[layer], wv_x[layer]], axis=1).astype(BF16)
        kx, vx = _xattn_kv(mem, norm_mem[layer].reshape(1, D), wkv)
        wo_mix = w_out[layer].astype(BF16)
        x = _xattn(x, a, p, wo_mix[:NSA_WIDTH], wo_mix[NSA_WIDTH:], norm_x[layer].reshape(1, D),
                   wq_x[layer].astype(BF16), kx, vx, wo_x[layer].astype(BF16))
        x2 = x.reshape(B * S, D)
        j = layer // 2
        if layer % 2 == 0:
            x2 = _ffn(x2, norm_ffn[layer].reshape(1, D), ffn_wg[j].astype(BF16), ffn_wu[j].astype(BF16),
                      ffn_wd[j].astype(BF16))
        else:
            x2 = _moe_layer(x2, norm_ffn[layer], moe_router[j], moe_wg[j], moe_wu[j], moe_wd[j], norm_final)
        x = x2.reshape(B, S, D)
    return x
```

```python
import functools
import math

import numpy as np
import jax
import jax.numpy as jnp
from jax import lax
from jax.experimental import pallas as pl
from jax.experimental.pallas import tpu as pltpu

F32 = jnp.float32
BF16 = jnp.bfloat16

D_MODEL = 1024
HEAD_DIM = 64
N_HEADS = 8
N_KV_HEADS = 2
GROUP = N_HEADS // N_KV_HEADS
NSA_WIDTH = N_HEADS * HEAD_DIM
KV_WIDTH = N_KV_HEADS * HEAD_DIM
N_BRANCH = 3
GATE_ROWS = 16
POOL_WIDTH = 512
POOL_WINDOWS = (2, 4, 8, 16)
POOL_GROUP = 128
POOL_HALO = 16
ROPE_DIM = 16
ROPE_HALF = 8
ROPE_THETA = 500000.0
CMP_LEN = 32
CMP_STRIDE = 16
CMP_HIDDEN = 256
SEL_BLOCK = 64
SEL_TOP_N = 16
N_LOCAL_BLOCKS = 2
WINDOW = 512
X_HEADS = 4
X_HEAD_DIM = 256
N_EXPERTS = 8
EPS = 1e-6

LANE = 128
IN_PACKED = NSA_WIDTH + 6 * KV_WIDTH + LANE + POOL_WIDTH
GATE_OFF = NSA_WIDTH + 6 * KV_WIDTH
POOL_OFF = GATE_OFF + LANE

NEG = -1e30
TINY = float(np.finfo(np.float32).tiny)
VMEM_LIMIT = 56 * 1024 * 1024


def _dot(a, b):
    return jnp.dot(a, b, preferred_element_type=F32)


def _dot_nt(a, b):
    return lax.dot_general(a, b, (((1,), (1,)), ((), ())), preferred_element_type=F32)


def _rms(x, g):
    y = x * lax.rsqrt(jnp.mean(x * x, axis=-1, keepdims=True) + EPS)
    return y * g


def _params(sem, limit=VMEM_LIMIT):
    return pltpu.CompilerParams(dimension_semantics=sem, vmem_limit_bytes=limit)


def _mixer_proj_kernel(x_ref, g_ref, w_ref, cos_ref, sin_ref, wpool_ref, pscale_ref,
                       qt_ref, qrt_ref, kcr_ref, vcr_ref, ks_ref, vst_ref, kw_ref, vwt_ref,
                       gate_ref, p_ref, pool_buf):
    h = _rms(x_ref[0], g_ref[...]).astype(BF16)
    z = _dot(h, w_ref[...])
    tm = z.shape[0]
    cos = cos_ref[...]
    sin = sin_ref[...]
    lane = lax.broadcasted_iota(jnp.int32, (tm, LANE), 1)
    first = (lane & (HEAD_DIM - 1)) < ROPE_HALF
    scale = HEAD_DIM ** -0.5 * math.log2(math.e)

    def rope(xs):
        partner = jnp.where(first, pltpu.roll(xs, LANE - ROPE_HALF, 1), pltpu.roll(xs, ROPE_HALF, 1))
        return xs * cos + partner * sin

    for s in range(NSA_WIDTH // LANE):
        xs = z[:, s * LANE:(s + 1) * LANE]
        for src, ref in ((xs, qt_ref), (rope(xs), qrt_ref)):
            t = (src * scale).T.astype(BF16)
            ref[0, 2 * s] = t[:HEAD_DIM]
            ref[0, 2 * s + 1] = t[HEAD_DIM:]

    def kv_slab(i):
        return z[:, NSA_WIDTH + i * KV_WIDTH:NSA_WIDTH + (i + 1) * KV_WIDTH]

    for slab, ref in ((kv_slab(0), kcr_ref), (kv_slab(1), vcr_ref),
                      (rope(kv_slab(2)), ks_ref), (rope(kv_slab(4)), kw_ref)):
        for gg in range(N_KV_HEADS):
            ref[0, gg] = slab[:, gg * HEAD_DIM:(gg + 1) * HEAD_DIM].astype(ref.dtype)

    for slab, ref in ((kv_slab(3), vst_ref), (kv_slab(5), vwt_ref)):
        t = slab.T.astype(BF16)
        for gg in range(N_KV_HEADS):
            for k in range(tm // LANE):
                ref[0, gg, k] = t[gg * HEAD_DIM:(gg + 1) * HEAD_DIM, k * LANE:(k + 1) * LANE]

    sig_t = jax.nn.sigmoid(z[:, GATE_OFF:GATE_OFF + LANE]).T
    for gg in range(N_KV_HEADS):
        gate_ref[0, gg] = sig_t[gg * GATE_ROWS:(gg + 1) * GATE_ROWS]

    i = pl.program_id(1)

    @pl.when(i == 0)
    def _():
        pool_buf[0:POOL_HALO, :] = jnp.zeros((POOL_HALO, POOL_WIDTH), F32)

    @pl.when(i > 0)
    def _():
        pool_buf[0:POOL_HALO, :] = pool_buf[tm:tm + POOL_HALO, :]

    pool_buf[POOL_HALO:POOL_HALO + tm, :] = z[:, POOL_OFF:POOL_OFF + POOL_WIDTH]
    t1 = i * tm + lax.broadcasted_iota(jnp.int32, (tm, 1), 0) + 1
    for gi, w in enumerate(POOL_WINDOWS):
        cols = slice(gi * POOL_GROUP, (gi + 1) * POOL_GROUP)
        cur = pool_buf[POOL_HALO:POOL_HALO + tm, cols]
        tot = cur
        for k in range(1, w):
            tot = tot + pool_buf[POOL_HALO - k:POOL_HALO - k + tm, cols]
        d = tot / jnp.minimum(t1, w).astype(F32) - cur
        p_ref[0, :, cols] = (_dot(d.astype(BF16), wpool_ref[gi]) * pscale_ref[:, cols]).astype(p_ref.dtype)


def _mixer_proj(x, g, w_packed, cos_t, sin_t, w_pool, pool_scale, tm=1024):
    B, S, D = x.shape
    G = N_KV_HEADS
    sd = jax.ShapeDtypeStruct
    out_shape = (sd((B, N_HEADS, HEAD_DIM, S), BF16), sd((B, N_HEADS, HEAD_DIM, S), BF16),
                 sd((B, G, S, HEAD_DIM), F32), sd((B, G, S, HEAD_DIM), F32),
                 sd((B, G, S, HEAD_DIM), BF16), sd((B, G, S // LANE, HEAD_DIM, LANE), BF16),
                 sd((B, G, S, HEAD_DIM), BF16), sd((B, G, S // LANE, HEAD_DIM, LANE), BF16),
                 sd((B, G, GATE_ROWS, S), F32),
                 sd((B, S, POOL_WIDTH), BF16))
    qspec = pl.BlockSpec((1, N_HEADS, HEAD_DIM, tm), lambda b, i: (b, 0, 0, i))
    kspec = pl.BlockSpec((1, G, tm, HEAD_DIM), lambda b, i: (b, 0, i, 0))
    vspec = pl.BlockSpec((1, G, tm // LANE, HEAD_DIM, LANE), lambda b, i: (b, 0, i, 0, 0))
    out_specs = (qspec, qspec, kspec, kspec, kspec, vspec, kspec, vspec,
                 pl.BlockSpec((1, G, GATE_ROWS, tm), lambda b, i: (b, 0, 0, i)),
                 pl.BlockSpec((1, tm, POOL_WIDTH), lambda b, i: (b, i, 0)))
    return pl.pallas_call(
        _mixer_proj_kernel,
        grid=(B, S // tm),
        in_specs=[pl.BlockSpec((1, tm, D), lambda b, i: (b, i, 0)),
                  pl.BlockSpec((1, D), lambda b, i: (0, 0)),
                  pl.BlockSpec((D, IN_PACKED), lambda b, i: (0, 0)),
                  pl.BlockSpec((tm, LANE), lambda b, i: (i, 0)),
                  pl.BlockSpec((tm, LANE), lambda b, i: (i, 0)),
                  pl.BlockSpec(w_pool.shape, lambda b, i: (0, 0, 0)),
                  pl.BlockSpec((1, POOL_WIDTH), lambda b, i: (0, 0))],
        out_specs=out_specs,
        out_shape=out_shape,
        scratch_shapes=[pltpu.VMEM((tm + POOL_HALO, POOL_WIDTH), F32)],
        compiler_params=_params(("parallel", "arbitrary")),
        name="mixer_proj",
    )(x, g, w_packed, cos_t, sin_t, w_pool, pool_scale)


def _gelu_tanh(x):
    return 0.5 * x * (1.0 + jnp.tanh(math.sqrt(2.0 / math.pi) * (x + 0.044715 * (x * x * x))))


def _compress_kernel(kc_ref, vc_ref, pek_ref, w1k_ref, b1k_ref, w2k_ref,
                     pev_ref, w1v_ref, b1v_ref, w2v_ref, ko_ref, vto_ref):
    def comp(x_ref, pe_ref, w1_ref, b1_ref, w2_ref):
        n = x_ref.shape[2] // CMP_STRIDE
        x = jnp.concatenate([x_ref[0, 0, pl.ds(l, n, stride=CMP_STRIDE), :] for l in range(CMP_STRIDE)], axis=1)
        a = _dot((x + pe_ref[0:1, :]).astype(BF16), w1_ref[0])
        b = _dot((x + pe_ref[1:2, :]).astype(BF16), w1_ref[1])
        pre = a + pltpu.roll(b, n - 1, 0) + b1_ref[...]
        return _dot(_gelu_tanh(pre).astype(BF16), w2_ref[...])

    ko_ref[0, 0] = comp(kc_ref, pek_ref, w1k_ref, b1k_ref, w2k_ref)[:, :HEAD_DIM].astype(ko_ref.dtype)
    vto_ref[0, 0] = comp(vc_ref, pev_ref, w1v_ref, b1v_ref, w2v_ref).T[:HEAD_DIM].astype(vto_ref.dtype)


def _compress(kcr, vcr, pek, w1k, b1k, w2k, pev, w1v, b1v, w2v):
    B, G, S, _ = kcr.shape
    n = S // CMP_STRIDE
    xspec = pl.BlockSpec((1, 1, S, HEAD_DIM), lambda b, g: (b, g, 0, 0))
    full = lambda a: pl.BlockSpec(a.shape, lambda b, g: (0,) * a.ndim)
    return pl.pallas_call(
        _compress_kernel,
        grid=(B, G),
        in_specs=[xspec, xspec, full(pek), full(w1k), full(b1k), full(w2k),
                  full(pev), full(w1v), full(b1v), full(w2v)],
        out_specs=(pl.BlockSpec((1, 1, n, HEAD_DIM), lambda b, g: (b, g, 0, 0)),
                   pl.BlockSpec((1, 1, HEAD_DIM, n), lambda b, g: (b, g, 0, 0))),
        out_shape=(jax.ShapeDtypeStruct((B, G, n, HEAD_DIM), BF16),
                   jax.ShapeDtypeStruct((B, G, HEAD_DIM, n), BF16)),
        compiler_params=_params(("parallel", "parallel")),
        name="compress_kv",
    )(kcr, vcr, pek, w1k, b1k, w2k, pev, w1v, b1v, w2v)


def _nsa_kernel(qt_ref, qrt_ref, kc_ref, vct_ref, ks_ref, vst_ref, kw_ref, vwt_ref, gate_ref,
                covt_ref, o_ref, selbias_ref, *, tq, tk, tk_narrow):
    qt = pl.program_id(2)
    q0 = qt * tq
    n_cmp = kc_ref.shape[2]
    n_blk = covt_ref.shape[0]
    heads = lambda ref: jnp.concatenate([ref[0, hh] for hh in range(GROUP)], axis=1)
    q_t = heads(qt_ref)
    qr_t = heads(qrt_ref)
    t_lane = q0 + lax.broadcasted_iota(jnp.int32, (1, tq), 1)
    per_head = lambda a: [a[:, hh * tq:(hh + 1) * tq] for hh in range(GROUP)]
    all_heads = lambda a: jnp.concatenate([a] * GROUP, axis=1)

    sc = _dot(kc_ref[0, 0], q_t)
    cmp_end = lax.broadcasted_iota(jnp.int32, (n_cmp, 1), 0) * CMP_STRIDE + (CMP_LEN - 1)
    valid = cmp_end <= t_lane
    pcs = []
    for s_h in per_head(sc):
        s_h = jnp.where(valid, s_h, NEG)
        e = jnp.where(valid, jnp.exp2(s_h - jnp.max(s_h, axis=0, keepdims=True)), 0.0)
        pcs.append(e * (1.0 / jnp.maximum(jnp.sum(e, axis=0, keepdims=True), TINY)))
    oc_t = _dot(vct_ref[0, 0], jnp.concatenate(pcs, axis=1).astype(BF16))

    psum = pcs[0] + pcs[1] + pcs[2] + pcs[3]
    p_hi = psum.astype(BF16)
    r1 = psum - p_hi.astype(F32)
    p_mid = r1.astype(BF16)
    p_lo = (r1 - p_mid.astype(F32)).astype(BF16)
    covt = covt_ref[...]
    imp_t = _dot(covt, p_hi) + _dot(covt, p_mid) + _dot(covt, p_lo)

    jb = lax.broadcasted_iota(jnp.int32, (n_blk, tq), 0)
    tb = (q0 + lax.broadcasted_iota(jnp.int32, (n_blk, tq), 1)) // SEL_BLOCK
    dist = tb - jb
    forced = (jb == 0) | ((dist >= 0) & (dist < N_LOCAL_BLOCKS))
    score = jnp.where(jb > tb, -jnp.inf, jnp.where(forced, jnp.inf, imp_t))
    rank = jnp.zeros((n_blk, tq), jnp.int32)
    for jp in range(n_blk):
        rowv = score[jp:jp + 1, :]
        beats = (rowv > score) | ((rowv == score) & (jb > jp))
        rank = rank + jnp.where(beats, 1, 0)
    selbias_ref[...] = jnp.where(rank < min(SEL_TOP_N, n_blk), 0.0, NEG)

    R = GROUP * tq

    def with_ones(v_t):
        extra = jnp.where(lax.broadcasted_iota(jnp.int32, (16, v_t.shape[1]), 0) == 0, 1.0, 0.0)
        return jnp.concatenate([v_t, extra.astype(BF16)], axis=0)

    def sel_chunk(c, carry, width, causal):
        m_i, l_i, acc = carry
        bpc = width // SEL_BLOCK
        vpc = width // LANE
        start = pl.multiple_of(c * width, width)
        k_c = ks_ref[0, 0, pl.ds(start, width), :]
        v_t = jnp.concatenate([vst_ref[0, 0, c * vpc + k] for k in range(vpc)], axis=1)
        rows = [jnp.broadcast_to(selbias_ref[pl.ds(c * bpc + j, 1), :], (SEL_BLOCK, tq)) for j in range(bpc)]
        bias = jnp.concatenate(rows, axis=0)
        if causal:
            kpos = start + lax.broadcasted_iota(jnp.int32, (width, 1), 0)
            bias = jnp.where(kpos <= t_lane, bias, NEG)
        s = _dot(k_c, qr_t) + all_heads(bias)
        m_new = jnp.maximum(m_i, jnp.max(s, axis=0, keepdims=True))
        alpha = jnp.exp2(m_i - m_new)
        pv = _dot(with_ones(v_t), jnp.exp2(s - m_new).astype(BF16))
        return m_new, alpha * l_i + pv[HEAD_DIM:HEAD_DIM + 1], alpha * acc + pv[:HEAD_DIM]

    init = (jnp.full((1, R), NEG, F32), jnp.zeros((1, R), F32), jnp.zeros((HEAD_DIM, R), F32))
    n_wide = q0 // tk
    j_diag = q0 // tk_narrow
    carry = lax.fori_loop(0, n_wide, lambda c, cr: sel_chunk(c, cr, tk, False), init)
    carry = lax.fori_loop(n_wide * (tk // tk_narrow), j_diag, lambda c, cr: sel_chunk(c, cr, tk_narrow, False), carry)
    _, l_s, acc_s = sel_chunk(j_diag, carry, tk_narrow, True)
    os_t = acc_s * (1.0 / jnp.maximum(l_s, TINY))

    wk = WINDOW + tq
    w0 = pl.multiple_of(jnp.maximum(q0 - WINDOW, 0), LANE)
    k_w = kw_ref[0, 0, pl.ds(w0, wk), :]
    v_t = jnp.concatenate([vwt_ref[0, 0, w0 // LANE + k] for k in range(wk // LANE)], axis=1)
    diff = t_lane - (w0 + lax.broadcasted_iota(jnp.int32, (wk, 1), 0))
    bias = jnp.where((diff >= 0) & (diff < WINDOW), 0.0, NEG)
    s = _dot(k_w, qr_t) + all_heads(bias)
    pv = _dot(with_ones(v_t), jnp.exp2(s - jnp.max(s, axis=0, keepdims=True)).astype(BF16))
    ow_t = pv[:HEAD_DIM] * (1.0 / pv[HEAD_DIM:HEAD_DIM + 1])

    gt = gate_ref[0, 0]
    outs = []
    for hh, (a, b, c) in enumerate(zip(per_head(oc_t), per_head(os_t), per_head(ow_t))):
        r = hh * N_BRANCH
        outs.append(gt[r:r + 1] * a + gt[r + 1:r + 2] * b + gt[r + 2:r + 3] * c)
    o_ref[0] = jnp.concatenate(outs, axis=0).T.astype(o_ref.dtype)


def _cover_table(S):
    n_cmp = S // CMP_STRIDE
    n_blk = S // SEL_BLOCK
    cs = np.arange(n_cmp) * CMP_STRIDE
    ss = np.arange(n_blk) * SEL_BLOCK
    cover_t = ((cs[None, :] < ss[:, None] + SEL_BLOCK) & (cs[None, :] + CMP_LEN > ss[:, None]))
    cover_t[:, n_cmp - 1] = False
    return jnp.asarray(cover_t, BF16)


def _nsa_attention(qt, qrt, kc, vct, ks, vst, kw, vwt, gates, tq=256, tk=1024):
    B, _, _, S = qt.shape
    tk = min(tk, S)
    assert tq % LANE == 0 and tk % tq == 0 and S % tk == 0 and S >= WINDOW + tq
    covt = _cover_table(S)
    n_cmp = kc.shape[2]
    n_blk = covt.shape[0]
    qspec = pl.BlockSpec((1, GROUP, HEAD_DIM, tq), lambda b, g, i: (b, g, 0, i))
    kspec = lambda a: pl.BlockSpec((1, 1) + a.shape[2:], lambda b, g, i: (b, g, 0, 0))
    vspec = pl.BlockSpec((1, 1, S // LANE, HEAD_DIM, LANE), lambda b, g, i: (b, g, 0, 0, 0))
    return pl.pallas_call(
        functools.partial(_nsa_kernel, tq=tq, tk=tk, tk_narrow=min(tk, 2 * tq)),
        grid=(B, N_KV_HEADS, S // tq),
        in_specs=[qspec, qspec,
                  pl.BlockSpec((1, 1, n_cmp, HEAD_DIM), lambda b, g, i: (b, g, 0, 0)),
                  pl.BlockSpec((1, 1, HEAD_DIM, n_cmp), lambda b, g, i: (b, g, 0, 0)),
                  kspec(ks), vspec, kspec(kw), vspec,
                  pl.BlockSpec((1, 1, GATE_ROWS, tq), lambda b, g, i: (b, g, 0, i)),
                  pl.BlockSpec(covt.shape, lambda b, g, i: (0, 0))],
        out_specs=pl.BlockSpec((1, tq, GROUP * HEAD_DIM), lambda b, g, i: (b, i, g)),
        out_shape=jax.ShapeDtypeStruct((B, S, NSA_WIDTH), BF16),
        scratch_shapes=[pltpu.VMEM((n_blk, tq), F32)],
        compiler_params=_params(("parallel", "parallel", "parallel")),
        name="nsa_attention",
    )(qt, qrt, kc, vct, ks, vst, kw, vwt, gates, covt)


def _xkv_kernel(m_ref, g_ref, w_ref, k_ref, v_ref):
    h = _rms(m_ref[0], g_ref[...]).astype(BF16)
    kv = _dot(h, w_ref[...])
    d = k_ref.shape[2]
    k_ref[0] = kv[:, :d].astype(k_ref.dtype)
    v_ref[0] = kv[:, d:].astype(v_ref.dtype)


def _xattn_kv(mem, g, wkv):
    B, M, D = mem.shape
    ospec = pl.BlockSpec((1, M, D), lambda b: (b, 0, 0))
    oshape = jax.ShapeDtypeStruct((B, M, D), BF16)
    return pl.pallas_call(
        _xkv_kernel,
        grid=(B,),
        in_specs=[pl.BlockSpec((1, M, D), lambda b: (b, 0, 0)),
                  pl.BlockSpec((1, D), lambda b: (0, 0)),
                  pl.BlockSpec(wkv.shape, lambda b: (0, 0))],
        out_specs=(ospec, ospec),
        out_shape=(oshape, oshape),
        compiler_params=_params(("parallel",)),
        name="xattn_kv",
    )(mem, g, wkv)


def _xattn_kernel(x_ref, a_ref, p_ref, wa_ref, wp_ref, g_ref, wq_ref, k_ref, v_ref, wo_ref, o_ref):
    x = x_ref[0] + _dot(a_ref[0], wa_ref[...]) + _dot(p_ref[0], wp_ref[...])
    h = _rms(x, g_ref[...]).astype(BF16)
    scale = X_HEAD_DIM ** -0.5
    q = (_dot(h, wq_ref[...]) * scale).astype(BF16)
    outs = []
    for hd in range(X_HEADS):
        sl = slice(hd * X_HEAD_DIM, (hd + 1) * X_HEAD_DIM)
        s = _dot_nt(q[:, sl], k_ref[0, :, sl])
        e = jnp.exp(s - jnp.max(s, axis=-1, keepdims=True))
        p = e / jnp.sum(e, axis=-1, keepdims=True)
        outs.append(_dot(p.astype(BF16), v_ref[0, :, sl]).astype(BF16))
    o = jnp.concatenate(outs, axis=-1)
    o_ref[0] = x + _dot(o, wo_ref[...])


def _xattn(x, a, p, wa, wp, g, wq, kx, vx, wo, tm=1024):
    B, S, D = x.shape
    M = kx.shape[1]
    row = lambda n: pl.BlockSpec((1, tm, n), lambda b, i: (b, i, 0))
    full = lambda w: pl.BlockSpec(w.shape, lambda b, i: (0, 0))
    mspec = pl.BlockSpec((1, M, D), lambda b, i: (b, 0, 0))
    return pl.pallas_call(
        _xattn_kernel,
        grid=(B, S // tm),
        in_specs=[row(D), row(a.shape[2]), row(p.shape[2]), full(wa), full(wp),
                  full(g), full(wq), mspec, mspec, full(wo)],
        out_specs=row(D),
        out_shape=jax.ShapeDtypeStruct((B, S, D), F32),
        compiler_params=_params(("parallel", "parallel")),
        name="out_proj_xattn",
    )(x, a, p, wa, wp, g, wq, kx, vx, wo)


def _silu(x):
    return x * jax.nn.sigmoid(x)


def _ffn_kernel(x_ref, g_ref, wg_ref, wu_ref, wd_ref, o_ref):
    x = x_ref[...]
    h = _rms(x, g_ref[...]).astype(BF16)
    act = _silu(_dot(h, wg_ref[...])) * _dot(h, wu_ref[...])
    o_ref[...] = x + _dot(act.astype(BF16), wd_ref[...])


def _ffn(x2, g, wg, wu, wd, tm=512):
    T, D = x2.shape
    resident = lambda w: pl.BlockSpec(w.shape, lambda i: (0, 0), pipeline_mode=pl.Buffered(1))
    return pl.pallas_call(
        _ffn_kernel,
        grid=(T // tm,),
        in_specs=[pl.BlockSpec((tm, D), lambda i: (i, 0)),
                  pl.BlockSpec((1, D), lambda i: (0, 0)),
                  resident(wg), resident(wu), resident(wd)],
        out_specs=pl.BlockSpec((tm, D), lambda i: (i, 0)),
        out_shape=jax.ShapeDtypeStruct((T, D), F32),
        compiler_params=_params(("parallel",)),
        name="dense_swiglu",
    )(x2, g, wg, wu, wd)


MOE_CHUNK = 512
MOE_ROW_TILE = 512
assert MOE_ROW_TILE % MOE_CHUNK == 0


MOE_PARTS = 2


def _tile_meta(m):
    return m >> (2 * MOE_PARTS), [(m & (1 << k)) != 0 for k in range(2 * MOE_PARTS)]


def _router_kernel(x_ref, g_ref, wr_ref, tri_ref, h_ref, route_ref, gate_ref, before_ref, total_ref, run_ref):
    c = pl.program_id(0)

    @pl.when(c == 0)
    def _():
        run_ref[...] = jnp.zeros_like(run_ref)

    hf = _rms(x_ref[...], g_ref[...])
    h_ref[...] = hf.astype(BF16)
    logits = jnp.dot(hf, wr_ref[...], preferred_element_type=F32, precision=lax.Precision.HIGHEST)
    lane = lax.broadcasted_iota(jnp.int32, logits.shape, 1)
    logits = jnp.where(lane < N_EXPERTS, logits, -jnp.inf)
    v1 = jnp.max(logits, axis=-1, keepdims=True)
    i1 = jnp.min(jnp.where(logits == v1, lane, LANE), axis=-1, keepdims=True)
    rest = jnp.where(lane == i1, -jnp.inf, logits)
    v2 = jnp.max(rest, axis=-1, keepdims=True)
    i2 = jnp.min(jnp.where(rest == v2, lane, LANE), axis=-1, keepdims=True)
    e2 = jnp.exp(v2 - v1)
    den = 1.0 + e2
    gate_ref[...] = jnp.where(lane == 0, 1.0 / den, jnp.where(lane == 1, e2 / den, 0.0))

    onehot = jnp.where((lane == i1) | (lane == i2), 1.0, 0.0)
    run = run_ref[...]
    rank = run + _dot(tri_ref[...], onehot.astype(BF16))
    r1 = jnp.sum(jnp.where(lane == i1, rank, 0.0), axis=-1, keepdims=True).astype(jnp.int32)
    r2 = jnp.sum(jnp.where(lane == i2, rank, 0.0), axis=-1, keepdims=True).astype(jnp.int32)
    route_ref[...] = jnp.where(lane == 0, i1, jnp.where(lane == 1, i2,
                               jnp.where(lane == 2, r1, jnp.where(lane == 3, r2, 0))))
    before_ref[0] = run
    run = run + jnp.sum(onehot, axis=0, keepdims=True)
    run_ref[...] = run
    total_ref[...] = run


def _router(x2, g, w_router):
    T, D = x2.shape
    C = T // MOE_CHUNK
    tri = jnp.asarray(np.tril(np.ones((MOE_CHUNK, MOE_CHUNK), np.float32), -1), BF16)
    row = lambda n: pl.BlockSpec((MOE_CHUNK, n), lambda c: (c, 0))
    full = lambda a: pl.BlockSpec(a.shape, lambda c: (0, 0))
    return pl.pallas_call(
        _router_kernel,
        grid=(C,),
        in_specs=[row(D), full(g), full(w_router), full(tri)],
        out_specs=(row(D), row(LANE), row(LANE),
                   pl.BlockSpec((1, 1, LANE), lambda c: (c, 0, 0)),
                   pl.BlockSpec((1, LANE), lambda c: (0, 0))),
        out_shape=(jax.ShapeDtypeStruct((T, D), BF16),
                   jax.ShapeDtypeStruct((T, LANE), jnp.int32),
                   jax.ShapeDtypeStruct((T, LANE), F32),
                   jax.ShapeDtypeStruct((C, 1, LANE), F32),
                   jax.ShapeDtypeStruct((1, LANE), F32)),
        scratch_shapes=[pltpu.VMEM((1, LANE), F32)],
        compiler_params=_params(("arbitrary",)),
        name="moe_router",
    )(x2, g, w_router, tri)


def _dispatch_kernel(a_ref, h_ref, tok_ref, gate_ref, zlo_ref, zhi_ref, zglo_ref, zghi_ref,
                     lo_ref, hi_ref, glo_ref, ghi_ref):
    del zlo_ref, zhi_ref, zglo_ref, zghi_ref
    e = pl.program_id(0)
    c = pl.program_id(1)
    idx = e * pl.num_programs(1) + c
    a, touched = _tile_meta(a_ref[idx])
    first = (c == 0) | (a != _tile_meta(a_ref[jnp.maximum(idx - 1, 0)])[0])

    @pl.when(first)
    def _():
        lo_ref[...] = jnp.zeros_like(lo_ref)
        hi_ref[...] = jnp.zeros_like(hi_ref)
        glo_ref[...] = jnp.zeros_like(glo_ref)
        ghi_ref[...] = jnp.zeros_like(ghi_ref)

    rows_per_part = MOE_CHUNK // MOE_PARTS

    def scatter(tile, part, x_ref, g_ref):
        tok = tok_ref[0]
        gts = gate_ref[0]
        r0 = part * rows_per_part
        rows = tile * MOE_CHUNK + r0 + lax.broadcasted_iota(jnp.int32, (rows_per_part, MOE_CHUNK), 0)
        c1 = tok[0:1] == rows
        c2 = tok[1:2] == rows
        x_ref[r0:r0 + rows_per_part, :] += _dot(jnp.where(c1 | c2, 1.0, 0.0).astype(BF16),
                                                h_ref[...]).astype(x_ref.dtype)
        g_ref[r0:r0 + rows_per_part, :] += jnp.sum(jnp.where(c1, gts[0:1], 0.0) + jnp.where(c2, gts[1:2], 0.0),
                                                   axis=1, keepdims=True)

    for k, (x_ref, g_ref) in enumerate(((lo_ref, glo_ref), (hi_ref, ghi_ref))):
        for part in range(MOE_PARTS):
            pl.when(touched[k * MOE_PARTS + part])(functools.partial(scatter, a + k, part, x_ref, g_ref))


def _dispatch(a_ec, h, tok_rows, gate_rows, n_slots):
    T, D = h.shape
    C = T // MOE_CHUNK
    zx = jnp.zeros((n_slots, D), BF16)
    zg = jnp.zeros((n_slots, 1), F32)
    any_spec = pl.BlockSpec(memory_space=pl.ANY)
    lo = lambda n: pl.BlockSpec((MOE_CHUNK, n), lambda e, c, a: (a[e * C + c] >> (2 * MOE_PARTS), 0))
    hi = lambda n: pl.BlockSpec((MOE_CHUNK, n), lambda e, c, a: ((a[e * C + c] >> (2 * MOE_PARTS)) + 1, 0))
    return pl.pallas_call(
        _dispatch_kernel,
        grid_spec=pltpu.PrefetchScalarGridSpec(
            num_scalar_prefetch=1,
            grid=(N_EXPERTS, C),
            in_specs=[pl.BlockSpec((MOE_CHUNK, D), lambda e, c, a: (c, 0)),
                      pl.BlockSpec((1, 8, MOE_CHUNK), lambda e, c, a: (c, 0, 0)),
                      pl.BlockSpec((1, 8, MOE_CHUNK), lambda e, c, a: (c, 0, 0)),
                      any_spec, any_spec, any_spec, any_spec],
            out_specs=(lo(D), hi(D), lo(1), hi(1))),
        out_shape=(jax.ShapeDtypeStruct(zx.shape, BF16), jax.ShapeDtypeStruct(zx.shape, BF16),
                   jax.ShapeDtypeStruct(zg.shape, F32), jax.ShapeDtypeStruct(zg.shape, F32)),
        input_output_aliases={4: 0, 5: 1, 6: 2, 7: 3},
        compiler_params=_params(("arbitrary", "arbitrary")),
        name="moe_dispatch",
    )(a_ec, h, tok_rows, gate_rows, zx, zx, zg, zg)


def _expert_ffn_kernel(te_ref, nu_ref, lo_ref, hi_ref, glo_ref, ghi_ref, wg_ref, wu_ref, wd_ref, o_ref):
    del te_ref
    used = pl.program_id(0) < nu_ref[0]

    @pl.when(used)
    def _():
        x = lo_ref[...] + hi_ref[...]
        act = _silu(_dot(x, wg_ref[0])) * _dot(x, wu_ref[0])
        o_ref[...] = (_dot(act.astype(BF16), wd_ref[0]) * (glo_ref[...] + ghi_ref[...])).astype(o_ref.dtype)

    @pl.when(jnp.logical_not(used))
    def _():
        o_ref[...] = jnp.zeros_like(o_ref)


def _expert_ffn(tile_expert, n_used, xs_lo, xs_hi, gs_lo, gs_hi, wg, wu, wd, tr=256):
    N, D = xs_lo.shape
    sub = MOE_ROW_TILE // tr
    tile_expert = jnp.repeat(tile_expert, sub)
    n_used = n_used * sub
    row = lambda n: pl.BlockSpec((tr, n), lambda i, te, nu: (i, 0))
    expert = lambda w: pl.BlockSpec((1,) + w.shape[1:], lambda i, te, nu: (te[i], 0, 0))
    return pl.pallas_call(
        _expert_ffn_kernel,
        grid_spec=pltpu.PrefetchScalarGridSpec(
            num_scalar_prefetch=2,
            grid=(N // tr,),
            in_specs=[row(D), row(D), row(1), row(1), expert(wg), expert(wu), expert(wd)],
            out_specs=row(D)),
        out_shape=jax.ShapeDtypeStruct((N, D), BF16),
        compiler_params=_params(("arbitrary",)),
        name="moe_expert_ffn",
    )(tile_expert, n_used, xs_lo, xs_hi, gs_lo, gs_hi, wg, wu, wd)


def _combine_kernel(a_ref, x_ref, ylo_ref, yhi_ref, s1_ref, s2_ref, gf_ref, o_ref, acc_ref):
    c = pl.program_id(0)
    e = pl.program_id(1)

    @pl.when(e == 0)
    def _():
        acc_ref[...] = jnp.zeros_like(acc_ref)

    a, touched = _tile_meta(a_ref[c * pl.num_programs(1) + e])
    cols_per_part = MOE_CHUNK // MOE_PARTS

    def gather(tile, part, y_ref):
        s1 = s1_ref[...]
        s2 = s2_ref[...]
        c0 = part * cols_per_part
        picks = []
        for j in range(cols_per_part // LANE):
            cols = tile * MOE_CHUNK + c0 + j * LANE + lax.broadcasted_iota(jnp.int32, (MOE_CHUNK, LANE), 1)
            picks.append(jnp.where((s1 == cols) | (s2 == cols), 1.0, 0.0).astype(BF16))
        acc_ref[...] += _dot(jnp.concatenate(picks, axis=1), y_ref[c0:c0 + cols_per_part, :])

    for k, y_ref in enumerate((ylo_ref, yhi_ref)):
        for part in range(MOE_PARTS):
            pl.when(touched[k * MOE_PARTS + part])(functools.partial(gather, a + k, part, y_ref))

    @pl.when(e == pl.num_programs(1) - 1)
    def _():
        o_ref[...] = _rms(x_ref[...] + acc_ref[...], gf_ref[...])


def _combine(a_ce, x2, ys, slot1_rep, slot2_rep, g_final):
    T, D = x2.shape
    C = T // MOE_CHUNK
    E = N_EXPERTS
    return pl.pallas_call(
        _combine_kernel,
        grid_spec=pltpu.PrefetchScalarGridSpec(
            num_scalar_prefetch=1,
            grid=(C, E),
            in_specs=[pl.BlockSpec((MOE_CHUNK, D), lambda c, e, a: (c, 0)),
                      pl.BlockSpec((MOE_CHUNK, D), lambda c, e, a: (a[c * E + e] >> (2 * MOE_PARTS), 0)),
                      pl.BlockSpec((MOE_CHUNK, D), lambda c, e, a: ((a[c * E + e] >> (2 * MOE_PARTS)) + 1, 0)),
                      pl.BlockSpec((MOE_CHUNK, LANE), lambda c, e, a: (c, 0)),
                      pl.BlockSpec((MOE_CHUNK, LANE), lambda c, e, a: (c, 0)),
                      pl.BlockSpec((1, D), lambda c, e, a: (0, 0))],
            out_specs=pl.BlockSpec((MOE_CHUNK, D), lambda c, e, a: (c, 0)),
            scratch_shapes=[pltpu.VMEM((MOE_CHUNK, D), F32)]),
        out_shape=jax.ShapeDtypeStruct((T, D), F32),
        compiler_params=_params(("arbitrary", "arbitrary")),
        name="moe_combine_final_norm",
    )(a_ce, x2, ys, ys, slot1_rep, slot2_rep, g_final)


def _rope_tables(S):
    inv = ROPE_THETA ** (-jnp.arange(ROPE_HALF, dtype=F32) / ROPE_HALF)
    ang = jnp.arange(S, dtype=F32)[:, None] * inv[None, :]
    cos, sin = jnp.cos(ang), jnp.sin(ang)
    pad = HEAD_DIM - ROPE_DIM
    cos_h = jnp.concatenate([cos, cos, jnp.ones((S, pad), F32)], axis=-1)
    sin_h = jnp.concatenate([-sin, sin, jnp.zeros((S, pad), F32)], axis=-1)
    reps = LANE // HEAD_DIM
    return jnp.tile(cos_h, (1, reps)), jnp.tile(sin_h, (1, reps))


def _pack_w_in(w):
    per_group = GROUP * N_BRANCH
    gates = w[:, GATE_OFF:GATE_OFF + N_KV_HEADS * per_group].reshape(-1, N_KV_HEADS, per_group)
    gates = jnp.pad(gates, ((0, 0), (0, LANE // GATE_ROWS - N_KV_HEADS), (0, GATE_ROWS - per_group)))
    gates = gates.reshape(-1, LANE)
    return jnp.concatenate([w[:, :GATE_OFF], gates, w[:, GATE_OFF + N_KV_HEADS * per_group:]], axis=1).astype(BF16)


def _mixer_heads(x, norm_g, w_in, pe_k, w1_k, b1_k, w2_k, pe_v, w1_v, b1_v, w2_v,
                 w_pool, pool_scale, cos_t, sin_t):
    B, S, D = x.shape
    qt, qrt, kcr, vcr, ks, vst, kw, vwt, gates, p = _mixer_proj(
        x, norm_g.reshape(1, D), _pack_w_in(w_in), cos_t, sin_t, w_pool.astype(BF16), pool_scale.reshape(1, -1))
    half = CMP_LEN // 2
    cw = half * HEAD_DIM
    pad_w2 = lambda w2: jnp.pad(w2, ((0, 0), (0, LANE - HEAD_DIM))).astype(BF16)
    kc, vct = _compress(
        kcr, vcr,
        pe_k.reshape(2, cw), w1_k.reshape(2, cw, CMP_HIDDEN).astype(BF16), b1_k.reshape(1, -1), pad_w2(w2_k),
        pe_v.reshape(2, cw), w1_v.reshape(2, cw, CMP_HIDDEN).astype(BF16), b1_v.reshape(1, -1), pad_w2(w2_v))
    return _nsa_attention(qt, qrt, kc, vct, ks, vst, kw, vwt, gates), p


def _moe_layer(x2, g_ffn, router, wg, wu, wd, g_final):
    T, D = x2.shape
    E = N_EXPERTS
    C = T // MOE_CHUNK
    i32 = jnp.int32
    w_router = jnp.pad(router, ((0, 0), (0, LANE - E)))
    h, route, gates, before, total = _router(x2, g_ffn.reshape(1, D), w_router)

    counts = total[0, :E].astype(i32)
    tiles = (counts + MOE_ROW_TILE - 1) // MOE_ROW_TILE
    ends = jnp.cumsum(tiles)
    off = (ends - tiles) * MOE_ROW_TILE
    e1, e2, r1, r2 = route[:, 0], route[:, 1], route[:, 2], route[:, 3]
    slot1 = off[e1] + r1
    slot2 = off[e2] + r2
    first = off[None, :] + before[:, 0, :E].astype(i32)
    after = jnp.concatenate([before[1:, 0, :E], total[:, :E]], axis=0).astype(i32)
    last = off[None, :] + after - 1
    lo_tile = first // MOE_CHUNK
    tile_meta = lo_tile << (2 * MOE_PARTS)
    part_rows = MOE_CHUNK // MOE_PARTS
    for k in range(2 * MOE_PARTS):
        start = lo_tile * MOE_CHUNK + k * part_rows
        touched = (last >= first) & (first < start + part_rows) & (last >= start)
        tile_meta = tile_meta + (touched.astype(i32) << k)
    n_row_tiles = (2 * T) // MOE_ROW_TILE + E + 1 + MOE_CHUNK // MOE_ROW_TILE
    n_used = ends[-1:]
    tile_ids = jnp.minimum(jnp.arange(n_row_tiles, dtype=i32), n_used[0] - 1)
    tile_expert = jnp.minimum(jnp.sum(ends[None, :] <= tile_ids[:, None], axis=1), E - 1).astype(i32)

    zero = jnp.zeros_like(slot1)
    tok = jnp.stack([slot1, slot2, zero, zero, zero, zero, zero, zero])
    tok_rows = tok.reshape(8, C, MOE_CHUNK).transpose(1, 0, 2)
    gz = jnp.zeros((T,), F32)
    gate_rows = jnp.stack([gates[:, 0], gates[:, 1], gz, gz, gz, gz, gz, gz]).reshape(8, C, MOE_CHUNK).transpose(1, 0, 2)
    slot1_rep = jnp.broadcast_to(slot1[:, None], (T, LANE))
    slot2_rep = jnp.broadcast_to(slot2[:, None], (T, LANE))

    xs_lo, xs_hi, gs_lo, gs_hi = _dispatch(tile_meta.T.reshape(-1).astype(i32), h, tok_rows, gate_rows,
                                           n_row_tiles * MOE_ROW_TILE)
    ys = _expert_ffn(tile_expert, n_used.astype(i32), xs_lo, xs_hi, gs_lo, gs_hi,
                     wg.astype(BF16), wu.astype(BF16), wd.astype(BF16))
    return _combine(tile_meta.reshape(-1).astype(i32), x2, ys, slot1_rep, slot2_rep, g_final.reshape(1, D))


def kernel(x, mem, norm_mix, w_in, cmp_pe_k, cmp_w1_k, cmp_b1_k, cmp_w2_k, cmp_pe_v, cmp_w1_v, cmp_b1_v, cmp_w2_v, w_pool, pool_scale, w_out, norm_x, norm_mem, wq_x, wk_x, wv_x, wo_x, norm_ffn, ffn_wg, ffn_wu, ffn_wd, moe_router, moe_wg, moe_wu, moe_wd, norm_final):
    B, S, D = x.shape
    depth = norm_mix.shape[0]
    assert depth == 2, "the final RMSNorm is fused into the expert layer, which must be the last one"
    cos_t, sin_t = _rope_tables(S)
    for layer in range(depth):
        a, p = _mixer_heads(x, norm_mix[layer], w_in[layer],
                            cmp_pe_k[layer], cmp_w1_k[layer], cmp_b1_k[layer], cmp_w2_k[layer],
                            cmp_pe_v[layer], cmp_w1_v[layer], cmp_b1_v[layer], cmp_w2_v[layer],
                            w_pool[layer], pool_scale[layer], cos_t, sin_t)
        wkv = jnp.concatenate([wk_x[layer], wv_x[layer]], axis=1).astype(BF16)
        kx, vx = _xattn_kv(mem, norm_mem[layer].reshape(1, D), wkv)
        wo_mix = w_out[layer].astype(BF16)
        x = _xattn(x, a, p, wo_mix[:NSA_WIDTH], wo_mix[NSA_WIDTH:], norm_x[layer].reshape(1, D),
                   wq_x[layer].astype(BF16), kx, vx, wo_x[layer].astype(BF16))
        x2 = x.reshape(B * S, D)
        j = layer // 2
        if layer % 2 == 0:
            x2 = _ffn(x2, norm_ffn[layer].reshape(1, D), ffn_wg[j].astype(BF16), ffn_wu[j].astype(BF16),
                      ffn_wd[j].astype(BF16))
        else:
            x2 = _moe_layer(x2, norm_ffn[layer], moe_router[j], moe_wg[j], moe_wu[j], moe_wd[j], norm_final)
        x = x2.reshape(B, S, D)
    return x
```

```python
import functools
import math

import numpy as np
import jax
import jax.numpy as jnp
from jax import lax
from jax.experimental import pallas as pl
from jax.experimental.pallas import tpu as pltpu

F32 = jnp.float32
BF16 = jnp.bfloat16

D_MODEL = 1024
HEAD_DIM = 64
N_HEADS = 8
N_KV_HEADS = 2
GROUP = N_HEADS // N_KV_HEADS
NSA_WIDTH = N_HEADS * HEAD_DIM
KV_WIDTH = N_KV_HEADS * HEAD_DIM
N_BRANCH = 3
GATE_ROWS = 16
POOL_WIDTH = 512
POOL_WINDOWS = (2, 4, 8, 16)
POOL_GROUP = 128
POOL_HALO = 16
ROPE_DIM = 16
ROPE_HALF = 8
ROPE_THETA = 500000.0
CMP_LEN = 32
CMP_STRIDE = 16
CMP_HIDDEN = 256
SEL_BLOCK = 64
SEL_TOP_N = 16
N_LOCAL_BLOCKS = 2
WINDOW = 512
X_HEADS = 4
X_HEAD_DIM = 256
N_EXPERTS = 8
EPS = 1e-6

LANE = 128
IN_PACKED = NSA_WIDTH + 6 * KV_WIDTH + LANE + POOL_WIDTH
GATE_OFF = NSA_WIDTH + 6 * KV_WIDTH
POOL_OFF = GATE_OFF + LANE

NEG = -1e30
TINY = float(np.finfo(np.float32).tiny)
VMEM_LIMIT = 56 * 1024 * 1024


def _dot(a, b):
    return jnp.dot(a, b, preferred_element_type=F32)


def _dot_nt(a, b):
    return lax.dot_general(a, b, (((1,), (1,)), ((), ())), preferred_element_type=F32)


def _rms(x, g):
    y = x * lax.rsqrt(jnp.mean(x * x, axis=-1, keepdims=True) + EPS)
    return y * g


def _params(sem, limit=VMEM_LIMIT):
    return pltpu.CompilerParams(dimension_semantics=sem, vmem_limit_bytes=limit)


def _mixer_proj_kernel(x_ref, g_ref, w_ref, cos_ref, sin_ref, wpool_ref, pscale_ref,
                       qt_ref, qrt_ref, kcr_ref, vcr_ref, ks_ref, vst_ref, kw_ref, vwt_ref,
                       gate_ref, p_ref, pool_buf):
    h = _rms(x_ref[0], g_ref[...]).astype(BF16)
    z = _dot(h, w_ref[...])
    tm = z.shape[0]
    cos = cos_ref[...]
    sin = sin_ref[...]
    lane = lax.broadcasted_iota(jnp.int32, (tm, LANE), 1)
    first = (lane & (HEAD_DIM - 1)) < ROPE_HALF
    scale = HEAD_DIM ** -0.5 * math.log2(math.e)

    def rope(xs):
        partner = jnp.where(first, pltpu.roll(xs, LANE - ROPE_HALF, 1), pltpu.roll(xs, ROPE_HALF, 1))
        return xs * cos + partner * sin

    for s in range(NSA_WIDTH // LANE):
        xs = z[:, s * LANE:(s + 1) * LANE]
        for src, ref in ((xs, qt_ref), (rope(xs), qrt_ref)):
            t = (src * scale).T.astype(BF16)
            ref[0, 2 * s] = t[:HEAD_DIM]
            ref[0, 2 * s + 1] = t[HEAD_DIM:]

    def kv_slab(i):
        return z[:, NSA_WIDTH + i * KV_WIDTH:NSA_WIDTH + (i + 1) * KV_WIDTH]

    for slab, ref in ((kv_slab(0), kcr_ref), (kv_slab(1), vcr_ref),
                      (rope(kv_slab(2)), ks_ref), (rope(kv_slab(4)), kw_ref)):
        for gg in range(N_KV_HEADS):
            ref[0, gg] = slab[:, gg * HEAD_DIM:(gg + 1) * HEAD_DIM].astype(ref.dtype)

    for slab, ref in ((kv_slab(3), vst_ref), (kv_slab(5), vwt_ref)):
        t = slab.T.astype(BF16)
        for gg in range(N_KV_HEADS):
            for k in range(tm // LANE):
                ref[0, gg, k] = t[gg * HEAD_DIM:(gg + 1) * HEAD_DIM, k * LANE:(k + 1) * LANE]

    sig_t = jax.nn.sigmoid(z[:, GATE_OFF:GATE_OFF + LANE]).T
    for gg in range(N_KV_HEADS):
        gate_ref[0, gg] = sig_t[gg * GATE_ROWS:(gg + 1) * GATE_ROWS]

    i = pl.program_id(1)

    @pl.when(i == 0)
    def _():
        pool_buf[0:POOL_HALO, :] = jnp.zeros((POOL_HALO, POOL_WIDTH), F32)

    @pl.when(i > 0)
    def _():
        pool_buf[0:POOL_HALO, :] = pool_buf[tm:tm + POOL_HALO, :]

    pool_buf[POOL_HALO:POOL_HALO + tm, :] = z[:, POOL_OFF:POOL_OFF + POOL_WIDTH]
    t1 = i * tm + lax.broadcasted_iota(jnp.int32, (tm, 1), 0) + 1
    for gi, w in enumerate(POOL_WINDOWS):
        cols = slice(gi * POOL_GROUP, (gi + 1) * POOL_GROUP)
        cur = pool_buf[POOL_HALO:POOL_HALO + tm, cols]
        tot = cur
        for k in range(1, w):
            tot = tot + pool_buf[POOL_HALO - k:POOL_HALO - k + tm, cols]
        d = tot / jnp.minimum(t1, w).astype(F32) - cur
        p_ref[0, :, cols] = (_dot(d.astype(BF16), wpool_ref[gi]) * pscale_ref[:, cols]).astype(p_ref.dtype)


def _mixer_proj(x, g, w_packed, cos_t, sin_t, w_pool, pool_scale, tm=1024):
    B, S, D = x.shape
    G = N_KV_HEADS
    sd = jax.ShapeDtypeStruct
    out_shape = (sd((B, N_HEADS, HEAD_DIM, S), BF16), sd((B, N_HEADS, HEAD_DIM, S), BF16),
                 sd((B, G, S, HEAD_DIM), F32), sd((B, G, S, HEAD_DIM), F32),
                 sd((B, G, S, HEAD_DIM), BF16), sd((B, G, S // LANE, HEAD_DIM, LANE), BF16),
                 sd((B, G, S, HEAD_DIM), BF16), sd((B, G, S // LANE, HEAD_DIM, LANE), BF16),
                 sd((B, G, GATE_ROWS, S), F32),
                 sd((B, S, POOL_WIDTH), BF16))
    qspec = pl.BlockSpec((1, N_HEADS, HEAD_DIM, tm), lambda b, i: (b, 0, 0, i))
    kspec = pl.BlockSpec((1, G, tm, HEAD_DIM), lambda b, i: (b, 0, i, 0))
    vspec = pl.BlockSpec((1, G, tm // LANE, HEAD_DIM, LANE), lambda b, i: (b, 0, i, 0, 0))
    out_specs = (qspec, qspec, kspec, kspec, kspec, vspec, kspec, vspec,
                 pl.BlockSpec((1, G, GATE_ROWS, tm), lambda b, i: (b, 0, 0, i)),
                 pl.BlockSpec((1, tm, POOL_WIDTH), lambda b, i: (b, i, 0)))
    return pl.pallas_call(
        _mixer_proj_kernel,
        grid=(B, S // tm),
        in_specs=[pl.BlockSpec((1, tm, D), lambda b, i: (b, i, 0)),
                  pl.BlockSpec((1, D), lambda b, i: (0, 0)),
                  pl.BlockSpec((D, IN_PACKED), lambda b, i: (0, 0)),
                  pl.BlockSpec((tm, LANE), lambda b, i: (i, 0)),
                  pl.BlockSpec((tm, LANE), lambda b, i: (i, 0)),
                  pl.BlockSpec(w_pool.shape, lambda b, i: (0, 0, 0)),
                  pl.BlockSpec((1, POOL_WIDTH), lambda b, i: (0, 0))],
        out_specs=out_specs,
        out_shape=out_shape,
        scratch_shapes=[pltpu.VMEM((tm + POOL_HALO, POOL_WIDTH), F32)],
        compiler_params=_params(("parallel", "arbitrary")),
        name="mixer_proj",
    )(x, g, w_packed, cos_t, sin_t, w_pool, pool_scale)


def _gelu_tanh(x):
    return 0.5 * x * (1.0 + jnp.tanh(math.sqrt(2.0 / math.pi) * (x + 0.044715 * (x * x * x))))


def _compress_kernel(kc_ref, vc_ref, pek_ref, w1k_ref, b1k_ref, w2k_ref,
                     pev_ref, w1v_ref, b1v_ref, w2v_ref, ko_ref, vto_ref):
    def comp(x_ref, pe_ref, w1_ref, b1_ref, w2_ref):
        n = x_ref.shape[2] // CMP_STRIDE
        x = jnp.concatenate([x_ref[0, 0, pl.ds(l, n, stride=CMP_STRIDE), :] for l in range(CMP_STRIDE)], axis=1)
        a = _dot((x + pe_ref[0:1, :]).astype(BF16), w1_ref[0])
        b = _dot((x + pe_ref[1:2, :]).astype(BF16), w1_ref[1])
        pre = a + pltpu.roll(b, n - 1, 0) + b1_ref[...]
        return _dot(_gelu_tanh(pre).astype(BF16), w2_ref[...])

    ko_ref[0, 0] = comp(kc_ref, pek_ref, w1k_ref, b1k_ref, w2k_ref)[:, :HEAD_DIM].astype(ko_ref.dtype)
    vto_ref[0, 0] = comp(vc_ref, pev_ref, w1v_ref, b1v_ref, w2v_ref).T[:HEAD_DIM].astype(vto_ref.dtype)


def _compress(kcr, vcr, pek, w1k, b1k, w2k, pev, w1v, b1v, w2v):
    B, G, S, _ = kcr.shape
    n = S // CMP_STRIDE
    xspec = pl.BlockSpec((1, 1, S, HEAD_DIM), lambda b, g: (b, g, 0, 0))
    full = lambda a: pl.BlockSpec(a.shape, lambda b, g: (0,) * a.ndim)
    return pl.pallas_call(
        _compress_kernel,
        grid=(B, G),
        in_specs=[xspec, xspec, full(pek), full(w1k), full(b1k), full(w2k),
                  full(pev), full(w1v), full(b1v), full(w2v)],
        out_specs=(pl.BlockSpec((1, 1, n, HEAD_DIM), lambda b, g: (b, g, 0, 0)),
                   pl.BlockSpec((1, 1, HEAD_DIM, n), lambda b, g: (b, g, 0, 0))),
        out_shape=(jax.ShapeDtypeStruct((B, G, n, HEAD_DIM), BF16),
                   jax.ShapeDtypeStruct((B, G, HEAD_DIM, n), BF16)),
        compiler_params=_params(("parallel", "parallel")),
        name="compress_kv",
    )(kcr, vcr, pek, w1k, b1k, w2k, pev, w1v, b1v, w2v)


def _nsa_kernel(qt_ref, qrt_ref, kc_ref, vct_ref, ks_ref, vst_ref, kw_ref, vwt_ref, gate_ref,
                covt_ref, o_ref, selbias_ref, *, tq, tk, tk_narrow):
    qt = pl.program_id(2)
    q0 = qt * tq
    n_cmp = kc_ref.shape[2]
    n_blk = covt_ref.shape[0]
    heads = lambda ref: jnp.concatenate([ref[0, hh] for hh in range(GROUP)], axis=1)
    q_t = heads(qt_ref)
    qr_t = heads(qrt_ref)
    t_lane = q0 + lax.broadcasted_iota(jnp.int32, (1, tq), 1)
    per_head = lambda a: [a[:, hh * tq:(hh + 1) * tq] for hh in range(GROUP)]
    all_heads = lambda a: jnp.concatenate([a] * GROUP, axis=1)

    sc = _dot(kc_ref[0, 0], q_t)
    cmp_end = lax.broadcasted_iota(jnp.int32, (n_cmp, 1), 0) * CMP_STRIDE + (CMP_LEN - 1)
    valid = cmp_end <= t_lane
    pcs = []
    for s_h in per_head(sc):
        s_h = jnp.where(valid, s_h, NEG)
        e = jnp.where(valid, jnp.exp2(s_h - jnp.max(s_h, axis=0, keepdims=True)), 0.0)
        pcs.append(e * (1.0 / jnp.maximum(jnp.sum(e, axis=0, keepdims=True), TINY)))
    oc_t = _dot(vct_ref[0, 0], jnp.concatenate(pcs, axis=1).astype(BF16))

    psum = pcs[0] + pcs[1] + pcs[2] + pcs[3]
    p_hi = psum.astype(BF16)
    r1 = psum - p_hi.astype(F32)
    p_mid = r1.astype(BF16)
    p_lo = (r1 - p_mid.astype(F32)).astype(BF16)
    covt = covt_ref[...]
    imp_t = _dot(covt, p_hi) + _dot(covt, p_mid) + _dot(covt, p_lo)

    jb = lax.broadcasted_iota(jnp.int32, (n_blk, tq), 0)
    tb = (q0 + lax.broadcasted_iota(jnp.int32, (n_blk, tq), 1)) // SEL_BLOCK
    dist = tb - jb
    forced = (jb == 0) | ((dist >= 0) & (dist < N_LOCAL_BLOCKS))
    score = jnp.where(jb > tb, -jnp.inf, jnp.where(forced, jnp.inf, imp_t))
    rank = jnp.zeros((n_blk, tq), jnp.int32)
    for jp in range(n_blk):
        rowv = score[jp:jp + 1, :]
        beats = (rowv > score) | ((rowv == score) & (jb > jp))
        rank = rank + jnp.where(beats, 1, 0)
    selbias_ref[...] = jnp.where(rank < min(SEL_TOP_N, n_blk), 0.0, NEG)

    R = GROUP * tq

    def with_ones(v_t):
        extra = jnp.where(lax.broadcasted_iota(jnp.int32, (16, v_t.shape[1]), 0) == 0, 1.0, 0.0)
        return jnp.concatenate([v_t, extra.astype(BF16)], axis=0)

    def sel_chunk(c, carry, width, causal):
        m_i, l_i, acc = carry
        bpc = width // SEL_BLOCK
        vpc = width // LANE
        start = pl.multiple_of(c * width, width)
        k_c = ks_ref[0, 0, pl.ds(start, width), :]
        v_t = jnp.concatenate([vst_ref[0, 0, c * vpc + k] for k in range(vpc)], axis=1)
        rows = [jnp.broadcast_to(selbias_ref[pl.ds(c * bpc + j, 1), :], (SEL_BLOCK, tq)) for j in range(bpc)]
        bias = jnp.concatenate(rows, axis=0)
        if causal:
            kpos = start + lax.broadcasted_iota(jnp.int32, (width, 1), 0)
            bias = jnp.where(kpos <= t_lane, bias, NEG)
        s = _dot(k_c, qr_t) + all_heads(bias)
        m_new = jnp.maximum(m_i, jnp.max(s, axis=0, keepdims=True))
        alpha = jnp.exp2(m_i - m_new)
        pv = _dot(with_ones(v_t), jnp.exp2(s - m_new).astype(BF16))
        return m_new, alpha * l_i + pv[HEAD_DIM:HEAD_DIM + 1], alpha * acc + pv[:HEAD_DIM]

    init = (jnp.full((1, R), NEG, F32), jnp.zeros((1, R), F32), jnp.zeros((HEAD_DIM, R), F32))
    n_wide = q0 // tk
    j_diag = q0 // tk_narrow
    carry = lax.fori_loop(0, n_wide, lambda c, cr: sel_chunk(c, cr, tk, False), init)
    carry = lax.fori_loop(n_wide * (tk // tk_narrow), j_diag, lambda c, cr: sel_chunk(c, cr, tk_narrow, False), carry)
    _, l_s, acc_s = sel_chunk(j_diag, carry, tk_narrow, True)
    os_t = acc_s * (1.0 / jnp.maximum(l_s, TINY))

    wk = WINDOW + tq
    w0 = pl.multiple_of(jnp.maximum(q0 - WINDOW, 0), LANE)
    k_w = kw_ref[0, 0, pl.ds(w0, wk), :]
    v_t = jnp.concatenate([vwt_ref[0, 0, w0 // LANE + k] for k in range(wk // LANE)], axis=1)
    diff = t_lane - (w0 + lax.broadcasted_iota(jnp.int32, (wk, 1), 0))
    bias = jnp.where((diff >= 0) & (diff < WINDOW), 0.0, NEG)
    s = _dot(k_w, qr_t) + all_heads(bias)
    pv = _dot(with_ones(v_t), jnp.exp2(s - jnp.max(s, axis=0, keepdims=True)).astype(BF16))
    ow_t = pv[:HEAD_DIM] * (1.0 / pv[HEAD_DIM:HEAD_DIM + 1])

    gt = gate_ref[0, 0]
    outs = []
    for hh, (a, b, c) in enumerate(zip(per_head(oc_t), per_head(os_t), per_head(ow_t))):
        r = hh * N_BRANCH
        outs.append(gt[r:r + 1] * a + gt[r + 1:r + 2] * b + gt[r + 2:r + 3] * c)
    o_ref[0] = jnp.concatenate(outs, axis=0).T.astype(o_ref.dtype)


def _cover_table(S):
    n_cmp = S // CMP_STRIDE
    n_blk = S // SEL_BLOCK
    cs = np.arange(n_cmp) * CMP_STRIDE
    ss = np.arange(n_blk) * SEL_BLOCK
    cover_t = ((cs[None, :] < ss[:, None] + SEL_BLOCK) & (cs[None, :] + CMP_LEN > ss[:, None]))
    cover_t[:, n_cmp - 1] = False
    return jnp.asarray(cover_t, BF16)


def _nsa_attention(qt, qrt, kc, vct, ks, vst, kw, vwt, gates, tq=256, tk=1024):
    B, _, _, S = qt.shape
    tk = min(tk, S)
    assert tq % LANE == 0 and tk % tq == 0 and S % tk == 0 and S >= WINDOW + tq
    covt = _cover_table(S)
    n_cmp = kc.shape[2]
    n_blk = covt.shape[0]
    qspec = pl.BlockSpec((1, GROUP, HEAD_DIM, tq), lambda b, g, i: (b, g, 0, i))
    kspec = lambda a: pl.BlockSpec((1, 1) + a.shape[2:], lambda b, g, i: (b, g, 0, 0))
    vspec = pl.BlockSpec((1, 1, S // LANE, HEAD_DIM, LANE), lambda b, g, i: (b, g, 0, 0, 0))
    return pl.pallas_call(
        functools.partial(_nsa_kernel, tq=tq, tk=tk, tk_narrow=min(tk, 2 * tq)),
        grid=(B, N_KV_HEADS, S // tq),
        in_specs=[qspec, qspec,
                  pl.BlockSpec((1, 1, n_cmp, HEAD_DIM), lambda b, g, i: (b, g, 0, 0)),
                  pl.BlockSpec((1, 1, HEAD_DIM, n_cmp), lambda b, g, i: (b, g, 0, 0)),
                  kspec(ks), vspec, kspec(kw), vspec,
                  pl.BlockSpec((1, 1, GATE_ROWS, tq), lambda b, g, i: (b, g, 0, i)),
                  pl.BlockSpec(covt.shape, lambda b, g, i: (0, 0))],
        out_specs=pl.BlockSpec((1, tq, GROUP * HEAD_DIM), lambda b, g, i: (b, i, g)),
        out_shape=jax.ShapeDtypeStruct((B, S, NSA_WIDTH), BF16),
        scratch_shapes=[pltpu.VMEM((n_blk, tq), F32)],
        compiler_params=_params(("parallel", "parallel", "parallel")),
        name="nsa_attention",
    )(qt, qrt, kc, vct, ks, vst, kw, vwt, gates, covt)


def _xkv_kernel(m_ref, g_ref, w_ref, k_ref, v_ref):
    h = _rms(m_ref[0], g_ref[...]).astype(BF16)
    kv = _dot(h, w_ref[...])
    d = k_ref.shape[2]
    k_ref[0] = kv[:, :d].astype(k_ref.dtype)
    v_ref[0] = kv[:, d:].astype(v_ref.dtype)


def _xattn_kv(mem, g, wkv):
    B, M, D = mem.shape
    ospec = pl.BlockSpec((1, M, D), lambda b: (b, 0, 0))
    oshape = jax.ShapeDtypeStruct((B, M, D), BF16)
    return pl.pallas_call(
        _xkv_kernel,
        grid=(B,),
        in_specs=[pl.BlockSpec((1, M, D), lambda b: (b, 0, 0)),
                  pl.BlockSpec((1, D), lambda b: (0, 0)),
                  pl.BlockSpec(wkv.shape, lambda b: (0, 0))],
        out_specs=(ospec, ospec),
        out_shape=(oshape, oshape),
        compiler_params=_params(("parallel",)),
        name="xattn_kv",
    )(mem, g, wkv)


def _xattn_kernel(x_ref, a_ref, p_ref, wa_ref, wp_ref, g_ref, wq_ref, k_ref, v_ref, wo_ref, o_ref):
    x = x_ref[0] + _dot(a_ref[0], wa_ref[...]) + _dot(p_ref[0], wp_ref[...])
    h = _rms(x, g_ref[...]).astype(BF16)
    scale = X_HEAD_DIM ** -0.5
    q = (_dot(h, wq_ref[...]) * scale).astype(BF16)
    outs = []
    for hd in range(X_HEADS):
        sl = slice(hd * X_HEAD_DIM, (hd + 1) * X_HEAD_DIM)
        s = _dot_nt(q[:, sl], k_ref[0, :, sl])
        e = jnp.exp(s - jnp.max(s, axis=-1, keepdims=True))
        p = e / jnp.sum(e, axis=-1, keepdims=True)
        outs.append(_dot(p.astype(BF16), v_ref[0, :, sl]).astype(BF16))
    o = jnp.concatenate(outs, axis=-1)
    o_ref[0] = x + _dot(o, wo_ref[...])


def _xattn(x, a, p, wa, wp, g, wq, kx, vx, wo, tm=1024):
    B, S, D = x.shape
    M = kx.shape[1]
    row = lambda n: pl.BlockSpec((1, tm, n), lambda b, i: (b, i, 0))
    full = lambda w: pl.BlockSpec(w.shape, lambda b, i: (0, 0))
    mspec = pl.BlockSpec((1, M, D), lambda b, i: (b, 0, 0))
    return pl.pallas_call(
        _xattn_kernel,
        grid=(B, S // tm),
        in_specs=[row(D), row(a.shape[2]), row(p.shape[2]), full(wa), full(wp),
                  full(g), full(wq), mspec, mspec, full(wo)],
        out_specs=row(D),
        out_shape=jax.ShapeDtypeStruct((B, S, D), F32),
        compiler_params=_params(("parallel", "parallel")),
        name="out_proj_xattn",
    )(x, a, p, wa, wp, g, wq, kx, vx, wo)


def _silu(x):
    return x * jax.nn.sigmoid(x)


def _ffn_kernel(x_ref, g_ref, wg_ref, wu_ref, wd_ref, o_ref):
    x = x_ref[...]
    h = _rms(x, g_ref[...]).astype(BF16)
    act = _silu(_dot(h, wg_ref[...])) * _dot(h, wu_ref[...])
    o_ref[...] = x + _dot(act.astype(BF16), wd_ref[...])


def _ffn(x2, g, wg, wu, wd, tm=512):
    T, D = x2.shape
    resident = lambda w: pl.BlockSpec(w.shape, lambda i: (0, 0), pipeline_mode=pl.Buffered(1))
    return pl.pallas_call(
        _ffn_kernel,
        grid=(T // tm,),
        in_specs=[pl.BlockSpec((tm, D), lambda i: (i, 0)),
                  pl.BlockSpec((1, D), lambda i: (0, 0)),
                  resident(wg), resident(wu), resident(wd)],
        out_specs=pl.BlockSpec((tm, D), lambda i: (i, 0)),
        out_shape=jax.ShapeDtypeStruct((T, D), F32),
        compiler_params=_params(("parallel",)),
        name="dense_swiglu",
    )(x2, g, wg, wu, wd)


MOE_CHUNK = 512
MOE_ROW_TILE = 512
assert MOE_ROW_TILE % MOE_CHUNK == 0


MOE_PARTS = 2


def _tile_meta(m):
    return m >> (2 * MOE_PARTS), [(m & (1 << k)) != 0 for k in range(2 * MOE_PARTS)]


def _router_kernel(x_ref, g_ref, wr_ref, tri_ref, h_ref, route_ref, gate_ref, before_ref, total_ref, run_ref):
    c = pl.program_id(0)

    @pl.when(c == 0)
    def _():
        run_ref[...] = jnp.zeros_like(run_ref)

    hf = _rms(x_ref[...], g_ref[...])
    h_ref[...] = hf.astype(BF16)
    logits = jnp.dot(hf, wr_ref[...], preferred_element_type=F32, precision=lax.Precision.HIGHEST)
    lane = lax.broadcasted_iota(jnp.int32, logits.shape, 1)
    logits = jnp.where(lane < N_EXPERTS, logits, -jnp.inf)
    v1 = jnp.max(logits, axis=-1, keepdims=True)
    i1 = jnp.min(jnp.where(logits == v1, lane, LANE), axis=-1, keepdims=True)
    rest = jnp.where(lane == i1, -jnp.inf, logits)
    v2 = jnp.max(rest, axis=-1, keepdims=True)
    i2 = jnp.min(jnp.where(rest == v2, lane, LANE), axis=-1, keepdims=True)
    e2 = jnp.exp(v2 - v1)
    den = 1.0 + e2
    gate_ref[0] = jnp.where(lane == 0, 1.0 / den, jnp.where(lane == 1, e2 / den, 0.0)).T[:8]

    onehot = jnp.where((lane == i1) | (lane == i2), 1.0, 0.0)
    run = run_ref[...]
    rank = run + _dot(tri_ref[...], onehot.astype(BF16))
    r1 = jnp.sum(jnp.where(lane == i1, rank, 0.0), axis=-1, keepdims=True).astype(jnp.int32)
    r2 = jnp.sum(jnp.where(lane == i2, rank, 0.0), axis=-1, keepdims=True).astype(jnp.int32)
    route = jnp.where(lane == 0, i1, jnp.where(lane == 1, i2, jnp.where(lane == 2, r1, jnp.where(lane == 3, r2, 0))))
    route_ref[0] = route.T[:8]
    before_ref[0] = run
    run = run + jnp.sum(onehot, axis=0, keepdims=True)
    run_ref[...] = run
    total_ref[...] = run


def _router(x2, g, w_router):
    T, D = x2.shape
    C = T // MOE_CHUNK
    tri = jnp.asarray(np.tril(np.ones((MOE_CHUNK, MOE_CHUNK), np.float32), -1), BF16)
    row = lambda n: pl.BlockSpec((MOE_CHUNK, n), lambda c: (c, 0))
    full = lambda a: pl.BlockSpec(a.shape, lambda c: (0, 0))
    return pl.pallas_call(
        _router_kernel,
        grid=(C,),
        in_specs=[row(D), full(g), full(w_router), full(tri)],
        out_specs=(row(D),
                   pl.BlockSpec((1, 8, MOE_CHUNK), lambda c: (c, 0, 0)),
                   pl.BlockSpec((1, 8, MOE_CHUNK), lambda c: (c, 0, 0)),
                   pl.BlockSpec((1, 1, LANE), lambda c: (c, 0, 0)),
                   pl.BlockSpec((1, LANE), lambda c: (0, 0))),
        out_shape=(jax.ShapeDtypeStruct((T, D), BF16),
                   jax.ShapeDtypeStruct((C, 8, MOE_CHUNK), jnp.int32),
                   jax.ShapeDtypeStruct((C, 8, MOE_CHUNK), F32),
                   jax.ShapeDtypeStruct((C, 1, LANE), F32),
                   jax.ShapeDtypeStruct((1, LANE), F32)),
        scratch_shapes=[pltpu.VMEM((1, LANE), F32)],
        compiler_params=_params(("arbitrary",)),
        name="moe_router",
    )(x2, g, w_router, tri)


def _dispatch_kernel(a_ref, h_ref, tok_ref, gate_ref, zlo_ref, zhi_ref, zglo_ref, zghi_ref,
                     lo_ref, hi_ref, glo_ref, ghi_ref):
    del zlo_ref, zhi_ref, zglo_ref, zghi_ref
    e = pl.program_id(0)
    c = pl.program_id(1)
    idx = e * pl.num_programs(1) + c
    a, touched = _tile_meta(a_ref[idx])
    first = (c == 0) | (a != _tile_meta(a_ref[jnp.maximum(idx - 1, 0)])[0])

    @pl.when(first)
    def _():
        lo_ref[...] = jnp.zeros_like(lo_ref)
        hi_ref[...] = jnp.zeros_like(hi_ref)
        glo_ref[...] = jnp.zeros_like(glo_ref)
        ghi_ref[...] = jnp.zeros_like(ghi_ref)

    rows_per_part = MOE_CHUNK // MOE_PARTS

    def scatter(tile, part, x_ref, g_ref):
        tok = tok_ref[0]
        gts = gate_ref[0]
        r0 = part * rows_per_part
        rows = tile * MOE_CHUNK + r0 + lax.broadcasted_iota(jnp.int32, (rows_per_part, MOE_CHUNK), 0)
        c1 = tok[0:1] == rows
        c2 = tok[1:2] == rows
        x_ref[r0:r0 + rows_per_part, :] += _dot(jnp.where(c1 | c2, 1.0, 0.0).astype(BF16),
                                                h_ref[...]).astype(x_ref.dtype)
        g_ref[r0:r0 + rows_per_part, :] += jnp.sum(jnp.where(c1, gts[0:1], 0.0) + jnp.where(c2, gts[1:2], 0.0),
                                                   axis=1, keepdims=True)

    for k, (x_ref, g_ref) in enumerate(((lo_ref, glo_ref), (hi_ref, ghi_ref))):
        for part in range(MOE_PARTS):
            pl.when(touched[k * MOE_PARTS + part])(functools.partial(scatter, a + k, part, x_ref, g_ref))


def _dispatch(a_ec, h, tok_rows, gate_rows, n_slots):
    T, D = h.shape
    C = T // MOE_CHUNK
    zx = jnp.zeros((n_slots, D), BF16)
    zg = jnp.zeros((n_slots, 1), F32)
    any_spec = pl.BlockSpec(memory_space=pl.ANY)
    lo = lambda n: pl.BlockSpec((MOE_CHUNK, n), lambda e, c, a: (a[e * C + c] >> (2 * MOE_PARTS), 0))
    hi = lambda n: pl.BlockSpec((MOE_CHUNK, n), lambda e, c, a: ((a[e * C + c] >> (2 * MOE_PARTS)) + 1, 0))
    return pl.pallas_call(
        _dispatch_kernel,
        grid_spec=pltpu.PrefetchScalarGridSpec(
            num_scalar_prefetch=1,
            grid=(N_EXPERTS, C),
            in_specs=[pl.BlockSpec((MOE_CHUNK, D), lambda e, c, a: (c, 0)),
                      pl.BlockSpec((1, 8, MOE_CHUNK), lambda e, c, a: (c, 0, 0)),
                      pl.BlockSpec((1, 8, MOE_CHUNK), lambda e, c, a: (c, 0, 0)),
                      any_spec, any_spec, any_spec, any_spec],
            out_specs=(lo(D), hi(D), lo(1), hi(1))),
        out_shape=(jax.ShapeDtypeStruct(zx.shape, BF16), jax.ShapeDtypeStruct(zx.shape, BF16),
                   jax.ShapeDtypeStruct(zg.shape, F32), jax.ShapeDtypeStruct(zg.shape, F32)),
        input_output_aliases={4: 0, 5: 1, 6: 2, 7: 3},
        compiler_params=_params(("arbitrary", "arbitrary")),
        name="moe_dispatch",
    )(a_ec, h, tok_rows, gate_rows, zx, zx, zg, zg)


def _expert_ffn_kernel(te_ref, nu_ref, lo_ref, hi_ref, glo_ref, ghi_ref, wg_ref, wu_ref, wd_ref, o_ref):
    del te_ref
    used = pl.program_id(0) < nu_ref[0]

    @pl.when(used)
    def _():
        x = lo_ref[...] + hi_ref[...]
        act = _silu(_dot(x, wg_ref[0])) * _dot(x, wu_ref[0])
        o_ref[...] = (_dot(act.astype(BF16), wd_ref[0]) * (glo_ref[...] + ghi_ref[...])).astype(o_ref.dtype)

    @pl.when(jnp.logical_not(used))
    def _():
        o_ref[...] = jnp.zeros_like(o_ref)


def _expert_ffn(tile_expert, n_used, xs_lo, xs_hi, gs_lo, gs_hi, wg, wu, wd, tr=256):
    N, D = xs_lo.shape
    sub = MOE_ROW_TILE // tr
    tile_expert = jnp.repeat(tile_expert, sub)
    n_used = n_used * sub
    row = lambda n: pl.BlockSpec((tr, n), lambda i, te, nu: (i, 0))
    expert = lambda w: pl.BlockSpec((1,) + w.shape[1:], lambda i, te, nu: (te[i], 0, 0))
    return pl.pallas_call(
        _expert_ffn_kernel,
        grid_spec=pltpu.PrefetchScalarGridSpec(
            num_scalar_prefetch=2,
            grid=(N // tr,),
            in_specs=[row(D), row(D), row(1), row(1), expert(wg), expert(wu), expert(wd)],
            out_specs=row(D)),
        out_shape=jax.ShapeDtypeStruct((N, D), BF16),
        compiler_params=_params(("arbitrary",)),
        name="moe_expert_ffn",
    )(tile_expert, n_used, xs_lo, xs_hi, gs_lo, gs_hi, wg, wu, wd)


def _combine_kernel(a_ref, x_ref, ylo_ref, yhi_ref, s1_ref, s2_ref, gf_ref, o_ref, acc_ref):
    c = pl.program_id(0)
    e = pl.program_id(1)

    @pl.when(e == 0)
    def _():
        acc_ref[...] = jnp.zeros_like(acc_ref)

    a, touched = _tile_meta(a_ref[c * pl.num_programs(1) + e])
    cols_per_part = MOE_CHUNK // MOE_PARTS

    def gather(tile, part, y_ref):
        s1 = s1_ref[...]
        s2 = s2_ref[...]
        c0 = part * cols_per_part
        picks = []
        for j in range(cols_per_part // LANE):
            cols = tile * MOE_CHUNK + c0 + j * LANE + lax.broadcasted_iota(jnp.int32, (MOE_CHUNK, LANE), 1)
            picks.append(jnp.where((s1 == cols) | (s2 == cols), 1.0, 0.0).astype(BF16))
        acc_ref[...] += _dot(jnp.concatenate(picks, axis=1), y_ref[c0:c0 + cols_per_part, :])

    for k, y_ref in enumerate((ylo_ref, yhi_ref)):
        for part in range(MOE_PARTS):
            pl.when(touched[k * MOE_PARTS + part])(functools.partial(gather, a + k, part, y_ref))

    @pl.when(e == pl.num_programs(1) - 1)
    def _():
        o_ref[...] = _rms(x_ref[...] + acc_ref[...], gf_ref[...])


def _combine(a_ce, x2, ys, slot1_rep, slot2_rep, g_final):
    T, D = x2.shape
    C = T // MOE_CHUNK
    E = N_EXPERTS
    return pl.pallas_call(
        _combine_kernel,
        grid_spec=pltpu.PrefetchScalarGridSpec(
            num_scalar_prefetch=1,
            grid=(C, E),
            in_specs=[pl.BlockSpec((MOE_CHUNK, D), lambda c, e, a: (c, 0)),
                      pl.BlockSpec((MOE_CHUNK, D), lambda c, e, a: (a[c * E + e] >> (2 * MOE_PARTS), 0)),
                      pl.BlockSpec((MOE_CHUNK, D), lambda c, e, a: ((a[c * E + e] >> (2 * MOE_PARTS)) + 1, 0)),
                      pl.BlockSpec((MOE_CHUNK, LANE), lambda c, e, a: (c, 0)),
                      pl.BlockSpec((MOE_CHUNK, LANE), lambda c, e, a: (c, 0)),
                      pl.BlockSpec((1, D), lambda c, e, a: (0, 0))],
            out_specs=pl.BlockSpec((MOE_CHUNK, D), lambda c, e, a: (c, 0)),
            scratch_shapes=[pltpu.VMEM((MOE_CHUNK, D), F32)]),
        out_shape=jax.ShapeDtypeStruct((T, D), F32),
        compiler_params=_params(("arbitrary", "arbitrary")),
        name="moe_combine_final_norm",
    )(a_ce, x2, ys, ys, slot1_rep, slot2_rep, g_final)


def _rope_tables(S):
    inv = ROPE_THETA ** (-jnp.arange(ROPE_HALF, dtype=F32) / ROPE_HALF)
    ang = jnp.arange(S, dtype=F32)[:, None] * inv[None, :]
    cos, sin = jnp.cos(ang), jnp.sin(ang)
    pad = HEAD_DIM - ROPE_DIM
    cos_h = jnp.concatenate([cos, cos, jnp.ones((S, pad), F32)], axis=-1)
    sin_h = jnp.concatenate([-sin, sin, jnp.zeros((S, pad), F32)], axis=-1)
    reps = LANE // HEAD_DIM
    return jnp.tile(cos_h, (1, reps)), jnp.tile(sin_h, (1, reps))


def _pack_w_in(w):
    per_group = GROUP * N_BRANCH
    gates = w[:, GATE_OFF:GATE_OFF + N_KV_HEADS * per_group].reshape(-1, N_KV_HEADS, per_group)
    gates = jnp.pad(gates, ((0, 0), (0, LANE // GATE_ROWS - N_KV_HEADS), (0, GATE_ROWS - per_group)))
    gates = gates.reshape(-1, LANE)
    return jnp.concatenate([w[:, :GATE_OFF], gates, w[:, GATE_OFF + N_KV_HEADS * per_group:]], axis=1).astype(BF16)


def _mixer_heads(x, norm_g, w_in, pe_k, w1_k, b1_k, w2_k, pe_v, w1_v, b1_v, w2_v,
                 w_pool, pool_scale, cos_t, sin_t):
    B, S, D = x.shape
    qt, qrt, kcr, vcr, ks, vst, kw, vwt, gates, p = _mixer_proj(
        x, norm_g.reshape(1, D), _pack_w_in(w_in), cos_t, sin_t, w_pool.astype(BF16), pool_scale.reshape(1, -1))
    half = CMP_LEN // 2
    cw = half * HEAD_DIM
    pad_w2 = lambda w2: jnp.pad(w2, ((0, 0), (0, LANE - HEAD_DIM))).astype(BF16)
    kc, vct = _compress(
        kcr, vcr,
        pe_k.reshape(2, cw), w1_k.reshape(2, cw, CMP_HIDDEN).astype(BF16), b1_k.reshape(1, -1), pad_w2(w2_k),
        pe_v.reshape(2, cw), w1_v.reshape(2, cw, CMP_HIDDEN).astype(BF16), b1_v.reshape(1, -1), pad_w2(w2_v))
    return _nsa_attention(qt, qrt, kc, vct, ks, vst, kw, vwt, gates), p


def _moe_layer(x2, g_ffn, router, wg, wu, wd, g_final):
    T, D = x2.shape
    E = N_EXPERTS
    C = T // MOE_CHUNK
    i32 = jnp.int32
    w_router = jnp.pad(router, ((0, 0), (0, LANE - E)))
    h, route, gate_rows, before, total = _router(x2, g_ffn.reshape(1, D), w_router)

    counts = total[0, :E].astype(i32)
    tiles = (counts + MOE_ROW_TILE - 1) // MOE_ROW_TILE
    ends = jnp.cumsum(tiles)
    off = (ends - tiles) * MOE_ROW_TILE
    e1, e2, r1, r2 = route[:, 0], route[:, 1], route[:, 2], route[:, 3]
    off_of = lambda e: sum(jnp.where(e == k, off[k], 0) for k in range(E))
    slot1 = off_of(e1) + r1
    slot2 = off_of(e2) + r2
    first = off[None, :] + before[:, 0, :E].astype(i32)
    after = jnp.concatenate([before[1:, 0, :E], total[:, :E]], axis=0).astype(i32)
    last = off[None, :] + after - 1
    lo_tile = first // MOE_CHUNK
    tile_meta = lo_tile << (2 * MOE_PARTS)
    part_rows = MOE_CHUNK // MOE_PARTS
    for k in range(2 * MOE_PARTS):
        start = lo_tile * MOE_CHUNK + k * part_rows
        touched = (last >= first) & (first < start + part_rows) & (last >= start)
        tile_meta = tile_meta + (touched.astype(i32) << k)
    n_row_tiles = (2 * T) // MOE_ROW_TILE + E + 1 + MOE_CHUNK // MOE_ROW_TILE
    n_used = ends[-1:]
    tile_ids = jnp.minimum(jnp.arange(n_row_tiles, dtype=i32), n_used[0] - 1)
    tile_expert = jnp.minimum(jnp.sum(ends[None, :] <= tile_ids[:, None], axis=1), E - 1).astype(i32)

    zero = jnp.zeros_like(slot1)
    tok_rows = jnp.stack([slot1, slot2, zero, zero, zero, zero, zero, zero], axis=1)
    slot1_rep = jnp.broadcast_to(slot1.reshape(T, 1), (T, LANE))
    slot2_rep = jnp.broadcast_to(slot2.reshape(T, 1), (T, LANE))

    xs_lo, xs_hi, gs_lo, gs_hi = _dispatch(tile_meta.T.reshape(-1).astype(i32), h, tok_rows, gate_rows,
                                           n_row_tiles * MOE_ROW_TILE)
    ys = _expert_ffn(tile_expert, n_used.astype(i32), xs_lo, xs_hi, gs_lo, gs_hi,
                     wg.astype(BF16), wu.astype(BF16), wd.astype(BF16))
    return _combine(tile_meta.reshape(-1).astype(i32), x2, ys, slot1_rep, slot2_rep, g_final.reshape(1, D))


def kernel(x, mem, norm_mix, w_in, cmp_pe_k, cmp_w1_k, cmp_b1_k, cmp_w2_k, cmp_pe_v, cmp_w1_v, cmp_b1_v, cmp_w2_v, w_pool, pool_scale, w_out, norm_x, norm_mem, wq_x, wk_x, wv_x, wo_x, norm_ffn, ffn_wg, ffn_wu, ffn_wd, moe_router, moe_wg, moe_wu, moe_wd, norm_final):
    B, S, D = x.shape
    depth = norm_mix.shape[0]
    assert depth == 2, "the final RMSNorm is fused into the expert layer, which must be the last one"
    cos_t, sin_t = _rope_tables(S)
    for layer in range(depth):
        a, p = _mixer_heads(x, norm_mix[layer], w_in[layer],
                            cmp_pe_k[layer], cmp_w1_k[layer], cmp_b1_k[layer], cmp_w2_k[layer],
                            cmp_pe_v[layer], cmp_w1_v[layer], cmp_b1_v[layer], cmp_w2_v[layer],
                            w_pool[layer], pool_scale[layer], cos_t, sin_t)
        wkv = jnp.concatenate([wk_x[layer], wv_x[layer]], axis=1).astype(BF16)
        kx, vx = _xattn_kv(mem, norm_mem[layer].reshape(1, D), wkv)
        wo_mix = w_out[layer].astype(BF16)
        x = _xattn(x, a, p, wo_mix[:NSA_WIDTH], wo_mix[NSA_WIDTH:], norm_x[layer].reshape(1, D),
                   wq_x[layer].astype(BF16), kx, vx, wo_x[layer].astype(BF16))
        x2 = x.reshape(B * S, D)
        j = layer // 2
        if layer % 2 == 0:
            x2 = _ffn(x2, norm_ffn[layer].reshape(1, D), ffn_wg[j].astype(BF16), ffn_wu[j].astype(BF16),
                      ffn_wd[j].astype(BF16))
        else:
            x2 = _moe_layer(x2, norm_ffn[layer], moe_router[j], moe_wg[j], moe_wu[j], moe_wd[j], norm_final)
        x = x2.reshape(B, S, D)
    return x
```

```python
import functools
import math

import numpy as np
import jax
import jax.numpy as jnp
from jax import lax
from jax.experimental import pallas as pl
from jax.experimental.pallas import tpu as pltpu

F32 = jnp.float32
BF16 = jnp.bfloat16

D_MODEL = 1024
HEAD_DIM = 64
N_HEADS = 8
N_KV_HEADS = 2
GROUP = N_HEADS // N_KV_HEADS
NSA_WIDTH = N_HEADS * HEAD_DIM
KV_WIDTH = N_KV_HEADS * HEAD_DIM
N_BRANCH = 3
GATE_ROWS = 16
POOL_WIDTH = 512
POOL_WINDOWS = (2, 4, 8, 16)
POOL_GROUP = 128
POOL_HALO = 16
ROPE_DIM = 16
ROPE_HALF = 8
ROPE_THETA = 500000.0
CMP_LEN = 32
CMP_STRIDE = 16
CMP_HIDDEN = 256
SEL_BLOCK = 64
SEL_TOP_N = 16
N_LOCAL_BLOCKS = 2
WINDOW = 512
X_HEADS = 4
X_HEAD_DIM = 256
N_EXPERTS = 8
EPS = 1e-6

LANE = 128
IN_PACKED = NSA_WIDTH + 6 * KV_WIDTH + LANE + POOL_WIDTH
GATE_OFF = NSA_WIDTH + 6 * KV_WIDTH
POOL_OFF = GATE_OFF + LANE

NEG = -1e30
TINY = float(np.finfo(np.float32).tiny)
VMEM_LIMIT = 56 * 1024 * 1024


def _dot(a, b):
    return jnp.dot(a, b, preferred_element_type=F32)


def _dot_nt(a, b):
    return lax.dot_general(a, b, (((1,), (1,)), ((), ())), preferred_element_type=F32)


def _rms(x, g):
    y = x * lax.rsqrt(jnp.mean(x * x, axis=-1, keepdims=True) + EPS)
    return y * g


def _params(sem, limit=VMEM_LIMIT):
    return pltpu.CompilerParams(dimension_semantics=sem, vmem_limit_bytes=limit)


def _mixer_proj_kernel(x_ref, g_ref, w_ref, cos_ref, sin_ref, wpool_ref, pscale_ref,
                       qt_ref, qrt_ref, kcr_ref, vcr_ref, ks_ref, vst_ref, kw_ref, vwt_ref,
                       gate_ref, p_ref, pool_buf):
    h = _rms(x_ref[0], g_ref[...]).astype(BF16)
    z = _dot(h, w_ref[...])
    tm = z.shape[0]
    cos = cos_ref[...]
    sin = sin_ref[...]
    lane = lax.broadcasted_iota(jnp.int32, (tm, LANE), 1)
    first = (lane & (HEAD_DIM - 1)) < ROPE_HALF
    scale = HEAD_DIM ** -0.5 * math.log2(math.e)

    def rope(xs):
        partner = jnp.where(first, pltpu.roll(xs, LANE - ROPE_HALF, 1), pltpu.roll(xs, ROPE_HALF, 1))
        return xs * cos + partner * sin

    for s in range(NSA_WIDTH // LANE):
        xs = z[:, s * LANE:(s + 1) * LANE]
        for src, ref in ((xs, qt_ref), (rope(xs), qrt_ref)):
            t = (src * scale).T.astype(BF16)
            ref[0, 2 * s] = t[:HEAD_DIM]
            ref[0, 2 * s + 1] = t[HEAD_DIM:]

    def kv_slab(i):
        return z[:, NSA_WIDTH + i * KV_WIDTH:NSA_WIDTH + (i + 1) * KV_WIDTH]

    for slab, ref in ((kv_slab(0), kcr_ref), (kv_slab(1), vcr_ref),
                      (rope(kv_slab(2)), ks_ref), (rope(kv_slab(4)), kw_ref)):
        for gg in range(N_KV_HEADS):
            ref[0, gg] = slab[:, gg * HEAD_DIM:(gg + 1) * HEAD_DIM].astype(ref.dtype)

    for slab, ref in ((kv_slab(3), vst_ref), (kv_slab(5), vwt_ref)):
        t = slab.T.astype(BF16)
        for gg in range(N_KV_HEADS):
            for k in range(tm // LANE):
                ref[0, gg, k] = t[gg * HEAD_DIM:(gg + 1) * HEAD_DIM, k * LANE:(k + 1) * LANE]

    sig_t = jax.nn.sigmoid(z[:, GATE_OFF:GATE_OFF + LANE]).T
    for gg in range(N_KV_HEADS):
        gate_ref[0, gg] = sig_t[gg * GATE_ROWS:(gg + 1) * GATE_ROWS]

    i = pl.program_id(1)

    @pl.when(i == 0)
    def _():
        pool_buf[0:POOL_HALO, :] = jnp.zeros((POOL_HALO, POOL_WIDTH), F32)

    @pl.when(i > 0)
    def _():
        pool_buf[0:POOL_HALO, :] = pool_buf[tm:tm + POOL_HALO, :]

    pool_buf[POOL_HALO:POOL_HALO + tm, :] = z[:, POOL_OFF:POOL_OFF + POOL_WIDTH]
    t1 = i * tm + lax.broadcasted_iota(jnp.int32, (tm, 1), 0) + 1
    for gi, w in enumerate(POOL_WINDOWS):
        cols = slice(gi * POOL_GROUP, (gi + 1) * POOL_GROUP)
        cur = pool_buf[POOL_HALO:POOL_HALO + tm, cols]
        tot = cur
        for k in range(1, w):
            tot = tot + pool_buf[POOL_HALO - k:POOL_HALO - k + tm, cols]
        d = tot / jnp.minimum(t1, w).astype(F32) - cur
        p_ref[0, :, cols] = (_dot(d.astype(BF16), wpool_ref[gi]) * pscale_ref[:, cols]).astype(p_ref.dtype)


def _mixer_proj(x, g, w_packed, cos_t, sin_t, w_pool, pool_scale, tm=1024):
    B, S, D = x.shape
    G = N_KV_HEADS
    sd = jax.ShapeDtypeStruct
    out_shape = (sd((B, N_HEADS, HEAD_DIM, S), BF16), sd((B, N_HEADS, HEAD_DIM, S), BF16),
                 sd((B, G, S, HEAD_DIM), F32), sd((B, G, S, HEAD_DIM), F32),
                 sd((B, G, S, HEAD_DIM), BF16), sd((B, G, S // LANE, HEAD_DIM, LANE), BF16),
                 sd((B, G, S, HEAD_DIM), BF16), sd((B, G, S // LANE, HEAD_DIM, LANE), BF16),
                 sd((B, G, GATE_ROWS, S), F32),
                 sd((B, S, POOL_WIDTH), BF16))
    qspec = pl.BlockSpec((1, N_HEADS, HEAD_DIM, tm), lambda b, i: (b, 0, 0, i))
    kspec = pl.BlockSpec((1, G, tm, HEAD_DIM), lambda b, i: (b, 0, i, 0))
    vspec = pl.BlockSpec((1, G, tm // LANE, HEAD_DIM, LANE), lambda b, i: (b, 0, i, 0, 0))
    out_specs = (qspec, qspec, kspec, kspec, kspec, vspec, kspec, vspec,
                 pl.BlockSpec((1, G, GATE_ROWS, tm), lambda b, i: (b, 0, 0, i)),
                 pl.BlockSpec((1, tm, POOL_WIDTH), lambda b, i: (b, i, 0)))
    return pl.pallas_call(
        _mixer_proj_kernel,
        grid=(B, S // tm),
        in_specs=[pl.BlockSpec((1, tm, D), lambda b, i: (b, i, 0)),
                  pl.BlockSpec((1, D), lambda b, i: (0, 0)),
                  pl.BlockSpec((D, IN_PACKED), lambda b, i: (0, 0)),
                  pl.BlockSpec((tm, LANE), lambda b, i: (i, 0)),
                  pl.BlockSpec((tm, LANE), lambda b, i: (i, 0)),
                  pl.BlockSpec(w_pool.shape, lambda b, i: (0, 0, 0)),
                  pl.BlockSpec((1, POOL_WIDTH), lambda b, i: (0, 0))],
        out_specs=out_specs,
        out_shape=out_shape,
        scratch_shapes=[pltpu.VMEM((tm + POOL_HALO, POOL_WIDTH), F32)],
        compiler_params=_params(("parallel", "arbitrary")),
        name="mixer_proj",
    )(x, g, w_packed, cos_t, sin_t, w_pool, pool_scale)


def _gelu_tanh(x):
    return 0.5 * x * (1.0 + jnp.tanh(math.sqrt(2.0 / math.pi) * (x + 0.044715 * (x * x * x))))


def _compress_kernel(kc_ref, vc_ref, pek_ref, w1k_ref, b1k_ref, w2k_ref,
                     pev_ref, w1v_ref, b1v_ref, w2v_ref, ko_ref, vto_ref):
    def comp(x_ref, pe_ref, w1_ref, b1_ref, w2_ref):
        n = x_ref.shape[2] // CMP_STRIDE
        x = jnp.concatenate([x_ref[0, 0, pl.ds(l, n, stride=CMP_STRIDE), :] for l in range(CMP_STRIDE)], axis=1)
        a = _dot((x + pe_ref[0:1, :]).astype(BF16), w1_ref[0])
        b = _dot((x + pe_ref[1:2, :]).astype(BF16), w1_ref[1])
        pre = a + pltpu.roll(b, n - 1, 0) + b1_ref[...]
        return _dot(_gelu_tanh(pre).astype(BF16), w2_ref[...])

    ko_ref[0, 0] = comp(kc_ref, pek_ref, w1k_ref, b1k_ref, w2k_ref)[:, :HEAD_DIM].astype(ko_ref.dtype)
    vto_ref[0, 0] = comp(vc_ref, pev_ref, w1v_ref, b1v_ref, w2v_ref).T[:HEAD_DIM].astype(vto_ref.dtype)


def _compress(kcr, vcr, pek, w1k, b1k, w2k, pev, w1v, b1v, w2v):
    B, G, S, _ = kcr.shape
    n = S // CMP_STRIDE
    xspec = pl.BlockSpec((1, 1, S, HEAD_DIM), lambda b, g: (b, g, 0, 0))
    full = lambda a: pl.BlockSpec(a.shape, lambda b, g: (0,) * a.ndim)
    return pl.pallas_call(
        _compress_kernel,
        grid=(B, G),
        in_specs=[xspec, xspec, full(pek), full(w1k), full(b1k), full(w2k),
                  full(pev), full(w1v), full(b1v), full(w2v)],
        out_specs=(pl.BlockSpec((1, 1, n, HEAD_DIM), lambda b, g: (b, g, 0, 0)),
                   pl.BlockSpec((1, 1, HEAD_DIM, n), lambda b, g: (b, g, 0, 0))),
        out_shape=(jax.ShapeDtypeStruct((B, G, n, HEAD_DIM), BF16),
                   jax.ShapeDtypeStruct((B, G, HEAD_DIM, n), BF16)),
        compiler_params=_params(("parallel", "parallel")),
        name="compress_kv",
    )(kcr, vcr, pek, w1k, b1k, w2k, pev, w1v, b1v, w2v)


def _nsa_kernel(qt_ref, qrt_ref, kc_ref, vct_ref, ks_ref, vst_ref, kw_ref, vwt_ref, gate_ref,
                covt_ref, o_ref, selbias_ref, *, tq, tk, tk_narrow):
    qt = pl.program_id(2)
    q0 = qt * tq
    n_cmp = kc_ref.shape[2]
    n_blk = covt_ref.shape[0]
    heads = lambda ref: jnp.concatenate([ref[0, hh] for hh in range(GROUP)], axis=1)
    q_t = heads(qt_ref)
    qr_t = heads(qrt_ref)
    t_lane = q0 + lax.broadcasted_iota(jnp.int32, (1, tq), 1)
    per_head = lambda a: [a[:, hh * tq:(hh + 1) * tq] for hh in range(GROUP)]
    all_heads = lambda a: jnp.concatenate([a] * GROUP, axis=1)

    sc = _dot(kc_ref[0, 0], q_t)
    cmp_end = lax.broadcasted_iota(jnp.int32, (n_cmp, 1), 0) * CMP_STRIDE + (CMP_LEN - 1)
    valid = cmp_end <= t_lane
    pcs = []
    for s_h in per_head(sc):
        s_h = jnp.where(valid, s_h, NEG)
        e = jnp.where(valid, jnp.exp2(s_h - jnp.max(s_h, axis=0, keepdims=True)), 0.0)
        pcs.append(e * (1.0 / jnp.maximum(jnp.sum(e, axis=0, keepdims=True), TINY)))
    oc_t = _dot(vct_ref[0, 0], jnp.concatenate(pcs, axis=1).astype(BF16))

    psum = pcs[0] + pcs[1] + pcs[2] + pcs[3]
    p_hi = psum.astype(BF16)
    r1 = psum - p_hi.astype(F32)
    p_mid = r1.astype(BF16)
    p_lo = (r1 - p_mid.astype(F32)).astype(BF16)
    covt = covt_ref[...]
    imp_t = _dot(covt, p_hi) + _dot(covt, p_mid) + _dot(covt, p_lo)

    jb = lax.broadcasted_iota(jnp.int32, (n_blk, tq), 0)
    tb = (q0 + lax.broadcasted_iota(jnp.int32, (n_blk, tq), 1)) // SEL_BLOCK
    dist = tb - jb
    forced = (jb == 0) | ((dist >= 0) & (dist < N_LOCAL_BLOCKS))
    score = jnp.where(jb > tb, -jnp.inf, jnp.where(forced, jnp.inf, imp_t))
    rank = jnp.zeros((n_blk, tq), jnp.int32)
    for jp in range(n_blk):
        rowv = score[jp:jp + 1, :]
        beats = (rowv > score) | ((rowv == score) & (jb > jp))
        rank = rank + jnp.where(beats, 1, 0)
    selbias_ref[...] = jnp.where(rank < min(SEL_TOP_N, n_blk), 0.0, NEG)

    R = GROUP * tq

    def with_ones(v_t):
        extra = jnp.where(lax.broadcasted_iota(jnp.int32, (16, v_t.shape[1]), 0) == 0, 1.0, 0.0)
        return jnp.concatenate([v_t, extra.astype(BF16)], axis=0)

    def sel_chunk(c, carry, width, causal):
        m_i, l_i, acc = carry
        bpc = width // SEL_BLOCK
        vpc = width // LANE
        start = pl.multiple_of(c * width, width)
        k_c = ks_ref[0, 0, pl.ds(start, width), :]
        v_t = jnp.concatenate([vst_ref[0, 0, c * vpc + k] for k in range(vpc)], axis=1)
        rows = [jnp.broadcast_to(selbias_ref[pl.ds(c * bpc + j, 1), :], (SEL_BLOCK, tq)) for j in range(bpc)]
        bias = jnp.concatenate(rows, axis=0)
        if causal:
            kpos = start + lax.broadcasted_iota(jnp.int32, (width, 1), 0)
            bias = jnp.where(kpos <= t_lane, bias, NEG)
        s = _dot(k_c, qr_t) + all_heads(bias)
        m_new = jnp.maximum(m_i, jnp.max(s, axis=0, keepdims=True))
        alpha = jnp.exp2(m_i - m_new)
        pv = _dot(with_ones(v_t), jnp.exp2(s - m_new).astype(BF16))
        return m_new, alpha * l_i + pv[HEAD_DIM:HEAD_DIM + 1], alpha * acc + pv[:HEAD_DIM]

    init = (jnp.full((1, R), NEG, F32), jnp.zeros((1, R), F32), jnp.zeros((HEAD_DIM, R), F32))
    n_wide = q0 // tk
    j_diag = q0 // tk_narrow
    carry = lax.fori_loop(0, n_wide, lambda c, cr: sel_chunk(c, cr, tk, False), init)
    carry = lax.fori_loop(n_wide * (tk // tk_narrow), j_diag, lambda c, cr: sel_chunk(c, cr, tk_narrow, False), carry)
    _, l_s, acc_s = sel_chunk(j_diag, carry, tk_narrow, True)
    os_t = acc_s * (1.0 / jnp.maximum(l_s, TINY))

    wk = WINDOW + tq
    w0 = pl.multiple_of(jnp.maximum(q0 - WINDOW, 0), LANE)
    k_w = kw_ref[0, 0, pl.ds(w0, wk), :]
    v_t = jnp.concatenate([vwt_ref[0, 0, w0 // LANE + k] for k in range(wk // LANE)], axis=1)
    diff = t_lane - (w0 + lax.broadcasted_iota(jnp.int32, (wk, 1), 0))
    bias = jnp.where((diff >= 0) & (diff < WINDOW), 0.0, NEG)
    s = _dot(k_w, qr_t) + all_heads(bias)
    pv = _dot(with_ones(v_t), jnp.exp2(s - jnp.max(s, axis=0, keepdims=True)).astype(BF16))
    ow_t = pv[:HEAD_DIM] * (1.0 / pv[HEAD_DIM:HEAD_DIM + 1])

    gt = gate_ref[0, 0]
    outs = []
    for hh, (a, b, c) in enumerate(zip(per_head(oc_t), per_head(os_t), per_head(ow_t))):
        r = hh * N_BRANCH
        outs.append(gt[r:r + 1] * a + gt[r + 1:r + 2] * b + gt[r + 2:r + 3] * c)
    o_ref[0] = jnp.concatenate(outs, axis=0).T.astype(o_ref.dtype)


def _cover_table(S):
    n_cmp = S // CMP_STRIDE
    n_blk = S // SEL_BLOCK
    cs = np.arange(n_cmp) * CMP_STRIDE
    ss = np.arange(n_blk) * SEL_BLOCK
    cover_t = ((cs[None, :] < ss[:, None] + SEL_BLOCK) & (cs[None, :] + CMP_LEN > ss[:, None]))
    cover_t[:, n_cmp - 1] = False
    return jnp.asarray(cover_t, BF16)


def _nsa_attention(qt, qrt, kc, vct, ks, vst, kw, vwt, gates, tq=256, tk=1024):
    B, _, _, S = qt.shape
    tk = min(tk, S)
    assert tq % LANE == 0 and tk % tq == 0 and S % tk == 0 and S >= WINDOW + tq
    covt = _cover_table(S)
    n_cmp = kc.shape[2]
    n_blk = covt.shape[0]
    qspec = pl.BlockSpec((1, GROUP, HEAD_DIM, tq), lambda b, g, i: (b, g, 0, i))
    kspec = lambda a: pl.BlockSpec((1, 1) + a.shape[2:], lambda b, g, i: (b, g, 0, 0))
    vspec = pl.BlockSpec((1, 1, S // LANE, HEAD_DIM, LANE), lambda b, g, i: (b, g, 0, 0, 0))
    return pl.pallas_call(
        functools.partial(_nsa_kernel, tq=tq, tk=tk, tk_narrow=min(tk, 2 * tq)),
        grid=(B, N_KV_HEADS, S // tq),
        in_specs=[qspec, qspec,
                  pl.BlockSpec((1, 1, n_cmp, HEAD_DIM), lambda b, g, i: (b, g, 0, 0)),
                  pl.BlockSpec((1, 1, HEAD_DIM, n_cmp), lambda b, g, i: (b, g, 0, 0)),
                  kspec(ks), vspec, kspec(kw), vspec,
                  pl.BlockSpec((1, 1, GATE_ROWS, tq), lambda b, g, i: (b, g, 0, i)),
                  pl.BlockSpec(covt.shape, lambda b, g, i: (0, 0))],
        out_specs=pl.BlockSpec((1, tq, GROUP * HEAD_DIM), lambda b, g, i: (b, i, g)),
        out_shape=jax.ShapeDtypeStruct((B, S, NSA_WIDTH), BF16),
        scratch_shapes=[pltpu.VMEM((n_blk, tq), F32)],
        compiler_params=_params(("parallel", "parallel", "parallel")),
        name="nsa_attention",
    )(qt, qrt, kc, vct, ks, vst, kw, vwt, gates, covt)


def _xkv_kernel(m_ref, g_ref, w_ref, k_ref, v_ref):
    h = _rms(m_ref[0], g_ref[...]).astype(BF16)
    kv = _dot(h, w_ref[...])
    d = k_ref.shape[2]
    k_ref[0] = kv[:, :d].astype(k_ref.dtype)
    v_ref[0] = kv[:, d:].astype(v_ref.dtype)


def _xattn_kv(mem, g, wkv):
    B, M, D = mem.shape
    ospec = pl.BlockSpec((1, M, D), lambda b: (b, 0, 0))
    oshape = jax.ShapeDtypeStruct((B, M, D), BF16)
    return pl.pallas_call(
        _xkv_kernel,
        grid=(B,),
        in_specs=[pl.BlockSpec((1, M, D), lambda b: (b, 0, 0)),
                  pl.BlockSpec((1, D), lambda b: (0, 0)),
                  pl.BlockSpec(wkv.shape, lambda b: (0, 0))],
        out_specs=(ospec, ospec),
        out_shape=(oshape, oshape),
        compiler_params=_params(("parallel",)),
        name="xattn_kv",
    )(mem, g, wkv)


def _xattn_kernel(x_ref, a_ref, p_ref, wa_ref, wp_ref, g_ref, wq_ref, k_ref, v_ref, wo_ref, o_ref):
    x = x_ref[0] + _dot(a_ref[0], wa_ref[...]) + _dot(p_ref[0], wp_ref[...])
    h = _rms(x, g_ref[...]).astype(BF16)
    scale = X_HEAD_DIM ** -0.5
    q = (_dot(h, wq_ref[...]) * scale).astype(BF16)
    outs = []
    for hd in range(X_HEADS):
        sl = slice(hd * X_HEAD_DIM, (hd + 1) * X_HEAD_DIM)
        s = _dot_nt(q[:, sl], k_ref[0, :, sl])
        e = jnp.exp(s - jnp.max(s, axis=-1, keepdims=True))
        p = e / jnp.sum(e, axis=-1, keepdims=True)
        outs.append(_dot(p.astype(BF16), v_ref[0, :, sl]).astype(BF16))
    o = jnp.concatenate(outs, axis=-1)
    o_ref[0] = x + _dot(o, wo_ref[...])


def _xattn(x, a, p, wa, wp, g, wq, kx, vx, wo, tm=1024):
    B, S, D = x.shape
    M = kx.shape[1]
    row = lambda n: pl.BlockSpec((1, tm, n), lambda b, i: (b, i, 0))
    full = lambda w: pl.BlockSpec(w.shape, lambda b, i: (0, 0))
    mspec = pl.BlockSpec((1, M, D), lambda b, i: (b, 0, 0))
    return pl.pallas_call(
        _xattn_kernel,
        grid=(B, S // tm),
        in_specs=[row(D), row(a.shape[2]), row(p.shape[2]), full(wa), full(wp),
                  full(g), full(wq), mspec, mspec, full(wo)],
        out_specs=row(D),
        out_shape=jax.ShapeDtypeStruct((B, S, D), F32),
        compiler_params=_params(("parallel", "parallel")),
        name="out_proj_xattn",
    )(x, a, p, wa, wp, g, wq, kx, vx, wo)


def _silu(x):
    return x * jax.nn.sigmoid(x)


def _ffn_kernel(x_ref, g_ref, wg_ref, wu_ref, wd_ref, o_ref):
    x = x_ref[...]
    h = _rms(x, g_ref[...]).astype(BF16)
    act = _silu(_dot(h, wg_ref[...])) * _dot(h, wu_ref[...])
    o_ref[...] = x + _dot(act.astype(BF16), wd_ref[...])


def _ffn(x2, g, wg, wu, wd, tm=512):
    T, D = x2.shape
    resident = lambda w: pl.BlockSpec(w.shape, lambda i: (0, 0), pipeline_mode=pl.Buffered(1))
    return pl.pallas_call(
        _ffn_kernel,
        grid=(T // tm,),
        in_specs=[pl.BlockSpec((tm, D), lambda i: (i, 0)),
                  pl.BlockSpec((1, D), lambda i: (0, 0)),
                  resident(wg), resident(wu), resident(wd)],
        out_specs=pl.BlockSpec((tm, D), lambda i: (i, 0)),
        out_shape=jax.ShapeDtypeStruct((T, D), F32),
        compiler_params=_params(("parallel",)),
        name="dense_swiglu",
    )(x2, g, wg, wu, wd)


MOE_CHUNK = 512
MOE_ROW_TILE = 512
assert MOE_ROW_TILE % MOE_CHUNK == 0


MOE_PARTS = 2


def _tile_meta(m):
    return m >> (2 * MOE_PARTS), [(m & (1 << k)) != 0 for k in range(2 * MOE_PARTS)]


def _router_kernel(x_ref, g_ref, wr_ref, tri_ref, h_ref, route_ref, gate_ref, before_ref, total_ref, run_ref):
    c = pl.program_id(0)

    @pl.when(c == 0)
    def _():
        run_ref[...] = jnp.zeros_like(run_ref)

    hf = _rms(x_ref[...], g_ref[...])
    h_ref[...] = hf.astype(BF16)
    logits = jnp.dot(hf, wr_ref[...], preferred_element_type=F32, precision=lax.Precision.HIGHEST)
    lane = lax.broadcasted_iota(jnp.int32, logits.shape, 1)
    logits = jnp.where(lane < N_EXPERTS, logits, -jnp.inf)
    v1 = jnp.max(logits, axis=-1, keepdims=True)
    i1 = jnp.min(jnp.where(logits == v1, lane, LANE), axis=-1, keepdims=True)
    rest = jnp.where(lane == i1, -jnp.inf, logits)
    v2 = jnp.max(rest, axis=-1, keepdims=True)
    i2 = jnp.min(jnp.where(rest == v2, lane, LANE), axis=-1, keepdims=True)
    e2 = jnp.exp(v2 - v1)
    den = 1.0 + e2
    gate_ref[0] = jnp.where(lane == 0, 1.0 / den, jnp.where(lane == 1, e2 / den, 0.0)).T[:8]

    onehot = jnp.where((lane == i1) | (lane == i2), 1.0, 0.0)
    run = run_ref[...]
    rank = run + _dot(tri_ref[...], onehot.astype(BF16))
    r1 = jnp.sum(jnp.where(lane == i1, rank, 0.0), axis=-1, keepdims=True).astype(jnp.int32)
    r2 = jnp.sum(jnp.where(lane == i2, rank, 0.0), axis=-1, keepdims=True).astype(jnp.int32)
    route = jnp.where(lane == 0, i1, jnp.where(lane == 1, i2, jnp.where(lane == 2, r1, jnp.where(lane == 3, r2, 0))))
    route_ref[0] = route.T[:8]
    before_ref[0] = run
    run = run + jnp.sum(onehot, axis=0, keepdims=True)
    run_ref[...] = run
    total_ref[...] = run


def _router(x2, g, w_router):
    T, D = x2.shape
    C = T // MOE_CHUNK
    tri = jnp.asarray(np.tril(np.ones((MOE_CHUNK, MOE_CHUNK), np.float32), -1), BF16)
    row = lambda n: pl.BlockSpec((MOE_CHUNK, n), lambda c: (c, 0))
    full = lambda a: pl.BlockSpec(a.shape, lambda c: (0, 0))
    return pl.pallas_call(
        _router_kernel,
        grid=(C,),
        in_specs=[row(D), full(g), full(w_router), full(tri)],
        out_specs=(row(D),
                   pl.BlockSpec((1, 8, MOE_CHUNK), lambda c: (c, 0, 0)),
                   pl.BlockSpec((1, 8, MOE_CHUNK), lambda c: (c, 0, 0)),
                   pl.BlockSpec((1, 1, LANE), lambda c: (c, 0, 0)),
                   pl.BlockSpec((1, LANE), lambda c: (0, 0))),
        out_shape=(jax.ShapeDtypeStruct((T, D), BF16),
                   jax.ShapeDtypeStruct((C, 8, MOE_CHUNK), jnp.int32),
                   jax.ShapeDtypeStruct((C, 8, MOE_CHUNK), F32),
                   jax.ShapeDtypeStruct((C, 1, LANE), F32),
                   jax.ShapeDtypeStruct((1, LANE), F32)),
        scratch_shapes=[pltpu.VMEM((1, LANE), F32)],
        compiler_params=_params(("arbitrary",)),
        name="moe_router",
    )(x2, g, w_router, tri)


def _dispatch_kernel(a_ref, h_ref, tok_ref, gate_ref, zlo_ref, zhi_ref, zglo_ref, zghi_ref,
                     lo_ref, hi_ref, glo_ref, ghi_ref):
    del zlo_ref, zhi_ref, zglo_ref, zghi_ref
    e = pl.program_id(0)
    c = pl.program_id(1)
    idx = e * pl.num_programs(1) + c
    a, touched = _tile_meta(a_ref[idx])
    first = (c == 0) | (a != _tile_meta(a_ref[jnp.maximum(idx - 1, 0)])[0])

    @pl.when(first)
    def _():
        lo_ref[...] = jnp.zeros_like(lo_ref)
        hi_ref[...] = jnp.zeros_like(hi_ref)
        glo_ref[...] = jnp.zeros_like(glo_ref)
        ghi_ref[...] = jnp.zeros_like(ghi_ref)

    rows_per_part = MOE_CHUNK // MOE_PARTS

    def scatter(tile, part, x_ref, g_ref):
        tok = tok_ref[0]
        gts = gate_ref[0]
        r0 = part * rows_per_part
        rows = tile * MOE_CHUNK + r0 + lax.broadcasted_iota(jnp.int32, (rows_per_part, MOE_CHUNK), 0)
        c1 = tok[0:1] == rows
        c2 = tok[1:2] == rows
        x_ref[r0:r0 + rows_per_part, :] += _dot(jnp.where(c1 | c2, 1.0, 0.0).astype(BF16),
                                                h_ref[...]).astype(x_ref.dtype)
        g_ref[r0:r0 + rows_per_part, :] += jnp.sum(jnp.where(c1, gts[0:1], 0.0) + jnp.where(c2, gts[1:2], 0.0),
                                                   axis=1, keepdims=True)

    for k, (x_ref, g_ref) in enumerate(((lo_ref, glo_ref), (hi_ref, ghi_ref))):
        for part in range(MOE_PARTS):
            pl.when(touched[k * MOE_PARTS + part])(functools.partial(scatter, a + k, part, x_ref, g_ref))


def _dispatch(a_ec, h, tok_rows, gate_rows, n_slots):
    T, D = h.shape
    C = T // MOE_CHUNK
    zx = jnp.zeros((n_slots, D), BF16)
    zg = jnp.zeros((n_slots, 1), F32)
    any_spec = pl.BlockSpec(memory_space=pl.ANY)
    lo = lambda n: pl.BlockSpec((MOE_CHUNK, n), lambda e, c, a: (a[e * C + c] >> (2 * MOE_PARTS), 0))
    hi = lambda n: pl.BlockSpec((MOE_CHUNK, n), lambda e, c, a: ((a[e * C + c] >> (2 * MOE_PARTS)) + 1, 0))
    return pl.pallas_call(
        _dispatch_kernel,
        grid_spec=pltpu.PrefetchScalarGridSpec(
            num_scalar_prefetch=1,
            grid=(N_EXPERTS, C),
            in_specs=[pl.BlockSpec((MOE_CHUNK, D), lambda e, c, a: (c, 0)),
                      pl.BlockSpec((1, 8, MOE_CHUNK), lambda e, c, a: (c, 0, 0)),
                      pl.BlockSpec((1, 8, MOE_CHUNK), lambda e, c, a: (c, 0, 0)),
                      any_spec, any_spec, any_spec, any_spec],
            out_specs=(lo(D), hi(D), lo(1), hi(1))),
        out_shape=(jax.ShapeDtypeStruct(zx.shape, BF16), jax.ShapeDtypeStruct(zx.shape, BF16),
                   jax.ShapeDtypeStruct(zg.shape, F32), jax.ShapeDtypeStruct(zg.shape, F32)),
        input_output_aliases={4: 0, 5: 1, 6: 2, 7: 3},
        compiler_params=_params(("arbitrary", "arbitrary")),
        name="moe_dispatch",
    )(a_ec, h, tok_rows, gate_rows, zx, zx, zg, zg)


def _expert_ffn_kernel(te_ref, nu_ref, lo_ref, hi_ref, glo_ref, ghi_ref, wg_ref, wu_ref, wd_ref, o_ref):
    del te_ref
    used = pl.program_id(0) < nu_ref[0]

    @pl.when(used)
    def _():
        x = lo_ref[...] + hi_ref[...]
        act = _silu(_dot(x, wg_ref[0])) * _dot(x, wu_ref[0])
        o_ref[...] = (_dot(act.astype(BF16), wd_ref[0]) * (glo_ref[...] + ghi_ref[...])).astype(o_ref.dtype)

    @pl.when(jnp.logical_not(used))
    def _():
        o_ref[...] = jnp.zeros_like(o_ref)


def _expert_ffn(tile_expert, n_used, xs_lo, xs_hi, gs_lo, gs_hi, wg, wu, wd, tr=256):
    N, D = xs_lo.shape
    sub = MOE_ROW_TILE // tr
    tile_expert = jnp.repeat(tile_expert, sub)
    n_used = n_used * sub
    row = lambda n: pl.BlockSpec((tr, n), lambda i, te, nu: (i, 0))
    expert = lambda w: pl.BlockSpec((1,) + w.shape[1:], lambda i, te, nu: (te[i], 0, 0))
    return pl.pallas_call(
        _expert_ffn_kernel,
        grid_spec=pltpu.PrefetchScalarGridSpec(
            num_scalar_prefetch=2,
            grid=(N // tr,),
            in_specs=[row(D), row(D), row(1), row(1), expert(wg), expert(wu), expert(wd)],
            out_specs=row(D)),
        out_shape=jax.ShapeDtypeStruct((N, D), BF16),
        compiler_params=_params(("arbitrary",)),
        name="moe_expert_ffn",
    )(tile_expert, n_used, xs_lo, xs_hi, gs_lo, gs_hi, wg, wu, wd)


def _combine_kernel(meta_ref, hidx_ref, x_ref, *refs):
    del hidx_ref
    E = N_EXPERTS
    ylo, yhi = refs[:E], refs[E:2 * E]
    s1_ref, s2_ref, gf_ref, o_ref, acc_ref = refs[2 * E:]
    c = pl.program_id(0)
    s1 = s1_ref[...]
    s2 = s2_ref[...]

    def pick(tile):
        parts = []
        for j in range(MOE_CHUNK // LANE):
            cols = tile * MOE_CHUNK + j * LANE + lax.broadcasted_iota(jnp.int32, (MOE_CHUNK, LANE), 1)
            parts.append(jnp.where((s1 == cols) | (s2 == cols), 1.0, 0.0).astype(BF16))
        return jnp.concatenate(parts, axis=1)

    metas = [_tile_meta(meta_ref[c * E + e]) for e in range(E)]
    total = x_ref[...]
    for (a, touched), y_ref in zip(metas, ylo):
        in_lo = functools.reduce(jnp.logical_or, touched[:MOE_PARTS])
        total = total + _dot(pick(jnp.where(in_lo, a, -1)), y_ref[...])
    acc_ref[...] = total

    for (a, touched), y_ref in zip(metas, yhi):
        @pl.when(functools.reduce(jnp.logical_or, touched[MOE_PARTS:]))
        def _(a=a, y_ref=y_ref):
            acc_ref[...] += _dot(pick(a + 1), y_ref[...])

    o_ref[...] = _rms(acc_ref[...], gf_ref[...])


def _combine(meta_ce, hi_tile_ce, x2, ys, slot1_rep, slot2_rep, g_final):
    T, D = x2.shape
    C = T // MOE_CHUNK
    E = N_EXPERTS
    lo_spec = lambda e: pl.BlockSpec((MOE_CHUNK, D), lambda c, m, h: (m[c * E + e] >> (2 * MOE_PARTS), 0))
    hi_spec = lambda e: pl.BlockSpec((MOE_CHUNK, D), lambda c, m, h: (h[c * E + e], 0))
    return pl.pallas_call(
        _combine_kernel,
        grid_spec=pltpu.PrefetchScalarGridSpec(
            num_scalar_prefetch=2,
            grid=(C,),
            in_specs=([pl.BlockSpec((MOE_CHUNK, D), lambda c, m, h: (c, 0))]
                      + [lo_spec(e) for e in range(E)] + [hi_spec(e) for e in range(E)]
                      + [pl.BlockSpec((MOE_CHUNK, LANE), lambda c, m, h: (c, 0)),
                         pl.BlockSpec((MOE_CHUNK, LANE), lambda c, m, h: (c, 0)),
                         pl.BlockSpec((1, D), lambda c, m, h: (0, 0))]),
            out_specs=pl.BlockSpec((MOE_CHUNK, D), lambda c, m, h: (c, 0)),
            scratch_shapes=[pltpu.VMEM((MOE_CHUNK, D), F32)]),
        out_shape=jax.ShapeDtypeStruct((T, D), F32),
        compiler_params=_params(("arbitrary",)),
        name="moe_combine_final_norm",
    )(meta_ce, hi_tile_ce, x2, *([ys] * (2 * E)), slot1_rep, slot2_rep, g_final)


def _rope_tables(S):
    inv = ROPE_THETA ** (-jnp.arange(ROPE_HALF, dtype=F32) / ROPE_HALF)
    ang = jnp.arange(S, dtype=F32)[:, None] * inv[None, :]
    cos, sin = jnp.cos(ang), jnp.sin(ang)
    pad = HEAD_DIM - ROPE_DIM
    cos_h = jnp.concatenate([cos, cos, jnp.ones((S, pad), F32)], axis=-1)
    sin_h = jnp.concatenate([-sin, sin, jnp.zeros((S, pad), F32)], axis=-1)
    reps = LANE // HEAD_DIM
    return jnp.tile(cos_h, (1, reps)), jnp.tile(sin_h, (1, reps))


def _pack_w_in(w):
    per_group = GROUP * N_BRANCH
    gates = w[:, GATE_OFF:GATE_OFF + N_KV_HEADS * per_group].reshape(-1, N_KV_HEADS, per_group)
    gates = jnp.pad(gates, ((0, 0), (0, LANE // GATE_ROWS - N_KV_HEADS), (0, GATE_ROWS - per_group)))
    gates = gates.reshape(-1, LANE)
    return jnp.concatenate([w[:, :GATE_OFF], gates, w[:, GATE_OFF + N_KV_HEADS * per_group:]], axis=1).astype(BF16)


def _mixer_heads(x, norm_g, w_in, pe_k, w1_k, b1_k, w2_k, pe_v, w1_v, b1_v, w2_v,
                 w_pool, pool_scale, cos_t, sin_t):
    B, S, D = x.shape
    qt, qrt, kcr, vcr, ks, vst, kw, vwt, gates, p = _mixer_proj(
        x, norm_g.reshape(1, D), _pack_w_in(w_in), cos_t, sin_t, w_pool.astype(BF16), pool_scale.reshape(1, -1))
    half = CMP_LEN // 2
    cw = half * HEAD_DIM
    pad_w2 = lambda w2: jnp.pad(w2, ((0, 0), (0, LANE - HEAD_DIM))).astype(BF16)
    kc, vct = _compress(
        kcr, vcr,
        pe_k.reshape(2, cw), w1_k.reshape(2, cw, CMP_HIDDEN).astype(BF16), b1_k.reshape(1, -1), pad_w2(w2_k),
        pe_v.reshape(2, cw), w1_v.reshape(2, cw, CMP_HIDDEN).astype(BF16), b1_v.reshape(1, -1), pad_w2(w2_v))
    return _nsa_attention(qt, qrt, kc, vct, ks, vst, kw, vwt, gates), p


def _moe_layer(x2, g_ffn, router, wg, wu, wd, g_final):
    T, D = x2.shape
    E = N_EXPERTS
    C = T // MOE_CHUNK
    i32 = jnp.int32
    w_router = jnp.pad(router, ((0, 0), (0, LANE - E)))
    h, route, gate_rows, before, total = _router(x2, g_ffn.reshape(1, D), w_router)

    counts = total[0, :E].astype(i32)
    tiles = (counts + MOE_ROW_TILE - 1) // MOE_ROW_TILE
    ends = jnp.cumsum(tiles)
    off = (ends - tiles) * MOE_ROW_TILE
    e1, e2, r1, r2 = route[:, 0], route[:, 1], route[:, 2], route[:, 3]
    off_of = lambda e: sum(jnp.where(e == k, off[k], 0) for k in range(E))
    slot1 = off_of(e1) + r1
    slot2 = off_of(e2) + r2
    first = off[None, :] + before[:, 0, :E].astype(i32)
    after = jnp.concatenate([before[1:, 0, :E], total[:, :E]], axis=0).astype(i32)
    last = off[None, :] + after - 1
    lo_tile = first // MOE_CHUNK
    tile_meta = lo_tile << (2 * MOE_PARTS)
    part_rows = MOE_CHUNK // MOE_PARTS
    for k in range(2 * MOE_PARTS):
        start = lo_tile * MOE_CHUNK + k * part_rows
        touched = (last >= first) & (first < start + part_rows) & (last >= start)
        tile_meta = tile_meta + (touched.astype(i32) << k)
    n_row_tiles = (2 * T) // MOE_ROW_TILE + E + 1 + MOE_CHUNK // MOE_ROW_TILE
    n_used = ends[-1:]
    tile_ids = jnp.minimum(jnp.arange(n_row_tiles, dtype=i32), n_used[0] - 1)
    tile_expert = jnp.minimum(jnp.sum(ends[None, :] <= tile_ids[:, None], axis=1), E - 1).astype(i32)

    zero = jnp.zeros_like(slot1)
    tok_rows = jnp.stack([slot1, slot2, zero, zero, zero, zero, zero, zero], axis=1)
    slot1_rep = jnp.broadcast_to(slot1.reshape(T, 1), (T, LANE))
    slot2_rep = jnp.broadcast_to(slot2.reshape(T, 1), (T, LANE))

    xs_lo, xs_hi, gs_lo, gs_hi = _dispatch(tile_meta.T.reshape(-1).astype(i32), h, tok_rows, gate_rows,
                                           n_row_tiles * MOE_ROW_TILE)
    ys = _expert_ffn(tile_expert, n_used.astype(i32), xs_lo, xs_hi, gs_lo, gs_hi,
                     wg.astype(BF16), wu.astype(BF16), wd.astype(BF16))
    in_hi = (tile_meta >> MOE_PARTS) & ((1 << MOE_PARTS) - 1) != 0
    hi_tile = lax.cummax(jnp.where(in_hi, lo_tile + 1, 0), axis=0)
    return _combine(tile_meta.reshape(-1).astype(i32), hi_tile.reshape(-1).astype(i32), x2, ys,
                    slot1_rep, slot2_rep, g_final.reshape(1, D))


def kernel(x, mem, norm_mix, w_in, cmp_pe_k, cmp_w1_k, cmp_b1_k, cmp_w2_k, cmp_pe_v, cmp_w1_v, cmp_b1_v, cmp_w2_v, w_pool, pool_scale, w_out, norm_x, norm_mem, wq_x, wk_x, wv_x, wo_x, norm_ffn, ffn_wg, ffn_wu, ffn_wd, moe_router, moe_wg, moe_wu, moe_wd, norm_final):
    B, S, D = x.shape
    depth = norm_mix.shape[0]
    assert depth == 2, "the final RMSNorm is fused into the expert layer, which must be the last one"
    cos_t, sin_t = _rope_tables(S)
    for layer in range(depth):
        a, p = _mixer_heads(x, norm_mix[layer], w_in[layer],
                            cmp_pe_k[layer], cmp_w1_k[layer], cmp_b1_k[layer], cmp_w2_k[layer],
                            cmp_pe_v[layer], cmp_w1_v[layer], cmp_b1_v[layer], cmp_w2_v[layer],
                            w_pool[layer], pool_scale[layer], cos_t, sin_t)
        wkv = jnp.concatenate([wk_x[layer], wv_x[layer]], axis=1).astype(BF16)
        kx, vx = _xattn_kv(mem, norm_mem[layer].reshape(1, D), wkv)
        wo_mix = w_out[layer].astype(BF16)
        x = _xattn(x, a, p, wo_mix[:NSA_WIDTH], wo_mix[NSA_WIDTH:], norm_x[layer].reshape(1, D),
                   wq_x[layer].astype(BF16), kx, vx, wo_x[layer].astype(BF16))
        x2 = x.reshape(B * S, D)
        j = layer // 2
        if layer % 2 == 0:
            x2 = _ffn(x2, norm_ffn[layer].reshape(1, D), ffn_wg[j].astype(BF16), ffn_wu[j].astype(BF16),
                      ffn_wd[j].astype(BF16))
        else:
            x2 = _moe_layer(x2, norm_ffn[layer], moe_router[j], moe_wg[j], moe_wu[j], moe_wd[j], norm_final)
        x = x2.reshape(B, S, D)
    return x
```

```python
import functools
import math

import numpy as np
import jax
import jax.numpy as jnp
from jax import lax
from jax.experimental import pallas as pl
from jax.experimental.pallas import tpu as pltpu

F32 = jnp.float32
BF16 = jnp.bfloat16

D_MODEL = 1024
HEAD_DIM = 64
N_HEADS = 8
N_KV_HEADS = 2
GROUP = N_HEADS // N_KV_HEADS
NSA_WIDTH = N_HEADS * HEAD_DIM
KV_WIDTH = N_KV_HEADS * HEAD_DIM
N_BRANCH = 3
GATE_ROWS = 16
POOL_WIDTH = 512
POOL_WINDOWS = (2, 4, 8, 16)
POOL_GROUP = 128
POOL_HALO = 16
ROPE_DIM = 16
ROPE_HALF = 8
ROPE_THETA = 500000.0
CMP_LEN = 32
CMP_STRIDE = 16
CMP_HIDDEN = 256
SEL_BLOCK = 64
SEL_TOP_N = 16
N_LOCAL_BLOCKS = 2
WINDOW = 512
X_HEADS = 4
X_HEAD_DIM = 256
N_EXPERTS = 8
EPS = 1e-6

LANE = 128
SUBLANE = 8
BF16_ROWS = 16
IN_PACKED = NSA_WIDTH + 6 * KV_WIDTH + LANE + POOL_WIDTH
GATE_OFF = NSA_WIDTH + 6 * KV_WIDTH
POOL_OFF = GATE_OFF + LANE

NEG = -1e30
TINY = float(np.finfo(np.float32).tiny)
VMEM_LIMIT = 56 * 1024 * 1024


def _dot(a, b):
    return jnp.dot(a, b, preferred_element_type=F32)


def _dot_nt(a, b):
    return lax.dot_general(a, b, (((1,), (1,)), ((), ())), preferred_element_type=F32)


def _rms(x, g):
    y = x * lax.rsqrt(jnp.mean(x * x, axis=-1, keepdims=True) + EPS)
    return y * g


def _params(sem, limit=VMEM_LIMIT):
    return pltpu.CompilerParams(dimension_semantics=sem, vmem_limit_bytes=limit)


def _mixer_proj_kernel(x_ref, g_ref, w_ref, cos_ref, sin_ref, wpool_ref, pscale_ref,
                       qt_ref, qrt_ref, kcr_ref, vcr_ref, ks_ref, vst_ref, kw_ref, vwt_ref,
                       gate_ref, p_ref, pool_buf):
    h = _rms(x_ref[0], g_ref[...]).astype(BF16)
    z = _dot(h, w_ref[...])
    tm = z.shape[0]
    cos = cos_ref[...]
    sin = sin_ref[...]
    lane = lax.broadcasted_iota(jnp.int32, (tm, LANE), 1)
    first = (lane & (HEAD_DIM - 1)) < ROPE_HALF
    scale = HEAD_DIM ** -0.5 * math.log2(math.e)

    def rope(xs):
        partner = jnp.where(first, pltpu.roll(xs, LANE - ROPE_HALF, 1), pltpu.roll(xs, ROPE_HALF, 1))
        return xs * cos + partner * sin

    for s in range(NSA_WIDTH // LANE):
        xs = z[:, s * LANE:(s + 1) * LANE]
        for src, ref in ((xs, qt_ref), (rope(xs), qrt_ref)):
            t = (src * scale).T.astype(BF16)
            ref[0, 2 * s] = t[:HEAD_DIM]
            ref[0, 2 * s + 1] = t[HEAD_DIM:]

    def kv_slab(i):
        return z[:, NSA_WIDTH + i * KV_WIDTH:NSA_WIDTH + (i + 1) * KV_WIDTH]

    for slab, ref in ((kv_slab(0), kcr_ref), (kv_slab(1), vcr_ref),
                      (rope(kv_slab(2)), ks_ref), (rope(kv_slab(4)), kw_ref)):
        for gg in range(N_KV_HEADS):
            ref[0, gg] = slab[:, gg * HEAD_DIM:(gg + 1) * HEAD_DIM].astype(ref.dtype)

    for slab, ref in ((kv_slab(3), vst_ref), (kv_slab(5), vwt_ref)):
        t = slab.T.astype(BF16)
        for gg in range(N_KV_HEADS):
            for k in range(tm // LANE):
                ref[0, gg, k] = t[gg * HEAD_DIM:(gg + 1) * HEAD_DIM, k * LANE:(k + 1) * LANE]

    sig_t = jax.nn.sigmoid(z[:, GATE_OFF:GATE_OFF + LANE]).T
    for gg in range(N_KV_HEADS):
        gate_ref[0, gg] = sig_t[gg * GATE_ROWS:(gg + 1) * GATE_ROWS]

    i = pl.program_id(1)

    @pl.when(i == 0)
    def _():
        pool_buf[0:POOL_HALO, :] = jnp.zeros((POOL_HALO, POOL_WIDTH), F32)

    @pl.when(i > 0)
    def _():
        pool_buf[0:POOL_HALO, :] = pool_buf[tm:tm + POOL_HALO, :]

    pool_buf[POOL_HALO:POOL_HALO + tm, :] = z[:, POOL_OFF:POOL_OFF + POOL_WIDTH]
    t1 = i * tm + lax.broadcasted_iota(jnp.int32, (tm, 1), 0) + 1
    for gi, w in enumerate(POOL_WINDOWS):
        cols = slice(gi * POOL_GROUP, (gi + 1) * POOL_GROUP)
        cur = pool_buf[POOL_HALO:POOL_HALO + tm, cols]
        tot = cur
        for k in range(1, w):
            tot = tot + pool_buf[POOL_HALO - k:POOL_HALO - k + tm, cols]
        d = tot / jnp.minimum(t1, w).astype(F32) - cur
        p_ref[0, :, cols] = (_dot(d.astype(BF16), wpool_ref[gi]) * pscale_ref[:, cols]).astype(p_ref.dtype)


def _mixer_proj(x, g, w_packed, cos_t, sin_t, w_pool, pool_scale, tm=1024):
    B, S, D = x.shape
    G = N_KV_HEADS
    sd = jax.ShapeDtypeStruct
    out_shape = (sd((B, N_HEADS, HEAD_DIM, S), BF16), sd((B, N_HEADS, HEAD_DIM, S), BF16),
                 sd((B, G, S, HEAD_DIM), F32), sd((B, G, S, HEAD_DIM), F32),
                 sd((B, G, S, HEAD_DIM), BF16), sd((B, G, S // LANE, HEAD_DIM, LANE), BF16),
                 sd((B, G, S, HEAD_DIM), BF16), sd((B, G, S // LANE, HEAD_DIM, LANE), BF16),
                 sd((B, G, GATE_ROWS, S), F32),
                 sd((B, S, POOL_WIDTH), BF16))
    qspec = pl.BlockSpec((1, N_HEADS, HEAD_DIM, tm), lambda b, i: (b, 0, 0, i))
    kspec = pl.BlockSpec((1, G, tm, HEAD_DIM), lambda b, i: (b, 0, i, 0))
    vspec = pl.BlockSpec((1, G, tm // LANE, HEAD_DIM, LANE), lambda b, i: (b, 0, i, 0, 0))
    out_specs = (qspec, qspec, kspec, kspec, kspec, vspec, kspec, vspec,
                 pl.BlockSpec((1, G, GATE_ROWS, tm), lambda b, i: (b, 0, 0, i)),
                 pl.BlockSpec((1, tm, POOL_WIDTH), lambda b, i: (b, i, 0)))
    return pl.pallas_call(
        _mixer_proj_kernel,
        grid=(B, S // tm),
        in_specs=[pl.BlockSpec((1, tm, D), lambda b, i: (b, i, 0)),
                  pl.BlockSpec((1, D), lambda b, i: (0, 0)),
                  pl.BlockSpec((D, IN_PACKED), lambda b, i: (0, 0)),
                  pl.BlockSpec((tm, LANE), lambda b, i: (i, 0)),
                  pl.BlockSpec((tm, LANE), lambda b, i: (i, 0)),
                  pl.BlockSpec(w_pool.shape, lambda b, i: (0, 0, 0)),
                  pl.BlockSpec((1, POOL_WIDTH), lambda b, i: (0, 0))],
        out_specs=out_specs,
        out_shape=out_shape,
        scratch_shapes=[pltpu.VMEM((tm + POOL_HALO, POOL_WIDTH), F32)],
        compiler_params=_params(("parallel", "arbitrary")),
        name="mixer_proj",
    )(x, g, w_packed, cos_t, sin_t, w_pool, pool_scale)


def _gelu_tanh(x):
    return 0.5 * x * (1.0 + jnp.tanh(math.sqrt(2.0 / math.pi) * (x + 0.044715 * (x * x * x))))


def _compress_kernel(kc_ref, vc_ref, pek_ref, w1k_ref, b1k_ref, w2k_ref,
                     pev_ref, w1v_ref, b1v_ref, w2v_ref, ko_ref, vto_ref):
    def comp(x_ref, pe_ref, w1_ref, b1_ref, w2_ref):
        n = x_ref.shape[2] // CMP_STRIDE
        x = jnp.concatenate([x_ref[0, 0, pl.ds(l, n, stride=CMP_STRIDE), :] for l in range(CMP_STRIDE)], axis=1)
        a = _dot((x + pe_ref[0:1, :]).astype(BF16), w1_ref[0])
        b = _dot((x + pe_ref[1:2, :]).astype(BF16), w1_ref[1])
        pre = a + pltpu.roll(b, n - 1, 0) + b1_ref[...]
        return _dot(_gelu_tanh(pre).astype(BF16), w2_ref[...])

    ko_ref[0, 0] = comp(kc_ref, pek_ref, w1k_ref, b1k_ref, w2k_ref)[:, :HEAD_DIM].astype(ko_ref.dtype)
    vto_ref[0, 0] = comp(vc_ref, pev_ref, w1v_ref, b1v_ref, w2v_ref).T[:HEAD_DIM].astype(vto_ref.dtype)


def _compress(kcr, vcr, pek, w1k, b1k, w2k, pev, w1v, b1v, w2v):
    B, G, S, _ = kcr.shape
    n = S // CMP_STRIDE
    xspec = pl.BlockSpec((1, 1, S, HEAD_DIM), lambda b, g: (b, g, 0, 0))
    full = lambda a: pl.BlockSpec(a.shape, lambda b, g: (0,) * a.ndim)
    return pl.pallas_call(
        _compress_kernel,
        grid=(B, G),
        in_specs=[xspec, xspec, full(pek), full(w1k), full(b1k), full(w2k),
                  full(pev), full(w1v), full(b1v), full(w2v)],
        out_specs=(pl.BlockSpec((1, 1, n, HEAD_DIM), lambda b, g: (b, g, 0, 0)),
                   pl.BlockSpec((1, 1, HEAD_DIM, n), lambda b, g: (b, g, 0, 0))),
        out_shape=(jax.ShapeDtypeStruct((B, G, n, HEAD_DIM), BF16),
                   jax.ShapeDtypeStruct((B, G, HEAD_DIM, n), BF16)),
        compiler_params=_params(("parallel", "parallel")),
        name="compress_kv",
    )(kcr, vcr, pek, w1k, b1k, w2k, pev, w1v, b1v, w2v)


def _nsa_kernel(qt_ref, qrt_ref, kc_ref, vct_ref, ks_ref, vst_ref, kw_ref, vwt_ref, gate_ref,
                covt_ref, o_ref, selbias_ref, rank_ref, *, tq, tk, tk_narrow):
    qt = pl.program_id(2)
    q0 = qt * tq
    n_cmp = kc_ref.shape[2]
    n_blk = covt_ref.shape[0]
    heads = lambda ref: jnp.concatenate([ref[0, hh] for hh in range(GROUP)], axis=1)
    q_t = heads(qt_ref)
    qr_t = heads(qrt_ref)
    t_lane = q0 + lax.broadcasted_iota(jnp.int32, (1, tq), 1)
    per_head = lambda a: [a[:, hh * tq:(hh + 1) * tq] for hh in range(GROUP)]
    all_heads = lambda a: jnp.concatenate([a] * GROUP, axis=1)

    sc = _dot(kc_ref[0, 0], q_t)
    cmp_end = lax.broadcasted_iota(jnp.int32, (n_cmp, 1), 0) * CMP_STRIDE + (CMP_LEN - 1)
    valid = cmp_end <= t_lane
    pcs = []
    for s_h in per_head(sc):
        s_h = jnp.where(valid, s_h, NEG)
        e = jnp.where(valid, jnp.exp2(s_h - jnp.max(s_h, axis=0, keepdims=True)), 0.0)
        pcs.append(e * (1.0 / jnp.maximum(jnp.sum(e, axis=0, keepdims=True), TINY)))
    oc_t = _dot(vct_ref[0, 0], jnp.concatenate(pcs, axis=1).astype(BF16))

    psum = pcs[0] + pcs[1] + pcs[2] + pcs[3]
    p_hi = psum.astype(BF16)
    r1 = psum - p_hi.astype(F32)
    p_mid = r1.astype(BF16)
    p_lo = (r1 - p_mid.astype(F32)).astype(BF16)
    covt = covt_ref[...]
    imp_t = _dot(covt, p_hi) + _dot(covt, p_mid) + _dot(covt, p_lo)

    jb = lax.broadcasted_iota(jnp.int32, (n_blk, tq), 0)
    tb = (q0 + lax.broadcasted_iota(jnp.int32, (n_blk, tq), 1)) // SEL_BLOCK
    dist = tb - jb
    forced = (jb == 0) | ((dist >= 0) & (dist < N_LOCAL_BLOCKS))
    score = jnp.where(jb > tb, -jnp.inf, jnp.where(forced, jnp.inf, imp_t))
    rank_ref[...] = jnp.zeros((n_blk, tq), jnp.int32)
    n_live = (q0 + tq - 1) // SEL_BLOCK + 1
    for g8 in range(n_blk // SUBLANE):
        @pl.when(g8 * SUBLANE < n_live)
        def _(g8=g8):
            rank = rank_ref[...]
            for jp in range(g8 * SUBLANE, (g8 + 1) * SUBLANE):
                rowv = score[jp:jp + 1, :]
                beats = (rowv > score) | ((rowv == score) & (jb > jp))
                rank = rank + jnp.where(beats, 1, 0)
            rank_ref[...] = rank
    selbias_ref[...] = jnp.where(rank_ref[...] < min(SEL_TOP_N, n_blk), 0.0, NEG)

    R = GROUP * tq

    def with_ones(v_t):
        extra = jnp.where(lax.broadcasted_iota(jnp.int32, (BF16_ROWS, v_t.shape[1]), 0) == 0, 1.0, 0.0)
        return jnp.concatenate([v_t, extra.astype(BF16)], axis=0)

    def sel_chunk(c, carry, width, causal):
        m_i, l_i, acc = carry
        bpc = width // SEL_BLOCK
        vpc = width // LANE
        start = pl.multiple_of(c * width, width)
        k_c = ks_ref[0, 0, pl.ds(start, width), :]
        v_t = jnp.concatenate([vst_ref[0, 0, c * vpc + k] for k in range(vpc)], axis=1)
        rows = [jnp.broadcast_to(selbias_ref[pl.ds(c * bpc + j, 1), :], (SEL_BLOCK, tq)) for j in range(bpc)]
        bias = jnp.concatenate(rows, axis=0)
        if causal:
            kpos = start + lax.broadcasted_iota(jnp.int32, (width, 1), 0)
            bias = jnp.where(kpos <= t_lane, bias, NEG)
        s = _dot(k_c, qr_t) + all_heads(bias)
        m_new = jnp.maximum(m_i, jnp.max(s, axis=0, keepdims=True))
        alpha = jnp.exp2(m_i - m_new)
        pv = _dot(with_ones(v_t), jnp.exp2(s - m_new).astype(BF16))
        return m_new, alpha * l_i + pv[HEAD_DIM:HEAD_DIM + 1], alpha * acc + pv[:HEAD_DIM]

    init = (jnp.full((1, R), NEG, F32), jnp.zeros((1, R), F32), jnp.zeros((HEAD_DIM, R), F32))
    n_wide = q0 // tk
    j_diag = q0 // tk_narrow
    carry = lax.fori_loop(0, n_wide, lambda c, cr: sel_chunk(c, cr, tk, False), init)
    carry = lax.fori_loop(n_wide * (tk // tk_narrow), j_diag, lambda c, cr: sel_chunk(c, cr, tk_narrow, False), carry)
    _, l_s, acc_s = sel_chunk(j_diag, carry, tk_narrow, True)
    os_t = acc_s * (1.0 / jnp.maximum(l_s, TINY))

    wk = WINDOW + tq
    w0 = pl.multiple_of(jnp.maximum(q0 - WINDOW, 0), LANE)
    k_w = kw_ref[0, 0, pl.ds(w0, wk), :]
    v_t = jnp.concatenate([vwt_ref[0, 0, w0 // LANE + k] for k in range(wk // LANE)], axis=1)
    diff = t_lane - (w0 + lax.broadcasted_iota(jnp.int32, (wk, 1), 0))
    bias = jnp.where((diff >= 0) & (diff < WINDOW), 0.0, NEG)
    s = _dot(k_w, qr_t) + all_heads(bias)
    pv = _dot(with_ones(v_t), jnp.exp2(s - jnp.max(s, axis=0, keepdims=True)).astype(BF16))
    ow_t = pv[:HEAD_DIM] * (1.0 / pv[HEAD_DIM:HEAD_DIM + 1])

    gt = gate_ref[0, 0]
    outs = []
    for hh, (a, b, c) in enumerate(zip(per_head(oc_t), per_head(os_t), per_head(ow_t))):
        r = hh * N_BRANCH
        outs.append(gt[r:r + 1] * a + gt[r + 1:r + 2] * b + gt[r + 2:r + 3] * c)
    o_ref[0] = jnp.concatenate(outs, axis=0).T.astype(o_ref.dtype)


def _cover_table(S):
    n_cmp = S // CMP_STRIDE
    n_blk = S // SEL_BLOCK
    cs = np.arange(n_cmp) * CMP_STRIDE
    ss = np.arange(n_blk) * SEL_BLOCK
    cover_t = ((cs[None, :] < ss[:, None] + SEL_BLOCK) & (cs[None, :] + CMP_LEN > ss[:, None]))
    cover_t[:, n_cmp - 1] = False
    return jnp.asarray(cover_t, BF16)


def _nsa_attention(qt, qrt, kc, vct, ks, vst, kw, vwt, gates, tq=256, tk=1024):
    B, _, _, S = qt.shape
    tk = min(tk, S)
    assert tq % LANE == 0 and tk % tq == 0 and S % tk == 0 and S >= WINDOW + tq
    covt = _cover_table(S)
    n_cmp = kc.shape[2]
    n_blk = covt.shape[0]
    qspec = pl.BlockSpec((1, GROUP, HEAD_DIM, tq), lambda b, g, i: (b, g, 0, i))
    kspec = lambda a: pl.BlockSpec((1, 1) + a.shape[2:], lambda b, g, i: (b, g, 0, 0))
    vspec = pl.BlockSpec((1, 1, S // LANE, HEAD_DIM, LANE), lambda b, g, i: (b, g, 0, 0, 0))
    return pl.pallas_call(
        functools.partial(_nsa_kernel, tq=tq, tk=tk, tk_narrow=min(tk, 2 * tq)),
        grid=(B, N_KV_HEADS, S // tq),
        in_specs=[qspec, qspec,
                  pl.BlockSpec((1, 1, n_cmp, HEAD_DIM), lambda b, g, i: (b, g, 0, 0)),
                  pl.BlockSpec((1, 1, HEAD_DIM, n_cmp), lambda b, g, i: (b, g, 0, 0)),
                  kspec(ks), vspec, kspec(kw), vspec,
                  pl.BlockSpec((1, 1, GATE_ROWS, tq), lambda b, g, i: (b, g, 0, i)),
                  pl.BlockSpec(covt.shape, lambda b, g, i: (0, 0))],
        out_specs=pl.BlockSpec((1, tq, GROUP * HEAD_DIM), lambda b, g, i: (b, i, g)),
        out_shape=jax.ShapeDtypeStruct((B, S, NSA_WIDTH), BF16),
        scratch_shapes=[pltpu.VMEM((n_blk, tq), F32), pltpu.VMEM((n_blk, tq), jnp.int32)],
        compiler_params=_params(("parallel", "parallel", "parallel")),
        name="nsa_attention",
    )(qt, qrt, kc, vct, ks, vst, kw, vwt, gates, covt)


def _xkv_kernel(m_ref, g_ref, w_ref, k_ref, v_ref):
    h = _rms(m_ref[0], g_ref[...]).astype(BF16)
    kv = _dot(h, w_ref[...])
    d = k_ref.shape[2]
    k_ref[0] = kv[:, :d].astype(k_ref.dtype)
    v_ref[0] = kv[:, d:].astype(v_ref.dtype)


def _xattn_kv(mem, g, wkv):
    B, M, D = mem.shape
    ospec = pl.BlockSpec((1, M, D), lambda b: (b, 0, 0))
    oshape = jax.ShapeDtypeStruct((B, M, D), BF16)
    return pl.pallas_call(
        _xkv_kernel,
        grid=(B,),
        in_specs=[pl.BlockSpec((1, M, D), lambda b: (b, 0, 0)),
                  pl.BlockSpec((1, D), lambda b: (0, 0)),
                  pl.BlockSpec(wkv.shape, lambda b: (0, 0))],
        out_specs=(ospec, ospec),
        out_shape=(oshape, oshape),
        compiler_params=_params(("parallel",)),
        name="xattn_kv",
    )(mem, g, wkv)


def _xattn_kernel(x_ref, a_ref, p_ref, wa_ref, wp_ref, g_ref, wq_ref, k_ref, v_ref, wo_ref, o_ref):
    x = x_ref[0] + _dot(a_ref[0], wa_ref[...]) + _dot(p_ref[0], wp_ref[...])
    h = _rms(x, g_ref[...]).astype(BF16)
    scale = X_HEAD_DIM ** -0.5
    q = (_dot(h, wq_ref[...]) * scale).astype(BF16)
    outs = []
    for hd in range(X_HEADS):
        sl = slice(hd * X_HEAD_DIM, (hd + 1) * X_HEAD_DIM)
        s = _dot_nt(q[:, sl], k_ref[0, :, sl])
        e = jnp.exp(s - jnp.max(s, axis=-1, keepdims=True))
        p = e / jnp.sum(e, axis=-1, keepdims=True)
        outs.append(_dot(p.astype(BF16), v_ref[0, :, sl]).astype(BF16))
    o = jnp.concatenate(outs, axis=-1)
    o_ref[0] = x + _dot(o, wo_ref[...])


def _xattn(x, a, p, wa, wp, g, wq, kx, vx, wo, tm=1024):
    B, S, D = x.shape
    M = kx.shape[1]
    row = lambda n: pl.BlockSpec((1, tm, n), lambda b, i: (b, i, 0))
    full = lambda w: pl.BlockSpec(w.shape, lambda b, i: (0, 0))
    mspec = pl.BlockSpec((1, M, D), lambda b, i: (b, 0, 0))
    return pl.pallas_call(
        _xattn_kernel,
        grid=(B, S // tm),
        in_specs=[row(D), row(a.shape[2]), row(p.shape[2]), full(wa), full(wp),
                  full(g), full(wq), mspec, mspec, full(wo)],
        out_specs=row(D),
        out_shape=jax.ShapeDtypeStruct((B, S, D), F32),
        compiler_params=_params(("parallel", "parallel")),
        name="out_proj_xattn",
    )(x, a, p, wa, wp, g, wq, kx, vx, wo)


def _silu(x):
    return x * jax.nn.sigmoid(x)


def _ffn_kernel(x_ref, g_ref, wg_ref, wu_ref, wd_ref, o_ref):
    x = x_ref[...]
    h = _rms(x, g_ref[...]).astype(BF16)
    act = _silu(_dot(h, wg_ref[...])) * _dot(h, wu_ref[...])
    o_ref[...] = x + _dot(act.astype(BF16), wd_ref[...])


def _ffn(x2, g, wg, wu, wd, tm=512):
    T, D = x2.shape
    resident = lambda w: pl.BlockSpec(w.shape, lambda i: (0, 0), pipeline_mode=pl.Buffered(1))
    return pl.pallas_call(
        _ffn_kernel,
        grid=(T // tm,),
        in_specs=[pl.BlockSpec((tm, D), lambda i: (i, 0)),
                  pl.BlockSpec((1, D), lambda i: (0, 0)),
                  resident(wg), resident(wu), resident(wd)],
        out_specs=pl.BlockSpec((tm, D), lambda i: (i, 0)),
        out_shape=jax.ShapeDtypeStruct((T, D), F32),
        compiler_params=_params(("parallel",)),
        name="dense_swiglu",
    )(x2, g, wg, wu, wd)


MOE_CHUNK = 512
MOE_ROW_TILE = 512
assert MOE_ROW_TILE % MOE_CHUNK == 0


MOE_PARTS = 2


def _tile_meta(m):
    return m >> (2 * MOE_PARTS), [(m & (1 << k)) != 0 for k in range(2 * MOE_PARTS)]


def _router_kernel(x_ref, g_ref, wr_ref, tri_ref, h_ref, route_ref, gate_ref, before_ref, total_ref, run_ref):
    c = pl.program_id(0)

    @pl.when(c == 0)
    def _():
        run_ref[...] = jnp.zeros_like(run_ref)

    hf = _rms(x_ref[...], g_ref[...])
    h_ref[...] = hf.astype(BF16)
    logits = jnp.dot(hf, wr_ref[...], preferred_element_type=F32, precision=lax.Precision.HIGHEST)
    lane = lax.broadcasted_iota(jnp.int32, logits.shape, 1)
    logits = jnp.where(lane < N_EXPERTS, logits, -jnp.inf)
    v1 = jnp.max(logits, axis=-1, keepdims=True)
    i1 = jnp.min(jnp.where(logits == v1, lane, LANE), axis=-1, keepdims=True)
    rest = jnp.where(lane == i1, -jnp.inf, logits)
    v2 = jnp.max(rest, axis=-1, keepdims=True)
    i2 = jnp.min(jnp.where(rest == v2, lane, LANE), axis=-1, keepdims=True)
    e2 = jnp.exp(v2 - v1)
    den = 1.0 + e2
    gate_ref[0] = jnp.where(lane == 0, 1.0 / den, jnp.where(lane == 1, e2 / den, 0.0)).T[:SUBLANE]

    onehot = jnp.where((lane == i1) | (lane == i2), 1.0, 0.0)
    run = run_ref[...]
    rank = run + _dot(tri_ref[...], onehot.astype(BF16))
    r1 = jnp.sum(jnp.where(lane == i1, rank, 0.0), axis=-1, keepdims=True).astype(jnp.int32)
    r2 = jnp.sum(jnp.where(lane == i2, rank, 0.0), axis=-1, keepdims=True).astype(jnp.int32)
    route = jnp.where(lane == 0, i1, jnp.where(lane == 1, i2, jnp.where(lane == 2, r1, jnp.where(lane == 3, r2, 0))))
    route_ref[0] = route.T[:SUBLANE]
    before_ref[0] = run
    run = run + jnp.sum(onehot, axis=0, keepdims=True)
    run_ref[...] = run
    total_ref[...] = run


def _router(x2, g, w_router):
    T, D = x2.shape
    C = T // MOE_CHUNK
    tri = jnp.asarray(np.tril(np.ones((MOE_CHUNK, MOE_CHUNK), np.float32), -1), BF16)
    row = lambda n: pl.BlockSpec((MOE_CHUNK, n), lambda c: (c, 0))
    full = lambda a: pl.BlockSpec(a.shape, lambda c: (0, 0))
    return pl.pallas_call(
        _router_kernel,
        grid=(C,),
        in_specs=[row(D), full(g), full(w_router), full(tri)],
        out_specs=(row(D),
                   pl.BlockSpec((1, SUBLANE, MOE_CHUNK), lambda c: (c, 0, 0)),
                   pl.BlockSpec((1, SUBLANE, MOE_CHUNK), lambda c: (c, 0, 0)),
                   pl.BlockSpec((1, 1, LANE), lambda c: (c, 0, 0)),
                   pl.BlockSpec((1, LANE), lambda c: (0, 0))),
        out_shape=(jax.ShapeDtypeStruct((T, D), BF16),
                   jax.ShapeDtypeStruct((C, SUBLANE, MOE_CHUNK), jnp.int32),
                   jax.ShapeDtypeStruct((C, SUBLANE, MOE_CHUNK), F32),
                   jax.ShapeDtypeStruct((C, 1, LANE), F32),
                   jax.ShapeDtypeStruct((1, LANE), F32)),
        scratch_shapes=[pltpu.VMEM((1, LANE), F32)],
        compiler_params=_params(("arbitrary",)),
        name="moe_router",
    )(x2, g, w_router, tri)


def _dispatch_kernel(a_ref, h_ref, tok_ref, gate_ref, zlo_ref, zhi_ref, zglo_ref, zghi_ref,
                     lo_ref, hi_ref, glo_ref, ghi_ref):
    del zlo_ref, zhi_ref, zglo_ref, zghi_ref
    e = pl.program_id(0)
    c = pl.program_id(1)
    idx = e * pl.num_programs(1) + c
    a, touched = _tile_meta(a_ref[idx])
    first = (c == 0) | (a != _tile_meta(a_ref[jnp.maximum(idx - 1, 0)])[0])

    @pl.when(first)
    def _():
        lo_ref[...] = jnp.zeros_like(lo_ref)
        hi_ref[...] = jnp.zeros_like(hi_ref)
        glo_ref[...] = jnp.zeros_like(glo_ref)
        ghi_ref[...] = jnp.zeros_like(ghi_ref)

    rows_per_part = MOE_CHUNK // MOE_PARTS

    def scatter(tile, part, x_ref, g_ref):
        tok = tok_ref[0]
        gts = gate_ref[0]
        r0 = part * rows_per_part
        rows = tile * MOE_CHUNK + r0 + lax.broadcasted_iota(jnp.int32, (rows_per_part, MOE_CHUNK), 0)
        c1 = tok[0:1] == rows
        c2 = tok[1:2] == rows
        x_ref[r0:r0 + rows_per_part, :] += _dot(jnp.where(c1 | c2, 1.0, 0.0).astype(BF16),
                                                h_ref[...]).astype(x_ref.dtype)
        g_ref[r0:r0 + rows_per_part, :] += jnp.sum(jnp.where(c1, gts[0:1], 0.0) + jnp.where(c2, gts[1:2], 0.0),
                                                   axis=1, keepdims=True)

    for k, (x_ref, g_ref) in enumerate(((lo_ref, glo_ref), (hi_ref, ghi_ref))):
        for part in range(MOE_PARTS):
            pl.when(touched[k * MOE_PARTS + part])(functools.partial(scatter, a + k, part, x_ref, g_ref))


def _dispatch(a_ec, h, tok_rows, gate_rows, n_slots):
    T, D = h.shape
    C = T // MOE_CHUNK
    zx = jnp.zeros((n_slots, D), BF16)
    zg = jnp.zeros((n_slots, 1), F32)
    any_spec = pl.BlockSpec(memory_space=pl.ANY)
    lo = lambda n: pl.BlockSpec((MOE_CHUNK, n), lambda e, c, a: (a[e * C + c] >> (2 * MOE_PARTS), 0))
    hi = lambda n: pl.BlockSpec((MOE_CHUNK, n), lambda e, c, a: ((a[e * C + c] >> (2 * MOE_PARTS)) + 1, 0))
    return pl.pallas_call(
        _dispatch_kernel,
        grid_spec=pltpu.PrefetchScalarGridSpec(
            num_scalar_prefetch=1,
            grid=(N_EXPERTS, C),
            in_specs=[pl.BlockSpec((MOE_CHUNK, D), lambda e, c, a: (c, 0)),
                      pl.BlockSpec((1, SUBLANE, MOE_CHUNK), lambda e, c, a: (c, 0, 0)),
                      pl.BlockSpec((1, SUBLANE, MOE_CHUNK), lambda e, c, a: (c, 0, 0)),
                      any_spec, any_spec, any_spec, any_spec],
            out_specs=(lo(D), hi(D), lo(1), hi(1))),
        out_shape=(jax.ShapeDtypeStruct(zx.shape, BF16), jax.ShapeDtypeStruct(zx.shape, BF16),
                   jax.ShapeDtypeStruct(zg.shape, F32), jax.ShapeDtypeStruct(zg.shape, F32)),
        input_output_aliases={4: 0, 5: 1, 6: 2, 7: 3},
        compiler_params=_params(("arbitrary", "arbitrary")),
        name="moe_dispatch",
    )(a_ec, h, tok_rows, gate_rows, zx, zx, zg, zg)


def _expert_ffn_kernel(te_ref, nu_ref, lo_ref, hi_ref, glo_ref, ghi_ref, wg_ref, wu_ref, wd_ref, o_ref):
    del te_ref
    used = pl.program_id(0) < nu_ref[0]

    @pl.when(used)
    def _():
        x = lo_ref[...] + hi_ref[...]
        act = _silu(_dot(x, wg_ref[0])) * _dot(x, wu_ref[0])
        o_ref[...] = (_dot(act.astype(BF16), wd_ref[0]) * (glo_ref[...] + ghi_ref[...])).astype(o_ref.dtype)

    @pl.when(jnp.logical_not(used))
    def _():
        o_ref[...] = jnp.zeros_like(o_ref)


def _expert_ffn(tile_expert, n_used, xs_lo, xs_hi, gs_lo, gs_hi, wg, wu, wd, tr=256):
    N, D = xs_lo.shape
    sub = MOE_ROW_TILE // tr
    tile_expert = jnp.repeat(tile_expert, sub)
    n_used = n_used * sub
    row = lambda n: pl.BlockSpec((tr, n), lambda i, te, nu: (i, 0))
    expert = lambda w: pl.BlockSpec((1,) + w.shape[1:], lambda i, te, nu: (te[i], 0, 0))
    return pl.pallas_call(
        _expert_ffn_kernel,
        grid_spec=pltpu.PrefetchScalarGridSpec(
            num_scalar_prefetch=2,
            grid=(N // tr,),
            in_specs=[row(D), row(D), row(1), row(1), expert(wg), expert(wu), expert(wd)],
            out_specs=row(D)),
        out_shape=jax.ShapeDtypeStruct((N, D), BF16),
        compiler_params=_params(("arbitrary",)),
        name="moe_expert_ffn",
    )(tile_expert, n_used, xs_lo, xs_hi, gs_lo, gs_hi, wg, wu, wd)


def _combine_kernel(meta_ref, hidx_ref, x_ref, *refs):
    del hidx_ref
    E = N_EXPERTS
    ylo, yhi = refs[:E], refs[E:2 * E]
    s1_ref, s2_ref, gf_ref, o_ref, acc_ref = refs[2 * E:]
    c = pl.program_id(0)
    s1 = s1_ref[...]
    s2 = s2_ref[...]

    def pick(tile):
        parts = []
        for j in range(MOE_CHUNK // LANE):
            cols = tile * MOE_CHUNK + j * LANE + lax.broadcasted_iota(jnp.int32, (MOE_CHUNK, LANE), 1)
            parts.append(jnp.where((s1 == cols) | (s2 == cols), 1.0, 0.0).astype(BF16))
        return jnp.concatenate(parts, axis=1)

    metas = [_tile_meta(meta_ref[c * E + e]) for e in range(E)]
    total = x_ref[...]
    for (a, touched), y_ref in zip(metas, ylo):
        in_lo = functools.reduce(jnp.logical_or, touched[:MOE_PARTS])
        total = total + _dot(pick(jnp.where(in_lo, a, -1)), y_ref[...])
    acc_ref[...] = total

    for (a, touched), y_ref in zip(metas, yhi):
        @pl.when(functools.reduce(jnp.logical_or, touched[MOE_PARTS:]))
        def _(a=a, y_ref=y_ref):
            acc_ref[...] += _dot(pick(a + 1), y_ref[...])

    o_ref[...] = _rms(acc_ref[...], gf_ref[...])


def _combine(meta_ce, hi_tile_ce, x2, ys, slot1_rep, slot2_rep, g_final):
    T, D = x2.shape
    C = T // MOE_CHUNK
    E = N_EXPERTS
    lo_spec = lambda e: pl.BlockSpec((MOE_CHUNK, D), lambda c, m, h: (m[c * E + e] >> (2 * MOE_PARTS), 0))
    hi_spec = lambda e: pl.BlockSpec((MOE_CHUNK, D), lambda c, m, h: (h[c * E + e], 0))
    return pl.pallas_call(
        _combine_kernel,
        grid_spec=pltpu.PrefetchScalarGridSpec(
            num_scalar_prefetch=2,
            grid=(C,),
            in_specs=([pl.BlockSpec((MOE_CHUNK, D), lambda c, m, h: (c, 0))]
                      + [lo_spec(e) for e in range(E)] + [hi_spec(e) for e in range(E)]
                      + [pl.BlockSpec((MOE_CHUNK, LANE), lambda c, m, h: (c, 0)),
                         pl.BlockSpec((MOE_CHUNK, LANE), lambda c, m, h: (c, 0)),
                         pl.BlockSpec((1, D), lambda c, m, h: (0, 0))]),
            out_specs=pl.BlockSpec((MOE_CHUNK, D), lambda c, m, h: (c, 0)),
            scratch_shapes=[pltpu.VMEM((MOE_CHUNK, D), F32)]),
        out_shape=jax.ShapeDtypeStruct((T, D), F32),
        compiler_params=_params(("arbitrary",)),
        name="moe_combine_final_norm",
    )(meta_ce, hi_tile_ce, x2, *([ys] * (2 * E)), slot1_rep, slot2_rep, g_final)


def _rope_tables(S):
    inv = ROPE_THETA ** (-jnp.arange(ROPE_HALF, dtype=F32) / ROPE_HALF)
    ang = jnp.arange(S, dtype=F32)[:, None] * inv[None, :]
    cos, sin = jnp.cos(ang), jnp.sin(ang)
    pad = HEAD_DIM - ROPE_DIM
    cos_h = jnp.concatenate([cos, cos, jnp.ones((S, pad), F32)], axis=-1)
    sin_h = jnp.concatenate([-sin, sin, jnp.zeros((S, pad), F32)], axis=-1)
    reps = LANE // HEAD_DIM
    return jnp.tile(cos_h, (1, reps)), jnp.tile(sin_h, (1, reps))


def _pack_w_in(w):
    per_group = GROUP * N_BRANCH
    gates = w[:, GATE_OFF:GATE_OFF + N_KV_HEADS * per_group].reshape(-1, N_KV_HEADS, per_group)
    gates = jnp.pad(gates, ((0, 0), (0, LANE // GATE_ROWS - N_KV_HEADS), (0, GATE_ROWS - per_group)))
    gates = gates.reshape(-1, LANE)
    return jnp.concatenate([w[:, :GATE_OFF], gates, w[:, GATE_OFF + N_KV_HEADS * per_group:]], axis=1).astype(BF16)


def _mixer_heads(x, norm_g, w_in, pe_k, w1_k, b1_k, w2_k, pe_v, w1_v, b1_v, w2_v,
                 w_pool, pool_scale, cos_t, sin_t):
    B, S, D = x.shape
    qt, qrt, kcr, vcr, ks, vst, kw, vwt, gates, p = _mixer_proj(
        x, norm_g.reshape(1, D), _pack_w_in(w_in), cos_t, sin_t, w_pool.astype(BF16), pool_scale.reshape(1, -1))
    half = CMP_LEN // 2
    cw = half * HEAD_DIM
    pad_w2 = lambda w2: jnp.pad(w2, ((0, 0), (0, LANE - HEAD_DIM))).astype(BF16)
    kc, vct = _compress(
        kcr, vcr,
        pe_k.reshape(2, cw), w1_k.reshape(2, cw, CMP_HIDDEN).astype(BF16), b1_k.reshape(1, -1), pad_w2(w2_k),
        pe_v.reshape(2, cw), w1_v.reshape(2, cw, CMP_HIDDEN).astype(BF16), b1_v.reshape(1, -1), pad_w2(w2_v))
    return _nsa_attention(qt, qrt, kc, vct, ks, vst, kw, vwt, gates), p


def _moe_layer(x2, g_ffn, router, wg, wu, wd, g_final):
    T, D = x2.shape
    E = N_EXPERTS
    C = T // MOE_CHUNK
    i32 = jnp.int32
    w_router = jnp.pad(router, ((0, 0), (0, LANE - E)))
    h, route, gate_rows, before, total = _router(x2, g_ffn.reshape(1, D), w_router)

    counts = total[0, :E].astype(i32)
    tiles = (counts + MOE_ROW_TILE - 1) // MOE_ROW_TILE
    ends = jnp.cumsum(tiles)
    off = (ends - tiles) * MOE_ROW_TILE
    e1, e2, r1, r2 = route[:, 0], route[:, 1], route[:, 2], route[:, 3]
    off_of = lambda e: sum(jnp.where(e == k, off[k], 0) for k in range(E))
    slot1 = off_of(e1) + r1
    slot2 = off_of(e2) + r2
    first = off[None, :] + before[:, 0, :E].astype(i32)
    after = jnp.concatenate([before[1:, 0, :E], total[:, :E]], axis=0).astype(i32)
    last = off[None, :] + after - 1
    lo_tile = first // MOE_CHUNK
    tile_meta = lo_tile << (2 * MOE_PARTS)
    part_rows = MOE_CHUNK // MOE_PARTS
    for k in range(2 * MOE_PARTS):
        start = lo_tile * MOE_CHUNK + k * part_rows
        touched = (last >= first) & (first < start + part_rows) & (last >= start)
        tile_meta = tile_meta + (touched.astype(i32) << k)
    n_row_tiles = (2 * T) // MOE_ROW_TILE + E + 1 + MOE_CHUNK // MOE_ROW_TILE
    n_used = ends[-1:]
    tile_ids = jnp.minimum(jnp.arange(n_row_tiles, dtype=i32), n_used[0] - 1)
    tile_expert = jnp.minimum(jnp.sum(ends[None, :] <= tile_ids[:, None], axis=1), E - 1).astype(i32)

    zero = jnp.zeros_like(slot1)
    tok_rows = jnp.stack([slot1, slot2, zero, zero, zero, zero, zero, zero], axis=1)
    slot1_rep = jnp.broadcast_to(slot1.reshape(T, 1), (T, LANE))
    slot2_rep = jnp.broadcast_to(slot2.reshape(T, 1), (T, LANE))

    xs_lo, xs_hi, gs_lo, gs_hi = _dispatch(tile_meta.T.reshape(-1).astype(i32), h, tok_rows, gate_rows,
                                           n_row_tiles * MOE_ROW_TILE)
    ys = _expert_ffn(tile_expert, n_used.astype(i32), xs_lo, xs_hi, gs_lo, gs_hi,
                     wg.astype(BF16), wu.astype(BF16), wd.astype(BF16))
    in_hi = (tile_meta >> MOE_PARTS) & ((1 << MOE_PARTS) - 1) != 0
    hi_tile = lax.cummax(jnp.where(in_hi, lo_tile + 1, 0), axis=0)
    return _combine(tile_meta.reshape(-1).astype(i32), hi_tile.reshape(-1).astype(i32), x2, ys,
                    slot1_rep, slot2_rep, g_final.reshape(1, D))


def kernel(x, mem, norm_mix, w_in, cmp_pe_k, cmp_w1_k, cmp_b1_k, cmp_w2_k, cmp_pe_v, cmp_w1_v, cmp_b1_v, cmp_w2_v, w_pool, pool_scale, w_out, norm_x, norm_mem, wq_x, wk_x, wv_x, wo_x, norm_ffn, ffn_wg, ffn_wu, ffn_wd, moe_router, moe_wg, moe_wu, moe_wd, norm_final):
    B, S, D = x.shape
    depth = norm_mix.shape[0]
    assert depth == 2, "the final RMSNorm is fused into the expert layer, which must be the last one"
    cos_t, sin_t = _rope_tables(S)
    for layer in range(depth):
        a, p = _mixer_heads(x, norm_mix[layer], w_in[layer],
                            cmp_pe_k[layer], cmp_w1_k[layer], cmp_b1_k[layer], cmp_w2_k[layer],
                            cmp_pe_v[layer], cmp_w1_v[layer], cmp_b1_v[layer], cmp_w2_v[layer],
                            w_pool[layer], pool_scale[layer], cos_t, sin_t)
        wkv = jnp.concatenate([wk_x[layer], wv_x[layer]], axis=1).astype(BF16)
        kx, vx = _xattn_kv(mem, norm_mem[layer].reshape(1, D), wkv)
        wo_mix = w_out[layer].astype(BF16)
        x = _xattn(x, a, p, wo_mix[:NSA_WIDTH], wo_mix[NSA_WIDTH:], norm_x[layer].reshape(1, D),
                   wq_x[layer].astype(BF16), kx, vx, wo_x[layer].astype(BF16))
        x2 = x.reshape(B * S, D)
        j = layer // 2
        if layer % 2 == 0:
            x2 = _ffn(x2, norm_ffn[layer].reshape(1, D), ffn_wg[j].astype(BF16), ffn_wu[j].astype(BF16),
                      ffn_wd[j].astype(BF16))
        else:
            x2 = _moe_layer(x2, norm_ffn[layer], moe_router[j], moe_wg[j], moe_wu[j], moe_wd[j], norm_final)
        x = x2.reshape(B, S, D)
    return x
```

```python
import functools
import math

import numpy as np
import jax
import jax.numpy as jnp
from jax import lax
from jax.experimental import pallas as pl
from jax.experimental.pallas import tpu as pltpu

F32 = jnp.float32
BF16 = jnp.bfloat16

D_MODEL = 1024
HEAD_DIM = 64
N_HEADS = 8
N_KV_HEADS = 2
GROUP = N_HEADS // N_KV_HEADS
NSA_WIDTH = N_HEADS * HEAD_DIM
KV_WIDTH = N_KV_HEADS * HEAD_DIM
N_BRANCH = 3
GATE_ROWS = 16
POOL_WIDTH = 512
POOL_WINDOWS = (2, 4, 8, 16)
POOL_GROUP = 128
POOL_HALO = 16
ROPE_DIM = 16
ROPE_HALF = 8
ROPE_THETA = 500000.0
CMP_LEN = 32
CMP_STRIDE = 16
CMP_HIDDEN = 256
SEL_BLOCK = 64
SEL_TOP_N = 16
N_LOCAL_BLOCKS = 2
WINDOW = 512
X_HEADS = 4
X_HEAD_DIM = 256
N_EXPERTS = 8
EPS = 1e-6

LANE = 128
SUBLANE = 8
BF16_ROWS = 16
IN_PACKED = NSA_WIDTH + 6 * KV_WIDTH + LANE + POOL_WIDTH
GATE_OFF = NSA_WIDTH + 6 * KV_WIDTH
POOL_OFF = GATE_OFF + LANE

NEG = -1e30
TINY = float(np.finfo(np.float32).tiny)
VMEM_LIMIT = 56 * 1024 * 1024


def _dot(a, b):
    return jnp.dot(a, b, preferred_element_type=F32)


def _dot_nt(a, b):
    return lax.dot_general(a, b, (((1,), (1,)), ((), ())), preferred_element_type=F32)


def _rms(x, g):
    y = x * lax.rsqrt(jnp.mean(x * x, axis=-1, keepdims=True) + EPS)
    return y * g


def _params(sem, limit=VMEM_LIMIT):
    return pltpu.CompilerParams(dimension_semantics=sem, vmem_limit_bytes=limit)


def _mixer_proj_kernel(x_ref, g_ref, w_ref, cos_ref, sin_ref, wpool_ref, pscale_ref,
                       qt_ref, qrt_ref, kcr_ref, vcr_ref, ks_ref, vst_ref, kw_ref, vwt_ref,
                       gate_ref, p_ref, pool_buf):
    h = _rms(x_ref[0], g_ref[...]).astype(BF16)
    z = _dot(h, w_ref[...])
    tm = z.shape[0]
    cos = cos_ref[...]
    sin = sin_ref[...]
    lane = lax.broadcasted_iota(jnp.int32, (tm, LANE), 1)
    first = (lane & (HEAD_DIM - 1)) < ROPE_HALF
    scale = HEAD_DIM ** -0.5 * math.log2(math.e)

    def rope(xs):
        partner = jnp.where(first, pltpu.roll(xs, LANE - ROPE_HALF, 1), pltpu.roll(xs, ROPE_HALF, 1))
        return xs * cos + partner * sin

    for s in range(NSA_WIDTH // LANE):
        xs = z[:, s * LANE:(s + 1) * LANE]
        for src, ref in ((xs, qt_ref), (rope(xs), qrt_ref)):
            t = (src * scale).T.astype(BF16)
            ref[0, 2 * s] = t[:HEAD_DIM]
            ref[0, 2 * s + 1] = t[HEAD_DIM:]

    def kv_slab(i):
        return z[:, NSA_WIDTH + i * KV_WIDTH:NSA_WIDTH + (i + 1) * KV_WIDTH]

    for slab, ref in ((kv_slab(0), kcr_ref), (kv_slab(1), vcr_ref),
                      (rope(kv_slab(2)), ks_ref), (rope(kv_slab(4)), kw_ref)):
        for gg in range(N_KV_HEADS):
            ref[0, gg] = slab[:, gg * HEAD_DIM:(gg + 1) * HEAD_DIM].astype(ref.dtype)

    for slab, ref in ((kv_slab(3), vst_ref), (kv_slab(5), vwt_ref)):
        t = slab.T.astype(BF16)
        for gg in range(N_KV_HEADS):
            for k in range(tm // LANE):
                ref[0, gg, k] = t[gg * HEAD_DIM:(gg + 1) * HEAD_DIM, k * LANE:(k + 1) * LANE]

    sig_t = jax.nn.sigmoid(z[:, GATE_OFF:GATE_OFF + LANE]).T
    for gg in range(N_KV_HEADS):
        gate_ref[0, gg] = sig_t[gg * GATE_ROWS:(gg + 1) * GATE_ROWS]

    i = pl.program_id(1)

    @pl.when(i == 0)
    def _():
        pool_buf[0:POOL_HALO, :] = jnp.zeros((POOL_HALO, POOL_WIDTH), F32)

    @pl.when(i > 0)
    def _():
        pool_buf[0:POOL_HALO, :] = pool_buf[tm:tm + POOL_HALO, :]

    pool_buf[POOL_HALO:POOL_HALO + tm, :] = z[:, POOL_OFF:POOL_OFF + POOL_WIDTH]
    t1 = i * tm + lax.broadcasted_iota(jnp.int32, (tm, 1), 0) + 1
    for gi, w in enumerate(POOL_WINDOWS):
        cols = slice(gi * POOL_GROUP, (gi + 1) * POOL_GROUP)
        cur = pool_buf[POOL_HALO:POOL_HALO + tm, cols]
        tot = cur
        for k in range(1, w):
            tot = tot + pool_buf[POOL_HALO - k:POOL_HALO - k + tm, cols]
        d = tot / jnp.minimum(t1, w).astype(F32) - cur
        p_ref[0, :, cols] = (_dot(d.astype(BF16), wpool_ref[gi]) * pscale_ref[:, cols]).astype(p_ref.dtype)


def _mixer_proj(x, g, w_packed, cos_t, sin_t, w_pool, pool_scale, tm=1024):
    B, S, D = x.shape
    G = N_KV_HEADS
    sd = jax.ShapeDtypeStruct
    out_shape = (sd((B, N_HEADS, HEAD_DIM, S), BF16), sd((B, N_HEADS, HEAD_DIM, S), BF16),
                 sd((B, G, S, HEAD_DIM), F32), sd((B, G, S, HEAD_DIM), F32),
                 sd((B, G, S, HEAD_DIM), BF16), sd((B, G, S // LANE, HEAD_DIM, LANE), BF16),
                 sd((B, G, S, HEAD_DIM), BF16), sd((B, G, S // LANE, HEAD_DIM, LANE), BF16),
                 sd((B, G, GATE_ROWS, S), F32),
                 sd((B, S, POOL_WIDTH), BF16))
    qspec = pl.BlockSpec((1, N_HEADS, HEAD_DIM, tm), lambda b, i: (b, 0, 0, i))
    kspec = pl.BlockSpec((1, G, tm, HEAD_DIM), lambda b, i: (b, 0, i, 0))
    vspec = pl.BlockSpec((1, G, tm // LANE, HEAD_DIM, LANE), lambda b, i: (b, 0, i, 0, 0))
    out_specs = (qspec, qspec, kspec, kspec, kspec, vspec, kspec, vspec,
                 pl.BlockSpec((1, G, GATE_ROWS, tm), lambda b, i: (b, 0, 0, i)),
                 pl.BlockSpec((1, tm, POOL_WIDTH), lambda b, i: (b, i, 0)))
    return pl.pallas_call(
        _mixer_proj_kernel,
        grid=(B, S // tm),
        in_specs=[pl.BlockSpec((1, tm, D), lambda b, i: (b, i, 0)),
                  pl.BlockSpec((1, D), lambda b, i: (0, 0)),
                  pl.BlockSpec((D, IN_PACKED), lambda b, i: (0, 0)),
                  pl.BlockSpec((tm, LANE), lambda b, i: (i, 0)),
                  pl.BlockSpec((tm, LANE), lambda b, i: (i, 0)),
                  pl.BlockSpec(w_pool.shape, lambda b, i: (0, 0, 0)),
                  pl.BlockSpec((1, POOL_WIDTH), lambda b, i: (0, 0))],
        out_specs=out_specs,
        out_shape=out_shape,
        scratch_shapes=[pltpu.VMEM((tm + POOL_HALO, POOL_WIDTH), F32)],
        compiler_params=_params(("parallel", "arbitrary")),
        name="mixer_proj",
    )(x, g, w_packed, cos_t, sin_t, w_pool, pool_scale)


def _gelu_tanh(x):
    return 0.5 * x * (1.0 + jnp.tanh(math.sqrt(2.0 / math.pi) * (x + 0.044715 * (x * x * x))))


def _compress_kernel(kc_ref, vc_ref, pek_ref, w1k_ref, b1k_ref, w2k_ref,
                     pev_ref, w1v_ref, b1v_ref, w2v_ref, ko_ref, vto_ref):
    def comp(x_ref, pe_ref, w1_ref, b1_ref, w2_ref):
        n = x_ref.shape[2] // CMP_STRIDE
        x = jnp.concatenate([x_ref[0, 0, pl.ds(l, n, stride=CMP_STRIDE), :] for l in range(CMP_STRIDE)], axis=1)
        a = _dot((x + pe_ref[0:1, :]).astype(BF16), w1_ref[0])
        b = _dot((x + pe_ref[1:2, :]).astype(BF16), w1_ref[1])
        pre = a + pltpu.roll(b, n - 1, 0) + b1_ref[...]
        return _dot(_gelu_tanh(pre).astype(BF16), w2_ref[...])

    ko_ref[0, 0] = comp(kc_ref, pek_ref, w1k_ref, b1k_ref, w2k_ref)[:, :HEAD_DIM].astype(ko_ref.dtype)
    vto_ref[0, 0] = comp(vc_ref, pev_ref, w1v_ref, b1v_ref, w2v_ref).T[:HEAD_DIM].astype(vto_ref.dtype)


def _compress(kcr, vcr, pek, w1k, b1k, w2k, pev, w1v, b1v, w2v):
    B, G, S, _ = kcr.shape
    n = S // CMP_STRIDE
    xspec = pl.BlockSpec((1, 1, S, HEAD_DIM), lambda b, g: (b, g, 0, 0))
    full = lambda a: pl.BlockSpec(a.shape, lambda b, g: (0,) * a.ndim)
    return pl.pallas_call(
        _compress_kernel,
        grid=(B, G),
        in_specs=[xspec, xspec, full(pek), full(w1k), full(b1k), full(w2k),
                  full(pev), full(w1v), full(b1v), full(w2v)],
        out_specs=(pl.BlockSpec((1, 1, n, HEAD_DIM), lambda b, g: (b, g, 0, 0)),
                   pl.BlockSpec((1, 1, HEAD_DIM, n), lambda b, g: (b, g, 0, 0))),
        out_shape=(jax.ShapeDtypeStruct((B, G, n, HEAD_DIM), BF16),
                   jax.ShapeDtypeStruct((B, G, HEAD_DIM, n), BF16)),
        compiler_params=_params(("parallel", "parallel")),
        name="compress_kv",
    )(kcr, vcr, pek, w1k, b1k, w2k, pev, w1v, b1v, w2v)


def _nsa_kernel(qt_ref, qrt_ref, kc_ref, vct_ref, ks_ref, vst_ref, kw_ref, vwt_ref, gate_ref,
                covt_ref, o_ref, selbias_ref, rank_ref, *, tq, tk, tk_narrow):
    qt = pl.program_id(2)
    q0 = qt * tq
    n_cmp = kc_ref.shape[2]
    n_blk = covt_ref.shape[0]
    heads = lambda ref: jnp.concatenate([ref[0, hh] for hh in range(GROUP)], axis=1)
    q_t = heads(qt_ref)
    qr_t = heads(qrt_ref)
    t_lane = q0 + lax.broadcasted_iota(jnp.int32, (1, tq), 1)
    per_head = lambda a: [a[:, hh * tq:(hh + 1) * tq] for hh in range(GROUP)]
    all_heads = lambda a: jnp.concatenate([a] * GROUP, axis=1)

    sc = _dot(kc_ref[0, 0], q_t)
    cmp_end = lax.broadcasted_iota(jnp.int32, (n_cmp, 1), 0) * CMP_STRIDE + (CMP_LEN - 1)
    valid = cmp_end <= t_lane
    pcs = []
    for s_h in per_head(sc):
        s_h = jnp.where(valid, s_h, NEG)
        e = jnp.where(valid, jnp.exp2(s_h - jnp.max(s_h, axis=0, keepdims=True)), 0.0)
        pcs.append(e * (1.0 / jnp.maximum(jnp.sum(e, axis=0, keepdims=True), TINY)))
    oc_t = _dot(vct_ref[0, 0], jnp.concatenate(pcs, axis=1).astype(BF16))

    psum = pcs[0] + pcs[1] + pcs[2] + pcs[3]
    p_hi = psum.astype(BF16)
    r1 = psum - p_hi.astype(F32)
    p_mid = r1.astype(BF16)
    p_lo = (r1 - p_mid.astype(F32)).astype(BF16)
    covt = covt_ref[...]
    imp_t = _dot(covt, p_hi) + _dot(covt, p_mid) + _dot(covt, p_lo)

    jb = lax.broadcasted_iota(jnp.int32, (n_blk, tq), 0)
    tb = (q0 + lax.broadcasted_iota(jnp.int32, (n_blk, tq), 1)) // SEL_BLOCK
    dist = tb - jb
    forced = (jb == 0) | ((dist >= 0) & (dist < N_LOCAL_BLOCKS))
    score = jnp.where(jb > tb, -jnp.inf, jnp.where(forced, jnp.inf, imp_t))
    rank_ref[...] = jnp.zeros((n_blk, tq), jnp.int32)
    n_live = (q0 + tq - 1) // SEL_BLOCK + 1
    for g8 in range(n_blk // SUBLANE):
        @pl.when(g8 * SUBLANE < n_live)
        def _(g8=g8):
            rank = rank_ref[...]
            for jp in range(g8 * SUBLANE, (g8 + 1) * SUBLANE):
                rowv = score[jp:jp + 1, :]
                beats = (rowv > score) | ((rowv == score) & (jb > jp))
                rank = rank + jnp.where(beats, 1, 0)
            rank_ref[...] = rank
    selbias_ref[...] = jnp.where(rank_ref[...] < min(SEL_TOP_N, n_blk), 0.0, NEG)

    R = GROUP * tq

    def with_ones(v_t):
        extra = jnp.where(lax.broadcasted_iota(jnp.int32, (BF16_ROWS, v_t.shape[1]), 0) == 0, 1.0, 0.0)
        return jnp.concatenate([v_t, extra.astype(BF16)], axis=0)

    def sel_chunk(c, carry, width, causal):
        m_i, l_i, acc = carry
        bpc = width // SEL_BLOCK
        vpc = width // LANE
        start = pl.multiple_of(c * width, width)
        k_c = ks_ref[0, 0, pl.ds(start, width), :]
        v_t = jnp.concatenate([vst_ref[0, 0, c * vpc + k] for k in range(vpc)], axis=1)
        rows = [jnp.broadcast_to(selbias_ref[pl.ds(c * bpc + j, 1), :], (SEL_BLOCK, tq)) for j in range(bpc)]
        bias = jnp.concatenate(rows, axis=0)
        if causal:
            kpos = start + lax.broadcasted_iota(jnp.int32, (width, 1), 0)
            bias = jnp.where(kpos <= t_lane, bias, NEG)
        s = _dot(k_c, qr_t) + all_heads(bias)
        m_new = jnp.maximum(m_i, jnp.max(s, axis=0, keepdims=True))
        alpha = jnp.exp2(m_i - m_new)
        pv = _dot(with_ones(v_t), jnp.exp2(s - m_new).astype(BF16))
        return m_new, alpha * l_i + pv[HEAD_DIM:HEAD_DIM + 1], alpha * acc + pv[:HEAD_DIM]

    init = (jnp.full((1, R), NEG, F32), jnp.zeros((1, R), F32), jnp.zeros((HEAD_DIM, R), F32))
    n_wide = q0 // tk
    j_diag = q0 // tk_narrow
    carry = lax.fori_loop(0, n_wide, lambda c, cr: sel_chunk(c, cr, tk, False), init)
    carry = lax.fori_loop(n_wide * (tk // tk_narrow), j_diag, lambda c, cr: sel_chunk(c, cr, tk_narrow, False), carry)
    _, l_s, acc_s = sel_chunk(j_diag, carry, tk_narrow, True)
    os_t = acc_s * (1.0 / jnp.maximum(l_s, TINY))

    wk = WINDOW + tq
    w0 = pl.multiple_of(jnp.maximum(q0 - WINDOW, 0), LANE)
    k_w = kw_ref[0, 0, pl.ds(w0, wk), :]
    v_t = jnp.concatenate([vwt_ref[0, 0, w0 // LANE + k] for k in range(wk // LANE)], axis=1)
    diff = t_lane - (w0 + lax.broadcasted_iota(jnp.int32, (wk, 1), 0))
    bias = jnp.where((diff >= 0) & (diff < WINDOW), 0.0, NEG)
    s = _dot(k_w, qr_t) + all_heads(bias)
    pv = _dot(with_ones(v_t), jnp.exp2(s - jnp.max(s, axis=0, keepdims=True)).astype(BF16))
    ow_t = pv[:HEAD_DIM] * (1.0 / pv[HEAD_DIM:HEAD_DIM + 1])

    gt = gate_ref[0, 0]
    outs = []
    for hh, (a, b, c) in enumerate(zip(per_head(oc_t), per_head(os_t), per_head(ow_t))):
        r = hh * N_BRANCH
        outs.append(gt[r:r + 1] * a + gt[r + 1:r + 2] * b + gt[r + 2:r + 3] * c)
    o_ref[0] = jnp.concatenate(outs, axis=0).T.astype(o_ref.dtype)


def _cover_table(S):
    n_cmp = S // CMP_STRIDE
    n_blk = S // SEL_BLOCK
    cs = np.arange(n_cmp) * CMP_STRIDE
    ss = np.arange(n_blk) * SEL_BLOCK
    cover_t = ((cs[None, :] < ss[:, None] + SEL_BLOCK) & (cs[None, :] + CMP_LEN > ss[:, None]))
    cover_t[:, n_cmp - 1] = False
    return jnp.asarray(cover_t, BF16)


def _nsa_attention(qt, qrt, kc, vct, ks, vst, kw, vwt, gates, tq=512, tk=1024):
    B, _, _, S = qt.shape
    tk = min(tk, S)
    assert tq % LANE == 0 and tk % tq == 0 and S % tk == 0 and S >= WINDOW + tq
    covt = _cover_table(S)
    n_cmp = kc.shape[2]
    n_blk = covt.shape[0]
    qspec = pl.BlockSpec((1, GROUP, HEAD_DIM, tq), lambda b, g, i: (b, g, 0, i))
    kspec = lambda a: pl.BlockSpec((1, 1) + a.shape[2:], lambda b, g, i: (b, g, 0, 0))
    vspec = pl.BlockSpec((1, 1, S // LANE, HEAD_DIM, LANE), lambda b, g, i: (b, g, 0, 0, 0))
    return pl.pallas_call(
        functools.partial(_nsa_kernel, tq=tq, tk=tk, tk_narrow=min(tk, 2 * tq)),
        grid=(B, N_KV_HEADS, S // tq),
        in_specs=[qspec, qspec,
                  pl.BlockSpec((1, 1, n_cmp, HEAD_DIM), lambda b, g, i: (b, g, 0, 0)),
                  pl.BlockSpec((1, 1, HEAD_DIM, n_cmp), lambda b, g, i: (b, g, 0, 0)),
                  kspec(ks), vspec, kspec(kw), vspec,
                  pl.BlockSpec((1, 1, GATE_ROWS, tq), lambda b, g, i: (b, g, 0, i)),
                  pl.BlockSpec(covt.shape, lambda b, g, i: (0, 0))],
        out_specs=pl.BlockSpec((1, tq, GROUP * HEAD_DIM), lambda b, g, i: (b, i, g)),
        out_shape=jax.ShapeDtypeStruct((B, S, NSA_WIDTH), BF16),
        scratch_shapes=[pltpu.VMEM((n_blk, tq), F32), pltpu.VMEM((n_blk, tq), jnp.int32)],
        compiler_params=_params(("parallel", "parallel", "parallel")),
        name="nsa_attention",
    )(qt, qrt, kc, vct, ks, vst, kw, vwt, gates, covt)


def _xkv_kernel(m_ref, g_ref, w_ref, k_ref, v_ref):
    h = _rms(m_ref[0], g_ref[...]).astype(BF16)
    kv = _dot(h, w_ref[...])
    d = k_ref.shape[2]
    k_ref[0] = kv[:, :d].astype(k_ref.dtype)
    v_ref[0] = kv[:, d:].astype(v_ref.dtype)


def _xattn_kv(mem, g, wkv):
    B, M, D = mem.shape
    ospec = pl.BlockSpec((1, M, D), lambda b: (b, 0, 0))
    oshape = jax.ShapeDtypeStruct((B, M, D), BF16)
    return pl.pallas_call(
        _xkv_kernel,
        grid=(B,),
        in_specs=[pl.BlockSpec((1, M, D), lambda b: (b, 0, 0)),
                  pl.BlockSpec((1, D), lambda b: (0, 0)),
                  pl.BlockSpec(wkv.shape, lambda b: (0, 0))],
        out_specs=(ospec, ospec),
        out_shape=(oshape, oshape),
        compiler_params=_params(("parallel",)),
        name="xattn_kv",
    )(mem, g, wkv)


def _xattn_kernel(x_ref, a_ref, p_ref, wa_ref, wp_ref, g_ref, wq_ref, k_ref, v_ref, wo_ref, o_ref):
    x = x_ref[0] + _dot(a_ref[0], wa_ref[...]) + _dot(p_ref[0], wp_ref[...])
    h = _rms(x, g_ref[...]).astype(BF16)
    scale = X_HEAD_DIM ** -0.5
    q = (_dot(h, wq_ref[...]) * scale).astype(BF16)
    outs = []
    for hd in range(X_HEADS):
        sl = slice(hd * X_HEAD_DIM, (hd + 1) * X_HEAD_DIM)
        s = _dot_nt(q[:, sl], k_ref[0, :, sl])
        e = jnp.exp(s - jnp.max(s, axis=-1, keepdims=True))
        p = e / jnp.sum(e, axis=-1, keepdims=True)
        outs.append(_dot(p.astype(BF16), v_ref[0, :, sl]).astype(BF16))
    o = jnp.concatenate(outs, axis=-1)
    o_ref[0] = x + _dot(o, wo_ref[...])


def _xattn(x, a, p, wa, wp, g, wq, kx, vx, wo, tm=1024):
    B, S, D = x.shape
    M = kx.shape[1]
    row = lambda n: pl.BlockSpec((1, tm, n), lambda b, i: (b, i, 0))
    full = lambda w: pl.BlockSpec(w.shape, lambda b, i: (0, 0))
    mspec = pl.BlockSpec((1, M, D), lambda b, i: (b, 0, 0))
    return pl.pallas_call(
        _xattn_kernel,
        grid=(B, S // tm),
        in_specs=[row(D), row(a.shape[2]), row(p.shape[2]), full(wa), full(wp),
                  full(g), full(wq), mspec, mspec, full(wo)],
        out_specs=row(D),
        out_shape=jax.ShapeDtypeStruct((B, S, D), F32),
        compiler_params=_params(("parallel", "parallel")),
        name="out_proj_xattn",
    )(x, a, p, wa, wp, g, wq, kx, vx, wo)


def _silu(x):
    return x * jax.nn.sigmoid(x)


def _ffn_kernel(x_ref, g_ref, wg_ref, wu_ref, wd_ref, o_ref):
    x = x_ref[...]
    h = _rms(x, g_ref[...]).astype(BF16)
    act = _silu(_dot(h, wg_ref[...])) * _dot(h, wu_ref[...])
    o_ref[...] = x + _dot(act.astype(BF16), wd_ref[...])


def _ffn(x2, g, wg, wu, wd, tm=512):
    T, D = x2.shape
    resident = lambda w: pl.BlockSpec(w.shape, lambda i: (0, 0), pipeline_mode=pl.Buffered(1))
    return pl.pallas_call(
        _ffn_kernel,
        grid=(T // tm,),
        in_specs=[pl.BlockSpec((tm, D), lambda i: (i, 0)),
                  pl.BlockSpec((1, D), lambda i: (0, 0)),
                  resident(wg), resident(wu), resident(wd)],
        out_specs=pl.BlockSpec((tm, D), lambda i: (i, 0)),
        out_shape=jax.ShapeDtypeStruct((T, D), F32),
        compiler_params=_params(("parallel",)),
        name="dense_swiglu",
    )(x2, g, wg, wu, wd)


MOE_CHUNK = 512
MOE_ROW_TILE = 512
assert MOE_ROW_TILE % MOE_CHUNK == 0


MOE_PARTS = 2


def _tile_meta(m):
    return m >> (2 * MOE_PARTS), [(m & (1 << k)) != 0 for k in range(2 * MOE_PARTS)]


def _router_kernel(x_ref, g_ref, wr_ref, tri_ref, h_ref, route_ref, gate_ref, before_ref, total_ref, run_ref):
    c = pl.program_id(0)

    @pl.when(c == 0)
    def _():
        run_ref[...] = jnp.zeros_like(run_ref)

    hf = _rms(x_ref[...], g_ref[...])
    h_ref[...] = hf.astype(BF16)
    logits = jnp.dot(hf, wr_ref[...], preferred_element_type=F32, precision=lax.Precision.HIGHEST)
    lane = lax.broadcasted_iota(jnp.int32, logits.shape, 1)
    logits = jnp.where(lane < N_EXPERTS, logits, -jnp.inf)
    v1 = jnp.max(logits, axis=-1, keepdims=True)
    i1 = jnp.min(jnp.where(logits == v1, lane, LANE), axis=-1, keepdims=True)
    rest = jnp.where(lane == i1, -jnp.inf, logits)
    v2 = jnp.max(rest, axis=-1, keepdims=True)
    i2 = jnp.min(jnp.where(rest == v2, lane, LANE), axis=-1, keepdims=True)
    e2 = jnp.exp(v2 - v1)
    den = 1.0 + e2
    gate_ref[0] = jnp.where(lane == 0, 1.0 / den, jnp.where(lane == 1, e2 / den, 0.0)).T[:SUBLANE]

    onehot = jnp.where((lane == i1) | (lane == i2), 1.0, 0.0)
    run = run_ref[...]
    rank = run + _dot(tri_ref[...], onehot.astype(BF16))
    r1 = jnp.sum(jnp.where(lane == i1, rank, 0.0), axis=-1, keepdims=True).astype(jnp.int32)
    r2 = jnp.sum(jnp.where(lane == i2, rank, 0.0), axis=-1, keepdims=True).astype(jnp.int32)
    route = jnp.where(lane == 0, i1, jnp.where(lane == 1, i2, jnp.where(lane == 2, r1, jnp.where(lane == 3, r2, 0))))
    route_ref[0] = route.T[:SUBLANE]
    before_ref[0] = run
    run = run + jnp.sum(onehot, axis=0, keepdims=True)
    run_ref[...] = run
    total_ref[...] = run


def _router(x2, g, w_router):
    T, D = x2.shape
    C = T // MOE_CHUNK
    tri = jnp.asarray(np.tril(np.ones((MOE_CHUNK, MOE_CHUNK), np.float32), -1), BF16)
    row = lambda n: pl.BlockSpec((MOE_CHUNK, n), lambda c: (c, 0))
    full = lambda a: pl.BlockSpec(a.shape, lambda c: (0, 0))
    return pl.pallas_call(
        _router_kernel,
        grid=(C,),
        in_specs=[row(D), full(g), full(w_router), full(tri)],
        out_specs=(row(D),
                   pl.BlockSpec((1, SUBLANE, MOE_CHUNK), lambda c: (c, 0, 0)),
                   pl.BlockSpec((1, SUBLANE, MOE_CHUNK), lambda c: (c, 0, 0)),
                   pl.BlockSpec((1, 1, LANE), lambda c: (c, 0, 0)),
                   pl.BlockSpec((1, LANE), lambda c: (0, 0))),
        out_shape=(jax.ShapeDtypeStruct((T, D), BF16),
                   jax.ShapeDtypeStruct((C, SUBLANE, MOE_CHUNK), jnp.int32),
                   jax.ShapeDtypeStruct((C, SUBLANE, MOE_CHUNK), F32),
                   jax.ShapeDtypeStruct((C, 1, LANE), F32),
                   jax.ShapeDtypeStruct((1, LANE), F32)),
        scratch_shapes=[pltpu.VMEM((1, LANE), F32)],
        compiler_params=_params(("arbitrary",)),
        name="moe_router",
    )(x2, g, w_router, tri)


def _dispatch_kernel(a_ref, h_ref, tok_ref, gate_ref, zlo_ref, zhi_ref, zglo_ref, zghi_ref,
                     lo_ref, hi_ref, glo_ref, ghi_ref):
    del zlo_ref, zhi_ref, zglo_ref, zghi_ref
    e = pl.program_id(0)
    c = pl.program_id(1)
    idx = e * pl.num_programs(1) + c
    a, touched = _tile_meta(a_ref[idx])
    first = (c == 0) | (a != _tile_meta(a_ref[jnp.maximum(idx - 1, 0)])[0])

    @pl.when(first)
    def _():
        lo_ref[...] = jnp.zeros_like(lo_ref)
        hi_ref[...] = jnp.zeros_like(hi_ref)
        glo_ref[...] = jnp.zeros_like(glo_ref)
        ghi_ref[...] = jnp.zeros_like(ghi_ref)

    rows_per_part = MOE_CHUNK // MOE_PARTS

    def scatter(tile, part, x_ref, g_ref):
        tok = tok_ref[0]
        gts = gate_ref[0]
        r0 = part * rows_per_part
        rows = tile * MOE_CHUNK + r0 + lax.broadcasted_iota(jnp.int32, (rows_per_part, MOE_CHUNK), 0)
        c1 = tok[0:1] == rows
        c2 = tok[1:2] == rows
        x_ref[r0:r0 + rows_per_part, :] += _dot(jnp.where(c1 | c2, 1.0, 0.0).astype(BF16),
                                                h_ref[...]).astype(x_ref.dtype)
        g_ref[r0:r0 + rows_per_part, :] += jnp.sum(jnp.where(c1, gts[0:1], 0.0) + jnp.where(c2, gts[1:2], 0.0),
                                                   axis=1, keepdims=True)

    for k, (x_ref, g_ref) in enumerate(((lo_ref, glo_ref), (hi_ref, ghi_ref))):
        for part in range(MOE_PARTS):
            pl.when(touched[k * MOE_PARTS + part])(functools.partial(scatter, a + k, part, x_ref, g_ref))


def _dispatch(a_ec, h, tok_rows, gate_rows, n_slots):
    T, D = h.shape
    C = T // MOE_CHUNK
    zx = jnp.zeros((n_slots, D), BF16)
    zg = jnp.zeros((n_slots, 1), F32)
    any_spec = pl.BlockSpec(memory_space=pl.ANY)
    lo = lambda n: pl.BlockSpec((MOE_CHUNK, n), lambda e, c, a: (a[e * C + c] >> (2 * MOE_PARTS), 0))
    hi = lambda n: pl.BlockSpec((MOE_CHUNK, n), lambda e, c, a: ((a[e * C + c] >> (2 * MOE_PARTS)) + 1, 0))
    return pl.pallas_call(
        _dispatch_kernel,
        grid_spec=pltpu.PrefetchScalarGridSpec(
            num_scalar_prefetch=1,
            grid=(N_EXPERTS, C),
            in_specs=[pl.BlockSpec((MOE_CHUNK, D), lambda e, c, a: (c, 0)),
                      pl.BlockSpec((1, SUBLANE, MOE_CHUNK), lambda e, c, a: (c, 0, 0)),
                      pl.BlockSpec((1, SUBLANE, MOE_CHUNK), lambda e, c, a: (c, 0, 0)),
                      any_spec, any_spec, any_spec, any_spec],
            out_specs=(lo(D), hi(D), lo(1), hi(1))),
        out_shape=(jax.ShapeDtypeStruct(zx.shape, BF16), jax.ShapeDtypeStruct(zx.shape, BF16),
                   jax.ShapeDtypeStruct(zg.shape, F32), jax.ShapeDtypeStruct(zg.shape, F32)),
        input_output_aliases={4: 0, 5: 1, 6: 2, 7: 3},
        compiler_params=_params(("arbitrary", "arbitrary")),
        name="moe_dispatch",
    )(a_ec, h, tok_rows, gate_rows, zx, zx, zg, zg)


def _expert_ffn_kernel(te_ref, nu_ref, lo_ref, hi_ref, glo_ref, ghi_ref, wg_ref, wu_ref, wd_ref, o_ref):
    del te_ref
    used = pl.program_id(0) < nu_ref[0]

    @pl.when(used)
    def _():
        x = lo_ref[...] + hi_ref[...]
        act = _silu(_dot(x, wg_ref[0])) * _dot(x, wu_ref[0])
        o_ref[...] = (_dot(act.astype(BF16), wd_ref[0]) * (glo_ref[...] + ghi_ref[...])).astype(o_ref.dtype)

    @pl.when(jnp.logical_not(used))
    def _():
        o_ref[...] = jnp.zeros_like(o_ref)


def _expert_ffn(tile_expert, n_used, xs_lo, xs_hi, gs_lo, gs_hi, wg, wu, wd, tr=256):
    N, D = xs_lo.shape
    sub = MOE_ROW_TILE // tr
    tile_expert = jnp.repeat(tile_expert, sub)
    n_used = n_used * sub
    row = lambda n: pl.BlockSpec((tr, n), lambda i, te, nu: (i, 0))
    expert = lambda w: pl.BlockSpec((1,) + w.shape[1:], lambda i, te, nu: (te[i], 0, 0))
    return pl.pallas_call(
        _expert_ffn_kernel,
        grid_spec=pltpu.PrefetchScalarGridSpec(
            num_scalar_prefetch=2,
            grid=(N // tr,),
            in_specs=[row(D), row(D), row(1), row(1), expert(wg), expert(wu), expert(wd)],
            out_specs=row(D)),
        out_shape=jax.ShapeDtypeStruct((N, D), BF16),
        compiler_params=_params(("arbitrary",)),
        name="moe_expert_ffn",
    )(tile_expert, n_used, xs_lo, xs_hi, gs_lo, gs_hi, wg, wu, wd)


def _combine_kernel(meta_ref, hidx_ref, x_ref, *refs):
    del hidx_ref
    E = N_EXPERTS
    ylo, yhi = refs[:E], refs[E:2 * E]
    s1_ref, s2_ref, gf_ref, o_ref, acc_ref = refs[2 * E:]
    c = pl.program_id(0)
    s1 = s1_ref[...]
    s2 = s2_ref[...]

    def pick(tile):
        parts = []
        for j in range(MOE_CHUNK // LANE):
            cols = tile * MOE_CHUNK + j * LANE + lax.broadcasted_iota(jnp.int32, (MOE_CHUNK, LANE), 1)
            parts.append(jnp.where((s1 == cols) | (s2 == cols), 1.0, 0.0).astype(BF16))
        return jnp.concatenate(parts, axis=1)

    metas = [_tile_meta(meta_ref[c * E + e]) for e in range(E)]
    total = x_ref[...]
    for (a, touched), y_ref in zip(metas, ylo):
        in_lo = functools.reduce(jnp.logical_or, touched[:MOE_PARTS])
        total = total + _dot(pick(jnp.where(in_lo, a, -1)), y_ref[...])
    acc_ref[...] = total

    for (a, touched), y_ref in zip(metas, yhi):
        @pl.when(functools.reduce(jnp.logical_or, touched[MOE_PARTS:]))
        def _(a=a, y_ref=y_ref):
            acc_ref[...] += _dot(pick(a + 1), y_ref[...])

    o_ref[...] = _rms(acc_ref[...], gf_ref[...])


def _combine(meta_ce, hi_tile_ce, x2, ys, slot1_rep, slot2_rep, g_final):
    T, D = x2.shape
    C = T // MOE_CHUNK
    E = N_EXPERTS
    lo_spec = lambda e: pl.BlockSpec((MOE_CHUNK, D), lambda c, m, h: (m[c * E + e] >> (2 * MOE_PARTS), 0))
    hi_spec = lambda e: pl.BlockSpec((MOE_CHUNK, D), lambda c, m, h: (h[c * E + e], 0))
    return pl.pallas_call(
        _combine_kernel,
        grid_spec=pltpu.PrefetchScalarGridSpec(
            num_scalar_prefetch=2,
            grid=(C,),
            in_specs=([pl.BlockSpec((MOE_CHUNK, D), lambda c, m, h: (c, 0))]
                      + [lo_spec(e) for e in range(E)] + [hi_spec(e) for e in range(E)]
                      + [pl.BlockSpec((MOE_CHUNK, LANE), lambda c, m, h: (c, 0)),
                         pl.BlockSpec((MOE_CHUNK, LANE), lambda c, m, h: (c, 0)),
                         pl.BlockSpec((1, D), lambda c, m, h: (0, 0))]),
            out_specs=pl.BlockSpec((MOE_CHUNK, D), lambda c, m, h: (c, 0)),
            scratch_shapes=[pltpu.VMEM((MOE_CHUNK, D), F32)]),
        out_shape=jax.ShapeDtypeStruct((T, D), F32),
        compiler_params=_params(("arbitrary",)),
        name="moe_combine_final_norm",
    )(meta_ce, hi_tile_ce, x2, *([ys] * (2 * E)), slot1_rep, slot2_rep, g_final)


def _rope_tables(S):
    inv = ROPE_THETA ** (-jnp.arange(ROPE_HALF, dtype=F32) / ROPE_HALF)
    ang = jnp.arange(S, dtype=F32)[:, None] * inv[None, :]
    cos, sin = jnp.cos(ang), jnp.sin(ang)
    pad = HEAD_DIM - ROPE_DIM
    cos_h = jnp.concatenate([cos, cos, jnp.ones((S, pad), F32)], axis=-1)
    sin_h = jnp.concatenate([-sin, sin, jnp.zeros((S, pad), F32)], axis=-1)
    reps = LANE // HEAD_DIM
    return jnp.tile(cos_h, (1, reps)), jnp.tile(sin_h, (1, reps))


def _pack_w_in(w):
    per_group = GROUP * N_BRANCH
    gates = w[:, GATE_OFF:GATE_OFF + N_KV_HEADS * per_group].reshape(-1, N_KV_HEADS, per_group)
    gates = jnp.pad(gates, ((0, 0), (0, LANE // GATE_ROWS - N_KV_HEADS), (0, GATE_ROWS - per_group)))
    gates = gates.reshape(-1, LANE)
    return jnp.concatenate([w[:, :GATE_OFF], gates, w[:, GATE_OFF + N_KV_HEADS * per_group:]], axis=1).astype(BF16)


def _mixer_heads(x, norm_g, w_in, pe_k, w1_k, b1_k, w2_k, pe_v, w1_v, b1_v, w2_v,
                 w_pool, pool_scale, cos_t, sin_t):
    B, S, D = x.shape
    qt, qrt, kcr, vcr, ks, vst, kw, vwt, gates, p = _mixer_proj(
        x, norm_g.reshape(1, D), _pack_w_in(w_in), cos_t, sin_t, w_pool.astype(BF16), pool_scale.reshape(1, -1))
    half = CMP_LEN // 2
    cw = half * HEAD_DIM
    pad_w2 = lambda w2: jnp.pad(w2, ((0, 0), (0, LANE - HEAD_DIM))).astype(BF16)
    kc, vct = _compress(
        kcr, vcr,
        pe_k.reshape(2, cw), w1_k.reshape(2, cw, CMP_HIDDEN).astype(BF16), b1_k.reshape(1, -1), pad_w2(w2_k),
        pe_v.reshape(2, cw), w1_v.reshape(2, cw, CMP_HIDDEN).astype(BF16), b1_v.reshape(1, -1), pad_w2(w2_v))
    return _nsa_attention(qt, qrt, kc, vct, ks, vst, kw, vwt, gates), p


def _moe_layer(x2, g_ffn, router, wg, wu, wd, g_final):
    T, D = x2.shape
    E = N_EXPERTS
    C = T // MOE_CHUNK
    i32 = jnp.int32
    w_router = jnp.pad(router, ((0, 0), (0, LANE - E)))
    h, route, gate_rows, before, total = _router(x2, g_ffn.reshape(1, D), w_router)

    counts = total[0, :E].astype(i32)
    tiles = (counts + MOE_ROW_TILE - 1) // MOE_ROW_TILE
    ends = jnp.cumsum(tiles)
    off = (ends - tiles) * MOE_ROW_TILE
    e1, e2, r1, r2 = route[:, 0], route[:, 1], route[:, 2], route[:, 3]
    off_of = lambda e: sum(jnp.where(e == k, off[k], 0) for k in range(E))
    slot1 = off_of(e1) + r1
    slot2 = off_of(e2) + r2
    first = off[None, :] + before[:, 0, :E].astype(i32)
    after = jnp.concatenate([before[1:, 0, :E], total[:, :E]], axis=0).astype(i32)
    last = off[None, :] + after - 1
    lo_tile = first // MOE_CHUNK
    tile_meta = lo_tile << (2 * MOE_PARTS)
    part_rows = MOE_CHUNK // MOE_PARTS
    for k in range(2 * MOE_PARTS):
        start = lo_tile * MOE_CHUNK + k * part_rows
        touched = (last >= first) & (first < start + part_rows) & (last >= start)
        tile_meta = tile_meta + (touched.astype(i32) << k)
    n_row_tiles = (2 * T) // MOE_ROW_TILE + E + 1 + MOE_CHUNK // MOE_ROW_TILE
    n_used = ends[-1:]
    tile_ids = jnp.minimum(jnp.arange(n_row_tiles, dtype=i32), n_used[0] - 1)
    tile_expert = jnp.minimum(jnp.sum(ends[None, :] <= tile_ids[:, None], axis=1), E - 1).astype(i32)

    zero = jnp.zeros_like(slot1)
    tok_rows = jnp.stack([slot1, slot2, zero, zero, zero, zero, zero, zero], axis=1)
    slot1_rep = jnp.broadcast_to(slot1.reshape(T, 1), (T, LANE))
    slot2_rep = jnp.broadcast_to(slot2.reshape(T, 1), (T, LANE))

    xs_lo, xs_hi, gs_lo, gs_hi = _dispatch(tile_meta.T.reshape(-1).astype(i32), h, tok_rows, gate_rows,
                                           n_row_tiles * MOE_ROW_TILE)
    ys = _expert_ffn(tile_expert, n_used.astype(i32), xs_lo, xs_hi, gs_lo, gs_hi,
                     wg.astype(BF16), wu.astype(BF16), wd.astype(BF16))
    in_hi = (tile_meta >> MOE_PARTS) & ((1 << MOE_PARTS) - 1) != 0
    hi_tile = lax.cummax(jnp.where(in_hi, lo_tile + 1, 0), axis=0)
    return _combine(tile_meta.reshape(-1).astype(i32), hi_tile.reshape(-1).astype(i32), x2, ys,
                    slot1_rep, slot2_rep, g_final.reshape(1, D))


def kernel(x, mem, norm_mix, w_in, cmp_pe_k, cmp_w1_k, cmp_b1_k, cmp_w2_k, cmp_pe_v, cmp_w1_v, cmp_b1_v, cmp_w2_v, w_pool, pool_scale, w_out, norm_x, norm_mem, wq_x, wk_x, wv_x, wo_x, norm_ffn, ffn_wg, ffn_wu, ffn_wd, moe_router, moe_wg, moe_wu, moe_wd, norm_final):
    B, S, D = x.shape
    depth = norm_mix.shape[0]
    assert depth == 2, "the final RMSNorm is fused into the expert layer, which must be the last one"
    cos_t, sin_t = _rope_tables(S)
    for layer in range(depth):
        a, p = _mixer_heads(x, norm_mix[layer], w_in[layer],
                            cmp_pe_k[layer], cmp_w1_k[layer], cmp_b1_k[layer], cmp_w2_k[layer],
                            cmp_pe_v[layer], cmp_w1_v[layer], cmp_b1_v[layer], cmp_w2_v[layer],
                            w_pool[layer], pool_scale[layer], cos_t, sin_t)
        wkv = jnp.concatenate([wk_x[layer], wv_x[layer]], axis=1).astype(BF16)
        kx, vx = _xattn_kv(mem, norm_mem[layer].reshape(1, D), wkv)
        wo_mix = w_out[layer].astype(BF16)
        x = _xattn(x, a, p, wo_mix[:NSA_WIDTH], wo_mix[NSA_WIDTH:], norm_x[layer].reshape(1, D),
                   wq_x[layer].astype(BF16), kx, vx, wo_x[layer].astype(BF16))
        x2 = x.reshape(B * S, D)
        j = layer // 2
        if layer % 2 == 0:
            x2 = _ffn(x2, norm_ffn[layer].reshape(1, D), ffn_wg[j].astype(BF16), ffn_wu[j].astype(BF16),
                      ffn_wd[j].astype(BF16))
        else:
            x2 = _moe_layer(x2, norm_ffn[layer], moe_router[j], moe_wg[j], moe_wu[j], moe_wd[j], norm_final)
        x = x2.reshape(B, S, D)
    return x
```

```python
import functools
import math

import numpy as np
import jax
import jax.numpy as jnp
from jax import lax
from jax.experimental import pallas as pl
from jax.experimental.pallas import tpu as pltpu

F32 = jnp.float32
BF16 = jnp.bfloat16

D_MODEL = 1024
HEAD_DIM = 64
N_HEADS = 8
N_KV_HEADS = 2
GROUP = N_HEADS // N_KV_HEADS
NSA_WIDTH = N_HEADS * HEAD_DIM
KV_WIDTH = N_KV_HEADS * HEAD_DIM
N_BRANCH = 3
GATE_ROWS = 16
POOL_WIDTH = 512
POOL_WINDOWS = (2, 4, 8, 16)
POOL_GROUP = 128
POOL_HALO = 16
ROPE_DIM = 16
ROPE_HALF = 8
ROPE_THETA = 500000.0
CMP_LEN = 32
CMP_STRIDE = 16
CMP_HIDDEN = 256
SEL_BLOCK = 64
SEL_TOP_N = 16
N_LOCAL_BLOCKS = 2
WINDOW = 512
X_HEADS = 4
X_HEAD_DIM = 256
N_EXPERTS = 8
EPS = 1e-6

LANE = 128
SUBLANE = 8
BF16_ROWS = 16
IN_PACKED = NSA_WIDTH + 6 * KV_WIDTH + LANE + POOL_WIDTH
GATE_OFF = NSA_WIDTH + 6 * KV_WIDTH
POOL_OFF = GATE_OFF + LANE

NEG = -1e30
TINY = float(np.finfo(np.float32).tiny)
VMEM_LIMIT = 56 * 1024 * 1024


def _dot(a, b):
    return jnp.dot(a, b, preferred_element_type=F32)


def _dot_nt(a, b):
    return lax.dot_general(a, b, (((1,), (1,)), ((), ())), preferred_element_type=F32)


def _rms(x, g):
    y = x * lax.rsqrt(jnp.mean(x * x, axis=-1, keepdims=True) + EPS)
    return y * g


def _params(sem, limit=VMEM_LIMIT):
    return pltpu.CompilerParams(dimension_semantics=sem, vmem_limit_bytes=limit)


def _mixer_proj_kernel(x_ref, g_ref, w_ref, cos_ref, sin_ref, wpool_ref, pscale_ref,
                       qt_ref, qrt_ref, kcr_ref, vcr_ref, ks_ref, vst_ref, kw_ref, vwt_ref,
                       gate_ref, p_ref, pool_buf):
    h = _rms(x_ref[0], g_ref[...]).astype(BF16)
    z = _dot(h, w_ref[...])
    tm = z.shape[0]
    cos = cos_ref[...]
    sin = sin_ref[...]
    lane = lax.broadcasted_iota(jnp.int32, (tm, LANE), 1)
    first = (lane & (HEAD_DIM - 1)) < ROPE_HALF
    scale = HEAD_DIM ** -0.5 * math.log2(math.e)

    def rope(xs):
        partner = jnp.where(first, pltpu.roll(xs, LANE - ROPE_HALF, 1), pltpu.roll(xs, ROPE_HALF, 1))
        return xs * cos + partner * sin

    for s in range(NSA_WIDTH // LANE):
        xs = z[:, s * LANE:(s + 1) * LANE]
        for src, ref in ((xs, qt_ref), (rope(xs), qrt_ref)):
            t = (src * scale).T.astype(BF16)
            ref[0, 2 * s] = t[:HEAD_DIM]
            ref[0, 2 * s + 1] = t[HEAD_DIM:]

    def kv_slab(i):
        return z[:, NSA_WIDTH + i * KV_WIDTH:NSA_WIDTH + (i + 1) * KV_WIDTH]

    for slab, ref in ((kv_slab(0), kcr_ref), (kv_slab(1), vcr_ref),
                      (rope(kv_slab(2)), ks_ref), (rope(kv_slab(4)), kw_ref)):
        for gg in range(N_KV_HEADS):
            ref[0, gg] = slab[:, gg * HEAD_DIM:(gg + 1) * HEAD_DIM].astype(ref.dtype)

    for slab, ref in ((kv_slab(3), vst_ref), (kv_slab(5), vwt_ref)):
        t = slab.T.astype(BF16)
        for gg in range(N_KV_HEADS):
            for k in range(tm // LANE):
                ref[0, gg, k] = t[gg * HEAD_DIM:(gg + 1) * HEAD_DIM, k * LANE:(k + 1) * LANE]

    sig_t = jax.nn.sigmoid(z[:, GATE_OFF:GATE_OFF + LANE]).T
    for gg in range(N_KV_HEADS):
        gate_ref[0, gg] = sig_t[gg * GATE_ROWS:(gg + 1) * GATE_ROWS]

    i = pl.program_id(1)

    @pl.when(i == 0)
    def _():
        pool_buf[0:POOL_HALO, :] = jnp.zeros((POOL_HALO, POOL_WIDTH), F32)

    @pl.when(i > 0)
    def _():
        pool_buf[0:POOL_HALO, :] = pool_buf[tm:tm + POOL_HALO, :]

    pool_buf[POOL_HALO:POOL_HALO + tm, :] = z[:, POOL_OFF:POOL_OFF + POOL_WIDTH]
    t1 = i * tm + lax.broadcasted_iota(jnp.int32, (tm, 1), 0) + 1
    for gi, w in enumerate(POOL_WINDOWS):
        cols = slice(gi * POOL_GROUP, (gi + 1) * POOL_GROUP)
        cur = pool_buf[POOL_HALO:POOL_HALO + tm, cols]
        tot = cur
        for k in range(1, w):
            tot = tot + pool_buf[POOL_HALO - k:POOL_HALO - k + tm, cols]
        d = tot / jnp.minimum(t1, w).astype(F32) - cur
        p_ref[0, :, cols] = (_dot(d.astype(BF16), wpool_ref[gi]) * pscale_ref[:, cols]).astype(p_ref.dtype)


def _mixer_proj(x, g, w_packed, cos_t, sin_t, w_pool, pool_scale, tm=1024):
    B, S, D = x.shape
    G = N_KV_HEADS
    sd = jax.ShapeDtypeStruct
    out_shape = (sd((B, N_HEADS, HEAD_DIM, S), BF16), sd((B, N_HEADS, HEAD_DIM, S), BF16),
                 sd((B, G, S, HEAD_DIM), F32), sd((B, G, S, HEAD_DIM), F32),
                 sd((B, G, S, HEAD_DIM), BF16), sd((B, G, S // LANE, HEAD_DIM, LANE), BF16),
                 sd((B, G, S, HEAD_DIM), BF16), sd((B, G, S // LANE, HEAD_DIM, LANE), BF16),
                 sd((B, G, GATE_ROWS, S), F32),
                 sd((B, S, POOL_WIDTH), BF16))
    qspec = pl.BlockSpec((1, N_HEADS, HEAD_DIM, tm), lambda b, i: (b, 0, 0, i))
    kspec = pl.BlockSpec((1, G, tm, HEAD_DIM), lambda b, i: (b, 0, i, 0))
    vspec = pl.BlockSpec((1, G, tm // LANE, HEAD_DIM, LANE), lambda b, i: (b, 0, i, 0, 0))
    out_specs = (qspec, qspec, kspec, kspec, kspec, vspec, kspec, vspec,
                 pl.BlockSpec((1, G, GATE_ROWS, tm), lambda b, i: (b, 0, 0, i)),
                 pl.BlockSpec((1, tm, POOL_WIDTH), lambda b, i: (b, i, 0)))
    return pl.pallas_call(
        _mixer_proj_kernel,
        grid=(B, S // tm),
        in_specs=[pl.BlockSpec((1, tm, D), lambda b, i: (b, i, 0)),
                  pl.BlockSpec((1, D), lambda b, i: (0, 0)),
                  pl.BlockSpec((D, IN_PACKED), lambda b, i: (0, 0)),
                  pl.BlockSpec((tm, LANE), lambda b, i: (i, 0)),
                  pl.BlockSpec((tm, LANE), lambda b, i: (i, 0)),
                  pl.BlockSpec(w_pool.shape, lambda b, i: (0, 0, 0)),
                  pl.BlockSpec((1, POOL_WIDTH), lambda b, i: (0, 0))],
        out_specs=out_specs,
        out_shape=out_shape,
        scratch_shapes=[pltpu.VMEM((tm + POOL_HALO, POOL_WIDTH), F32)],
        compiler_params=_params(("parallel", "arbitrary")),
        name="mixer_proj",
    )(x, g, w_packed, cos_t, sin_t, w_pool, pool_scale)


def _gelu_tanh(x):
    return 0.5 * x * (1.0 + jnp.tanh(math.sqrt(2.0 / math.pi) * (x + 0.044715 * (x * x * x))))


def _compress_kernel(kc_ref, vc_ref, pek_ref, w1k_ref, b1k_ref, w2k_ref,
                     pev_ref, w1v_ref, b1v_ref, w2v_ref, ko_ref, vto_ref):
    def comp(x_ref, pe_ref, w1_ref, b1_ref, w2_ref):
        n = x_ref.shape[2] // CMP_STRIDE
        x = jnp.concatenate([x_ref[0, 0, pl.ds(l, n, stride=CMP_STRIDE), :] for l in range(CMP_STRIDE)], axis=1)
        a = _dot((x + pe_ref[0:1, :]).astype(BF16), w1_ref[0])
        b = _dot((x + pe_ref[1:2, :]).astype(BF16), w1_ref[1])
        pre = a + pltpu.roll(b, n - 1, 0) + b1_ref[...]
        return _dot(_gelu_tanh(pre).astype(BF16), w2_ref[...])

    ko_ref[0, 0] = comp(kc_ref, pek_ref, w1k_ref, b1k_ref, w2k_ref)[:, :HEAD_DIM].astype(ko_ref.dtype)
    vto_ref[0, 0] = comp(vc_ref, pev_ref, w1v_ref, b1v_ref, w2v_ref).T[:HEAD_DIM].astype(vto_ref.dtype)


def _compress(kcr, vcr, pek, w1k, b1k, w2k, pev, w1v, b1v, w2v):
    B, G, S, _ = kcr.shape
    n = S // CMP_STRIDE
    xspec = pl.BlockSpec((1, 1, S, HEAD_DIM), lambda b, g: (b, g, 0, 0))
    full = lambda a: pl.BlockSpec(a.shape, lambda b, g: (0,) * a.ndim)
    return pl.pallas_call(
        _compress_kernel,
        grid=(B, G),
        in_specs=[xspec, xspec, full(pek), full(w1k), full(b1k), full(w2k),
                  full(pev), full(w1v), full(b1v), full(w2v)],
        out_specs=(pl.BlockSpec((1, 1, n, HEAD_DIM), lambda b, g: (b, g, 0, 0)),
                   pl.BlockSpec((1, 1, HEAD_DIM, n), lambda b, g: (b, g, 0, 0))),
        out_shape=(jax.ShapeDtypeStruct((B, G, n, HEAD_DIM), BF16),
                   jax.ShapeDtypeStruct((B, G, HEAD_DIM, n), BF16)),
        compiler_params=_params(("parallel", "parallel")),
        name="compress_kv",
    )(kcr, vcr, pek, w1k, b1k, w2k, pev, w1v, b1v, w2v)


def _nsa_kernel(*refs, tq, tk, tk_narrow):
    for g in range(N_KV_HEADS):
        _nsa_group(g, *refs, tq=tq, tk=tk, tk_narrow=tk_narrow)


def _nsa_group(g, qt_ref, qrt_ref, kc_ref, vct_ref, ks_ref, vst_ref, kw_ref, vwt_ref, gate_ref,
               covt_ref, o_ref, selbias_ref, rank_ref, *, tq, tk, tk_narrow):
    qt = pl.program_id(1)
    q0 = qt * tq
    n_cmp = kc_ref.shape[2]
    n_blk = covt_ref.shape[0]
    heads = lambda ref: jnp.concatenate([ref[0, GROUP * g + hh] for hh in range(GROUP)], axis=1)
    q_t = heads(qt_ref)
    qr_t = heads(qrt_ref)
    t_lane = q0 + lax.broadcasted_iota(jnp.int32, (1, tq), 1)
    per_head = lambda a: [a[:, hh * tq:(hh + 1) * tq] for hh in range(GROUP)]
    all_heads = lambda a: jnp.concatenate([a] * GROUP, axis=1)

    sc = _dot(kc_ref[0, g], q_t)
    cmp_end = lax.broadcasted_iota(jnp.int32, (n_cmp, 1), 0) * CMP_STRIDE + (CMP_LEN - 1)
    valid = cmp_end <= t_lane
    pcs = []
    for s_h in per_head(sc):
        s_h = jnp.where(valid, s_h, NEG)
        e = jnp.where(valid, jnp.exp2(s_h - jnp.max(s_h, axis=0, keepdims=True)), 0.0)
        pcs.append(e * (1.0 / jnp.maximum(jnp.sum(e, axis=0, keepdims=True), TINY)))
    oc_t = _dot(vct_ref[0, g],jnp.concatenate(pcs, axis=1).astype(BF16))

    psum = pcs[0] + pcs[1] + pcs[2] + pcs[3]
    p_hi = psum.astype(BF16)
    r1 = psum - p_hi.astype(F32)
    p_mid = r1.astype(BF16)
    p_lo = (r1 - p_mid.astype(F32)).astype(BF16)
    covt = covt_ref[...]
    imp_t = _dot(covt, p_hi) + _dot(covt, p_mid) + _dot(covt, p_lo)

    jb = lax.broadcasted_iota(jnp.int32, (n_blk, tq), 0)
    tb = (q0 + lax.broadcasted_iota(jnp.int32, (n_blk, tq), 1)) // SEL_BLOCK
    dist = tb - jb
    forced = (jb == 0) | ((dist >= 0) & (dist < N_LOCAL_BLOCKS))
    score = jnp.where(jb > tb, -jnp.inf, jnp.where(forced, jnp.inf, imp_t))
    rank_ref[...] = jnp.zeros((n_blk, tq), jnp.int32)
    n_live = (q0 + tq - 1) // SEL_BLOCK + 1
    for g8 in range(n_blk // SUBLANE):
        @pl.when(g8 * SUBLANE < n_live)
        def _(g8=g8):
            rank = rank_ref[...]
            for jp in range(g8 * SUBLANE, (g8 + 1) * SUBLANE):
                rowv = score[jp:jp + 1, :]
                beats = (rowv > score) | ((rowv == score) & (jb > jp))
                rank = rank + jnp.where(beats, 1, 0)
            rank_ref[...] = rank
    selbias_ref[...] = jnp.where(rank_ref[...] < min(SEL_TOP_N, n_blk), 0.0, NEG)

    R = GROUP * tq

    def with_ones(v_t):
        extra = jnp.where(lax.broadcasted_iota(jnp.int32, (BF16_ROWS, v_t.shape[1]), 0) == 0, 1.0, 0.0)
        return jnp.concatenate([v_t, extra.astype(BF16)], axis=0)

    def sel_chunk(c, carry, width, causal):
        m_i, l_i, acc = carry
        bpc = width // SEL_BLOCK
        vpc = width // LANE
        start = pl.multiple_of(c * width, width)
        k_c = ks_ref[0, g, pl.ds(start, width), :]
        v_t = jnp.concatenate([vst_ref[0, g, c * vpc + k] for k in range(vpc)], axis=1)
        rows = [jnp.broadcast_to(selbias_ref[pl.ds(c * bpc + j, 1), :], (SEL_BLOCK, tq)) for j in range(bpc)]
        bias = jnp.concatenate(rows, axis=0)
        if causal:
            kpos = start + lax.broadcasted_iota(jnp.int32, (width, 1), 0)
            bias = jnp.where(kpos <= t_lane, bias, NEG)
        s = _dot(k_c, qr_t) + all_heads(bias)
        m_new = jnp.maximum(m_i, jnp.max(s, axis=0, keepdims=True))
        alpha = jnp.exp2(m_i - m_new)
        pv = _dot(with_ones(v_t), jnp.exp2(s - m_new).astype(BF16))
        return m_new, alpha * l_i + pv[HEAD_DIM:HEAD_DIM + 1], alpha * acc + pv[:HEAD_DIM]

    init = (jnp.full((1, R), NEG, F32), jnp.zeros((1, R), F32), jnp.zeros((HEAD_DIM, R), F32))
    n_wide = q0 // tk
    j_diag = q0 // tk_narrow
    carry = lax.fori_loop(0, n_wide, lambda c, cr: sel_chunk(c, cr, tk, False), init)
    carry = lax.fori_loop(n_wide * (tk // tk_narrow), j_diag, lambda c, cr: sel_chunk(c, cr, tk_narrow, False), carry)
    _, l_s, acc_s = sel_chunk(j_diag, carry, tk_narrow, True)
    os_t = acc_s * (1.0 / jnp.maximum(l_s, TINY))

    wk = WINDOW + tq
    w0 = pl.multiple_of(jnp.maximum(q0 - WINDOW, 0), LANE)
    k_w = kw_ref[0, g, pl.ds(w0, wk), :]
    v_t = jnp.concatenate([vwt_ref[0, g, w0 // LANE + k] for k in range(wk // LANE)], axis=1)
    diff = t_lane - (w0 + lax.broadcasted_iota(jnp.int32, (wk, 1), 0))
    bias = jnp.where((diff >= 0) & (diff < WINDOW), 0.0, NEG)
    s = _dot(k_w, qr_t) + all_heads(bias)
    pv = _dot(with_ones(v_t), jnp.exp2(s - jnp.max(s, axis=0, keepdims=True)).astype(BF16))
    ow_t = pv[:HEAD_DIM] * (1.0 / pv[HEAD_DIM:HEAD_DIM + 1])

    gt = gate_ref[0, g]
    outs = []
    for hh, (a, b, c) in enumerate(zip(per_head(oc_t), per_head(os_t), per_head(ow_t))):
        r = hh * N_BRANCH
        outs.append(gt[r:r + 1] * a + gt[r + 1:r + 2] * b + gt[r + 2:r + 3] * c)
    gw = GROUP * HEAD_DIM
    o_ref[0, :, g * gw:(g + 1) * gw] = jnp.concatenate(outs, axis=0).T.astype(o_ref.dtype)


def _cover_table(S):
    n_cmp = S // CMP_STRIDE
    n_blk = S // SEL_BLOCK
    cs = np.arange(n_cmp) * CMP_STRIDE
    ss = np.arange(n_blk) * SEL_BLOCK
    cover_t = ((cs[None, :] < ss[:, None] + SEL_BLOCK) & (cs[None, :] + CMP_LEN > ss[:, None]))
    cover_t[:, n_cmp - 1] = False
    return jnp.asarray(cover_t, BF16)


def _nsa_attention(qt, qrt, kc, vct, ks, vst, kw, vwt, gates, tq=256, tk=1024):
    B, _, _, S = qt.shape
    tk = min(tk, S)
    assert tq % LANE == 0 and tk % tq == 0 and S % tk == 0 and S >= WINDOW + tq
    covt = _cover_table(S)
    n_blk = covt.shape[0]
    G = N_KV_HEADS
    qspec = pl.BlockSpec((1, N_HEADS, HEAD_DIM, tq), lambda b, i: (b, 0, 0, i))
    whole = lambda a: pl.BlockSpec((1,) + a.shape[1:], lambda b, i: (b,) + (0,) * (a.ndim - 1))
    return pl.pallas_call(
        functools.partial(_nsa_kernel, tq=tq, tk=tk, tk_narrow=min(tk, 2 * tq)),
        grid=(B, S // tq),
        in_specs=[qspec, qspec, whole(kc), whole(vct), whole(ks), whole(vst), whole(kw), whole(vwt),
                  pl.BlockSpec((1, G, GATE_ROWS, tq), lambda b, i: (b, 0, 0, i)),
                  pl.BlockSpec(covt.shape, lambda b, i: (0, 0))],
        out_specs=pl.BlockSpec((1, tq, NSA_WIDTH), lambda b, i: (b, i, 0)),
        out_shape=jax.ShapeDtypeStruct((B, S, NSA_WIDTH), BF16),
        scratch_shapes=[pltpu.VMEM((n_blk, tq), F32), pltpu.VMEM((n_blk, tq), jnp.int32)],
        compiler_params=_params(("parallel", "parallel")),
        name="nsa_attention",
    )(qt, qrt, kc, vct, ks, vst, kw, vwt, gates, covt)


def _xkv_kernel(m_ref, g_ref, w_ref, k_ref, v_ref):
    h = _rms(m_ref[0], g_ref[...]).astype(BF16)
    kv = _dot(h, w_ref[...])
    d = k_ref.shape[2]
    k_ref[0] = kv[:, :d].astype(k_ref.dtype)
    v_ref[0] = kv[:, d:].astype(v_ref.dtype)


def _xattn_kv(mem, g, wkv):
    B, M, D = mem.shape
    ospec = pl.BlockSpec((1, M, D), lambda b: (b, 0, 0))
    oshape = jax.ShapeDtypeStruct((B, M, D), BF16)
    return pl.pallas_call(
        _xkv_kernel,
        grid=(B,),
        in_specs=[pl.BlockSpec((1, M, D), lambda b: (b, 0, 0)),
                  pl.BlockSpec((1, D), lambda b: (0, 0)),
                  pl.BlockSpec(wkv.shape, lambda b: (0, 0))],
        out_specs=(ospec, ospec),
        out_shape=(oshape, oshape),
        compiler_params=_params(("parallel",)),
        name="xattn_kv",
    )(mem, g, wkv)


def _xattn_kernel(x_ref, a_ref, p_ref, wa_ref, wp_ref, g_ref, wq_ref, k_ref, v_ref, wo_ref, o_ref):
    x = x_ref[0] + _dot(a_ref[0], wa_ref[...]) + _dot(p_ref[0], wp_ref[...])
    h = _rms(x, g_ref[...]).astype(BF16)
    scale = X_HEAD_DIM ** -0.5
    q = (_dot(h, wq_ref[...]) * scale).astype(BF16)
    outs = []
    for hd in range(X_HEADS):
        sl = slice(hd * X_HEAD_DIM, (hd + 1) * X_HEAD_DIM)
        s = _dot_nt(q[:, sl], k_ref[0, :, sl])
        e = jnp.exp(s - jnp.max(s, axis=-1, keepdims=True))
        p = e / jnp.sum(e, axis=-1, keepdims=True)
        outs.append(_dot(p.astype(BF16), v_ref[0, :, sl]).astype(BF16))
    o = jnp.concatenate(outs, axis=-1)
    o_ref[0] = x + _dot(o, wo_ref[...])


def _xattn(x, a, p, wa, wp, g, wq, kx, vx, wo, tm=1024):
    B, S, D = x.shape
    M = kx.shape[1]
    row = lambda n: pl.BlockSpec((1, tm, n), lambda b, i: (b, i, 0))
    full = lambda w: pl.BlockSpec(w.shape, lambda b, i: (0, 0))
    mspec = pl.BlockSpec((1, M, D), lambda b, i: (b, 0, 0))
    return pl.pallas_call(
        _xattn_kernel,
        grid=(B, S // tm),
        in_specs=[row(D), row(a.shape[2]), row(p.shape[2]), full(wa), full(wp),
                  full(g), full(wq), mspec, mspec, full(wo)],
        out_specs=row(D),
        out_shape=jax.ShapeDtypeStruct((B, S, D), F32),
        compiler_params=_params(("parallel", "parallel")),
        name="out_proj_xattn",
    )(x, a, p, wa, wp, g, wq, kx, vx, wo)


def _silu(x):
    return x * jax.nn.sigmoid(x)


def _ffn_kernel(x_ref, g_ref, wg_ref, wu_ref, wd_ref, o_ref):
    x = x_ref[...]
    h = _rms(x, g_ref[...]).astype(BF16)
    act = _silu(_dot(h, wg_ref[...])) * _dot(h, wu_ref[...])
    o_ref[...] = x + _dot(act.astype(BF16), wd_ref[...])


def _ffn(x2, g, wg, wu, wd, tm=512):
    T, D = x2.shape
    resident = lambda w: pl.BlockSpec(w.shape, lambda i: (0, 0), pipeline_mode=pl.Buffered(1))
    return pl.pallas_call(
        _ffn_kernel,
        grid=(T // tm,),
        in_specs=[pl.BlockSpec((tm, D), lambda i: (i, 0)),
                  pl.BlockSpec((1, D), lambda i: (0, 0)),
                  resident(wg), resident(wu), resident(wd)],
        out_specs=pl.BlockSpec((tm, D), lambda i: (i, 0)),
        out_shape=jax.ShapeDtypeStruct((T, D), F32),
        compiler_params=_params(("parallel",)),
        name="dense_swiglu",
    )(x2, g, wg, wu, wd)


MOE_CHUNK = 512
MOE_ROW_TILE = 512
assert MOE_ROW_TILE % MOE_CHUNK == 0


MOE_PARTS = 2


def _tile_meta(m):
    return m >> (2 * MOE_PARTS), [(m & (1 << k)) != 0 for k in range(2 * MOE_PARTS)]


def _router_kernel(x_ref, g_ref, wr_ref, tri_ref, h_ref, route_ref, gate_ref, before_ref, total_ref, run_ref):
    c = pl.program_id(0)

    @pl.when(c == 0)
    def _():
        run_ref[...] = jnp.zeros_like(run_ref)

    hf = _rms(x_ref[...], g_ref[...])
    h_ref[...] = hf.astype(BF16)
    logits = jnp.dot(hf, wr_ref[...], preferred_element_type=F32, precision=lax.Precision.HIGHEST)
    lane = lax.broadcasted_iota(jnp.int32, logits.shape, 1)
    logits = jnp.where(lane < N_EXPERTS, logits, -jnp.inf)
    v1 = jnp.max(logits, axis=-1, keepdims=True)
    i1 = jnp.min(jnp.where(logits == v1, lane, LANE), axis=-1, keepdims=True)
    rest = jnp.where(lane == i1, -jnp.inf, logits)
    v2 = jnp.max(rest, axis=-1, keepdims=True)
    i2 = jnp.min(jnp.where(rest == v2, lane, LANE), axis=-1, keepdims=True)
    e2 = jnp.exp(v2 - v1)
    den = 1.0 + e2
    gate_ref[0] = jnp.where(lane == 0, 1.0 / den, jnp.where(lane == 1, e2 / den, 0.0)).T[:SUBLANE]

    onehot = jnp.where((lane == i1) | (lane == i2), 1.0, 0.0)
    run = run_ref[...]
    rank = run + _dot(tri_ref[...], onehot.astype(BF16))
    r1 = jnp.sum(jnp.where(lane == i1, rank, 0.0), axis=-1, keepdims=True).astype(jnp.int32)
    r2 = jnp.sum(jnp.where(lane == i2, rank, 0.0), axis=-1, keepdims=True).astype(jnp.int32)
    route = jnp.where(lane == 0, i1, jnp.where(lane == 1, i2, jnp.where(lane == 2, r1, jnp.where(lane == 3, r2, 0))))
    route_ref[0] = route.T[:SUBLANE]
    before_ref[0] = run
    run = run + jnp.sum(onehot, axis=0, keepdims=True)
    run_ref[...] = run
    total_ref[...] = run


def _router(x2, g, w_router):
    T, D = x2.shape
    C = T // MOE_CHUNK
    tri = jnp.asarray(np.tril(np.ones((MOE_CHUNK, MOE_CHUNK), np.float32), -1), BF16)
    row = lambda n: pl.BlockSpec((MOE_CHUNK, n), lambda c: (c, 0))
    full = lambda a: pl.BlockSpec(a.shape, lambda c: (0, 0))
    return pl.pallas_call(
        _router_kernel,
        grid=(C,),
        in_specs=[row(D), full(g), full(w_router), full(tri)],
        out_specs=(row(D),
                   pl.BlockSpec((1, SUBLANE, MOE_CHUNK), lambda c: (c, 0, 0)),
                   pl.BlockSpec((1, SUBLANE, MOE_CHUNK), lambda c: (c, 0, 0)),
                   pl.BlockSpec((1, 1, LANE), lambda c: (c, 0, 0)),
                   pl.BlockSpec((1, LANE), lambda c: (0, 0))),
        out_shape=(jax.ShapeDtypeStruct((T, D), BF16),
                   jax.ShapeDtypeStruct((C, SUBLANE, MOE_CHUNK), jnp.int32),
                   jax.ShapeDtypeStruct((C, SUBLANE, MOE_CHUNK), F32),
                   jax.ShapeDtypeStruct((C, 1, LANE), F32),
                   jax.ShapeDtypeStruct((1, LANE), F32)),
        scratch_shapes=[pltpu.VMEM((1, LANE), F32)],
        compiler_params=_params(("arbitrary",)),
        name="moe_router",
    )(x2, g, w_router, tri)


def _dispatch_kernel(a_ref, h_ref, tok_ref, gate_ref, zlo_ref, zhi_ref, zglo_ref, zghi_ref,
                     lo_ref, hi_ref, glo_ref, ghi_ref):
    del zlo_ref, zhi_ref, zglo_ref, zghi_ref
    e = pl.program_id(0)
    c = pl.program_id(1)
    idx = e * pl.num_programs(1) + c
    a, touched = _tile_meta(a_ref[idx])
    first = (c == 0) | (a != _tile_meta(a_ref[jnp.maximum(idx - 1, 0)])[0])

    @pl.when(first)
    def _():
        lo_ref[...] = jnp.zeros_like(lo_ref)
        hi_ref[...] = jnp.zeros_like(hi_ref)
        glo_ref[...] = jnp.zeros_like(glo_ref)
        ghi_ref[...] = jnp.zeros_like(ghi_ref)

    rows_per_part = MOE_CHUNK // MOE_PARTS

    def scatter(tile, part, x_ref, g_ref):
        tok = tok_ref[0]
        gts = gate_ref[0]
        r0 = part * rows_per_part
        rows = tile * MOE_CHUNK + r0 + lax.broadcasted_iota(jnp.int32, (rows_per_part, MOE_CHUNK), 0)
        c1 = tok[0:1] == rows
        c2 = tok[1:2] == rows
        x_ref[r0:r0 + rows_per_part, :] += _dot(jnp.where(c1 | c2, 1.0, 0.0).astype(BF16),
                                                h_ref[...]).astype(x_ref.dtype)
        g_ref[r0:r0 + rows_per_part, :] += jnp.sum(jnp.where(c1, gts[0:1], 0.0) + jnp.where(c2, gts[1:2], 0.0),
                                                   axis=1, keepdims=True)

    for k, (x_ref, g_ref) in enumerate(((lo_ref, glo_ref), (hi_ref, ghi_ref))):
        for part in range(MOE_PARTS):
            pl.when(touched[k * MOE_PARTS + part])(functools.partial(scatter, a + k, part, x_ref, g_ref))


def _dispatch(a_ec, h, tok_rows, gate_rows, n_slots):
    T, D = h.shape
    C = T // MOE_CHUNK
    zx = jnp.zeros((n_slots, D), BF16)
    zg = jnp.zeros((n_slots, 1), F32)
    any_spec = pl.BlockSpec(memory_space=pl.ANY)
    lo = lambda n: pl.BlockSpec((MOE_CHUNK, n), lambda e, c, a: (a[e * C + c] >> (2 * MOE_PARTS), 0))
    hi = lambda n: pl.BlockSpec((MOE_CHUNK, n), lambda e, c, a: ((a[e * C + c] >> (2 * MOE_PARTS)) + 1, 0))
    return pl.pallas_call(
        _dispatch_kernel,
        grid_spec=pltpu.PrefetchScalarGridSpec(
            num_scalar_prefetch=1,
            grid=(N_EXPERTS, C),
            in_specs=[pl.BlockSpec((MOE_CHUNK, D), lambda e, c, a: (c, 0)),
                      pl.BlockSpec((1, SUBLANE, MOE_CHUNK), lambda e, c, a: (c, 0, 0)),
                      pl.BlockSpec((1, SUBLANE, MOE_CHUNK), lambda e, c, a: (c, 0, 0)),
                      any_spec, any_spec, any_spec, any_spec],
            out_specs=(lo(D), hi(D), lo(1), hi(1))),
        out_shape=(jax.ShapeDtypeStruct(zx.shape, BF16), jax.ShapeDtypeStruct(zx.shape, BF16),
                   jax.ShapeDtypeStruct(zg.shape, F32), jax.ShapeDtypeStruct(zg.shape, F32)),
        input_output_aliases={4: 0, 5: 1, 6: 2, 7: 3},
        compiler_params=_params(("arbitrary", "arbitrary")),
        name="moe_dispatch",
    )(a_ec, h, tok_rows, gate_rows, zx, zx, zg, zg)


def _expert_ffn_kernel(te_ref, nu_ref, lo_ref, hi_ref, glo_ref, ghi_ref, wg_ref, wu_ref, wd_ref, o_ref):
    del te_ref
    used = pl.program_id(0) < nu_ref[0]

    @pl.when(used)
    def _():
        x = lo_ref[...] + hi_ref[...]
        act = _silu(_dot(x, wg_ref[0])) * _dot(x, wu_ref[0])
        o_ref[...] = (_dot(act.astype(BF16), wd_ref[0]) * (glo_ref[...] + ghi_ref[...])).astype(o_ref.dtype)

    @pl.when(jnp.logical_not(used))
    def _():
        o_ref[...] = jnp.zeros_like(o_ref)


def _expert_ffn(tile_expert, n_used, xs_lo, xs_hi, gs_lo, gs_hi, wg, wu, wd, tr=256):
    N, D = xs_lo.shape
    sub = MOE_ROW_TILE // tr
    tile_expert = jnp.repeat(tile_expert, sub)
    n_used = n_used * sub
    row = lambda n: pl.BlockSpec((tr, n), lambda i, te, nu: (i, 0))
    expert = lambda w: pl.BlockSpec((1,) + w.shape[1:], lambda i, te, nu: (te[i], 0, 0))
    return pl.pallas_call(
        _expert_ffn_kernel,
        grid_spec=pltpu.PrefetchScalarGridSpec(
            num_scalar_prefetch=2,
            grid=(N // tr,),
            in_specs=[row(D), row(D), row(1), row(1), expert(wg), expert(wu), expert(wd)],
            out_specs=row(D)),
        out_shape=jax.ShapeDtypeStruct((N, D), BF16),
        compiler_params=_params(("arbitrary",)),
        name="moe_expert_ffn",
    )(tile_expert, n_used, xs_lo, xs_hi, gs_lo, gs_hi, wg, wu, wd)


def _combine_kernel(meta_ref, hidx_ref, x_ref, *refs):
    del hidx_ref
    E = N_EXPERTS
    ylo, yhi = refs[:E], refs[E:2 * E]
    s1_ref, s2_ref, gf_ref, o_ref, acc_ref = refs[2 * E:]
    c = pl.program_id(0)
    s1 = s1_ref[...]
    s2 = s2_ref[...]

    def pick(tile):
        parts = []
        for j in range(MOE_CHUNK // LANE):
            cols = tile * MOE_CHUNK + j * LANE + lax.broadcasted_iota(jnp.int32, (MOE_CHUNK, LANE), 1)
            parts.append(jnp.where((s1 == cols) | (s2 == cols), 1.0, 0.0).astype(BF16))
        return jnp.concatenate(parts, axis=1)

    metas = [_tile_meta(meta_ref[c * E + e]) for e in range(E)]
    total = x_ref[...]
    for (a, touched), y_ref in zip(metas, ylo):
        in_lo = functools.reduce(jnp.logical_or, touched[:MOE_PARTS])
        total = total + _dot(pick(jnp.where(in_lo, a, -1)), y_ref[...])
    acc_ref[...] = total

    for (a, touched), y_ref in zip(metas, yhi):
        @pl.when(functools.reduce(jnp.logical_or, touched[MOE_PARTS:]))
        def _(a=a, y_ref=y_ref):
            acc_ref[...] += _dot(pick(a + 1), y_ref[...])

    o_ref[...] = _rms(acc_ref[...], gf_ref[...])


def _combine(meta_ce, hi_tile_ce, x2, ys, slot1_rep, slot2_rep, g_final):
    T, D = x2.shape
    C = T // MOE_CHUNK
    E = N_EXPERTS
    lo_spec = lambda e: pl.BlockSpec((MOE_CHUNK, D), lambda c, m, h: (m[c * E + e] >> (2 * MOE_PARTS), 0))
    hi_spec = lambda e: pl.BlockSpec((MOE_CHUNK, D), lambda c, m, h: (h[c * E + e], 0))
    return pl.pallas_call(
        _combine_kernel,
        grid_spec=pltpu.PrefetchScalarGridSpec(
            num_scalar_prefetch=2,
            grid=(C,),
            in_specs=([pl.BlockSpec((MOE_CHUNK, D), lambda c, m, h: (c, 0))]
                      + [lo_spec(e) for e in range(E)] + [hi_spec(e) for e in range(E)]
                      + [pl.BlockSpec((MOE_CHUNK, LANE), lambda c, m, h: (c, 0)),
                         pl.BlockSpec((MOE_CHUNK, LANE), lambda c, m, h: (c, 0)),
                         pl.BlockSpec((1, D), lambda c, m, h: (0, 0))]),
            out_specs=pl.BlockSpec((MOE_CHUNK, D), lambda c, m, h: (c, 0)),
            scratch_shapes=[pltpu.VMEM((MOE_CHUNK, D), F32)]),
        out_shape=jax.ShapeDtypeStruct((T, D), F32),
        compiler_params=_params(("arbitrary",)),
        name="moe_combine_final_norm",
    )(meta_ce, hi_tile_ce, x2, *([ys] * (2 * E)), slot1_rep, slot2_rep, g_final)


def _rope_tables(S):
    inv = ROPE_THETA ** (-jnp.arange(ROPE_HALF, dtype=F32) / ROPE_HALF)
    ang = jnp.arange(S, dtype=F32)[:, None] * inv[None, :]
    cos, sin = jnp.cos(ang), jnp.sin(ang)
    pad = HEAD_DIM - ROPE_DIM
    cos_h = jnp.concatenate([cos, cos, jnp.ones((S, pad), F32)], axis=-1)
    sin_h = jnp.concatenate([-sin, sin, jnp.zeros((S, pad), F32)], axis=-1)
    reps = LANE // HEAD_DIM
    return jnp.tile(cos_h, (1, reps)), jnp.tile(sin_h, (1, reps))


def _pack_w_in(w):
    per_group = GROUP * N_BRANCH
    gates = w[:, GATE_OFF:GATE_OFF + N_KV_HEADS * per_group].reshape(-1, N_KV_HEADS, per_group)
    gates = jnp.pad(gates, ((0, 0), (0, LANE // GATE_ROWS - N_KV_HEADS), (0, GATE_ROWS - per_group)))
    gates = gates.reshape(-1, LANE)
    return jnp.concatenate([w[:, :GATE_OFF], gates, w[:, GATE_OFF + N_KV_HEADS * per_group:]], axis=1).astype(BF16)


def _mixer_heads(x, norm_g, w_in, pe_k, w1_k, b1_k, w2_k, pe_v, w1_v, b1_v, w2_v,
                 w_pool, pool_scale, cos_t, sin_t):
    B, S, D = x.shape
    qt, qrt, kcr, vcr, ks, vst, kw, vwt, gates, p = _mixer_proj(
        x, norm_g.reshape(1, D), _pack_w_in(w_in), cos_t, sin_t, w_pool.astype(BF16), pool_scale.reshape(1, -1))
    half = CMP_LEN // 2
    cw = half * HEAD_DIM
    pad_w2 = lambda w2: jnp.pad(w2, ((0, 0), (0, LANE - HEAD_DIM))).astype(BF16)
    kc, vct = _compress(
        kcr, vcr,
        pe_k.reshape(2, cw), w1_k.reshape(2, cw, CMP_HIDDEN).astype(BF16), b1_k.reshape(1, -1), pad_w2(w2_k),
        pe_v.reshape(2, cw), w1_v.reshape(2, cw, CMP_HIDDEN).astype(BF16), b1_v.reshape(1, -1), pad_w2(w2_v))
    return _nsa_attention(qt, qrt, kc, vct, ks, vst, kw, vwt, gates), p


def _moe_layer(x2, g_ffn, router, wg, wu, wd, g_final):
    T, D = x2.shape
    E = N_EXPERTS
    C = T // MOE_CHUNK
    i32 = jnp.int32
    w_router = jnp.pad(router, ((0, 0), (0, LANE - E)))
    h, route, gate_rows, before, total = _router(x2, g_ffn.reshape(1, D), w_router)

    counts = total[0, :E].astype(i32)
    tiles = (counts + MOE_ROW_TILE - 1) // MOE_ROW_TILE
    ends = jnp.cumsum(tiles)
    off = (ends - tiles) * MOE_ROW_TILE
    e1, e2, r1, r2 = route[:, 0], route[:, 1], route[:, 2], route[:, 3]
    off_of = lambda e: sum(jnp.where(e == k, off[k], 0) for k in range(E))
    slot1 = off_of(e1) + r1
    slot2 = off_of(e2) + r2
    first = off[None, :] + before[:, 0, :E].astype(i32)
    after = jnp.concatenate([before[1:, 0, :E], total[:, :E]], axis=0).astype(i32)
    last = off[None, :] + after - 1
    lo_tile = first // MOE_CHUNK
    tile_meta = lo_tile << (2 * MOE_PARTS)
    part_rows = MOE_CHUNK // MOE_PARTS
    for k in range(2 * MOE_PARTS):
        start = lo_tile * MOE_CHUNK + k * part_rows
        touched = (last >= first) & (first < start + part_rows) & (last >= start)
        tile_meta = tile_meta + (touched.astype(i32) << k)
    n_row_tiles = (2 * T) // MOE_ROW_TILE + E + 1 + MOE_CHUNK // MOE_ROW_TILE
    n_used = ends[-1:]
    tile_ids = jnp.minimum(jnp.arange(n_row_tiles, dtype=i32), n_used[0] - 1)
    tile_expert = jnp.minimum(jnp.sum(ends[None, :] <= tile_ids[:, None], axis=1), E - 1).astype(i32)

    zero = jnp.zeros_like(slot1)
    tok_rows = jnp.stack([slot1, slot2, zero, zero, zero, zero, zero, zero], axis=1)
    slot1_rep = jnp.broadcast_to(slot1.reshape(T, 1), (T, LANE))
    slot2_rep = jnp.broadcast_to(slot2.reshape(T, 1), (T, LANE))

    xs_lo, xs_hi, gs_lo, gs_hi = _dispatch(tile_meta.T.reshape(-1).astype(i32), h, tok_rows, gate_rows,
                                           n_row_tiles * MOE_ROW_TILE)
    ys = _expert_ffn(tile_expert, n_used.astype(i32), xs_lo, xs_hi, gs_lo, gs_hi,
                     wg.astype(BF16), wu.astype(BF16), wd.astype(BF16))
    in_hi = (tile_meta >> MOE_PARTS) & ((1 << MOE_PARTS) - 1) != 0
    hi_tile = lax.cummax(jnp.where(in_hi, lo_tile + 1, 0), axis=0)
    return _combine(tile_meta.reshape(-1).astype(i32), hi_tile.reshape(-1).astype(i32), x2, ys,
                    slot1_rep, slot2_rep, g_final.reshape(1, D))


def kernel(x, mem, norm_mix, w_in, cmp_pe_k, cmp_w1_k, cmp_b1_k, cmp_w2_k, cmp_pe_v, cmp_w1_v, cmp_b1_v, cmp_w2_v, w_pool, pool_scale, w_out, norm_x, norm_mem, wq_x, wk_x, wv_x, wo_x, norm_ffn, ffn_wg, ffn_wu, ffn_wd, moe_router, moe_wg, moe_wu, moe_wd, norm_final):
    B, S, D = x.shape
    depth = norm_mix.shape[0]
    assert depth == 2, "the final RMSNorm is fused into the expert layer, which must be the last one"
    cos_t, sin_t = _rope_tables(S)
    for layer in range(depth):
        a, p = _mixer_heads(x, norm_mix[layer], w_in[layer],
                            cmp_pe_k[layer], cmp_w1_k[layer], cmp_b1_k[layer], cmp_w2_k[layer],
                            cmp_pe_v[layer], cmp_w1_v[layer], cmp_b1_v[layer], cmp_w2_v[layer],
                            w_pool[layer], pool_scale[layer], cos_t, sin_t)
        wkv = jnp.concatenate([wk_x[layer], wv_x[layer]], axis=1).astype(BF16)
        kx, vx = _xattn_kv(mem, norm_mem[layer].reshape(1, D), wkv)
        wo_mix = w_out[layer].astype(BF16)
        x = _xattn(x, a, p, wo_mix[:NSA_WIDTH], wo_mix[NSA_WIDTH:], norm_x[layer].reshape(1, D),
                   wq_x[layer].astype(BF16), kx, vx, wo_x[layer].astype(BF16))
        x2 = x.reshape(B * S, D)
        j = layer // 2
        if layer % 2 == 0:
            x2 = _ffn(x2, norm_ffn[layer].reshape(1, D), ffn_wg[j].astype(BF16), ffn_wu[j].astype(BF16),
                      ffn_wd[j].astype(BF16))
        else:
            x2 = _moe_layer(x2, norm_ffn[layer], moe_router[j], moe_wg[j], moe_wu[j], moe_wd[j], norm_final)
        x = x2.reshape(B, S, D)
    return x
```

```python
import functools
import math

import numpy as np
import jax
import jax.numpy as jnp
from jax import lax
from jax.experimental import pallas as pl
from jax.experimental.pallas import tpu as pltpu

F32 = jnp.float32
BF16 = jnp.bfloat16

D_MODEL = 1024
HEAD_DIM = 64
N_HEADS = 8
N_KV_HEADS = 2
GROUP = N_HEADS // N_KV_HEADS
NSA_WIDTH = N_HEADS * HEAD_DIM
KV_WIDTH = N_KV_HEADS * HEAD_DIM
N_BRANCH = 3
GATE_ROWS = 16
POOL_WIDTH = 512
POOL_WINDOWS = (2, 4, 8, 16)
POOL_GROUP = 128
POOL_HALO = 16
ROPE_DIM = 16
ROPE_HALF = 8
ROPE_THETA = 500000.0
CMP_LEN = 32
CMP_STRIDE = 16
CMP_HIDDEN = 256
SEL_BLOCK = 64
SEL_TOP_N = 16
N_LOCAL_BLOCKS = 2
WINDOW = 512
X_HEADS = 4
X_HEAD_DIM = 256
N_EXPERTS = 8
EPS = 1e-6

LANE = 128
SUBLANE = 8
BF16_ROWS = 16
IN_PACKED = NSA_WIDTH + 6 * KV_WIDTH + LANE + POOL_WIDTH
GATE_OFF = NSA_WIDTH + 6 * KV_WIDTH
POOL_OFF = GATE_OFF + LANE

NEG = -1e30
TINY = float(np.finfo(np.float32).tiny)
VMEM_LIMIT = 56 * 1024 * 1024


def _dot(a, b):
    return jnp.dot(a, b, preferred_element_type=F32)


def _dot_nt(a, b):
    return lax.dot_general(a, b, (((1,), (1,)), ((), ())), preferred_element_type=F32)


def _rms(x, g):
    y = x * lax.rsqrt(jnp.mean(x * x, axis=-1, keepdims=True) + EPS)
    return y * g


def _params(sem, limit=VMEM_LIMIT):
    return pltpu.CompilerParams(dimension_semantics=sem, vmem_limit_bytes=limit)


def _mixer_proj_kernel(x_ref, g_ref, w_ref, cos_ref, sin_ref, wpool_ref, pscale_ref,
                       qt_ref, qrt_ref, kcr_ref, vcr_ref, ks_ref, vst_ref, kw_ref, vwt_ref,
                       gate_ref, p_ref, pool_buf):
    h = _rms(x_ref[0], g_ref[...]).astype(BF16)
    z = _dot(h, w_ref[...])
    tm = z.shape[0]
    cos = cos_ref[...]
    sin = sin_ref[...]
    lane = lax.broadcasted_iota(jnp.int32, (tm, LANE), 1)
    first = (lane & (HEAD_DIM - 1)) < ROPE_HALF
    scale = HEAD_DIM ** -0.5 * math.log2(math.e)

    def rope(xs):
        partner = jnp.where(first, pltpu.roll(xs, LANE - ROPE_HALF, 1), pltpu.roll(xs, ROPE_HALF, 1))
        return xs * cos + partner * sin

    for s in range(NSA_WIDTH // LANE):
        xs = z[:, s * LANE:(s + 1) * LANE]
        for src, ref in ((xs, qt_ref), (rope(xs), qrt_ref)):
            t = (src * scale).T.astype(BF16)
            ref[0, 2 * s] = t[:HEAD_DIM]
            ref[0, 2 * s + 1] = t[HEAD_DIM:]

    def kv_slab(i):
        return z[:, NSA_WIDTH + i * KV_WIDTH:NSA_WIDTH + (i + 1) * KV_WIDTH]

    for slab, ref in ((kv_slab(0), kcr_ref), (kv_slab(1), vcr_ref),
                      (rope(kv_slab(2)), ks_ref), (rope(kv_slab(4)), kw_ref)):
        for gg in range(N_KV_HEADS):
            ref[0, gg] = slab[:, gg * HEAD_DIM:(gg + 1) * HEAD_DIM].astype(ref.dtype)

    for slab, ref in ((kv_slab(3), vst_ref), (kv_slab(5), vwt_ref)):
        t = slab.T.astype(BF16)
        for gg in range(N_KV_HEADS):
            for k in range(tm // LANE):
                ref[0, gg, k] = t[gg * HEAD_DIM:(gg + 1) * HEAD_DIM, k * LANE:(k + 1) * LANE]

    sig_t = jax.nn.sigmoid(z[:, GATE_OFF:GATE_OFF + LANE]).T
    for gg in range(N_KV_HEADS):
        gate_ref[0, gg] = sig_t[gg * GATE_ROWS:(gg + 1) * GATE_ROWS]

    i = pl.program_id(1)

    @pl.when(i == 0)
    def _():
        pool_buf[0:POOL_HALO, :] = jnp.zeros((POOL_HALO, POOL_WIDTH), F32)

    @pl.when(i > 0)
    def _():
        pool_buf[0:POOL_HALO, :] = pool_buf[tm:tm + POOL_HALO, :]

    pool_buf[POOL_HALO:POOL_HALO + tm, :] = z[:, POOL_OFF:POOL_OFF + POOL_WIDTH]
    t1 = i * tm + lax.broadcasted_iota(jnp.int32, (tm, 1), 0) + 1
    for gi, w in enumerate(POOL_WINDOWS):
        cols = slice(gi * POOL_GROUP, (gi + 1) * POOL_GROUP)
        cur = pool_buf[POOL_HALO:POOL_HALO + tm, cols]
        tot = cur
        for k in range(1, w):
            tot = tot + pool_buf[POOL_HALO - k:POOL_HALO - k + tm, cols]
        d = tot / jnp.minimum(t1, w).astype(F32) - cur
        p_ref[0, :, cols] = (_dot(d.astype(BF16), wpool_ref[gi]) * pscale_ref[:, cols]).astype(p_ref.dtype)


def _mixer_proj(x, g, w_packed, cos_t, sin_t, w_pool, pool_scale, tm=1024):
    B, S, D = x.shape
    G = N_KV_HEADS
    sd = jax.ShapeDtypeStruct
    out_shape = (sd((B, N_HEADS, HEAD_DIM, S), BF16), sd((B, N_HEADS, HEAD_DIM, S), BF16),
                 sd((B, G, S, HEAD_DIM), F32), sd((B, G, S, HEAD_DIM), F32),
                 sd((B, G, S, HEAD_DIM), BF16), sd((B, G, S // LANE, HEAD_DIM, LANE), BF16),
                 sd((B, G, S, HEAD_DIM), BF16), sd((B, G, S // LANE, HEAD_DIM, LANE), BF16),
                 sd((B, G, GATE_ROWS, S), F32),
                 sd((B, S, POOL_WIDTH), BF16))
    qspec = pl.BlockSpec((1, N_HEADS, HEAD_DIM, tm), lambda b, i: (b, 0, 0, i))
    kspec = pl.BlockSpec((1, G, tm, HEAD_DIM), lambda b, i: (b, 0, i, 0))
    vspec = pl.BlockSpec((1, G, tm // LANE, HEAD_DIM, LANE), lambda b, i: (b, 0, i, 0, 0))
    out_specs = (qspec, qspec, kspec, kspec, kspec, vspec, kspec, vspec,
                 pl.BlockSpec((1, G, GATE_ROWS, tm), lambda b, i: (b, 0, 0, i)),
                 pl.BlockSpec((1, tm, POOL_WIDTH), lambda b, i: (b, i, 0)))
    return pl.pallas_call(
        _mixer_proj_kernel,
        grid=(B, S // tm),
        in_specs=[pl.BlockSpec((1, tm, D), lambda b, i: (b, i, 0)),
                  pl.BlockSpec((1, D), lambda b, i: (0, 0)),
                  pl.BlockSpec((D, IN_PACKED), lambda b, i: (0, 0)),
                  pl.BlockSpec((tm, LANE), lambda b, i: (i, 0)),
                  pl.BlockSpec((tm, LANE), lambda b, i: (i, 0)),
                  pl.BlockSpec(w_pool.shape, lambda b, i: (0, 0, 0)),
                  pl.BlockSpec((1, POOL_WIDTH), lambda b, i: (0, 0))],
        out_specs=out_specs,
        out_shape=out_shape,
        scratch_shapes=[pltpu.VMEM((tm + POOL_HALO, POOL_WIDTH), F32)],
        compiler_params=_params(("parallel", "arbitrary")),
        name="mixer_proj",
    )(x, g, w_packed, cos_t, sin_t, w_pool, pool_scale)


def _gelu_tanh(x):
    return 0.5 * x * (1.0 + jnp.tanh(math.sqrt(2.0 / math.pi) * (x + 0.044715 * (x * x * x))))


def _compress_kernel(kc_ref, vc_ref, pek_ref, w1k_ref, b1k_ref, w2k_ref,
                     pev_ref, w1v_ref, b1v_ref, w2v_ref, ko_ref, vto_ref):
    def comp(x_ref, pe_ref, w1_ref, b1_ref, w2_ref):
        n = x_ref.shape[2] // CMP_STRIDE
        x = jnp.concatenate([x_ref[0, 0, pl.ds(l, n, stride=CMP_STRIDE), :] for l in range(CMP_STRIDE)], axis=1)
        a = _dot((x + pe_ref[0:1, :]).astype(BF16), w1_ref[0])
        b = _dot((x + pe_ref[1:2, :]).astype(BF16), w1_ref[1])
        pre = a + pltpu.roll(b, n - 1, 0) + b1_ref[...]
        return _dot(_gelu_tanh(pre).astype(BF16), w2_ref[...])

    ko_ref[0, 0] = comp(kc_ref, pek_ref, w1k_ref, b1k_ref, w2k_ref)[:, :HEAD_DIM].astype(ko_ref.dtype)
    vto_ref[0, 0] = comp(vc_ref, pev_ref, w1v_ref, b1v_ref, w2v_ref).T[:HEAD_DIM].astype(vto_ref.dtype)


def _compress(kcr, vcr, pek, w1k, b1k, w2k, pev, w1v, b1v, w2v):
    B, G, S, _ = kcr.shape
    n = S // CMP_STRIDE
    xspec = pl.BlockSpec((1, 1, S, HEAD_DIM), lambda b, g: (b, g, 0, 0))
    full = lambda a: pl.BlockSpec(a.shape, lambda b, g: (0,) * a.ndim)
    return pl.pallas_call(
        _compress_kernel,
        grid=(B, G),
        in_specs=[xspec, xspec, full(pek), full(w1k), full(b1k), full(w2k),
                  full(pev), full(w1v), full(b1v), full(w2v)],
        out_specs=(pl.BlockSpec((1, 1, n, HEAD_DIM), lambda b, g: (b, g, 0, 0)),
                   pl.BlockSpec((1, 1, HEAD_DIM, n), lambda b, g: (b, g, 0, 0))),
        out_shape=(jax.ShapeDtypeStruct((B, G, n, HEAD_DIM), BF16),
                   jax.ShapeDtypeStruct((B, G, HEAD_DIM, n), BF16)),
        compiler_params=_params(("parallel", "parallel")),
        name="compress_kv",
    )(kcr, vcr, pek, w1k, b1k, w2k, pev, w1v, b1v, w2v)


def _nsa_kernel(*refs, tq, tk, tk_narrow):
    for g in range(N_KV_HEADS):
        _nsa_group(g, *refs, tq=tq, tk=tk, tk_narrow=tk_narrow)


def _nsa_group(g, qt_ref, qrt_ref, kc_ref, vct_ref, ks_ref, vst_ref, kw_ref, vwt_ref, gate_ref,
               covt_ref, o_ref, selbias_ref, rank_ref, *, tq, tk, tk_narrow):
    qt = pl.program_id(1)
    q0 = qt * tq
    n_cmp = kc_ref.shape[2]
    n_blk = covt_ref.shape[0]
    heads = lambda ref: jnp.concatenate([ref[0, GROUP * g + hh] for hh in range(GROUP)], axis=1)
    q_t = heads(qt_ref)
    qr_t = heads(qrt_ref)
    t_lane = q0 + lax.broadcasted_iota(jnp.int32, (1, tq), 1)
    per_head = lambda a: [a[:, hh * tq:(hh + 1) * tq] for hh in range(GROUP)]
    all_heads = lambda a: jnp.concatenate([a] * GROUP, axis=1)

    sc = _dot(kc_ref[0, g], q_t)
    cmp_end = lax.broadcasted_iota(jnp.int32, (n_cmp, 1), 0) * CMP_STRIDE + (CMP_LEN - 1)
    valid = cmp_end <= t_lane
    pcs = []
    for s_h in per_head(sc):
        s_h = jnp.where(valid, s_h, NEG)
        e = jnp.where(valid, jnp.exp2(s_h - jnp.max(s_h, axis=0, keepdims=True)), 0.0)
        pcs.append(e * (1.0 / jnp.maximum(jnp.sum(e, axis=0, keepdims=True), TINY)))
    oc_t = _dot(vct_ref[0, g],jnp.concatenate(pcs, axis=1).astype(BF16))

    psum = pcs[0] + pcs[1] + pcs[2] + pcs[3]
    p_hi = psum.astype(BF16)
    r1 = psum - p_hi.astype(F32)
    p_mid = r1.astype(BF16)
    p_lo = (r1 - p_mid.astype(F32)).astype(BF16)
    covt = covt_ref[...]
    imp_t = _dot(covt, p_hi) + _dot(covt, p_mid) + _dot(covt, p_lo)

    jb = lax.broadcasted_iota(jnp.int32, (n_blk, tq), 0)
    tb = (q0 + lax.broadcasted_iota(jnp.int32, (n_blk, tq), 1)) // SEL_BLOCK
    dist = tb - jb
    forced = (jb == 0) | ((dist >= 0) & (dist < N_LOCAL_BLOCKS))
    score = jnp.where(jb > tb, -jnp.inf, jnp.where(forced, jnp.inf, imp_t))
    rank_ref[...] = jnp.zeros((n_blk, tq), jnp.int32)
    n_live = (q0 + tq - 1) // SEL_BLOCK + 1
    for g8 in range(n_blk // SUBLANE):
        @pl.when(g8 * SUBLANE < n_live)
        def _(g8=g8):
            rank = rank_ref[...]
            for jp in range(g8 * SUBLANE, (g8 + 1) * SUBLANE):
                rowv = score[jp:jp + 1, :]
                beats = (rowv > score) | ((rowv == score) & (jb > jp))
                rank = rank + jnp.where(beats, 1, 0)
            rank_ref[...] = rank
    selbias_ref[...] = jnp.where(rank_ref[...] < min(SEL_TOP_N, n_blk), 0.0, NEG)

    R = GROUP * tq

    def with_ones(v_t):
        extra = jnp.where(lax.broadcasted_iota(jnp.int32, (BF16_ROWS, v_t.shape[1]), 0) == 0, 1.0, 0.0)
        return jnp.concatenate([v_t, extra.astype(BF16)], axis=0)

    def sel_chunk(c, carry, width, causal):
        m_i, l_i, acc = carry
        bpc = width // SEL_BLOCK
        vpc = width // LANE
        start = pl.multiple_of(c * width, width)
        k_c = ks_ref[0, g, pl.ds(start, width), :]
        v_t = jnp.concatenate([vst_ref[0, g, c * vpc + k] for k in range(vpc)], axis=1)
        rows = [jnp.broadcast_to(selbias_ref[pl.ds(c * bpc + j, 1), :], (SEL_BLOCK, tq)) for j in range(bpc)]
        bias = jnp.concatenate(rows, axis=0)
        if causal:
            kpos = start + lax.broadcasted_iota(jnp.int32, (width, 1), 0)
            bias = jnp.where(kpos <= t_lane, bias, NEG)
        s = _dot(k_c, qr_t) + all_heads(bias)
        m_new = jnp.maximum(m_i, jnp.max(s, axis=0, keepdims=True))
        alpha = jnp.exp2(m_i - m_new)
        pv = _dot(with_ones(v_t), jnp.exp2(s - m_new).astype(BF16))
        return m_new, alpha * l_i + pv[HEAD_DIM:HEAD_DIM + 1], alpha * acc + pv[:HEAD_DIM]

    init = (jnp.full((1, R), NEG, F32), jnp.zeros((1, R), F32), jnp.zeros((HEAD_DIM, R), F32))
    n_wide = q0 // tk
    j_diag = q0 // tk_narrow
    carry = lax.fori_loop(0, n_wide, lambda c, cr: sel_chunk(c, cr, tk, False), init)
    carry = lax.fori_loop(n_wide * (tk // tk_narrow), j_diag, lambda c, cr: sel_chunk(c, cr, tk_narrow, False), carry)
    _, l_s, acc_s = sel_chunk(j_diag, carry, tk_narrow, True)
    os_t = acc_s * (1.0 / jnp.maximum(l_s, TINY))

    wk = WINDOW + tq
    w0 = pl.multiple_of(jnp.maximum(q0 - WINDOW, 0), LANE)
    k_w = kw_ref[0, g, pl.ds(w0, wk), :]
    v_t = jnp.concatenate([vwt_ref[0, g, w0 // LANE + k] for k in range(wk // LANE)], axis=1)
    diff = t_lane - (w0 + lax.broadcasted_iota(jnp.int32, (wk, 1), 0))
    bias = jnp.where((diff >= 0) & (diff < WINDOW), 0.0, NEG)
    s = _dot(k_w, qr_t) + all_heads(bias)
    pv = _dot(with_ones(v_t), jnp.exp2(s - jnp.max(s, axis=0, keepdims=True)).astype(BF16))
    ow_t = pv[:HEAD_DIM] * (1.0 / pv[HEAD_DIM:HEAD_DIM + 1])

    gt = gate_ref[0, g]
    outs = []
    for hh, (a, b, c) in enumerate(zip(per_head(oc_t), per_head(os_t), per_head(ow_t))):
        r = hh * N_BRANCH
        outs.append(gt[r:r + 1] * a + gt[r + 1:r + 2] * b + gt[r + 2:r + 3] * c)
    gw = GROUP * HEAD_DIM
    o_ref[0, :, g * gw:(g + 1) * gw] = jnp.concatenate(outs, axis=0).T.astype(o_ref.dtype)


def _cover_table(S):
    n_cmp = S // CMP_STRIDE
    n_blk = S // SEL_BLOCK
    cs = np.arange(n_cmp) * CMP_STRIDE
    ss = np.arange(n_blk) * SEL_BLOCK
    cover_t = ((cs[None, :] < ss[:, None] + SEL_BLOCK) & (cs[None, :] + CMP_LEN > ss[:, None]))
    cover_t[:, n_cmp - 1] = False
    return jnp.asarray(cover_t, BF16)


def _nsa_attention(qt, qrt, kc, vct, ks, vst, kw, vwt, gates, tq=256, tk=1024):
    B, _, _, S = qt.shape
    tk = min(tk, S)
    assert tq % LANE == 0 and tk % tq == 0 and S % tk == 0 and S >= WINDOW + tq
    covt = _cover_table(S)
    n_blk = covt.shape[0]
    G = N_KV_HEADS
    qspec = pl.BlockSpec((1, N_HEADS, HEAD_DIM, tq), lambda b, i: (b, 0, 0, i))
    whole = lambda a: pl.BlockSpec((1,) + a.shape[1:], lambda b, i: (b,) + (0,) * (a.ndim - 1))
    return pl.pallas_call(
        functools.partial(_nsa_kernel, tq=tq, tk=tk, tk_narrow=min(tk, 2 * tq)),
        grid=(B, S // tq),
        in_specs=[qspec, qspec, whole(kc), whole(vct), whole(ks), whole(vst), whole(kw), whole(vwt),
                  pl.BlockSpec((1, G, GATE_ROWS, tq), lambda b, i: (b, 0, 0, i)),
                  pl.BlockSpec(covt.shape, lambda b, i: (0, 0))],
        out_specs=pl.BlockSpec((1, tq, NSA_WIDTH), lambda b, i: (b, i, 0)),
        out_shape=jax.ShapeDtypeStruct((B, S, NSA_WIDTH), BF16),
        scratch_shapes=[pltpu.VMEM((n_blk, tq), F32), pltpu.VMEM((n_blk, tq), jnp.int32)],
        compiler_params=_params(("parallel", "parallel")),
        name="nsa_attention",
    )(qt, qrt, kc, vct, ks, vst, kw, vwt, gates, covt)


def _xkv_kernel(m_ref, g_ref, w_ref, k_ref, v_ref):
    h = _rms(m_ref[0], g_ref[...]).astype(BF16)
    kv = _dot(h, w_ref[...])
    d = k_ref.shape[2]
    k_ref[0] = kv[:, :d].astype(k_ref.dtype)
    v_ref[0] = kv[:, d:].astype(v_ref.dtype)


def _xattn_kv(mem, g, wkv):
    B, M, D = mem.shape
    ospec = pl.BlockSpec((1, M, D), lambda b: (b, 0, 0))
    oshape = jax.ShapeDtypeStruct((B, M, D), BF16)
    return pl.pallas_call(
        _xkv_kernel,
        grid=(B,),
        in_specs=[pl.BlockSpec((1, M, D), lambda b: (b, 0, 0)),
                  pl.BlockSpec((1, D), lambda b: (0, 0)),
                  pl.BlockSpec(wkv.shape, lambda b: (0, 0))],
        out_specs=(ospec, ospec),
        out_shape=(oshape, oshape),
        compiler_params=_params(("parallel",)),
        name="xattn_kv",
    )(mem, g, wkv)


def _xattn_kernel(x_ref, a_ref, p_ref, wa_ref, wp_ref, g_ref, wq_ref, k_ref, v_ref, wo_ref, o_ref):
    x = x_ref[0] + _dot(a_ref[0], wa_ref[...]) + _dot(p_ref[0], wp_ref[...])
    h = _rms(x, g_ref[...]).astype(BF16)
    scale = X_HEAD_DIM ** -0.5
    q = (_dot(h, wq_ref[...]) * scale).astype(BF16)
    outs = []
    for hd in range(X_HEADS):
        sl = slice(hd * X_HEAD_DIM, (hd + 1) * X_HEAD_DIM)
        s = _dot_nt(q[:, sl], k_ref[0, :, sl])
        e = jnp.exp(s - jnp.max(s, axis=-1, keepdims=True))
        p = e / jnp.sum(e, axis=-1, keepdims=True)
        outs.append(_dot(p.astype(BF16), v_ref[0, :, sl]).astype(BF16))
    o = jnp.concatenate(outs, axis=-1)
    o_ref[0] = x + _dot(o, wo_ref[...])


def _xattn(x, a, p, wa, wp, g, wq, kx, vx, wo, tm=1024):
    B, S, D = x.shape
    M = kx.shape[1]
    row = lambda n: pl.BlockSpec((1, tm, n), lambda b, i: (b, i, 0))
    full = lambda w: pl.BlockSpec(w.shape, lambda b, i: (0, 0))
    mspec = pl.BlockSpec((1, M, D), lambda b, i: (b, 0, 0))
    return pl.pallas_call(
        _xattn_kernel,
        grid=(B, S // tm),
        in_specs=[row(D), row(a.shape[2]), row(p.shape[2]), full(wa), full(wp),
                  full(g), full(wq), mspec, mspec, full(wo)],
        out_specs=row(D),
        out_shape=jax.ShapeDtypeStruct((B, S, D), F32),
        compiler_params=_params(("parallel", "parallel")),
        name="out_proj_xattn",
    )(x, a, p, wa, wp, g, wq, kx, vx, wo)


def _silu(x):
    return x * jax.nn.sigmoid(x)


def _ffn_kernel(x_ref, g_ref, wg_ref, wu_ref, wd_ref, o_ref):
    x = x_ref[...]
    h = _rms(x, g_ref[...]).astype(BF16)
    act = _silu(_dot(h, wg_ref[...])) * _dot(h, wu_ref[...])
    o_ref[...] = x + _dot(act.astype(BF16), wd_ref[...])


def _ffn(x2, g, wg, wu, wd, tm=512):
    T, D = x2.shape
    resident = lambda w: pl.BlockSpec(w.shape, lambda i: (0, 0), pipeline_mode=pl.Buffered(1))
    return pl.pallas_call(
        _ffn_kernel,
        grid=(T // tm,),
        in_specs=[pl.BlockSpec((tm, D), lambda i: (i, 0)),
                  pl.BlockSpec((1, D), lambda i: (0, 0)),
                  resident(wg), resident(wu), resident(wd)],
        out_specs=pl.BlockSpec((tm, D), lambda i: (i, 0)),
        out_shape=jax.ShapeDtypeStruct((T, D), F32),
        compiler_params=_params(("parallel",)),
        name="dense_swiglu",
    )(x2, g, wg, wu, wd)


MOE_CHUNK = 512
MOE_ROW_TILE = 512
assert MOE_ROW_TILE % MOE_CHUNK == 0


MOE_PARTS = 2


def _tile_meta(m):
    return m >> (2 * MOE_PARTS), [(m & (1 << k)) != 0 for k in range(2 * MOE_PARTS)]


def _router_kernel(x_ref, g_ref, wr_ref, tri_ref, h_ref, route_ref, gate_ref, before_ref, total_ref, run_ref):
    c = pl.program_id(0)

    @pl.when(c == 0)
    def _():
        run_ref[...] = jnp.zeros_like(run_ref)

    hf = _rms(x_ref[...], g_ref[...])
    h_ref[...] = hf.astype(BF16)
    logits = jnp.dot(hf, wr_ref[...], preferred_element_type=F32, precision=lax.Precision.HIGHEST)
    lane = lax.broadcasted_iota(jnp.int32, logits.shape, 1)
    logits = jnp.where(lane < N_EXPERTS, logits, -jnp.inf)
    v1 = jnp.max(logits, axis=-1, keepdims=True)
    i1 = jnp.min(jnp.where(logits == v1, lane, LANE), axis=-1, keepdims=True)
    rest = jnp.where(lane == i1, -jnp.inf, logits)
    v2 = jnp.max(rest, axis=-1, keepdims=True)
    i2 = jnp.min(jnp.where(rest == v2, lane, LANE), axis=-1, keepdims=True)
    e2 = jnp.exp(v2 - v1)
    den = 1.0 + e2
    gate_ref[0] = jnp.where(lane == 0, 1.0 / den, jnp.where(lane == 1, e2 / den, 0.0)).T[:SUBLANE]

    onehot = jnp.where((lane == i1) | (lane == i2), 1.0, 0.0)
    run = run_ref[...]
    rank = run + _dot(tri_ref[...], onehot.astype(BF16))
    r1 = jnp.sum(jnp.where(lane == i1, rank, 0.0), axis=-1, keepdims=True).astype(jnp.int32)
    r2 = jnp.sum(jnp.where(lane == i2, rank, 0.0), axis=-1, keepdims=True).astype(jnp.int32)
    route = jnp.where(lane == 0, i1, jnp.where(lane == 1, i2, jnp.where(lane == 2, r1, jnp.where(lane == 3, r2, 0))))
    route_ref[0] = route.T[:SUBLANE]
    before_ref[0] = run
    run = run + jnp.sum(onehot, axis=0, keepdims=True)
    run_ref[...] = run
    total_ref[...] = run


def _router(x2, g, w_router):
    T, D = x2.shape
    C = T // MOE_CHUNK
    tri = jnp.asarray(np.tril(np.ones((MOE_CHUNK, MOE_CHUNK), np.float32), -1), BF16)
    row = lambda n: pl.BlockSpec((MOE_CHUNK, n), lambda c: (c, 0))
    full = lambda a: pl.BlockSpec(a.shape, lambda c: (0, 0))
    return pl.pallas_call(
        _router_kernel,
        grid=(C,),
        in_specs=[row(D), full(g), full(w_router), full(tri)],
        out_specs=(row(D),
                   pl.BlockSpec((1, SUBLANE, MOE_CHUNK), lambda c: (c, 0, 0)),
                   pl.BlockSpec((1, SUBLANE, MOE_CHUNK), lambda c: (c, 0, 0)),
                   pl.BlockSpec((1, 1, LANE), lambda c: (c, 0, 0)),
                   pl.BlockSpec((1, LANE), lambda c: (0, 0))),
        out_shape=(jax.ShapeDtypeStruct((T, D), BF16),
                   jax.ShapeDtypeStruct((C, SUBLANE, MOE_CHUNK), jnp.int32),
                   jax.ShapeDtypeStruct((C, SUBLANE, MOE_CHUNK), F32),
                   jax.ShapeDtypeStruct((C, 1, LANE), F32),
                   jax.ShapeDtypeStruct((1, LANE), F32)),
        scratch_shapes=[pltpu.VMEM((1, LANE), F32)],
        compiler_params=_params(("arbitrary",)),
        name="moe_router",
    )(x2, g, w_router, tri)


def _dispatch_kernel(a_ref, h_ref, tok_ref, gate_ref, zlo_ref, zhi_ref, zglo_ref, zghi_ref,
                     lo_ref, hi_ref, glo_ref, ghi_ref):
    del zlo_ref, zhi_ref, zglo_ref, zghi_ref
    e = pl.program_id(0)
    c = pl.program_id(1)
    idx = e * pl.num_programs(1) + c
    a, touched = _tile_meta(a_ref[idx])
    first = (c == 0) | (a != _tile_meta(a_ref[jnp.maximum(idx - 1, 0)])[0])

    @pl.when(first)
    def _():
        lo_ref[...] = jnp.zeros_like(lo_ref)
        hi_ref[...] = jnp.zeros_like(hi_ref)
        glo_ref[...] = jnp.zeros_like(glo_ref)
        ghi_ref[...] = jnp.zeros_like(ghi_ref)

    rows_per_part = MOE_CHUNK // MOE_PARTS

    def scatter(tile, part, x_ref, g_ref):
        tok = tok_ref[0]
        gts = gate_ref[0]
        r0 = part * rows_per_part
        rows = tile * MOE_CHUNK + r0 + lax.broadcasted_iota(jnp.int32, (rows_per_part, MOE_CHUNK), 0)
        c1 = tok[0:1] == rows
        c2 = tok[1:2] == rows
        x_ref[r0:r0 + rows_per_part, :] += _dot(jnp.where(c1 | c2, 1.0, 0.0).astype(BF16),
                                                h_ref[...]).astype(x_ref.dtype)
        g_ref[r0:r0 + rows_per_part, :] += jnp.sum(jnp.where(c1, gts[0:1], 0.0) + jnp.where(c2, gts[1:2], 0.0),
                                                   axis=1, keepdims=True)

    for k, (x_ref, g_ref) in enumerate(((lo_ref, glo_ref), (hi_ref, ghi_ref))):
        for part in range(MOE_PARTS):
            pl.when(touched[k * MOE_PARTS + part])(functools.partial(scatter, a + k, part, x_ref, g_ref))


def _dispatch(a_ec, h, tok_rows, gate_rows, n_slots):
    T, D = h.shape
    C = T // MOE_CHUNK
    zx = jnp.zeros((n_slots, D), BF16)
    zg = jnp.zeros((n_slots, 1), F32)
    any_spec = pl.BlockSpec(memory_space=pl.ANY)
    lo = lambda n: pl.BlockSpec((MOE_CHUNK, n), lambda e, c, a: (a[e * C + c] >> (2 * MOE_PARTS), 0))
    hi = lambda n: pl.BlockSpec((MOE_CHUNK, n), lambda e, c, a: ((a[e * C + c] >> (2 * MOE_PARTS)) + 1, 0))
    return pl.pallas_call(
        _dispatch_kernel,
        grid_spec=pltpu.PrefetchScalarGridSpec(
            num_scalar_prefetch=1,
            grid=(N_EXPERTS, C),
            in_specs=[pl.BlockSpec((MOE_CHUNK, D), lambda e, c, a: (c, 0)),
                      pl.BlockSpec((1, SUBLANE, MOE_CHUNK), lambda e, c, a: (c, 0, 0)),
                      pl.BlockSpec((1, SUBLANE, MOE_CHUNK), lambda e, c, a: (c, 0, 0)),
                      any_spec, any_spec, any_spec, any_spec],
            out_specs=(lo(D), hi(D), lo(1), hi(1))),
        out_shape=(jax.ShapeDtypeStruct(zx.shape, BF16), jax.ShapeDtypeStruct(zx.shape, BF16),
                   jax.ShapeDtypeStruct(zg.shape, F32), jax.ShapeDtypeStruct(zg.shape, F32)),
        input_output_aliases={4: 0, 5: 1, 6: 2, 7: 3},
        compiler_params=_params(("arbitrary", "arbitrary")),
        name="moe_dispatch",
    )(a_ec, h, tok_rows, gate_rows, zx, zx, zg, zg)


def _expert_ffn_kernel(te_ref, nu_ref, lo_ref, hi_ref, glo_ref, ghi_ref, wg_ref, wu_ref, wd_ref, o_ref, *, sub_rows):
    del te_ref
    used = pl.program_id(0) < nu_ref[0]

    @pl.when(used)
    def _():
        for r0 in range(0, o_ref.shape[0], sub_rows):
            rows = slice(r0, r0 + sub_rows)
            x = lo_ref[rows, :] + hi_ref[rows, :]
            act = _silu(_dot(x, wg_ref[0])) * _dot(x, wu_ref[0])
            o_ref[rows, :] = (_dot(act.astype(BF16), wd_ref[0])
                              * (glo_ref[rows, :] + ghi_ref[rows, :])).astype(o_ref.dtype)

    @pl.when(jnp.logical_not(used))
    def _():
        o_ref[...] = jnp.zeros_like(o_ref)


def _expert_ffn(tile_expert, n_used, xs_lo, xs_hi, gs_lo, gs_hi, wg, wu, wd, sub_rows=256):
    N, D = xs_lo.shape
    tr = MOE_ROW_TILE
    row = lambda n: pl.BlockSpec((tr, n), lambda i, te, nu: (i, 0))
    expert = lambda w: pl.BlockSpec((1,) + w.shape[1:], lambda i, te, nu: (te[i], 0, 0))
    return pl.pallas_call(
        functools.partial(_expert_ffn_kernel, sub_rows=sub_rows),
        grid_spec=pltpu.PrefetchScalarGridSpec(
            num_scalar_prefetch=2,
            grid=(N // tr,),
            in_specs=[row(D), row(D), row(1), row(1), expert(wg), expert(wu), expert(wd)],
            out_specs=row(D)),
        out_shape=jax.ShapeDtypeStruct((N, D), BF16),
        compiler_params=_params(("arbitrary",)),
        name="moe_expert_ffn",
    )(tile_expert, n_used, xs_lo, xs_hi, gs_lo, gs_hi, wg, wu, wd)


def _combine_kernel(meta_ref, hidx_ref, x_ref, *refs):
    del hidx_ref
    E = N_EXPERTS
    ylo, yhi = refs[:E], refs[E:2 * E]
    s1_ref, s2_ref, gf_ref, o_ref, acc_ref = refs[2 * E:]
    c = pl.program_id(0)
    s1 = s1_ref[...]
    s2 = s2_ref[...]

    def pick(tile):
        parts = []
        for j in range(MOE_CHUNK // LANE):
            cols = tile * MOE_CHUNK + j * LANE + lax.broadcasted_iota(jnp.int32, (MOE_CHUNK, LANE), 1)
            parts.append(jnp.where((s1 == cols) | (s2 == cols), 1.0, 0.0).astype(BF16))
        return jnp.concatenate(parts, axis=1)

    metas = [_tile_meta(meta_ref[c * E + e]) for e in range(E)]
    total = x_ref[...]
    for (a, touched), y_ref in zip(metas, ylo):
        in_lo = functools.reduce(jnp.logical_or, touched[:MOE_PARTS])
        total = total + _dot(pick(jnp.where(in_lo, a, -1)), y_ref[...])
    acc_ref[...] = total

    for (a, touched), y_ref in zip(metas, yhi):
        @pl.when(functools.reduce(jnp.logical_or, touched[MOE_PARTS:]))
        def _(a=a, y_ref=y_ref):
            acc_ref[...] += _dot(pick(a + 1), y_ref[...])

    o_ref[...] = _rms(acc_ref[...], gf_ref[...])


def _combine(meta_ce, hi_tile_ce, x2, ys, slot1_rep, slot2_rep, g_final):
    T, D = x2.shape
    C = T // MOE_CHUNK
    E = N_EXPERTS
    lo_spec = lambda e: pl.BlockSpec((MOE_CHUNK, D), lambda c, m, h: (m[c * E + e] >> (2 * MOE_PARTS), 0))
    hi_spec = lambda e: pl.BlockSpec((MOE_CHUNK, D), lambda c, m, h: (h[c * E + e], 0))
    return pl.pallas_call(
        _combine_kernel,
        grid_spec=pltpu.PrefetchScalarGridSpec(
            num_scalar_prefetch=2,
            grid=(C,),
            in_specs=([pl.BlockSpec((MOE_CHUNK, D), lambda c, m, h: (c, 0))]
                      + [lo_spec(e) for e in range(E)] + [hi_spec(e) for e in range(E)]
                      + [pl.BlockSpec((MOE_CHUNK, LANE), lambda c, m, h: (c, 0)),
                         pl.BlockSpec((MOE_CHUNK, LANE), lambda c, m, h: (c, 0)),
                         pl.BlockSpec((1, D), lambda c, m, h: (0, 0))]),
            out_specs=pl.BlockSpec((MOE_CHUNK, D), lambda c, m, h: (c, 0)),
            scratch_shapes=[pltpu.VMEM((MOE_CHUNK, D), F32)]),
        out_shape=jax.ShapeDtypeStruct((T, D), F32),
        compiler_params=_params(("arbitrary",)),
        name="moe_combine_final_norm",
    )(meta_ce, hi_tile_ce, x2, *([ys] * (2 * E)), slot1_rep, slot2_rep, g_final)


def _rope_tables(S):
    inv = ROPE_THETA ** (-jnp.arange(ROPE_HALF, dtype=F32) / ROPE_HALF)
    ang = jnp.arange(S, dtype=F32)[:, None] * inv[None, :]
    cos, sin = jnp.cos(ang), jnp.sin(ang)
    pad = HEAD_DIM - ROPE_DIM
    cos_h = jnp.concatenate([cos, cos, jnp.ones((S, pad), F32)], axis=-1)
    sin_h = jnp.concatenate([-sin, sin, jnp.zeros((S, pad), F32)], axis=-1)
    reps = LANE // HEAD_DIM
    return jnp.tile(cos_h, (1, reps)), jnp.tile(sin_h, (1, reps))


def _pack_w_in(w):
    per_group = GROUP * N_BRANCH
    gates = w[:, GATE_OFF:GATE_OFF + N_KV_HEADS * per_group].reshape(-1, N_KV_HEADS, per_group)
    gates = jnp.pad(gates, ((0, 0), (0, LANE // GATE_ROWS - N_KV_HEADS), (0, GATE_ROWS - per_group)))
    gates = gates.reshape(-1, LANE)
    return jnp.concatenate([w[:, :GATE_OFF], gates, w[:, GATE_OFF + N_KV_HEADS * per_group:]], axis=1).astype(BF16)


def _mixer_heads(x, norm_g, w_in, pe_k, w1_k, b1_k, w2_k, pe_v, w1_v, b1_v, w2_v,
                 w_pool, pool_scale, cos_t, sin_t):
    B, S, D = x.shape
    qt, qrt, kcr, vcr, ks, vst, kw, vwt, gates, p = _mixer_proj(
        x, norm_g.reshape(1, D), _pack_w_in(w_in), cos_t, sin_t, w_pool.astype(BF16), pool_scale.reshape(1, -1))
    half = CMP_LEN // 2
    cw = half * HEAD_DIM
    pad_w2 = lambda w2: jnp.pad(w2, ((0, 0), (0, LANE - HEAD_DIM))).astype(BF16)
    kc, vct = _compress(
        kcr, vcr,
        pe_k.reshape(2, cw), w1_k.reshape(2, cw, CMP_HIDDEN).astype(BF16), b1_k.reshape(1, -1), pad_w2(w2_k),
        pe_v.reshape(2, cw), w1_v.reshape(2, cw, CMP_HIDDEN).astype(BF16), b1_v.reshape(1, -1), pad_w2(w2_v))
    return _nsa_attention(qt, qrt, kc, vct, ks, vst, kw, vwt, gates), p


def _moe_layer(x2, g_ffn, router, wg, wu, wd, g_final):
    T, D = x2.shape
    E = N_EXPERTS
    C = T // MOE_CHUNK
    i32 = jnp.int32
    w_router = jnp.pad(router, ((0, 0), (0, LANE - E)))
    h, route, gate_rows, before, total = _router(x2, g_ffn.reshape(1, D), w_router)

    counts = total[0, :E].astype(i32)
    tiles = (counts + MOE_ROW_TILE - 1) // MOE_ROW_TILE
    ends = jnp.cumsum(tiles)
    off = (ends - tiles) * MOE_ROW_TILE
    e1, e2, r1, r2 = route[:, 0], route[:, 1], route[:, 2], route[:, 3]
    off_of = lambda e: sum(jnp.where(e == k, off[k], 0) for k in range(E))
    slot1 = off_of(e1) + r1
    slot2 = off_of(e2) + r2
    first = off[None, :] + before[:, 0, :E].astype(i32)
    after = jnp.concatenate([before[1:, 0, :E], total[:, :E]], axis=0).astype(i32)
    last = off[None, :] + after - 1
    lo_tile = first // MOE_CHUNK
    tile_meta = lo_tile << (2 * MOE_PARTS)
    part_rows = MOE_CHUNK // MOE_PARTS
    for k in range(2 * MOE_PARTS):
        start = lo_tile * MOE_CHUNK + k * part_rows
        touched = (last >= first) & (first < start + part_rows) & (last >= start)
        tile_meta = tile_meta + (touched.astype(i32) << k)
    n_row_tiles = (2 * T) // MOE_ROW_TILE + E + 1 + MOE_CHUNK // MOE_ROW_TILE
    n_used = ends[-1:]
    tile_ids = jnp.minimum(jnp.arange(n_row_tiles, dtype=i32), n_used[0] - 1)
    tile_expert = jnp.minimum(jnp.sum(ends[None, :] <= tile_ids[:, None], axis=1), E - 1).astype(i32)

    zero = jnp.zeros_like(slot1)
    tok_rows = jnp.stack([slot1, slot2, zero, zero, zero, zero, zero, zero], axis=1)
    slot1_rep = jnp.broadcast_to(slot1.reshape(T, 1), (T, LANE))
    slot2_rep = jnp.broadcast_to(slot2.reshape(T, 1), (T, LANE))

    xs_lo, xs_hi, gs_lo, gs_hi = _dispatch(tile_meta.T.reshape(-1).astype(i32), h, tok_rows, gate_rows,
                                           n_row_tiles * MOE_ROW_TILE)
    ys = _expert_ffn(tile_expert, n_used.astype(i32), xs_lo, xs_hi, gs_lo, gs_hi,
                     wg.astype(BF16), wu.astype(BF16), wd.astype(BF16))
    in_hi = (tile_meta >> MOE_PARTS) & ((1 << MOE_PARTS) - 1) != 0
    hi_tile = lax.cummax(jnp.where(in_hi, lo_tile + 1, 0), axis=0)
    return _combine(tile_meta.reshape(-1).astype(i32), hi_tile.reshape(-1).astype(i32), x2, ys,
                    slot1_rep, slot2_rep, g_final.reshape(1, D))


def kernel(x, mem, norm_mix, w_in, cmp_pe_k, cmp_w1_k, cmp_b1_k, cmp_w2_k, cmp_pe_v, cmp_w1_v, cmp_b1_v, cmp_w2_v, w_pool, pool_scale, w_out, norm_x, norm_mem, wq_x, wk_x, wv_x, wo_x, norm_ffn, ffn_wg, ffn_wu, ffn_wd, moe_router, moe_wg, moe_wu, moe_wd, norm_final):
    B, S, D = x.shape
    depth = norm_mix.shape[0]
    assert depth == 2, "the final RMSNorm is fused into the expert layer, which must be the last one"
    cos_t, sin_t = _rope_tables(S)
    for layer in range(depth):
        a, p = _mixer_heads(x, norm_mix[layer], w_in[layer],
                            cmp_pe_k[layer], cmp_w1_k[layer], cmp_b1_k[layer], cmp_w2_k[layer],
                            cmp_pe_v[layer], cmp_w1_v[layer], cmp_b1_v[layer], cmp_w2_v[layer],
                            w_pool[layer], pool_scale[layer], cos_t, sin_t)
        wkv = jnp.concatenate([wk_x[layer], wv_x[layer]], axis=1).astype(BF16)
        kx, vx = _xattn_kv(mem, norm_mem[layer].reshape(1, D), wkv)
        wo_mix = w_out[layer].astype(BF16)
        x = _xattn(x, a, p, wo_mix[:NSA_WIDTH], wo_mix[NSA_WIDTH:], norm_x[layer].reshape(1, D),
                   wq_x[layer].astype(BF16), kx, vx, wo_x[layer].astype(BF16))
        x2 = x.reshape(B * S, D)
        j = layer // 2
        if layer % 2 == 0:
            x2 = _ffn(x2, norm_ffn[layer].reshape(1, D), ffn_wg[j].astype(BF16), ffn_wu[j].astype(BF16),
                      ffn_wd[j].astype(BF16))
        else:
            x2 = _moe_layer(x2, norm_ffn[layer], moe_router[j], moe_wg[j], moe_wu[j], moe_wd[j], norm_final)
        x = x2.reshape(B, S, D)
    return x
```

```python
import functools
import math

import numpy as np
import jax
import jax.numpy as jnp
from jax import lax
from jax.experimental import pallas as pl
from jax.experimental.pallas import tpu as pltpu

F32 = jnp.float32
BF16 = jnp.bfloat16

D_MODEL = 1024
HEAD_DIM = 64
N_HEADS = 8
N_KV_HEADS = 2
GROUP = N_HEADS // N_KV_HEADS
NSA_WIDTH = N_HEADS * HEAD_DIM
KV_WIDTH = N_KV_HEADS * HEAD_DIM
N_BRANCH = 3
GATE_ROWS = 16
POOL_WIDTH = 512
POOL_WINDOWS = (2, 4, 8, 16)
POOL_GROUP = 128
POOL_HALO = 16
ROPE_DIM = 16
ROPE_HALF = 8
ROPE_THETA = 500000.0
CMP_LEN = 32
CMP_STRIDE = 16
CMP_HIDDEN = 256
SEL_BLOCK = 64
SEL_TOP_N = 16
N_LOCAL_BLOCKS = 2
WINDOW = 512
X_HEADS = 4
X_HEAD_DIM = 256
N_EXPERTS = 8
EPS = 1e-6

LANE = 128
SUBLANE = 8
BF16_ROWS = 16
IN_PACKED = NSA_WIDTH + 6 * KV_WIDTH + LANE + POOL_WIDTH
GATE_OFF = NSA_WIDTH + 6 * KV_WIDTH
POOL_OFF = GATE_OFF + LANE

NEG = -1e30
TINY = float(np.finfo(np.float32).tiny)
VMEM_LIMIT = 56 * 1024 * 1024


def _dot(a, b):
    return jnp.dot(a, b, preferred_element_type=F32)


def _dot_nt(a, b):
    return lax.dot_general(a, b, (((1,), (1,)), ((), ())), preferred_element_type=F32)


def _rms(x, g):
    y = x * lax.rsqrt(jnp.mean(x * x, axis=-1, keepdims=True) + EPS)
    return y * g


def _params(sem, limit=VMEM_LIMIT):
    return pltpu.CompilerParams(dimension_semantics=sem, vmem_limit_bytes=limit)


def _mixer_proj_kernel(x_ref, g_ref, w_ref, cos_ref, sin_ref, wpool_ref, pscale_ref,
                       qt_ref, qrt_ref, kcr_ref, vcr_ref, ks_ref, vst_ref, kw_ref, vwt_ref,
                       gate_ref, p_ref, pool_buf):
    h = _rms(x_ref[0], g_ref[...]).astype(BF16)
    z = _dot(h, w_ref[...])
    tm = z.shape[0]
    cos = cos_ref[...]
    sin = sin_ref[...]
    lane = lax.broadcasted_iota(jnp.int32, (tm, LANE), 1)
    first = (lane & (HEAD_DIM - 1)) < ROPE_HALF
    scale = HEAD_DIM ** -0.5 * math.log2(math.e)

    def rope(xs):
        partner = jnp.where(first, pltpu.roll(xs, LANE - ROPE_HALF, 1), pltpu.roll(xs, ROPE_HALF, 1))
        return xs * cos + partner * sin

    for s in range(NSA_WIDTH // LANE):
        xs = z[:, s * LANE:(s + 1) * LANE]
        for src, ref in ((xs, qt_ref), (rope(xs), qrt_ref)):
            t = (src * scale).T.astype(BF16)
            ref[0, 2 * s] = t[:HEAD_DIM]
            ref[0, 2 * s + 1] = t[HEAD_DIM:]

    def kv_slab(i):
        return z[:, NSA_WIDTH + i * KV_WIDTH:NSA_WIDTH + (i + 1) * KV_WIDTH]

    for slab, ref in ((kv_slab(0), kcr_ref), (kv_slab(1), vcr_ref),
                      (rope(kv_slab(2)), ks_ref), (rope(kv_slab(4)), kw_ref)):
        for gg in range(N_KV_HEADS):
            ref[0, gg] = slab[:, gg * HEAD_DIM:(gg + 1) * HEAD_DIM].astype(ref.dtype)

    for slab, ref in ((kv_slab(3), vst_ref), (kv_slab(5), vwt_ref)):
        t = slab.T.astype(BF16)
        for gg in range(N_KV_HEADS):
            for k in range(tm // LANE):
                ref[0, gg, k] = t[gg * HEAD_DIM:(gg + 1) * HEAD_DIM, k * LANE:(k + 1) * LANE]

    sig_t = jax.nn.sigmoid(z[:, GATE_OFF:GATE_OFF + LANE]).T
    for gg in range(N_KV_HEADS):
        gate_ref[0, gg] = sig_t[gg * GATE_ROWS:(gg + 1) * GATE_ROWS]

    i = pl.program_id(1)

    @pl.when(i == 0)
    def _():
        pool_buf[0:POOL_HALO, :] = jnp.zeros((POOL_HALO, POOL_WIDTH), F32)

    @pl.when(i > 0)
    def _():
        pool_buf[0:POOL_HALO, :] = pool_buf[tm:tm + POOL_HALO, :]

    pool_buf[POOL_HALO:POOL_HALO + tm, :] = z[:, POOL_OFF:POOL_OFF + POOL_WIDTH]
    t1 = i * tm + lax.broadcasted_iota(jnp.int32, (tm, 1), 0) + 1
    for gi, w in enumerate(POOL_WINDOWS):
        cols = slice(gi * POOL_GROUP, (gi + 1) * POOL_GROUP)
        cur = pool_buf[POOL_HALO:POOL_HALO + tm, cols]
        tot = cur
        for k in range(1, w):
            tot = tot + pool_buf[POOL_HALO - k:POOL_HALO - k + tm, cols]
        d = tot / jnp.minimum(t1, w).astype(F32) - cur
        p_ref[0, :, cols] = (_dot(d.astype(BF16), wpool_ref[gi]) * pscale_ref[:, cols]).astype(p_ref.dtype)


def _mixer_proj(x, g, w_packed, cos_t, sin_t, w_pool, pool_scale, tm=1024):
    B, S, D = x.shape
    G = N_KV_HEADS
    sd = jax.ShapeDtypeStruct
    out_shape = (sd((B, N_HEADS, HEAD_DIM, S), BF16), sd((B, N_HEADS, HEAD_DIM, S), BF16),
                 sd((B, G, S, HEAD_DIM), F32), sd((B, G, S, HEAD_DIM), F32),
                 sd((B, G, S, HEAD_DIM), BF16), sd((B, G, S // LANE, HEAD_DIM, LANE), BF16),
                 sd((B, G, S, HEAD_DIM), BF16), sd((B, G, S // LANE, HEAD_DIM, LANE), BF16),
                 sd((B, G, GATE_ROWS, S), F32),
                 sd((B, S, POOL_WIDTH), BF16))
    qspec = pl.BlockSpec((1, N_HEADS, HEAD_DIM, tm), lambda b, i: (b, 0, 0, i))
    kspec = pl.BlockSpec((1, G, tm, HEAD_DIM), lambda b, i: (b, 0, i, 0))
    vspec = pl.BlockSpec((1, G, tm // LANE, HEAD_DIM, LANE), lambda b, i: (b, 0, i, 0, 0))
    out_specs = (qspec, qspec, kspec, kspec, kspec, vspec, kspec, vspec,
                 pl.BlockSpec((1, G, GATE_ROWS, tm), lambda b, i: (b, 0, 0, i)),
                 pl.BlockSpec((1, tm, POOL_WIDTH), lambda b, i: (b, i, 0)))
    return pl.pallas_call(
        _mixer_proj_kernel,
        grid=(B, S // tm),
        in_specs=[pl.BlockSpec((1, tm, D), lambda b, i: (b, i, 0)),
                  pl.BlockSpec((1, D), lambda b, i: (0, 0)),
                  pl.BlockSpec((D, IN_PACKED), lambda b, i: (0, 0)),
                  pl.BlockSpec((tm, LANE), lambda b, i: (i, 0)),
                  pl.BlockSpec((tm, LANE), lambda b, i: (i, 0)),
                  pl.BlockSpec(w_pool.shape, lambda b, i: (0, 0, 0)),
                  pl.BlockSpec((1, POOL_WIDTH), lambda b, i: (0, 0))],
        out_specs=out_specs,
        out_shape=out_shape,
        scratch_shapes=[pltpu.VMEM((tm + POOL_HALO, POOL_WIDTH), F32)],
        compiler_params=_params(("parallel", "arbitrary")),
        name="mixer_proj",
    )(x, g, w_packed, cos_t, sin_t, w_pool, pool_scale)


def _gelu_tanh(x):
    return 0.5 * x * (1.0 + jnp.tanh(math.sqrt(2.0 / math.pi) * (x + 0.044715 * (x * x * x))))


def _compress_kernel(kc_ref, vc_ref, pek_ref, w1k_ref, b1k_ref, w2k_ref,
                     pev_ref, w1v_ref, b1v_ref, w2v_ref, ko_ref, vto_ref):
    def comp(x_ref, pe_ref, w1_ref, b1_ref, w2_ref):
        n = x_ref.shape[2] // CMP_STRIDE
        x = jnp.concatenate([x_ref[0, 0, pl.ds(l, n, stride=CMP_STRIDE), :] for l in range(CMP_STRIDE)], axis=1)
        a = _dot((x + pe_ref[0:1, :]).astype(BF16), w1_ref[0])
        b = _dot((x + pe_ref[1:2, :]).astype(BF16), w1_ref[1])
        pre = a + pltpu.roll(b, n - 1, 0) + b1_ref[...]
        return _dot(_gelu_tanh(pre).astype(BF16), w2_ref[...])

    ko_ref[0, 0] = comp(kc_ref, pek_ref, w1k_ref, b1k_ref, w2k_ref)[:, :HEAD_DIM].astype(ko_ref.dtype)
    vto_ref[0, 0] = comp(vc_ref, pev_ref, w1v_ref, b1v_ref, w2v_ref).T[:HEAD_DIM].astype(vto_ref.dtype)


def _compress(kcr, vcr, pek, w1k, b1k, w2k, pev, w1v, b1v, w2v):
    B, G, S, _ = kcr.shape
    n = S // CMP_STRIDE
    xspec = pl.BlockSpec((1, 1, S, HEAD_DIM), lambda b, g: (b, g, 0, 0))
    full = lambda a: pl.BlockSpec(a.shape, lambda b, g: (0,) * a.ndim)
    return pl.pallas_call(
        _compress_kernel,
        grid=(B, G),
        in_specs=[xspec, xspec, full(pek), full(w1k), full(b1k), full(w2k),
                  full(pev), full(w1v), full(b1v), full(w2v)],
        out_specs=(pl.BlockSpec((1, 1, n, HEAD_DIM), lambda b, g: (b, g, 0, 0)),
                   pl.BlockSpec((1, 1, HEAD_DIM, n), lambda b, g: (b, g, 0, 0))),
        out_shape=(jax.ShapeDtypeStruct((B, G, n, HEAD_DIM), BF16),
                   jax.ShapeDtypeStruct((B, G, HEAD_DIM, n), BF16)),
        compiler_params=_params(("parallel", "parallel")),
        name="compress_kv",
    )(kcr, vcr, pek, w1k, b1k, w2k, pev, w1v, b1v, w2v)


def _nsa_kernel(*refs, tq, tk, tk_narrow, tiles):
    for sub in range(tiles):
        for g in range(N_KV_HEADS):
            _nsa_group(g, sub, tiles, *refs, tq=tq, tk=tk, tk_narrow=tk_narrow)


def _nsa_group(g, sub, tiles, qt_ref, qrt_ref, kc_ref, vct_ref, ks_ref, vst_ref, kw_ref, vwt_ref, gate_ref,
               covt_ref, o_ref, selbias_ref, rank_ref, *, tq, tk, tk_narrow):
    qt = pl.program_id(1) * tiles + sub
    lanes = slice(sub * tq, (sub + 1) * tq)
    q0 = qt * tq
    n_cmp = kc_ref.shape[2]
    n_blk = covt_ref.shape[0]
    heads = lambda ref: jnp.concatenate([ref[0, GROUP * g + hh, :, lanes] for hh in range(GROUP)], axis=1)
    q_t = heads(qt_ref)
    qr_t = heads(qrt_ref)
    t_lane = q0 + lax.broadcasted_iota(jnp.int32, (1, tq), 1)
    per_head = lambda a: [a[:, hh * tq:(hh + 1) * tq] for hh in range(GROUP)]
    all_heads = lambda a: jnp.concatenate([a] * GROUP, axis=1)

    sc = _dot(kc_ref[0, g], q_t)
    cmp_end = lax.broadcasted_iota(jnp.int32, (n_cmp, 1), 0) * CMP_STRIDE + (CMP_LEN - 1)
    valid = cmp_end <= t_lane
    pcs = []
    for s_h in per_head(sc):
        s_h = jnp.where(valid, s_h, NEG)
        e = jnp.where(valid, jnp.exp2(s_h - jnp.max(s_h, axis=0, keepdims=True)), 0.0)
        pcs.append(e * (1.0 / jnp.maximum(jnp.sum(e, axis=0, keepdims=True), TINY)))
    oc_t = _dot(vct_ref[0, g],jnp.concatenate(pcs, axis=1).astype(BF16))

    psum = pcs[0] + pcs[1] + pcs[2] + pcs[3]
    p_hi = psum.astype(BF16)
    r1 = psum - p_hi.astype(F32)
    p_mid = r1.astype(BF16)
    p_lo = (r1 - p_mid.astype(F32)).astype(BF16)
    covt = covt_ref[...]
    imp_t = _dot(covt, p_hi) + _dot(covt, p_mid) + _dot(covt, p_lo)

    jb = lax.broadcasted_iota(jnp.int32, (n_blk, tq), 0)
    tb = (q0 + lax.broadcasted_iota(jnp.int32, (n_blk, tq), 1)) // SEL_BLOCK
    dist = tb - jb
    forced = (jb == 0) | ((dist >= 0) & (dist < N_LOCAL_BLOCKS))
    score = jnp.where(jb > tb, -jnp.inf, jnp.where(forced, jnp.inf, imp_t))
    rank_ref[...] = jnp.zeros((n_blk, tq), jnp.int32)
    n_live = (q0 + tq - 1) // SEL_BLOCK + 1
    for g8 in range(n_blk // SUBLANE):
        @pl.when(g8 * SUBLANE < n_live)
        def _(g8=g8):
            rank = rank_ref[...]
            for jp in range(g8 * SUBLANE, (g8 + 1) * SUBLANE):
                rowv = score[jp:jp + 1, :]
                beats = (rowv > score) | ((rowv == score) & (jb > jp))
                rank = rank + jnp.where(beats, 1, 0)
            rank_ref[...] = rank
    selbias_ref[...] = jnp.where(rank_ref[...] < min(SEL_TOP_N, n_blk), 0.0, NEG)

    R = GROUP * tq

    def with_ones(v_t):
        extra = jnp.where(lax.broadcasted_iota(jnp.int32, (BF16_ROWS, v_t.shape[1]), 0) == 0, 1.0, 0.0)
        return jnp.concatenate([v_t, extra.astype(BF16)], axis=0)

    def sel_chunk(c, carry, width, causal):
        m_i, l_i, acc = carry
        bpc = width // SEL_BLOCK
        vpc = width // LANE
        start = pl.multiple_of(c * width, width)
        k_c = ks_ref[0, g, pl.ds(start, width), :]
        v_t = jnp.concatenate([vst_ref[0, g, c * vpc + k] for k in range(vpc)], axis=1)
        rows = [jnp.broadcast_to(selbias_ref[pl.ds(c * bpc + j, 1), :], (SEL_BLOCK, tq)) for j in range(bpc)]
        bias = jnp.concatenate(rows, axis=0)
        if causal:
            kpos = start + lax.broadcasted_iota(jnp.int32, (width, 1), 0)
            bias = jnp.where(kpos <= t_lane, bias, NEG)
        s = _dot(k_c, qr_t) + all_heads(bias)
        m_new = jnp.maximum(m_i, jnp.max(s, axis=0, keepdims=True))
        alpha = jnp.exp2(m_i - m_new)
        pv = _dot(with_ones(v_t), jnp.exp2(s - m_new).astype(BF16))
        return m_new, alpha * l_i + pv[HEAD_DIM:HEAD_DIM + 1], alpha * acc + pv[:HEAD_DIM]

    init = (jnp.full((1, R), NEG, F32), jnp.zeros((1, R), F32), jnp.zeros((HEAD_DIM, R), F32))
    n_wide = q0 // tk
    j_diag = q0 // tk_narrow
    carry = lax.fori_loop(0, n_wide, lambda c, cr: sel_chunk(c, cr, tk, False), init)
    carry = lax.fori_loop(n_wide * (tk // tk_narrow), j_diag, lambda c, cr: sel_chunk(c, cr, tk_narrow, False), carry)
    _, l_s, acc_s = sel_chunk(j_diag, carry, tk_narrow, True)
    os_t = acc_s * (1.0 / jnp.maximum(l_s, TINY))

    wk = WINDOW + tq
    w0 = pl.multiple_of(jnp.maximum(q0 - WINDOW, 0), LANE)
    k_w = kw_ref[0, g, pl.ds(w0, wk), :]
    v_t = jnp.concatenate([vwt_ref[0, g, w0 // LANE + k] for k in range(wk // LANE)], axis=1)
    diff = t_lane - (w0 + lax.broadcasted_iota(jnp.int32, (wk, 1), 0))
    bias = jnp.where((diff >= 0) & (diff < WINDOW), 0.0, NEG)
    s = _dot(k_w, qr_t) + all_heads(bias)
    pv = _dot(with_ones(v_t), jnp.exp2(s - jnp.max(s, axis=0, keepdims=True)).astype(BF16))
    ow_t = pv[:HEAD_DIM] * (1.0 / pv[HEAD_DIM:HEAD_DIM + 1])

    gt = gate_ref[0, g, :, lanes]
    outs = []
    for hh, (a, b, c) in enumerate(zip(per_head(oc_t), per_head(os_t), per_head(ow_t))):
        r = hh * N_BRANCH
        outs.append(gt[r:r + 1] * a + gt[r + 1:r + 2] * b + gt[r + 2:r + 3] * c)
    gw = GROUP * HEAD_DIM
    o_ref[0, lanes, g * gw:(g + 1) * gw] = jnp.concatenate(outs, axis=0).T.astype(o_ref.dtype)


def _cover_table(S):
    n_cmp = S // CMP_STRIDE
    n_blk = S // SEL_BLOCK
    cs = np.arange(n_cmp) * CMP_STRIDE
    ss = np.arange(n_blk) * SEL_BLOCK
    cover_t = ((cs[None, :] < ss[:, None] + SEL_BLOCK) & (cs[None, :] + CMP_LEN > ss[:, None]))
    cover_t[:, n_cmp - 1] = False
    return jnp.asarray(cover_t, BF16)


def _nsa_attention(qt, qrt, kc, vct, ks, vst, kw, vwt, gates, tq=256, tk=1024, tiles=2):
    B, _, _, S = qt.shape
    tk = min(tk, S)
    assert tq % LANE == 0 and tk % tq == 0 and S % tk == 0 and S >= WINDOW + tq
    covt = _cover_table(S)
    n_blk = covt.shape[0]
    G = N_KV_HEADS
    qspec = pl.BlockSpec((1, N_HEADS, HEAD_DIM, tiles * tq), lambda b, i: (b, 0, 0, i))
    whole = lambda a: pl.BlockSpec((1,) + a.shape[1:], lambda b, i: (b,) + (0,) * (a.ndim - 1))
    return pl.pallas_call(
        functools.partial(_nsa_kernel, tq=tq, tk=tk, tk_narrow=min(tk, 2 * tq), tiles=tiles),
        grid=(B, S // (tiles * tq)),
        in_specs=[qspec, qspec, whole(kc), whole(vct), whole(ks), whole(vst), whole(kw), whole(vwt),
                  pl.BlockSpec((1, G, GATE_ROWS, tiles * tq), lambda b, i: (b, 0, 0, i)),
                  pl.BlockSpec(covt.shape, lambda b, i: (0, 0))],
        out_specs=pl.BlockSpec((1, tiles * tq, NSA_WIDTH), lambda b, i: (b, i, 0)),
        out_shape=jax.ShapeDtypeStruct((B, S, NSA_WIDTH), BF16),
        scratch_shapes=[pltpu.VMEM((n_blk, tq), F32), pltpu.VMEM((n_blk, tq), jnp.int32)],
        compiler_params=_params(("parallel", "parallel")),
        name="nsa_attention",
    )(qt, qrt, kc, vct, ks, vst, kw, vwt, gates, covt)


def _xkv_kernel(m_ref, g_ref, w_ref, k_ref, v_ref):
    h = _rms(m_ref[0], g_ref[...]).astype(BF16)
    kv = _dot(h, w_ref[...])
    d = k_ref.shape[2]
    k_ref[0] = kv[:, :d].astype(k_ref.dtype)
    v_ref[0] = kv[:, d:].astype(v_ref.dtype)


def _xattn_kv(mem, g, wkv):
    B, M, D = mem.shape
    ospec = pl.BlockSpec((1, M, D), lambda b: (b, 0, 0))
    oshape = jax.ShapeDtypeStruct((B, M, D), BF16)
    return pl.pallas_call(
        _xkv_kernel,
        grid=(B,),
        in_specs=[pl.BlockSpec((1, M, D), lambda b: (b, 0, 0)),
                  pl.BlockSpec((1, D), lambda b: (0, 0)),
                  pl.BlockSpec(wkv.shape, lambda b: (0, 0))],
        out_specs=(ospec, ospec),
        out_shape=(oshape, oshape),
        compiler_params=_params(("parallel",)),
        name="xattn_kv",
    )(mem, g, wkv)


def _xattn_kernel(x_ref, a_ref, p_ref, wa_ref, wp_ref, g_ref, wq_ref, k_ref, v_ref, wo_ref, o_ref):
    x = x_ref[0] + _dot(a_ref[0], wa_ref[...]) + _dot(p_ref[0], wp_ref[...])
    h = _rms(x, g_ref[...]).astype(BF16)
    scale = X_HEAD_DIM ** -0.5
    q = (_dot(h, wq_ref[...]) * scale).astype(BF16)
    outs = []
    for hd in range(X_HEADS):
        sl = slice(hd * X_HEAD_DIM, (hd + 1) * X_HEAD_DIM)
        s = _dot_nt(q[:, sl], k_ref[0, :, sl])
        e = jnp.exp(s - jnp.max(s, axis=-1, keepdims=True))
        p = e / jnp.sum(e, axis=-1, keepdims=True)
        outs.append(_dot(p.astype(BF16), v_ref[0, :, sl]).astype(BF16))
    o = jnp.concatenate(outs, axis=-1)
    o_ref[0] = x + _dot(o, wo_ref[...])


def _xattn(x, a, p, wa, wp, g, wq, kx, vx, wo, tm=1024):
    B, S, D = x.shape
    M = kx.shape[1]
    row = lambda n: pl.BlockSpec((1, tm, n), lambda b, i: (b, i, 0))
    full = lambda w: pl.BlockSpec(w.shape, lambda b, i: (0, 0))
    mspec = pl.BlockSpec((1, M, D), lambda b, i: (b, 0, 0))
    return pl.pallas_call(
        _xattn_kernel,
        grid=(B, S // tm),
        in_specs=[row(D), row(a.shape[2]), row(p.shape[2]), full(wa), full(wp),
                  full(g), full(wq), mspec, mspec, full(wo)],
        out_specs=row(D),
        out_shape=jax.ShapeDtypeStruct((B, S, D), F32),
        compiler_params=_params(("parallel", "parallel")),
        name="out_proj_xattn",
    )(x, a, p, wa, wp, g, wq, kx, vx, wo)


def _silu(x):
    return x * jax.nn.sigmoid(x)


def _ffn_kernel(x_ref, g_ref, wg_ref, wu_ref, wd_ref, o_ref):
    x = x_ref[...]
    h = _rms(x, g_ref[...]).astype(BF16)
    act = _silu(_dot(h, wg_ref[...])) * _dot(h, wu_ref[...])
    o_ref[...] = x + _dot(act.astype(BF16), wd_ref[...])


def _ffn(x2, g, wg, wu, wd, tm=512):
    T, D = x2.shape
    resident = lambda w: pl.BlockSpec(w.shape, lambda i: (0, 0), pipeline_mode=pl.Buffered(1))
    return pl.pallas_call(
        _ffn_kernel,
        grid=(T // tm,),
        in_specs=[pl.BlockSpec((tm, D), lambda i: (i, 0)),
                  pl.BlockSpec((1, D), lambda i: (0, 0)),
                  resident(wg), resident(wu), resident(wd)],
        out_specs=pl.BlockSpec((tm, D), lambda i: (i, 0)),
        out_shape=jax.ShapeDtypeStruct((T, D), F32),
        compiler_params=_params(("parallel",)),
        name="dense_swiglu",
    )(x2, g, wg, wu, wd)


MOE_CHUNK = 512
MOE_ROW_TILE = 512
assert MOE_ROW_TILE % MOE_CHUNK == 0


MOE_PARTS = 2


def _tile_meta(m):
    return m >> (2 * MOE_PARTS), [(m & (1 << k)) != 0 for k in range(2 * MOE_PARTS)]


def _router_kernel(x_ref, g_ref, wr_ref, tri_ref, h_ref, route_ref, gate_ref, before_ref, total_ref, run_ref):
    c = pl.program_id(0)

    @pl.when(c == 0)
    def _():
        run_ref[...] = jnp.zeros_like(run_ref)

    hf = _rms(x_ref[...], g_ref[...])
    h_ref[...] = hf.astype(BF16)
    logits = jnp.dot(hf, wr_ref[...], preferred_element_type=F32, precision=lax.Precision.HIGHEST)
    lane = lax.broadcasted_iota(jnp.int32, logits.shape, 1)
    logits = jnp.where(lane < N_EXPERTS, logits, -jnp.inf)
    v1 = jnp.max(logits, axis=-1, keepdims=True)
    i1 = jnp.min(jnp.where(logits == v1, lane, LANE), axis=-1, keepdims=True)
    rest = jnp.where(lane == i1, -jnp.inf, logits)
    v2 = jnp.max(rest, axis=-1, keepdims=True)
    i2 = jnp.min(jnp.where(rest == v2, lane, LANE), axis=-1, keepdims=True)
    e2 = jnp.exp(v2 - v1)
    den = 1.0 + e2
    gate_ref[0] = jnp.where(lane == 0, 1.0 / den, jnp.where(lane == 1, e2 / den, 0.0)).T[:SUBLANE]

    onehot = jnp.where((lane == i1) | (lane == i2), 1.0, 0.0)
    run = run_ref[...]
    rank = run + _dot(tri_ref[...], onehot.astype(BF16))
    r1 = jnp.sum(jnp.where(lane == i1, rank, 0.0), axis=-1, keepdims=True).astype(jnp.int32)
    r2 = jnp.sum(jnp.where(lane == i2, rank, 0.0), axis=-1, keepdims=True).astype(jnp.int32)
    route = jnp.where(lane == 0, i1, jnp.where(lane == 1, i2, jnp.where(lane == 2, r1, jnp.where(lane == 3, r2, 0))))
    route_ref[0] = route.T[:SUBLANE]
    before_ref[0] = run
    run = run + jnp.sum(onehot, axis=0, keepdims=True)
    run_ref[...] = run
    total_ref[...] = run


def _router(x2, g, w_router):
    T, D = x2.shape
    C = T // MOE_CHUNK
    tri = jnp.asarray(np.tril(np.ones((MOE_CHUNK, MOE_CHUNK), np.float32), -1), BF16)
    row = lambda n: pl.BlockSpec((MOE_CHUNK, n), lambda c: (c, 0))
    full = lambda a: pl.BlockSpec(a.shape, lambda c: (0, 0))
    return pl.pallas_call(
        _router_kernel,
        grid=(C,),
        in_specs=[row(D), full(g), full(w_router), full(tri)],
        out_specs=(row(D),
                   pl.BlockSpec((1, SUBLANE, MOE_CHUNK), lambda c: (c, 0, 0)),
                   pl.BlockSpec((1, SUBLANE, MOE_CHUNK), lambda c: (c, 0, 0)),
                   pl.BlockSpec((1, 1, LANE), lambda c: (c, 0, 0)),
                   pl.BlockSpec((1, LANE), lambda c: (0, 0))),
        out_shape=(jax.ShapeDtypeStruct((T, D), BF16),
                   jax.ShapeDtypeStruct((C, SUBLANE, MOE_CHUNK), jnp.int32),
                   jax.ShapeDtypeStruct((C, SUBLANE, MOE_CHUNK), F32),
                   jax.ShapeDtypeStruct((C, 1, LANE), F32),
                   jax.ShapeDtypeStruct((1, LANE), F32)),
        scratch_shapes=[pltpu.VMEM((1, LANE), F32)],
        compiler_params=_params(("arbitrary",)),
        name="moe_router",
    )(x2, g, w_router, tri)


def _dispatch_kernel(a_ref, h_ref, tok_ref, gate_ref, zlo_ref, zhi_ref, zglo_ref, zghi_ref,
                     lo_ref, hi_ref, glo_ref, ghi_ref):
    del zlo_ref, zhi_ref, zglo_ref, zghi_ref
    e = pl.program_id(0)
    c = pl.program_id(1)
    idx = e * pl.num_programs(1) + c
    a, touched = _tile_meta(a_ref[idx])
    first = (c == 0) | (a != _tile_meta(a_ref[jnp.maximum(idx - 1, 0)])[0])

    @pl.when(first)
    def _():
        lo_ref[...] = jnp.zeros_like(lo_ref)
        hi_ref[...] = jnp.zeros_like(hi_ref)
        glo_ref[...] = jnp.zeros_like(glo_ref)
        ghi_ref[...] = jnp.zeros_like(ghi_ref)

    rows_per_part = MOE_CHUNK // MOE_PARTS

    def scatter(tile, part, x_ref, g_ref):
        tok = tok_ref[0]
        gts = gate_ref[0]
        r0 = part * rows_per_part
        rows = tile * MOE_CHUNK + r0 + lax.broadcasted_iota(jnp.int32, (rows_per_part, MOE_CHUNK), 0)
        c1 = tok[0:1] == rows
        c2 = tok[1:2] == rows
        x_ref[r0:r0 + rows_per_part, :] += _dot(jnp.where(c1 | c2, 1.0, 0.0).astype(BF16),
                                                h_ref[...]).astype(x_ref.dtype)
        g_ref[r0:r0 + rows_per_part, :] += jnp.sum(jnp.where(c1, gts[0:1], 0.0) + jnp.where(c2, gts[1:2], 0.0),
                                                   axis=1, keepdims=True)

    for k, (x_ref, g_ref) in enumerate(((lo_ref, glo_ref), (hi_ref, ghi_ref))):
        for part in range(MOE_PARTS):
            pl.when(touched[k * MOE_PARTS + part])(functools.partial(scatter, a + k, part, x_ref, g_ref))


def _dispatch(a_ec, h, tok_rows, gate_rows, n_slots):
    T, D = h.shape
    C = T // MOE_CHUNK
    zx = jnp.zeros((n_slots, D), BF16)
    zg = jnp.zeros((n_slots, 1), F32)
    any_spec = pl.BlockSpec(memory_space=pl.ANY)
    lo = lambda n: pl.BlockSpec((MOE_CHUNK, n), lambda e, c, a: (a[e * C + c] >> (2 * MOE_PARTS), 0))
    hi = lambda n: pl.BlockSpec((MOE_CHUNK, n), lambda e, c, a: ((a[e * C + c] >> (2 * MOE_PARTS)) + 1, 0))
    return pl.pallas_call(
        _dispatch_kernel,
        grid_spec=pltpu.PrefetchScalarGridSpec(
            num_scalar_prefetch=1,
            grid=(N_EXPERTS, C),
            in_specs=[pl.BlockSpec((MOE_CHUNK, D), lambda e, c, a: (c, 0)),
                      pl.BlockSpec((1, SUBLANE, MOE_CHUNK), lambda e, c, a: (c, 0, 0)),
                      pl.BlockSpec((1, SUBLANE, MOE_CHUNK), lambda e, c, a: (c, 0, 0)),
                      any_spec, any_spec, any_spec, any_spec],
            out_specs=(lo(D), hi(D), lo(1), hi(1))),
        out_shape=(jax.ShapeDtypeStruct(zx.shape, BF16), jax.ShapeDtypeStruct(zx.shape, BF16),
                   jax.ShapeDtypeStruct(zg.shape, F32), jax.ShapeDtypeStruct(zg.shape, F32)),
        input_output_aliases={4: 0, 5: 1, 6: 2, 7: 3},
        compiler_params=_params(("arbitrary", "arbitrary")),
        name="moe_dispatch",
    )(a_ec, h, tok_rows, gate_rows, zx, zx, zg, zg)


def _expert_ffn_kernel(te_ref, nu_ref, lo_ref, hi_ref, glo_ref, ghi_ref, wg_ref, wu_ref, wd_ref, o_ref, *, sub_rows):
    del te_ref
    used = pl.program_id(0) < nu_ref[0]

    @pl.when(used)
    def _():
        for r0 in range(0, o_ref.shape[0], sub_rows):
            rows = slice(r0, r0 + sub_rows)
            x = lo_ref[rows, :] + hi_ref[rows, :]
            act = _silu(_dot(x, wg_ref[0])) * _dot(x, wu_ref[0])
            o_ref[rows, :] = (_dot(act.astype(BF16), wd_ref[0])
                              * (glo_ref[rows, :] + ghi_ref[rows, :])).astype(o_ref.dtype)

    @pl.when(jnp.logical_not(used))
    def _():
        o_ref[...] = jnp.zeros_like(o_ref)


def _expert_ffn(tile_expert, n_used, xs_lo, xs_hi, gs_lo, gs_hi, wg, wu, wd, sub_rows=256):
    N, D = xs_lo.shape
    tr = MOE_ROW_TILE
    row = lambda n: pl.BlockSpec((tr, n), lambda i, te, nu: (i, 0))
    expert = lambda w: pl.BlockSpec((1,) + w.shape[1:], lambda i, te, nu: (te[i], 0, 0))
    return pl.pallas_call(
        functools.partial(_expert_ffn_kernel, sub_rows=sub_rows),
        grid_spec=pltpu.PrefetchScalarGridSpec(
            num_scalar_prefetch=2,
            grid=(N // tr,),
            in_specs=[row(D), row(D), row(1), row(1), expert(wg), expert(wu), expert(wd)],
            out_specs=row(D)),
        out_shape=jax.ShapeDtypeStruct((N, D), BF16),
        compiler_params=_params(("arbitrary",)),
        name="moe_expert_ffn",
    )(tile_expert, n_used, xs_lo, xs_hi, gs_lo, gs_hi, wg, wu, wd)


def _combine_kernel(meta_ref, hidx_ref, x_ref, *refs):
    del hidx_ref
    E = N_EXPERTS
    ylo, yhi = refs[:E], refs[E:2 * E]
    s1_ref, s2_ref, gf_ref, o_ref, acc_ref = refs[2 * E:]
    c = pl.program_id(0)
    s1 = s1_ref[...]
    s2 = s2_ref[...]

    def pick(tile):
        parts = []
        for j in range(MOE_CHUNK // LANE):
            cols = tile * MOE_CHUNK + j * LANE + lax.broadcasted_iota(jnp.int32, (MOE_CHUNK, LANE), 1)
            parts.append(jnp.where((s1 == cols) | (s2 == cols), 1.0, 0.0).astype(BF16))
        return jnp.concatenate(parts, axis=1)

    metas = [_tile_meta(meta_ref[c * E + e]) for e in range(E)]
    total = x_ref[...]
    for (a, touched), y_ref in zip(metas, ylo):
        in_lo = functools.reduce(jnp.logical_or, touched[:MOE_PARTS])
        total = total + _dot(pick(jnp.where(in_lo, a, -1)), y_ref[...])
    acc_ref[...] = total

    for (a, touched), y_ref in zip(metas, yhi):
        @pl.when(functools.reduce(jnp.logical_or, touched[MOE_PARTS:]))
        def _(a=a, y_ref=y_ref):
            acc_ref[...] += _dot(pick(a + 1), y_ref[...])

    o_ref[...] = _rms(acc_ref[...], gf_ref[...])


def _combine(meta_ce, hi_tile_ce, x2, ys, slot1_rep, slot2_rep, g_final):
    T, D = x2.shape
    C = T // MOE_CHUNK
    E = N_EXPERTS
    lo_spec = lambda e: pl.BlockSpec((MOE_CHUNK, D), lambda c, m, h: (m[c * E + e] >> (2 * MOE_PARTS), 0))
    hi_spec = lambda e: pl.BlockSpec((MOE_CHUNK, D), lambda c, m, h: (h[c * E + e], 0))
    return pl.pallas_call(
        _combine_kernel,
        grid_spec=pltpu.PrefetchScalarGridSpec(
            num_scalar_prefetch=2,
            grid=(C,),
            in_specs=([pl.BlockSpec((MOE_CHUNK, D), lambda c, m, h: (c, 0))]
                      + [lo_spec(e) for e in range(E)] + [hi_spec(e) for e in range(E)]
                      + [pl.BlockSpec((MOE_CHUNK, LANE), lambda c, m, h: (c, 0)),
                         pl.BlockSpec((MOE_CHUNK, LANE), lambda c, m, h: (c, 0)),
                         pl.BlockSpec((1, D), lambda c, m, h: (0, 0))]),
            out_specs=pl.BlockSpec((MOE_CHUNK, D), lambda c, m, h: (c, 0)),
            scratch_shapes=[pltpu.VMEM((MOE_CHUNK, D), F32)]),
        out_shape=jax.ShapeDtypeStruct((T, D), F32),
        compiler_params=_params(("arbitrary",)),
        name="moe_combine_final_norm",
    )(meta_ce, hi_tile_ce, x2, *([ys] * (2 * E)), slot1_rep, slot2_rep, g_final)


def _rope_tables(S):
    inv = ROPE_THETA ** (-jnp.arange(ROPE_HALF, dtype=F32) / ROPE_HALF)
    ang = jnp.arange(S, dtype=F32)[:, None] * inv[None, :]
    cos, sin = jnp.cos(ang), jnp.sin(ang)
    pad = HEAD_DIM - ROPE_DIM
    cos_h = jnp.concatenate([cos, cos, jnp.ones((S, pad), F32)], axis=-1)
    sin_h = jnp.concatenate([-sin, sin, jnp.zeros((S, pad), F32)], axis=-1)
    reps = LANE // HEAD_DIM
    return jnp.tile(cos_h, (1, reps)), jnp.tile(sin_h, (1, reps))


def _pack_w_in(w):
    per_group = GROUP * N_BRANCH
    gates = w[:, GATE_OFF:GATE_OFF + N_KV_HEADS * per_group].reshape(-1, N_KV_HEADS, per_group)
    gates = jnp.pad(gates, ((0, 0), (0, LANE // GATE_ROWS - N_KV_HEADS), (0, GATE_ROWS - per_group)))
    gates = gates.reshape(-1, LANE)
    return jnp.concatenate([w[:, :GATE_OFF], gates, w[:, GATE_OFF + N_KV_HEADS * per_group:]], axis=1).astype(BF16)


def _mixer_heads(x, norm_g, w_in, pe_k, w1_k, b1_k, w2_k, pe_v, w1_v, b1_v, w2_v,
                 w_pool, pool_scale, cos_t, sin_t):
    B, S, D = x.shape
    qt, qrt, kcr, vcr, ks, vst, kw, vwt, gates, p = _mixer_proj(
        x, norm_g.reshape(1, D), _pack_w_in(w_in), cos_t, sin_t, w_pool.astype(BF16), pool_scale.reshape(1, -1))
    half = CMP_LEN // 2
    cw = half * HEAD_DIM
    pad_w2 = lambda w2: jnp.pad(w2, ((0, 0), (0, LANE - HEAD_DIM))).astype(BF16)
    kc, vct = _compress(
        kcr, vcr,
        pe_k.reshape(2, cw), w1_k.reshape(2, cw, CMP_HIDDEN).astype(BF16), b1_k.reshape(1, -1), pad_w2(w2_k),
        pe_v.reshape(2, cw), w1_v.reshape(2, cw, CMP_HIDDEN).astype(BF16), b1_v.reshape(1, -1), pad_w2(w2_v))
    return _nsa_attention(qt, qrt, kc, vct, ks, vst, kw, vwt, gates), p


def _moe_layer(x2, g_ffn, router, wg, wu, wd, g_final):
    T, D = x2.shape
    E = N_EXPERTS
    C = T // MOE_CHUNK
    i32 = jnp.int32
    w_router = jnp.pad(router, ((0, 0), (0, LANE - E)))
    h, route, gate_rows, before, total = _router(x2, g_ffn.reshape(1, D), w_router)

    counts = total[0, :E].astype(i32)
    tiles = (counts + MOE_ROW_TILE - 1) // MOE_ROW_TILE
    ends = jnp.cumsum(tiles)
    off = (ends - tiles) * MOE_ROW_TILE
    e1, e2, r1, r2 = route[:, 0], route[:, 1], route[:, 2], route[:, 3]
    off_of = lambda e: sum(jnp.where(e == k, off[k], 0) for k in range(E))
    slot1 = off_of(e1) + r1
    slot2 = off_of(e2) + r2
    first = off[None, :] + before[:, 0, :E].astype(i32)
    after = jnp.concatenate([before[1:, 0, :E], total[:, :E]], axis=0).astype(i32)
    last = off[None, :] + after - 1
    lo_tile = first // MOE_CHUNK
    tile_meta = lo_tile << (2 * MOE_PARTS)
    part_rows = MOE_CHUNK // MOE_PARTS
    for k in range(2 * MOE_PARTS):
        start = lo_tile * MOE_CHUNK + k * part_rows
        touched = (last >= first) & (first < start + part_rows) & (last >= start)
        tile_meta = tile_meta + (touched.astype(i32) << k)
    n_row_tiles = (2 * T) // MOE_ROW_TILE + E + 1 + MOE_CHUNK // MOE_ROW_TILE
    n_used = ends[-1:]
    tile_ids = jnp.minimum(jnp.arange(n_row_tiles, dtype=i32), n_used[0] - 1)
    tile_expert = jnp.minimum(jnp.sum(ends[None, :] <= tile_ids[:, None], axis=1), E - 1).astype(i32)

    zero = jnp.zeros_like(slot1)
    tok_rows = jnp.stack([slot1, slot2, zero, zero, zero, zero, zero, zero], axis=1)
    slot1_rep = jnp.broadcast_to(slot1.reshape(T, 1), (T, LANE))
    slot2_rep = jnp.broadcast_to(slot2.reshape(T, 1), (T, LANE))

    xs_lo, xs_hi, gs_lo, gs_hi = _dispatch(tile_meta.T.reshape(-1).astype(i32), h, tok_rows, gate_rows,
                                           n_row_tiles * MOE_ROW_TILE)
    ys = _expert_ffn(tile_expert, n_used.astype(i32), xs_lo, xs_hi, gs_lo, gs_hi,
                     wg.astype(BF16), wu.astype(BF16), wd.astype(BF16))
    in_hi = (tile_meta >> MOE_PARTS) & ((1 << MOE_PARTS) - 1) != 0
    hi_tile = lax.cummax(jnp.where(in_hi, lo_tile + 1, 0), axis=0)
    return _combine(tile_meta.reshape(-1).astype(i32), hi_tile.reshape(-1).astype(i32), x2, ys,
                    slot1_rep, slot2_rep, g_final.reshape(1, D))


def kernel(x, mem, norm_mix, w_in, cmp_pe_k, cmp_w1_k, cmp_b1_k, cmp_w2_k, cmp_pe_v, cmp_w1_v, cmp_b1_v, cmp_w2_v, w_pool, pool_scale, w_out, norm_x, norm_mem, wq_x, wk_x, wv_x, wo_x, norm_ffn, ffn_wg, ffn_wu, ffn_wd, moe_router, moe_wg, moe_wu, moe_wd, norm_final):
    B, S, D = x.shape
    depth = norm_mix.shape[0]
    assert depth == 2, "the final RMSNorm is fused into the expert layer, which must be the last one"
    cos_t, sin_t = _rope_tables(S)
    for layer in range(depth):
        a, p = _mixer_heads(x, norm_mix[layer], w_in[layer],
                            cmp_pe_k[layer], cmp_w1_k[layer], cmp_b1_k[layer], cmp_w2_k[layer],
                            cmp_pe_v[layer], cmp_w1_v[layer], cmp_b1_v[layer], cmp_w2_v[layer],
                            w_pool[layer], pool_scale[layer], cos_t, sin_t)
        wkv = jnp.concatenate([wk_x[layer], wv_x[layer]], axis=1).astype(BF16)
        kx, vx = _xattn_kv(mem, norm_mem[layer].reshape(1, D), wkv)
        wo_mix = w_out[layer].astype(BF16)
        x = _xattn(x, a, p, wo_mix[:NSA_WIDTH], wo_mix[NSA_WIDTH:], norm_x[layer].reshape(1, D),
                   wq_x[layer].astype(BF16), kx, vx, wo_x[layer].astype(BF16))
        x2 = x.reshape(B * S, D)
        j = layer // 2
        if layer % 2 == 0:
            x2 = _ffn(x2, norm_ffn[layer].reshape(1, D), ffn_wg[j].astype(BF16), ffn_wu[j].astype(BF16),
                      ffn_wd[j].astype(BF16))
        else:
            x2 = _moe_layer(x2, norm_ffn[layer], moe_router[j], moe_wg[j], moe_wu[j], moe_wd[j], norm_final)
        x = x2.reshape(B, S, D)
    return x
```
